```python
import math
import jax
import jax.numpy as jnp
from jax import lax
import numpy as np

D_MODEL = 2048
BATCH = 32
SEQ = 256
DEPTH = 2
DEC_BATCH = 4
DEC_SEQ = 1024
PAST_LEN = 256

GRID_W = 64
HEAD_DIM = 128
N_EVEN = (DEPTH + 1) // 2
N_ODD = DEPTH // 2
A_HEADS = D_MODEL // (2 * HEAD_DIM)
A_QK_DIM = HEAD_DIM // 2
B_HEADS = D_MODEL // (2 * HEAD_DIM)
B_Q_RANK = 3 * D_MODEL // 8
B_KV_RANK = D_MODEL // 4
B_NOPE_DIM = HEAD_DIM
B_ROPE_DIM = HEAD_DIM // 2
B_V_DIM = HEAD_DIM
C_HEADS = D_MODEL // (2 * HEAD_DIM)
NA_ROWS = 8
NA_COLS = 16
NA_QCOLS = 16
D_HEADS = D_MODEL // (2 * HEAD_DIM)
D_KV_HEADS = 2
D_GROUP = D_HEADS // D_KV_HEADS
D_WINDOW = 128
QBLK = 128
N_EXPERTS = 16
N_GROUPS = 4
TOP_K = 2
D_EXPERT = D_MODEL // 2
ROPE_THETA = 10000.0
EPS = 1e-6
NEG = -1e30
EVEN_SIZES = (A_HEADS * 2 * A_QK_DIM, A_HEADS * 2 * A_QK_DIM, A_HEADS * HEAD_DIM, B_Q_RANK, B_KV_RANK, B_ROPE_DIM)
ODD_SIZES = (C_HEADS * HEAD_DIM, C_HEADS * HEAD_DIM, C_HEADS * HEAD_DIM, D_HEADS * HEAD_DIM, D_KV_HEADS * HEAD_DIM, D_KV_HEADS * HEAD_DIM)
EVEN_IN = sum(EVEN_SIZES)
ODD_IN = sum(ODD_SIZES)

kernel_name = 'hybrid_flow_prefix_trunk_step'


def rmsnorm(x, g):
    xf = x.astype(jnp.float32)
    y = xf * lax.rsqrt(jnp.mean(xf * xf, axis=-1, keepdims=True) + EPS)
    return (y * g.astype(jnp.float32)).astype(x.dtype)


def split_cols(x, sizes):
    cuts = [int(v) for v in np.cumsum(sizes)[:-1]]
    return jnp.split(x, cuts, axis=-1)


def adaln(cond, w, b):
    m = jnp.einsum('...d,de->...e', jax.nn.silu(cond), w) + b
    return jnp.split(m[..., None, :], 6, axis=-1)


def modulate(x, g, shift, scale):
    return rmsnorm(x, g) * (1.0 + scale) + shift


def axial_rope_tables(n_tok, dim):
    half = dim // 2
    inv = ROPE_THETA ** (-jnp.arange(0, half, 2, dtype=jnp.float32) / half)
    t = jnp.arange(n_tok)
    ang_r = (t // GRID_W).astype(jnp.float32)[:, None] * inv[None, :]
    ang_c = (t % GRID_W).astype(jnp.float32)[:, None] * inv[None, :]
    ang = jnp.concatenate([ang_r, ang_r, ang_c, ang_c], axis=-1)
    return jnp.cos(ang), jnp.sin(ang)


def apply_rope(x, rope):
    cos, sin = rope
    shape = (cos.shape[0],) + (1,) * (x.ndim - 3) + (cos.shape[1],)
    cos = cos.reshape(shape).astype(x.dtype)
    sin = sin.reshape(shape).astype(x.dtype)
    x1, x2, x3, x4 = jnp.split(x, 4, axis=-1)
    rot = jnp.concatenate([-x2, x1, -x4, x3], axis=-1)
    return x * cos + rot * sin


def split_blocks(a):
    b, s = a.shape[:2]
    return jnp.moveaxis(a.reshape((b, s // QBLK, QBLK) + a.shape[2:]), 1, 0)


def merge_blocks(a):
    nb, b, qb = a.shape[:3]
    return jnp.moveaxis(a, 0, 1).reshape((b, nb * qb) + a.shape[3:])


def map_query_blocks(fn, *qs):
    return merge_blocks(lax.map(lambda xs: fn(*xs), tuple(split_blocks(a) for a in qs)))


def sink_softmax(s, sink):
    m = jnp.maximum(jnp.max(s, axis=-1, keepdims=True), sink)
    e = jnp.exp(s - m)
    return e / (jnp.sum(e, axis=-1, keepdims=True) + jnp.exp(sink - m))


def diff_attention(q, k, v, lam):
    scale = q.shape[-1] ** -0.5
    def block(qb):
        s = jnp.einsum('bqhid,bkhid->bhiqk', qb, k).astype(jnp.float32) * scale
        p = jax.nn.softmax(s, axis=-1)
        w = p[:, :, 0] - lam * p[:, :, 1]
        return jnp.einsum('bhqk,bkhd->bqhd', w.astype(v.dtype), v)
    return map_query_blocks(block, q)


def mla_attention(q_nope, q_rope, k_nope, k_rope, v):
    scale = (q_nope.shape[-1] + q_rope.shape[-1]) ** -0.5
    def block(qn, qr):
        s = (jnp.einsum('bqhd,bkhd->bhqk', qn, k_nope) + jnp.einsum('bqhd,bkd->bhqk', qr, k_rope)).astype(jnp.float32) * scale
        p = jax.nn.softmax(s, axis=-1)
        return jnp.einsum('bhqk,bkhd->bqhd', p.astype(v.dtype), v)
    return map_query_blocks(block, q_nope, q_rope)


def context_attention(q, k, v, sink):
    scale = q.shape[-1] ** -0.5
    def block(qb):
        s = jnp.einsum('bqngd,bknd->bngqk', qb, k).astype(jnp.float32) * scale
        if sink is None:
            p = jax.nn.softmax(s, axis=-1)
        else:
            p = sink_softmax(s, sink.astype(jnp.float32)[None, :, :, None, None])
        return jnp.einsum('bngqk,bknd->bqngd', p.astype(v.dtype), v)
    return map_query_blocks(block, q)


def neighbourhood_attention(q, k, v, k_ctx, v_ctx, rpb):
    B, S, H, d = q.shape
    P = k_ctx.shape[1]
    rows = S // GRID_W
    kh = min(NA_ROWS, rows)
    ncb = GRID_W // NA_QCOLS
    kbw = NA_QCOLS + NA_COLS
    blk_start = np.clip(np.arange(ncb) * NA_QCOLS - NA_COLS // 2, 0, GRID_W - kbw)
    key_col = blk_start[:, None] + np.arange(kbw)
    q_col = np.arange(ncb)[:, None] * NA_QCOLS + np.arange(NA_QCOLS)
    win_start = np.clip(q_col - NA_COLS // 2, 0, GRID_W - NA_COLS)
    kc = key_col[:, None, :]
    col_ok = (kc >= win_start[..., None]) & (kc < win_start[..., None] + NA_COLS)
    dc_idx = np.clip(kc - q_col[..., None], -(NA_COLS - 1), NA_COLS - 1) + NA_COLS - 1
    bias_c = rpb.astype(jnp.float32)[:, :, dc_idx]
    kg = k.reshape(B, rows, GRID_W, H, d)
    vg = v.reshape(B, rows, GRID_W, H, d)
    qg = q.reshape(B, rows, ncb, NA_QCOLS, H, d)
    scale = d ** -0.5
    def row_block(args):
        r, qr = args
        r0 = jnp.clip(r - kh // 2, 0, rows - kh)
        kr = lax.dynamic_slice_in_dim(kg, r0, kh, axis=1)[:, :, key_col]
        vr = lax.dynamic_slice_in_dim(vg, r0, kh, axis=1)[:, :, key_col]
        dr_idx = r0 + jnp.arange(kh) - r + NA_ROWS - 1
        bias = jnp.moveaxis(bias_c[:, dr_idx], 1, 3)
        s_loc = jnp.einsum('bjqhd,brjchd->bhjqrc', qr, kr).astype(jnp.float32) * scale + bias[None]
        s_loc = jnp.where(col_ok[:, :, None, :], s_loc, NEG)
        s_ctx = jnp.einsum('bjqhd,bkhd->bhjqk', qr, k_ctx).astype(jnp.float32) * scale
        s = jnp.concatenate([s_ctx, s_loc.reshape(B, H, ncb, NA_QCOLS, kh * kbw)], axis=-1)
        p = jax.nn.softmax(s, axis=-1).astype(v.dtype)
        p_loc = p[..., P:].reshape(B, H, ncb, NA_QCOLS, kh, kbw)
        return (jnp.einsum('bhjqk,bkhd->bjqhd', p[..., :P], v_ctx)
                + jnp.einsum('bhjqrc,brjchd->bjqhd', p_loc, vr))
    out = lax.map(row_block, (jnp.arange(rows), jnp.moveaxis(qg, 1, 0)))
    return jnp.moveaxis(out, 0, 1).reshape(B, S, H * d)


def windowed_attention(q, k, v, k_ctx, v_ctx, sink):
    B, S, n, g, d = q.shape
    P = k_ctx.shape[1]
    span = 3 * QBLK
    pad = ((0, 0), (QBLK, QBLK), (0, 0), (0, 0))
    kp = jnp.pad(k, pad)
    vp = jnp.pad(v, pad)
    rel = np.arange(span)[None, :] - QBLK - np.arange(QBLK)[:, None]
    band_ok = np.abs(rel) <= D_WINDOW
    sink_f = sink.astype(jnp.float32)[None, :, :, None, None]
    scale = d ** -0.5
    def block(args):
        j, qb = args
        kb = lax.dynamic_slice_in_dim(kp, j * QBLK, span, axis=1)
        vb = lax.dynamic_slice_in_dim(vp, j * QBLK, span, axis=1)
        kpos = j * QBLK - QBLK + jnp.arange(span)
        ok = band_ok & ((kpos >= 0) & (kpos < S))[None, :]
        s_loc = jnp.where(ok, jnp.einsum('bqngd,bknd->bngqk', qb, kb).astype(jnp.float32) * scale, NEG)
        s_ctx = jnp.einsum('bqngd,bknd->bngqk', qb, k_ctx).astype(jnp.float32) * scale
        p = sink_softmax(jnp.concatenate([s_ctx, s_loc], axis=-1), sink_f).astype(v.dtype)
        return (jnp.einsum('bngqk,bknd->bqngd', p[..., :P], v_ctx)
                + jnp.einsum('bngqk,bknd->bqngd', p[..., P:], vb))
    out = lax.map(block, (jnp.arange(S // QBLK), split_blocks(q)))
    return merge_blocks(out)


def even_mixer(h, w_in, lam_vec, subln_g, q_norm_g, kv_norm_g, w_uq, w_ukv, w_out, lam_init, lat):
    B, S = h.shape[:2]
    a_q, a_k, a_v, b_cq, b_ckv, b_kr = split_cols(jnp.einsum('bsd,de->bse', h, w_in), EVEN_SIZES)
    a_q = a_q.reshape(B, S, A_HEADS, 2, A_QK_DIM)
    a_k = a_k.reshape(B, S, A_HEADS, 2, A_QK_DIM)
    a_v = a_v.reshape(B, S, A_HEADS, HEAD_DIM)
    b_q = jnp.einsum('bsr,re->bse', rmsnorm(b_cq, q_norm_g), w_uq).reshape(B, S, B_HEADS, B_NOPE_DIM + B_ROPE_DIM)
    b_q_nope, b_q_rope = b_q[..., :B_NOPE_DIM], b_q[..., B_NOPE_DIM:]
    b_ckv = rmsnorm(b_ckv, kv_norm_g)
    if lat is not None:
        rope_a, rope_b, _ = lat
        a_q = apply_rope(a_q, rope_a)
        a_k = apply_rope(a_k, rope_a)
        b_q_rope = apply_rope(b_q_rope, rope_b)
        b_kr = apply_rope(b_kr, rope_b)
    own = (a_k.reshape(B, S, A_HEADS, 2 * A_QK_DIM), a_v, b_ckv, b_kr)
    if lat is None:
        k_a, v_a, ckv_all, kr_all = own
    else:
        k_a, v_a, ckv_all, kr_all = [jnp.concatenate([cx, ow], axis=1) for cx, ow in zip(lat[2], own)]
    T = k_a.shape[1]
    k_a = k_a.reshape(B, T, A_HEADS, 2, A_QK_DIM)
    kv = jnp.einsum('btr,re->bte', ckv_all, w_ukv).reshape(B, T, B_HEADS, B_NOPE_DIM + B_V_DIM)
    lv = lam_vec.astype(jnp.float32)
    lam = jnp.exp(jnp.sum(lv[0] * lv[1])) - jnp.exp(jnp.sum(lv[2] * lv[3])) + lam_init
    o_a = rmsnorm(diff_attention(a_q, k_a, v_a, lam), subln_g) * (1.0 - lam_init)
    o_b = mla_attention(b_q_nope, b_q_rope, kv[..., :B_NOPE_DIM], kr_all, kv[..., B_NOPE_DIM:])
    o = jnp.concatenate([o_a.reshape(B, S, -1), o_b.reshape(B, S, -1)], axis=-1)
    return jnp.einsum('bse,ed->bsd', o, w_out), own


def odd_mixer(h, w_in, rpb, sink, w_out, lat):
    B, S = h.shape[:2]
    c_q, c_k, c_v, d_q, d_k, d_v = split_cols(jnp.einsum('bsd,de->bse', h, w_in), ODD_SIZES)
    c_q = c_q.reshape(B, S, C_HEADS, HEAD_DIM)
    c_k = c_k.reshape(B, S, C_HEADS, HEAD_DIM)
    c_v = c_v.reshape(B, S, C_HEADS, HEAD_DIM)
    d_q = d_q.reshape(B, S, D_KV_HEADS, D_GROUP, HEAD_DIM)
    d_k = d_k.reshape(B, S, D_KV_HEADS, HEAD_DIM)
    d_v = d_v.reshape(B, S, D_KV_HEADS, HEAD_DIM)
    sink = sink.reshape(D_KV_HEADS, D_GROUP)
    if lat is None:
        o_c = context_attention(c_q[:, :, :, None], c_k, c_v, None)
        o_d = context_attention(d_q, d_k, d_v, sink)
    else:
        rope_d, (ck_ctx, cv_ctx, dk_ctx, dv_ctx) = lat
        d_q = apply_rope(d_q, rope_d)
        d_k = apply_rope(d_k, rope_d)
        o_c = neighbourhood_attention(c_q, c_k, c_v, ck_ctx, cv_ctx, rpb)
        o_d = windowed_attention(d_q, d_k, d_v, dk_ctx, dv_ctx, sink)
    o = jnp.concatenate([o_c.reshape(B, S, -1), o_d.reshape(B, S, -1)], axis=-1)
    return jnp.einsum('bse,ed->bsd', o, w_out), (c_k, c_v, d_k, d_v)


def moe_ffn(h, router_w, router_b, w_gate, w_up, w_down):
    B, S, D = h.shape
    n = B * S
    xt = h.reshape(n, D)
    scores = jax.nn.sigmoid(jnp.einsum('nd,de->ne', xt, router_w).astype(jnp.float32))
    sel = scores + router_b.astype(jnp.float32)
    per = N_EXPERTS // N_GROUPS
    grp_score = lax.top_k(sel.reshape(n, N_GROUPS, per), 2)[0].sum(-1)
    _, best = lax.top_k(grp_score, 1)
    in_grp = (jnp.arange(N_EXPERTS) // per)[None, :] == best
    _, e_idx = lax.top_k(jnp.where(in_grp, sel, NEG), TOP_K)
    gate = jnp.take_along_axis(scores, e_idx, axis=1)
    gate = gate / jnp.sum(gate, axis=-1, keepdims=True)
    flat_e = e_idx.reshape(-1)
    order = jnp.argsort(flat_e)
    tok = order // TOP_K
    sizes = jnp.bincount(flat_e, length=N_EXPERTS).astype(jnp.int32)
    xs = xt[tok]
    hid = jax.nn.silu(lax.ragged_dot(xs, w_gate, sizes)) * lax.ragged_dot(xs, w_up, sizes)
    ys = lax.ragged_dot(hid, w_down, sizes) * gate.reshape(-1)[order][:, None].astype(xt.dtype)
    return jax.ops.segment_sum(ys, tok, num_segments=n).reshape(B, S, D)


def diff_lambda_init(layer):
    return 0.8 - 0.6 * math.exp(-0.3 * layer)


def setup_inputs(seed: int = 0) -> dict:
    key = jax.random.key(seed)
    keys = list(jax.random.split(key, 40))
    def nrm(shape, scale):
        return jax.random.normal(keys.pop(), shape, jnp.float32) * scale
    def gain(shape):
        return 1.0 + nrm(shape, 0.1)
    D = D_MODEL
    return {
        'x_prompt': nrm((BATCH, SEQ, D), 1.0),
        'x_sample': nrm((DEC_BATCH, DEC_SEQ, D), 1.0),
        'cache_a_k': nrm((DEC_BATCH, N_EVEN, PAST_LEN, A_HEADS, 2 * A_QK_DIM), 1.0),
        'cache_a_v': nrm((DEC_BATCH, N_EVEN, PAST_LEN, A_HEADS, HEAD_DIM), 1.0),
        'cache_b_ckv': nrm((DEC_BATCH, N_EVEN, PAST_LEN, B_KV_RANK), 1.0),
        'cache_b_krope': nrm((DEC_BATCH, N_EVEN, PAST_LEN, B_ROPE_DIM), 1.0),
        'cache_c_k': nrm((DEC_BATCH, N_ODD, PAST_LEN, C_HEADS, HEAD_DIM), 1.0),
        'cache_c_v': nrm((DEC_BATCH, N_ODD, PAST_LEN, C_HEADS, HEAD_DIM), 1.0),
        'cache_d_k': nrm((DEC_BATCH, N_ODD, PAST_LEN, D_KV_HEADS, HEAD_DIM), 1.0),
        'cache_d_v': nrm((DEC_BATCH, N_ODD, PAST_LEN, D_KV_HEADS, HEAD_DIM), 1.0),
        'c': nrm((DEC_BATCH, D), 1.0),
        'c_ctx': nrm((D,), 1.0),
        'w_ada': nrm((DEPTH, D, 6 * D), 0.5 * D ** -0.5),
        'b_ada': nrm((DEPTH, 6 * D), 0.02),
        'norm_mix_g': gain((DEPTH, D)),
        'norm_ffn_g': gain((DEPTH, D)),
        'ev_w_in': nrm((N_EVEN, D, EVEN_IN), D ** -0.5),
        'ev_lambda': nrm((N_EVEN, 4, A_QK_DIM), 0.1),
        'ev_subln_g': gain((N_EVEN, HEAD_DIM)),
        'ev_q_norm_g': gain((N_EVEN, B_Q_RANK)),
        'ev_kv_norm_g': gain((N_EVEN, B_KV_RANK)),
        'ev_w_uq': nrm((N_EVEN, B_Q_RANK, B_HEADS * (B_NOPE_DIM + B_ROPE_DIM)), B_Q_RANK ** -0.5),
        'ev_w_ukv': nrm((N_EVEN, B_KV_RANK, B_HEADS * (B_NOPE_DIM + B_V_DIM)), B_KV_RANK ** -0.5),
        'ev_w_out': nrm((N_EVEN, D, D), D ** -0.5),
        'od_w_in': nrm((N_ODD, D, ODD_IN), D ** -0.5),
        'od_rpb': nrm((N_ODD, C_HEADS, 2 * NA_ROWS - 1, 2 * NA_COLS - 1), 0.2),
        'od_sink': nrm((N_ODD, D_HEADS), 0.5),
        'od_w_out': nrm((N_ODD, D, D), D ** -0.5),
        'router_w': nrm((D, N_EXPERTS), D ** -0.5),
        'router_b': nrm((N_EXPERTS,), 0.01),
        'moe_w_gate': nrm((DEPTH, N_EXPERTS, D, D_EXPERT), D ** -0.5),
        'moe_w_up': nrm((DEPTH, N_EXPERTS, D, D_EXPERT), D ** -0.5),
        'moe_w_down': nrm((DEPTH, N_EXPERTS, D_EXPERT, D), D_EXPERT ** -0.5),
        'final_g': gain((D,)),
    }


def reference(x_prompt, x_sample, cache_a_k, cache_a_v, cache_b_ckv, cache_b_krope, cache_c_k, cache_c_v,
              cache_d_k, cache_d_v, c, c_ctx, w_ada, b_ada, norm_mix_g, norm_ffn_g, ev_w_in, ev_lambda,
              ev_subln_g, ev_q_norm_g, ev_kv_norm_g, ev_w_uq, ev_w_ukv, ev_w_out, od_w_in, od_rpb, od_sink,
              od_w_out, router_w, router_b, moe_w_gate, moe_w_up, moe_w_down, final_g):
    xp, xs = x_prompt, x_sample
    n_lat = x_sample.shape[1]
    rope_a = axial_rope_tables(n_lat, A_QK_DIM)
    rope_b = axial_rope_tables(n_lat, B_ROPE_DIM)
    rope_d = axial_rope_tables(n_lat, HEAD_DIM)
    even_states = ([], [], [], [])
    odd_states = ([], [], [], [])
    for l in range(DEPTH):
        i = l // 2
        sh1p, sc1p, g1p, sh2p, sc2p, g2p = adaln(c_ctx, w_ada[l], b_ada[l])
        sh1s, sc1s, g1s, sh2s, sc2s, g2s = adaln(c, w_ada[l], b_ada[l])
        hp = modulate(xp, norm_mix_g[l], sh1p, sc1p)
        hs = modulate(xs, norm_mix_g[l], sh1s, sc1s)
        if l % 2 == 0:
            args = (ev_w_in[i], ev_lambda[i], ev_subln_g[i], ev_q_norm_g[i], ev_kv_norm_g[i],
                    ev_w_uq[i], ev_w_ukv[i], ev_w_out[i], diff_lambda_init(l))
            yp, st = even_mixer(hp, *args, None)
            ctx = (cache_a_k[:, i], cache_a_v[:, i], cache_b_ckv[:, i], cache_b_krope[:, i])
            ys, _ = even_mixer(hs, *args, (rope_a, rope_b, ctx))
            for lst, t in zip(even_states, st):
                lst.append(t)
        else:
            args = (od_w_in[i], od_rpb[i], od_sink[i], od_w_out[i])
            yp, st = odd_mixer(hp, *args, None)
            ctx = (cache_c_k[:, i], cache_c_v[:, i], cache_d_k[:, i], cache_d_v[:, i])
            ys, _ = odd_mixer(hs, *args, (rope_d, ctx))
            for lst, t in zip(odd_states, st):
                lst.append(t)
        xp = xp + g1p * yp
        xs = xs + g1s * ys
        moe_args = (router_w, router_b, moe_w_gate[l], moe_w_up[l], moe_w_down[l])
        xp = xp + g2p * moe_ffn(modulate(xp, norm_ffn_g[l], sh2p, sc2p), *moe_args)
        xs = xs + g2s * moe_ffn(modulate(xs, norm_ffn_g[l], sh2s, sc2s), *moe_args)
    y_prompt = rmsnorm(xp, final_g)
    y_sample = rmsnorm(xs, final_g)
    new_a_k = jnp.stack(even_states[0], axis=1)
    new_a_v = jnp.stack(even_states[1], axis=1)
    new_b_ckv = jnp.stack(even_states[2], axis=1)
    new_b_krope = jnp.stack(even_states[3], axis=1)
    new_c_k = jnp.stack(odd_states[0], axis=1)
    new_c_v = jnp.stack(odd_states[1], axis=1)
    new_d_k = jnp.stack(odd_states[2], axis=1)
    new_d_v = jnp.stack(odd_states[3], axis=1)
    return (y_prompt, y_sample, new_a_k, new_a_v, new_b_ckv, new_b_krope, new_c_k, new_c_v, new_d_k, new_d_v)
```

```python
import functools
import math

import numpy as np
import jax
import jax.numpy as jnp
from jax import lax
from jax.experimental import pallas as pl
from jax.experimental.pallas import tpu as pltpu

D_MODEL = 2048
BATCH = 32
SEQ = 256
DEPTH = 2
DEC_BATCH = 4
DEC_SEQ = 1024
PAST_LEN = 256
GRID_W = 64
HEAD_DIM = 128
N_HEADS = 8
A_QK_DIM = 64
B_Q_RANK = 768
B_KV_RANK = 512
B_ROPE_DIM = 64
NA_ROWS = 8
NA_COLS = 16
D_KV_HEADS = 2
D_GROUP = 4
D_WINDOW = 128
N_EXPERTS = 16
N_GROUPS = 4
D_EXPERT = 1024
ROPE_THETA = 10000.0
EPS = 1e-6
NEG = -1e30

SEG = 1024
N_PROMPT = BATCH * SEQ
N_TOK = N_PROMPT + DEC_BATCH * DEC_SEQ
N_SEG = N_TOK // SEG
N_SEG_P = N_PROMPT // SEG
PROJ_W = 4608
TN = 512
LANES = 128
MOE_TM = 256
MOE_ROWS = 2 * N_TOK + N_EXPERTS * MOE_TM
MOE_TILES = MOE_ROWS // MOE_TM

F32 = jnp.float32
BF16 = jnp.bfloat16


def _cparams(sem, vmem_mb):
    return pltpu.CompilerParams(dimension_semantics=sem, vmem_limit_bytes=vmem_mb * 1024 * 1024)


def _dot(a, b):
    return jnp.dot(a, b, preferred_element_type=F32)


def _dot_nt(a, b):
    return lax.dot_general(a, b, (((1,), (1,)), ((), ())), preferred_element_type=F32)


def _sigmoid(x):
    return 1.0 / (1.0 + jnp.exp(-x))


def _adaln_kernel(c_ref, w_ref, b_ref, o_ref):
    c = c_ref[...]
    a = (c * _sigmoid(c)).astype(BF16)
    o_ref[...] = _dot(a, w_ref[...].astype(BF16)) + b_ref[...]


def _adaln(cond8, w_ada, b_ada):
    tn = 1024
    n = 6 * D_MODEL
    return pl.pallas_call(
        _adaln_kernel,
        grid=(DEPTH, n // tn),
        in_specs=[
            pl.BlockSpec((8, D_MODEL), lambda l, j: (0, 0)),
            pl.BlockSpec((None, D_MODEL, tn), lambda l, j: (l, 0, j)),
            pl.BlockSpec((None, 1, tn), lambda l, j: (l, 0, j)),
        ],
        out_specs=pl.BlockSpec((None, 8, tn), lambda l, j: (l, 0, j)),
        out_shape=jax.ShapeDtypeStruct((DEPTH, 8, n), F32),
        compiler_params=_cparams(("arbitrary", "arbitrary"), 40),
        name="adaln",
    )(cond8, w_ada, b_ada.reshape(DEPTH, 1, n))


def _rope(a, cos, sp, sm, sh):
    return a * cos + pltpu.roll(a, sh, 1) * sp + pltpu.roll(a, LANES - sh, 1) * sm


def _proj_kernel(x_ref, mt_ref, g_ref, w_ref, cos_ref, sp_ref, sm_ref, ng_ref, o_ref, xn_ref, *,
                 prologue, k_in, sh, rope_full, rope_part, norm_tile):
    i = pl.program_id(0)
    j = pl.program_id(1)

    @pl.when(j == 0)
    def _():
        x = x_ref[:, :k_in]
        if prologue == "cast":
            xn_ref[...] = x.astype(BF16)
        else:
            y = x * lax.rsqrt(jnp.mean(x * x, axis=-1, keepdims=True) + EPS) * g_ref[...]
            if prologue == "modulate":
                y = y * (1.0 + mt_ref[0, 1:2, :]) + mt_ref[0, 0:1, :]
            xn_ref[...] = y.astype(BF16)

    acc = _dot(xn_ref[...], w_ref[...])
    tn = acc.shape[1]
    n_grp = tn // LANES
    is_lat = i >= N_SEG_P

    def any_of(tiles):
        c = jnp.bool_(False)
        for t in tiles:
            c = jnp.logical_or(c, j == t)
        return c

    def store_roped(groups):
        cos, sp, sm = cos_ref[...], sp_ref[...], sm_ref[...]
        for g in range(n_grp):
            a = acc[:, g * LANES:(g + 1) * LANES]
            if g in groups:
                a = _rope(a, cos, sp, sm, sh)
            o_ref[:, g * LANES:(g + 1) * LANES] = a.astype(o_ref.dtype)

    full = jnp.logical_and(is_lat, any_of(rope_full))
    handled = full
    pl.when(full)(lambda: store_roped(range(n_grp)))
    if rope_part is not None:
        part = jnp.logical_and(is_lat, j == rope_part[0])
        handled = jnp.logical_or(handled, part)
        pl.when(part)(lambda: store_roped(range(*rope_part[1])))
    if norm_tile is not None:
        isn = j == norm_tile
        handled = jnp.logical_or(handled, isn)

        @pl.when(isn)
        def _():
            y = acc * lax.rsqrt(jnp.mean(acc * acc, axis=-1, keepdims=True) + EPS) * ng_ref[...]
            o_ref[...] = y.astype(o_ref.dtype)

    @pl.when(jnp.logical_not(handled))
    def _():
        o_ref[...] = acc.astype(o_ref.dtype)


def _projection(x, x_blk_w, x_blk_idx, k_in, w, *, prologue, mt=None, g=None, rope=None, sh=16,
                rope_full=(), rope_part=None, norm_tile=None, norm_g=None, out_dtype=F32, name):
    m = x.shape[0]
    n = w.shape[1]
    assert m % SEG == 0 and n % TN == 0 and w.shape[0] == k_in
    if mt is None:
        mt = jnp.zeros((m // SEG, 6, LANES), F32)
    if g is None:
        g = jnp.ones((1, k_in), F32)
    if rope is None:
        rope = tuple(jnp.zeros((SEG, LANES), F32) for _ in range(3))
    if norm_g is None:
        norm_g = jnp.ones((1, TN), F32)
    kern = functools.partial(_proj_kernel, prologue=prologue, k_in=k_in, sh=sh, rope_full=tuple(rope_full),
                             rope_part=rope_part, norm_tile=norm_tile)
    full2 = lambda i, j: (0, 0)
    return pl.pallas_call(
        kern,
        grid=(m // SEG, n // TN),
        in_specs=[
            pl.BlockSpec((SEG, x_blk_w), lambda i, j: (i, x_blk_idx)),
            pl.BlockSpec((1, 6, mt.shape[2]), lambda i, j: (i, 0, 0)),
            pl.BlockSpec((1, k_in), full2),
            pl.BlockSpec((k_in, TN), lambda i, j: (0, j)),
            pl.BlockSpec((SEG, LANES), full2),
            pl.BlockSpec((SEG, LANES), full2),
            pl.BlockSpec((SEG, LANES), full2),
            pl.BlockSpec((1, TN), full2),
        ],
        out_specs=pl.BlockSpec((SEG, TN), lambda i, j: (i, j)),
        out_shape=jax.ShapeDtypeStruct((m, n), out_dtype),
        scratch_shapes=[pltpu.VMEM((SEG, k_in), BF16)],
        compiler_params=_cparams(("arbitrary", "arbitrary"), 48),
        name=name,
    )(x, mt, g, w, *rope, norm_g)


def _attn_kernel(*refs, kind, n_parts, has_ctx, has_bias, has_band, has_sink, nb, s_len, tq, scale, lam_init):
    it = iter(refs)
    q_refs = [next(it) for _ in range(n_parts)]
    ko_refs = [next(it) for _ in range(n_parts)]
    vo_ref = next(it)
    kc_refs, vc_ref = [], None
    if has_ctx:
        kc_refs = [next(it) for _ in range(n_parts)]
        vc_ref = next(it)
    bias_ref = next(it) if has_bias else None
    lam_ref = sg_ref = None
    if kind == "diff":
        lam_ref = next(it)
        sg_ref = next(it)
    sink_ref = next(it) if has_sink else None
    o_ref = next(it)

    sink = sink_ref[pl.program_id(1)] if has_sink else None
    lane = lax.broadcasted_iota(jnp.int32, (1, LANES), 1)

    def softmax_parts(s_list):
        m = None
        for s in s_list:
            mm = jnp.max(s, axis=-1, keepdims=True)
            m = mm if m is None else jnp.maximum(m, mm)
        if sink is not None:
            m = jnp.maximum(m, sink)
        e_list = [jnp.exp(s - m) for s in s_list]
        l = None
        for e in e_list:
            ss = jnp.sum(e, axis=-1, keepdims=True)
            l = ss if l is None else l + ss
        if sink is not None:
            l = l + jnp.exp(sink - m)
        return e_list, l

    for bi in range(nb):
        qrows = pl.ds(bi * tq, tq)
        krows = pl.ds(bi * s_len, s_len)
        k_srcs, v_srcs, is_own = [], [], []
        if has_ctx:
            k_srcs.append([r[...].astype(BF16) for r in kc_refs])
            v_srcs.append(vc_ref[...].astype(BF16))
            is_own.append(False)
        k_srcs.append([r[krows, :].astype(BF16) for r in ko_refs])
        v_srcs.append(vo_ref[krows, :].astype(BF16))
        is_own.append(True)

        def mask_own(s):
            if has_bias:
                s = s + bias_ref[...]
            if has_band:
                qpos = pl.program_id(2) * tq + lax.broadcasted_iota(jnp.int32, (tq, 1), 0)
                kpos = lax.broadcasted_iota(jnp.int32, (1, s_len), 1)
                s = jnp.where(jnp.abs(kpos - qpos) <= D_WINDOW, s, NEG)
            return s

        if kind == "diff":
            q = q_refs[0][qrows, :]
            q1 = jnp.where(lane < A_QK_DIM, q, 0.0).astype(BF16)
            q2 = jnp.where(lane < A_QK_DIM, 0.0, q).astype(BF16)
            s1 = [_dot_nt(q1, ks[0]) * scale for ks in k_srcs]
            s2 = [_dot_nt(q2, ks[0]) * scale for ks in k_srcs]
            e1, l1 = softmax_parts(s1)
            e2, l2 = softmax_parts(s2)
            lv = lam_ref[...]
            lam = (jnp.exp(jnp.sum(lv[0:1] * lv[1:2], axis=-1, keepdims=True))
                   - jnp.exp(jnp.sum(lv[2:3] * lv[3:4], axis=-1, keepdims=True)) + lam_init)
            r1 = 1.0 / l1
            r2 = lam / l2
            o = None
            for a, b, v in zip(e1, e2, v_srcs):
                w = (a * r1 - b * r2).astype(BF16)
                pv = _dot(w, v)
                o = pv if o is None else o + pv
            o = o * lax.rsqrt(jnp.mean(o * o, axis=-1, keepdims=True) + EPS) * sg_ref[...] * (1.0 - lam_init)
        else:
            qs = [r[qrows, :].astype(BF16) for r in q_refs]
            s_list = []
            for ks, own in zip(k_srcs, is_own):
                s = None
                for qp, kp in zip(qs, ks):
                    d = _dot_nt(qp, kp)
                    s = d if s is None else s + d
                s = s * scale
                s_list.append(mask_own(s) if own else s)
            e_list, l = softmax_parts(s_list)
            o = None
            for e, v in zip(e_list, v_srcs):
                pv = _dot(e.astype(BF16), v)
                o = pv if o is None else o + pv
            o = o * (1.0 / l)
        o_ref[qrows, :] = o.astype(o_ref.dtype)


def _attention(*, kind, latent, q_parts, ko_parts, vo, kc_parts=(), vc=None, bias=None, lam=None, subln=None,
               sink=None, o_arr, o_blk, scale, lam_init=0.0, has_band=False, name):
    n_parts = len(q_parts)
    has_ctx = latent
    if latent:
        tq, s_len, nb = 512, DEC_SEQ, 1
        grid = (DEC_BATCH, N_HEADS, DEC_SEQ // tq)
        qpb = DEC_SEQ // tq
        q_row = lambda b, h, t: N_PROMPT // tq + b * qpb + t
        k_row = lambda b, h, t: N_SEG_P + b
        qblk = tq
    else:
        tq, s_len, nb = SEQ, SEQ, SEG // SEQ
        grid = (N_SEG_P, N_HEADS, 1)
        q_row = lambda b, h, t: b
        k_row = lambda b, h, t: b
        qblk = SEG
    args, specs = [], []

    def add(arr, shape, imap, **kw):
        args.append(arr)
        specs.append(pl.BlockSpec(shape, imap, **kw))

    for arr, f in q_parts:
        add(arr, (qblk, LANES), lambda b, h, t, f=f: (q_row(b, h, t), f(h)))
    for arr, f in list(ko_parts) + [vo]:
        add(arr, (SEG, LANES), lambda b, h, t, f=f: (k_row(b, h, t), f(h)))
    if has_ctx:
        for arr, f in list(kc_parts) + [vc]:
            add(arr, (None, PAST_LEN, LANES), lambda b, h, t, f=f: (b, 0, f(h)))
    if bias is not None:
        add(bias, (None, tq, DEC_SEQ), lambda b, h, t: (h, t, 0))
    if kind == "diff":
        add(lam, lam.shape, lambda b, h, t: (0, 0))
        add(subln, subln.shape, lambda b, h, t: (0, 0))
    if sink is not None:
        args.append(sink)
        specs.append(pl.BlockSpec(memory_space=pltpu.SMEM))
    n_in = len(args)
    args.append(o_arr)
    specs.append(pl.BlockSpec(memory_space=pl.ANY))
    kern = functools.partial(_attn_kernel_aliased, kind=kind, n_parts=n_parts, has_ctx=has_ctx,
                             has_bias=bias is not None, has_band=has_band, has_sink=sink is not None,
                             nb=nb, s_len=s_len, tq=tq, scale=scale, lam_init=lam_init)
    return pl.pallas_call(
        kern,
        grid=grid,
        in_specs=specs,
        out_specs=pl.BlockSpec((qblk, LANES), lambda b, h, t: (q_row(b, h, t), o_blk(h))),
        out_shape=jax.ShapeDtypeStruct(o_arr.shape, o_arr.dtype),
        input_output_aliases={n_in: 0},
        compiler_params=_cparams(("arbitrary", "arbitrary", "arbitrary"), 48),
        name=name,
    )(*args)


def _attn_kernel_aliased(*refs, **kw):
    _attn_kernel(*refs[:-2], refs[-1], **kw)


def _outproj_kernel(o_ref, w_ref, x_ref, mt_ref, y_ref):
    y_ref[...] = x_ref[...] + mt_ref[0, 2:3, :] * _dot(o_ref[...], w_ref[...])


def _out_projection(o, w, x, mt):
    return pl.pallas_call(
        _outproj_kernel,
        grid=(N_SEG, D_MODEL // TN),
        in_specs=[
            pl.BlockSpec((SEG, D_MODEL), lambda i, j: (i, 0)),
            pl.BlockSpec((D_MODEL, TN), lambda i, j: (0, j)),
            pl.BlockSpec((SEG, TN), lambda i, j: (i, j)),
            pl.BlockSpec((1, 6, TN), lambda i, j: (i, 0, j)),
        ],
        out_specs=pl.BlockSpec((SEG, TN), lambda i, j: (i, j)),
        out_shape=jax.ShapeDtypeStruct((N_TOK, D_MODEL), F32),
        compiler_params=_cparams(("arbitrary", "arbitrary"), 40),
        name="out_proj",
    )(o, w, x, mt)


def _router_kernel(x_ref, mt_ref, g_ref, wh_ref, wl_ref, rb_ref, h_ref, e_ref, gt_ref):
    x = x_ref[...]
    y = x * lax.rsqrt(jnp.mean(x * x, axis=-1, keepdims=True) + EPS) * g_ref[...]
    h = y * (1.0 + mt_ref[0, 4:5, :]) + mt_ref[0, 3:4, :]
    hh = h.astype(BF16)
    h_ref[...] = hh
    hl = (h - hh.astype(F32)).astype(BF16)
    wh, wl = wh_ref[...], wl_ref[...]
    logits = _dot_nt(wh, hh) + _dot_nt(wh, hl) + _dot_nt(wl, hh)
    scores = _sigmoid(logits)
    sel = scores + rb_ref[...]
    per = N_EXPERTS // N_GROUPS
    sc = [scores[e:e + 1, :] for e in range(N_EXPERTS)]
    sl = [sel[e:e + 1, :] for e in range(N_EXPERTS)]
    best_g, best_v = None, None
    for gi in range(N_GROUPS):
        a, b, c, d = sl[gi * per:(gi + 1) * per]
        hi1, lo1, hi2, lo2 = jnp.maximum(a, b), jnp.minimum(a, b), jnp.maximum(c, d), jnp.minimum(c, d)
        gs = jnp.maximum(hi1, hi2) + jnp.maximum(jnp.minimum(hi1, hi2), jnp.maximum(lo1, lo2))
        if gi == 0:
            best_g, best_v = jnp.zeros_like(gs, dtype=jnp.int32), gs
        else:
            better = gs > best_v
            best_g = jnp.where(better, gi, best_g)
            best_v = jnp.where(better, gs, best_v)
    masked = [jnp.where(best_g == (e // per), sl[e], NEG) for e in range(N_EXPERTS)]
    i1, v1 = jnp.zeros_like(best_g), masked[0]
    for e in range(1, N_EXPERTS):
        better = masked[e] > v1
        i1 = jnp.where(better, e, i1)
        v1 = jnp.where(better, masked[e], v1)
    i2, v2 = None, None
    for e in range(N_EXPERTS):
        cand = jnp.where(i1 == e, -2e30, masked[e])
        if e == 0:
            i2, v2 = jnp.zeros_like(best_g), cand
        else:
            better = cand > v2
            i2 = jnp.where(better, e, i2)
            v2 = jnp.where(better, cand, v2)
    g1 = jnp.zeros_like(v1)
    g2 = jnp.zeros_like(v1)
    for e in range(N_EXPERTS):
        g1 = jnp.where(i1 == e, sc[e], g1)
        g2 = jnp.where(i2 == e, sc[e], g2)
    tot = g1 + g2
    rows = i1.shape[1]
    e_ref[...] = jnp.concatenate([i1, i2, jnp.zeros((6, rows), jnp.int32)], axis=0)
    gt_ref[...] = jnp.concatenate([g1 / tot, g2 / tot, jnp.zeros((6, rows), F32)], axis=0)


def _router(x, mt, g, router_w, router_b):
    wt = router_w.T
    wh = wt.astype(BF16)
    wl = (wt - wh.astype(F32)).astype(BF16)
    return pl.pallas_call(
        _router_kernel,
        grid=(N_SEG,),
        in_specs=[
            pl.BlockSpec((SEG, D_MODEL), lambda i: (i, 0)),
            pl.BlockSpec((1, 6, D_MODEL), lambda i: (i, 0, 0)),
            pl.BlockSpec((1, D_MODEL), lambda i: (0, 0)),
            pl.BlockSpec((N_EXPERTS, D_MODEL), lambda i: (0, 0)),
            pl.BlockSpec((N_EXPERTS, D_MODEL), lambda i: (0, 0)),
            pl.BlockSpec((N_EXPERTS, 1), lambda i: (0, 0)),
        ],
        out_specs=[
            pl.BlockSpec((SEG, D_MODEL), lambda i: (i, 0)),
            pl.BlockSpec((8, SEG), lambda i: (0, i)),
            pl.BlockSpec((8, SEG), lambda i: (0, i)),
        ],
        out_shape=[
            jax.ShapeDtypeStruct((N_TOK, D_MODEL), BF16),
            jax.ShapeDtypeStruct((8, N_TOK), jnp.int32),
            jax.ShapeDtypeStruct((8, N_TOK), F32),
        ],
        compiler_params=_cparams(("arbitrary",), 48),
        name="ffn_norm_router",
    )(x, mt, g, wh, wl, router_b.reshape(N_EXPERTS, 1))


def _expert_changed(te_ref, i):
    return jnp.logical_or(i == 0, te_ref[i] != te_ref[jnp.maximum(i - 1, 0)])


def _moe_up_kernel(te_ref, na_ref, xs_ref, wg_ref, wu_ref, hid_ref, wgb_ref, wub_ref):
    i = pl.program_id(0)

    @pl.when(_expert_changed(te_ref, i))
    def _():
        wgb_ref[...] = wg_ref[...].astype(BF16)
        wub_ref[...] = wu_ref[...].astype(BF16)

    @pl.when(i < na_ref[0])
    def _():
        x = xs_ref[...]
        g = _dot(x, wgb_ref[...])
        u = _dot(x, wub_ref[...])
        hid_ref[...] = (g * _sigmoid(g) * u).astype(BF16)

    @pl.when(i >= na_ref[0])
    def _():
        hid_ref[...] = jnp.zeros_like(hid_ref)


def _moe_down_kernel(te_ref, na_ref, hid_ref, wd_ref, gate_ref, ys_ref, wdb_ref):
    i = pl.program_id(0)

    @pl.when(_expert_changed(te_ref, i))
    def _():
        wdb_ref[...] = wd_ref[...].astype(BF16)

    @pl.when(i < na_ref[0])
    def _():
        ys_ref[...] = _dot(hid_ref[...], wdb_ref[...]) * gate_ref[...]

    @pl.when(i >= na_ref[0])
    def _():
        ys_ref[...] = jnp.zeros_like(ys_ref)


def _moe_experts(xs, gate_rows, tile_expert, n_active, w_gate, w_up, w_down):
    hid = pl.pallas_call(
        _moe_up_kernel,
        grid_spec=pltpu.PrefetchScalarGridSpec(
            num_scalar_prefetch=2,
            grid=(MOE_TILES,),
            in_specs=[
                pl.BlockSpec((MOE_TM, D_MODEL), lambda i, te, na: (i, 0)),
                pl.BlockSpec((None, D_MODEL, D_EXPERT), lambda i, te, na: (te[i], 0, 0)),
                pl.BlockSpec((None, D_MODEL, D_EXPERT), lambda i, te, na: (te[i], 0, 0)),
            ],
            out_specs=pl.BlockSpec((MOE_TM, D_EXPERT), lambda i, te, na: (i, 0)),
            scratch_shapes=[pltpu.VMEM((D_MODEL, D_EXPERT), BF16), pltpu.VMEM((D_MODEL, D_EXPERT), BF16)],
        ),
        out_shape=jax.ShapeDtypeStruct((MOE_ROWS, D_EXPERT), BF16),
        compiler_params=_cparams(("arbitrary",), 52),
        name="moe_up",
    )(tile_expert, n_active, xs, w_gate, w_up)
    return pl.pallas_call(
        _moe_down_kernel,
        grid_spec=pltpu.PrefetchScalarGridSpec(
            num_scalar_prefetch=2,
            grid=(MOE_TILES,),
            in_specs=[
                pl.BlockSpec((MOE_TM, D_EXPERT), lambda i, te, na: (i, 0)),
                pl.BlockSpec((None, D_EXPERT, D_MODEL), lambda i, te, na: (te[i], 0, 0)),
                pl.BlockSpec((MOE_TM, 1), lambda i, te, na: (i, 0)),
            ],
            out_specs=pl.BlockSpec((MOE_TM, D_MODEL), lambda i, te, na: (i, 0)),
            scratch_shapes=[pltpu.VMEM((D_EXPERT, D_MODEL), BF16)],
        ),
        out_shape=jax.ShapeDtypeStruct((MOE_ROWS, D_MODEL), F32),
        compiler_params=_cparams(("arbitrary",), 40),
        name="moe_down",
    )(tile_expert, n_active, hid, w_down, gate_rows)


def _route_plan(eidx, gates):
    flat_e = eidx[:2].reshape(-1)
    flat_g = gates[:2].reshape(-1)
    n_asg = flat_e.shape[0]
    cnt = jnp.bincount(flat_e, length=N_EXPERTS).astype(jnp.int32)
    tiles = (cnt + MOE_TM - 1) // MOE_TM
    tile_end = jnp.cumsum(tiles)
    row_off = (tile_end - tiles) * MOE_TM
    start = jnp.cumsum(cnt) - cnt
    order = jnp.argsort(flat_e, stable=True).astype(jnp.int32)
    n_active = tile_end[-1]
    tile_ids = jnp.arange(MOE_TILES, dtype=jnp.int32)
    te = jnp.searchsorted(tile_end, jnp.minimum(tile_ids, n_active - 1), side="right").astype(jnp.int32)
    row_e = jnp.repeat(te, MOE_TM)
    rows = jnp.arange(MOE_ROWS, dtype=jnp.int32)
    rank = rows - row_off[row_e]
    valid = jnp.logical_and(rank < cnt[row_e], rows < n_active * MOE_TM)
    src_asg = order[jnp.clip(start[row_e] + rank, 0, n_asg - 1)]
    src_tok = jnp.where(valid, src_asg % N_TOK, 0)
    gate_rows = jnp.where(valid, flat_g[src_asg], 0.0)
    sorted_e = flat_e[order]
    dest_sorted = row_off[sorted_e] + jnp.arange(n_asg, dtype=jnp.int32) - start[sorted_e]
    pos = jnp.zeros((n_asg,), jnp.int32).at[order].set(dest_sorted)
    return src_tok, gate_rows.reshape(MOE_ROWS, 1), te, n_active.reshape(1).astype(jnp.int32), pos


def _combine_kernel(x_ref, y0_ref, y1_ref, mt_ref, fg_ref, o_ref, *, final):
    x = x_ref[...] + mt_ref[0, 5:6, :] * (y0_ref[...] + y1_ref[...])
    if final:
        x = x * lax.rsqrt(jnp.mean(x * x, axis=-1, keepdims=True) + EPS) * fg_ref[...]
    o_ref[...] = x


def _combine(x, ysel, mt, final_g, final):
    tm = 256
    return pl.pallas_call(
        functools.partial(_combine_kernel, final=final),
        grid=(N_TOK // tm,),
        in_specs=[
            pl.BlockSpec((tm, D_MODEL), lambda i: (i, 0)),
            pl.BlockSpec((None, tm, D_MODEL), lambda i: (0, i, 0)),
            pl.BlockSpec((None, tm, D_MODEL), lambda i: (1, i, 0)),
            pl.BlockSpec((1, 6, D_MODEL), lambda i: (i // (SEG // tm), 0, 0)),
            pl.BlockSpec((1, D_MODEL), lambda i: (0, 0)),
        ],
        out_specs=pl.BlockSpec((tm, D_MODEL), lambda i: (i, 0)),
        out_shape=jax.ShapeDtypeStruct((N_TOK, D_MODEL), F32),
        compiler_params=_cparams(("arbitrary",), 40),
        name="moe_combine",
    )(x, ysel, ysel, mt, final_g)


def _rope_tables(dim):
    half = dim // 2
    inv = ROPE_THETA ** (-jnp.arange(0, half, 2, dtype=F32) / half)
    t = jnp.arange(DEC_SEQ)
    ang_r = (t // GRID_W).astype(F32)[:, None] * inv[None, :]
    ang_c = (t % GRID_W).astype(F32)[:, None] * inv[None, :]
    ang = jnp.concatenate([ang_r, ang_r, ang_c, ang_c], axis=-1)
    cos, sin = jnp.cos(ang), jnp.sin(ang)
    reps = LANES // dim
    cos = jnp.tile(cos, (1, reps))
    sin = jnp.tile(sin, (1, reps))
    sh = dim // 4
    second = (np.arange(LANES) % (2 * sh)) >= sh
    sp = jnp.where(second[None, :], sin, 0.0)
    sm = jnp.where(second[None, :], 0.0, -sin)
    return cos, sp, sm


def _neighbourhood_bias(rpb):
    rows = DEC_SEQ // GRID_W
    kh = min(NA_ROWS, rows)
    r = np.arange(rows)
    r0 = np.clip(r - kh // 2, 0, rows - kh)
    kr = np.arange(rows)
    row_ok = (kr[None, :] >= r0[:, None]) & (kr[None, :] < r0[:, None] + kh)
    dr_idx = np.clip(kr[None, :] - r[:, None] + NA_ROWS - 1, 0, 2 * NA_ROWS - 2)
    c = np.arange(GRID_W)
    ws = np.clip(c - NA_COLS // 2, 0, GRID_W - NA_COLS)
    kc = np.arange(GRID_W)
    col_ok = (kc[None, :] >= ws[:, None]) & (kc[None, :] < ws[:, None] + NA_COLS)
    dc_idx = np.clip(kc[None, :] - c[:, None], -(NA_COLS - 1), NA_COLS - 1) + NA_COLS - 1
    b = rpb.astype(F32)[:, dr_idx[:, :, None, None], dc_idx[None, None, :, :]]
    ok = row_ok[:, :, None, None] & col_ok[None, None, :, :]
    b = jnp.where(ok[None], b, NEG)
    return jnp.transpose(b, (0, 1, 3, 2, 4)).reshape(N_HEADS, DEC_SEQ, DEC_SEQ)


def _even_layer(x, mt, l, i, rope64, cache_a_k, cache_a_v, cache_b_ckv, cache_b_krope, norm_mix_g, ev_w_in,
                ev_lambda, ev_subln_g, ev_q_norm_g, ev_kv_norm_g, ev_w_uq, ev_w_ukv, ev_w_out):
    w = ev_w_in[i]
    kr = w[:, 4352:4416]
    z = jnp.zeros_like(kr)
    w_in = jnp.concatenate([w[:, :3840], kr, z, z, kr, w[:, 3840:4352]], axis=1).astype(BF16)
    proj = _projection(x, D_MODEL, 0, D_MODEL, w_in, prologue="modulate", mt=mt, g=norm_mix_g[l][None, :],
                       rope=rope64, sh=A_QK_DIM // 4, rope_full=(0, 1, 2, 3), rope_part=(7, (2, 4)),
                       norm_tile=8, norm_g=ev_kv_norm_g[i][None, :], name="even_in_proj")
    wq = ev_w_uq[i].reshape(B_Q_RANK, N_HEADS, HEAD_DIM + B_ROPE_DIM)
    w_uq = jnp.concatenate([wq[:, :, :HEAD_DIM].reshape(B_Q_RANK, -1), wq[:, :, HEAD_DIM:].reshape(B_Q_RANK, -1)],
                           axis=1).astype(BF16)
    b_q = _projection(proj, 1024, 3, B_Q_RANK, w_uq, prologue="rmsnorm", g=ev_q_norm_g[i][None, :], rope=rope64,
                      sh=B_ROPE_DIM // 4, rope_full=(2,), out_dtype=BF16, name="mla_q_up")
    w_ukv = ev_w_ukv[i].astype(BF16)
    kv = _projection(proj, 512, 8, B_KV_RANK, w_ukv, prologue="cast", out_dtype=BF16, name="mla_kv_up")
    kv_ctx = _projection(cache_b_ckv[:, i].reshape(DEC_BATCH * PAST_LEN, B_KV_RANK), 512, 0, B_KV_RANK, w_ukv,
                         prologue="cast", out_dtype=BF16, name="mla_kv_up_ctx")
    kv_ctx = kv_ctx.reshape(DEC_BATCH, PAST_LEN, 2 * N_HEADS * HEAD_DIM)
    krc = cache_b_krope[:, i]
    zc = jnp.zeros_like(krc)
    kr_ctx = jnp.concatenate([krc, zc, zc, krc], axis=-1)
    ak_ctx = cache_a_k[:, i].reshape(DEC_BATCH, PAST_LEN, N_HEADS * HEAD_DIM)
    av_ctx = cache_a_v[:, i].reshape(DEC_BATCH, PAST_LEN, N_HEADS * HEAD_DIM)
    lam_init = 0.8 - 0.6 * math.exp(-0.3 * l)
    o = jnp.zeros((N_TOK, D_MODEL), BF16)
    for latent in (False, True):
        tag = "lat" if latent else "ctx"
        o = _attention(kind="diff", latent=latent, q_parts=[(proj, lambda h: h)], ko_parts=[(proj, lambda h: 8 + h)],
                       vo=(proj, lambda h: 16 + h), kc_parts=[(ak_ctx, lambda h: h)], vc=(av_ctx, lambda h: h),
                       lam=ev_lambda[i], subln=ev_subln_g[i][None, :], o_arr=o, o_blk=lambda h: h,
                       scale=A_QK_DIM ** -0.5, lam_init=lam_init, name="diff_attn_" + tag)
        o = _attention(kind="mla", latent=latent, q_parts=[(b_q, lambda h: h), (b_q, lambda h: 8 + h // 2)],
                       ko_parts=[(kv, lambda h: 2 * h), (proj, lambda h: 30 + h % 2)], vo=(kv, lambda h: 2 * h + 1),
                       kc_parts=[(kv_ctx, lambda h: 2 * h), (kr_ctx, lambda h: h % 2)], vc=(kv_ctx, lambda h: 2 * h + 1),
                       o_arr=o, o_blk=lambda h: 8 + h, scale=(HEAD_DIM + B_ROPE_DIM) ** -0.5, name="mla_attn_" + tag)
    y = _out_projection(o, ev_w_out[i].astype(BF16), x, mt)
    p = proj[:N_PROMPT]
    states = (p[:, 1024:2048].reshape(BATCH, SEQ, N_HEADS, HEAD_DIM),
              p[:, 2048:3072].reshape(BATCH, SEQ, N_HEADS, HEAD_DIM),
              p[:, 4096:4608].reshape(BATCH, SEQ, B_KV_RANK),
              p[:, 3840:3904].reshape(BATCH, SEQ, B_ROPE_DIM))
    return y, states


def _odd_layer(x, mt, l, i, rope128, cache_c_k, cache_c_v, cache_d_k, cache_d_v, norm_mix_g, od_w_in, od_rpb,
               od_sink, od_w_out):
    proj = _projection(x, D_MODEL, 0, D_MODEL, od_w_in[i].astype(BF16), prologue="modulate", mt=mt,
                       g=norm_mix_g[l][None, :], rope=rope128, sh=HEAD_DIM // 4, rope_full=(6, 7),
                       rope_part=(8, (0, 2)), name="odd_in_proj")
    ck_ctx = cache_c_k[:, i].reshape(DEC_BATCH, PAST_LEN, N_HEADS * HEAD_DIM)
    cv_ctx = cache_c_v[:, i].reshape(DEC_BATCH, PAST_LEN, N_HEADS * HEAD_DIM)
    dk_ctx = cache_d_k[:, i].reshape(DEC_BATCH, PAST_LEN, D_KV_HEADS * HEAD_DIM)
    dv_ctx = cache_d_v[:, i].reshape(DEC_BATCH, PAST_LEN, D_KV_HEADS * HEAD_DIM)
    bias = _neighbourhood_bias(od_rpb[i])
    sink = od_sink[i].astype(F32)
    o = jnp.zeros((N_TOK, D_MODEL), BF16)
    for latent in (False, True):
        tag = "lat" if latent else "ctx"
        o = _attention(kind="plain", latent=latent, q_parts=[(proj, lambda h: h)], ko_parts=[(proj, lambda h: 8 + h)],
                       vo=(proj, lambda h: 16 + h), kc_parts=[(ck_ctx, lambda h: h)], vc=(cv_ctx, lambda h: h),
                       bias=bias if latent else None, o_arr=o, o_blk=lambda h: h, scale=HEAD_DIM ** -0.5,
                       name="nbr_attn_" + tag)
        o = _attention(kind="plain", latent=latent, q_parts=[(proj, lambda h: 24 + h)],
                       ko_parts=[(proj, lambda h: 32 + h // D_GROUP)], vo=(proj, lambda h: 34 + h // D_GROUP),
                       kc_parts=[(dk_ctx, lambda h: h // D_GROUP)], vc=(dv_ctx, lambda h: h // D_GROUP), sink=sink,
                       o_arr=o, o_blk=lambda h: 8 + h, scale=HEAD_DIM ** -0.5, has_band=latent,
                       name="win_attn_" + tag)
    y = _out_projection(o, od_w_out[i].astype(BF16), x, mt)
    p = proj[:N_PROMPT]
    states = (p[:, 1024:2048].reshape(BATCH, SEQ, N_HEADS, HEAD_DIM),
              p[:, 2048:3072].reshape(BATCH, SEQ, N_HEADS, HEAD_DIM),
              p[:, 4096:4352].reshape(BATCH, SEQ, D_KV_HEADS, HEAD_DIM),
              p[:, 4352:4608].reshape(BATCH, SEQ, D_KV_HEADS, HEAD_DIM))
    return y, states


def _moe_layer(x, mt, g, router_w, router_b, w_gate, w_up, w_down, final_g, final):
    h2, eidx, gates = _router(x, mt, g, router_w, router_b)
    src_tok, gate_rows, te, n_active, pos = _route_plan(eidx, gates)
    xs = jnp.take(h2, src_tok, axis=0)
    ys = _moe_experts(xs, gate_rows, te, n_active, w_gate, w_up, w_down)
    ysel = jnp.take(ys, pos, axis=0).reshape(2, N_TOK, D_MODEL)
    return _combine(x, ysel, mt, final_g, final)


def kernel(x_prompt, x_sample, cache_a_k, cache_a_v, cache_b_ckv, cache_b_krope, cache_c_k, cache_c_v, cache_d_k, cache_d_v, c, c_ctx, w_ada, b_ada, norm_mix_g, norm_ffn_g, ev_w_in, ev_lambda, ev_subln_g, ev_q_norm_g, ev_kv_norm_g, ev_w_uq, ev_w_ukv, ev_w_out, od_w_in, od_rpb, od_sink, od_w_out, router_w, router_b, moe_w_gate, moe_w_up, moe_w_down, final_g):
    x = jnp.concatenate([x_prompt.reshape(N_PROMPT, D_MODEL), x_sample.reshape(DEC_BATCH * DEC_SEQ, D_MODEL)], axis=0)
    cond8 = jnp.concatenate([c_ctx[None, :], c, jnp.zeros((3, D_MODEL), F32)], axis=0)
    mod = _adaln(cond8, w_ada, b_ada)
    seg_row = np.array([0] * N_SEG_P + [1 + b for b in range(DEC_BATCH)])
    mt_all = mod[:, seg_row].reshape(DEPTH, N_SEG, 6, D_MODEL)
    rope64 = _rope_tables(A_QK_DIM)
    rope128 = _rope_tables(HEAD_DIM)
    even_states, odd_states = [], []
    for l in range(DEPTH):
        i = l // 2
        mt = mt_all[l]
        if l % 2 == 0:
            x, st = _even_layer(x, mt, l, i, rope64, cache_a_k, cache_a_v, cache_b_ckv, cache_b_krope, norm_mix_g,
                                ev_w_in, ev_lambda, ev_subln_g, ev_q_norm_g, ev_kv_norm_g, ev_w_uq, ev_w_ukv, ev_w_out)
            even_states.append(st)
        else:
            x, st = _odd_layer(x, mt, l, i, rope128, cache_c_k, cache_c_v, cache_d_k, cache_d_v, norm_mix_g,
                               od_w_in, od_rpb, od_sink, od_w_out)
            odd_states.append(st)
        x = _moe_layer(x, mt, norm_ffn_g[l][None, :], router_w, router_b, moe_w_gate[l], moe_w_up[l], moe_w_down[l],
                       final_g[None, :], final=(l == DEPTH - 1))
    y_prompt = x[:N_PROMPT].reshape(BATCH, SEQ, D_MODEL)
    y_sample = x[N_PROMPT:].reshape(DEC_BATCH, DEC_SEQ, D_MODEL)
    new_even = tuple(jnp.stack([st[k] for st in even_states], axis=1) for k in range(4))
    new_odd = tuple(jnp.stack([st[k] for st in odd_states], axis=1) for k in range(4))
    return (y_prompt, y_sample) + new_even + new_odd
```

```python
import functools
import math

import numpy as np
import jax
import jax.numpy as jnp
from jax import lax
from jax.experimental import pallas as pl
from jax.experimental.pallas import tpu as pltpu

D_MODEL = 2048
BATCH = 32
SEQ = 256
DEPTH = 2
DEC_BATCH = 4
DEC_SEQ = 1024
PAST_LEN = 256
GRID_W = 64
HEAD_DIM = 128
N_HEADS = 8
A_QK_DIM = 64
B_Q_RANK = 768
B_KV_RANK = 512
B_ROPE_DIM = 64
NA_ROWS = 8
NA_COLS = 16
D_KV_HEADS = 2
D_GROUP = 4
D_WINDOW = 128
N_EXPERTS = 16
N_GROUPS = 4
D_EXPERT = 1024
ROPE_THETA = 10000.0
EPS = 1e-6
NEG = -1e30

SEG = 1024
N_PROMPT = BATCH * SEQ
N_TOK = N_PROMPT + DEC_BATCH * DEC_SEQ
N_SEG = N_TOK // SEG
N_SEG_P = N_PROMPT // SEG
PROJ_W = 4608
TN = 512
LANES = 128
MOE_TM = 256
MOE_ROWS = 2 * N_TOK + N_EXPERTS * MOE_TM
MOE_TILES = MOE_ROWS // MOE_TM

F32 = jnp.float32
BF16 = jnp.bfloat16


def _cparams(sem, vmem_mb):
    return pltpu.CompilerParams(dimension_semantics=sem, vmem_limit_bytes=vmem_mb * 1024 * 1024)


def _dot(a, b):
    return jnp.dot(a, b, preferred_element_type=F32)


def _dot_nt(a, b):
    return lax.dot_general(a, b, (((1,), (1,)), ((), ())), preferred_element_type=F32)


def _sigmoid(x):
    return 1.0 / (1.0 + jnp.exp(-x))


def _adaln_kernel(c_ref, w_ref, b_ref, o_ref):
    c = c_ref[...]
    a = (c * _sigmoid(c)).astype(BF16)
    o_ref[...] = _dot(a, w_ref[...].astype(BF16)) + b_ref[...]


def _adaln(cond8, w_ada, b_ada):
    tn = 1024
    n = 6 * D_MODEL
    return pl.pallas_call(
        _adaln_kernel,
        grid=(DEPTH, n // tn),
        in_specs=[
            pl.BlockSpec((8, D_MODEL), lambda l, j: (0, 0)),
            pl.BlockSpec((None, D_MODEL, tn), lambda l, j: (l, 0, j)),
            pl.BlockSpec((None, 1, tn), lambda l, j: (l, 0, j)),
        ],
        out_specs=pl.BlockSpec((None, 8, tn), lambda l, j: (l, 0, j)),
        out_shape=jax.ShapeDtypeStruct((DEPTH, 8, n), F32),
        compiler_params=_cparams(("arbitrary", "arbitrary"), 40),
        name="adaln",
    )(cond8, w_ada, b_ada.reshape(DEPTH, 1, n))


def _rope(a, cos, sp, sm, sh):
    return a * cos + pltpu.roll(a, sh, 1) * sp + pltpu.roll(a, LANES - sh, 1) * sm


def _proj_kernel(x_ref, mt_ref, g_ref, w_ref, cos_ref, sp_ref, sm_ref, ng_ref, o_ref, xn_ref, *,
                 prologue, k_in, sh, rope_full, rope_part, norm_tile):
    i = pl.program_id(0)
    j = pl.program_id(1)

    @pl.when(j == 0)
    def _():
        x = x_ref[:, :k_in]
        if prologue == "cast":
            xn_ref[...] = x.astype(BF16)
        else:
            y = x * lax.rsqrt(jnp.mean(x * x, axis=-1, keepdims=True) + EPS) * g_ref[...]
            if prologue == "modulate":
                y = y * (1.0 + mt_ref[0, 1:2, :]) + mt_ref[0, 0:1, :]
            xn_ref[...] = y.astype(BF16)

    acc = _dot(xn_ref[...], w_ref[...])
    tn = acc.shape[1]
    n_grp = tn // LANES
    is_lat = i >= N_SEG_P

    def any_of(tiles):
        c = jnp.bool_(False)
        for t in tiles:
            c = jnp.logical_or(c, j == t)
        return c

    def store_roped(groups):
        cos, sp, sm = cos_ref[...], sp_ref[...], sm_ref[...]
        for g in range(n_grp):
            a = acc[:, g * LANES:(g + 1) * LANES]
            if g in groups:
                a = _rope(a, cos, sp, sm, sh)
            o_ref[:, g * LANES:(g + 1) * LANES] = a.astype(o_ref.dtype)

    full = jnp.logical_and(is_lat, any_of(rope_full))
    handled = full
    pl.when(full)(lambda: store_roped(range(n_grp)))
    if rope_part is not None:
        part = jnp.logical_and(is_lat, j == rope_part[0])
        handled = jnp.logical_or(handled, part)
        pl.when(part)(lambda: store_roped(range(*rope_part[1])))
    if norm_tile is not None:
        isn = j == norm_tile
        handled = jnp.logical_or(handled, isn)

        @pl.when(isn)
        def _():
            y = acc * lax.rsqrt(jnp.mean(acc * acc, axis=-1, keepdims=True) + EPS) * ng_ref[...]
            o_ref[...] = y.astype(o_ref.dtype)

    @pl.when(jnp.logical_not(handled))
    def _():
        o_ref[...] = acc.astype(o_ref.dtype)


def _projection(x, x_blk_w, x_blk_idx, k_in, w, *, prologue, mt=None, g=None, rope=None, sh=16,
                rope_full=(), rope_part=None, norm_tile=None, norm_g=None, out_dtype=F32, name):
    m = x.shape[0]
    n = w.shape[1]
    assert m % SEG == 0 and n % TN == 0 and w.shape[0] == k_in
    if mt is None:
        mt = jnp.zeros((m // SEG, 6, LANES), F32)
    if g is None:
        g = jnp.ones((1, k_in), F32)
    if rope is None:
        rope = tuple(jnp.zeros((SEG, LANES), F32) for _ in range(3))
    if norm_g is None:
        norm_g = jnp.ones((1, TN), F32)
    kern = functools.partial(_proj_kernel, prologue=prologue, k_in=k_in, sh=sh, rope_full=tuple(rope_full),
                             rope_part=rope_part, norm_tile=norm_tile)
    full2 = lambda i, j: (0, 0)
    return pl.pallas_call(
        kern,
        grid=(m // SEG, n // TN),
        in_specs=[
            pl.BlockSpec((SEG, x_blk_w), lambda i, j: (i, x_blk_idx)),
            pl.BlockSpec((1, 6, mt.shape[2]), lambda i, j: (i, 0, 0)),
            pl.BlockSpec((1, k_in), full2),
            pl.BlockSpec((k_in, TN), lambda i, j: (0, j)),
            pl.BlockSpec((SEG, LANES), full2),
            pl.BlockSpec((SEG, LANES), full2),
            pl.BlockSpec((SEG, LANES), full2),
            pl.BlockSpec((1, TN), full2),
        ],
        out_specs=pl.BlockSpec((SEG, TN), lambda i, j: (i, j)),
        out_shape=jax.ShapeDtypeStruct((m, n), out_dtype),
        scratch_shapes=[pltpu.VMEM((SEG, k_in), BF16)],
        compiler_params=_cparams(("arbitrary", "arbitrary"), 48),
        name=name,
    )(x, mt, g, w, *rope, norm_g)


def _attn_kernel(*refs, kind, n_parts, has_ctx, has_bias, has_band, has_sink, nb, s_len, tq, scale, lam_init):
    it = iter(refs)
    q_refs = [next(it) for _ in range(n_parts)]
    ko_refs = [next(it) for _ in range(n_parts)]
    vo_ref = next(it)
    kc_refs, vc_ref = [], None
    if has_ctx:
        kc_refs = [next(it) for _ in range(n_parts)]
        vc_ref = next(it)
    bias_ref = next(it) if has_bias else None
    lam_ref = sg_ref = None
    if kind == "diff":
        lam_ref = next(it)
        sg_ref = next(it)
    sink_ref = next(it) if has_sink else None
    o_ref = next(it)

    sink = sink_ref[pl.program_id(1)] if has_sink else None
    lane = lax.broadcasted_iota(jnp.int32, (1, LANES), 1)

    def softmax_parts(s_list):
        m = None
        for s in s_list:
            mm = jnp.max(s, axis=-1, keepdims=True)
            m = mm if m is None else jnp.maximum(m, mm)
        if sink is not None:
            m = jnp.maximum(m, sink)
        e_list = [jnp.exp(s - m) for s in s_list]
        l = None
        for e in e_list:
            ss = jnp.sum(e, axis=-1, keepdims=True)
            l = ss if l is None else l + ss
        if sink is not None:
            l = l + jnp.exp(sink - m)
        return e_list, l

    for bi in range(nb):
        qrows = pl.ds(bi * tq, tq)
        krows = pl.ds(bi * s_len, s_len)
        k_srcs, v_srcs, is_own = [], [], []
        if has_ctx:
            k_srcs.append([r[...].astype(BF16) for r in kc_refs])
            v_srcs.append(vc_ref[...].astype(BF16))
            is_own.append(False)
        k_srcs.append([r[krows, :].astype(BF16) for r in ko_refs])
        v_srcs.append(vo_ref[krows, :].astype(BF16))
        is_own.append(True)

        def mask_own(s):
            if has_bias:
                s = s + bias_ref[...]
            if has_band:
                qpos = pl.program_id(2) * tq + lax.broadcasted_iota(jnp.int32, (tq, 1), 0)
                kpos = lax.broadcasted_iota(jnp.int32, (1, s_len), 1)
                s = jnp.where(jnp.abs(kpos - qpos) <= D_WINDOW, s, NEG)
            return s

        if kind == "diff":
            q = q_refs[0][qrows, :]
            q1 = jnp.where(lane < A_QK_DIM, q, 0.0).astype(BF16)
            q2 = jnp.where(lane < A_QK_DIM, 0.0, q).astype(BF16)
            s1 = [_dot_nt(q1, ks[0]) * scale for ks in k_srcs]
            s2 = [_dot_nt(q2, ks[0]) * scale for ks in k_srcs]
            e1, l1 = softmax_parts(s1)
            e2, l2 = softmax_parts(s2)
            lv = lam_ref[...]
            lam = (jnp.exp(jnp.sum(lv[0:1] * lv[1:2], axis=-1, keepdims=True))
                   - jnp.exp(jnp.sum(lv[2:3] * lv[3:4], axis=-1, keepdims=True)) + lam_init)
            r1 = 1.0 / l1
            r2 = lam / l2
            o = None
            for a, b, v in zip(e1, e2, v_srcs):
                w = (a * r1 - b * r2).astype(BF16)
                pv = _dot(w, v)
                o = pv if o is None else o + pv
            o = o * lax.rsqrt(jnp.mean(o * o, axis=-1, keepdims=True) + EPS) * sg_ref[...] * (1.0 - lam_init)
        else:
            qs = [r[qrows, :].astype(BF16) for r in q_refs]
            s_list = []
            for ks, own in zip(k_srcs, is_own):
                s = None
                for qp, kp in zip(qs, ks):
                    d = _dot_nt(qp, kp)
                    s = d if s is None else s + d
                s = s * scale
                s_list.append(mask_own(s) if own else s)
            e_list, l = softmax_parts(s_list)
            o = None
            for e, v in zip(e_list, v_srcs):
                pv = _dot(e.astype(BF16), v)
                o = pv if o is None else o + pv
            o = o * (1.0 / l)
        o_ref[qrows, :] = o.astype(o_ref.dtype)


def _attention(*, kind, latent, q_parts, ko_parts, vo, kc_parts=(), vc=None, bias=None, lam=None, subln=None,
               sink=None, o_arr, o_blk, scale, lam_init=0.0, has_band=False, name):
    n_parts = len(q_parts)
    has_ctx = latent
    if latent:
        tq, s_len, nb = 512, DEC_SEQ, 1
        grid = (DEC_BATCH, N_HEADS, DEC_SEQ // tq)
        qpb = DEC_SEQ // tq
        q_row = lambda b, h, t: N_PROMPT // tq + b * qpb + t
        k_row = lambda b, h, t: N_SEG_P + b
        qblk = tq
    else:
        tq, s_len, nb = SEQ, SEQ, SEG // SEQ
        grid = (N_SEG_P, N_HEADS, 1)
        q_row = lambda b, h, t: b
        k_row = lambda b, h, t: b
        qblk = SEG
    args, specs = [], []

    def add(arr, shape, imap, **kw):
        args.append(arr)
        specs.append(pl.BlockSpec(shape, imap, **kw))

    for arr, f in q_parts:
        add(arr, (qblk, LANES), lambda b, h, t, f=f: (q_row(b, h, t), f(h)))
    for arr, f in list(ko_parts) + [vo]:
        add(arr, (SEG, LANES), lambda b, h, t, f=f: (k_row(b, h, t), f(h)))
    if has_ctx:
        for arr, f in list(kc_parts) + [vc]:
            add(arr, (None, PAST_LEN, LANES), lambda b, h, t, f=f: (b, 0, f(h)))
    if bias is not None:
        add(bias, (None, tq, DEC_SEQ), lambda b, h, t: (h, t, 0))
    if kind == "diff":
        add(lam, lam.shape, lambda b, h, t: (0, 0))
        add(subln, subln.shape, lambda b, h, t: (0, 0))
    if sink is not None:
        args.append(sink)
        specs.append(pl.BlockSpec(memory_space=pltpu.SMEM))
    n_in = len(args)
    args.append(o_arr)
    specs.append(pl.BlockSpec(memory_space=pl.ANY))
    kern = functools.partial(_attn_kernel_aliased, kind=kind, n_parts=n_parts, has_ctx=has_ctx,
                             has_bias=bias is not None, has_band=has_band, has_sink=sink is not None,
                             nb=nb, s_len=s_len, tq=tq, scale=scale, lam_init=lam_init)
    return pl.pallas_call(
        kern,
        grid=grid,
        in_specs=specs,
        out_specs=pl.BlockSpec((qblk, LANES), lambda b, h, t: (q_row(b, h, t), o_blk(h))),
        out_shape=jax.ShapeDtypeStruct(o_arr.shape, o_arr.dtype),
        input_output_aliases={n_in: 0},
        compiler_params=_cparams(("arbitrary", "arbitrary", "arbitrary"), 48),
        name=name,
    )(*args)


def _attn_kernel_aliased(*refs, **kw):
    _attn_kernel(*refs[:-2], refs[-1], **kw)


def _outproj_kernel(o_ref, w_ref, x_ref, mt_ref, y_ref):
    y_ref[...] = x_ref[...] + mt_ref[0, 2:3, :] * _dot(o_ref[...], w_ref[...])


def _out_projection(o, w, x, mt):
    return pl.pallas_call(
        _outproj_kernel,
        grid=(N_SEG, D_MODEL // TN),
        in_specs=[
            pl.BlockSpec((SEG, D_MODEL), lambda i, j: (i, 0)),
            pl.BlockSpec((D_MODEL, TN), lambda i, j: (0, j)),
            pl.BlockSpec((SEG, TN), lambda i, j: (i, j)),
            pl.BlockSpec((1, 6, TN), lambda i, j: (i, 0, j)),
        ],
        out_specs=pl.BlockSpec((SEG, TN), lambda i, j: (i, j)),
        out_shape=jax.ShapeDtypeStruct((N_TOK, D_MODEL), F32),
        compiler_params=_cparams(("arbitrary", "arbitrary"), 40),
        name="out_proj",
    )(o, w, x, mt)


def _router_kernel(x_ref, mt_ref, g_ref, wh_ref, wl_ref, rb_ref, h_ref, e_ref, gt_ref):
    x = x_ref[...]
    y = x * lax.rsqrt(jnp.mean(x * x, axis=-1, keepdims=True) + EPS) * g_ref[...]
    h = y * (1.0 + mt_ref[0, 4:5, :]) + mt_ref[0, 3:4, :]
    hh = h.astype(BF16)
    h_ref[...] = h
    hl = (h - hh.astype(F32)).astype(BF16)
    wh, wl = wh_ref[...], wl_ref[...]
    logits = _dot_nt(wh, hh) + _dot_nt(wh, hl) + _dot_nt(wl, hh)
    scores = _sigmoid(logits)
    sel = scores + rb_ref[...]
    per = N_EXPERTS // N_GROUPS
    sc = [scores[e:e + 1, :] for e in range(N_EXPERTS)]
    sl = [sel[e:e + 1, :] for e in range(N_EXPERTS)]
    best_g, best_v = None, None
    for gi in range(N_GROUPS):
        a, b, c, d = sl[gi * per:(gi + 1) * per]
        hi1, lo1, hi2, lo2 = jnp.maximum(a, b), jnp.minimum(a, b), jnp.maximum(c, d), jnp.minimum(c, d)
        gs = jnp.maximum(hi1, hi2) + jnp.maximum(jnp.minimum(hi1, hi2), jnp.maximum(lo1, lo2))
        if gi == 0:
            best_g, best_v = jnp.zeros_like(gs, dtype=jnp.int32), gs
        else:
            better = gs > best_v
            best_g = jnp.where(better, gi, best_g)
            best_v = jnp.where(better, gs, best_v)
    masked = [jnp.where(best_g == (e // per), sl[e], NEG) for e in range(N_EXPERTS)]
    i1, v1 = jnp.zeros_like(best_g), masked[0]
    for e in range(1, N_EXPERTS):
        better = masked[e] > v1
        i1 = jnp.where(better, e, i1)
        v1 = jnp.where(better, masked[e], v1)
    i2, v2 = None, None
    for e in range(N_EXPERTS):
        cand = jnp.where(i1 == e, -2e30, masked[e])
        if e == 0:
            i2, v2 = jnp.zeros_like(best_g), cand
        else:
            better = cand > v2
            i2 = jnp.where(better, e, i2)
            v2 = jnp.where(better, cand, v2)
    g1 = jnp.zeros_like(v1)
    g2 = jnp.zeros_like(v1)
    for e in range(N_EXPERTS):
        g1 = jnp.where(i1 == e, sc[e], g1)
        g2 = jnp.where(i2 == e, sc[e], g2)
    tot = g1 + g2
    rows = i1.shape[1]
    e_ref[...] = jnp.concatenate([i1, i2, jnp.zeros((6, rows), jnp.int32)], axis=0)
    gt_ref[...] = jnp.concatenate([g1 / tot, g2 / tot, jnp.zeros((6, rows), F32)], axis=0)


def _router(x, mt, g, router_w, router_b):
    wt = router_w.T
    wh = wt.astype(BF16)
    wl = (wt - wh.astype(F32)).astype(BF16)
    return pl.pallas_call(
        _router_kernel,
        grid=(N_SEG,),
        in_specs=[
            pl.BlockSpec((SEG, D_MODEL), lambda i: (i, 0)),
            pl.BlockSpec((1, 6, D_MODEL), lambda i: (i, 0, 0)),
            pl.BlockSpec((1, D_MODEL), lambda i: (0, 0)),
            pl.BlockSpec((N_EXPERTS, D_MODEL), lambda i: (0, 0)),
            pl.BlockSpec((N_EXPERTS, D_MODEL), lambda i: (0, 0)),
            pl.BlockSpec((N_EXPERTS, 1), lambda i: (0, 0)),
        ],
        out_specs=[
            pl.BlockSpec((SEG, D_MODEL), lambda i: (i, 0)),
            pl.BlockSpec((8, SEG), lambda i: (0, i)),
            pl.BlockSpec((8, SEG), lambda i: (0, i)),
        ],
        out_shape=[
            jax.ShapeDtypeStruct((N_TOK, D_MODEL), F32),
            jax.ShapeDtypeStruct((8, N_TOK), jnp.int32),
            jax.ShapeDtypeStruct((8, N_TOK), F32),
        ],
        compiler_params=_cparams(("arbitrary",), 48),
        name="ffn_norm_router",
    )(x, mt, g, wh, wl, router_b.reshape(N_EXPERTS, 1))


def _expert_changed(te_ref, i):
    return jnp.logical_or(i == 0, te_ref[i] != te_ref[jnp.maximum(i - 1, 0)])


def _moe_up_kernel(te_ref, na_ref, xs_ref, wg_ref, wu_ref, hid_ref, wgb_ref, wub_ref):
    i = pl.program_id(0)

    @pl.when(_expert_changed(te_ref, i))
    def _():
        wgb_ref[...] = wg_ref[...].astype(BF16)
        wub_ref[...] = wu_ref[...].astype(BF16)

    @pl.when(i < na_ref[0])
    def _():
        x = xs_ref[...].astype(BF16)
        g = _dot(x, wgb_ref[...])
        u = _dot(x, wub_ref[...])
        hid_ref[...] = (g * _sigmoid(g) * u).astype(BF16)

    @pl.when(i >= na_ref[0])
    def _():
        hid_ref[...] = jnp.zeros_like(hid_ref)


def _moe_down_kernel(te_ref, na_ref, hid_ref, wd_ref, gate_ref, ys_ref, wdb_ref):
    i = pl.program_id(0)

    @pl.when(_expert_changed(te_ref, i))
    def _():
        wdb_ref[...] = wd_ref[...].astype(BF16)

    @pl.when(i < na_ref[0])
    def _():
        ys_ref[...] = _dot(hid_ref[...], wdb_ref[...]) * gate_ref[...]

    @pl.when(i >= na_ref[0])
    def _():
        ys_ref[...] = jnp.zeros_like(ys_ref)


def _moe_experts(xs, gate_rows, tile_expert, n_active, w_gate, w_up, w_down):
    hid = pl.pallas_call(
        _moe_up_kernel,
        grid_spec=pltpu.PrefetchScalarGridSpec(
            num_scalar_prefetch=2,
            grid=(MOE_TILES,),
            in_specs=[
                pl.BlockSpec((MOE_TM, D_MODEL), lambda i, te, na: (i, 0)),
                pl.BlockSpec((None, D_MODEL, D_EXPERT), lambda i, te, na: (te[i], 0, 0)),
                pl.BlockSpec((None, D_MODEL, D_EXPERT), lambda i, te, na: (te[i], 0, 0)),
            ],
            out_specs=pl.BlockSpec((MOE_TM, D_EXPERT), lambda i, te, na: (i, 0)),
            scratch_shapes=[pltpu.VMEM((D_MODEL, D_EXPERT), BF16), pltpu.VMEM((D_MODEL, D_EXPERT), BF16)],
        ),
        out_shape=jax.ShapeDtypeStruct((MOE_ROWS, D_EXPERT), BF16),
        compiler_params=_cparams(("arbitrary",), 52),
        name="moe_up",
    )(tile_expert, n_active, xs, w_gate, w_up)
    return pl.pallas_call(
        _moe_down_kernel,
        grid_spec=pltpu.PrefetchScalarGridSpec(
            num_scalar_prefetch=2,
            grid=(MOE_TILES,),
            in_specs=[
                pl.BlockSpec((MOE_TM, D_EXPERT), lambda i, te, na: (i, 0)),
                pl.BlockSpec((None, D_EXPERT, D_MODEL), lambda i, te, na: (te[i], 0, 0)),
                pl.BlockSpec((MOE_TM, 1), lambda i, te, na: (i, 0)),
            ],
            out_specs=pl.BlockSpec((MOE_TM, D_MODEL), lambda i, te, na: (i, 0)),
            scratch_shapes=[pltpu.VMEM((D_EXPERT, D_MODEL), BF16)],
        ),
        out_shape=jax.ShapeDtypeStruct((MOE_ROWS, D_MODEL), F32),
        compiler_params=_cparams(("arbitrary",), 40),
        name="moe_down",
    )(tile_expert, n_active, hid, w_down, gate_rows)


def _route_plan(eidx, gates):
    flat_e = eidx[:2].reshape(-1)
    flat_g = gates[:2].reshape(-1)
    n_asg = flat_e.shape[0]
    cnt = jnp.bincount(flat_e, length=N_EXPERTS).astype(jnp.int32)
    tiles = (cnt + MOE_TM - 1) // MOE_TM
    tile_end = jnp.cumsum(tiles)
    row_off = (tile_end - tiles) * MOE_TM
    start = jnp.cumsum(cnt) - cnt
    order = jnp.argsort(flat_e, stable=True).astype(jnp.int32)
    n_active = tile_end[-1]
    tile_ids = jnp.arange(MOE_TILES, dtype=jnp.int32)
    te = jnp.searchsorted(tile_end, jnp.minimum(tile_ids, n_active - 1), side="right").astype(jnp.int32)
    row_e = jnp.repeat(te, MOE_TM)
    rows = jnp.arange(MOE_ROWS, dtype=jnp.int32)
    rank = rows - row_off[row_e]
    valid = jnp.logical_and(rank < cnt[row_e], rows < n_active * MOE_TM)
    src_asg = order[jnp.clip(start[row_e] + rank, 0, n_asg - 1)]
    src_tok = jnp.where(valid, src_asg % N_TOK, 0)
    gate_rows = jnp.where(valid, flat_g[src_asg], 0.0)
    sorted_e = flat_e[order]
    dest_sorted = row_off[sorted_e] + jnp.arange(n_asg, dtype=jnp.int32) - start[sorted_e]
    pos = jnp.zeros((n_asg,), jnp.int32).at[order].set(dest_sorted)
    return src_tok, gate_rows.reshape(MOE_ROWS, 1), te, n_active.reshape(1).astype(jnp.int32), pos


def _combine_kernel(x_ref, y0_ref, y1_ref, mt_ref, fg_ref, o_ref, *, final):
    x = x_ref[...] + mt_ref[0, 5:6, :] * (y0_ref[...] + y1_ref[...])
    if final:
        x = x * lax.rsqrt(jnp.mean(x * x, axis=-1, keepdims=True) + EPS) * fg_ref[...]
    o_ref[...] = x


def _combine(x, ysel, mt, final_g, final):
    tm = 256
    return pl.pallas_call(
        functools.partial(_combine_kernel, final=final),
        grid=(N_TOK // tm,),
        in_specs=[
            pl.BlockSpec((tm, D_MODEL), lambda i: (i, 0)),
            pl.BlockSpec((None, tm, D_MODEL), lambda i: (0, i, 0)),
            pl.BlockSpec((None, tm, D_MODEL), lambda i: (1, i, 0)),
            pl.BlockSpec((1, 6, D_MODEL), lambda i: (i // (SEG // tm), 0, 0)),
            pl.BlockSpec((1, D_MODEL), lambda i: (0, 0)),
        ],
        out_specs=pl.BlockSpec((tm, D_MODEL), lambda i: (i, 0)),
        out_shape=jax.ShapeDtypeStruct((N_TOK, D_MODEL), F32),
        compiler_params=_cparams(("arbitrary",), 40),
        name="moe_combine",
    )(x, ysel, ysel, mt, final_g)


def _rope_tables(dim):
    half = dim // 2
    inv = ROPE_THETA ** (-jnp.arange(0, half, 2, dtype=F32) / half)
    t = jnp.arange(DEC_SEQ)
    ang_r = (t // GRID_W).astype(F32)[:, None] * inv[None, :]
    ang_c = (t % GRID_W).astype(F32)[:, None] * inv[None, :]
    ang = jnp.concatenate([ang_r, ang_r, ang_c, ang_c], axis=-1)
    cos, sin = jnp.cos(ang), jnp.sin(ang)
    reps = LANES // dim
    cos = jnp.tile(cos, (1, reps))
    sin = jnp.tile(sin, (1, reps))
    sh = dim // 4
    second = (np.arange(LANES) % (2 * sh)) >= sh
    sp = jnp.where(second[None, :], sin, 0.0)
    sm = jnp.where(second[None, :], 0.0, -sin)
    return cos, sp, sm


def _neighbourhood_bias(rpb):
    rows = DEC_SEQ // GRID_W
    kh = min(NA_ROWS, rows)
    r = np.arange(rows)
    r0 = np.clip(r - kh // 2, 0, rows - kh)
    kr = np.arange(rows)
    row_ok = (kr[None, :] >= r0[:, None]) & (kr[None, :] < r0[:, None] + kh)
    dr_idx = np.clip(kr[None, :] - r[:, None] + NA_ROWS - 1, 0, 2 * NA_ROWS - 2)
    c = np.arange(GRID_W)
    ws = np.clip(c - NA_COLS // 2, 0, GRID_W - NA_COLS)
    kc = np.arange(GRID_W)
    col_ok = (kc[None, :] >= ws[:, None]) & (kc[None, :] < ws[:, None] + NA_COLS)
    dc_idx = np.clip(kc[None, :] - c[:, None], -(NA_COLS - 1), NA_COLS - 1) + NA_COLS - 1
    sel_r = ((np.arange(2 * NA_ROWS - 1)[:, None, None] == dr_idx[None]) & row_ok[None]).astype(np.float32)
    sel_c = ((np.arange(2 * NA_COLS - 1)[:, None, None] == dc_idx[None]) & col_ok[None]).astype(np.float32)
    t = jnp.einsum("hab,ark->hrkb", rpb.astype(F32), sel_r, precision=lax.Precision.HIGHEST)
    b = jnp.einsum("hrkb,bcx->hrckx", t, sel_c, precision=lax.Precision.HIGHEST)
    ok = row_ok[:, None, :, None] & col_ok[None, :, None, :]
    return jnp.where(ok[None], b, NEG).reshape(N_HEADS, DEC_SEQ, DEC_SEQ)


def _even_layer(x, mt, l, i, rope64, cache_a_k, cache_a_v, cache_b_ckv, cache_b_krope, norm_mix_g, ev_w_in,
                ev_lambda, ev_subln_g, ev_q_norm_g, ev_kv_norm_g, ev_w_uq, ev_w_ukv, ev_w_out):
    w = ev_w_in[i]
    kr = w[:, 4352:4416]
    z = jnp.zeros_like(kr)
    w_in = jnp.concatenate([w[:, :3840], kr, z, z, kr, w[:, 3840:4352]], axis=1).astype(BF16)
    proj = _projection(x, D_MODEL, 0, D_MODEL, w_in, prologue="modulate", mt=mt, g=norm_mix_g[l][None, :],
                       rope=rope64, sh=A_QK_DIM // 4, rope_full=(0, 1, 2, 3), rope_part=(7, (2, 4)),
                       norm_tile=8, norm_g=ev_kv_norm_g[i][None, :], name="even_in_proj")
    wq = ev_w_uq[i].reshape(B_Q_RANK, N_HEADS, HEAD_DIM + B_ROPE_DIM)
    w_uq = jnp.concatenate([wq[:, :, :HEAD_DIM].reshape(B_Q_RANK, -1), wq[:, :, HEAD_DIM:].reshape(B_Q_RANK, -1)],
                           axis=1).astype(BF16)
    b_q = _projection(proj, 1024, 3, B_Q_RANK, w_uq, prologue="rmsnorm", g=ev_q_norm_g[i][None, :], rope=rope64,
                      sh=B_ROPE_DIM // 4, rope_full=(2,), out_dtype=BF16, name="mla_q_up")
    w_ukv = ev_w_ukv[i].astype(BF16)
    kv = _projection(proj, 512, 8, B_KV_RANK, w_ukv, prologue="cast", out_dtype=BF16, name="mla_kv_up")
    kv_ctx = _projection(cache_b_ckv[:, i].reshape(DEC_BATCH * PAST_LEN, B_KV_RANK), 512, 0, B_KV_RANK, w_ukv,
                         prologue="cast", out_dtype=BF16, name="mla_kv_up_ctx")
    kv_ctx = kv_ctx.reshape(DEC_BATCH, PAST_LEN, 2 * N_HEADS * HEAD_DIM)
    krc = cache_b_krope[:, i]
    zc = jnp.zeros_like(krc)
    kr_ctx = jnp.concatenate([krc, zc, zc, krc], axis=-1)
    ak_ctx = cache_a_k[:, i].reshape(DEC_BATCH, PAST_LEN, N_HEADS * HEAD_DIM)
    av_ctx = cache_a_v[:, i].reshape(DEC_BATCH, PAST_LEN, N_HEADS * HEAD_DIM)
    lam_init = 0.8 - 0.6 * math.exp(-0.3 * l)
    o = jnp.zeros((N_TOK, D_MODEL), BF16)
    for latent in (False, True):
        tag = "lat" if latent else "ctx"
        o = _attention(kind="diff", latent=latent, q_parts=[(proj, lambda h: h)], ko_parts=[(proj, lambda h: 8 + h)],
                       vo=(proj, lambda h: 16 + h), kc_parts=[(ak_ctx, lambda h: h)], vc=(av_ctx, lambda h: h),
                       lam=ev_lambda[i], subln=ev_subln_g[i][None, :], o_arr=o, o_blk=lambda h: h,
                       scale=A_QK_DIM ** -0.5, lam_init=lam_init, name="diff_attn_" + tag)
        o = _attention(kind="mla", latent=latent, q_parts=[(b_q, lambda h: h), (b_q, lambda h: 8 + h // 2)],
                       ko_parts=[(kv, lambda h: 2 * h), (proj, lambda h: 30 + h % 2)], vo=(kv, lambda h: 2 * h + 1),
                       kc_parts=[(kv_ctx, lambda h: 2 * h), (kr_ctx, lambda h: h % 2)], vc=(kv_ctx, lambda h: 2 * h + 1),
                       o_arr=o, o_blk=lambda h: 8 + h, scale=(HEAD_DIM + B_ROPE_DIM) ** -0.5, name="mla_attn_" + tag)
    y = _out_projection(o, ev_w_out[i].astype(BF16), x, mt)
    p = proj[:N_PROMPT]
    states = (p[:, 1024:2048].reshape(BATCH, SEQ, N_HEADS, HEAD_DIM),
              p[:, 2048:3072].reshape(BATCH, SEQ, N_HEADS, HEAD_DIM),
              p[:, 4096:4608].reshape(BATCH, SEQ, B_KV_RANK),
              p[:, 3840:3904].reshape(BATCH, SEQ, B_ROPE_DIM))
    return y, states


def _odd_layer(x, mt, l, i, rope128, cache_c_k, cache_c_v, cache_d_k, cache_d_v, norm_mix_g, od_w_in, od_rpb,
               od_sink, od_w_out):
    proj = _projection(x, D_MODEL, 0, D_MODEL, od_w_in[i].astype(BF16), prologue="modulate", mt=mt,
                       g=norm_mix_g[l][None, :], rope=rope128, sh=HEAD_DIM // 4, rope_full=(6, 7),
                       rope_part=(8, (0, 2)), name="odd_in_proj")
    ck_ctx = cache_c_k[:, i].reshape(DEC_BATCH, PAST_LEN, N_HEADS * HEAD_DIM)
    cv_ctx = cache_c_v[:, i].reshape(DEC_BATCH, PAST_LEN, N_HEADS * HEAD_DIM)
    dk_ctx = cache_d_k[:, i].reshape(DEC_BATCH, PAST_LEN, D_KV_HEADS * HEAD_DIM)
    dv_ctx = cache_d_v[:, i].reshape(DEC_BATCH, PAST_LEN, D_KV_HEADS * HEAD_DIM)
    bias = _neighbourhood_bias(od_rpb[i])
    sink = od_sink[i].astype(F32)
    o = jnp.zeros((N_TOK, D_MODEL), BF16)
    for latent in (False, True):
        tag = "lat" if latent else "ctx"
        o = _attention(kind="plain", latent=latent, q_parts=[(proj, lambda h: h)], ko_parts=[(proj, lambda h: 8 + h)],
                       vo=(proj, lambda h: 16 + h), kc_parts=[(ck_ctx, lambda h: h)], vc=(cv_ctx, lambda h: h),
                       bias=bias if latent else None, o_arr=o, o_blk=lambda h: h, scale=HEAD_DIM ** -0.5,
                       name="nbr_attn_" + tag)
        o = _attention(kind="plain", latent=latent, q_parts=[(proj, lambda h: 24 + h)],
                       ko_parts=[(proj, lambda h: 32 + h // D_GROUP)], vo=(proj, lambda h: 34 + h // D_GROUP),
                       kc_parts=[(dk_ctx, lambda h: h // D_GROUP)], vc=(dv_ctx, lambda h: h // D_GROUP), sink=sink,
                       o_arr=o, o_blk=lambda h: 8 + h, scale=HEAD_DIM ** -0.5, has_band=latent,
                       name="win_attn_" + tag)
    y = _out_projection(o, od_w_out[i].astype(BF16), x, mt)
    p = proj[:N_PROMPT]
    states = (p[:, 1024:2048].reshape(BATCH, SEQ, N_HEADS, HEAD_DIM),
              p[:, 2048:3072].reshape(BATCH, SEQ, N_HEADS, HEAD_DIM),
              p[:, 4096:4352].reshape(BATCH, SEQ, D_KV_HEADS, HEAD_DIM),
              p[:, 4352:4608].reshape(BATCH, SEQ, D_KV_HEADS, HEAD_DIM))
    return y, states


def _moe_layer(x, mt, g, router_w, router_b, w_gate, w_up, w_down, final_g, final):
    h2, eidx, gates = _router(x, mt, g, router_w, router_b)
    src_tok, gate_rows, te, n_active, pos = _route_plan(eidx, gates)
    xs = h2.at[src_tok].get(mode="promise_in_bounds")
    ys = _moe_experts(xs, gate_rows, te, n_active, w_gate, w_up, w_down)
    ysel = ys.at[pos].get(mode="promise_in_bounds").reshape(2, N_TOK, D_MODEL)
    return _combine(x, ysel, mt, final_g, final)


def kernel(x_prompt, x_sample, cache_a_k, cache_a_v, cache_b_ckv, cache_b_krope, cache_c_k, cache_c_v, cache_d_k, cache_d_v, c, c_ctx, w_ada, b_ada, norm_mix_g, norm_ffn_g, ev_w_in, ev_lambda, ev_subln_g, ev_q_norm_g, ev_kv_norm_g, ev_w_uq, ev_w_ukv, ev_w_out, od_w_in, od_rpb, od_sink, od_w_out, router_w, router_b, moe_w_gate, moe_w_up, moe_w_down, final_g):
    x = jnp.concatenate([x_prompt.reshape(N_PROMPT, D_MODEL), x_sample.reshape(DEC_BATCH * DEC_SEQ, D_MODEL)], axis=0)
    cond8 = jnp.concatenate([c_ctx[None, :], c, jnp.zeros((3, D_MODEL), F32)], axis=0)
    mod = _adaln(cond8, w_ada, b_ada)
    seg_row = np.array([0] * N_SEG_P + [1 + b for b in range(DEC_BATCH)])
    mt_all = mod[:, seg_row].reshape(DEPTH, N_SEG, 6, D_MODEL)
    rope64 = _rope_tables(A_QK_DIM)
    rope128 = _rope_tables(HEAD_DIM)
    even_states, odd_states = [], []
    for l in range(DEPTH):
        i = l // 2
        mt = mt_all[l]
        if l % 2 == 0:
            x, st = _even_layer(x, mt, l, i, rope64, cache_a_k, cache_a_v, cache_b_ckv, cache_b_krope, norm_mix_g,
                                ev_w_in, ev_lambda, ev_subln_g, ev_q_norm_g, ev_kv_norm_g, ev_w_uq, ev_w_ukv, ev_w_out)
            even_states.append(st)
        else:
            x, st = _odd_layer(x, mt, l, i, rope128, cache_c_k, cache_c_v, cache_d_k, cache_d_v, norm_mix_g,
                               od_w_in, od_rpb, od_sink, od_w_out)
            odd_states.append(st)
        x = _moe_layer(x, mt, norm_ffn_g[l][None, :], router_w, router_b, moe_w_gate[l], moe_w_up[l], moe_w_down[l],
                       final_g[None, :], final=(l == DEPTH - 1))
    y_prompt = x[:N_PROMPT].reshape(BATCH, SEQ, D_MODEL)
    y_sample = x[N_PROMPT:].reshape(DEC_BATCH, DEC_SEQ, D_MODEL)
    new_even = tuple(jnp.stack([st[k] for st in even_states], axis=1) for k in range(4))
    new_odd = tuple(jnp.stack([st[k] for st in odd_states], axis=1) for k in range(4))
    return (y_prompt, y_sample) + new_even + new_odd
```

```python
import functools
import math

import numpy as np
import jax
import jax.numpy as jnp
from jax import lax
from jax.experimental import pallas as pl
from jax.experimental.pallas import tpu as pltpu

D_MODEL = 2048
BATCH = 32
SEQ = 256
DEPTH = 2
DEC_BATCH = 4
DEC_SEQ = 1024
PAST_LEN = 256
GRID_W = 64
HEAD_DIM = 128
N_HEADS = 8
A_QK_DIM = 64
B_Q_RANK = 768
B_KV_RANK = 512
B_ROPE_DIM = 64
NA_ROWS = 8
NA_COLS = 16
D_KV_HEADS = 2
D_GROUP = 4
D_WINDOW = 128
N_EXPERTS = 16
N_GROUPS = 4
D_EXPERT = 1024
ROPE_THETA = 10000.0
EPS = 1e-6
NEG = -1e30

SEG = 1024
N_PROMPT = BATCH * SEQ
N_TOK = N_PROMPT + DEC_BATCH * DEC_SEQ
N_SEG = N_TOK // SEG
N_SEG_P = N_PROMPT // SEG
PROJ_W = 4608
TN = 512
LANES = 128
MOE_TM = 256
MOE_ROWS = 2 * N_TOK + N_EXPERTS * MOE_TM
MOE_TILES = MOE_ROWS // MOE_TM

F32 = jnp.float32
BF16 = jnp.bfloat16


def _cparams(sem, vmem_mb):
    return pltpu.CompilerParams(dimension_semantics=sem, vmem_limit_bytes=vmem_mb * 1024 * 1024)


def _dot(a, b):
    return jnp.dot(a, b, preferred_element_type=F32)


def _dot_nt(a, b):
    return lax.dot_general(a, b, (((1,), (1,)), ((), ())), preferred_element_type=F32)


def _sigmoid(x):
    return 1.0 / (1.0 + jnp.exp(-x))


def _adaln_kernel(c_ref, w_ref, b_ref, o_ref):
    c = c_ref[...]
    a = (c * _sigmoid(c)).astype(BF16)
    o_ref[...] = _dot(a, w_ref[...].astype(BF16)) + b_ref[...]


def _adaln(cond8, w_ada, b_ada):
    tn = 1024
    n = 6 * D_MODEL
    return pl.pallas_call(
        _adaln_kernel,
        grid=(DEPTH, n // tn),
        in_specs=[
            pl.BlockSpec((8, D_MODEL), lambda l, j: (0, 0)),
            pl.BlockSpec((None, D_MODEL, tn), lambda l, j: (l, 0, j)),
            pl.BlockSpec((None, 1, tn), lambda l, j: (l, 0, j)),
        ],
        out_specs=pl.BlockSpec((None, 8, tn), lambda l, j: (l, 0, j)),
        out_shape=jax.ShapeDtypeStruct((DEPTH, 8, n), F32),
        compiler_params=_cparams(("arbitrary", "arbitrary"), 40),
        name="adaln",
    )(cond8, w_ada, b_ada.reshape(DEPTH, 1, n))


def _rope(a, cos, sp, sm, sh):
    return a * cos + pltpu.roll(a, sh, 1) * sp + pltpu.roll(a, LANES - sh, 1) * sm


def _proj_kernel(*refs, prologue, k_in, sh, n_tiles, rope_groups, norm_tile, states):
    x_ref, mt_ref, g_ref, w_ref, cos_ref, sp_ref, sm_ref, ng_ref, o_ref = refs[:9]
    st_refs = refs[9:-1]
    xn_ref = refs[-1]
    j = pl.program_id(1)

    @pl.when(j == 0)
    def _():
        x = x_ref[:, :k_in]
        if prologue == "cast":
            xn_ref[...] = x.astype(BF16)
        else:
            y = x * lax.rsqrt(jnp.mean(x * x, axis=-1, keepdims=True) + EPS) * g_ref[...]
            if prologue == "modulate":
                y = y * (1.0 + mt_ref[0, 1:2, :]) + mt_ref[0, 0:1, :]
            xn_ref[...] = y.astype(BF16)

    acc = _dot(xn_ref[...], w_ref[...])
    n_grp = acc.shape[1] // LANES

    def treatment(t):
        acts = tuple((k, a, b) for k, st in enumerate(states) for (tt, a, b) in st[1] if tt == t)
        return (tuple(rope_groups.get(t, ())), t == norm_tile, acts)

    branches = {}
    for t in range(n_tiles):
        branches.setdefault(treatment(t), []).append(t)

    for (rg, is_norm, acts), tiles in branches.items():
        cond = j == tiles[0]
        for t in tiles[1:]:
            cond = jnp.logical_or(cond, j == t)

        @pl.when(cond)
        def _(rg=rg, is_norm=is_norm, acts=acts):
            if is_norm:
                y = acc * lax.rsqrt(jnp.mean(acc * acc, axis=-1, keepdims=True) + EPS) * ng_ref[...]
                vals = [y[:, g * LANES:(g + 1) * LANES] for g in range(n_grp)]
            else:
                vals = [acc[:, g * LANES:(g + 1) * LANES] for g in range(n_grp)]
                if rg:
                    cos, sp, sm = cos_ref[...], sp_ref[...], sm_ref[...]
                    vals = [_rope(v, cos, sp, sm, sh) if g in rg else v for g, v in enumerate(vals)]
            for g, v in enumerate(vals):
                o_ref[:, g * LANES:(g + 1) * LANES] = v.astype(o_ref.dtype)
            for k, a, b in acts:
                n_heads = states[k][0]
                if n_heads:
                    st_refs[k][pl.ds(b, SEG, stride=n_heads), :] = vals[a]
                elif b % LANES == 0:
                    for q in range(b // LANES):
                        st_refs[k][:, q * LANES:(q + 1) * LANES] = vals[a // LANES + q]
                else:
                    st_refs[k][...] = vals[a // LANES][:, a % LANES:a % LANES + b]


def _projection(x, row_off, n_seg, x_blk_w, x_blk_idx, k_in, w, *, prologue, mt=None, g=None, rope=None, sh=16,
                rope_groups=None, norm_tile=None, norm_g=None, states=(), out_dtype=F32, name):
    n = w.shape[1]
    m = n_seg * SEG
    assert n % TN == 0 and w.shape[0] == k_in
    if mt is None:
        mt = jnp.zeros((n_seg, 6, LANES), F32)
    if g is None:
        g = jnp.ones((1, k_in), F32)
    if rope is None:
        rope = tuple(jnp.zeros((8, LANES), F32) for _ in range(3))
    if norm_g is None:
        norm_g = jnp.ones((1, TN), F32)
    kern = functools.partial(_proj_kernel, prologue=prologue, k_in=k_in, sh=sh, n_tiles=n // TN,
                             rope_groups=dict(rope_groups or {}), norm_tile=norm_tile, states=tuple(states))
    full2 = lambda i, j: (0, 0)
    out_shape = [jax.ShapeDtypeStruct((m, n), out_dtype)]
    out_specs = [pl.BlockSpec((SEG, TN), lambda i, j: (i, j))]
    for n_heads, src in states:
        if n_heads:
            out_shape.append(jax.ShapeDtypeStruct((m * n_heads, LANES), F32))
            out_specs.append(pl.BlockSpec((SEG * n_heads, LANES), lambda i, j: (i, 0)))
        else:
            width = src[0][2]
            out_shape.append(jax.ShapeDtypeStruct((m, width), F32))
            out_specs.append(pl.BlockSpec((SEG, width), lambda i, j: (i, 0)))
    x_kw = dict(pipeline_mode=pl.Buffered(1)) if states else {}
    res = pl.pallas_call(
        kern,
        grid=(n_seg, n // TN),
        in_specs=[
            pl.BlockSpec((SEG, x_blk_w), lambda i, j: (i + row_off, x_blk_idx), **x_kw),
            pl.BlockSpec((1, 6, mt.shape[2]), lambda i, j: (i, 0, 0)),
            pl.BlockSpec((1, k_in), full2),
            pl.BlockSpec((k_in, TN), lambda i, j: (0, j)),
            pl.BlockSpec(rope[0].shape, full2),
            pl.BlockSpec(rope[1].shape, full2),
            pl.BlockSpec(rope[2].shape, full2),
            pl.BlockSpec((1, TN), full2),
        ],
        out_specs=out_specs,
        out_shape=out_shape,
        scratch_shapes=[pltpu.VMEM((SEG, k_in), BF16)],
        compiler_params=_cparams(("arbitrary", "arbitrary"), 56 if states else 48),
        name=name,
    )(x, mt, g, w, *rope, norm_g)
    return res if states else res[0]


def _attn_kernel(*refs, kind, n_parts, has_ctx, has_bias, has_band, has_sink, nb, s_len, tq, scale, lam_init):
    it = iter(refs)
    q_refs = [next(it) for _ in range(n_parts)]
    ko_refs = [next(it) for _ in range(n_parts)]
    vo_ref = next(it)
    kc_refs, vc_ref = [], None
    if has_ctx:
        kc_refs = [next(it) for _ in range(n_parts)]
        vc_ref = next(it)
    bias_ref = next(it) if has_bias else None
    lam_ref = sg_ref = None
    if kind == "diff":
        lam_ref = next(it)
        sg_ref = next(it)
    sink_ref = next(it) if has_sink else None
    o_ref = next(it)

    sink = sink_ref[pl.program_id(1)] if has_sink else None
    lane = lax.broadcasted_iota(jnp.int32, (1, LANES), 1)

    def softmax_parts(s_list):
        m = None
        for s in s_list:
            mm = jnp.max(s, axis=-1, keepdims=True)
            m = mm if m is None else jnp.maximum(m, mm)
        if sink is not None:
            m = jnp.maximum(m, sink)
        e_list = [jnp.exp(s - m) for s in s_list]
        l = None
        for e in e_list:
            ss = jnp.sum(e, axis=-1, keepdims=True)
            l = ss if l is None else l + ss
        if sink is not None:
            l = l + jnp.exp(sink - m)
        return e_list, l

    for bi in range(nb):
        qrows = pl.ds(bi * tq, tq)
        krows = pl.ds(bi * s_len, s_len)
        k_srcs, v_srcs, is_own = [], [], []
        if has_ctx:
            k_srcs.append([r[...].astype(BF16) for r in kc_refs])
            v_srcs.append(vc_ref[...].astype(BF16))
            is_own.append(False)
        k_srcs.append([r[krows, :].astype(BF16) for r in ko_refs])
        v_srcs.append(vo_ref[krows, :].astype(BF16))
        is_own.append(True)

        def mask_own(s):
            if has_bias:
                s = s + bias_ref[...]
            if has_band:
                qpos = pl.program_id(2) * tq + lax.broadcasted_iota(jnp.int32, (tq, 1), 0)
                kpos = lax.broadcasted_iota(jnp.int32, (1, s_len), 1)
                s = jnp.where(jnp.abs(kpos - qpos) <= D_WINDOW, s, NEG)
            return s

        if kind == "diff":
            q = q_refs[0][qrows, :]
            q1 = jnp.where(lane < A_QK_DIM, q, 0.0).astype(BF16)
            q2 = jnp.where(lane < A_QK_DIM, 0.0, q).astype(BF16)
            s1 = [_dot_nt(q1, ks[0]) * scale for ks in k_srcs]
            s2 = [_dot_nt(q2, ks[0]) * scale for ks in k_srcs]
            e1, l1 = softmax_parts(s1)
            e2, l2 = softmax_parts(s2)
            lv = lam_ref[...]
            lam = (jnp.exp(jnp.sum(lv[0:1] * lv[1:2], axis=-1, keepdims=True))
                   - jnp.exp(jnp.sum(lv[2:3] * lv[3:4], axis=-1, keepdims=True)) + lam_init)
            r1 = 1.0 / l1
            r2 = lam / l2
            o = None
            for a, b, v in zip(e1, e2, v_srcs):
                w = (a * r1 - b * r2).astype(BF16)
                pv = _dot(w, v)
                o = pv if o is None else o + pv
            o = o * lax.rsqrt(jnp.mean(o * o, axis=-1, keepdims=True) + EPS) * sg_ref[...] * (1.0 - lam_init)
        else:
            qs = [r[qrows, :].astype(BF16) for r in q_refs]
            s_list = []
            for ks, own in zip(k_srcs, is_own):
                s = None
                for qp, kp in zip(qs, ks):
                    d = _dot_nt(qp, kp)
                    s = d if s is None else s + d
                s = s * scale
                s_list.append(mask_own(s) if own else s)
            e_list, l = softmax_parts(s_list)
            o = None
            for e, v in zip(e_list, v_srcs):
                pv = _dot(e.astype(BF16), v)
                o = pv if o is None else o + pv
            o = o * (1.0 / l)
        o_ref[qrows, :] = o.astype(o_ref.dtype)


def _attention(*, kind, latent, q_parts, ko_parts, vo, kc_parts=(), vc=None, bias=None, lam=None, subln=None,
               sink=None, o_arr, o_blk, scale, lam_init=0.0, has_band=False, name):
    n_parts = len(q_parts)
    has_ctx = latent
    if latent:
        tq, s_len, nb = 512, DEC_SEQ, 1
        grid = (DEC_BATCH, N_HEADS, DEC_SEQ // tq)
        qpb = DEC_SEQ // tq
        q_row = lambda b, h, t: b * qpb + t
        o_row = lambda b, h, t: N_PROMPT // tq + b * qpb + t
        k_row = lambda b, h, t: b
        qblk = tq
    else:
        tq, s_len, nb = SEQ, SEQ, SEG // SEQ
        grid = (N_SEG_P, N_HEADS, 1)
        q_row = o_row = k_row = lambda b, h, t: b
        qblk = SEG
    args, specs = [], []

    def add(arr, shape, imap, **kw):
        args.append(arr)
        specs.append(pl.BlockSpec(shape, imap, **kw))

    for arr, f in q_parts:
        add(arr, (qblk, LANES), lambda b, h, t, f=f: (q_row(b, h, t), f(h)))
    for arr, f in list(ko_parts) + [vo]:
        add(arr, (SEG, LANES), lambda b, h, t, f=f: (k_row(b, h, t), f(h)))
    if has_ctx:
        for arr, f in list(kc_parts) + [vc]:
            add(arr, (None, PAST_LEN, LANES), lambda b, h, t, f=f: (b, 0, f(h)))
    if bias is not None:
        add(bias, (None, tq, DEC_SEQ), lambda b, h, t: (h, t, 0))
    if kind == "diff":
        add(lam, lam.shape, lambda b, h, t: (0, 0))
        add(subln, subln.shape, lambda b, h, t: (0, 0))
    if sink is not None:
        args.append(sink)
        specs.append(pl.BlockSpec(memory_space=pltpu.SMEM))
    n_in = len(args)
    args.append(o_arr)
    specs.append(pl.BlockSpec(memory_space=pl.ANY))
    kern = functools.partial(_attn_kernel_aliased, kind=kind, n_parts=n_parts, has_ctx=has_ctx,
                             has_bias=bias is not None, has_band=has_band, has_sink=sink is not None,
                             nb=nb, s_len=s_len, tq=tq, scale=scale, lam_init=lam_init)
    return pl.pallas_call(
        kern,
        grid=grid,
        in_specs=specs,
        out_specs=pl.BlockSpec((qblk, LANES), lambda b, h, t: (o_row(b, h, t), o_blk(h))),
        out_shape=jax.ShapeDtypeStruct(o_arr.shape, o_arr.dtype),
        input_output_aliases={n_in: 0},
        compiler_params=_cparams(("arbitrary", "arbitrary", "arbitrary"), 48),
        name=name,
    )(*args)


def _attn_kernel_aliased(*refs, **kw):
    _attn_kernel(*refs[:-2], refs[-1], **kw)


def _outproj_kernel(o_ref, w_ref, x_ref, mt_ref, y_ref):
    y_ref[...] = x_ref[...] + mt_ref[0, 2:3, :] * _dot(o_ref[...], w_ref[...])


def _out_projection(o, w, x, mt):
    return pl.pallas_call(
        _outproj_kernel,
        grid=(N_SEG, D_MODEL // TN),
        in_specs=[
            pl.BlockSpec((SEG, D_MODEL), lambda i, j: (i, 0)),
            pl.BlockSpec((D_MODEL, TN), lambda i, j: (0, j)),
            pl.BlockSpec((SEG, TN), lambda i, j: (i, j)),
            pl.BlockSpec((1, 6, TN), lambda i, j: (i, 0, j)),
        ],
        out_specs=pl.BlockSpec((SEG, TN), lambda i, j: (i, j)),
        out_shape=jax.ShapeDtypeStruct((N_TOK, D_MODEL), F32),
        compiler_params=_cparams(("arbitrary", "arbitrary"), 40),
        name="out_proj",
    )(o, w, x, mt)


def _router_kernel(x_ref, mt_ref, g_ref, wh_ref, wl_ref, rb_ref, h_ref, e_ref, gt_ref):
    x = x_ref[...]
    y = x * lax.rsqrt(jnp.mean(x * x, axis=-1, keepdims=True) + EPS) * g_ref[...]
    h = y * (1.0 + mt_ref[0, 4:5, :]) + mt_ref[0, 3:4, :]
    hh = h.astype(BF16)
    h_ref[...] = h
    hl = (h - hh.astype(F32)).astype(BF16)
    wh, wl = wh_ref[...], wl_ref[...]
    logits = _dot_nt(wh, hh) + _dot_nt(wh, hl) + _dot_nt(wl, hh)
    scores = _sigmoid(logits)
    sel = scores + rb_ref[...]
    per = N_EXPERTS // N_GROUPS
    sc = [scores[e:e + 1, :] for e in range(N_EXPERTS)]
    sl = [sel[e:e + 1, :] for e in range(N_EXPERTS)]
    best_g, best_v = None, None
    for gi in range(N_GROUPS):
        a, b, c, d = sl[gi * per:(gi + 1) * per]
        hi1, lo1, hi2, lo2 = jnp.maximum(a, b), jnp.minimum(a, b), jnp.maximum(c, d), jnp.minimum(c, d)
        gs = jnp.maximum(hi1, hi2) + jnp.maximum(jnp.minimum(hi1, hi2), jnp.maximum(lo1, lo2))
        if gi == 0:
            best_g, best_v = jnp.zeros_like(gs, dtype=jnp.int32), gs
        else:
            better = gs > best_v
            best_g = jnp.where(better, gi, best_g)
            best_v = jnp.where(better, gs, best_v)
    masked = [jnp.where(best_g == (e // per), sl[e], NEG) for e in range(N_EXPERTS)]
    i1, v1 = jnp.zeros_like(best_g), masked[0]
    for e in range(1, N_EXPERTS):
        better = masked[e] > v1
        i1 = jnp.where(better, e, i1)
        v1 = jnp.where(better, masked[e], v1)
    i2, v2 = None, None
    for e in range(N_EXPERTS):
        cand = jnp.where(i1 == e, -2e30, masked[e])
        if e == 0:
            i2, v2 = jnp.zeros_like(best_g), cand
        else:
            better = cand > v2
            i2 = jnp.where(better, e, i2)
            v2 = jnp.where(better, cand, v2)
    g1 = jnp.zeros_like(v1)
    g2 = jnp.zeros_like(v1)
    for e in range(N_EXPERTS):
        g1 = jnp.where(i1 == e, sc[e], g1)
        g2 = jnp.where(i2 == e, sc[e], g2)
    tot = g1 + g2
    rows = i1.shape[1]
    e_ref[...] = jnp.concatenate([i1, i2, jnp.zeros((6, rows), jnp.int32)], axis=0)
    gt_ref[...] = jnp.concatenate([g1 / tot, g2 / tot, jnp.zeros((6, rows), F32)], axis=0)


def _router(x, mt, g, router_w, router_b):
    wt = router_w.T
    wh = wt.astype(BF16)
    wl = (wt - wh.astype(F32)).astype(BF16)
    return pl.pallas_call(
        _router_kernel,
        grid=(N_SEG,),
        in_specs=[
            pl.BlockSpec((SEG, D_MODEL), lambda i: (i, 0)),
            pl.BlockSpec((1, 6, D_MODEL), lambda i: (i, 0, 0)),
            pl.BlockSpec((1, D_MODEL), lambda i: (0, 0)),
            pl.BlockSpec((N_EXPERTS, D_MODEL), lambda i: (0, 0)),
            pl.BlockSpec((N_EXPERTS, D_MODEL), lambda i: (0, 0)),
            pl.BlockSpec((N_EXPERTS, 1), lambda i: (0, 0)),
        ],
        out_specs=[
            pl.BlockSpec((SEG, D_MODEL), lambda i: (i, 0)),
            pl.BlockSpec((8, SEG), lambda i: (0, i)),
            pl.BlockSpec((8, SEG), lambda i: (0, i)),
        ],
        out_shape=[
            jax.ShapeDtypeStruct((N_TOK, D_MODEL), F32),
            jax.ShapeDtypeStruct((8, N_TOK), jnp.int32),
            jax.ShapeDtypeStruct((8, N_TOK), F32),
        ],
        compiler_params=_cparams(("arbitrary",), 48),
        name="ffn_norm_router",
    )(x, mt, g, wh, wl, router_b.reshape(N_EXPERTS, 1))


def _expert_changed(te_ref, i):
    return jnp.logical_or(i == 0, te_ref[i] != te_ref[jnp.maximum(i - 1, 0)])


def _moe_up_kernel(te_ref, na_ref, xs_ref, wg_ref, wu_ref, hid_ref, wgb_ref, wub_ref):
    i = pl.program_id(0)

    @pl.when(_expert_changed(te_ref, i))
    def _():
        wgb_ref[...] = wg_ref[...].astype(BF16)
        wub_ref[...] = wu_ref[...].astype(BF16)

    @pl.when(i < na_ref[0])
    def _():
        x = xs_ref[...].astype(BF16)
        g = _dot(x, wgb_ref[...])
        u = _dot(x, wub_ref[...])
        hid_ref[...] = (g * _sigmoid(g) * u).astype(BF16)

    @pl.when(i >= na_ref[0])
    def _():
        hid_ref[...] = jnp.zeros_like(hid_ref)


def _moe_down_kernel(te_ref, na_ref, hid_ref, wd_ref, gate_ref, ys_ref, wdb_ref):
    i = pl.program_id(0)

    @pl.when(_expert_changed(te_ref, i))
    def _():
        wdb_ref[...] = wd_ref[...].astype(BF16)

    @pl.when(i < na_ref[0])
    def _():
        ys_ref[...] = _dot(hid_ref[...], wdb_ref[...]) * gate_ref[...]

    @pl.when(i >= na_ref[0])
    def _():
        ys_ref[...] = jnp.zeros_like(ys_ref)


def _moe_experts(xs, gate_rows, tile_expert, n_active, w_gate, w_up, w_down):
    hid = pl.pallas_call(
        _moe_up_kernel,
        grid_spec=pltpu.PrefetchScalarGridSpec(
            num_scalar_prefetch=2,
            grid=(MOE_TILES,),
            in_specs=[
                pl.BlockSpec((MOE_TM, D_MODEL), lambda i, te, na: (i, 0)),
                pl.BlockSpec((None, D_MODEL, D_EXPERT), lambda i, te, na: (te[i], 0, 0)),
                pl.BlockSpec((None, D_MODEL, D_EXPERT), lambda i, te, na: (te[i], 0, 0)),
            ],
            out_specs=pl.BlockSpec((MOE_TM, D_EXPERT), lambda i, te, na: (i, 0)),
            scratch_shapes=[pltpu.VMEM((D_MODEL, D_EXPERT), BF16), pltpu.VMEM((D_MODEL, D_EXPERT), BF16)],
        ),
        out_shape=jax.ShapeDtypeStruct((MOE_ROWS, D_EXPERT), BF16),
        compiler_params=_cparams(("arbitrary",), 52),
        name="moe_up",
    )(tile_expert, n_active, xs, w_gate, w_up)
    return pl.pallas_call(
        _moe_down_kernel,
        grid_spec=pltpu.PrefetchScalarGridSpec(
            num_scalar_prefetch=2,
            grid=(MOE_TILES,),
            in_specs=[
                pl.BlockSpec((MOE_TM, D_EXPERT), lambda i, te, na: (i, 0)),
                pl.BlockSpec((None, D_EXPERT, D_MODEL), lambda i, te, na: (te[i], 0, 0)),
                pl.BlockSpec((MOE_TM, 1), lambda i, te, na: (i, 0)),
            ],
            out_specs=pl.BlockSpec((MOE_TM, D_MODEL), lambda i, te, na: (i, 0)),
            scratch_shapes=[pltpu.VMEM((D_EXPERT, D_MODEL), BF16)],
        ),
        out_shape=jax.ShapeDtypeStruct((MOE_ROWS, D_MODEL), F32),
        compiler_params=_cparams(("arbitrary",), 40),
        name="moe_down",
    )(tile_expert, n_active, hid, w_down, gate_rows)


def _route_plan(eidx, gates):
    flat_e = eidx[:2].reshape(-1)
    flat_g = gates[:2].reshape(-1)
    n_asg = flat_e.shape[0]
    cnt = jnp.bincount(flat_e, length=N_EXPERTS).astype(jnp.int32)
    tiles = (cnt + MOE_TM - 1) // MOE_TM
    tile_end = jnp.cumsum(tiles)
    row_off = (tile_end - tiles) * MOE_TM
    start = jnp.cumsum(cnt) - cnt
    order = jnp.argsort(flat_e, stable=True).astype(jnp.int32)
    n_active = tile_end[-1]
    tile_ids = jnp.arange(MOE_TILES, dtype=jnp.int32)
    te = jnp.sum(tile_end[None, :] <= jnp.minimum(tile_ids, n_active - 1)[:, None], axis=1).astype(jnp.int32)
    row_e = jnp.repeat(te, MOE_TM)
    rows = jnp.arange(MOE_ROWS, dtype=jnp.int32)
    rank = rows - row_off[row_e]
    valid = jnp.logical_and(rank < cnt[row_e], rows < n_active * MOE_TM)
    src_asg = order[jnp.clip(start[row_e] + rank, 0, n_asg - 1)]
    src_tok = jnp.where(valid, src_asg % N_TOK, 0)
    gate_rows = jnp.where(valid, flat_g[src_asg], 0.0)
    sorted_e = flat_e[order]
    dest_sorted = row_off[sorted_e] + jnp.arange(n_asg, dtype=jnp.int32) - start[sorted_e]
    pos = dest_sorted[jnp.argsort(order)]
    return src_tok, gate_rows.reshape(MOE_ROWS, 1), te, n_active.reshape(1).astype(jnp.int32), pos


def _combine_kernel(x_ref, y0_ref, y1_ref, mt_ref, fg_ref, o_ref, *, final):
    x = x_ref[...] + mt_ref[0, 5:6, :] * (y0_ref[...] + y1_ref[...])
    if final:
        x = x * lax.rsqrt(jnp.mean(x * x, axis=-1, keepdims=True) + EPS) * fg_ref[...]
    o_ref[...] = x


def _combine(x, ysel, mt, final_g, final):
    tm = 256
    return pl.pallas_call(
        functools.partial(_combine_kernel, final=final),
        grid=(N_TOK // tm,),
        in_specs=[
            pl.BlockSpec((tm, D_MODEL), lambda i: (i, 0)),
            pl.BlockSpec((None, tm, D_MODEL), lambda i: (0, i, 0)),
            pl.BlockSpec((None, tm, D_MODEL), lambda i: (1, i, 0)),
            pl.BlockSpec((1, 6, D_MODEL), lambda i: (i // (SEG // tm), 0, 0)),
            pl.BlockSpec((1, D_MODEL), lambda i: (0, 0)),
        ],
        out_specs=pl.BlockSpec((tm, D_MODEL), lambda i: (i, 0)),
        out_shape=jax.ShapeDtypeStruct((N_TOK, D_MODEL), F32),
        compiler_params=_cparams(("arbitrary",), 40),
        name="moe_combine",
    )(x, ysel, ysel, mt, final_g)


def _rope_tables(dim):
    half = dim // 2
    inv = ROPE_THETA ** (-jnp.arange(0, half, 2, dtype=F32) / half)
    t = jnp.arange(DEC_SEQ)
    ang_r = (t // GRID_W).astype(F32)[:, None] * inv[None, :]
    ang_c = (t % GRID_W).astype(F32)[:, None] * inv[None, :]
    ang = jnp.concatenate([ang_r, ang_r, ang_c, ang_c], axis=-1)
    cos, sin = jnp.cos(ang), jnp.sin(ang)
    reps = LANES // dim
    cos = jnp.tile(cos, (1, reps))
    sin = jnp.tile(sin, (1, reps))
    sh = dim // 4
    second = (np.arange(LANES) % (2 * sh)) >= sh
    sp = jnp.where(second[None, :], sin, 0.0)
    sm = jnp.where(second[None, :], 0.0, -sin)
    return cos, sp, sm


def _neighbourhood_bias(rpb):
    rows = DEC_SEQ // GRID_W
    kh = min(NA_ROWS, rows)
    r = np.arange(rows)
    r0 = np.clip(r - kh // 2, 0, rows - kh)
    kr = np.arange(rows)
    row_ok = (kr[None, :] >= r0[:, None]) & (kr[None, :] < r0[:, None] + kh)
    dr_idx = np.clip(kr[None, :] - r[:, None] + NA_ROWS - 1, 0, 2 * NA_ROWS - 2)
    c = np.arange(GRID_W)
    ws = np.clip(c - NA_COLS // 2, 0, GRID_W - NA_COLS)
    kc = np.arange(GRID_W)
    col_ok = (kc[None, :] >= ws[:, None]) & (kc[None, :] < ws[:, None] + NA_COLS)
    dc_idx = np.clip(kc[None, :] - c[:, None], -(NA_COLS - 1), NA_COLS - 1) + NA_COLS - 1
    sel_r = ((np.arange(2 * NA_ROWS - 1)[:, None, None] == dr_idx[None]) & row_ok[None]).astype(np.float32)
    sel_c = ((np.arange(2 * NA_COLS - 1)[:, None, None] == dc_idx[None]) & col_ok[None]).astype(np.float32)
    t = jnp.einsum("hab,ark->hrkb", rpb.astype(F32), sel_r, precision=lax.Precision.HIGHEST)
    b = jnp.einsum("hrkb,bcx->hrckx", t, sel_c, precision=lax.Precision.HIGHEST)
    ok = row_ok[:, None, :, None] & col_ok[None, :, None, :]
    return jnp.where(ok[None], b, NEG).reshape(N_HEADS, DEC_SEQ, DEC_SEQ)


def _even_layer(x, mt, l, i, rope64, cache_a_k, cache_a_v, cache_b_ckv, cache_b_krope, norm_mix_g, ev_w_in,
                ev_lambda, ev_subln_g, ev_q_norm_g, ev_kv_norm_g, ev_w_uq, ev_w_ukv, ev_w_out):
    w = ev_w_in[i]
    kr = w[:, 4352:4416]
    z = jnp.zeros_like(kr)
    w_in = jnp.concatenate([w[:, :3840], kr, z, z, kr, w[:, 3840:4352]], axis=1).astype(BF16)
    all_g = (0, 1, 2, 3)
    in_kw = dict(prologue="modulate", g=norm_mix_g[l][None, :], norm_tile=8, norm_g=ev_kv_norm_g[i][None, :])
    head_src = lambda t0: tuple((t0 + t, gq, 4 * t + gq) for t in range(2) for gq in range(4))
    proj_p, st_ak, st_av, st_ckv, st_kr = _projection(
        x, 0, N_SEG_P, D_MODEL, 0, D_MODEL, w_in, mt=mt[:N_SEG_P], name="even_in_proj_ctx",
        states=((N_HEADS, head_src(2)), (N_HEADS, head_src(4)), (0, ((8, 0, B_KV_RANK),)), (0, ((7, 256, B_ROPE_DIM),))),
        **in_kw)
    proj_l = _projection(x, N_SEG_P, DEC_BATCH, D_MODEL, 0, D_MODEL, w_in, mt=mt[N_SEG_P:], rope=rope64,
                         sh=A_QK_DIM // 4, rope_groups={0: all_g, 1: all_g, 2: all_g, 3: all_g, 7: (2, 3)},
                         name="even_in_proj_lat", **in_kw)
    wq = ev_w_uq[i].reshape(B_Q_RANK, N_HEADS, HEAD_DIM + B_ROPE_DIM)
    w_uq = jnp.concatenate([wq[:, :, :HEAD_DIM].reshape(B_Q_RANK, -1), wq[:, :, HEAD_DIM:].reshape(B_Q_RANK, -1)],
                           axis=1).astype(BF16)
    q_kw = dict(prologue="rmsnorm", g=ev_q_norm_g[i][None, :], out_dtype=BF16)
    bq_p = _projection(proj_p, 0, N_SEG_P, 1024, 3, B_Q_RANK, w_uq, name="mla_q_up_ctx", **q_kw)
    bq_l = _projection(proj_l, 0, DEC_BATCH, 1024, 3, B_Q_RANK, w_uq, rope=rope64, sh=B_ROPE_DIM // 4,
                       rope_groups={2: all_g}, name="mla_q_up_lat", **q_kw)
    w_ukv = ev_w_ukv[i].astype(BF16)
    kv_kw = dict(prologue="cast", out_dtype=BF16)
    kv_p = _projection(proj_p, 0, N_SEG_P, 512, 8, B_KV_RANK, w_ukv, name="mla_kv_up_ctx", **kv_kw)
    kv_l = _projection(proj_l, 0, DEC_BATCH, 512, 8, B_KV_RANK, w_ukv, name="mla_kv_up_lat", **kv_kw)
    kv_ctx = _projection(cache_b_ckv[:, i].reshape(DEC_BATCH * PAST_LEN, B_KV_RANK), 0, 1, 512, 0, B_KV_RANK, w_ukv,
                         name="mla_kv_up_cache", **kv_kw)
    kv_ctx = kv_ctx.reshape(DEC_BATCH, PAST_LEN, 2 * N_HEADS * HEAD_DIM)
    krc = cache_b_krope[:, i]
    zc = jnp.zeros_like(krc)
    kr_ctx = jnp.concatenate([krc, zc, zc, krc], axis=-1)
    ak_ctx = cache_a_k[:, i].reshape(DEC_BATCH, PAST_LEN, N_HEADS * HEAD_DIM)
    av_ctx = cache_a_v[:, i].reshape(DEC_BATCH, PAST_LEN, N_HEADS * HEAD_DIM)
    lam_init = 0.8 - 0.6 * math.exp(-0.3 * l)
    o = jnp.zeros((N_TOK, D_MODEL), BF16)
    for latent, proj, b_q, kv in ((False, proj_p, bq_p, kv_p), (True, proj_l, bq_l, kv_l)):
        tag = "lat" if latent else "ctx"
        o = _attention(kind="diff", latent=latent, q_parts=[(proj, lambda h: h)], ko_parts=[(proj, lambda h: 8 + h)],
                       vo=(proj, lambda h: 16 + h), kc_parts=[(ak_ctx, lambda h: h)], vc=(av_ctx, lambda h: h),
                       lam=ev_lambda[i], subln=ev_subln_g[i][None, :], o_arr=o, o_blk=lambda h: h,
                       scale=A_QK_DIM ** -0.5, lam_init=lam_init, name="diff_attn_" + tag)
        o = _attention(kind="mla", latent=latent, q_parts=[(b_q, lambda h: h), (b_q, lambda h: 8 + h // 2)],
                       ko_parts=[(kv, lambda h: 2 * h), (proj, lambda h: 30 + h % 2)], vo=(kv, lambda h: 2 * h + 1),
                       kc_parts=[(kv_ctx, lambda h: 2 * h), (kr_ctx, lambda h: h % 2)], vc=(kv_ctx, lambda h: 2 * h + 1),
                       o_arr=o, o_blk=lambda h: 8 + h, scale=(HEAD_DIM + B_ROPE_DIM) ** -0.5, name="mla_attn_" + tag)
    y = _out_projection(o, ev_w_out[i].astype(BF16), x, mt)
    states = (st_ak.reshape(BATCH, SEQ, N_HEADS, HEAD_DIM), st_av.reshape(BATCH, SEQ, N_HEADS, HEAD_DIM),
              st_ckv.reshape(BATCH, SEQ, B_KV_RANK), st_kr.reshape(BATCH, SEQ, B_ROPE_DIM))
    return y, states


def _odd_layer(x, mt, l, i, rope128, cache_c_k, cache_c_v, cache_d_k, cache_d_v, norm_mix_g, od_w_in, od_rpb,
               od_sink, od_w_out):
    w_in = od_w_in[i].astype(BF16)
    all_g = (0, 1, 2, 3)
    in_kw = dict(prologue="modulate", g=norm_mix_g[l][None, :])
    head_src = lambda t0: tuple((t0 + t, gq, 4 * t + gq) for t in range(2) for gq in range(4))
    proj_p, st_ck, st_cv, st_dk, st_dv = _projection(
        x, 0, N_SEG_P, D_MODEL, 0, D_MODEL, w_in, mt=mt[:N_SEG_P], name="odd_in_proj_ctx",
        states=((N_HEADS, head_src(2)), (N_HEADS, head_src(4)), (D_KV_HEADS, ((8, 0, 0), (8, 1, 1))),
                (D_KV_HEADS, ((8, 2, 0), (8, 3, 1)))), **in_kw)
    proj_l = _projection(x, N_SEG_P, DEC_BATCH, D_MODEL, 0, D_MODEL, w_in, mt=mt[N_SEG_P:], rope=rope128,
                         sh=HEAD_DIM // 4, rope_groups={6: all_g, 7: all_g, 8: (0, 1)}, name="odd_in_proj_lat", **in_kw)
    ck_ctx = cache_c_k[:, i].reshape(DEC_BATCH, PAST_LEN, N_HEADS * HEAD_DIM)
    cv_ctx = cache_c_v[:, i].reshape(DEC_BATCH, PAST_LEN, N_HEADS * HEAD_DIM)
    dk_ctx = cache_d_k[:, i].reshape(DEC_BATCH, PAST_LEN, D_KV_HEADS * HEAD_DIM)
    dv_ctx = cache_d_v[:, i].reshape(DEC_BATCH, PAST_LEN, D_KV_HEADS * HEAD_DIM)
    bias = _neighbourhood_bias(od_rpb[i])
    sink = od_sink[i].astype(F32)
    o = jnp.zeros((N_TOK, D_MODEL), BF16)
    for latent, proj in ((False, proj_p), (True, proj_l)):
        tag = "lat" if latent else "ctx"
        o = _attention(kind="plain", latent=latent, q_parts=[(proj, lambda h: h)], ko_parts=[(proj, lambda h: 8 + h)],
                       vo=(proj, lambda h: 16 + h), kc_parts=[(ck_ctx, lambda h: h)], vc=(cv_ctx, lambda h: h),
                       bias=bias if latent else None, o_arr=o, o_blk=lambda h: h, scale=HEAD_DIM ** -0.5,
                       name="nbr_attn_" + tag)
        o = _attention(kind="plain", latent=latent, q_parts=[(proj, lambda h: 24 + h)],
                       ko_parts=[(proj, lambda h: 32 + h // D_GROUP)], vo=(proj, lambda h: 34 + h // D_GROUP),
                       kc_parts=[(dk_ctx, lambda h: h // D_GROUP)], vc=(dv_ctx, lambda h: h // D_GROUP), sink=sink,
                       o_arr=o, o_blk=lambda h: 8 + h, scale=HEAD_DIM ** -0.5, has_band=latent,
                       name="win_attn_" + tag)
    y = _out_projection(o, od_w_out[i].astype(BF16), x, mt)
    states = (st_ck.reshape(BATCH, SEQ, N_HEADS, HEAD_DIM), st_cv.reshape(BATCH, SEQ, N_HEADS, HEAD_DIM),
              st_dk.reshape(BATCH, SEQ, D_KV_HEADS, HEAD_DIM), st_dv.reshape(BATCH, SEQ, D_KV_HEADS, HEAD_DIM))
    return y, states


def _moe_layer(x, mt, g, router_w, router_b, w_gate, w_up, w_down, final_g, final):
    h2, eidx, gates = _router(x, mt, g, router_w, router_b)
    src_tok, gate_rows, te, n_active, pos = _route_plan(eidx, gates)
    xs = h2.at[src_tok].get(mode="promise_in_bounds")
    ys = _moe_experts(xs, gate_rows, te, n_active, w_gate, w_up, w_down)
    ysel = ys.at[pos].get(mode="promise_in_bounds").reshape(2, N_TOK, D_MODEL)
    return _combine(x, ysel, mt, final_g, final)


def kernel(x_prompt, x_sample, cache_a_k, cache_a_v, cache_b_ckv, cache_b_krope, cache_c_k, cache_c_v, cache_d_k, cache_d_v, c, c_ctx, w_ada, b_ada, norm_mix_g, norm_ffn_g, ev_w_in, ev_lambda, ev_subln_g, ev_q_norm_g, ev_kv_norm_g, ev_w_uq, ev_w_ukv, ev_w_out, od_w_in, od_rpb, od_sink, od_w_out, router_w, router_b, moe_w_gate, moe_w_up, moe_w_down, final_g):
    x = jnp.concatenate([x_prompt.reshape(N_PROMPT, D_MODEL), x_sample.reshape(DEC_BATCH * DEC_SEQ, D_MODEL)], axis=0)
    cond8 = jnp.concatenate([c_ctx[None, :], c, jnp.zeros((3, D_MODEL), F32)], axis=0)
    mod = _adaln(cond8, w_ada, b_ada)
    seg_row = np.array([0] * N_SEG_P + [1 + b for b in range(DEC_BATCH)])
    mt_all = mod[:, seg_row].reshape(DEPTH, N_SEG, 6, D_MODEL)
    rope64 = _rope_tables(A_QK_DIM)
    rope128 = _rope_tables(HEAD_DIM)
    even_states, odd_states = [], []
    for l in range(DEPTH):
        i = l // 2
        mt = mt_all[l]
        if l % 2 == 0:
            x, st = _even_layer(x, mt, l, i, rope64, cache_a_k, cache_a_v, cache_b_ckv, cache_b_krope, norm_mix_g,
                                ev_w_in, ev_lambda, ev_subln_g, ev_q_norm_g, ev_kv_norm_g, ev_w_uq, ev_w_ukv, ev_w_out)
            even_states.append(st)
        else:
            x, st = _odd_layer(x, mt, l, i, rope128, cache_c_k, cache_c_v, cache_d_k, cache_d_v, norm_mix_g,
                               od_w_in, od_rpb, od_sink, od_w_out)
            odd_states.append(st)
        x = _moe_layer(x, mt, norm_ffn_g[l][None, :], router_w, router_b, moe_w_gate[l], moe_w_up[l], moe_w_down[l],
                       final_g[None, :], final=(l == DEPTH - 1))
    y_prompt = x[:N_PROMPT].reshape(BATCH, SEQ, D_MODEL)
    y_sample = x[N_PROMPT:].reshape(DEC_BATCH, DEC_SEQ, D_MODEL)
    new_even = tuple(jnp.stack([st[k] for st in even_states], axis=1) for k in range(4))
    new_odd = tuple(jnp.stack([st[k] for st in odd_states], axis=1) for k in range(4))
    return (y_prompt, y_sample) + new_even + new_odd
```

```python
import functools
import math

import numpy as np
import jax
import jax.numpy as jnp
from jax import lax
from jax.experimental import pallas as pl
from jax.experimental.pallas import tpu as pltpu

D_MODEL = 2048
BATCH = 32
SEQ = 256
DEPTH = 2
DEC_BATCH = 4
DEC_SEQ = 1024
PAST_LEN = 256
GRID_W = 64
HEAD_DIM = 128
N_HEADS = 8
A_QK_DIM = 64
B_Q_RANK = 768
B_KV_RANK = 512
B_ROPE_DIM = 64
NA_ROWS = 8
NA_COLS = 16
D_KV_HEADS = 2
D_GROUP = 4
D_WINDOW = 128
N_EXPERTS = 16
N_GROUPS = 4
D_EXPERT = 1024
ROPE_THETA = 10000.0
EPS = 1e-6
NEG = -1e30

SEG = 1024
N_PROMPT = BATCH * SEQ
N_TOK = N_PROMPT + DEC_BATCH * DEC_SEQ
N_SEG = N_TOK // SEG
N_SEG_P = N_PROMPT // SEG
PROJ_W = 4608
TN = 512
LANES = 128
MOE_TM = 256
MOE_ROWS = 2 * N_TOK + N_EXPERTS * MOE_TM
MOE_TILES = MOE_ROWS // MOE_TM

F32 = jnp.float32
BF16 = jnp.bfloat16


def _cparams(sem, vmem_mb):
    return pltpu.CompilerParams(dimension_semantics=sem, vmem_limit_bytes=vmem_mb * 1024 * 1024)


def _dot(a, b):
    return jnp.dot(a, b, preferred_element_type=F32)


def _dot_nt(a, b):
    return lax.dot_general(a, b, (((1,), (1,)), ((), ())), preferred_element_type=F32)


def _sigmoid(x):
    return 1.0 / (1.0 + jnp.exp(-x))


def _adaln_kernel(c_ref, w_ref, b_ref, o_ref):
    c = c_ref[...]
    a = (c * _sigmoid(c)).astype(BF16)
    o_ref[...] = _dot(a, w_ref[...].astype(BF16)) + b_ref[...]


def _adaln(cond8, w_ada, b_ada):
    tn = 1024
    n = 6 * D_MODEL
    return pl.pallas_call(
        _adaln_kernel,
        grid=(DEPTH, n // tn),
        in_specs=[
            pl.BlockSpec((8, D_MODEL), lambda l, j: (0, 0)),
            pl.BlockSpec((None, D_MODEL, tn), lambda l, j: (l, 0, j)),
            pl.BlockSpec((None, 1, tn), lambda l, j: (l, 0, j)),
        ],
        out_specs=pl.BlockSpec((None, 8, tn), lambda l, j: (l, 0, j)),
        out_shape=jax.ShapeDtypeStruct((DEPTH, 8, n), F32),
        compiler_params=_cparams(("arbitrary", "arbitrary"), 40),
        name="adaln",
    )(cond8, w_ada, b_ada.reshape(DEPTH, 1, n))


def _rope(a, cos, sp, sm, sh):
    return a * cos + pltpu.roll(a, sh, 1) * sp + pltpu.roll(a, LANES - sh, 1) * sm


def _proj_kernel(*refs, prologue, k_in, sh, n_tiles, rope_groups, norm_tile, states):
    x_ref, mt_ref, g_ref, w_ref, cos_ref, sp_ref, sm_ref, ng_ref, o_ref = refs[:9]
    st_refs = refs[9:-1]
    xn_ref = refs[-1]
    j = pl.program_id(1)

    @pl.when(j == 0)
    def _():
        x = x_ref[:, :k_in]
        if prologue == "cast":
            xn_ref[...] = x.astype(BF16)
        else:
            y = x * lax.rsqrt(jnp.mean(x * x, axis=-1, keepdims=True) + EPS) * g_ref[...]
            if prologue == "modulate":
                y = y * (1.0 + mt_ref[0, 1:2, :]) + mt_ref[0, 0:1, :]
            xn_ref[...] = y.astype(BF16)

    acc = _dot(xn_ref[...], w_ref[...])
    n_grp = acc.shape[1] // LANES

    def treatment(t):
        acts = tuple((k, a, b) for k, st in enumerate(states) for (tt, a, b) in st[1] if tt == t)
        return (tuple(rope_groups.get(t, ())), t == norm_tile, acts)

    branches = {}
    for t in range(n_tiles):
        branches.setdefault(treatment(t), []).append(t)

    o_ref[...] = acc.astype(o_ref.dtype)
    reread = o_ref.dtype == F32

    for (rg, is_norm, acts), tiles in branches.items():
        if not rg and not is_norm and not acts:
            continue
        cond = j == tiles[0]
        for t in tiles[1:]:
            cond = jnp.logical_or(cond, j == t)

        @pl.when(cond)
        def _(rg=rg, is_norm=is_norm, acts=acts):
            vals = [(o_ref if reread else acc)[:, g * LANES:(g + 1) * LANES] for g in range(n_grp)]
            if is_norm:
                ssq = None
                for v in vals:
                    s = jnp.sum(v * v, axis=-1, keepdims=True)
                    ssq = s if ssq is None else ssq + s
                scale = lax.rsqrt(ssq * (1.0 / (n_grp * LANES)) + EPS)
                ng = ng_ref[...]
                vals = [v * scale * ng[:, g * LANES:(g + 1) * LANES] for g, v in enumerate(vals)]
            elif rg:
                cos, sp, sm = cos_ref[...], sp_ref[...], sm_ref[...]
                vals = [_rope(v, cos, sp, sm, sh) if g in rg else v for g, v in enumerate(vals)]
            if is_norm or rg:
                for g, v in enumerate(vals):
                    if is_norm or g in rg:
                        o_ref[:, g * LANES:(g + 1) * LANES] = v.astype(o_ref.dtype)
            for k, a, b in acts:
                n_heads = states[k][0]
                if n_heads:
                    st_refs[k][pl.ds(b, SEG, stride=n_heads), :] = vals[a]
                elif b % LANES == 0:
                    for q in range(b // LANES):
                        st_refs[k][:, q * LANES:(q + 1) * LANES] = vals[a // LANES + q]
                else:
                    st_refs[k][...] = vals[a // LANES][:, a % LANES:a % LANES + b]


def _projection(x, row_off, n_seg, x_blk_w, x_blk_idx, k_in, w, *, prologue, mt=None, g=None, rope=None, sh=16,
                rope_groups=None, norm_tile=None, norm_g=None, states=(), out_dtype=F32, name):
    n = w.shape[1]
    m = n_seg * SEG
    assert n % TN == 0 and w.shape[0] == k_in
    if mt is None:
        mt = jnp.zeros((n_seg, 6, LANES), F32)
    if g is None:
        g = jnp.ones((1, k_in), F32)
    if rope is None:
        rope = tuple(jnp.zeros((8, LANES), F32) for _ in range(3))
    if norm_g is None:
        norm_g = jnp.ones((1, TN), F32)
    kern = functools.partial(_proj_kernel, prologue=prologue, k_in=k_in, sh=sh, n_tiles=n // TN,
                             rope_groups=dict(rope_groups or {}), norm_tile=norm_tile, states=tuple(states))
    full2 = lambda i, j: (0, 0)
    out_shape = [jax.ShapeDtypeStruct((m, n), out_dtype)]
    out_specs = [pl.BlockSpec((SEG, TN), lambda i, j: (i, j))]
    for n_heads, src in states:
        if n_heads:
            out_shape.append(jax.ShapeDtypeStruct((m * n_heads, LANES), F32))
            out_specs.append(pl.BlockSpec((SEG * n_heads, LANES), lambda i, j: (i, 0)))
        else:
            width = src[0][2]
            out_shape.append(jax.ShapeDtypeStruct((m, width), F32))
            out_specs.append(pl.BlockSpec((SEG, width), lambda i, j: (i, 0)))
    x_kw = dict(pipeline_mode=pl.Buffered(1)) if states else {}
    res = pl.pallas_call(
        kern,
        grid=(n_seg, n // TN),
        in_specs=[
            pl.BlockSpec((SEG, x_blk_w), lambda i, j: (i + row_off, x_blk_idx), **x_kw),
            pl.BlockSpec((1, 6, mt.shape[2]), lambda i, j: (i, 0, 0)),
            pl.BlockSpec((1, k_in), full2),
            pl.BlockSpec((k_in, TN), lambda i, j: (0, j)),
            pl.BlockSpec(rope[0].shape, full2),
            pl.BlockSpec(rope[1].shape, full2),
            pl.BlockSpec(rope[2].shape, full2),
            pl.BlockSpec((1, TN), full2),
        ],
        out_specs=out_specs,
        out_shape=out_shape,
        scratch_shapes=[pltpu.VMEM((SEG, k_in), BF16)],
        compiler_params=_cparams(("arbitrary", "arbitrary"), 56 if states else 48),
        name=name,
    )(x, mt, g, w, *rope, norm_g)
    return res if states else res[0]


def _attn_kernel(*refs, kind, n_parts, has_ctx, has_bias, has_band, has_sink, nb, s_len, tq, scale, lam_init):
    it = iter(refs)
    q_refs = [next(it) for _ in range(n_parts)]
    ko_refs = [next(it) for _ in range(n_parts)]
    vo_ref = next(it)
    kc_refs, vc_ref = [], None
    if has_ctx:
        kc_refs = [next(it) for _ in range(n_parts)]
        vc_ref = next(it)
    bias_ref = next(it) if has_bias else None
    lam_ref = sg_ref = None
    if kind == "diff":
        lam_ref = next(it)
        sg_ref = next(it)
    sink_ref = next(it) if has_sink else None
    o_ref = next(it)

    sink = sink_ref[pl.program_id(1)] if has_sink else None
    lane = lax.broadcasted_iota(jnp.int32, (1, LANES), 1)

    def softmax_parts(s_list):
        m = None
        for s in s_list:
            mm = jnp.max(s, axis=-1, keepdims=True)
            m = mm if m is None else jnp.maximum(m, mm)
        if sink is not None:
            m = jnp.maximum(m, sink)
        e_list = [jnp.exp(s - m) for s in s_list]
        l = None
        for e in e_list:
            ss = jnp.sum(e, axis=-1, keepdims=True)
            l = ss if l is None else l + ss
        if sink is not None:
            l = l + jnp.exp(sink - m)
        return e_list, l

    for bi in range(nb):
        qrows = pl.ds(bi * tq, tq)
        krows = pl.ds(bi * s_len, s_len)
        k_srcs, v_srcs, is_own = [], [], []
        if has_ctx:
            k_srcs.append([r[...].astype(BF16) for r in kc_refs])
            v_srcs.append(vc_ref[...].astype(BF16))
            is_own.append(False)
        k_srcs.append([r[krows, :].astype(BF16) for r in ko_refs])
        v_srcs.append(vo_ref[krows, :].astype(BF16))
        is_own.append(True)

        def mask_own(s):
            if has_bias:
                s = s + bias_ref[...]
            if has_band:
                qpos = pl.program_id(2) * tq + lax.broadcasted_iota(jnp.int32, (tq, 1), 0)
                kpos = lax.broadcasted_iota(jnp.int32, (1, s_len), 1)
                s = jnp.where(jnp.abs(kpos - qpos) <= D_WINDOW, s, NEG)
            return s

        if kind == "diff":
            q = q_refs[0][qrows, :]
            q1 = jnp.where(lane < A_QK_DIM, q, 0.0).astype(BF16)
            q2 = jnp.where(lane < A_QK_DIM, 0.0, q).astype(BF16)
            s1 = [_dot_nt(q1, ks[0]) * scale for ks in k_srcs]
            s2 = [_dot_nt(q2, ks[0]) * scale for ks in k_srcs]
            e1, l1 = softmax_parts(s1)
            e2, l2 = softmax_parts(s2)
            lv = lam_ref[...]
            lam = (jnp.exp(jnp.sum(lv[0:1] * lv[1:2], axis=-1, keepdims=True))
                   - jnp.exp(jnp.sum(lv[2:3] * lv[3:4], axis=-1, keepdims=True)) + lam_init)
            r1 = 1.0 / l1
            r2 = lam / l2
            o = None
            for a, b, v in zip(e1, e2, v_srcs):
                w = (a * r1 - b * r2).astype(BF16)
                pv = _dot(w, v)
                o = pv if o is None else o + pv
            o = o * lax.rsqrt(jnp.mean(o * o, axis=-1, keepdims=True) + EPS) * sg_ref[...] * (1.0 - lam_init)
        else:
            qs = [r[qrows, :].astype(BF16) for r in q_refs]
            s_list = []
            for ks, own in zip(k_srcs, is_own):
                s = None
                for qp, kp in zip(qs, ks):
                    d = _dot_nt(qp, kp)
                    s = d if s is None else s + d
                s = s * scale
                s_list.append(mask_own(s) if own else s)
            e_list, l = softmax_parts(s_list)
            o = None
            for e, v in zip(e_list, v_srcs):
                pv = _dot(e.astype(BF16), v)
                o = pv if o is None else o + pv
            o = o * (1.0 / l)
        o_ref[qrows, :] = o.astype(o_ref.dtype)


def _attention(*, kind, latent, q_parts, ko_parts, vo, kc_parts=(), vc=None, bias=None, lam=None, subln=None,
               sink=None, o_arr, o_blk, scale, lam_init=0.0, has_band=False, name):
    n_parts = len(q_parts)
    has_ctx = latent
    if latent:
        tq, s_len, nb = 512, DEC_SEQ, 1
        grid = (DEC_BATCH, N_HEADS, DEC_SEQ // tq)
        qpb = DEC_SEQ // tq
        q_row = lambda b, h, t: b * qpb + t
        o_row = lambda b, h, t: N_PROMPT // tq + b * qpb + t
        k_row = lambda b, h, t: b
        qblk = tq
    else:
        tq, s_len, nb = SEQ, SEQ, SEG // SEQ
        grid = (N_SEG_P, N_HEADS, 1)
        q_row = o_row = k_row = lambda b, h, t: b
        qblk = SEG
    args, specs = [], []

    def add(arr, shape, imap, **kw):
        args.append(arr)
        specs.append(pl.BlockSpec(shape, imap, **kw))

    for arr, f in q_parts:
        add(arr, (qblk, LANES), lambda b, h, t, f=f: (q_row(b, h, t), f(h)))
    for arr, f in list(ko_parts) + [vo]:
        add(arr, (SEG, LANES), lambda b, h, t, f=f: (k_row(b, h, t), f(h)))
    if has_ctx:
        for arr, f in list(kc_parts) + [vc]:
            add(arr, (None, PAST_LEN, LANES), lambda b, h, t, f=f: (b, 0, f(h)))
    if bias is not None:
        add(bias, (None, tq, DEC_SEQ), lambda b, h, t: (h, t, 0))
    if kind == "diff":
        add(lam, lam.shape, lambda b, h, t: (0, 0))
        add(subln, subln.shape, lambda b, h, t: (0, 0))
    if sink is not None:
        args.append(sink)
        specs.append(pl.BlockSpec(memory_space=pltpu.SMEM))
    n_in = len(args)
    args.append(o_arr)
    specs.append(pl.BlockSpec(memory_space=pl.ANY))
    kern = functools.partial(_attn_kernel_aliased, kind=kind, n_parts=n_parts, has_ctx=has_ctx,
                             has_bias=bias is not None, has_band=has_band, has_sink=sink is not None,
                             nb=nb, s_len=s_len, tq=tq, scale=scale, lam_init=lam_init)
    return pl.pallas_call(
        kern,
        grid=grid,
        in_specs=specs,
        out_specs=pl.BlockSpec((qblk, LANES), lambda b, h, t: (o_row(b, h, t), o_blk(h))),
        out_shape=jax.ShapeDtypeStruct(o_arr.shape, o_arr.dtype),
        input_output_aliases={n_in: 0},
        compiler_params=_cparams(("arbitrary", "arbitrary", "arbitrary"), 48),
        name=name,
    )(*args)


def _attn_kernel_aliased(*refs, **kw):
    _attn_kernel(*refs[:-2], refs[-1], **kw)


def _outproj_kernel(o_ref, w_ref, x_ref, mt_ref, y_ref):
    y_ref[...] = x_ref[...] + mt_ref[0, 2:3, :] * _dot(o_ref[...], w_ref[...])


def _out_projection(o, w, x, mt):
    return pl.pallas_call(
        _outproj_kernel,
        grid=(N_SEG, D_MODEL // TN),
        in_specs=[
            pl.BlockSpec((SEG, D_MODEL), lambda i, j: (i, 0)),
            pl.BlockSpec((D_MODEL, TN), lambda i, j: (0, j)),
            pl.BlockSpec((SEG, TN), lambda i, j: (i, j)),
            pl.BlockSpec((1, 6, TN), lambda i, j: (i, 0, j)),
        ],
        out_specs=pl.BlockSpec((SEG, TN), lambda i, j: (i, j)),
        out_shape=jax.ShapeDtypeStruct((N_TOK, D_MODEL), F32),
        compiler_params=_cparams(("arbitrary", "arbitrary"), 40),
        name="out_proj",
    )(o, w, x, mt)


def _router_kernel(x_ref, mt_ref, g_ref, wh_ref, wl_ref, rb_ref, h_ref, e_ref, gt_ref):
    x = x_ref[...]
    y = x * lax.rsqrt(jnp.mean(x * x, axis=-1, keepdims=True) + EPS) * g_ref[...]
    h = y * (1.0 + mt_ref[0, 4:5, :]) + mt_ref[0, 3:4, :]
    hh = h.astype(BF16)
    h_ref[...] = h
    hl = (h - hh.astype(F32)).astype(BF16)
    wh, wl = wh_ref[...], wl_ref[...]
    logits = _dot_nt(wh, hh) + _dot_nt(wh, hl) + _dot_nt(wl, hh)
    scores = _sigmoid(logits)
    sel = scores + rb_ref[...]
    per = N_EXPERTS // N_GROUPS
    sc = [scores[e:e + 1, :] for e in range(N_EXPERTS)]
    sl = [sel[e:e + 1, :] for e in range(N_EXPERTS)]
    best_g, best_v = None, None
    for gi in range(N_GROUPS):
        a, b, c, d = sl[gi * per:(gi + 1) * per]
        hi1, lo1, hi2, lo2 = jnp.maximum(a, b), jnp.minimum(a, b), jnp.maximum(c, d), jnp.minimum(c, d)
        gs = jnp.maximum(hi1, hi2) + jnp.maximum(jnp.minimum(hi1, hi2), jnp.maximum(lo1, lo2))
        if gi == 0:
            best_g, best_v = jnp.zeros_like(gs, dtype=jnp.int32), gs
        else:
            better = gs > best_v
            best_g = jnp.where(better, gi, best_g)
            best_v = jnp.where(better, gs, best_v)
    masked = [jnp.where(best_g == (e // per), sl[e], NEG) for e in range(N_EXPERTS)]
    i1, v1 = jnp.zeros_like(best_g), masked[0]
    for e in range(1, N_EXPERTS):
        better = masked[e] > v1
        i1 = jnp.where(better, e, i1)
        v1 = jnp.where(better, masked[e], v1)
    i2, v2 = None, None
    for e in range(N_EXPERTS):
        cand = jnp.where(i1 == e, -2e30, masked[e])
        if e == 0:
            i2, v2 = jnp.zeros_like(best_g), cand
        else:
            better = cand > v2
            i2 = jnp.where(better, e, i2)
            v2 = jnp.where(better, cand, v2)
    g1 = jnp.zeros_like(v1)
    g2 = jnp.zeros_like(v1)
    for e in range(N_EXPERTS):
        g1 = jnp.where(i1 == e, sc[e], g1)
        g2 = jnp.where(i2 == e, sc[e], g2)
    tot = g1 + g2
    rows = i1.shape[1]
    e_ref[...] = jnp.concatenate([i1, i2, jnp.zeros((6, rows), jnp.int32)], axis=0)
    gt_ref[...] = jnp.concatenate([g1 / tot, g2 / tot, jnp.zeros((6, rows), F32)], axis=0)


def _router(x, mt, g, router_w, router_b):
    wt = router_w.T
    wh = wt.astype(BF16)
    wl = (wt - wh.astype(F32)).astype(BF16)
    return pl.pallas_call(
        _router_kernel,
        grid=(N_SEG,),
        in_specs=[
            pl.BlockSpec((SEG, D_MODEL), lambda i: (i, 0)),
            pl.BlockSpec((1, 6, D_MODEL), lambda i: (i, 0, 0)),
            pl.BlockSpec((1, D_MODEL), lambda i: (0, 0)),
            pl.BlockSpec((N_EXPERTS, D_MODEL), lambda i: (0, 0)),
            pl.BlockSpec((N_EXPERTS, D_MODEL), lambda i: (0, 0)),
            pl.BlockSpec((N_EXPERTS, 1), lambda i: (0, 0)),
        ],
        out_specs=[
            pl.BlockSpec((SEG, D_MODEL), lambda i: (i, 0)),
            pl.BlockSpec((8, SEG), lambda i: (0, i)),
            pl.BlockSpec((8, SEG), lambda i: (0, i)),
        ],
        out_shape=[
            jax.ShapeDtypeStruct((N_TOK, D_MODEL), F32),
            jax.ShapeDtypeStruct((8, N_TOK), jnp.int32),
            jax.ShapeDtypeStruct((8, N_TOK), F32),
        ],
        compiler_params=_cparams(("arbitrary",), 48),
        name="ffn_norm_router",
    )(x, mt, g, wh, wl, router_b.reshape(N_EXPERTS, 1))


def _expert_changed(te_ref, i):
    return jnp.logical_or(i == 0, te_ref[i] != te_ref[jnp.maximum(i - 1, 0)])


def _moe_up_kernel(te_ref, na_ref, xs_ref, wg_ref, wu_ref, hid_ref, wgb_ref, wub_ref):
    i = pl.program_id(0)

    @pl.when(_expert_changed(te_ref, i))
    def _():
        wgb_ref[...] = wg_ref[...].astype(BF16)
        wub_ref[...] = wu_ref[...].astype(BF16)

    @pl.when(i < na_ref[0])
    def _():
        x = xs_ref[...].astype(BF16)
        g = _dot(x, wgb_ref[...])
        u = _dot(x, wub_ref[...])
        hid_ref[...] = (g * _sigmoid(g) * u).astype(BF16)

    @pl.when(i >= na_ref[0])
    def _():
        hid_ref[...] = jnp.zeros_like(hid_ref)


def _moe_down_kernel(te_ref, na_ref, hid_ref, wd_ref, ys_ref, wdb_ref):
    i = pl.program_id(0)

    @pl.when(_expert_changed(te_ref, i))
    def _():
        wdb_ref[...] = wd_ref[...].astype(BF16)

    @pl.when(i < na_ref[0])
    def _():
        ys_ref[...] = _dot(hid_ref[...], wdb_ref[...])

    @pl.when(i >= na_ref[0])
    def _():
        ys_ref[...] = jnp.zeros_like(ys_ref)


def _moe_experts(xs, tile_expert, n_active, layer, w_gate, w_up, w_down):
    hid = pl.pallas_call(
        _moe_up_kernel,
        grid_spec=pltpu.PrefetchScalarGridSpec(
            num_scalar_prefetch=2,
            grid=(MOE_TILES,),
            in_specs=[
                pl.BlockSpec((MOE_TM, D_MODEL), lambda i, te, na: (i, 0)),
                pl.BlockSpec((None, None, D_MODEL, D_EXPERT), lambda i, te, na: (layer, te[i], 0, 0)),
                pl.BlockSpec((None, None, D_MODEL, D_EXPERT), lambda i, te, na: (layer, te[i], 0, 0)),
            ],
            out_specs=pl.BlockSpec((MOE_TM, D_EXPERT), lambda i, te, na: (i, 0)),
            scratch_shapes=[pltpu.VMEM((D_MODEL, D_EXPERT), BF16), pltpu.VMEM((D_MODEL, D_EXPERT), BF16)],
        ),
        out_shape=jax.ShapeDtypeStruct((MOE_ROWS, D_EXPERT), BF16),
        compiler_params=_cparams(("arbitrary",), 52),
        name="moe_up",
    )(tile_expert, n_active, xs, w_gate, w_up)
    return pl.pallas_call(
        _moe_down_kernel,
        grid_spec=pltpu.PrefetchScalarGridSpec(
            num_scalar_prefetch=2,
            grid=(MOE_TILES,),
            in_specs=[
                pl.BlockSpec((MOE_TM, D_EXPERT), lambda i, te, na: (i, 0)),
                pl.BlockSpec((None, None, D_EXPERT, D_MODEL), lambda i, te, na: (layer, te[i], 0, 0)),
            ],
            out_specs=pl.BlockSpec((MOE_TM, D_MODEL), lambda i, te, na: (i, 0)),
            scratch_shapes=[pltpu.VMEM((D_EXPERT, D_MODEL), BF16)],
        ),
        out_shape=jax.ShapeDtypeStruct((MOE_ROWS, D_MODEL), F32),
        compiler_params=_cparams(("arbitrary",), 40),
        name="moe_down",
    )(tile_expert, n_active, hid, w_down)


PLAN_R = 2 * N_TOK // LANES


def _plan_kernel(e_ref, pos_ref, meta_ref):
    e = e_ref[...]
    r_i = lax.broadcasted_iota(jnp.int32, (LANES, LANES), 0)
    c_i = lax.broadcasted_iota(jnp.int32, (LANES, LANES), 1)
    upper = jnp.where(r_i <= c_i, 1.0, 0.0).astype(BF16)
    r_j = lax.broadcasted_iota(jnp.int32, (PLAN_R, PLAN_R), 0)
    c_j = lax.broadcasted_iota(jnp.int32, (PLAN_R, PLAN_R), 1)
    lower = jnp.where(c_j < r_j, 1.0, 0.0).astype(BF16)
    lane = lax.broadcasted_iota(jnp.int32, (1, LANES), 1)
    pos = jnp.zeros((PLAN_R, LANES), F32)
    cnt_row = jnp.zeros((1, LANES), F32)
    off_row = jnp.zeros((1, LANES), F32)
    end_row = []
    row_off = jnp.zeros((1, 1), F32)
    tiles_done = jnp.zeros((1, 1), F32)
    for ex in range(N_EXPERTS):
        m = jnp.where(e == ex, 1.0, 0.0)
        inc = _dot(m.astype(BF16), upper)
        tot = jnp.broadcast_to(inc[:, LANES - 1:LANES], (PLAN_R, LANES))
        before = _dot(lower, tot.astype(BF16))
        cnt = jnp.sum(inc[:, LANES - 1:LANES], axis=0, keepdims=True)
        pos = pos + m * (row_off + before + inc - 1.0)
        tiles = jnp.floor((cnt + (MOE_TM - 1)) * (1.0 / MOE_TM))
        cnt_row = jnp.where(lane == ex, cnt, cnt_row)
        off_row = jnp.where(lane == ex, row_off, off_row)
        tiles_done = tiles_done + tiles
        end_row.append(tiles_done)
        row_off = row_off + tiles * MOE_TM
    n_active = tiles_done
    tid = jnp.minimum(lane.astype(F32), n_active - 1.0)
    te = jnp.zeros((1, LANES), F32)
    for ex in range(N_EXPERTS):
        te = te + jnp.where(end_row[ex] <= tid, 1.0, 0.0)
    pos_ref[...] = pos.astype(jnp.int32)
    meta = jnp.concatenate([cnt_row, off_row, te, jnp.broadcast_to(n_active, (1, LANES)),
                            jnp.zeros((4, LANES), F32)], axis=0)
    meta_ref[...] = meta.astype(jnp.int32)


def _route_plan(eidx):
    e2 = eidx[:2].reshape(PLAN_R, LANES)
    pos, meta = pl.pallas_call(
        _plan_kernel,
        out_shape=[jax.ShapeDtypeStruct((PLAN_R, LANES), jnp.int32), jax.ShapeDtypeStruct((8, LANES), jnp.int32)],
        compiler_params=pltpu.CompilerParams(vmem_limit_bytes=32 * 1024 * 1024),
        name="route_plan",
    )(e2)
    cnt, row_off = meta[0, :N_EXPERTS], meta[1, :N_EXPERTS]
    te, n_active = meta[2, :MOE_TILES], meta[3, :1]
    start = jnp.cumsum(cnt) - cnt
    order = jnp.argsort(e2.reshape(-1), stable=True).astype(jnp.int32)
    row_e = jnp.repeat(te, MOE_TM)
    rank = jnp.arange(MOE_ROWS, dtype=jnp.int32) - row_off[row_e]
    valid = rank < cnt[row_e]
    src_asg = order[jnp.clip(start[row_e] + rank, 0, 2 * N_TOK - 1)]
    src_tok = jnp.where(valid, src_asg % N_TOK, 0)
    return src_tok, te, n_active, pos.reshape(-1)


def _combine_kernel(x_ref, y0_ref, y1_ref, gt_ref, mt_ref, fg_ref, o_ref, *, final):
    gt = gt_ref[...]
    moe = y0_ref[...] * gt[:, 0:1] + y1_ref[...] * gt[:, 1:2]
    x = x_ref[...] + mt_ref[0, 5:6, :] * moe
    if final:
        x = x * lax.rsqrt(jnp.mean(x * x, axis=-1, keepdims=True) + EPS) * fg_ref[...]
    o_ref[...] = x


def _combine(x, ysel, gates_t, mt, final_g, final):
    tm = 256
    return pl.pallas_call(
        functools.partial(_combine_kernel, final=final),
        grid=(N_TOK // tm,),
        in_specs=[
            pl.BlockSpec((tm, D_MODEL), lambda i: (i, 0)),
            pl.BlockSpec((None, tm, D_MODEL), lambda i: (0, i, 0)),
            pl.BlockSpec((None, tm, D_MODEL), lambda i: (1, i, 0)),
            pl.BlockSpec((tm, 8), lambda i: (i, 0)),
            pl.BlockSpec((1, 6, D_MODEL), lambda i: (i // (SEG // tm), 0, 0)),
            pl.BlockSpec((1, D_MODEL), lambda i: (0, 0)),
        ],
        out_specs=pl.BlockSpec((tm, D_MODEL), lambda i: (i, 0)),
        out_shape=jax.ShapeDtypeStruct((N_TOK, D_MODEL), F32),
        compiler_params=_cparams(("arbitrary",), 40),
        name="moe_combine",
    )(x, ysel, ysel, gates_t, mt, final_g)


def _rope_tables(dim):
    half = dim // 2
    inv = ROPE_THETA ** (-jnp.arange(0, half, 2, dtype=F32) / half)
    t = jnp.arange(DEC_SEQ)
    ang_r = (t // GRID_W).astype(F32)[:, None] * inv[None, :]
    ang_c = (t % GRID_W).astype(F32)[:, None] * inv[None, :]
    ang = jnp.concatenate([ang_r, ang_r, ang_c, ang_c], axis=-1)
    cos, sin = jnp.cos(ang), jnp.sin(ang)
    reps = LANES // dim
    cos = jnp.tile(cos, (1, reps))
    sin = jnp.tile(sin, (1, reps))
    sh = dim // 4
    second = (np.arange(LANES) % (2 * sh)) >= sh
    sp = jnp.where(second[None, :], sin, 0.0)
    sm = jnp.where(second[None, :], 0.0, -sin)
    return cos, sp, sm


def _neighbourhood_bias(rpb):
    rows = DEC_SEQ // GRID_W
    kh = min(NA_ROWS, rows)
    r = np.arange(rows)
    r0 = np.clip(r - kh // 2, 0, rows - kh)
    kr = np.arange(rows)
    row_ok = (kr[None, :] >= r0[:, None]) & (kr[None, :] < r0[:, None] + kh)
    dr_idx = np.clip(kr[None, :] - r[:, None] + NA_ROWS - 1, 0, 2 * NA_ROWS - 2)
    c = np.arange(GRID_W)
    ws = np.clip(c - NA_COLS // 2, 0, GRID_W - NA_COLS)
    kc = np.arange(GRID_W)
    col_ok = (kc[None, :] >= ws[:, None]) & (kc[None, :] < ws[:, None] + NA_COLS)
    dc_idx = np.clip(kc[None, :] - c[:, None], -(NA_COLS - 1), NA_COLS - 1) + NA_COLS - 1
    sel_r = ((np.arange(2 * NA_ROWS - 1)[:, None, None] == dr_idx[None]) & row_ok[None]).astype(np.float32)
    sel_c = ((np.arange(2 * NA_COLS - 1)[:, None, None] == dc_idx[None]) & col_ok[None]).astype(np.float32)
    t = jnp.einsum("hab,ark->hrkb", rpb.astype(F32), sel_r, precision=lax.Precision.HIGHEST)
    b = jnp.einsum("hrkb,bcx->hrckx", t, sel_c, precision=lax.Precision.HIGHEST)
    ok = row_ok[:, None, :, None] & col_ok[None, :, None, :]
    return jnp.where(ok[None], b, NEG).reshape(N_HEADS, DEC_SEQ, DEC_SEQ)


def _even_layer(x, mt, l, i, rope64, cache_a_k, cache_a_v, cache_b_ckv, cache_b_krope, norm_mix_g, ev_w_in,
                ev_lambda, ev_subln_g, ev_q_norm_g, ev_kv_norm_g, ev_w_uq, ev_w_ukv, ev_w_out):
    w = ev_w_in[i]
    kr = w[:, 4352:4416]
    z = jnp.zeros_like(kr)
    w_in = jnp.concatenate([w[:, :3840], kr, z, z, kr, w[:, 3840:4352]], axis=1).astype(BF16)
    all_g = (0, 1, 2, 3)
    in_kw = dict(prologue="modulate", g=norm_mix_g[l][None, :], norm_tile=8, norm_g=ev_kv_norm_g[i][None, :])
    head_src = lambda t0: tuple((t0 + t, gq, 4 * t + gq) for t in range(2) for gq in range(4))
    proj_p, st_ak, st_av, st_ckv, st_kr = _projection(
        x, 0, N_SEG_P, D_MODEL, 0, D_MODEL, w_in, mt=mt[:N_SEG_P], name="even_in_proj_ctx",
        states=((N_HEADS, head_src(2)), (N_HEADS, head_src(4)), (0, ((8, 0, B_KV_RANK),)), (0, ((7, 256, B_ROPE_DIM),))),
        **in_kw)
    proj_l = _projection(x, N_SEG_P, DEC_BATCH, D_MODEL, 0, D_MODEL, w_in, mt=mt[N_SEG_P:], rope=rope64,
                         sh=A_QK_DIM // 4, rope_groups={0: all_g, 1: all_g, 2: all_g, 3: all_g, 7: (2, 3)},
                         name="even_in_proj_lat", **in_kw)
    wq = ev_w_uq[i].reshape(B_Q_RANK, N_HEADS, HEAD_DIM + B_ROPE_DIM)
    w_uq = jnp.concatenate([wq[:, :, :HEAD_DIM].reshape(B_Q_RANK, -1), wq[:, :, HEAD_DIM:].reshape(B_Q_RANK, -1)],
                           axis=1).astype(BF16)
    q_kw = dict(prologue="rmsnorm", g=ev_q_norm_g[i][None, :], out_dtype=BF16)
    bq_p = _projection(proj_p, 0, N_SEG_P, 1024, 3, B_Q_RANK, w_uq, name="mla_q_up_ctx", **q_kw)
    bq_l = _projection(proj_l, 0, DEC_BATCH, 1024, 3, B_Q_RANK, w_uq, rope=rope64, sh=B_ROPE_DIM // 4,
                       rope_groups={2: all_g}, name="mla_q_up_lat", **q_kw)
    w_ukv = ev_w_ukv[i].astype(BF16)
    kv_kw = dict(prologue="cast", out_dtype=BF16)
    kv_p = _projection(proj_p, 0, N_SEG_P, 512, 8, B_KV_RANK, w_ukv, name="mla_kv_up_ctx", **kv_kw)
    kv_l = _projection(proj_l, 0, DEC_BATCH, 512, 8, B_KV_RANK, w_ukv, name="mla_kv_up_lat", **kv_kw)
    kv_ctx = _projection(cache_b_ckv[:, i].reshape(DEC_BATCH * PAST_LEN, B_KV_RANK), 0, 1, 512, 0, B_KV_RANK, w_ukv,
                         name="mla_kv_up_cache", **kv_kw)
    kv_ctx = kv_ctx.reshape(DEC_BATCH, PAST_LEN, 2 * N_HEADS * HEAD_DIM)
    krc = cache_b_krope[:, i]
    zc = jnp.zeros_like(krc)
    kr_ctx = jnp.concatenate([krc, zc, zc, krc], axis=-1)
    ak_ctx = cache_a_k[:, i].reshape(DEC_BATCH, PAST_LEN, N_HEADS * HEAD_DIM)
    av_ctx = cache_a_v[:, i].reshape(DEC_BATCH, PAST_LEN, N_HEADS * HEAD_DIM)
    lam_init = 0.8 - 0.6 * math.exp(-0.3 * l)
    o = jnp.zeros((N_TOK, D_MODEL), BF16)
    for latent, proj, b_q, kv in ((False, proj_p, bq_p, kv_p), (True, proj_l, bq_l, kv_l)):
        tag = "lat" if latent else "ctx"
        o = _attention(kind="diff", latent=latent, q_parts=[(proj, lambda h: h)], ko_parts=[(proj, lambda h: 8 + h)],
                       vo=(proj, lambda h: 16 + h), kc_parts=[(ak_ctx, lambda h: h)], vc=(av_ctx, lambda h: h),
                       lam=ev_lambda[i], subln=ev_subln_g[i][None, :], o_arr=o, o_blk=lambda h: h,
                       scale=A_QK_DIM ** -0.5, lam_init=lam_init, name="diff_attn_" + tag)
        o = _attention(kind="mla", latent=latent, q_parts=[(b_q, lambda h: h), (b_q, lambda h: 8 + h // 2)],
                       ko_parts=[(kv, lambda h: 2 * h), (proj, lambda h: 30 + h % 2)], vo=(kv, lambda h: 2 * h + 1),
                       kc_parts=[(kv_ctx, lambda h: 2 * h), (kr_ctx, lambda h: h % 2)], vc=(kv_ctx, lambda h: 2 * h + 1),
                       o_arr=o, o_blk=lambda h: 8 + h, scale=(HEAD_DIM + B_ROPE_DIM) ** -0.5, name="mla_attn_" + tag)
    y = _out_projection(o, ev_w_out[i].astype(BF16), x, mt)
    states = (st_ak.reshape(BATCH, SEQ, N_HEADS, HEAD_DIM), st_av.reshape(BATCH, SEQ, N_HEADS, HEAD_DIM),
              st_ckv.reshape(BATCH, SEQ, B_KV_RANK), st_kr.reshape(BATCH, SEQ, B_ROPE_DIM))
    return y, states


def _odd_layer(x, mt, l, i, rope128, cache_c_k, cache_c_v, cache_d_k, cache_d_v, norm_mix_g, od_w_in, od_rpb,
               od_sink, od_w_out):
    w_in = od_w_in[i].astype(BF16)
    all_g = (0, 1, 2, 3)
    in_kw = dict(prologue="modulate", g=norm_mix_g[l][None, :])
    head_src = lambda t0: tuple((t0 + t, gq, 4 * t + gq) for t in range(2) for gq in range(4))
    proj_p, st_ck, st_cv, st_dk, st_dv = _projection(
        x, 0, N_SEG_P, D_MODEL, 0, D_MODEL, w_in, mt=mt[:N_SEG_P], name="odd_in_proj_ctx",
        states=((N_HEADS, head_src(2)), (N_HEADS, head_src(4)), (D_KV_HEADS, ((8, 0, 0), (8, 1, 1))),
                (D_KV_HEADS, ((8, 2, 0), (8, 3, 1)))), **in_kw)
    proj_l = _projection(x, N_SEG_P, DEC_BATCH, D_MODEL, 0, D_MODEL, w_in, mt=mt[N_SEG_P:], rope=rope128,
                         sh=HEAD_DIM // 4, rope_groups={6: all_g, 7: all_g, 8: (0, 1)}, name="odd_in_proj_lat", **in_kw)
    ck_ctx = cache_c_k[:, i].reshape(DEC_BATCH, PAST_LEN, N_HEADS * HEAD_DIM)
    cv_ctx = cache_c_v[:, i].reshape(DEC_BATCH, PAST_LEN, N_HEADS * HEAD_DIM)
    dk_ctx = cache_d_k[:, i].reshape(DEC_BATCH, PAST_LEN, D_KV_HEADS * HEAD_DIM)
    dv_ctx = cache_d_v[:, i].reshape(DEC_BATCH, PAST_LEN, D_KV_HEADS * HEAD_DIM)
    bias = _neighbourhood_bias(od_rpb[i])
    sink = od_sink[i].astype(F32)
    o = jnp.zeros((N_TOK, D_MODEL), BF16)
    for latent, proj in ((False, proj_p), (True, proj_l)):
        tag = "lat" if latent else "ctx"
        o = _attention(kind="plain", latent=latent, q_parts=[(proj, lambda h: h)], ko_parts=[(proj, lambda h: 8 + h)],
                       vo=(proj, lambda h: 16 + h), kc_parts=[(ck_ctx, lambda h: h)], vc=(cv_ctx, lambda h: h),
                       bias=bias if latent else None, o_arr=o, o_blk=lambda h: h, scale=HEAD_DIM ** -0.5,
                       name="nbr_attn_" + tag)
        o = _attention(kind="plain", latent=latent, q_parts=[(proj, lambda h: 24 + h)],
                       ko_parts=[(proj, lambda h: 32 + h // D_GROUP)], vo=(proj, lambda h: 34 + h // D_GROUP),
                       kc_parts=[(dk_ctx, lambda h: h // D_GROUP)], vc=(dv_ctx, lambda h: h // D_GROUP), sink=sink,
                       o_arr=o, o_blk=lambda h: 8 + h, scale=HEAD_DIM ** -0.5, has_band=latent,
                       name="win_attn_" + tag)
    y = _out_projection(o, od_w_out[i].astype(BF16), x, mt)
    states = (st_ck.reshape(BATCH, SEQ, N_HEADS, HEAD_DIM), st_cv.reshape(BATCH, SEQ, N_HEADS, HEAD_DIM),
              st_dk.reshape(BATCH, SEQ, D_KV_HEADS, HEAD_DIM), st_dv.reshape(BATCH, SEQ, D_KV_HEADS, HEAD_DIM))
    return y, states


def _moe_layer(x, mt, g, router_w, router_b, layer, w_gate, w_up, w_down, final_g, final):
    h2, eidx, gates = _router(x, mt, g, router_w, router_b)
    src_tok, te, n_active, pos = _route_plan(eidx)
    xs = h2.at[src_tok].get(mode="promise_in_bounds")
    ys = _moe_experts(xs, te, n_active, layer, w_gate, w_up, w_down)
    ysel = ys.at[pos].get(mode="promise_in_bounds").reshape(2, N_TOK, D_MODEL)
    return _combine(x, ysel, gates.T, mt, final_g, final)


def kernel(x_prompt, x_sample, cache_a_k, cache_a_v, cache_b_ckv, cache_b_krope, cache_c_k, cache_c_v, cache_d_k, cache_d_v, c, c_ctx, w_ada, b_ada, norm_mix_g, norm_ffn_g, ev_w_in, ev_lambda, ev_subln_g, ev_q_norm_g, ev_kv_norm_g, ev_w_uq, ev_w_ukv, ev_w_out, od_w_in, od_rpb, od_sink, od_w_out, router_w, router_b, moe_w_gate, moe_w_up, moe_w_down, final_g):
    x = jnp.concatenate([x_prompt.reshape(N_PROMPT, D_MODEL), x_sample.reshape(DEC_BATCH * DEC_SEQ, D_MODEL)], axis=0)
    cond8 = jnp.concatenate([c_ctx[None, :], c, jnp.zeros((3, D_MODEL), F32)], axis=0)
    mod = _adaln(cond8, w_ada, b_ada)
    seg_row = np.array([0] * N_SEG_P + [1 + b for b in range(DEC_BATCH)])
    mt_all = mod[:, seg_row].reshape(DEPTH, N_SEG, 6, D_MODEL)
    rope64 = _rope_tables(A_QK_DIM)
    rope128 = _rope_tables(HEAD_DIM)
    even_states, odd_states = [], []
    for l in range(DEPTH):
        i = l // 2
        mt = mt_all[l]
        if l % 2 == 0:
            x, st = _even_layer(x, mt, l, i, rope64, cache_a_k, cache_a_v, cache_b_ckv, cache_b_krope, norm_mix_g,
                                ev_w_in, ev_lambda, ev_subln_g, ev_q_norm_g, ev_kv_norm_g, ev_w_uq, ev_w_ukv, ev_w_out)
            even_states.append(st)
        else:
            x, st = _odd_layer(x, mt, l, i, rope128, cache_c_k, cache_c_v, cache_d_k, cache_d_v, norm_mix_g,
                               od_w_in, od_rpb, od_sink, od_w_out)
            odd_states.append(st)
        x = _moe_layer(x, mt, norm_ffn_g[l][None, :], router_w, router_b, l, moe_w_gate, moe_w_up, moe_w_down,
                       final_g[None, :], final=(l == DEPTH - 1))
    y_prompt = x[:N_PROMPT].reshape(BATCH, SEQ, D_MODEL)
    y_sample = x[N_PROMPT:].reshape(DEC_BATCH, DEC_SEQ, D_MODEL)
    new_even = tuple(jnp.stack([st[k] for st in even_states], axis=1) for k in range(4))
    new_odd = tuple(jnp.stack([st[k] for st in odd_states], axis=1) for k in range(4))
    return (y_prompt, y_sample) + new_even + new_odd
```

```python
import functools
import math

import numpy as np
import jax
import jax.numpy as jnp
from jax import lax
from jax.experimental import pallas as pl
from jax.experimental.pallas import tpu as pltpu

D_MODEL = 2048
BATCH = 32
SEQ = 256
DEPTH = 2
DEC_BATCH = 4
DEC_SEQ = 1024
PAST_LEN = 256
GRID_W = 64
HEAD_DIM = 128
N_HEADS = 8
A_QK_DIM = 64
B_Q_RANK = 768
B_KV_RANK = 512
B_ROPE_DIM = 64
NA_ROWS = 8
NA_COLS = 16
D_KV_HEADS = 2
D_GROUP = 4
D_WINDOW = 128
N_EXPERTS = 16
N_GROUPS = 4
D_EXPERT = 1024
ROPE_THETA = 10000.0
EPS = 1e-6
NEG = -1e30
LOG2E = 1.4426950408889634

SEG = 1024
N_PROMPT = BATCH * SEQ
N_TOK = N_PROMPT + DEC_BATCH * DEC_SEQ
N_SEG = N_TOK // SEG
N_SEG_P = N_PROMPT // SEG
PROJ_W = 4608
TN = 512
LANES = 128
MOE_TM = 256
MOE_ROWS = 2 * N_TOK + N_EXPERTS * MOE_TM
MOE_TILES = MOE_ROWS // MOE_TM

F32 = jnp.float32
BF16 = jnp.bfloat16


def _cparams(sem, vmem_mb):
    return pltpu.CompilerParams(dimension_semantics=sem, vmem_limit_bytes=vmem_mb * 1024 * 1024)


def _dot(a, b):
    return jnp.dot(a, b, preferred_element_type=F32)


def _dot_nt(a, b):
    return lax.dot_general(a, b, (((1,), (1,)), ((), ())), preferred_element_type=F32)


def _sigmoid(x):
    return 1.0 / (1.0 + jnp.exp(-x))


def _adaln_kernel(c_ref, w_ref, b_ref, o_ref):
    c = c_ref[...]
    a = (c * _sigmoid(c)).astype(BF16)
    o_ref[...] = _dot(a, w_ref[...].astype(BF16)) + b_ref[...]


def _adaln(cond8, w_ada, b_ada):
    tn = 1024
    n = 6 * D_MODEL
    return pl.pallas_call(
        _adaln_kernel,
        grid=(DEPTH, n // tn),
        in_specs=[
            pl.BlockSpec((8, D_MODEL), lambda l, j: (0, 0)),
            pl.BlockSpec((None, D_MODEL, tn), lambda l, j: (l, 0, j)),
            pl.BlockSpec((None, 1, tn), lambda l, j: (l, 0, j)),
        ],
        out_specs=pl.BlockSpec((None, 8, tn), lambda l, j: (l, 0, j)),
        out_shape=jax.ShapeDtypeStruct((DEPTH, 8, n), F32),
        compiler_params=_cparams(("arbitrary", "arbitrary"), 40),
        name="adaln",
    )(cond8, w_ada, b_ada.reshape(DEPTH, 1, n))


def _rope(a, cos, sp, sm, sh):
    return a * cos + pltpu.roll(a, sh, 1) * sp + pltpu.roll(a, LANES - sh, 1) * sm


def _proj_kernel(*refs, prologue, k_in, sh, n_tiles, rope_groups, norm_tile, states, out_scale):
    x_ref, mt_ref, g_ref, w_ref, cos_ref, sp_ref, sm_ref, ng_ref, o_ref = refs[:9]
    st_refs = refs[9:-1]
    xn_ref = refs[-1]
    j = pl.program_id(1)

    @pl.when(j == 0)
    def _():
        x = x_ref[:, :k_in]
        if prologue == "cast":
            xn_ref[...] = x.astype(BF16)
        else:
            y = x * lax.rsqrt(jnp.mean(x * x, axis=-1, keepdims=True) + EPS) * g_ref[...]
            if prologue == "modulate":
                y = y * (1.0 + mt_ref[0, 1:2, :]) + mt_ref[0, 0:1, :]
            xn_ref[...] = y.astype(BF16)

    acc = _dot(xn_ref[...], w_ref[...])
    if out_scale != 1.0:
        acc = acc * out_scale
    n_grp = acc.shape[1] // LANES

    def treatment(t):
        acts = tuple((k, a, b) for k, st in enumerate(states) for (tt, a, b) in st[1] if tt == t)
        return (tuple(rope_groups.get(t, ())), t == norm_tile, acts)

    branches = {}
    for t in range(n_tiles):
        branches.setdefault(treatment(t), []).append(t)

    o_ref[...] = acc.astype(o_ref.dtype)
    reread = o_ref.dtype == F32

    for (rg, is_norm, acts), tiles in branches.items():
        if not rg and not is_norm and not acts:
            continue
        cond = j == tiles[0]
        for t in tiles[1:]:
            cond = jnp.logical_or(cond, j == t)

        @pl.when(cond)
        def _(rg=rg, is_norm=is_norm, acts=acts):
            vals = [(o_ref if reread else acc)[:, g * LANES:(g + 1) * LANES] for g in range(n_grp)]
            if is_norm:
                ssq = None
                for v in vals:
                    s = jnp.sum(v * v, axis=-1, keepdims=True)
                    ssq = s if ssq is None else ssq + s
                scale = lax.rsqrt(ssq * (1.0 / (n_grp * LANES)) + EPS)
                ng = ng_ref[...]
                vals = [v * scale * ng[:, g * LANES:(g + 1) * LANES] for g, v in enumerate(vals)]
            elif rg:
                cos, sp, sm = cos_ref[...], sp_ref[...], sm_ref[...]
                vals = [_rope(v, cos, sp, sm, sh) if g in rg else v for g, v in enumerate(vals)]
            if is_norm or rg:
                for g, v in enumerate(vals):
                    if is_norm or g in rg:
                        o_ref[:, g * LANES:(g + 1) * LANES] = v.astype(o_ref.dtype)
            for k, a, b in acts:
                n_heads = states[k][0]
                if n_heads:
                    st_refs[k][pl.ds(b, SEG, stride=n_heads), :] = vals[a]
                elif b % LANES == 0:
                    for q in range(b // LANES):
                        st_refs[k][:, q * LANES:(q + 1) * LANES] = vals[a // LANES + q]
                else:
                    st_refs[k][...] = vals[a // LANES][:, a % LANES:a % LANES + b]


def _projection(x, row_off, n_seg, x_blk_w, x_blk_idx, k_in, w, *, prologue, mt=None, g=None, rope=None, sh=16,
                rope_groups=None, norm_tile=None, norm_g=None, states=(), out_dtype=F32, out_scale=1.0, name):
    n = w.shape[1]
    m = n_seg * SEG
    assert n % TN == 0 and w.shape[0] == k_in
    if mt is None:
        mt = jnp.zeros((n_seg, 6, LANES), F32)
    if g is None:
        g = jnp.ones((1, k_in), F32)
    if rope is None:
        rope = tuple(jnp.zeros((8, LANES), F32) for _ in range(3))
    if norm_g is None:
        norm_g = jnp.ones((1, TN), F32)
    kern = functools.partial(_proj_kernel, prologue=prologue, k_in=k_in, sh=sh, n_tiles=n // TN,
                             rope_groups=dict(rope_groups or {}), norm_tile=norm_tile, states=tuple(states),
                             out_scale=float(out_scale))
    full2 = lambda i, j: (0, 0)
    out_shape = [jax.ShapeDtypeStruct((m, n), out_dtype)]
    out_specs = [pl.BlockSpec((SEG, TN), lambda i, j: (i, j))]
    for n_heads, src in states:
        if n_heads:
            out_shape.append(jax.ShapeDtypeStruct((m * n_heads, LANES), F32))
            out_specs.append(pl.BlockSpec((SEG * n_heads, LANES), lambda i, j: (i, 0)))
        else:
            width = src[0][2]
            out_shape.append(jax.ShapeDtypeStruct((m, width), F32))
            out_specs.append(pl.BlockSpec((SEG, width), lambda i, j: (i, 0)))
    x_kw = dict(pipeline_mode=pl.Buffered(1)) if states else {}
    res = pl.pallas_call(
        kern,
        grid=(n_seg, n // TN),
        in_specs=[
            pl.BlockSpec((SEG, x_blk_w), lambda i, j: (i + row_off, x_blk_idx), **x_kw),
            pl.BlockSpec((1, 6, mt.shape[2]), lambda i, j: (i, 0, 0)),
            pl.BlockSpec((1, k_in), full2),
            pl.BlockSpec((k_in, TN), lambda i, j: (0, j)),
            pl.BlockSpec(rope[0].shape, full2),
            pl.BlockSpec(rope[1].shape, full2),
            pl.BlockSpec(rope[2].shape, full2),
            pl.BlockSpec((1, TN), full2),
        ],
        out_specs=out_specs,
        out_shape=out_shape,
        scratch_shapes=[pltpu.VMEM((SEG, k_in), BF16)],
        compiler_params=_cparams(("arbitrary", "arbitrary"), 56 if states else 48),
        name=name,
    )(x, mt, g, w, *rope, norm_g)
    return res if states else res[0]


def _attn_kernel(*refs, kind, n_parts, has_ctx, has_bias, has_band, has_sink, nb, s_len, tq, q_scale, lam_init,
                 own_win):
    it = iter(refs)
    q_refs = [next(it) for _ in range(n_parts)]
    ko_refs = [next(it) for _ in range(n_parts)]
    vo_ref = next(it)
    kc_refs, vc_ref = [], None
    if has_ctx:
        kc_refs = [next(it) for _ in range(n_parts)]
        vc_ref = next(it)
    bias_ref = next(it) if has_bias else None
    lam_ref = sg_ref = None
    if kind == "diff":
        lam_ref = next(it)
        sg_ref = next(it)
    sink_ref = next(it) if has_sink else None
    o_ref = next(it)

    sink2 = sink_ref[pl.program_id(1)] * LOG2E if has_sink else None
    lane = lax.broadcasted_iota(jnp.int32, (1, LANES), 1)

    def softmax_pv(s_list, v_list):
        m = None
        for s in s_list:
            mm = jnp.max(s, axis=-1, keepdims=True)
            m = mm if m is None else jnp.maximum(m, mm)
        if sink2 is not None:
            m = jnp.maximum(m, sink2)
        l = jnp.exp2(sink2 - m) if sink2 is not None else None
        o = None
        for s, v in zip(s_list, v_list):
            e = jnp.exp2(s - m)
            ss = jnp.sum(e, axis=-1, keepdims=True)
            l = ss if l is None else l + ss
            pv = _dot(e.astype(BF16), v)
            o = pv if o is None else o + pv
        return o, l

    for bi in range(nb):
        qrows = pl.ds(bi * tq, tq)
        q0 = pl.program_id(2) * tq
        if own_win is None:
            krows, k0 = pl.ds(bi * s_len, s_len), 0
            n_own = s_len
        else:
            k0 = pl.multiple_of(jnp.clip(q0 - D_WINDOW, 0, s_len - own_win), LANES)
            krows = pl.ds(k0, own_win)
            n_own = own_win
        k_srcs, v_srcs, is_own = [], [], []
        if has_ctx:
            k_srcs.append([r[...].astype(BF16) for r in kc_refs])
            v_srcs.append(vc_ref[...].astype(BF16))
            is_own.append(False)
        k_srcs.append([r[krows, :].astype(BF16) for r in ko_refs])
        v_srcs.append(vo_ref[krows, :].astype(BF16))
        is_own.append(True)

        def mask_own(s):
            if has_bias:
                s = s + bias_ref[...]
            if has_band:
                qpos = q0 + lax.broadcasted_iota(jnp.int32, (tq, 1), 0)
                kpos = k0 + lax.broadcasted_iota(jnp.int32, (1, n_own), 1)
                s = jnp.where(jnp.abs(kpos - qpos) <= D_WINDOW, s, NEG)
            return s

        if kind == "diff":
            q = q_refs[0][qrows, :] * q_scale
            q1 = jnp.where(lane < A_QK_DIM, q, 0.0).astype(BF16)
            q2 = jnp.where(lane < A_QK_DIM, 0.0, q).astype(BF16)
            o1, l1 = softmax_pv([_dot_nt(q1, ks[0]) for ks in k_srcs], v_srcs)
            o2, l2 = softmax_pv([_dot_nt(q2, ks[0]) for ks in k_srcs], v_srcs)
            lv = lam_ref[...]
            lam = (jnp.exp(jnp.sum(lv[0:1] * lv[1:2], axis=-1, keepdims=True))
                   - jnp.exp(jnp.sum(lv[2:3] * lv[3:4], axis=-1, keepdims=True)) + lam_init)
            o = o1 * (1.0 / l1) - o2 * (lam / l2)
            o = o * lax.rsqrt(jnp.mean(o * o, axis=-1, keepdims=True) + EPS) * sg_ref[...] * (1.0 - lam_init)
        else:
            qs = [r[qrows, :] for r in q_refs]
            qs = [(qp * q_scale).astype(BF16) if qp.dtype == F32 else qp for qp in qs]
            s_list = []
            for ks, own in zip(k_srcs, is_own):
                s = None
                for qp, kp in zip(qs, ks):
                    d = _dot_nt(qp, kp)
                    s = d if s is None else s + d
                s_list.append(mask_own(s) if own else s)
            o, l = softmax_pv(s_list, v_srcs)
            o = o * (1.0 / l)
        o_ref[qrows, :] = o.astype(o_ref.dtype)


def _attention(*, kind, latent, q_parts, ko_parts, vo, kc_parts=(), vc=None, bias=None, lam=None, subln=None,
               sink=None, o_arr, o_blk, scale, lam_init=0.0, has_band=False, tq_lat=512, name):
    n_parts = len(q_parts)
    has_ctx = latent
    if latent:
        tq, s_len, nb = tq_lat, DEC_SEQ, 1
        grid = (DEC_BATCH, N_HEADS, DEC_SEQ // tq)
        qpb = DEC_SEQ // tq
        q_row = lambda b, h, t: b * qpb + t
        o_row = lambda b, h, t: N_PROMPT // tq + b * qpb + t
        k_row = lambda b, h, t: b
        qblk = tq
    else:
        tq, s_len, nb = SEQ, SEQ, SEG // SEQ
        grid = (N_SEG_P, N_HEADS, 1)
        q_row = o_row = k_row = lambda b, h, t: b
        qblk = SEG
    args, specs = [], []

    def add(arr, shape, imap, **kw):
        args.append(arr)
        specs.append(pl.BlockSpec(shape, imap, **kw))

    for arr, f in q_parts:
        add(arr, (qblk, LANES), lambda b, h, t, f=f: (q_row(b, h, t), f(h)))
    for arr, f in list(ko_parts) + [vo]:
        add(arr, (SEG, LANES), lambda b, h, t, f=f: (k_row(b, h, t), f(h)))
    if has_ctx:
        for arr, f in list(kc_parts) + [vc]:
            add(arr, (None, PAST_LEN, LANES), lambda b, h, t, f=f: (b, 0, f(h)))
    if bias is not None:
        add(bias, (None, tq, DEC_SEQ), lambda b, h, t: (h, t, 0))
    if kind == "diff":
        add(lam, lam.shape, lambda b, h, t: (0, 0))
        add(subln, subln.shape, lambda b, h, t: (0, 0))
    if sink is not None:
        args.append(sink)
        specs.append(pl.BlockSpec(memory_space=pltpu.SMEM))
    n_in = len(args)
    args.append(o_arr)
    specs.append(pl.BlockSpec(memory_space=pl.ANY))
    kern = functools.partial(_attn_kernel_aliased, kind=kind, n_parts=n_parts, has_ctx=has_ctx,
                             has_bias=bias is not None, has_band=has_band, has_sink=sink is not None,
                             nb=nb, s_len=s_len, tq=tq, q_scale=scale * LOG2E, lam_init=lam_init,
                             own_win=(tq + 2 * D_WINDOW) if has_band else None)
    return pl.pallas_call(
        kern,
        grid=grid,
        in_specs=specs,
        out_specs=pl.BlockSpec((qblk, LANES), lambda b, h, t: (o_row(b, h, t), o_blk(h))),
        out_shape=jax.ShapeDtypeStruct(o_arr.shape, o_arr.dtype),
        input_output_aliases={n_in: 0},
        compiler_params=_cparams(("arbitrary", "arbitrary", "arbitrary"), 48),
        name=name,
    )(*args)


def _attn_kernel_aliased(*refs, **kw):
    _attn_kernel(*refs[:-2], refs[-1], **kw)


def _outproj_kernel(o_ref, w_ref, x_ref, mt_ref, y_ref):
    y_ref[...] = x_ref[...] + mt_ref[0, 2:3, :] * _dot(o_ref[...], w_ref[...])


def _out_projection(o, w, x, mt):
    return pl.pallas_call(
        _outproj_kernel,
        grid=(N_SEG, D_MODEL // TN),
        in_specs=[
            pl.BlockSpec((SEG, D_MODEL), lambda i, j: (i, 0)),
            pl.BlockSpec((D_MODEL, TN), lambda i, j: (0, j)),
            pl.BlockSpec((SEG, TN), lambda i, j: (i, j)),
            pl.BlockSpec((1, 6, TN), lambda i, j: (i, 0, j)),
        ],
        out_specs=pl.BlockSpec((SEG, TN), lambda i, j: (i, j)),
        out_shape=jax.ShapeDtypeStruct((N_TOK, D_MODEL), F32),
        compiler_params=_cparams(("arbitrary", "arbitrary"), 40),
        name="out_proj",
    )(o, w, x, mt)


def _router_kernel(x_ref, mt_ref, g_ref, wh_ref, wl_ref, rb_ref, h_ref, e_ref, gt_ref):
    x = x_ref[...]
    y = x * lax.rsqrt(jnp.mean(x * x, axis=-1, keepdims=True) + EPS) * g_ref[...]
    h = y * (1.0 + mt_ref[0, 4:5, :]) + mt_ref[0, 3:4, :]
    hh = h.astype(BF16)
    h_ref[...] = h
    hl = (h - hh.astype(F32)).astype(BF16)
    wh, wl = wh_ref[...], wl_ref[...]
    logits = _dot_nt(wh, hh) + _dot_nt(wh, hl) + _dot_nt(wl, hh)
    scores = _sigmoid(logits)
    sel = scores + rb_ref[...]
    per = N_EXPERTS // N_GROUPS
    sc = [scores[e:e + 1, :] for e in range(N_EXPERTS)]
    sl = [sel[e:e + 1, :] for e in range(N_EXPERTS)]
    best_g, best_v = None, None
    for gi in range(N_GROUPS):
        a, b, c, d = sl[gi * per:(gi + 1) * per]
        hi1, lo1, hi2, lo2 = jnp.maximum(a, b), jnp.minimum(a, b), jnp.maximum(c, d), jnp.minimum(c, d)
        gs = jnp.maximum(hi1, hi2) + jnp.maximum(jnp.minimum(hi1, hi2), jnp.maximum(lo1, lo2))
        if gi == 0:
            best_g, best_v = jnp.zeros_like(gs, dtype=jnp.int32), gs
        else:
            better = gs > best_v
            best_g = jnp.where(better, gi, best_g)
            best_v = jnp.where(better, gs, best_v)
    masked = [jnp.where(best_g == (e // per), sl[e], NEG) for e in range(N_EXPERTS)]
    i1, v1 = jnp.zeros_like(best_g), masked[0]
    for e in range(1, N_EXPERTS):
        better = masked[e] > v1
        i1 = jnp.where(better, e, i1)
        v1 = jnp.where(better, masked[e], v1)
    i2, v2 = None, None
    for e in range(N_EXPERTS):
        cand = jnp.where(i1 == e, -2e30, masked[e])
        if e == 0:
            i2, v2 = jnp.zeros_like(best_g), cand
        else:
            better = cand > v2
            i2 = jnp.where(better, e, i2)
            v2 = jnp.where(better, cand, v2)
    g1 = jnp.zeros_like(v1)
    g2 = jnp.zeros_like(v1)
    for e in range(N_EXPERTS):
        g1 = jnp.where(i1 == e, sc[e], g1)
        g2 = jnp.where(i2 == e, sc[e], g2)
    tot = g1 + g2
    rows = i1.shape[1]
    e_ref[...] = jnp.concatenate([i1, i2, jnp.zeros((6, rows), jnp.int32)], axis=0)
    gt_ref[...] = jnp.concatenate([g1 / tot, g2 / tot, jnp.zeros((6, rows), F32)], axis=0)


def _router(x, mt, g, router_w, router_b):
    wt = router_w.T
    wh = wt.astype(BF16)
    wl = (wt - wh.astype(F32)).astype(BF16)
    return pl.pallas_call(
        _router_kernel,
        grid=(N_SEG,),
        in_specs=[
            pl.BlockSpec((SEG, D_MODEL), lambda i: (i, 0)),
            pl.BlockSpec((1, 6, D_MODEL), lambda i: (i, 0, 0)),
            pl.BlockSpec((1, D_MODEL), lambda i: (0, 0)),
            pl.BlockSpec((N_EXPERTS, D_MODEL), lambda i: (0, 0)),
            pl.BlockSpec((N_EXPERTS, D_MODEL), lambda i: (0, 0)),
            pl.BlockSpec((N_EXPERTS, 1), lambda i: (0, 0)),
        ],
        out_specs=[
            pl.BlockSpec((SEG, D_MODEL), lambda i: (i, 0)),
            pl.BlockSpec((8, SEG), lambda i: (0, i)),
            pl.BlockSpec((8, SEG), lambda i: (0, i)),
        ],
        out_shape=[
            jax.ShapeDtypeStruct((N_TOK, D_MODEL), F32),
            jax.ShapeDtypeStruct((8, N_TOK), jnp.int32),
            jax.ShapeDtypeStruct((8, N_TOK), F32),
        ],
        compiler_params=_cparams(("arbitrary",), 48),
        name="ffn_norm_router",
    )(x, mt, g, wh, wl, router_b.reshape(N_EXPERTS, 1))


def _expert_changed(te_ref, i, t):
    return jnp.logical_or(i == 0, te_ref[t] != te_ref[jnp.maximum(t - 1, 0)])


def _moe_up_kernel(te_ref, na_ref, xs_ref, wg_ref, wu_ref, hid_ref, wgb_ref, wub_ref, *, tile0):
    i = pl.program_id(0)
    t = i + tile0

    @pl.when(_expert_changed(te_ref, i, t))
    def _():
        wgb_ref[...] = wg_ref[...].astype(BF16)
        wub_ref[...] = wu_ref[...].astype(BF16)

    @pl.when(t < na_ref[0])
    def _():
        x = xs_ref[...].astype(BF16)
        g = _dot(x, wgb_ref[...])
        u = _dot(x, wub_ref[...])
        hid_ref[...] = (g * _sigmoid(g) * u).astype(BF16)

    @pl.when(t >= na_ref[0])
    def _():
        hid_ref[...] = jnp.zeros_like(hid_ref)


def _moe_down_kernel(te_ref, na_ref, hid_ref, wd_ref, *rest, tile0):
    ys_ref, wdb_ref = rest[-2:]
    i = pl.program_id(0)
    t = i + tile0

    @pl.when(_expert_changed(te_ref, i, t))
    def _():
        wdb_ref[...] = wd_ref[...].astype(BF16)

    @pl.when(t < na_ref[0])
    def _():
        ys_ref[...] = _dot(hid_ref[...], wdb_ref[...])

    @pl.when(t >= na_ref[0])
    def _():
        ys_ref[...] = jnp.zeros_like(ys_ref)


MOE_CHUNKS = 2


def _moe_experts(h2, src_tok, tile_expert, n_active, layer, w_gate, w_up, w_down):
    tiles = MOE_TILES // MOE_CHUNKS
    rows = tiles * MOE_TM
    ys = None
    for c in range(MOE_CHUNKS):
        tile0 = c * tiles
        xs = h2.at[src_tok[c * rows:(c + 1) * rows]].get(mode="promise_in_bounds")
        hid = pl.pallas_call(
            functools.partial(_moe_up_kernel, tile0=tile0),
            grid_spec=pltpu.PrefetchScalarGridSpec(
                num_scalar_prefetch=2,
                grid=(tiles,),
                in_specs=[
                    pl.BlockSpec((MOE_TM, D_MODEL), lambda i, te, na: (i, 0)),
                    pl.BlockSpec((None, None, D_MODEL, D_EXPERT), lambda i, te, na: (layer, te[i + tile0], 0, 0)),
                    pl.BlockSpec((None, None, D_MODEL, D_EXPERT), lambda i, te, na: (layer, te[i + tile0], 0, 0)),
                ],
                out_specs=pl.BlockSpec((MOE_TM, D_EXPERT), lambda i, te, na: (i, 0)),
                scratch_shapes=[pltpu.VMEM((D_MODEL, D_EXPERT), BF16), pltpu.VMEM((D_MODEL, D_EXPERT), BF16)],
            ),
            out_shape=jax.ShapeDtypeStruct((rows, D_EXPERT), BF16),
            compiler_params=_cparams(("arbitrary",), 52),
            name="moe_up",
        )(tile_expert, n_active, xs, w_gate, w_up)
        in_specs = [
            pl.BlockSpec((MOE_TM, D_EXPERT), lambda i, te, na: (i, 0)),
            pl.BlockSpec((None, None, D_EXPERT, D_MODEL), lambda i, te, na: (layer, te[i + tile0], 0, 0)),
        ]
        args = [tile_expert, n_active, hid, w_down]
        aliases = {}
        if ys is not None:
            in_specs.append(pl.BlockSpec(memory_space=pl.ANY))
            args.append(ys)
            aliases = {4: 0}
        ys = pl.pallas_call(
            functools.partial(_moe_down_kernel, tile0=tile0),
            grid_spec=pltpu.PrefetchScalarGridSpec(
                num_scalar_prefetch=2,
                grid=(tiles,),
                in_specs=in_specs,
                out_specs=pl.BlockSpec((MOE_TM, D_MODEL), lambda i, te, na: (i + tile0, 0)),
                scratch_shapes=[pltpu.VMEM((D_EXPERT, D_MODEL), BF16)],
            ),
            out_shape=jax.ShapeDtypeStruct((MOE_ROWS, D_MODEL), F32),
            input_output_aliases=aliases,
            compiler_params=_cparams(("arbitrary",), 40),
            name="moe_down",
        )(*args)
    return ys


PLAN_R = 2 * N_TOK // LANES


def _plan_kernel(e_ref, pos_ref, meta_ref):
    e = e_ref[...]
    r_i = lax.broadcasted_iota(jnp.int32, (LANES, LANES), 0)
    c_i = lax.broadcasted_iota(jnp.int32, (LANES, LANES), 1)
    upper = jnp.where(r_i <= c_i, 1.0, 0.0).astype(BF16)
    r_j = lax.broadcasted_iota(jnp.int32, (PLAN_R, PLAN_R), 0)
    c_j = lax.broadcasted_iota(jnp.int32, (PLAN_R, PLAN_R), 1)
    lower = jnp.where(c_j < r_j, 1.0, 0.0).astype(BF16)
    lane = lax.broadcasted_iota(jnp.int32, (1, LANES), 1)
    pos = jnp.zeros((PLAN_R, LANES), F32)
    cnt_row = jnp.zeros((1, LANES), F32)
    off_row = jnp.zeros((1, LANES), F32)
    end_row = []
    row_off = jnp.zeros((1, 1), F32)
    tiles_done = jnp.zeros((1, 1), F32)
    for ex in range(N_EXPERTS):
        m = jnp.where(e == ex, 1.0, 0.0)
        inc = _dot(m.astype(BF16), upper)
        tot = jnp.broadcast_to(inc[:, LANES - 1:LANES], (PLAN_R, LANES))
        before = _dot(lower, tot.astype(BF16))
        cnt = jnp.sum(inc[:, LANES - 1:LANES], axis=0, keepdims=True)
        pos = pos + m * (row_off + before + inc - 1.0)
        tiles = jnp.floor((cnt + (MOE_TM - 1)) * (1.0 / MOE_TM))
        cnt_row = jnp.where(lane == ex, cnt, cnt_row)
        off_row = jnp.where(lane == ex, row_off, off_row)
        tiles_done = tiles_done + tiles
        end_row.append(tiles_done)
        row_off = row_off + tiles * MOE_TM
    n_active = tiles_done
    tid = jnp.minimum(lane.astype(F32), n_active - 1.0)
    te = jnp.zeros((1, LANES), F32)
    for ex in range(N_EXPERTS):
        te = te + jnp.where(end_row[ex] <= tid, 1.0, 0.0)
    pos_ref[...] = pos.astype(jnp.int32)
    meta = jnp.concatenate([cnt_row, off_row, te, jnp.broadcast_to(n_active, (1, LANES)),
                            jnp.zeros((4, LANES), F32)], axis=0)
    meta_ref[...] = meta.astype(jnp.int32)


def _route_plan(eidx):
    e2 = eidx[:2].reshape(PLAN_R, LANES)
    pos, meta = pl.pallas_call(
        _plan_kernel,
        out_shape=[jax.ShapeDtypeStruct((PLAN_R, LANES), jnp.int32), jax.ShapeDtypeStruct((8, LANES), jnp.int32)],
        compiler_params=pltpu.CompilerParams(vmem_limit_bytes=32 * 1024 * 1024),
        name="route_plan",
    )(e2)
    cnt, row_off = meta[0, :N_EXPERTS], meta[1, :N_EXPERTS]
    te, n_active = meta[2, :MOE_TILES], meta[3, :1]
    start = jnp.cumsum(cnt) - cnt
    order = jnp.argsort(e2.reshape(-1), stable=True).astype(jnp.int32)
    row_e = jnp.repeat(te, MOE_TM)
    rank = jnp.arange(MOE_ROWS, dtype=jnp.int32) - row_off[row_e]
    valid = rank < cnt[row_e]
    src_asg = order[jnp.clip(start[row_e] + rank, 0, 2 * N_TOK - 1)]
    rows = jnp.arange(MOE_ROWS, dtype=jnp.int32)
    src_tok = jnp.where(valid, src_asg % N_TOK, rows % N_TOK)
    return src_tok, te, n_active, pos.reshape(-1)


def _combine_kernel(x_ref, y0_ref, y1_ref, gt_ref, mt_ref, fg_ref, o_ref, *, final):
    gt = gt_ref[...]
    moe = y0_ref[...] * gt[:, 0:1] + y1_ref[...] * gt[:, 1:2]
    x = x_ref[...] + mt_ref[0, 5:6, :] * moe
    if final:
        x = x * lax.rsqrt(jnp.mean(x * x, axis=-1, keepdims=True) + EPS) * fg_ref[...]
    o_ref[...] = x


def _combine(x, ysel, gates_t, mt, final_g, final):
    tm = 256
    return pl.pallas_call(
        functools.partial(_combine_kernel, final=final),
        grid=(N_TOK // tm,),
        in_specs=[
            pl.BlockSpec((tm, D_MODEL), lambda i: (i, 0)),
            pl.BlockSpec((None, tm, D_MODEL), lambda i: (0, i, 0)),
            pl.BlockSpec((None, tm, D_MODEL), lambda i: (1, i, 0)),
            pl.BlockSpec((tm, 8), lambda i: (i, 0)),
            pl.BlockSpec((1, 6, D_MODEL), lambda i: (i // (SEG // tm), 0, 0)),
            pl.BlockSpec((1, D_MODEL), lambda i: (0, 0)),
        ],
        out_specs=pl.BlockSpec((tm, D_MODEL), lambda i: (i, 0)),
        out_shape=jax.ShapeDtypeStruct((N_TOK, D_MODEL), F32),
        compiler_params=_cparams(("arbitrary",), 40),
        name="moe_combine",
    )(x, ysel, ysel, gates_t, mt, final_g)


def _rope_tables(dim):
    half = dim // 2
    inv = ROPE_THETA ** (-jnp.arange(0, half, 2, dtype=F32) / half)
    t = jnp.arange(DEC_SEQ)
    ang_r = (t // GRID_W).astype(F32)[:, None] * inv[None, :]
    ang_c = (t % GRID_W).astype(F32)[:, None] * inv[None, :]
    ang = jnp.concatenate([ang_r, ang_r, ang_c, ang_c], axis=-1)
    cos, sin = jnp.cos(ang), jnp.sin(ang)
    reps = LANES // dim
    cos = jnp.tile(cos, (1, reps))
    sin = jnp.tile(sin, (1, reps))
    sh = dim // 4
    second = (np.arange(LANES) % (2 * sh)) >= sh
    sp = jnp.where(second[None, :], sin, 0.0)
    sm = jnp.where(second[None, :], 0.0, -sin)
    return cos, sp, sm


def _neighbourhood_bias(rpb):
    rows = DEC_SEQ // GRID_W
    kh = min(NA_ROWS, rows)
    r = np.arange(rows)
    r0 = np.clip(r - kh // 2, 0, rows - kh)
    kr = np.arange(rows)
    row_ok = (kr[None, :] >= r0[:, None]) & (kr[None, :] < r0[:, None] + kh)
    dr_idx = np.clip(kr[None, :] - r[:, None] + NA_ROWS - 1, 0, 2 * NA_ROWS - 2)
    c = np.arange(GRID_W)
    ws = np.clip(c - NA_COLS // 2, 0, GRID_W - NA_COLS)
    kc = np.arange(GRID_W)
    col_ok = (kc[None, :] >= ws[:, None]) & (kc[None, :] < ws[:, None] + NA_COLS)
    dc_idx = np.clip(kc[None, :] - c[:, None], -(NA_COLS - 1), NA_COLS - 1) + NA_COLS - 1
    sel_r = ((np.arange(2 * NA_ROWS - 1)[:, None, None] == dr_idx[None]) & row_ok[None]).astype(np.float32)
    sel_c = ((np.arange(2 * NA_COLS - 1)[:, None, None] == dc_idx[None]) & col_ok[None]).astype(np.float32)
    t = jnp.einsum("hab,ark->hrkb", rpb.astype(F32), sel_r, precision=lax.Precision.HIGHEST)
    b = jnp.einsum("hrkb,bcx->hrckx", t, sel_c, precision=lax.Precision.HIGHEST)
    ok = row_ok[:, None, :, None] & col_ok[None, :, None, :]
    return jnp.where(ok[None], b * LOG2E, NEG).reshape(N_HEADS, DEC_SEQ, DEC_SEQ)


def _even_layer(x, mt, l, i, rope64, cache_a_k, cache_a_v, cache_b_ckv, cache_b_krope, norm_mix_g, ev_w_in,
                ev_lambda, ev_subln_g, ev_q_norm_g, ev_kv_norm_g, ev_w_uq, ev_w_ukv, ev_w_out):
    w = ev_w_in[i]
    kr = w[:, 4352:4416]
    z = jnp.zeros_like(kr)
    w_in = jnp.concatenate([w[:, :3840], kr, z, z, kr, w[:, 3840:4352]], axis=1).astype(BF16)
    all_g = (0, 1, 2, 3)
    in_kw = dict(prologue="modulate", g=norm_mix_g[l][None, :], norm_tile=8, norm_g=ev_kv_norm_g[i][None, :])
    head_src = lambda t0: tuple((t0 + t, gq, 4 * t + gq) for t in range(2) for gq in range(4))
    proj_p, st_ak, st_av, st_ckv, st_kr = _projection(
        x, 0, N_SEG_P, D_MODEL, 0, D_MODEL, w_in, mt=mt[:N_SEG_P], name="even_in_proj_ctx",
        states=((N_HEADS, head_src(2)), (N_HEADS, head_src(4)), (0, ((8, 0, B_KV_RANK),)), (0, ((7, 256, B_ROPE_DIM),))),
        **in_kw)
    proj_l = _projection(x, N_SEG_P, DEC_BATCH, D_MODEL, 0, D_MODEL, w_in, mt=mt[N_SEG_P:], rope=rope64,
                         sh=A_QK_DIM // 4, rope_groups={0: all_g, 1: all_g, 2: all_g, 3: all_g, 7: (2, 3)},
                         name="even_in_proj_lat", **in_kw)
    wq = ev_w_uq[i].reshape(B_Q_RANK, N_HEADS, HEAD_DIM + B_ROPE_DIM)
    w_uq = jnp.concatenate([wq[:, :, :HEAD_DIM].reshape(B_Q_RANK, -1), wq[:, :, HEAD_DIM:].reshape(B_Q_RANK, -1)],
                           axis=1).astype(BF16)
    mla_scale = (HEAD_DIM + B_ROPE_DIM) ** -0.5
    q_kw = dict(prologue="rmsnorm", g=ev_q_norm_g[i][None, :], out_dtype=BF16, out_scale=mla_scale * LOG2E)
    bq_p = _projection(proj_p, 0, N_SEG_P, 1024, 3, B_Q_RANK, w_uq, name="mla_q_up_ctx", **q_kw)
    bq_l = _projection(proj_l, 0, DEC_BATCH, 1024, 3, B_Q_RANK, w_uq, rope=rope64, sh=B_ROPE_DIM // 4,
                       rope_groups={2: all_g}, name="mla_q_up_lat", **q_kw)
    w_ukv = ev_w_ukv[i].astype(BF16)
    kv_kw = dict(prologue="cast", out_dtype=BF16)
    kv_p = _projection(proj_p, 0, N_SEG_P, 512, 8, B_KV_RANK, w_ukv, name="mla_kv_up_ctx", **kv_kw)
    kv_l = _projection(proj_l, 0, DEC_BATCH, 512, 8, B_KV_RANK, w_ukv, name="mla_kv_up_lat", **kv_kw)
    kv_ctx = _projection(cache_b_ckv[:, i].reshape(DEC_BATCH * PAST_LEN, B_KV_RANK), 0, 1, 512, 0, B_KV_RANK, w_ukv,
                         name="mla_kv_up_cache", **kv_kw)
    kv_ctx = kv_ctx.reshape(DEC_BATCH, PAST_LEN, 2 * N_HEADS * HEAD_DIM)
    krc = cache_b_krope[:, i]
    zc = jnp.zeros_like(krc)
    kr_ctx = jnp.concatenate([krc, zc, zc, krc], axis=-1)
    ak_ctx = cache_a_k[:, i].reshape(DEC_BATCH, PAST_LEN, N_HEADS * HEAD_DIM)
    av_ctx = cache_a_v[:, i].reshape(DEC_BATCH, PAST_LEN, N_HEADS * HEAD_DIM)
    lam_init = 0.8 - 0.6 * math.exp(-0.3 * l)
    o = jnp.zeros((N_TOK, D_MODEL), BF16)
    for latent, proj, b_q, kv in ((False, proj_p, bq_p, kv_p), (True, proj_l, bq_l, kv_l)):
        tag = "lat" if latent else "ctx"
        o = _attention(kind="diff", latent=latent, q_parts=[(proj, lambda h: h)], ko_parts=[(proj, lambda h: 8 + h)],
                       vo=(proj, lambda h: 16 + h), kc_parts=[(ak_ctx, lambda h: h)], vc=(av_ctx, lambda h: h),
                       lam=ev_lambda[i], subln=ev_subln_g[i][None, :], o_arr=o, o_blk=lambda h: h,
                       scale=A_QK_DIM ** -0.5, lam_init=lam_init, name="diff_attn_" + tag)
        o = _attention(kind="mla", latent=latent, q_parts=[(b_q, lambda h: h), (b_q, lambda h: 8 + h // 2)],
                       ko_parts=[(kv, lambda h: 2 * h), (proj, lambda h: 30 + h % 2)], vo=(kv, lambda h: 2 * h + 1),
                       kc_parts=[(kv_ctx, lambda h: 2 * h), (kr_ctx, lambda h: h % 2)], vc=(kv_ctx, lambda h: 2 * h + 1),
                       o_arr=o, o_blk=lambda h: 8 + h, scale=(HEAD_DIM + B_ROPE_DIM) ** -0.5, name="mla_attn_" + tag)
    y = _out_projection(o, ev_w_out[i].astype(BF16), x, mt)
    states = (st_ak.reshape(BATCH, SEQ, N_HEADS, HEAD_DIM), st_av.reshape(BATCH, SEQ, N_HEADS, HEAD_DIM),
              st_ckv.reshape(BATCH, SEQ, B_KV_RANK), st_kr.reshape(BATCH, SEQ, B_ROPE_DIM))
    return y, states


def _odd_layer(x, mt, l, i, rope128, cache_c_k, cache_c_v, cache_d_k, cache_d_v, norm_mix_g, od_w_in, od_rpb,
               od_sink, od_w_out):
    w_in = od_w_in[i].astype(BF16)
    all_g = (0, 1, 2, 3)
    in_kw = dict(prologue="modulate", g=norm_mix_g[l][None, :])
    head_src = lambda t0: tuple((t0 + t, gq, 4 * t + gq) for t in range(2) for gq in range(4))
    proj_p, st_ck, st_cv, st_dk, st_dv = _projection(
        x, 0, N_SEG_P, D_MODEL, 0, D_MODEL, w_in, mt=mt[:N_SEG_P], name="odd_in_proj_ctx",
        states=((N_HEADS, head_src(2)), (N_HEADS, head_src(4)), (D_KV_HEADS, ((8, 0, 0), (8, 1, 1))),
                (D_KV_HEADS, ((8, 2, 0), (8, 3, 1)))), **in_kw)
    proj_l = _projection(x, N_SEG_P, DEC_BATCH, D_MODEL, 0, D_MODEL, w_in, mt=mt[N_SEG_P:], rope=rope128,
                         sh=HEAD_DIM // 4, rope_groups={6: all_g, 7: all_g, 8: (0, 1)}, name="odd_in_proj_lat", **in_kw)
    ck_ctx = cache_c_k[:, i].reshape(DEC_BATCH, PAST_LEN, N_HEADS * HEAD_DIM)
    cv_ctx = cache_c_v[:, i].reshape(DEC_BATCH, PAST_LEN, N_HEADS * HEAD_DIM)
    dk_ctx = cache_d_k[:, i].reshape(DEC_BATCH, PAST_LEN, D_KV_HEADS * HEAD_DIM)
    dv_ctx = cache_d_v[:, i].reshape(DEC_BATCH, PAST_LEN, D_KV_HEADS * HEAD_DIM)
    bias = _neighbourhood_bias(od_rpb[i])
    sink = od_sink[i].astype(F32)
    o = jnp.zeros((N_TOK, D_MODEL), BF16)
    for latent, proj in ((False, proj_p), (True, proj_l)):
        tag = "lat" if latent else "ctx"
        o = _attention(kind="plain", latent=latent, q_parts=[(proj, lambda h: h)], ko_parts=[(proj, lambda h: 8 + h)],
                       vo=(proj, lambda h: 16 + h), kc_parts=[(ck_ctx, lambda h: h)], vc=(cv_ctx, lambda h: h),
                       bias=bias if latent else None, o_arr=o, o_blk=lambda h: h, scale=HEAD_DIM ** -0.5,
                       name="nbr_attn_" + tag)
        o = _attention(kind="plain", latent=latent, q_parts=[(proj, lambda h: 24 + h)],
                       ko_parts=[(proj, lambda h: 32 + h // D_GROUP)], vo=(proj, lambda h: 34 + h // D_GROUP),
                       kc_parts=[(dk_ctx, lambda h: h // D_GROUP)], vc=(dv_ctx, lambda h: h // D_GROUP), sink=sink,
                       o_arr=o, o_blk=lambda h: 8 + h, scale=HEAD_DIM ** -0.5, has_band=latent, tq_lat=256,
                       name="win_attn_" + tag)
    y = _out_projection(o, od_w_out[i].astype(BF16), x, mt)
    states = (st_ck.reshape(BATCH, SEQ, N_HEADS, HEAD_DIM), st_cv.reshape(BATCH, SEQ, N_HEADS, HEAD_DIM),
              st_dk.reshape(BATCH, SEQ, D_KV_HEADS, HEAD_DIM), st_dv.reshape(BATCH, SEQ, D_KV_HEADS, HEAD_DIM))
    return y, states


def _moe_layer(x, mt, g, router_w, router_b, layer, w_gate, w_up, w_down, final_g, final):
    h2, eidx, gates = _router(x, mt, g, router_w, router_b)
    src_tok, te, n_active, pos = _route_plan(eidx)
    ys = _moe_experts(h2, src_tok, te, n_active, layer, w_gate, w_up, w_down)
    ysel = ys.at[pos].get(mode="promise_in_bounds").reshape(2, N_TOK, D_MODEL)
    return _combine(x, ysel, gates.T, mt, final_g, final)


def kernel(x_prompt, x_sample, cache_a_k, cache_a_v, cache_b_ckv, cache_b_krope, cache_c_k, cache_c_v, cache_d_k, cache_d_v, c, c_ctx, w_ada, b_ada, norm_mix_g, norm_ffn_g, ev_w_in, ev_lambda, ev_subln_g, ev_q_norm_g, ev_kv_norm_g, ev_w_uq, ev_w_ukv, ev_w_out, od_w_in, od_rpb, od_sink, od_w_out, router_w, router_b, moe_w_gate, moe_w_up, moe_w_down, final_g):
    x = jnp.concatenate([x_prompt.reshape(N_PROMPT, D_MODEL), x_sample.reshape(DEC_BATCH * DEC_SEQ, D_MODEL)], axis=0)
    cond8 = jnp.concatenate([c_ctx[None, :], c, jnp.zeros((3, D_MODEL), F32)], axis=0)
    mod = _adaln(cond8, w_ada, b_ada)
    seg_row = np.array([0] * N_SEG_P + [1 + b for b in range(DEC_BATCH)])
    mt_all = mod[:, seg_row].reshape(DEPTH, N_SEG, 6, D_MODEL)
    rope64 = _rope_tables(A_QK_DIM)
    rope128 = _rope_tables(HEAD_DIM)
    even_states, odd_states = [], []
    for l in range(DEPTH):
        i = l // 2
        mt = mt_all[l]
        if l % 2 == 0:
            x, st = _even_layer(x, mt, l, i, rope64, cache_a_k, cache_a_v, cache_b_ckv, cache_b_krope, norm_mix_g,
                                ev_w_in, ev_lambda, ev_subln_g, ev_q_norm_g, ev_kv_norm_g, ev_w_uq, ev_w_ukv, ev_w_out)
            even_states.append(st)
        else:
            x, st = _odd_layer(x, mt, l, i, rope128, cache_c_k, cache_c_v, cache_d_k, cache_d_v, norm_mix_g,
                               od_w_in, od_rpb, od_sink, od_w_out)
            odd_states.append(st)
        x = _moe_layer(x, mt, norm_ffn_g[l][None, :], router_w, router_b, l, moe_w_gate, moe_w_up, moe_w_down,
                       final_g[None, :], final=(l == DEPTH - 1))
    y_prompt = x[:N_PROMPT].reshape(BATCH, SEQ, D_MODEL)
    y_sample = x[N_PROMPT:].reshape(DEC_BATCH, DEC_SEQ, D_MODEL)
    new_even = tuple(jnp.stack([st[k] for st in even_states], axis=1) for k in range(4))
    new_odd = tuple(jnp.stack([st[k] for st in odd_states], axis=1) for k in range(4))
    return (y_prompt, y_sample) + new_even + new_odd
```

```python
import functools
import math

import numpy as np
import jax
import jax.numpy as jnp
from jax import lax
from jax.experimental import pallas as pl
from jax.experimental.pallas import tpu as pltpu

D_MODEL = 2048
BATCH = 32
SEQ = 256
DEPTH = 2
DEC_BATCH = 4
DEC_SEQ = 1024
PAST_LEN = 256
GRID_W = 64
HEAD_DIM = 128
N_HEADS = 8
A_QK_DIM = 64
B_Q_RANK = 768
B_KV_RANK = 512
B_ROPE_DIM = 64
NA_ROWS = 8
NA_COLS = 16
D_KV_HEADS = 2
D_GROUP = 4
D_WINDOW = 128
N_EXPERTS = 16
N_GROUPS = 4
D_EXPERT = 1024
ROPE_THETA = 10000.0
EPS = 1e-6
NEG = -1e30
LOG2E = 1.4426950408889634

SEG = 1024
N_PROMPT = BATCH * SEQ
N_TOK = N_PROMPT + DEC_BATCH * DEC_SEQ
N_SEG = N_TOK // SEG
N_SEG_P = N_PROMPT // SEG
PROJ_W = 4608
TN = 512
LANES = 128
MOE_TM = 256
MOE_ROWS = 2 * N_TOK + N_EXPERTS * MOE_TM
MOE_TILES = MOE_ROWS // MOE_TM

F32 = jnp.float32
BF16 = jnp.bfloat16


def _cparams(sem, vmem_mb):
    return pltpu.CompilerParams(dimension_semantics=sem, vmem_limit_bytes=vmem_mb * 1024 * 1024)


def _dot(a, b):
    return jnp.dot(a, b, preferred_element_type=F32)


def _dot_nt(a, b):
    return lax.dot_general(a, b, (((1,), (1,)), ((), ())), preferred_element_type=F32)


def _sigmoid(x):
    return 1.0 / (1.0 + jnp.exp(-x))


def _pack_bf16_pair(x):
    c = x.shape[1] // 2
    bits = pltpu.bitcast(x.astype(BF16).astype(F32), jnp.int32)
    return bits[:, :c] | lax.shift_right_logical(bits[:, c:], 16)


def _unpack_bf16_pair(w):
    hi = pltpu.bitcast(w & jnp.int32(-65536), F32)
    lo = pltpu.bitcast(lax.shift_left(w, 16), F32)
    return hi, lo


def _adaln_kernel(c_ref, w_ref, b_ref, o_ref):
    c = c_ref[...]
    a = (c * _sigmoid(c)).astype(BF16)
    o_ref[...] = _dot(a, w_ref[...].astype(BF16)) + b_ref[...]


def _adaln(cond8, w_ada, b_ada):
    tn = 1024
    n = 6 * D_MODEL
    return pl.pallas_call(
        _adaln_kernel,
        grid=(DEPTH, n // tn),
        in_specs=[
            pl.BlockSpec((8, D_MODEL), lambda l, j: (0, 0)),
            pl.BlockSpec((None, D_MODEL, tn), lambda l, j: (l, 0, j)),
            pl.BlockSpec((None, 1, tn), lambda l, j: (l, 0, j)),
        ],
        out_specs=pl.BlockSpec((None, 8, tn), lambda l, j: (l, 0, j)),
        out_shape=jax.ShapeDtypeStruct((DEPTH, 8, n), F32),
        compiler_params=_cparams(("arbitrary", "arbitrary"), 40),
        name="adaln",
    )(cond8, w_ada, b_ada.reshape(DEPTH, 1, n))


def _rope(a, cos, sp, sm, sh):
    return a * cos + pltpu.roll(a, sh, 1) * sp + pltpu.roll(a, LANES - sh, 1) * sm


def _proj_kernel(*refs, prologue, k_in, sh, n_tiles, rope_groups, norm_tile, states, out_scale):
    x_ref, mt_ref, g_ref, w_ref, cos_ref, sp_ref, sm_ref, ng_ref, o_ref = refs[:9]
    st_refs = refs[9:-1]
    xn_ref = refs[-1]
    j = pl.program_id(1)

    @pl.when(j == 0)
    def _():
        x = x_ref[:, :k_in]
        if prologue == "cast":
            xn_ref[...] = x.astype(BF16)
        else:
            y = x * lax.rsqrt(jnp.mean(x * x, axis=-1, keepdims=True) + EPS) * g_ref[...]
            if prologue == "modulate":
                y = y * (1.0 + mt_ref[0, 1:2, :]) + mt_ref[0, 0:1, :]
            xn_ref[...] = y.astype(BF16)

    acc = _dot(xn_ref[...], w_ref[...])
    if out_scale != 1.0:
        acc = acc * out_scale
    n_grp = acc.shape[1] // LANES

    def treatment(t):
        acts = tuple((k, a, b) for k, st in enumerate(states) for (tt, a, b) in st[1] if tt == t)
        return (tuple(rope_groups.get(t, ())), t == norm_tile, acts)

    branches = {}
    for t in range(n_tiles):
        branches.setdefault(treatment(t), []).append(t)

    o_ref[...] = acc.astype(o_ref.dtype)
    reread = o_ref.dtype == F32

    for (rg, is_norm, acts), tiles in branches.items():
        if not rg and not is_norm and not acts:
            continue
        cond = j == tiles[0]
        for t in tiles[1:]:
            cond = jnp.logical_or(cond, j == t)

        @pl.when(cond)
        def _(rg=rg, is_norm=is_norm, acts=acts):
            vals = [(o_ref if reread else acc)[:, g * LANES:(g + 1) * LANES] for g in range(n_grp)]
            if is_norm:
                ssq = None
                for v in vals:
                    s = jnp.sum(v * v, axis=-1, keepdims=True)
                    ssq = s if ssq is None else ssq + s
                scale = lax.rsqrt(ssq * (1.0 / (n_grp * LANES)) + EPS)
                ng = ng_ref[...]
                vals = [v * scale * ng[:, g * LANES:(g + 1) * LANES] for g, v in enumerate(vals)]
            elif rg:
                cos, sp, sm = cos_ref[...], sp_ref[...], sm_ref[...]
                vals = [_rope(v, cos, sp, sm, sh) if g in rg else v for g, v in enumerate(vals)]
            if is_norm or rg:
                for g, v in enumerate(vals):
                    if is_norm or g in rg:
                        o_ref[:, g * LANES:(g + 1) * LANES] = v.astype(o_ref.dtype)
            for k, a, b in acts:
                n_heads = states[k][0]
                if n_heads:
                    st_refs[k][pl.ds(b, SEG, stride=n_heads), :] = vals[a]
                elif b % LANES == 0:
                    for q in range(b // LANES):
                        st_refs[k][:, q * LANES:(q + 1) * LANES] = vals[a // LANES + q]
                else:
                    st_refs[k][...] = vals[a // LANES][:, a % LANES:a % LANES + b]


def _projection(x, row_off, n_seg, x_blk_w, x_blk_idx, k_in, w, *, prologue, mt=None, g=None, rope=None, sh=16,
                rope_groups=None, norm_tile=None, norm_g=None, states=(), out_dtype=F32, out_scale=1.0, name):
    n = w.shape[1]
    m = n_seg * SEG
    assert n % TN == 0 and w.shape[0] == k_in
    if mt is None:
        mt = jnp.zeros((n_seg, 6, LANES), F32)
    if g is None:
        g = jnp.ones((1, k_in), F32)
    if rope is None:
        rope = tuple(jnp.zeros((8, LANES), F32) for _ in range(3))
    if norm_g is None:
        norm_g = jnp.ones((1, TN), F32)
    kern = functools.partial(_proj_kernel, prologue=prologue, k_in=k_in, sh=sh, n_tiles=n // TN,
                             rope_groups=dict(rope_groups or {}), norm_tile=norm_tile, states=tuple(states),
                             out_scale=float(out_scale))
    full2 = lambda i, j: (0, 0)
    out_shape = [jax.ShapeDtypeStruct((m, n), out_dtype)]
    out_specs = [pl.BlockSpec((SEG, TN), lambda i, j: (i, j))]
    for n_heads, src in states:
        if n_heads:
            out_shape.append(jax.ShapeDtypeStruct((m * n_heads, LANES), F32))
            out_specs.append(pl.BlockSpec((SEG * n_heads, LANES), lambda i, j: (i, 0)))
        else:
            width = src[0][2]
            out_shape.append(jax.ShapeDtypeStruct((m, width), F32))
            out_specs.append(pl.BlockSpec((SEG, width), lambda i, j: (i, 0)))
    x_kw = dict(pipeline_mode=pl.Buffered(1)) if states else {}
    res = pl.pallas_call(
        kern,
        grid=(n_seg, n // TN),
        in_specs=[
            pl.BlockSpec((SEG, x_blk_w), lambda i, j: (i + row_off, x_blk_idx), **x_kw),
            pl.BlockSpec((1, 6, mt.shape[2]), lambda i, j: (i, 0, 0)),
            pl.BlockSpec((1, k_in), full2),
            pl.BlockSpec((k_in, TN), lambda i, j: (0, j)),
            pl.BlockSpec(rope[0].shape, full2),
            pl.BlockSpec(rope[1].shape, full2),
            pl.BlockSpec(rope[2].shape, full2),
            pl.BlockSpec((1, TN), full2),
        ],
        out_specs=out_specs,
        out_shape=out_shape,
        scratch_shapes=[pltpu.VMEM((SEG, k_in), BF16)],
        compiler_params=_cparams(("arbitrary", "arbitrary"), 56 if states else 48),
        name=name,
    )(x, mt, g, w, *rope, norm_g)
    return res if states else res[0]


def _attn_kernel(*refs, kind, n_parts, has_ctx, has_bias, has_band, has_sink, nb, s_len, tq, q_scale, lam_init,
                 own_win):
    it = iter(refs)
    q_refs = [next(it) for _ in range(n_parts)]
    ko_refs = [next(it) for _ in range(n_parts)]
    vo_ref = next(it)
    kc_refs, vc_ref = [], None
    if has_ctx:
        kc_refs = [next(it) for _ in range(n_parts)]
        vc_ref = next(it)
    bias_ref = next(it) if has_bias else None
    lam_ref = sg_ref = None
    if kind == "diff":
        lam_ref = next(it)
        sg_ref = next(it)
    sink_ref = next(it) if has_sink else None
    o_ref = next(it)

    sink2 = sink_ref[pl.program_id(1)] * LOG2E if has_sink else None
    lane = lax.broadcasted_iota(jnp.int32, (1, LANES), 1)

    def softmax_pv(s_list, v_list):
        m = None
        for s in s_list:
            mm = jnp.max(s, axis=-1, keepdims=True)
            m = mm if m is None else jnp.maximum(m, mm)
        if sink2 is not None:
            m = jnp.maximum(m, sink2)
        l = jnp.exp2(sink2 - m) if sink2 is not None else None
        o = None
        for s, v in zip(s_list, v_list):
            e = jnp.exp2(s - m)
            ss = jnp.sum(e, axis=-1, keepdims=True)
            l = ss if l is None else l + ss
            pv = _dot(e.astype(BF16), v)
            o = pv if o is None else o + pv
        return o, l

    for bi in range(nb):
        qrows = pl.ds(bi * tq, tq)
        q0 = pl.program_id(2) * tq
        if own_win is None:
            krows, k0 = pl.ds(bi * s_len, s_len), 0
            n_own = s_len
        else:
            k0 = pl.multiple_of(jnp.clip(q0 - D_WINDOW, 0, s_len - own_win), LANES)
            krows = pl.ds(k0, own_win)
            n_own = own_win
        k_srcs, v_srcs, is_own = [], [], []
        if has_ctx:
            k_srcs.append([r[...].astype(BF16) for r in kc_refs])
            v_srcs.append(vc_ref[...].astype(BF16))
            is_own.append(False)
        k_srcs.append([r[krows, :].astype(BF16) for r in ko_refs])
        v_srcs.append(vo_ref[krows, :].astype(BF16))
        is_own.append(True)

        def mask_own(s):
            if has_bias:
                s = s + bias_ref[...]
            if has_band:
                qpos = q0 + lax.broadcasted_iota(jnp.int32, (tq, 1), 0)
                kpos = k0 + lax.broadcasted_iota(jnp.int32, (1, n_own), 1)
                s = jnp.where(jnp.abs(kpos - qpos) <= D_WINDOW, s, NEG)
            return s

        if kind == "diff":
            q = q_refs[0][qrows, :] * q_scale
            q1 = jnp.where(lane < A_QK_DIM, q, 0.0).astype(BF16)
            q2 = jnp.where(lane < A_QK_DIM, 0.0, q).astype(BF16)
            o1, l1 = softmax_pv([_dot_nt(q1, ks[0]) for ks in k_srcs], v_srcs)
            o2, l2 = softmax_pv([_dot_nt(q2, ks[0]) for ks in k_srcs], v_srcs)
            lv = lam_ref[...]
            lam = (jnp.exp(jnp.sum(lv[0:1] * lv[1:2], axis=-1, keepdims=True))
                   - jnp.exp(jnp.sum(lv[2:3] * lv[3:4], axis=-1, keepdims=True)) + lam_init)
            o = o1 * (1.0 / l1) - o2 * (lam / l2)
            o = o * lax.rsqrt(jnp.mean(o * o, axis=-1, keepdims=True) + EPS) * sg_ref[...] * (1.0 - lam_init)
        else:
            qs = [r[qrows, :] for r in q_refs]
            qs = [(qp * q_scale).astype(BF16) if qp.dtype == F32 else qp for qp in qs]
            s_list = []
            for ks, own in zip(k_srcs, is_own):
                s = None
                for qp, kp in zip(qs, ks):
                    d = _dot_nt(qp, kp)
                    s = d if s is None else s + d
                s_list.append(mask_own(s) if own else s)
            o, l = softmax_pv(s_list, v_srcs)
            o = o * (1.0 / l)
        o_ref[qrows, :] = o.astype(o_ref.dtype)


def _attention(*, kind, latent, q_parts, ko_parts, vo, kc_parts=(), vc=None, bias=None, lam=None, subln=None,
               sink=None, o_arr, o_blk, scale, lam_init=0.0, has_band=False, tq_lat=512, name):
    n_parts = len(q_parts)
    has_ctx = latent
    if latent:
        tq, s_len, nb = tq_lat, DEC_SEQ, 1
        grid = (DEC_BATCH, N_HEADS, DEC_SEQ // tq)
        qpb = DEC_SEQ // tq
        q_row = lambda b, h, t: b * qpb + t
        o_row = lambda b, h, t: N_PROMPT // tq + b * qpb + t
        k_row = lambda b, h, t: b
        qblk = tq
    else:
        tq, s_len, nb = SEQ, SEQ, SEG // SEQ
        grid = (N_SEG_P, N_HEADS, 1)
        q_row = o_row = k_row = lambda b, h, t: b
        qblk = SEG
    args, specs = [], []

    def add(arr, shape, imap, **kw):
        args.append(arr)
        specs.append(pl.BlockSpec(shape, imap, **kw))

    for arr, f in q_parts:
        add(arr, (qblk, LANES), lambda b, h, t, f=f: (q_row(b, h, t), f(h)))
    for arr, f in list(ko_parts) + [vo]:
        add(arr, (SEG, LANES), lambda b, h, t, f=f: (k_row(b, h, t), f(h)))
    if has_ctx:
        for arr, f in list(kc_parts) + [vc]:
            add(arr, (None, PAST_LEN, LANES), lambda b, h, t, f=f: (b, 0, f(h)))
    if bias is not None:
        add(bias, (None, tq, DEC_SEQ), lambda b, h, t: (h, t, 0))
    if kind == "diff":
        add(lam, lam.shape, lambda b, h, t: (0, 0))
        add(subln, subln.shape, lambda b, h, t: (0, 0))
    if sink is not None:
        args.append(sink)
        specs.append(pl.BlockSpec(memory_space=pltpu.SMEM))
    n_in = len(args)
    args.append(o_arr)
    specs.append(pl.BlockSpec(memory_space=pl.ANY))
    kern = functools.partial(_attn_kernel_aliased, kind=kind, n_parts=n_parts, has_ctx=has_ctx,
                             has_bias=bias is not None, has_band=has_band, has_sink=sink is not None,
                             nb=nb, s_len=s_len, tq=tq, q_scale=scale * LOG2E, lam_init=lam_init,
                             own_win=(tq + 2 * D_WINDOW) if has_band else None)
    return pl.pallas_call(
        kern,
        grid=grid,
        in_specs=specs,
        out_specs=pl.BlockSpec((qblk, LANES), lambda b, h, t: (o_row(b, h, t), o_blk(h))),
        out_shape=jax.ShapeDtypeStruct(o_arr.shape, o_arr.dtype),
        input_output_aliases={n_in: 0},
        compiler_params=_cparams(("arbitrary", "arbitrary", "arbitrary"), 48),
        name=name,
    )(*args)


def _attn_kernel_aliased(*refs, **kw):
    _attn_kernel(*refs[:-2], refs[-1], **kw)


def _outproj_kernel(o_ref, w_ref, x_ref, mt_ref, y_ref):
    y_ref[...] = x_ref[...] + mt_ref[0, 2:3, :] * _dot(o_ref[...], w_ref[...])


def _out_projection(o, w, x, mt):
    return pl.pallas_call(
        _outproj_kernel,
        grid=(N_SEG, D_MODEL // TN),
        in_specs=[
            pl.BlockSpec((SEG, D_MODEL), lambda i, j: (i, 0)),
            pl.BlockSpec((D_MODEL, TN), lambda i, j: (0, j)),
            pl.BlockSpec((SEG, TN), lambda i, j: (i, j)),
            pl.BlockSpec((1, 6, TN), lambda i, j: (i, 0, j)),
        ],
        out_specs=pl.BlockSpec((SEG, TN), lambda i, j: (i, j)),
        out_shape=jax.ShapeDtypeStruct((N_TOK, D_MODEL), F32),
        compiler_params=_cparams(("arbitrary", "arbitrary"), 40),
        name="out_proj",
    )(o, w, x, mt)


def _router_kernel(x_ref, mt_ref, g_ref, wh_ref, wl_ref, rb_ref, h_ref, e_ref, gt_ref):
    x = x_ref[...]
    y = x * lax.rsqrt(jnp.mean(x * x, axis=-1, keepdims=True) + EPS) * g_ref[...]
    h = y * (1.0 + mt_ref[0, 4:5, :]) + mt_ref[0, 3:4, :]
    hh = h.astype(BF16)
    h_ref[...] = _pack_bf16_pair(h)
    hl = (h - hh.astype(F32)).astype(BF16)
    wh, wl = wh_ref[...], wl_ref[...]
    logits = _dot_nt(wh, hh) + _dot_nt(wh, hl) + _dot_nt(wl, hh)
    scores = _sigmoid(logits)
    sel = scores + rb_ref[...]
    per = N_EXPERTS // N_GROUPS
    sc = [scores[e:e + 1, :] for e in range(N_EXPERTS)]
    sl = [sel[e:e + 1, :] for e in range(N_EXPERTS)]
    best_g, best_v = None, None
    for gi in range(N_GROUPS):
        a, b, c, d = sl[gi * per:(gi + 1) * per]
        hi1, lo1, hi2, lo2 = jnp.maximum(a, b), jnp.minimum(a, b), jnp.maximum(c, d), jnp.minimum(c, d)
        gs = jnp.maximum(hi1, hi2) + jnp.maximum(jnp.minimum(hi1, hi2), jnp.maximum(lo1, lo2))
        if gi == 0:
            best_g, best_v = jnp.zeros_like(gs, dtype=jnp.int32), gs
        else:
            better = gs > best_v
            best_g = jnp.where(better, gi, best_g)
            best_v = jnp.where(better, gs, best_v)
    masked = [jnp.where(best_g == (e // per), sl[e], NEG) for e in range(N_EXPERTS)]
    i1, v1 = jnp.zeros_like(best_g), masked[0]
    for e in range(1, N_EXPERTS):
        better = masked[e] > v1
        i1 = jnp.where(better, e, i1)
        v1 = jnp.where(better, masked[e], v1)
    i2, v2 = None, None
    for e in range(N_EXPERTS):
        cand = jnp.where(i1 == e, -2e30, masked[e])
        if e == 0:
            i2, v2 = jnp.zeros_like(best_g), cand
        else:
            better = cand > v2
            i2 = jnp.where(better, e, i2)
            v2 = jnp.where(better, cand, v2)
    g1 = jnp.zeros_like(v1)
    g2 = jnp.zeros_like(v1)
    for e in range(N_EXPERTS):
        g1 = jnp.where(i1 == e, sc[e], g1)
        g2 = jnp.where(i2 == e, sc[e], g2)
    tot = g1 + g2
    rows = i1.shape[1]
    e_ref[...] = jnp.concatenate([i1, i2, jnp.zeros((6, rows), jnp.int32)], axis=0)
    gt_ref[...] = jnp.concatenate([g1 / tot, g2 / tot, jnp.zeros((6, rows), F32)], axis=0)


def _router(x, mt, g, router_w, router_b):
    wt = router_w.T
    wh = wt.astype(BF16)
    wl = (wt - wh.astype(F32)).astype(BF16)
    return pl.pallas_call(
        _router_kernel,
        grid=(N_SEG,),
        in_specs=[
            pl.BlockSpec((SEG, D_MODEL), lambda i: (i, 0)),
            pl.BlockSpec((1, 6, D_MODEL), lambda i: (i, 0, 0)),
            pl.BlockSpec((1, D_MODEL), lambda i: (0, 0)),
            pl.BlockSpec((N_EXPERTS, D_MODEL), lambda i: (0, 0)),
            pl.BlockSpec((N_EXPERTS, D_MODEL), lambda i: (0, 0)),
            pl.BlockSpec((N_EXPERTS, 1), lambda i: (0, 0)),
        ],
        out_specs=[
            pl.BlockSpec((SEG, D_MODEL // 2), lambda i: (i, 0)),
            pl.BlockSpec((8, SEG), lambda i: (0, i)),
            pl.BlockSpec((8, SEG), lambda i: (0, i)),
        ],
        out_shape=[
            jax.ShapeDtypeStruct((N_TOK, D_MODEL // 2), jnp.int32),
            jax.ShapeDtypeStruct((8, N_TOK), jnp.int32),
            jax.ShapeDtypeStruct((8, N_TOK), F32),
        ],
        compiler_params=_cparams(("arbitrary",), 48),
        name="ffn_norm_router",
    )(x, mt, g, wh, wl, router_b.reshape(N_EXPERTS, 1))


def _expert_changed(te_ref, i, t):
    return jnp.logical_or(i == 0, te_ref[t] != te_ref[jnp.maximum(t - 1, 0)])


def _moe_up_kernel(te_ref, na_ref, xs_ref, wg_ref, wu_ref, hid_ref, wgb_ref, wub_ref, *, tile0):
    i = pl.program_id(0)
    t = i + tile0

    @pl.when(_expert_changed(te_ref, i, t))
    def _():
        wgb_ref[...] = wg_ref[...].astype(BF16)
        wub_ref[...] = wu_ref[...].astype(BF16)

    @pl.when(t < na_ref[0])
    def _():
        x_hi, x_lo = _unpack_bf16_pair(xs_ref[...])
        x_hi, x_lo = x_hi.astype(BF16), x_lo.astype(BF16)
        half = D_MODEL // 2
        g = _dot(x_hi, wgb_ref[:half, :]) + _dot(x_lo, wgb_ref[half:, :])
        u = _dot(x_hi, wub_ref[:half, :]) + _dot(x_lo, wub_ref[half:, :])
        hid_ref[...] = (g * _sigmoid(g) * u).astype(BF16)

    @pl.when(t >= na_ref[0])
    def _():
        hid_ref[...] = jnp.zeros_like(hid_ref)


def _moe_down_kernel(te_ref, na_ref, hid_ref, wd_ref, *rest, tile0):
    ys_ref, wdb_ref = rest[-2:]
    i = pl.program_id(0)
    t = i + tile0

    @pl.when(_expert_changed(te_ref, i, t))
    def _():
        wdb_ref[...] = wd_ref[...].astype(BF16)

    @pl.when(t < na_ref[0])
    def _():
        ys_ref[...] = _pack_bf16_pair(_dot(hid_ref[...], wdb_ref[...]))

    @pl.when(t >= na_ref[0])
    def _():
        ys_ref[...] = jnp.zeros_like(ys_ref)


MOE_CHUNKS = 2


def _moe_experts(h2, src_tok, tile_expert, n_active, layer, w_gate, w_up, w_down):
    tiles = MOE_TILES // MOE_CHUNKS
    rows = tiles * MOE_TM
    ys = None
    for c in range(MOE_CHUNKS):
        tile0 = c * tiles
        xs = h2.at[src_tok[c * rows:(c + 1) * rows]].get(mode="promise_in_bounds")
        hid = pl.pallas_call(
            functools.partial(_moe_up_kernel, tile0=tile0),
            grid_spec=pltpu.PrefetchScalarGridSpec(
                num_scalar_prefetch=2,
                grid=(tiles,),
                in_specs=[
                    pl.BlockSpec((MOE_TM, D_MODEL // 2), lambda i, te, na: (i, 0)),
                    pl.BlockSpec((None, None, D_MODEL, D_EXPERT), lambda i, te, na: (layer, te[i + tile0], 0, 0)),
                    pl.BlockSpec((None, None, D_MODEL, D_EXPERT), lambda i, te, na: (layer, te[i + tile0], 0, 0)),
                ],
                out_specs=pl.BlockSpec((MOE_TM, D_EXPERT), lambda i, te, na: (i, 0)),
                scratch_shapes=[pltpu.VMEM((D_MODEL, D_EXPERT), BF16), pltpu.VMEM((D_MODEL, D_EXPERT), BF16)],
            ),
            out_shape=jax.ShapeDtypeStruct((rows, D_EXPERT), BF16),
            compiler_params=_cparams(("arbitrary",), 52),
            name="moe_up",
        )(tile_expert, n_active, xs, w_gate, w_up)
        in_specs = [
            pl.BlockSpec((MOE_TM, D_EXPERT), lambda i, te, na: (i, 0)),
            pl.BlockSpec((None, None, D_EXPERT, D_MODEL), lambda i, te, na: (layer, te[i + tile0], 0, 0)),
        ]
        args = [tile_expert, n_active, hid, w_down]
        aliases = {}
        if ys is not None:
            in_specs.append(pl.BlockSpec(memory_space=pl.ANY))
            args.append(ys)
            aliases = {4: 0}
        ys = pl.pallas_call(
            functools.partial(_moe_down_kernel, tile0=tile0),
            grid_spec=pltpu.PrefetchScalarGridSpec(
                num_scalar_prefetch=2,
                grid=(tiles,),
                in_specs=in_specs,
                out_specs=pl.BlockSpec((MOE_TM, D_MODEL // 2), lambda i, te, na: (i + tile0, 0)),
                scratch_shapes=[pltpu.VMEM((D_EXPERT, D_MODEL), BF16)],
            ),
            out_shape=jax.ShapeDtypeStruct((MOE_ROWS, D_MODEL // 2), jnp.int32),
            input_output_aliases=aliases,
            compiler_params=_cparams(("arbitrary",), 40),
            name="moe_down",
        )(*args)
    return ys


PLAN_R = 2 * N_TOK // LANES


def _plan_kernel(e_ref, pos_ref, meta_ref):
    e = e_ref[...]
    r_i = lax.broadcasted_iota(jnp.int32, (LANES, LANES), 0)
    c_i = lax.broadcasted_iota(jnp.int32, (LANES, LANES), 1)
    upper = jnp.where(r_i <= c_i, 1.0, 0.0).astype(BF16)
    r_j = lax.broadcasted_iota(jnp.int32, (PLAN_R, PLAN_R), 0)
    c_j = lax.broadcasted_iota(jnp.int32, (PLAN_R, PLAN_R), 1)
    lower = jnp.where(c_j < r_j, 1.0, 0.0).astype(BF16)
    lane = lax.broadcasted_iota(jnp.int32, (1, LANES), 1)
    pos = jnp.zeros((PLAN_R, LANES), F32)
    cnt_row = jnp.zeros((1, LANES), F32)
    off_row = jnp.zeros((1, LANES), F32)
    end_row = []
    row_off = jnp.zeros((1, 1), F32)
    tiles_done = jnp.zeros((1, 1), F32)
    for ex in range(N_EXPERTS):
        m = jnp.where(e == ex, 1.0, 0.0)
        inc = _dot(m.astype(BF16), upper)
        tot = jnp.broadcast_to(inc[:, LANES - 1:LANES], (PLAN_R, LANES))
        before = _dot(lower, tot.astype(BF16))
        cnt = jnp.sum(inc[:, LANES - 1:LANES], axis=0, keepdims=True)
        pos = pos + m * (row_off + before + inc - 1.0)
        tiles = jnp.floor((cnt + (MOE_TM - 1)) * (1.0 / MOE_TM))
        cnt_row = jnp.where(lane == ex, cnt, cnt_row)
        off_row = jnp.where(lane == ex, row_off, off_row)
        tiles_done = tiles_done + tiles
        end_row.append(tiles_done)
        row_off = row_off + tiles * MOE_TM
    n_active = tiles_done
    tid = jnp.minimum(lane.astype(F32), n_active - 1.0)
    te = jnp.zeros((1, LANES), F32)
    for ex in range(N_EXPERTS):
        te = te + jnp.where(end_row[ex] <= tid, 1.0, 0.0)
    pos_ref[...] = pos.astype(jnp.int32)
    meta = jnp.concatenate([cnt_row, off_row, te, jnp.broadcast_to(n_active, (1, LANES)),
                            jnp.zeros((4, LANES), F32)], axis=0)
    meta_ref[...] = meta.astype(jnp.int32)


def _route_plan(eidx):
    e2 = eidx[:2].reshape(PLAN_R, LANES)
    pos, meta = pl.pallas_call(
        _plan_kernel,
        out_shape=[jax.ShapeDtypeStruct((PLAN_R, LANES), jnp.int32), jax.ShapeDtypeStruct((8, LANES), jnp.int32)],
        compiler_params=pltpu.CompilerParams(vmem_limit_bytes=32 * 1024 * 1024),
        name="route_plan",
    )(e2)
    cnt, row_off = meta[0, :N_EXPERTS], meta[1, :N_EXPERTS]
    te, n_active = meta[2, :MOE_TILES], meta[3, :1]
    start = jnp.cumsum(cnt) - cnt
    order = jnp.argsort(e2.reshape(-1), stable=True).astype(jnp.int32)
    row_e = jnp.repeat(te, MOE_TM)
    rank = jnp.arange(MOE_ROWS, dtype=jnp.int32) - row_off[row_e]
    valid = rank < cnt[row_e]
    src_asg = order[jnp.clip(start[row_e] + rank, 0, 2 * N_TOK - 1)]
    rows = jnp.arange(MOE_ROWS, dtype=jnp.int32)
    src_tok = jnp.where(valid, src_asg % N_TOK, rows % N_TOK)
    return src_tok, te, n_active, pos.reshape(-1)


def _combine_kernel(x_ref, y0_ref, y1_ref, gt_ref, mt_ref, fg_ref, o_ref, *, final):
    gt = gt_ref[...]
    g0, g1 = gt[:, 0:1], gt[:, 1:2]
    half = D_MODEL // 2
    y0 = _unpack_bf16_pair(y0_ref[...])
    y1 = _unpack_bf16_pair(y1_ref[...])
    xs = []
    for c in range(2):
        cols = slice(c * half, (c + 1) * half)
        xs.append(x_ref[:, cols] + mt_ref[0, 5:6, cols] * (y0[c] * g0 + y1[c] * g1))
    if final:
        ssq = jnp.sum(xs[0] * xs[0], axis=-1, keepdims=True) + jnp.sum(xs[1] * xs[1], axis=-1, keepdims=True)
        r = lax.rsqrt(ssq * (1.0 / D_MODEL) + EPS)
        xs = [xc * r * fg_ref[:, c * half:(c + 1) * half] for c, xc in enumerate(xs)]
    for c, xc in enumerate(xs):
        o_ref[:, c * half:(c + 1) * half] = xc


def _combine(x, ysel, gates_t, mt, final_g, final):
    tm = 256
    return pl.pallas_call(
        functools.partial(_combine_kernel, final=final),
        grid=(N_TOK // tm,),
        in_specs=[
            pl.BlockSpec((tm, D_MODEL), lambda i: (i, 0)),
            pl.BlockSpec((None, tm, D_MODEL // 2), lambda i: (0, i, 0)),
            pl.BlockSpec((None, tm, D_MODEL // 2), lambda i: (1, i, 0)),
            pl.BlockSpec((tm, 8), lambda i: (i, 0)),
            pl.BlockSpec((1, 6, D_MODEL), lambda i: (i // (SEG // tm), 0, 0)),
            pl.BlockSpec((1, D_MODEL), lambda i: (0, 0)),
        ],
        out_specs=pl.BlockSpec((tm, D_MODEL), lambda i: (i, 0)),
        out_shape=jax.ShapeDtypeStruct((N_TOK, D_MODEL), F32),
        compiler_params=_cparams(("arbitrary",), 40),
        name="moe_combine",
    )(x, ysel, ysel, gates_t, mt, final_g)


def _rope_tables(dim):
    half = dim // 2
    inv = ROPE_THETA ** (-jnp.arange(0, half, 2, dtype=F32) / half)
    t = jnp.arange(DEC_SEQ)
    ang_r = (t // GRID_W).astype(F32)[:, None] * inv[None, :]
    ang_c = (t % GRID_W).astype(F32)[:, None] * inv[None, :]
    ang = jnp.concatenate([ang_r, ang_r, ang_c, ang_c], axis=-1)
    cos, sin = jnp.cos(ang), jnp.sin(ang)
    reps = LANES // dim
    cos = jnp.tile(cos, (1, reps))
    sin = jnp.tile(sin, (1, reps))
    sh = dim // 4
    second = (np.arange(LANES) % (2 * sh)) >= sh
    sp = jnp.where(second[None, :], sin, 0.0)
    sm = jnp.where(second[None, :], 0.0, -sin)
    return cos, sp, sm


def _neighbourhood_bias(rpb):
    rows = DEC_SEQ // GRID_W
    kh = min(NA_ROWS, rows)
    r = np.arange(rows)
    r0 = np.clip(r - kh // 2, 0, rows - kh)
    kr = np.arange(rows)
    row_ok = (kr[None, :] >= r0[:, None]) & (kr[None, :] < r0[:, None] + kh)
    dr_idx = np.clip(kr[None, :] - r[:, None] + NA_ROWS - 1, 0, 2 * NA_ROWS - 2)
    c = np.arange(GRID_W)
    ws = np.clip(c - NA_COLS // 2, 0, GRID_W - NA_COLS)
    kc = np.arange(GRID_W)
    col_ok = (kc[None, :] >= ws[:, None]) & (kc[None, :] < ws[:, None] + NA_COLS)
    dc_idx = np.clip(kc[None, :] - c[:, None], -(NA_COLS - 1), NA_COLS - 1) + NA_COLS - 1
    sel_r = ((np.arange(2 * NA_ROWS - 1)[:, None, None] == dr_idx[None]) & row_ok[None]).astype(np.float32)
    sel_c = ((np.arange(2 * NA_COLS - 1)[:, None, None] == dc_idx[None]) & col_ok[None]).astype(np.float32)
    t = jnp.einsum("hab,ark->hrkb", rpb.astype(F32), sel_r, precision=lax.Precision.HIGHEST)
    b = jnp.einsum("hrkb,bcx->hrckx", t, sel_c, precision=lax.Precision.HIGHEST)
    ok = row_ok[:, None, :, None] & col_ok[None, :, None, :]
    return jnp.where(ok[None], b * LOG2E, NEG).reshape(N_HEADS, DEC_SEQ, DEC_SEQ)


def _even_layer(x, mt, l, i, rope64, cache_a_k, cache_a_v, cache_b_ckv, cache_b_krope, norm_mix_g, ev_w_in,
                ev_lambda, ev_subln_g, ev_q_norm_g, ev_kv_norm_g, ev_w_uq, ev_w_ukv, ev_w_out):
    w = ev_w_in[i]
    kr = w[:, 4352:4416]
    z = jnp.zeros_like(kr)
    w_in = jnp.concatenate([w[:, :3840], kr, z, z, kr, w[:, 3840:4352]], axis=1).astype(BF16)
    all_g = (0, 1, 2, 3)
    in_kw = dict(prologue="modulate", g=norm_mix_g[l][None, :], norm_tile=8, norm_g=ev_kv_norm_g[i][None, :])
    head_src = lambda t0: tuple((t0 + t, gq, 4 * t + gq) for t in range(2) for gq in range(4))
    proj_p, st_ak, st_av, st_ckv, st_kr = _projection(
        x, 0, N_SEG_P, D_MODEL, 0, D_MODEL, w_in, mt=mt[:N_SEG_P], name="even_in_proj_ctx",
        states=((N_HEADS, head_src(2)), (N_HEADS, head_src(4)), (0, ((8, 0, B_KV_RANK),)), (0, ((7, 256, B_ROPE_DIM),))),
        **in_kw)
    proj_l = _projection(x, N_SEG_P, DEC_BATCH, D_MODEL, 0, D_MODEL, w_in, mt=mt[N_SEG_P:], rope=rope64,
                         sh=A_QK_DIM // 4, rope_groups={0: all_g, 1: all_g, 2: all_g, 3: all_g, 7: (2, 3)},
                         name="even_in_proj_lat", **in_kw)
    wq = ev_w_uq[i].reshape(B_Q_RANK, N_HEADS, HEAD_DIM + B_ROPE_DIM)
    w_uq = jnp.concatenate([wq[:, :, :HEAD_DIM].reshape(B_Q_RANK, -1), wq[:, :, HEAD_DIM:].reshape(B_Q_RANK, -1)],
                           axis=1).astype(BF16)
    mla_scale = (HEAD_DIM + B_ROPE_DIM) ** -0.5
    q_kw = dict(prologue="rmsnorm", g=ev_q_norm_g[i][None, :], out_dtype=BF16, out_scale=mla_scale * LOG2E)
    bq_p = _projection(proj_p, 0, N_SEG_P, 1024, 3, B_Q_RANK, w_uq, name="mla_q_up_ctx", **q_kw)
    bq_l = _projection(proj_l, 0, DEC_BATCH, 1024, 3, B_Q_RANK, w_uq, rope=rope64, sh=B_ROPE_DIM // 4,
                       rope_groups={2: all_g}, name="mla_q_up_lat", **q_kw)
    w_ukv = ev_w_ukv[i].astype(BF16)
    kv_kw = dict(prologue="cast", out_dtype=BF16)
    kv_p = _projection(proj_p, 0, N_SEG_P, 512, 8, B_KV_RANK, w_ukv, name="mla_kv_up_ctx", **kv_kw)
    kv_l = _projection(proj_l, 0, DEC_BATCH, 512, 8, B_KV_RANK, w_ukv, name="mla_kv_up_lat", **kv_kw)
    kv_ctx = _projection(cache_b_ckv[:, i].reshape(DEC_BATCH * PAST_LEN, B_KV_RANK), 0, 1, 512, 0, B_KV_RANK, w_ukv,
                         name="mla_kv_up_cache", **kv_kw)
    kv_ctx = kv_ctx.reshape(DEC_BATCH, PAST_LEN, 2 * N_HEADS * HEAD_DIM)
    krc = cache_b_krope[:, i]
    zc = jnp.zeros_like(krc)
    kr_ctx = jnp.concatenate([krc, zc, zc, krc], axis=-1)
    ak_ctx = cache_a_k[:, i].reshape(DEC_BATCH, PAST_LEN, N_HEADS * HEAD_DIM)
    av_ctx = cache_a_v[:, i].reshape(DEC_BATCH, PAST_LEN, N_HEADS * HEAD_DIM)
    lam_init = 0.8 - 0.6 * math.exp(-0.3 * l)
    o = jnp.zeros((N_TOK, D_MODEL), BF16)
    for latent, proj, b_q, kv in ((False, proj_p, bq_p, kv_p), (True, proj_l, bq_l, kv_l)):
        tag = "lat" if latent else "ctx"
        o = _attention(kind="diff", latent=latent, q_parts=[(proj, lambda h: h)], ko_parts=[(proj, lambda h: 8 + h)],
                       vo=(proj, lambda h: 16 + h), kc_parts=[(ak_ctx, lambda h: h)], vc=(av_ctx, lambda h: h),
                       lam=ev_lambda[i], subln=ev_subln_g[i][None, :], o_arr=o, o_blk=lambda h: h,
                       scale=A_QK_DIM ** -0.5, lam_init=lam_init, name="diff_attn_" + tag)
        o = _attention(kind="mla", latent=latent, q_parts=[(b_q, lambda h: h), (b_q, lambda h: 8 + h // 2)],
                       ko_parts=[(kv, lambda h: 2 * h), (proj, lambda h: 30 + h % 2)], vo=(kv, lambda h: 2 * h + 1),
                       kc_parts=[(kv_ctx, lambda h: 2 * h), (kr_ctx, lambda h: h % 2)], vc=(kv_ctx, lambda h: 2 * h + 1),
                       o_arr=o, o_blk=lambda h: 8 + h, scale=(HEAD_DIM + B_ROPE_DIM) ** -0.5, name="mla_attn_" + tag)
    y = _out_projection(o, ev_w_out[i].astype(BF16), x, mt)
    states = (st_ak.reshape(BATCH, SEQ, N_HEADS, HEAD_DIM), st_av.reshape(BATCH, SEQ, N_HEADS, HEAD_DIM),
              st_ckv.reshape(BATCH, SEQ, B_KV_RANK), st_kr.reshape(BATCH, SEQ, B_ROPE_DIM))
    return y, states


def _odd_layer(x, mt, l, i, rope128, cache_c_k, cache_c_v, cache_d_k, cache_d_v, norm_mix_g, od_w_in, od_rpb,
               od_sink, od_w_out):
    w_in = od_w_in[i].astype(BF16)
    all_g = (0, 1, 2, 3)
    in_kw = dict(prologue="modulate", g=norm_mix_g[l][None, :])
    head_src = lambda t0: tuple((t0 + t, gq, 4 * t + gq) for t in range(2) for gq in range(4))
    proj_p, st_ck, st_cv, st_dk, st_dv = _projection(
        x, 0, N_SEG_P, D_MODEL, 0, D_MODEL, w_in, mt=mt[:N_SEG_P], name="odd_in_proj_ctx",
        states=((N_HEADS, head_src(2)), (N_HEADS, head_src(4)), (D_KV_HEADS, ((8, 0, 0), (8, 1, 1))),
                (D_KV_HEADS, ((8, 2, 0), (8, 3, 1)))), **in_kw)
    proj_l = _projection(x, N_SEG_P, DEC_BATCH, D_MODEL, 0, D_MODEL, w_in, mt=mt[N_SEG_P:], rope=rope128,
                         sh=HEAD_DIM // 4, rope_groups={6: all_g, 7: all_g, 8: (0, 1)}, name="odd_in_proj_lat", **in_kw)
    ck_ctx = cache_c_k[:, i].reshape(DEC_BATCH, PAST_LEN, N_HEADS * HEAD_DIM)
    cv_ctx = cache_c_v[:, i].reshape(DEC_BATCH, PAST_LEN, N_HEADS * HEAD_DIM)
    dk_ctx = cache_d_k[:, i].reshape(DEC_BATCH, PAST_LEN, D_KV_HEADS * HEAD_DIM)
    dv_ctx = cache_d_v[:, i].reshape(DEC_BATCH, PAST_LEN, D_KV_HEADS * HEAD_DIM)
    bias = _neighbourhood_bias(od_rpb[i])
    sink = od_sink[i].astype(F32)
    o = jnp.zeros((N_TOK, D_MODEL), BF16)
    for latent, proj in ((False, proj_p), (True, proj_l)):
        tag = "lat" if latent else "ctx"
        o = _attention(kind="plain", latent=latent, q_parts=[(proj, lambda h: h)], ko_parts=[(proj, lambda h: 8 + h)],
                       vo=(proj, lambda h: 16 + h), kc_parts=[(ck_ctx, lambda h: h)], vc=(cv_ctx, lambda h: h),
                       bias=bias if latent else None, o_arr=o, o_blk=lambda h: h, scale=HEAD_DIM ** -0.5,
                       name="nbr_attn_" + tag)
        o = _attention(kind="plain", latent=latent, q_parts=[(proj, lambda h: 24 + h)],
                       ko_parts=[(proj, lambda h: 32 + h // D_GROUP)], vo=(proj, lambda h: 34 + h // D_GROUP),
                       kc_parts=[(dk_ctx, lambda h: h // D_GROUP)], vc=(dv_ctx, lambda h: h // D_GROUP), sink=sink,
                       o_arr=o, o_blk=lambda h: 8 + h, scale=HEAD_DIM ** -0.5, has_band=latent, tq_lat=256,
                       name="win_attn_" + tag)
    y = _out_projection(o, od_w_out[i].astype(BF16), x, mt)
    states = (st_ck.reshape(BATCH, SEQ, N_HEADS, HEAD_DIM), st_cv.reshape(BATCH, SEQ, N_HEADS, HEAD_DIM),
              st_dk.reshape(BATCH, SEQ, D_KV_HEADS, HEAD_DIM), st_dv.reshape(BATCH, SEQ, D_KV_HEADS, HEAD_DIM))
    return y, states


def _moe_layer(x, mt, g, router_w, router_b, layer, w_gate, w_up, w_down, final_g, final):
    h2, eidx, gates = _router(x, mt, g, router_w, router_b)
    src_tok, te, n_active, pos = _route_plan(eidx)
    ys = _moe_experts(h2, src_tok, te, n_active, layer, w_gate, w_up, w_down)
    ysel = ys.at[pos].get(mode="promise_in_bounds").reshape(2, N_TOK, D_MODEL // 2)
    return _combine(x, ysel, gates.T, mt, final_g, final)


def kernel(x_prompt, x_sample, cache_a_k, cache_a_v, cache_b_ckv, cache_b_krope, cache_c_k, cache_c_v, cache_d_k, cache_d_v, c, c_ctx, w_ada, b_ada, norm_mix_g, norm_ffn_g, ev_w_in, ev_lambda, ev_subln_g, ev_q_norm_g, ev_kv_norm_g, ev_w_uq, ev_w_ukv, ev_w_out, od_w_in, od_rpb, od_sink, od_w_out, router_w, router_b, moe_w_gate, moe_w_up, moe_w_down, final_g):
    x = jnp.concatenate([x_prompt.reshape(N_PROMPT, D_MODEL), x_sample.reshape(DEC_BATCH * DEC_SEQ, D_MODEL)], axis=0)
    cond8 = jnp.concatenate([c_ctx[None, :], c, jnp.zeros((3, D_MODEL), F32)], axis=0)
    mod = _adaln(cond8, w_ada, b_ada)
    seg_row = np.array([0] * N_SEG_P + [1 + b for b in range(DEC_BATCH)])
    mt_all = mod[:, seg_row].reshape(DEPTH, N_SEG, 6, D_MODEL)
    rope64 = _rope_tables(A_QK_DIM)
    rope128 = _rope_tables(HEAD_DIM)
    even_states, odd_states = [], []
    for l in range(DEPTH):
        i = l // 2
        mt = mt_all[l]
        if l % 2 == 0:
            x, st = _even_layer(x, mt, l, i, rope64, cache_a_k, cache_a_v, cache_b_ckv, cache_b_krope, norm_mix_g,
                                ev_w_in, ev_lambda, ev_subln_g, ev_q_norm_g, ev_kv_norm_g, ev_w_uq, ev_w_ukv, ev_w_out)
            even_states.append(st)
        else:
            x, st = _odd_layer(x, mt, l, i, rope128, cache_c_k, cache_c_v, cache_d_k, cache_d_v, norm_mix_g,
                               od_w_in, od_rpb, od_sink, od_w_out)
            odd_states.append(st)
        x = _moe_layer(x, mt, norm_ffn_g[l][None, :], router_w, router_b, l, moe_w_gate, moe_w_up, moe_w_down,
                       final_g[None, :], final=(l == DEPTH - 1))
    y_prompt = x[:N_PROMPT].reshape(BATCH, SEQ, D_MODEL)
    y_sample = x[N_PROMPT:].reshape(DEC_BATCH, DEC_SEQ, D_MODEL)
    new_even = tuple(jnp.stack([st[k] for st in even_states], axis=1) for k in range(4))
    new_odd = tuple(jnp.stack([st[k] for st in odd_states], axis=1) for k in range(4))
    return (y_prompt, y_sample) + new_even + new_odd
```

```python
import functools
import math

import numpy as np
import jax
import jax.numpy as jnp
from jax import lax
from jax.experimental import pallas as pl
from jax.experimental.pallas import tpu as pltpu

D_MODEL = 2048
BATCH = 32
SEQ = 256
DEPTH = 2
DEC_BATCH = 4
DEC_SEQ = 1024
PAST_LEN = 256
GRID_W = 64
HEAD_DIM = 128
N_HEADS = 8
A_QK_DIM = 64
B_Q_RANK = 768
B_KV_RANK = 512
B_ROPE_DIM = 64
NA_ROWS = 8
NA_COLS = 16
D_KV_HEADS = 2
D_GROUP = 4
D_WINDOW = 128
N_EXPERTS = 16
N_GROUPS = 4
D_EXPERT = 1024
ROPE_THETA = 10000.0
EPS = 1e-6
NEG = -1e30
LOG2E = 1.4426950408889634

SEG = 1024
N_PROMPT = BATCH * SEQ
N_TOK = N_PROMPT + DEC_BATCH * DEC_SEQ
N_SEG = N_TOK // SEG
N_SEG_P = N_PROMPT // SEG
PROJ_W = 4608
TN = 512
LANES = 128
MOE_TM = 256
MOE_ROWS = 2 * N_TOK + N_EXPERTS * MOE_TM
MOE_TILES = MOE_ROWS // MOE_TM

F32 = jnp.float32
BF16 = jnp.bfloat16


def _cparams(sem, vmem_mb):
    return pltpu.CompilerParams(dimension_semantics=sem, vmem_limit_bytes=vmem_mb * 1024 * 1024)


def _dot(a, b):
    return jnp.dot(a, b, preferred_element_type=F32)


def _dot_nt(a, b):
    return lax.dot_general(a, b, (((1,), (1,)), ((), ())), preferred_element_type=F32)


def _sigmoid(x):
    return 1.0 / (1.0 + jnp.exp(-x))


def _pack_bf16_pair(x):
    c = x.shape[1] // 2
    bits = pltpu.bitcast(x.astype(BF16).astype(F32), jnp.int32)
    return bits[:, :c] | lax.shift_right_logical(bits[:, c:], 16)


def _unpack_bf16_pair(w):
    hi = pltpu.bitcast(w & jnp.int32(-65536), F32)
    lo = pltpu.bitcast(lax.shift_left(w, 16), F32)
    return hi, lo


def _adaln_kernel(c_ref, w_ref, b_ref, o_ref):
    c = c_ref[...]
    a = (c * _sigmoid(c)).astype(BF16)
    o_ref[...] = _dot(a, w_ref[...].astype(BF16)) + b_ref[...]


def _adaln(cond8, w_ada, b_ada):
    tn = 1024
    n = 6 * D_MODEL
    return pl.pallas_call(
        _adaln_kernel,
        grid=(DEPTH, n // tn),
        in_specs=[
            pl.BlockSpec((8, D_MODEL), lambda l, j: (0, 0)),
            pl.BlockSpec((None, D_MODEL, tn), lambda l, j: (l, 0, j)),
            pl.BlockSpec((None, 1, tn), lambda l, j: (l, 0, j)),
        ],
        out_specs=pl.BlockSpec((None, 8, tn), lambda l, j: (l, 0, j)),
        out_shape=jax.ShapeDtypeStruct((DEPTH, 8, n), F32),
        compiler_params=_cparams(("arbitrary", "arbitrary"), 40),
        name="adaln",
    )(cond8, w_ada, b_ada.reshape(DEPTH, 1, n))


def _rope(a, cos, sp, sm, sh):
    return a * cos + pltpu.roll(a, sh, 1) * sp + pltpu.roll(a, LANES - sh, 1) * sm


def _proj_kernel(*refs, prologue, k_in, sh, n_tiles, rope_groups, norm_tile, states, out_scale):
    x_ref, mt_ref, g_ref, w_ref, cos_ref, sp_ref, sm_ref, ng_ref, o_ref = refs[:9]
    st_refs = refs[9:-1]
    xn_ref = refs[-1]
    j = pl.program_id(1)

    @pl.when(j == 0)
    def _():
        x = x_ref[:, :k_in]
        if prologue == "cast":
            xn_ref[...] = x.astype(BF16)
        else:
            y = x * lax.rsqrt(jnp.mean(x * x, axis=-1, keepdims=True) + EPS) * g_ref[...]
            if prologue == "modulate":
                y = y * (1.0 + mt_ref[0, 1:2, :]) + mt_ref[0, 0:1, :]
            xn_ref[...] = y.astype(BF16)

    acc = _dot(xn_ref[...], w_ref[...])
    if out_scale != 1.0:
        acc = acc * out_scale
    n_grp = acc.shape[1] // LANES

    def treatment(t):
        acts = tuple((k, a, b) for k, st in enumerate(states) for (tt, a, b) in st[1] if tt == t)
        return (tuple(rope_groups.get(t, ())), t == norm_tile, acts)

    branches = {}
    for t in range(n_tiles):
        branches.setdefault(treatment(t), []).append(t)

    o_ref[...] = acc.astype(o_ref.dtype)
    reread = o_ref.dtype == F32

    for (rg, is_norm, acts), tiles in branches.items():
        if not rg and not is_norm and not acts:
            continue
        cond = j == tiles[0]
        for t in tiles[1:]:
            cond = jnp.logical_or(cond, j == t)

        @pl.when(cond)
        def _(rg=rg, is_norm=is_norm, acts=acts):
            vals = [(o_ref if reread else acc)[:, g * LANES:(g + 1) * LANES] for g in range(n_grp)]
            if is_norm:
                ssq = None
                for v in vals:
                    s = jnp.sum(v * v, axis=-1, keepdims=True)
                    ssq = s if ssq is None else ssq + s
                scale = lax.rsqrt(ssq * (1.0 / (n_grp * LANES)) + EPS)
                ng = ng_ref[...]
                vals = [v * scale * ng[:, g * LANES:(g + 1) * LANES] for g, v in enumerate(vals)]
            elif rg:
                cos, sp, sm = cos_ref[...], sp_ref[...], sm_ref[...]
                vals = [_rope(v, cos, sp, sm, sh) if g in rg else v for g, v in enumerate(vals)]
            if is_norm or rg:
                for g, v in enumerate(vals):
                    if is_norm or g in rg:
                        o_ref[:, g * LANES:(g + 1) * LANES] = v.astype(o_ref.dtype)
            for k, a, b in acts:
                n_heads = states[k][0]
                if n_heads:
                    st_refs[k][pl.ds(b, SEG, stride=n_heads), :] = vals[a]
                elif b % LANES == 0:
                    for q in range(b // LANES):
                        st_refs[k][:, q * LANES:(q + 1) * LANES] = vals[a // LANES + q]
                else:
                    st_refs[k][...] = vals[a // LANES][:, a % LANES:a % LANES + b]


def _projection(x, row_off, n_seg, x_blk_w, x_blk_idx, k_in, w, *, prologue, mt=None, g=None, rope=None, sh=16,
                rope_groups=None, norm_tile=None, norm_g=None, states=(), out_dtype=F32, out_scale=1.0, name):
    n = w.shape[1]
    m = n_seg * SEG
    assert n % TN == 0 and w.shape[0] == k_in
    if mt is None:
        mt = jnp.zeros((n_seg, 6, LANES), F32)
    if g is None:
        g = jnp.ones((1, k_in), F32)
    if rope is None:
        rope = tuple(jnp.zeros((8, LANES), F32) for _ in range(3))
    if norm_g is None:
        norm_g = jnp.ones((1, TN), F32)
    kern = functools.partial(_proj_kernel, prologue=prologue, k_in=k_in, sh=sh, n_tiles=n // TN,
                             rope_groups=dict(rope_groups or {}), norm_tile=norm_tile, states=tuple(states),
                             out_scale=float(out_scale))
    full2 = lambda i, j: (0, 0)
    out_shape = [jax.ShapeDtypeStruct((m, n), out_dtype)]
    out_specs = [pl.BlockSpec((SEG, TN), lambda i, j: (i, j))]
    for n_heads, src in states:
        if n_heads:
            out_shape.append(jax.ShapeDtypeStruct((m * n_heads, LANES), F32))
            out_specs.append(pl.BlockSpec((SEG * n_heads, LANES), lambda i, j: (i, 0)))
        else:
            width = src[0][2]
            out_shape.append(jax.ShapeDtypeStruct((m, width), F32))
            out_specs.append(pl.BlockSpec((SEG, width), lambda i, j: (i, 0)))
    x_kw = dict(pipeline_mode=pl.Buffered(1)) if states else {}
    res = pl.pallas_call(
        kern,
        grid=(n_seg, n // TN),
        in_specs=[
            pl.BlockSpec((SEG, x_blk_w), lambda i, j: (i + row_off, x_blk_idx), **x_kw),
            pl.BlockSpec((1, 6, mt.shape[2]), lambda i, j: (i, 0, 0)),
            pl.BlockSpec((1, k_in), full2),
            pl.BlockSpec((k_in, TN), lambda i, j: (0, j)),
            pl.BlockSpec(rope[0].shape, full2),
            pl.BlockSpec(rope[1].shape, full2),
            pl.BlockSpec(rope[2].shape, full2),
            pl.BlockSpec((1, TN), full2),
        ],
        out_specs=out_specs,
        out_shape=out_shape,
        scratch_shapes=[pltpu.VMEM((SEG, k_in), BF16)],
        compiler_params=_cparams(("arbitrary", "arbitrary"), 56 if states else 48),
        name=name,
    )(x, mt, g, w, *rope, norm_g)
    return res if states else res[0]


def _attn_kernel(*refs, kind, n_parts, has_ctx, has_bias, has_band, has_sink, nb, s_len, tq, q_scale, lam_init,
                 own_win):
    it = iter(refs)
    q_refs = [next(it) for _ in range(n_parts)]
    ko_refs = [next(it) for _ in range(n_parts)]
    vo_ref = next(it)
    kc_refs, vc_ref = [], None
    if has_ctx:
        kc_refs = [next(it) for _ in range(n_parts)]
        vc_ref = next(it)
    bias_ref = next(it) if has_bias else None
    lam_ref = sg_ref = None
    if kind == "diff":
        lam_ref = next(it)
        sg_ref = next(it)
    sink_ref = next(it) if has_sink else None
    o_ref = next(it)

    sink2 = sink_ref[pl.program_id(1)] * LOG2E if has_sink else None
    lane = lax.broadcasted_iota(jnp.int32, (1, LANES), 1)

    def softmax_pv(s_list, v_list):
        m = None
        for s in s_list:
            mm = jnp.max(s, axis=-1, keepdims=True)
            m = mm if m is None else jnp.maximum(m, mm)
        if sink2 is not None:
            m = jnp.maximum(m, sink2)
        l = jnp.exp2(sink2 - m) if sink2 is not None else None
        o = None
        for s, v in zip(s_list, v_list):
            e = jnp.exp2(s - m)
            ss = jnp.sum(e, axis=-1, keepdims=True)
            l = ss if l is None else l + ss
            pv = _dot(e.astype(BF16), v)
            o = pv if o is None else o + pv
        return o, l

    for bi in range(nb):
        qrows = pl.ds(bi * tq, tq)
        q0 = pl.program_id(2) * tq
        if own_win is None:
            krows, k0 = pl.ds(bi * s_len, s_len), 0
            n_own = s_len
        else:
            k0 = pl.multiple_of(jnp.clip(q0 - D_WINDOW, 0, s_len - own_win), LANES)
            krows = pl.ds(k0, own_win)
            n_own = own_win
        k_srcs, v_srcs, is_own = [], [], []
        if has_ctx:
            k_srcs.append([r[...].astype(BF16) for r in kc_refs])
            v_srcs.append(vc_ref[...].astype(BF16))
            is_own.append(False)
        k_srcs.append([r[krows, :].astype(BF16) for r in ko_refs])
        v_srcs.append(vo_ref[krows, :].astype(BF16))
        is_own.append(True)

        def mask_own(s):
            if has_bias:
                s = s + bias_ref[...]
            if has_band:
                qpos = q0 + lax.broadcasted_iota(jnp.int32, (tq, 1), 0)
                kpos = k0 + lax.broadcasted_iota(jnp.int32, (1, n_own), 1)
                s = jnp.where(jnp.abs(kpos - qpos) <= D_WINDOW, s, NEG)
            return s

        if kind == "diff":
            q = q_refs[0][qrows, :] * q_scale
            q1 = jnp.where(lane < A_QK_DIM, q, 0.0).astype(BF16)
            q2 = jnp.where(lane < A_QK_DIM, 0.0, q).astype(BF16)
            o1, l1 = softmax_pv([_dot_nt(q1, ks[0]) for ks in k_srcs], v_srcs)
            o2, l2 = softmax_pv([_dot_nt(q2, ks[0]) for ks in k_srcs], v_srcs)
            lv = lam_ref[...]
            lam = (jnp.exp(jnp.sum(lv[0:1] * lv[1:2], axis=-1, keepdims=True))
                   - jnp.exp(jnp.sum(lv[2:3] * lv[3:4], axis=-1, keepdims=True)) + lam_init)
            o = o1 * (1.0 / l1) - o2 * (lam / l2)
            o = o * lax.rsqrt(jnp.mean(o * o, axis=-1, keepdims=True) + EPS) * sg_ref[...] * (1.0 - lam_init)
        else:
            qs = [r[qrows, :] for r in q_refs]
            qs = [(qp * q_scale).astype(BF16) if qp.dtype == F32 else qp for qp in qs]
            s_list = []
            for ks, own in zip(k_srcs, is_own):
                s = None
                for qp, kp in zip(qs, ks):
                    d = _dot_nt(qp, kp)
                    s = d if s is None else s + d
                s_list.append(mask_own(s) if own else s)
            o, l = softmax_pv(s_list, v_srcs)
            o = o * (1.0 / l)
        o_ref[qrows, :] = o.astype(o_ref.dtype)


def _attention(*, kind, latent, q_parts, ko_parts, vo, kc_parts=(), vc=None, bias=None, lam=None, subln=None,
               sink=None, o_arr, o_blk, scale, lam_init=0.0, has_band=False, tq_lat=512, name):
    n_parts = len(q_parts)
    has_ctx = latent
    if latent:
        tq, s_len, nb = tq_lat, DEC_SEQ, 1
        grid = (DEC_BATCH, N_HEADS, DEC_SEQ // tq)
        qpb = DEC_SEQ // tq
        q_row = lambda b, h, t: b * qpb + t
        o_row = lambda b, h, t: N_PROMPT // tq + b * qpb + t
        k_row = lambda b, h, t: b
        qblk = tq
    else:
        tq, s_len, nb = SEQ, SEQ, SEG // SEQ
        grid = (N_SEG_P, N_HEADS, 1)
        q_row = o_row = k_row = lambda b, h, t: b
        qblk = SEG
    args, specs = [], []

    def add(arr, shape, imap, **kw):
        args.append(arr)
        specs.append(pl.BlockSpec(shape, imap, **kw))

    for arr, f in q_parts:
        add(arr, (qblk, LANES), lambda b, h, t, f=f: (q_row(b, h, t), f(h)))
    for arr, f in list(ko_parts) + [vo]:
        add(arr, (SEG, LANES), lambda b, h, t, f=f: (k_row(b, h, t), f(h)))
    if has_ctx:
        for arr, f in list(kc_parts) + [vc]:
            add(arr, (None, PAST_LEN, LANES), lambda b, h, t, f=f: (b, 0, f(h)))
    if bias is not None:
        add(bias, (None, tq, DEC_SEQ), lambda b, h, t: (h, t, 0))
    if kind == "diff":
        add(lam, lam.shape, lambda b, h, t: (0, 0))
        add(subln, subln.shape, lambda b, h, t: (0, 0))
    if sink is not None:
        args.append(sink)
        specs.append(pl.BlockSpec(memory_space=pltpu.SMEM))
    n_in = len(args)
    args.append(o_arr)
    specs.append(pl.BlockSpec(memory_space=pl.ANY))
    kern = functools.partial(_attn_kernel_aliased, kind=kind, n_parts=n_parts, has_ctx=has_ctx,
                             has_bias=bias is not None, has_band=has_band, has_sink=sink is not None,
                             nb=nb, s_len=s_len, tq=tq, q_scale=scale * LOG2E, lam_init=lam_init,
                             own_win=(tq + 2 * D_WINDOW) if has_band else None)
    return pl.pallas_call(
        kern,
        grid=grid,
        in_specs=specs,
        out_specs=pl.BlockSpec((qblk, LANES), lambda b, h, t: (o_row(b, h, t), o_blk(h))),
        out_shape=jax.ShapeDtypeStruct(o_arr.shape, o_arr.dtype),
        input_output_aliases={n_in: 0},
        compiler_params=_cparams(("arbitrary", "arbitrary", "arbitrary"), 48),
        name=name,
    )(*args)


def _attn_kernel_aliased(*refs, **kw):
    _attn_kernel(*refs[:-2], refs[-1], **kw)


def _outproj_kernel(o_ref, w_ref, x_ref, mt_ref, y_ref):
    y_ref[...] = x_ref[...] + mt_ref[0, 2:3, :] * _dot(o_ref[...], w_ref[...])


def _out_projection(o, w, x, mt):
    return pl.pallas_call(
        _outproj_kernel,
        grid=(N_SEG, D_MODEL // TN),
        in_specs=[
            pl.BlockSpec((SEG, D_MODEL), lambda i, j: (i, 0)),
            pl.BlockSpec((D_MODEL, TN), lambda i, j: (0, j)),
            pl.BlockSpec((SEG, TN), lambda i, j: (i, j)),
            pl.BlockSpec((1, 6, TN), lambda i, j: (i, 0, j)),
        ],
        out_specs=pl.BlockSpec((SEG, TN), lambda i, j: (i, j)),
        out_shape=jax.ShapeDtypeStruct((N_TOK, D_MODEL), F32),
        compiler_params=_cparams(("arbitrary", "arbitrary"), 40),
        name="out_proj",
    )(o, w, x, mt)


def _router_kernel(x_ref, mt_ref, g_ref, wh_ref, wl_ref, rb_ref, h_ref, e_ref, gt_ref):
    x = x_ref[...]
    y = x * lax.rsqrt(jnp.mean(x * x, axis=-1, keepdims=True) + EPS) * g_ref[...]
    h = y * (1.0 + mt_ref[0, 4:5, :]) + mt_ref[0, 3:4, :]
    hh = h.astype(BF16)
    h_ref[...] = _pack_bf16_pair(h)
    hl = (h - hh.astype(F32)).astype(BF16)
    wh, wl = wh_ref[...], wl_ref[...]
    logits = _dot_nt(wh, hh) + _dot_nt(wh, hl) + _dot_nt(wl, hh)
    scores = _sigmoid(logits)
    sel = scores + rb_ref[...]
    per = N_EXPERTS // N_GROUPS
    sc = [scores[e:e + 1, :] for e in range(N_EXPERTS)]
    sl = [sel[e:e + 1, :] for e in range(N_EXPERTS)]
    best_g, best_v = None, None
    for gi in range(N_GROUPS):
        a, b, c, d = sl[gi * per:(gi + 1) * per]
        hi1, lo1, hi2, lo2 = jnp.maximum(a, b), jnp.minimum(a, b), jnp.maximum(c, d), jnp.minimum(c, d)
        gs = jnp.maximum(hi1, hi2) + jnp.maximum(jnp.minimum(hi1, hi2), jnp.maximum(lo1, lo2))
        if gi == 0:
            best_g, best_v = jnp.zeros_like(gs, dtype=jnp.int32), gs
        else:
            better = gs > best_v
            best_g = jnp.where(better, gi, best_g)
            best_v = jnp.where(better, gs, best_v)
    masked = [jnp.where(best_g == (e // per), sl[e], NEG) for e in range(N_EXPERTS)]
    i1, v1 = jnp.zeros_like(best_g), masked[0]
    for e in range(1, N_EXPERTS):
        better = masked[e] > v1
        i1 = jnp.where(better, e, i1)
        v1 = jnp.where(better, masked[e], v1)
    i2, v2 = None, None
    for e in range(N_EXPERTS):
        cand = jnp.where(i1 == e, -2e30, masked[e])
        if e == 0:
            i2, v2 = jnp.zeros_like(best_g), cand
        else:
            better = cand > v2
            i2 = jnp.where(better, e, i2)
            v2 = jnp.where(better, cand, v2)
    g1 = jnp.zeros_like(v1)
    g2 = jnp.zeros_like(v1)
    for e in range(N_EXPERTS):
        g1 = jnp.where(i1 == e, sc[e], g1)
        g2 = jnp.where(i2 == e, sc[e], g2)
    tot = g1 + g2
    rows = i1.shape[1]
    e_ref[...] = jnp.concatenate([i1, i2, jnp.zeros((6, rows), jnp.int32)], axis=0)
    gt_ref[...] = jnp.concatenate([g1 / tot, g2 / tot, jnp.zeros((6, rows), F32)], axis=0)


def _router(x, mt, g, router_w, router_b):
    wt = router_w.T
    wh = wt.astype(BF16)
    wl = (wt - wh.astype(F32)).astype(BF16)
    return pl.pallas_call(
        _router_kernel,
        grid=(N_SEG,),
        in_specs=[
            pl.BlockSpec((SEG, D_MODEL), lambda i: (i, 0)),
            pl.BlockSpec((1, 6, D_MODEL), lambda i: (i, 0, 0)),
            pl.BlockSpec((1, D_MODEL), lambda i: (0, 0)),
            pl.BlockSpec((N_EXPERTS, D_MODEL), lambda i: (0, 0)),
            pl.BlockSpec((N_EXPERTS, D_MODEL), lambda i: (0, 0)),
            pl.BlockSpec((N_EXPERTS, 1), lambda i: (0, 0)),
        ],
        out_specs=[
            pl.BlockSpec((SEG, D_MODEL // 2), lambda i: (i, 0)),
            pl.BlockSpec((8, SEG), lambda i: (0, i)),
            pl.BlockSpec((8, SEG), lambda i: (0, i)),
        ],
        out_shape=[
            jax.ShapeDtypeStruct((N_TOK, D_MODEL // 2), jnp.int32),
            jax.ShapeDtypeStruct((8, N_TOK), jnp.int32),
            jax.ShapeDtypeStruct((8, N_TOK), F32),
        ],
        compiler_params=_cparams(("arbitrary",), 48),
        name="ffn_norm_router",
    )(x, mt, g, wh, wl, router_b.reshape(N_EXPERTS, 1))


def _expert_changed(te_ref, i, t):
    return jnp.logical_or(i == 0, te_ref[t] != te_ref[jnp.maximum(t - 1, 0)])


def _moe_up_kernel(te_ref, na_ref, xs_ref, wg_ref, wu_ref, hid_ref, wgb_ref, wub_ref, *, tile0):
    i = pl.program_id(0)
    t = i + tile0

    @pl.when(_expert_changed(te_ref, i, t))
    def _():
        wgb_ref[...] = wg_ref[...].astype(BF16)
        wub_ref[...] = wu_ref[...].astype(BF16)

    @pl.when(t < na_ref[0])
    def _():
        x_hi, x_lo = _unpack_bf16_pair(xs_ref[...])
        x_hi, x_lo = x_hi.astype(BF16), x_lo.astype(BF16)
        half = D_MODEL // 2
        g = _dot(x_hi, wgb_ref[:half, :]) + _dot(x_lo, wgb_ref[half:, :])
        u = _dot(x_hi, wub_ref[:half, :]) + _dot(x_lo, wub_ref[half:, :])
        hid_ref[...] = (g * _sigmoid(g) * u).astype(BF16)

    @pl.when(t >= na_ref[0])
    def _():
        hid_ref[...] = jnp.zeros_like(hid_ref)


def _moe_down_kernel(te_ref, na_ref, hid_ref, wd_ref, *rest, tile0):
    ys_ref, wdb_ref = rest[-2:]
    i = pl.program_id(0)
    t = i + tile0

    @pl.when(_expert_changed(te_ref, i, t))
    def _():
        wdb_ref[...] = wd_ref[...].astype(BF16)

    @pl.when(t < na_ref[0])
    def _():
        ys_ref[...] = _pack_bf16_pair(_dot(hid_ref[...], wdb_ref[...]))

    @pl.when(t >= na_ref[0])
    def _():
        ys_ref[...] = jnp.zeros_like(ys_ref)


MOE_CHUNKS = 2


def _moe_experts(h2, src_tok, tile_expert, n_active, layer, w_gate, w_up, w_down):
    tiles = MOE_TILES // MOE_CHUNKS
    rows = tiles * MOE_TM
    ys = None
    for c in range(MOE_CHUNKS):
        tile0 = c * tiles
        xs = h2.at[src_tok[c * rows:(c + 1) * rows]].get(mode="promise_in_bounds")
        hid = pl.pallas_call(
            functools.partial(_moe_up_kernel, tile0=tile0),
            grid_spec=pltpu.PrefetchScalarGridSpec(
                num_scalar_prefetch=2,
                grid=(tiles,),
                in_specs=[
                    pl.BlockSpec((MOE_TM, D_MODEL // 2), lambda i, te, na: (i, 0)),
                    pl.BlockSpec((None, None, D_MODEL, D_EXPERT), lambda i, te, na: (layer, te[i + tile0], 0, 0)),
                    pl.BlockSpec((None, None, D_MODEL, D_EXPERT), lambda i, te, na: (layer, te[i + tile0], 0, 0)),
                ],
                out_specs=pl.BlockSpec((MOE_TM, D_EXPERT), lambda i, te, na: (i, 0)),
                scratch_shapes=[pltpu.VMEM((D_MODEL, D_EXPERT), BF16), pltpu.VMEM((D_MODEL, D_EXPERT), BF16)],
            ),
            out_shape=jax.ShapeDtypeStruct((rows, D_EXPERT), BF16),
            compiler_params=_cparams(("arbitrary",), 52),
            name="moe_up",
        )(tile_expert, n_active, xs, w_gate, w_up)
        in_specs = [
            pl.BlockSpec((MOE_TM, D_EXPERT), lambda i, te, na: (i, 0)),
            pl.BlockSpec((None, None, D_EXPERT, D_MODEL), lambda i, te, na: (layer, te[i + tile0], 0, 0)),
        ]
        args = [tile_expert, n_active, hid, w_down]
        aliases = {}
        if ys is not None:
            in_specs.append(pl.BlockSpec(memory_space=pl.ANY))
            args.append(ys)
            aliases = {4: 0}
        ys = pl.pallas_call(
            functools.partial(_moe_down_kernel, tile0=tile0),
            grid_spec=pltpu.PrefetchScalarGridSpec(
                num_scalar_prefetch=2,
                grid=(tiles,),
                in_specs=in_specs,
                out_specs=pl.BlockSpec((MOE_TM, D_MODEL // 2), lambda i, te, na: (i + tile0, 0)),
                scratch_shapes=[pltpu.VMEM((D_EXPERT, D_MODEL), BF16)],
            ),
            out_shape=jax.ShapeDtypeStruct((MOE_ROWS, D_MODEL // 2), jnp.int32),
            input_output_aliases=aliases,
            compiler_params=_cparams(("arbitrary",), 40),
            name="moe_down",
        )(*args)
    return ys


PLAN_R = 2 * N_TOK // LANES


def _plan_kernel(e_ref, pos_ref, meta_ref):
    e = e_ref[...]
    r_i = lax.broadcasted_iota(jnp.int32, (LANES, LANES), 0)
    c_i = lax.broadcasted_iota(jnp.int32, (LANES, LANES), 1)
    upper = jnp.where(r_i <= c_i, 1.0, 0.0).astype(BF16)
    r_j = lax.broadcasted_iota(jnp.int32, (PLAN_R, PLAN_R), 0)
    c_j = lax.broadcasted_iota(jnp.int32, (PLAN_R, PLAN_R), 1)
    lower = jnp.where(c_j < r_j, 1.0, 0.0).astype(BF16)
    lane = lax.broadcasted_iota(jnp.int32, (1, LANES), 1)
    pos = jnp.zeros((PLAN_R, LANES), F32)
    cnt_row = jnp.zeros((1, LANES), F32)
    off_row = jnp.zeros((1, LANES), F32)
    end_row = []
    row_off = jnp.zeros((1, 1), F32)
    tiles_done = jnp.zeros((1, 1), F32)
    for ex in range(N_EXPERTS):
        m = jnp.where(e == ex, 1.0, 0.0)
        inc = _dot(m.astype(BF16), upper)
        tot = jnp.broadcast_to(inc[:, LANES - 1:LANES], (PLAN_R, LANES))
        before = _dot(lower, tot.astype(BF16))
        cnt = jnp.sum(inc[:, LANES - 1:LANES], axis=0, keepdims=True)
        pos = pos + m * (row_off + before + inc - 1.0)
        tiles = jnp.floor((cnt + (MOE_TM - 1)) * (1.0 / MOE_TM))
        cnt_row = jnp.where(lane == ex, cnt, cnt_row)
        off_row = jnp.where(lane == ex, row_off, off_row)
        tiles_done = tiles_done + tiles
        end_row.append(tiles_done)
        row_off = row_off + tiles * MOE_TM
    n_active = tiles_done
    tid = jnp.minimum(lane.astype(F32), n_active - 1.0)
    te = jnp.zeros((1, LANES), F32)
    for ex in range(N_EXPERTS):
        te = te + jnp.where(end_row[ex] <= tid, 1.0, 0.0)
    pos_ref[...] = pos.astype(jnp.int32)
    meta = jnp.concatenate([cnt_row, off_row, te, jnp.broadcast_to(n_active, (1, LANES)),
                            jnp.zeros((4, LANES), F32)], axis=0)
    meta_ref[...] = meta.astype(jnp.int32)


def _route_plan(eidx):
    e2 = eidx[:2].reshape(PLAN_R, LANES)
    pos, meta = pl.pallas_call(
        _plan_kernel,
        out_shape=[jax.ShapeDtypeStruct((PLAN_R, LANES), jnp.int32), jax.ShapeDtypeStruct((8, LANES), jnp.int32)],
        compiler_params=pltpu.CompilerParams(vmem_limit_bytes=32 * 1024 * 1024),
        name="route_plan",
    )(e2)
    cnt, row_off = meta[0, :N_EXPERTS], meta[1, :N_EXPERTS]
    te, n_active = meta[2, :MOE_TILES], meta[3, :1]
    start = jnp.cumsum(cnt) - cnt
    order = jnp.argsort(e2.reshape(-1), stable=True).astype(jnp.int32)
    row_e = jnp.repeat(te, MOE_TM)
    rank = jnp.arange(MOE_ROWS, dtype=jnp.int32) - row_off[row_e]
    valid = rank < cnt[row_e]
    src_asg = order[jnp.clip(start[row_e] + rank, 0, 2 * N_TOK - 1)]
    rows = jnp.arange(MOE_ROWS, dtype=jnp.int32)
    src_tok = jnp.where(valid, src_asg % N_TOK, rows % N_TOK)
    return src_tok, te, n_active, pos.reshape(-1)


def _combine_kernel(x_ref, y0_ref, y1_ref, gt_ref, mt_ref, fg_ref, *o_refs, final, tm):
    if final:
        i = pl.program_id(0)
        for o_ref, cond in zip(o_refs, (i < N_PROMPT // tm, i >= N_PROMPT // tm)):
            pl.when(cond)(functools.partial(_combine_tile, x_ref, y0_ref, y1_ref, gt_ref, mt_ref, fg_ref, o_ref, True))
    else:
        _combine_tile(x_ref, y0_ref, y1_ref, gt_ref, mt_ref, fg_ref, o_refs[0], False)


def _combine_tile(x_ref, y0_ref, y1_ref, gt_ref, mt_ref, fg_ref, o_ref, final):
    gt = gt_ref[...]
    g0, g1 = gt[:, 0:1], gt[:, 1:2]
    half = D_MODEL // 2
    y0 = _unpack_bf16_pair(y0_ref[...])
    y1 = _unpack_bf16_pair(y1_ref[...])
    xs = []
    for c in range(2):
        cols = slice(c * half, (c + 1) * half)
        xs.append(x_ref[:, cols] + mt_ref[0, 5:6, cols] * (y0[c] * g0 + y1[c] * g1))
    if final:
        ssq = jnp.sum(xs[0] * xs[0], axis=-1, keepdims=True) + jnp.sum(xs[1] * xs[1], axis=-1, keepdims=True)
        r = lax.rsqrt(ssq * (1.0 / D_MODEL) + EPS)
        xs = [xc * r * fg_ref[:, c * half:(c + 1) * half] for c, xc in enumerate(xs)]
    for c, xc in enumerate(xs):
        o_ref[:, c * half:(c + 1) * half] = xc


def _combine(x, ysel, gates_t, mt, final_g, final):
    tm = 256
    n_p = N_PROMPT // tm
    if final:
        out_specs = [pl.BlockSpec((tm, D_MODEL), lambda i: (jnp.minimum(i, n_p - 1), 0)),
                     pl.BlockSpec((tm, D_MODEL), lambda i: (jnp.maximum(i - n_p, 0), 0))]
        out_shape = [jax.ShapeDtypeStruct((N_PROMPT, D_MODEL), F32),
                     jax.ShapeDtypeStruct((N_TOK - N_PROMPT, D_MODEL), F32)]
    else:
        out_specs = [pl.BlockSpec((tm, D_MODEL), lambda i: (i, 0))]
        out_shape = [jax.ShapeDtypeStruct((N_TOK, D_MODEL), F32)]
    return pl.pallas_call(
        functools.partial(_combine_kernel, final=final, tm=tm),
        grid=(N_TOK // tm,),
        in_specs=[
            pl.BlockSpec((tm, D_MODEL), lambda i: (i, 0)),
            pl.BlockSpec((None, tm, D_MODEL // 2), lambda i: (0, i, 0)),
            pl.BlockSpec((None, tm, D_MODEL // 2), lambda i: (1, i, 0)),
            pl.BlockSpec((tm, 8), lambda i: (i, 0)),
            pl.BlockSpec((1, 6, D_MODEL), lambda i: (i // (SEG // tm), 0, 0)),
            pl.BlockSpec((1, D_MODEL), lambda i: (0, 0)),
        ],
        out_specs=out_specs,
        out_shape=out_shape,
        compiler_params=_cparams(("arbitrary",), 40),
        name="moe_combine",
    )(x, ysel, ysel, gates_t, mt, final_g)


def _rope_tables(dim):
    half = dim // 2
    inv = ROPE_THETA ** (-jnp.arange(0, half, 2, dtype=F32) / half)
    t = jnp.arange(DEC_SEQ)
    ang_r = (t // GRID_W).astype(F32)[:, None] * inv[None, :]
    ang_c = (t % GRID_W).astype(F32)[:, None] * inv[None, :]
    ang = jnp.concatenate([ang_r, ang_r, ang_c, ang_c], axis=-1)
    cos, sin = jnp.cos(ang), jnp.sin(ang)
    reps = LANES // dim
    cos = jnp.tile(cos, (1, reps))
    sin = jnp.tile(sin, (1, reps))
    sh = dim // 4
    second = (np.arange(LANES) % (2 * sh)) >= sh
    sp = jnp.where(second[None, :], sin, 0.0)
    sm = jnp.where(second[None, :], 0.0, -sin)
    return cos, sp, sm


def _neighbourhood_bias(rpb):
    rows = DEC_SEQ // GRID_W
    kh = min(NA_ROWS, rows)
    r = np.arange(rows)
    r0 = np.clip(r - kh // 2, 0, rows - kh)
    kr = np.arange(rows)
    row_ok = (kr[None, :] >= r0[:, None]) & (kr[None, :] < r0[:, None] + kh)
    c = np.arange(GRID_W)
    ws = np.clip(c - NA_COLS // 2, 0, GRID_W - NA_COLS)
    kc = np.arange(GRID_W)
    col_ok = (kc[None, :] >= ws[:, None]) & (kc[None, :] < ws[:, None] + NA_COLS)
    dc_idx = np.clip(kc[None, :] - c[:, None], -(NA_COLS - 1), NA_COLS - 1) + NA_COLS - 1
    sel_c = ((np.arange(2 * NA_COLS - 1)[:, None, None] == dc_idx[None]) & col_ok[None]).astype(np.float32)
    t = jnp.einsum("hab,bcx->hacx", rpb.astype(F32), sel_c, precision=lax.Precision.HIGHEST)
    t = jnp.where(col_ok[None, None], t * LOG2E, NEG)
    t = jnp.concatenate([t, t], axis=-1)

    def build(t_ref, o_ref):
        left = lax.broadcasted_iota(jnp.int32, (1, LANES), 1) < GRID_W
        neg = jnp.full((GRID_W, LANES), NEG, F32)
        for rr in range(rows):
            for p in range(rows // 2):
                parts = [t_ref[k - rr + NA_ROWS - 1] if row_ok[rr, k] else neg for k in (2 * p, 2 * p + 1)]
                blk = jnp.where(left, parts[0], parts[1]) if (row_ok[rr, 2 * p] or row_ok[rr, 2 * p + 1]) else neg
                o_ref[rr * GRID_W:(rr + 1) * GRID_W, p * LANES:(p + 1) * LANES] = blk

    return pl.pallas_call(
        build,
        grid=(N_HEADS,),
        in_specs=[pl.BlockSpec((None, 2 * NA_ROWS - 1, GRID_W, LANES), lambda h: (h, 0, 0, 0))],
        out_specs=pl.BlockSpec((None, DEC_SEQ, DEC_SEQ), lambda h: (h, 0, 0)),
        out_shape=jax.ShapeDtypeStruct((N_HEADS, DEC_SEQ, DEC_SEQ), F32),
        compiler_params=_cparams(("arbitrary",), 32),
        name="nbr_bias",
    )(t)


def _even_layer(x, mt, l, i, rope64, cache_a_k, cache_a_v, cache_b_ckv, cache_b_krope, norm_mix_g, ev_w_in,
                ev_lambda, ev_subln_g, ev_q_norm_g, ev_kv_norm_g, ev_w_uq, ev_w_ukv, ev_w_out):
    w = ev_w_in[i]
    kr = w[:, 4352:4416]
    z = jnp.zeros_like(kr)
    w_in = jnp.concatenate([w[:, :3840], kr, z, z, kr, w[:, 3840:4352]], axis=1).astype(BF16)
    all_g = (0, 1, 2, 3)
    in_kw = dict(prologue="modulate", g=norm_mix_g[l][None, :], norm_tile=8, norm_g=ev_kv_norm_g[i][None, :])
    head_src = lambda t0: tuple((t0 + t, gq, 4 * t + gq) for t in range(2) for gq in range(4))
    proj_p, st_ak, st_av, st_ckv, st_kr = _projection(
        x, 0, N_SEG_P, D_MODEL, 0, D_MODEL, w_in, mt=mt[:N_SEG_P], name="even_in_proj_ctx",
        states=((N_HEADS, head_src(2)), (N_HEADS, head_src(4)), (0, ((8, 0, B_KV_RANK),)), (0, ((7, 256, B_ROPE_DIM),))),
        **in_kw)
    proj_l = _projection(x, N_SEG_P, DEC_BATCH, D_MODEL, 0, D_MODEL, w_in, mt=mt[N_SEG_P:], rope=rope64,
                         sh=A_QK_DIM // 4, rope_groups={0: all_g, 1: all_g, 2: all_g, 3: all_g, 7: (2, 3)},
                         name="even_in_proj_lat", **in_kw)
    wq = ev_w_uq[i].reshape(B_Q_RANK, N_HEADS, HEAD_DIM + B_ROPE_DIM)
    w_uq = jnp.concatenate([wq[:, :, :HEAD_DIM].reshape(B_Q_RANK, -1), wq[:, :, HEAD_DIM:].reshape(B_Q_RANK, -1)],
                           axis=1).astype(BF16)
    mla_scale = (HEAD_DIM + B_ROPE_DIM) ** -0.5
    q_kw = dict(prologue="rmsnorm", g=ev_q_norm_g[i][None, :], out_dtype=BF16, out_scale=mla_scale * LOG2E)
    bq_p = _projection(proj_p, 0, N_SEG_P, 1024, 3, B_Q_RANK, w_uq, name="mla_q_up_ctx", **q_kw)
    bq_l = _projection(proj_l, 0, DEC_BATCH, 1024, 3, B_Q_RANK, w_uq, rope=rope64, sh=B_ROPE_DIM // 4,
                       rope_groups={2: all_g}, name="mla_q_up_lat", **q_kw)
    w_ukv = ev_w_ukv[i].astype(BF16)
    kv_kw = dict(prologue="cast", out_dtype=BF16)
    kv_p = _projection(proj_p, 0, N_SEG_P, 512, 8, B_KV_RANK, w_ukv, name="mla_kv_up_ctx", **kv_kw)
    kv_l = _projection(proj_l, 0, DEC_BATCH, 512, 8, B_KV_RANK, w_ukv, name="mla_kv_up_lat", **kv_kw)
    kv_ctx = _projection(cache_b_ckv[:, i].reshape(DEC_BATCH * PAST_LEN, B_KV_RANK), 0, 1, 512, 0, B_KV_RANK, w_ukv,
                         name="mla_kv_up_cache", **kv_kw)
    kv_ctx = kv_ctx.reshape(DEC_BATCH, PAST_LEN, 2 * N_HEADS * HEAD_DIM)
    krc = cache_b_krope[:, i]
    zc = jnp.zeros_like(krc)
    kr_ctx = jnp.concatenate([krc, zc, zc, krc], axis=-1)
    ak_ctx = cache_a_k[:, i].reshape(DEC_BATCH, PAST_LEN, N_HEADS * HEAD_DIM)
    av_ctx = cache_a_v[:, i].reshape(DEC_BATCH, PAST_LEN, N_HEADS * HEAD_DIM)
    lam_init = 0.8 - 0.6 * math.exp(-0.3 * l)
    o = jnp.zeros((N_TOK, D_MODEL), BF16)
    for latent, proj, b_q, kv in ((False, proj_p, bq_p, kv_p), (True, proj_l, bq_l, kv_l)):
        tag = "lat" if latent else "ctx"
        o = _attention(kind="diff", latent=latent, q_parts=[(proj, lambda h: h)], ko_parts=[(proj, lambda h: 8 + h)],
                       vo=(proj, lambda h: 16 + h), kc_parts=[(ak_ctx, lambda h: h)], vc=(av_ctx, lambda h: h),
                       lam=ev_lambda[i], subln=ev_subln_g[i][None, :], o_arr=o, o_blk=lambda h: h,
                       scale=A_QK_DIM ** -0.5, lam_init=lam_init, name="diff_attn_" + tag)
        o = _attention(kind="mla", latent=latent, q_parts=[(b_q, lambda h: h), (b_q, lambda h: 8 + h // 2)],
                       ko_parts=[(kv, lambda h: 2 * h), (proj, lambda h: 30 + h % 2)], vo=(kv, lambda h: 2 * h + 1),
                       kc_parts=[(kv_ctx, lambda h: 2 * h), (kr_ctx, lambda h: h % 2)], vc=(kv_ctx, lambda h: 2 * h + 1),
                       o_arr=o, o_blk=lambda h: 8 + h, scale=(HEAD_DIM + B_ROPE_DIM) ** -0.5, name="mla_attn_" + tag)
    y = _out_projection(o, ev_w_out[i].astype(BF16), x, mt)
    states = (st_ak.reshape(BATCH, SEQ, N_HEADS, HEAD_DIM), st_av.reshape(BATCH, SEQ, N_HEADS, HEAD_DIM),
              st_ckv.reshape(BATCH, SEQ, B_KV_RANK), st_kr.reshape(BATCH, SEQ, B_ROPE_DIM))
    return y, states


def _odd_layer(x, mt, l, i, rope128, cache_c_k, cache_c_v, cache_d_k, cache_d_v, norm_mix_g, od_w_in, od_rpb,
               od_sink, od_w_out):
    w_in = od_w_in[i].astype(BF16)
    all_g = (0, 1, 2, 3)
    in_kw = dict(prologue="modulate", g=norm_mix_g[l][None, :])
    head_src = lambda t0: tuple((t0 + t, gq, 4 * t + gq) for t in range(2) for gq in range(4))
    proj_p, st_ck, st_cv, st_dk, st_dv = _projection(
        x, 0, N_SEG_P, D_MODEL, 0, D_MODEL, w_in, mt=mt[:N_SEG_P], name="odd_in_proj_ctx",
        states=((N_HEADS, head_src(2)), (N_HEADS, head_src(4)), (D_KV_HEADS, ((8, 0, 0), (8, 1, 1))),
                (D_KV_HEADS, ((8, 2, 0), (8, 3, 1)))), **in_kw)
    proj_l = _projection(x, N_SEG_P, DEC_BATCH, D_MODEL, 0, D_MODEL, w_in, mt=mt[N_SEG_P:], rope=rope128,
                         sh=HEAD_DIM // 4, rope_groups={6: all_g, 7: all_g, 8: (0, 1)}, name="odd_in_proj_lat", **in_kw)
    ck_ctx = cache_c_k[:, i].reshape(DEC_BATCH, PAST_LEN, N_HEADS * HEAD_DIM)
    cv_ctx = cache_c_v[:, i].reshape(DEC_BATCH, PAST_LEN, N_HEADS * HEAD_DIM)
    dk_ctx = cache_d_k[:, i].reshape(DEC_BATCH, PAST_LEN, D_KV_HEADS * HEAD_DIM)
    dv_ctx = cache_d_v[:, i].reshape(DEC_BATCH, PAST_LEN, D_KV_HEADS * HEAD_DIM)
    bias = _neighbourhood_bias(od_rpb[i])
    sink = od_sink[i].astype(F32)
    o = jnp.zeros((N_TOK, D_MODEL), BF16)
    for latent, proj in ((False, proj_p), (True, proj_l)):
        tag = "lat" if latent else "ctx"
        o = _attention(kind="plain", latent=latent, q_parts=[(proj, lambda h: h)], ko_parts=[(proj, lambda h: 8 + h)],
                       vo=(proj, lambda h: 16 + h), kc_parts=[(ck_ctx, lambda h: h)], vc=(cv_ctx, lambda h: h),
                       bias=bias if latent else None, o_arr=o, o_blk=lambda h: h, scale=HEAD_DIM ** -0.5,
                       name="nbr_attn_" + tag)
        o = _attention(kind="plain", latent=latent, q_parts=[(proj, lambda h: 24 + h)],
                       ko_parts=[(proj, lambda h: 32 + h // D_GROUP)], vo=(proj, lambda h: 34 + h // D_GROUP),
                       kc_parts=[(dk_ctx, lambda h: h // D_GROUP)], vc=(dv_ctx, lambda h: h // D_GROUP), sink=sink,
                       o_arr=o, o_blk=lambda h: 8 + h, scale=HEAD_DIM ** -0.5, has_band=latent, tq_lat=256,
                       name="win_attn_" + tag)
    y = _out_projection(o, od_w_out[i].astype(BF16), x, mt)
    states = (st_ck.reshape(BATCH, SEQ, N_HEADS, HEAD_DIM), st_cv.reshape(BATCH, SEQ, N_HEADS, HEAD_DIM),
              st_dk.reshape(BATCH, SEQ, D_KV_HEADS, HEAD_DIM), st_dv.reshape(BATCH, SEQ, D_KV_HEADS, HEAD_DIM))
    return y, states


def _moe_layer(x, mt, g, router_w, router_b, layer, w_gate, w_up, w_down, final_g, final):
    h2, eidx, gates = _router(x, mt, g, router_w, router_b)
    src_tok, te, n_active, pos = _route_plan(eidx)
    ys = _moe_experts(h2, src_tok, te, n_active, layer, w_gate, w_up, w_down)
    ysel = ys.at[pos].get(mode="promise_in_bounds").reshape(2, N_TOK, D_MODEL // 2)
    out = _combine(x, ysel, gates.T, mt, final_g, final)
    return out if final else out[0]


def kernel(x_prompt, x_sample, cache_a_k, cache_a_v, cache_b_ckv, cache_b_krope, cache_c_k, cache_c_v, cache_d_k, cache_d_v, c, c_ctx, w_ada, b_ada, norm_mix_g, norm_ffn_g, ev_w_in, ev_lambda, ev_subln_g, ev_q_norm_g, ev_kv_norm_g, ev_w_uq, ev_w_ukv, ev_w_out, od_w_in, od_rpb, od_sink, od_w_out, router_w, router_b, moe_w_gate, moe_w_up, moe_w_down, final_g):
    x = jnp.concatenate([x_prompt.reshape(N_PROMPT, D_MODEL), x_sample.reshape(DEC_BATCH * DEC_SEQ, D_MODEL)], axis=0)
    cond8 = jnp.concatenate([c_ctx[None, :], c, jnp.zeros((3, D_MODEL), F32)], axis=0)
    mod = _adaln(cond8, w_ada, b_ada)
    seg_row = np.array([0] * N_SEG_P + [1 + b for b in range(DEC_BATCH)])
    mt_all = mod[:, seg_row].reshape(DEPTH, N_SEG, 6, D_MODEL)
    rope64 = _rope_tables(A_QK_DIM)
    rope128 = _rope_tables(HEAD_DIM)
    even_states, odd_states = [], []
    for l in range(DEPTH):
        i = l // 2
        mt = mt_all[l]
        if l % 2 == 0:
            x, st = _even_layer(x, mt, l, i, rope64, cache_a_k, cache_a_v, cache_b_ckv, cache_b_krope, norm_mix_g,
                                ev_w_in, ev_lambda, ev_subln_g, ev_q_norm_g, ev_kv_norm_g, ev_w_uq, ev_w_ukv, ev_w_out)
            even_states.append(st)
        else:
            x, st = _odd_layer(x, mt, l, i, rope128, cache_c_k, cache_c_v, cache_d_k, cache_d_v, norm_mix_g,
                               od_w_in, od_rpb, od_sink, od_w_out)
            odd_states.append(st)
        x = _moe_layer(x, mt, norm_ffn_g[l][None, :], router_w, router_b, l, moe_w_gate, moe_w_up, moe_w_down,
                       final_g[None, :], final=(l == DEPTH - 1))
    y_prompt = x[0].reshape(BATCH, SEQ, D_MODEL)
    y_sample = x[1].reshape(DEC_BATCH, DEC_SEQ, D_MODEL)
    new_even = tuple(jnp.stack([st[k] for st in even_states], axis=1) for k in range(4))
    new_odd = tuple(jnp.stack([st[k] for st in odd_states], axis=1) for k in range(4))
    return (y_prompt, y_sample) + new_even + new_odd
```

```python
import functools
import math

import numpy as np
import jax
import jax.numpy as jnp
from jax import lax
from jax.experimental import pallas as pl
from jax.experimental.pallas import tpu as pltpu

D_MODEL = 2048
BATCH = 32
SEQ = 256
DEPTH = 2
DEC_BATCH = 4
DEC_SEQ = 1024
PAST_LEN = 256
GRID_W = 64
HEAD_DIM = 128
N_HEADS = 8
A_QK_DIM = 64
B_Q_RANK = 768
B_KV_RANK = 512
B_ROPE_DIM = 64
NA_ROWS = 8
NA_COLS = 16
D_KV_HEADS = 2
D_GROUP = 4
D_WINDOW = 128
N_EXPERTS = 16
N_GROUPS = 4
D_EXPERT = 1024
ROPE_THETA = 10000.0
EPS = 1e-6
NEG = -1e30
LOG2E = 1.4426950408889634

SEG = 1024
N_PROMPT = BATCH * SEQ
N_TOK = N_PROMPT + DEC_BATCH * DEC_SEQ
N_SEG = N_TOK // SEG
N_SEG_P = N_PROMPT // SEG
PROJ_W = 4608
TN = 512
LANES = 128
MOE_TM = 256
MOE_ROWS = 2 * N_TOK + N_EXPERTS * MOE_TM
MOE_TILES = MOE_ROWS // MOE_TM

F32 = jnp.float32
BF16 = jnp.bfloat16


def _cparams(sem, vmem_mb):
    return pltpu.CompilerParams(dimension_semantics=sem, vmem_limit_bytes=vmem_mb * 1024 * 1024)


def _dot(a, b):
    return jnp.dot(a, b, preferred_element_type=F32)


def _dot_nt(a, b):
    return lax.dot_general(a, b, (((1,), (1,)), ((), ())), preferred_element_type=F32)


def _sigmoid(x):
    return 1.0 / (1.0 + jnp.exp(-x))


def _pack_bf16_pair(x):
    c = x.shape[1] // 2
    bits = pltpu.bitcast(x.astype(BF16).astype(F32), jnp.int32)
    return bits[:, :c] | lax.shift_right_logical(bits[:, c:], 16)


def _unpack_bf16_pair(w):
    hi = pltpu.bitcast(w & jnp.int32(-65536), F32)
    lo = pltpu.bitcast(lax.shift_left(w, 16), F32)
    return hi, lo


def _adaln_kernel(c_ref, w_ref, b_ref, o_ref):
    c = c_ref[...]
    a = (c * _sigmoid(c)).astype(BF16)
    o_ref[...] = _dot(a, w_ref[...].astype(BF16)) + b_ref[...]


def _adaln(cond8, w_ada, b_ada):
    tn = 1024
    n = 6 * D_MODEL
    return pl.pallas_call(
        _adaln_kernel,
        grid=(DEPTH, n // tn),
        in_specs=[
            pl.BlockSpec((8, D_MODEL), lambda l, j: (0, 0)),
            pl.BlockSpec((None, D_MODEL, tn), lambda l, j: (l, 0, j)),
            pl.BlockSpec((None, 1, tn), lambda l, j: (l, 0, j)),
        ],
        out_specs=pl.BlockSpec((None, 8, tn), lambda l, j: (l, 0, j)),
        out_shape=jax.ShapeDtypeStruct((DEPTH, 8, n), F32),
        compiler_params=_cparams(("arbitrary", "arbitrary"), 40),
        name="adaln",
    )(cond8, w_ada, b_ada.reshape(DEPTH, 1, n))


def _rope(a, cos, sp, sm, sh):
    return a * cos + pltpu.roll(a, sh, 1) * sp + pltpu.roll(a, LANES - sh, 1) * sm


def _proj_kernel(*refs, prologue, k_in, sh, n_tiles, rope_groups, norm_tile, states, out_scale):
    x_ref, mt_ref, g_ref, w_ref, cos_ref, sp_ref, sm_ref, ng_ref, o_ref = refs[:9]
    st_refs = refs[9:-1]
    xn_ref = refs[-1]
    j = pl.program_id(1)

    @pl.when(j == 0)
    def _():
        x = x_ref[:, :k_in]
        if prologue == "cast":
            xn_ref[...] = x.astype(BF16)
        else:
            y = x * lax.rsqrt(jnp.mean(x * x, axis=-1, keepdims=True) + EPS) * g_ref[...]
            if prologue == "modulate":
                y = y * (1.0 + mt_ref[0, 1:2, :]) + mt_ref[0, 0:1, :]
            xn_ref[...] = y.astype(BF16)

    acc = _dot(xn_ref[...], w_ref[...])
    if out_scale != 1.0:
        acc = acc * out_scale
    n_grp = acc.shape[1] // LANES

    def treatment(t):
        acts = tuple((k, a, b) for k, st in enumerate(states) for (tt, a, b) in st[1] if tt == t)
        return (tuple(rope_groups.get(t, ())), t == norm_tile, acts)

    branches = {}
    for t in range(n_tiles):
        branches.setdefault(treatment(t), []).append(t)

    o_ref[...] = acc.astype(o_ref.dtype)
    reread = o_ref.dtype == F32

    for (rg, is_norm, acts), tiles in branches.items():
        if not rg and not is_norm and not acts:
            continue
        cond = j == tiles[0]
        for t in tiles[1:]:
            cond = jnp.logical_or(cond, j == t)

        @pl.when(cond)
        def _(rg=rg, is_norm=is_norm, acts=acts):
            vals = [(o_ref if reread else acc)[:, g * LANES:(g + 1) * LANES] for g in range(n_grp)]
            if is_norm:
                ssq = None
                for v in vals:
                    s = jnp.sum(v * v, axis=-1, keepdims=True)
                    ssq = s if ssq is None else ssq + s
                scale = lax.rsqrt(ssq * (1.0 / (n_grp * LANES)) + EPS)
                ng = ng_ref[...]
                vals = [v * scale * ng[:, g * LANES:(g + 1) * LANES] for g, v in enumerate(vals)]
            elif rg:
                cos, sp, sm = cos_ref[...], sp_ref[...], sm_ref[...]
                vals = [_rope(v, cos, sp, sm, sh) if g in rg else v for g, v in enumerate(vals)]
            if is_norm or rg:
                for g, v in enumerate(vals):
                    if is_norm or g in rg:
                        o_ref[:, g * LANES:(g + 1) * LANES] = v.astype(o_ref.dtype)
            for k, a, b in acts:
                n_heads = states[k][0]
                if n_heads:
                    st_refs[k][pl.ds(b, SEG, stride=n_heads), :] = vals[a]
                elif b % LANES == 0:
                    for q in range(b // LANES):
                        st_refs[k][:, q * LANES:(q + 1) * LANES] = vals[a // LANES + q]
                else:
                    st_refs[k][...] = vals[a // LANES][:, a % LANES:a % LANES + b]


def _projection(x, row_off, n_seg, x_blk_w, x_blk_idx, k_in, w, *, prologue, mt=None, g=None, rope=None, sh=16,
                rope_groups=None, norm_tile=None, norm_g=None, states=(), out_dtype=F32, out_scale=1.0, name):
    n = w.shape[1]
    m = n_seg * SEG
    assert n % TN == 0 and w.shape[0] == k_in
    if mt is None:
        mt = jnp.zeros((n_seg, 6, LANES), F32)
    if g is None:
        g = jnp.ones((1, k_in), F32)
    if rope is None:
        rope = tuple(jnp.zeros((8, LANES), F32) for _ in range(3))
    if norm_g is None:
        norm_g = jnp.ones((1, TN), F32)
    kern = functools.partial(_proj_kernel, prologue=prologue, k_in=k_in, sh=sh, n_tiles=n // TN,
                             rope_groups=dict(rope_groups or {}), norm_tile=norm_tile, states=tuple(states),
                             out_scale=float(out_scale))
    full2 = lambda i, j: (0, 0)
    out_shape = [jax.ShapeDtypeStruct((m, n), out_dtype)]
    out_specs = [pl.BlockSpec((SEG, TN), lambda i, j: (i, j))]
    for n_heads, src in states:
        if n_heads:
            out_shape.append(jax.ShapeDtypeStruct((m * n_heads, LANES), F32))
            out_specs.append(pl.BlockSpec((SEG * n_heads, LANES), lambda i, j: (i, 0)))
        else:
            width = src[0][2]
            out_shape.append(jax.ShapeDtypeStruct((m, width), F32))
            out_specs.append(pl.BlockSpec((SEG, width), lambda i, j: (i, 0)))
    x_kw = dict(pipeline_mode=pl.Buffered(1)) if states else {}
    res = pl.pallas_call(
        kern,
        grid=(n_seg, n // TN),
        in_specs=[
            pl.BlockSpec((SEG, x_blk_w), lambda i, j: (i + row_off, x_blk_idx), **x_kw),
            pl.BlockSpec((1, 6, mt.shape[2]), lambda i, j: (i, 0, 0)),
            pl.BlockSpec((1, k_in), full2),
            pl.BlockSpec((k_in, TN), lambda i, j: (0, j)),
            pl.BlockSpec(rope[0].shape, full2),
            pl.BlockSpec(rope[1].shape, full2),
            pl.BlockSpec(rope[2].shape, full2),
            pl.BlockSpec((1, TN), full2),
        ],
        out_specs=out_specs,
        out_shape=out_shape,
        scratch_shapes=[pltpu.VMEM((SEG, k_in), BF16)],
        compiler_params=_cparams(("arbitrary", "arbitrary"), 56 if states else 48),
        name=name,
    )(x, mt, g, w, *rope, norm_g)
    return res if states else res[0]


def _attn_kernel(*refs, kind, n_parts, has_ctx, has_bias, has_band, has_sink, nb, s_len, tq, q_scale, lam_init,
                 own_win):
    it = iter(refs)
    q_refs = [next(it) for _ in range(n_parts)]
    ko_refs = [next(it) for _ in range(n_parts)]
    vo_ref = next(it)
    kc_refs, vc_ref = [], None
    if has_ctx:
        kc_refs = [next(it) for _ in range(n_parts)]
        vc_ref = next(it)
    bias_ref = next(it) if has_bias else None
    lam_ref = sg_ref = None
    if kind == "diff":
        lam_ref = next(it)
        sg_ref = next(it)
    sink_ref = next(it) if has_sink else None
    o_ref = next(it)

    sink2 = sink_ref[pl.program_id(1)] * LOG2E if has_sink else None
    lane = lax.broadcasted_iota(jnp.int32, (1, LANES), 1)

    def softmax_pv(s_list, v_list):
        m = None
        for s in s_list:
            mm = jnp.max(s, axis=-1, keepdims=True)
            m = mm if m is None else jnp.maximum(m, mm)
        if sink2 is not None:
            m = jnp.maximum(m, sink2)
        l = jnp.exp2(sink2 - m) if sink2 is not None else None
        o = None
        for s, v in zip(s_list, v_list):
            e = jnp.exp2(s - m)
            ss = jnp.sum(e, axis=-1, keepdims=True)
            l = ss if l is None else l + ss
            pv = _dot(e.astype(BF16), v)
            o = pv if o is None else o + pv
        return o, l

    for bi in range(nb):
        qrows = pl.ds(bi * tq, tq)
        q0 = pl.program_id(2) * tq
        if own_win is None:
            krows, k0 = pl.ds(bi * s_len, s_len), 0
            n_own = s_len
        else:
            k0 = pl.multiple_of(jnp.clip(q0 - D_WINDOW, 0, s_len - own_win), LANES)
            krows = pl.ds(k0, own_win)
            n_own = own_win
        k_srcs, v_srcs, is_own = [], [], []
        if has_ctx:
            k_srcs.append([r[...].astype(BF16) for r in kc_refs])
            v_srcs.append(vc_ref[...].astype(BF16))
            is_own.append(False)
        k_srcs.append([r[krows, :].astype(BF16) for r in ko_refs])
        v_srcs.append(vo_ref[krows, :].astype(BF16))
        is_own.append(True)

        def mask_own(s):
            if has_bias:
                s = s + bias_ref[...]
            if has_band:
                qpos = q0 + lax.broadcasted_iota(jnp.int32, (tq, 1), 0)
                kpos = k0 + lax.broadcasted_iota(jnp.int32, (1, n_own), 1)
                s = jnp.where(jnp.abs(kpos - qpos) <= D_WINDOW, s, NEG)
            return s

        if kind == "diff":
            q = q_refs[0][qrows, :] * q_scale
            q1 = jnp.where(lane < A_QK_DIM, q, 0.0).astype(BF16)
            q2 = jnp.where(lane < A_QK_DIM, 0.0, q).astype(BF16)
            o1, l1 = softmax_pv([_dot_nt(q1, ks[0]) for ks in k_srcs], v_srcs)
            o2, l2 = softmax_pv([_dot_nt(q2, ks[0]) for ks in k_srcs], v_srcs)
            lv = lam_ref[...]
            lam = (jnp.exp(jnp.sum(lv[0:1] * lv[1:2], axis=-1, keepdims=True))
                   - jnp.exp(jnp.sum(lv[2:3] * lv[3:4], axis=-1, keepdims=True)) + lam_init)
            o = o1 * (1.0 / l1) - o2 * (lam / l2)
            o = o * lax.rsqrt(jnp.mean(o * o, axis=-1, keepdims=True) + EPS) * sg_ref[...] * (1.0 - lam_init)
        else:
            qs = [r[qrows, :] for r in q_refs]
            qs = [(qp * q_scale).astype(BF16) if qp.dtype == F32 else qp for qp in qs]
            s_list = []
            for ks, own in zip(k_srcs, is_own):
                s = None
                for qp, kp in zip(qs, ks):
                    d = _dot_nt(qp, kp)
                    s = d if s is None else s + d
                s_list.append(mask_own(s) if own else s)
            o, l = softmax_pv(s_list, v_srcs)
            o = o * (1.0 / l)
        o_ref[qrows, :] = o.astype(o_ref.dtype)


def _attention(*, kind, latent, q_parts, ko_parts, vo, kc_parts=(), vc=None, bias=None, lam=None, subln=None,
               sink=None, o_arr, o_blk, scale, lam_init=0.0, has_band=False, tq_lat=512, name):
    n_parts = len(q_parts)
    has_ctx = latent
    if latent:
        tq, s_len, nb = tq_lat, DEC_SEQ, 1
        grid = (DEC_BATCH, N_HEADS, DEC_SEQ // tq)
        qpb = DEC_SEQ // tq
        q_row = lambda b, h, t: b * qpb + t
        o_row = lambda b, h, t: N_PROMPT // tq + b * qpb + t
        k_row = lambda b, h, t: b
        qblk = tq
    else:
        tq, s_len, nb = SEQ, SEQ, SEG // SEQ
        grid = (N_SEG_P, N_HEADS, 1)
        q_row = o_row = k_row = lambda b, h, t: b
        qblk = SEG
    args, specs = [], []

    def add(arr, shape, imap, **kw):
        args.append(arr)
        specs.append(pl.BlockSpec(shape, imap, **kw))

    for arr, f in q_parts:
        add(arr, (qblk, LANES), lambda b, h, t, f=f: (q_row(b, h, t), f(h)))
    for arr, f in list(ko_parts) + [vo]:
        add(arr, (SEG, LANES), lambda b, h, t, f=f: (k_row(b, h, t), f(h)))
    if has_ctx:
        for arr, f in list(kc_parts) + [vc]:
            add(arr, (None, PAST_LEN, LANES), lambda b, h, t, f=f: (b, 0, f(h)))
    if bias is not None:
        add(bias, (None, tq, DEC_SEQ), lambda b, h, t: (h, t, 0))
    if kind == "diff":
        add(lam, lam.shape, lambda b, h, t: (0, 0))
        add(subln, subln.shape, lambda b, h, t: (0, 0))
    if sink is not None:
        args.append(sink)
        specs.append(pl.BlockSpec(memory_space=pltpu.SMEM))
    n_in = len(args)
    args.append(o_arr)
    specs.append(pl.BlockSpec(memory_space=pl.ANY))
    kern = functools.partial(_attn_kernel_aliased, kind=kind, n_parts=n_parts, has_ctx=has_ctx,
                             has_bias=bias is not None, has_band=has_band, has_sink=sink is not None,
                             nb=nb, s_len=s_len, tq=tq, q_scale=scale * LOG2E, lam_init=lam_init,
                             own_win=(tq + 2 * D_WINDOW) if has_band else None)
    return pl.pallas_call(
        kern,
        grid=grid,
        in_specs=specs,
        out_specs=pl.BlockSpec((qblk, LANES), lambda b, h, t: (o_row(b, h, t), o_blk(h))),
        out_shape=jax.ShapeDtypeStruct(o_arr.shape, o_arr.dtype),
        input_output_aliases={n_in: 0},
        compiler_params=_cparams(("arbitrary", "arbitrary", "arbitrary"), 48),
        name=name,
    )(*args)


def _attn_kernel_aliased(*refs, **kw):
    _attn_kernel(*refs[:-2], refs[-1], **kw)


def _outproj_kernel(o_ref, w_ref, x_ref, mt_ref, y_ref):
    y_ref[...] = x_ref[...] + mt_ref[0, 2:3, :] * _dot(o_ref[...], w_ref[...])


def _out_projection(o, w, x, mt):
    return pl.pallas_call(
        _outproj_kernel,
        grid=(N_SEG, D_MODEL // TN),
        in_specs=[
            pl.BlockSpec((SEG, D_MODEL), lambda i, j: (i, 0)),
            pl.BlockSpec((D_MODEL, TN), lambda i, j: (0, j)),
            pl.BlockSpec((SEG, TN), lambda i, j: (i, j)),
            pl.BlockSpec((1, 6, TN), lambda i, j: (i, 0, j)),
        ],
        out_specs=pl.BlockSpec((SEG, TN), lambda i, j: (i, j)),
        out_shape=jax.ShapeDtypeStruct((N_TOK, D_MODEL), F32),
        compiler_params=_cparams(("arbitrary", "arbitrary"), 40),
        name="out_proj",
    )(o, w, x, mt)


def _router_kernel(x_ref, mt_ref, g_ref, wh_ref, wl_ref, rb_ref, h_ref, e_ref, gt_ref):
    x = x_ref[...]
    y = x * lax.rsqrt(jnp.mean(x * x, axis=-1, keepdims=True) + EPS) * g_ref[...]
    h = y * (1.0 + mt_ref[0, 4:5, :]) + mt_ref[0, 3:4, :]
    hh = h.astype(BF16)
    h_ref[...] = _pack_bf16_pair(h)
    hl = (h - hh.astype(F32)).astype(BF16)
    wh, wl = wh_ref[...], wl_ref[...]
    logits = _dot_nt(wh, hh) + _dot_nt(wh, hl) + _dot_nt(wl, hh)
    scores = _sigmoid(logits)
    sel = scores + rb_ref[...]
    per = N_EXPERTS // N_GROUPS
    sc = [scores[e:e + 1, :] for e in range(N_EXPERTS)]
    sl = [sel[e:e + 1, :] for e in range(N_EXPERTS)]
    best_g, best_v = None, None
    for gi in range(N_GROUPS):
        a, b, c, d = sl[gi * per:(gi + 1) * per]
        hi1, lo1, hi2, lo2 = jnp.maximum(a, b), jnp.minimum(a, b), jnp.maximum(c, d), jnp.minimum(c, d)
        gs = jnp.maximum(hi1, hi2) + jnp.maximum(jnp.minimum(hi1, hi2), jnp.maximum(lo1, lo2))
        if gi == 0:
            best_g, best_v = jnp.zeros_like(gs, dtype=jnp.int32), gs
        else:
            better = gs > best_v
            best_g = jnp.where(better, gi, best_g)
            best_v = jnp.where(better, gs, best_v)
    masked = [jnp.where(best_g == (e // per), sl[e], NEG) for e in range(N_EXPERTS)]
    i1, v1 = jnp.zeros_like(best_g), masked[0]
    for e in range(1, N_EXPERTS):
        better = masked[e] > v1
        i1 = jnp.where(better, e, i1)
        v1 = jnp.where(better, masked[e], v1)
    i2, v2 = None, None
    for e in range(N_EXPERTS):
        cand = jnp.where(i1 == e, -2e30, masked[e])
        if e == 0:
            i2, v2 = jnp.zeros_like(best_g), cand
        else:
            better = cand > v2
            i2 = jnp.where(better, e, i2)
            v2 = jnp.where(better, cand, v2)
    g1 = jnp.zeros_like(v1)
    g2 = jnp.zeros_like(v1)
    for e in range(N_EXPERTS):
        g1 = jnp.where(i1 == e, sc[e], g1)
        g2 = jnp.where(i2 == e, sc[e], g2)
    tot = g1 + g2
    rows = i1.shape[1]
    e_ref[...] = jnp.concatenate([i1, i2, jnp.zeros((6, rows), jnp.int32)], axis=0)
    gt_ref[...] = jnp.concatenate([g1 / tot, g2 / tot, jnp.zeros((6, rows), F32)], axis=0)


def _router(x, mt, g, router_w, router_b):
    wt = router_w.T
    wh = wt.astype(BF16)
    wl = (wt - wh.astype(F32)).astype(BF16)
    return pl.pallas_call(
        _router_kernel,
        grid=(N_SEG,),
        in_specs=[
            pl.BlockSpec((SEG, D_MODEL), lambda i: (i, 0)),
            pl.BlockSpec((1, 6, D_MODEL), lambda i: (i, 0, 0)),
            pl.BlockSpec((1, D_MODEL), lambda i: (0, 0)),
            pl.BlockSpec((N_EXPERTS, D_MODEL), lambda i: (0, 0)),
            pl.BlockSpec((N_EXPERTS, D_MODEL), lambda i: (0, 0)),
            pl.BlockSpec((N_EXPERTS, 1), lambda i: (0, 0)),
        ],
        out_specs=[
            pl.BlockSpec((SEG, D_MODEL // 2), lambda i: (i, 0)),
            pl.BlockSpec((8, SEG), lambda i: (0, i)),
            pl.BlockSpec((8, SEG), lambda i: (0, i)),
        ],
        out_shape=[
            jax.ShapeDtypeStruct((N_TOK, D_MODEL // 2), jnp.int32),
            jax.ShapeDtypeStruct((8, N_TOK), jnp.int32),
            jax.ShapeDtypeStruct((8, N_TOK), F32),
        ],
        compiler_params=_cparams(("arbitrary",), 48),
        name="ffn_norm_router",
    )(x, mt, g, wh, wl, router_b.reshape(N_EXPERTS, 1))


def _expert_weights(te_ref, first_ref, slot_ref, nxt_ref, i, t, layer, w_hbm, stage_ref, sem_ref, w_bf_refs):
    def copies(e, s):
        return [pltpu.make_async_copy(w.at[layer, e], stage_ref.at[s, k], sem_ref.at[s, k])
                for k, w in enumerate(w_hbm)]

    s = slot_ref[t]

    @pl.when(i == 0)
    def _():
        for c in copies(te_ref[t], s):
            c.start()

    @pl.when(first_ref[t] == 1)
    def _():
        for c in copies(te_ref[t], s):
            c.wait()

        @pl.when(nxt_ref[t] >= 0)
        def _():
            for c in copies(nxt_ref[t], 1 - s):
                c.start()

        for k, w_bf in enumerate(w_bf_refs):
            w_bf[...] = stage_ref[s, k].astype(BF16)


def _moe_up_kernel(te_ref, na_ref, first_ref, slot_ref, nxt_ref, xs_ref, wg_hbm, wu_hbm, hid_ref,
                   stage_ref, wgb_ref, wub_ref, sem_ref, *, tile0, layer):
    i = pl.program_id(0)
    t = i + tile0
    _expert_weights(te_ref, first_ref, slot_ref, nxt_ref, i, t, layer, (wg_hbm, wu_hbm), stage_ref, sem_ref,
                    (wgb_ref, wub_ref))

    @pl.when(t < na_ref[0])
    def _():
        x_hi, x_lo = _unpack_bf16_pair(xs_ref[...])
        x_hi, x_lo = x_hi.astype(BF16), x_lo.astype(BF16)
        half = D_MODEL // 2
        g = _dot(x_hi, wgb_ref[:half, :]) + _dot(x_lo, wgb_ref[half:, :])
        u = _dot(x_hi, wub_ref[:half, :]) + _dot(x_lo, wub_ref[half:, :])
        hid_ref[...] = (g * _sigmoid(g) * u).astype(BF16)

    @pl.when(t >= na_ref[0])
    def _():
        hid_ref[...] = jnp.zeros_like(hid_ref)


def _moe_down_kernel(te_ref, na_ref, first_ref, slot_ref, nxt_ref, *rest, layer, chunk_tiles):
    ys_ref, stage_ref, wdb_ref, sem_ref = rest[-4:]
    wd_hbm = rest[-5]
    hid_refs = rest[:-5]
    t = pl.program_id(0)
    _expert_weights(te_ref, first_ref, slot_ref, nxt_ref, t, t, layer, (wd_hbm,), stage_ref, sem_ref, (wdb_ref,))

    for c, hid_ref in enumerate(hid_refs):
        @pl.when(jnp.logical_and(t < na_ref[0], t // chunk_tiles == c))
        def _(hid_ref=hid_ref):
            ys_ref[...] = _pack_bf16_pair(_dot(hid_ref[...], wdb_ref[...]))

    @pl.when(t >= na_ref[0])
    def _():
        ys_ref[...] = jnp.zeros_like(ys_ref)


MOE_CHUNKS = 2


def _moe_experts(h2, src_tok, tile_expert, n_active, layer, w_gate, w_up, w_down):
    tiles = MOE_TILES // MOE_CHUNKS
    rows = tiles * MOE_TM

    def weight_runs(call_tiles):
        t_ids = jnp.arange(MOE_TILES, dtype=jnp.int32)
        prev = jnp.concatenate([tile_expert[:1], tile_expert[:-1]])
        first = jnp.logical_or(t_ids % call_tiles == 0, tile_expert != prev)
        slot = ((jnp.cumsum(first.astype(jnp.int32)) - 1) % 2).astype(jnp.int32)
        first_at = lax.cummin(jnp.where(first, t_ids, MOE_TILES), reverse=True)
        next_at = jnp.concatenate([first_at[1:], jnp.full((1,), MOE_TILES, jnp.int32)])
        same_call = jnp.logical_and(next_at < MOE_TILES, next_at // call_tiles == t_ids // call_tiles)
        nxt = jnp.where(jnp.logical_and(first, same_call), tile_expert[jnp.minimum(next_at, MOE_TILES - 1)], -1)
        return (tile_expert, n_active, first.astype(jnp.int32), slot, nxt.astype(jnp.int32))

    row_blk = lambda i, *_: (i, 0)
    hbm = pl.BlockSpec(memory_space=pl.ANY)
    plan = weight_runs(tiles)
    hids = []
    for c in range(MOE_CHUNKS):
        xs = h2.at[src_tok[c * rows:(c + 1) * rows]].get(mode="promise_in_bounds")
        hids.append(pl.pallas_call(
            functools.partial(_moe_up_kernel, tile0=c * tiles, layer=layer),
            grid_spec=pltpu.PrefetchScalarGridSpec(
                num_scalar_prefetch=len(plan),
                grid=(tiles,),
                in_specs=[pl.BlockSpec((MOE_TM, D_MODEL // 2), row_blk), hbm, hbm],
                out_specs=pl.BlockSpec((MOE_TM, D_EXPERT), row_blk),
                scratch_shapes=[pltpu.VMEM((2, 2, D_MODEL, D_EXPERT), F32),
                                pltpu.VMEM((D_MODEL, D_EXPERT), BF16), pltpu.VMEM((D_MODEL, D_EXPERT), BF16),
                                pltpu.SemaphoreType.DMA((2, 2))],
            ),
            out_shape=jax.ShapeDtypeStruct((rows, D_EXPERT), BF16),
            compiler_params=_cparams(("arbitrary",), 52),
            name="moe_up",
        )(*plan, xs, w_gate, w_up))
    plan = weight_runs(MOE_TILES)
    hid_specs = [pl.BlockSpec((MOE_TM, D_EXPERT), lambda i, *_, c=c: (jnp.clip(i - c * tiles, 0, tiles - 1), 0))
                 for c in range(MOE_CHUNKS)]
    return pl.pallas_call(
        functools.partial(_moe_down_kernel, layer=layer, chunk_tiles=tiles),
        grid_spec=pltpu.PrefetchScalarGridSpec(
            num_scalar_prefetch=len(plan),
            grid=(MOE_TILES,),
            in_specs=hid_specs + [hbm],
            out_specs=pl.BlockSpec((MOE_TM, D_MODEL // 2), row_blk),
            scratch_shapes=[pltpu.VMEM((2, 1, D_EXPERT, D_MODEL), F32), pltpu.VMEM((D_EXPERT, D_MODEL), BF16),
                            pltpu.SemaphoreType.DMA((2, 1))],
        ),
        out_shape=jax.ShapeDtypeStruct((MOE_ROWS, D_MODEL // 2), jnp.int32),
        compiler_params=_cparams(("arbitrary",), 40),
        name="moe_down",
    )(*plan, *hids, w_down)


PLAN_R = 2 * N_TOK // LANES


def _plan_kernel(e_ref, pos_ref, meta_ref):
    e = e_ref[...]
    r_i = lax.broadcasted_iota(jnp.int32, (LANES, LANES), 0)
    c_i = lax.broadcasted_iota(jnp.int32, (LANES, LANES), 1)
    upper = jnp.where(r_i <= c_i, 1.0, 0.0).astype(BF16)
    r_j = lax.broadcasted_iota(jnp.int32, (PLAN_R, PLAN_R), 0)
    c_j = lax.broadcasted_iota(jnp.int32, (PLAN_R, PLAN_R), 1)
    lower = jnp.where(c_j < r_j, 1.0, 0.0).astype(BF16)
    lane = lax.broadcasted_iota(jnp.int32, (1, LANES), 1)
    pos = jnp.zeros((PLAN_R, LANES), F32)
    cnt_row = jnp.zeros((1, LANES), F32)
    off_row = jnp.zeros((1, LANES), F32)
    end_row = []
    row_off = jnp.zeros((1, 1), F32)
    tiles_done = jnp.zeros((1, 1), F32)
    for ex in range(N_EXPERTS):
        m = jnp.where(e == ex, 1.0, 0.0)
        inc = _dot(m.astype(BF16), upper)
        tot = jnp.broadcast_to(inc[:, LANES - 1:LANES], (PLAN_R, LANES))
        before = _dot(lower, tot.astype(BF16))
        cnt = jnp.sum(inc[:, LANES - 1:LANES], axis=0, keepdims=True)
        pos = pos + m * (row_off + before + inc - 1.0)
        tiles = jnp.floor((cnt + (MOE_TM - 1)) * (1.0 / MOE_TM))
        cnt_row = jnp.where(lane == ex, cnt, cnt_row)
        off_row = jnp.where(lane == ex, row_off, off_row)
        tiles_done = tiles_done + tiles
        end_row.append(tiles_done)
        row_off = row_off + tiles * MOE_TM
    n_active = tiles_done
    tid = jnp.minimum(lane.astype(F32), n_active - 1.0)
    te = jnp.zeros((1, LANES), F32)
    for ex in range(N_EXPERTS):
        te = te + jnp.where(end_row[ex] <= tid, 1.0, 0.0)
    pos_ref[...] = pos.astype(jnp.int32)
    meta = jnp.concatenate([cnt_row, off_row, te, jnp.broadcast_to(n_active, (1, LANES)),
                            jnp.zeros((4, LANES), F32)], axis=0)
    meta_ref[...] = meta.astype(jnp.int32)


def _route_plan(eidx):
    e2 = eidx[:2].reshape(PLAN_R, LANES)
    pos, meta = pl.pallas_call(
        _plan_kernel,
        out_shape=[jax.ShapeDtypeStruct((PLAN_R, LANES), jnp.int32), jax.ShapeDtypeStruct((8, LANES), jnp.int32)],
        compiler_params=pltpu.CompilerParams(vmem_limit_bytes=32 * 1024 * 1024),
        name="route_plan",
    )(e2)
    cnt, row_off = meta[0, :N_EXPERTS], meta[1, :N_EXPERTS]
    te, n_active = meta[2, :MOE_TILES], meta[3, :1]
    start = jnp.cumsum(cnt) - cnt
    order = jnp.argsort(e2.reshape(-1), stable=True).astype(jnp.int32)
    row_e = jnp.repeat(te, MOE_TM)
    rank = jnp.arange(MOE_ROWS, dtype=jnp.int32) - row_off[row_e]
    valid = rank < cnt[row_e]
    src_asg = order[jnp.clip(start[row_e] + rank, 0, 2 * N_TOK - 1)]
    rows = jnp.arange(MOE_ROWS, dtype=jnp.int32)
    src_tok = jnp.where(valid, src_asg % N_TOK, rows % N_TOK)
    return src_tok, te, n_active, pos.reshape(-1)


def _combine_kernel(x_ref, y0_ref, y1_ref, gt_ref, mt_ref, fg_ref, *o_refs, final, tm):
    if final:
        i = pl.program_id(0)
        for o_ref, cond in zip(o_refs, (i < N_PROMPT // tm, i >= N_PROMPT // tm)):
            pl.when(cond)(functools.partial(_combine_tile, x_ref, y0_ref, y1_ref, gt_ref, mt_ref, fg_ref, o_ref, True))
    else:
        _combine_tile(x_ref, y0_ref, y1_ref, gt_ref, mt_ref, fg_ref, o_refs[0], False)


def _combine_tile(x_ref, y0_ref, y1_ref, gt_ref, mt_ref, fg_ref, o_ref, final):
    gt = gt_ref[...]
    g0, g1 = gt[:, 0:1], gt[:, 1:2]
    half = D_MODEL // 2
    y0 = _unpack_bf16_pair(y0_ref[...])
    y1 = _unpack_bf16_pair(y1_ref[...])
    xs = []
    for c in range(2):
        cols = slice(c * half, (c + 1) * half)
        xs.append(x_ref[:, cols] + mt_ref[0, 5:6, cols] * (y0[c] * g0 + y1[c] * g1))
    if final:
        ssq = jnp.sum(xs[0] * xs[0], axis=-1, keepdims=True) + jnp.sum(xs[1] * xs[1], axis=-1, keepdims=True)
        r = lax.rsqrt(ssq * (1.0 / D_MODEL) + EPS)
        xs = [xc * r * fg_ref[:, c * half:(c + 1) * half] for c, xc in enumerate(xs)]
    for c, xc in enumerate(xs):
        o_ref[:, c * half:(c + 1) * half] = xc


def _combine(x, ysel, gates_t, mt, final_g, final):
    tm = 256
    n_p = N_PROMPT // tm
    if final:
        out_specs = [pl.BlockSpec((tm, D_MODEL), lambda i: (jnp.minimum(i, n_p - 1), 0)),
                     pl.BlockSpec((tm, D_MODEL), lambda i: (jnp.maximum(i - n_p, 0), 0))]
        out_shape = [jax.ShapeDtypeStruct((N_PROMPT, D_MODEL), F32),
                     jax.ShapeDtypeStruct((N_TOK - N_PROMPT, D_MODEL), F32)]
    else:
        out_specs = [pl.BlockSpec((tm, D_MODEL), lambda i: (i, 0))]
        out_shape = [jax.ShapeDtypeStruct((N_TOK, D_MODEL), F32)]
    return pl.pallas_call(
        functools.partial(_combine_kernel, final=final, tm=tm),
        grid=(N_TOK // tm,),
        in_specs=[
            pl.BlockSpec((tm, D_MODEL), lambda i: (i, 0)),
            pl.BlockSpec((None, tm, D_MODEL // 2), lambda i: (0, i, 0)),
            pl.BlockSpec((None, tm, D_MODEL // 2), lambda i: (1, i, 0)),
            pl.BlockSpec((tm, 8), lambda i: (i, 0)),
            pl.BlockSpec((1, 6, D_MODEL), lambda i: (i // (SEG // tm), 0, 0)),
            pl.BlockSpec((1, D_MODEL), lambda i: (0, 0)),
        ],
        out_specs=out_specs,
        out_shape=out_shape,
        compiler_params=_cparams(("arbitrary",), 40),
        name="moe_combine",
    )(x, ysel, ysel, gates_t, mt, final_g)


def _rope_tables(dim):
    half = dim // 2
    inv = ROPE_THETA ** (-jnp.arange(0, half, 2, dtype=F32) / half)
    t = jnp.arange(DEC_SEQ)
    ang_r = (t // GRID_W).astype(F32)[:, None] * inv[None, :]
    ang_c = (t % GRID_W).astype(F32)[:, None] * inv[None, :]
    ang = jnp.concatenate([ang_r, ang_r, ang_c, ang_c], axis=-1)
    cos, sin = jnp.cos(ang), jnp.sin(ang)
    reps = LANES // dim
    cos = jnp.tile(cos, (1, reps))
    sin = jnp.tile(sin, (1, reps))
    sh = dim // 4
    second = (np.arange(LANES) % (2 * sh)) >= sh
    sp = jnp.where(second[None, :], sin, 0.0)
    sm = jnp.where(second[None, :], 0.0, -sin)
    return cos, sp, sm


def _neighbourhood_bias(rpb):
    rows = DEC_SEQ // GRID_W
    kh = min(NA_ROWS, rows)
    r = np.arange(rows)
    r0 = np.clip(r - kh // 2, 0, rows - kh)
    kr = np.arange(rows)
    row_ok = (kr[None, :] >= r0[:, None]) & (kr[None, :] < r0[:, None] + kh)
    c = np.arange(GRID_W)
    ws = np.clip(c - NA_COLS // 2, 0, GRID_W - NA_COLS)
    kc = np.arange(GRID_W)
    col_ok = (kc[None, :] >= ws[:, None]) & (kc[None, :] < ws[:, None] + NA_COLS)
    dc_idx = np.clip(kc[None, :] - c[:, None], -(NA_COLS - 1), NA_COLS - 1) + NA_COLS - 1
    sel_c = ((np.arange(2 * NA_COLS - 1)[:, None, None] == dc_idx[None]) & col_ok[None]).astype(np.float32)
    t = jnp.einsum("hab,bcx->hacx", rpb.astype(F32), sel_c, precision=lax.Precision.HIGHEST)
    t = jnp.where(col_ok[None, None], t * LOG2E, NEG)
    t = jnp.concatenate([t, t], axis=-1)

    def build(t_ref, o_ref):
        left = lax.broadcasted_iota(jnp.int32, (1, LANES), 1) < GRID_W
        neg = jnp.full((GRID_W, LANES), NEG, F32)
        for rr in range(rows):
            for p in range(rows // 2):
                parts = [t_ref[k - rr + NA_ROWS - 1] if row_ok[rr, k] else neg for k in (2 * p, 2 * p + 1)]
                blk = jnp.where(left, parts[0], parts[1]) if (row_ok[rr, 2 * p] or row_ok[rr, 2 * p + 1]) else neg
                o_ref[rr * GRID_W:(rr + 1) * GRID_W, p * LANES:(p + 1) * LANES] = blk

    return pl.pallas_call(
        build,
        grid=(N_HEADS,),
        in_specs=[pl.BlockSpec((None, 2 * NA_ROWS - 1, GRID_W, LANES), lambda h: (h, 0, 0, 0))],
        out_specs=pl.BlockSpec((None, DEC_SEQ, DEC_SEQ), lambda h: (h, 0, 0)),
        out_shape=jax.ShapeDtypeStruct((N_HEADS, DEC_SEQ, DEC_SEQ), F32),
        compiler_params=_cparams(("arbitrary",), 32),
        name="nbr_bias",
    )(t)


def _even_layer(x, mt, l, i, rope64, cache_a_k, cache_a_v, cache_b_ckv, cache_b_krope, norm_mix_g, ev_w_in,
                ev_lambda, ev_subln_g, ev_q_norm_g, ev_kv_norm_g, ev_w_uq, ev_w_ukv, ev_w_out):
    w = ev_w_in[i]
    kr = w[:, 4352:4416]
    z = jnp.zeros_like(kr)
    w_in = jnp.concatenate([w[:, :3840], kr, z, z, kr, w[:, 3840:4352]], axis=1).astype(BF16)
    all_g = (0, 1, 2, 3)
    in_kw = dict(prologue="modulate", g=norm_mix_g[l][None, :], norm_tile=8, norm_g=ev_kv_norm_g[i][None, :])
    head_src = lambda t0: tuple((t0 + t, gq, 4 * t + gq) for t in range(2) for gq in range(4))
    proj_p, st_ak, st_av, st_ckv, st_kr = _projection(
        x, 0, N_SEG_P, D_MODEL, 0, D_MODEL, w_in, mt=mt[:N_SEG_P], name="even_in_proj_ctx",
        states=((N_HEADS, head_src(2)), (N_HEADS, head_src(4)), (0, ((8, 0, B_KV_RANK),)), (0, ((7, 256, B_ROPE_DIM),))),
        **in_kw)
    proj_l = _projection(x, N_SEG_P, DEC_BATCH, D_MODEL, 0, D_MODEL, w_in, mt=mt[N_SEG_P:], rope=rope64,
                         sh=A_QK_DIM // 4, rope_groups={0: all_g, 1: all_g, 2: all_g, 3: all_g, 7: (2, 3)},
                         name="even_in_proj_lat", **in_kw)
    wq = ev_w_uq[i].reshape(B_Q_RANK, N_HEADS, HEAD_DIM + B_ROPE_DIM)
    w_uq = jnp.concatenate([wq[:, :, :HEAD_DIM].reshape(B_Q_RANK, -1), wq[:, :, HEAD_DIM:].reshape(B_Q_RANK, -1)],
                           axis=1).astype(BF16)
    mla_scale = (HEAD_DIM + B_ROPE_DIM) ** -0.5
    q_kw = dict(prologue="rmsnorm", g=ev_q_norm_g[i][None, :], out_dtype=BF16, out_scale=mla_scale * LOG2E)
    bq_p = _projection(proj_p, 0, N_SEG_P, 1024, 3, B_Q_RANK, w_uq, name="mla_q_up_ctx", **q_kw)
    bq_l = _projection(proj_l, 0, DEC_BATCH, 1024, 3, B_Q_RANK, w_uq, rope=rope64, sh=B_ROPE_DIM // 4,
                       rope_groups={2: all_g}, name="mla_q_up_lat", **q_kw)
    w_ukv = ev_w_ukv[i].astype(BF16)
    kv_kw = dict(prologue="cast", out_dtype=BF16)
    kv_p = _projection(proj_p, 0, N_SEG_P, 512, 8, B_KV_RANK, w_ukv, name="mla_kv_up_ctx", **kv_kw)
    kv_l = _projection(proj_l, 0, DEC_BATCH, 512, 8, B_KV_RANK, w_ukv, name="mla_kv_up_lat", **kv_kw)
    kv_ctx = _projection(cache_b_ckv[:, i].reshape(DEC_BATCH * PAST_LEN, B_KV_RANK), 0, 1, 512, 0, B_KV_RANK, w_ukv,
                         name="mla_kv_up_cache", **kv_kw)
    kv_ctx = kv_ctx.reshape(DEC_BATCH, PAST_LEN, 2 * N_HEADS * HEAD_DIM)
    krc = cache_b_krope[:, i]
    zc = jnp.zeros_like(krc)
    kr_ctx = jnp.concatenate([krc, zc, zc, krc], axis=-1)
    ak_ctx = cache_a_k[:, i].reshape(DEC_BATCH, PAST_LEN, N_HEADS * HEAD_DIM)
    av_ctx = cache_a_v[:, i].reshape(DEC_BATCH, PAST_LEN, N_HEADS * HEAD_DIM)
    lam_init = 0.8 - 0.6 * math.exp(-0.3 * l)
    o = jnp.zeros((N_TOK, D_MODEL), BF16)
    for latent, proj, b_q, kv in ((False, proj_p, bq_p, kv_p), (True, proj_l, bq_l, kv_l)):
        tag = "lat" if latent else "ctx"
        o = _attention(kind="diff", latent=latent, q_parts=[(proj, lambda h: h)], ko_parts=[(proj, lambda h: 8 + h)],
                       vo=(proj, lambda h: 16 + h), kc_parts=[(ak_ctx, lambda h: h)], vc=(av_ctx, lambda h: h),
                       lam=ev_lambda[i], subln=ev_subln_g[i][None, :], o_arr=o, o_blk=lambda h: h,
                       scale=A_QK_DIM ** -0.5, lam_init=lam_init, name="diff_attn_" + tag)
        o = _attention(kind="mla", latent=latent, q_parts=[(b_q, lambda h: h), (b_q, lambda h: 8 + h // 2)],
                       ko_parts=[(kv, lambda h: 2 * h), (proj, lambda h: 30 + h % 2)], vo=(kv, lambda h: 2 * h + 1),
                       kc_parts=[(kv_ctx, lambda h: 2 * h), (kr_ctx, lambda h: h % 2)], vc=(kv_ctx, lambda h: 2 * h + 1),
                       o_arr=o, o_blk=lambda h: 8 + h, scale=(HEAD_DIM + B_ROPE_DIM) ** -0.5, name="mla_attn_" + tag)
    y = _out_projection(o, ev_w_out[i].astype(BF16), x, mt)
    states = (st_ak.reshape(BATCH, SEQ, N_HEADS, HEAD_DIM), st_av.reshape(BATCH, SEQ, N_HEADS, HEAD_DIM),
              st_ckv.reshape(BATCH, SEQ, B_KV_RANK), st_kr.reshape(BATCH, SEQ, B_ROPE_DIM))
    return y, states


def _odd_layer(x, mt, l, i, rope128, cache_c_k, cache_c_v, cache_d_k, cache_d_v, norm_mix_g, od_w_in, od_rpb,
               od_sink, od_w_out):
    w_in = od_w_in[i].astype(BF16)
    all_g = (0, 1, 2, 3)
    in_kw = dict(prologue="modulate", g=norm_mix_g[l][None, :])
    head_src = lambda t0: tuple((t0 + t, gq, 4 * t + gq) for t in range(2) for gq in range(4))
    proj_p, st_ck, st_cv, st_dk, st_dv = _projection(
        x, 0, N_SEG_P, D_MODEL, 0, D_MODEL, w_in, mt=mt[:N_SEG_P], name="odd_in_proj_ctx",
        states=((N_HEADS, head_src(2)), (N_HEADS, head_src(4)), (D_KV_HEADS, ((8, 0, 0), (8, 1, 1))),
                (D_KV_HEADS, ((8, 2, 0), (8, 3, 1)))), **in_kw)
    proj_l = _projection(x, N_SEG_P, DEC_BATCH, D_MODEL, 0, D_MODEL, w_in, mt=mt[N_SEG_P:], rope=rope128,
                         sh=HEAD_DIM // 4, rope_groups={6: all_g, 7: all_g, 8: (0, 1)}, name="odd_in_proj_lat", **in_kw)
    ck_ctx = cache_c_k[:, i].reshape(DEC_BATCH, PAST_LEN, N_HEADS * HEAD_DIM)
    cv_ctx = cache_c_v[:, i].reshape(DEC_BATCH, PAST_LEN, N_HEADS * HEAD_DIM)
    dk_ctx = cache_d_k[:, i].reshape(DEC_BATCH, PAST_LEN, D_KV_HEADS * HEAD_DIM)
    dv_ctx = cache_d_v[:, i].reshape(DEC_BATCH, PAST_LEN, D_KV_HEADS * HEAD_DIM)
    bias = _neighbourhood_bias(od_rpb[i])
    sink = od_sink[i].astype(F32)
    o = jnp.zeros((N_TOK, D_MODEL), BF16)
    for latent, proj in ((False, proj_p), (True, proj_l)):
        tag = "lat" if latent else "ctx"
        o = _attention(kind="plain", latent=latent, q_parts=[(proj, lambda h: h)], ko_parts=[(proj, lambda h: 8 + h)],
                       vo=(proj, lambda h: 16 + h), kc_parts=[(ck_ctx, lambda h: h)], vc=(cv_ctx, lambda h: h),
                       bias=bias if latent else None, o_arr=o, o_blk=lambda h: h, scale=HEAD_DIM ** -0.5,
                       name="nbr_attn_" + tag)
        o = _attention(kind="plain", latent=latent, q_parts=[(proj, lambda h: 24 + h)],
                       ko_parts=[(proj, lambda h: 32 + h // D_GROUP)], vo=(proj, lambda h: 34 + h // D_GROUP),
                       kc_parts=[(dk_ctx, lambda h: h // D_GROUP)], vc=(dv_ctx, lambda h: h // D_GROUP), sink=sink,
                       o_arr=o, o_blk=lambda h: 8 + h, scale=HEAD_DIM ** -0.5, has_band=latent, tq_lat=256,
                       name="win_attn_" + tag)
    y = _out_projection(o, od_w_out[i].astype(BF16), x, mt)
    states = (st_ck.reshape(BATCH, SEQ, N_HEADS, HEAD_DIM), st_cv.reshape(BATCH, SEQ, N_HEADS, HEAD_DIM),
              st_dk.reshape(BATCH, SEQ, D_KV_HEADS, HEAD_DIM), st_dv.reshape(BATCH, SEQ, D_KV_HEADS, HEAD_DIM))
    return y, states


def _moe_layer(x, mt, g, router_w, router_b, layer, w_gate, w_up, w_down, final_g, final):
    h2, eidx, gates = _router(x, mt, g, router_w, router_b)
    src_tok, te, n_active, pos = _route_plan(eidx)
    ys = _moe_experts(h2, src_tok, te, n_active, layer, w_gate, w_up, w_down)
    ysel = ys.at[pos].get(mode="promise_in_bounds").reshape(2, N_TOK, D_MODEL // 2)
    out = _combine(x, ysel, gates.T, mt, final_g, final)
    return out if final else out[0]


def kernel(x_prompt, x_sample, cache_a_k, cache_a_v, cache_b_ckv, cache_b_krope, cache_c_k, cache_c_v, cache_d_k, cache_d_v, c, c_ctx, w_ada, b_ada, norm_mix_g, norm_ffn_g, ev_w_in, ev_lambda, ev_subln_g, ev_q_norm_g, ev_kv_norm_g, ev_w_uq, ev_w_ukv, ev_w_out, od_w_in, od_rpb, od_sink, od_w_out, router_w, router_b, moe_w_gate, moe_w_up, moe_w_down, final_g):
    x = jnp.concatenate([x_prompt.reshape(N_PROMPT, D_MODEL), x_sample.reshape(DEC_BATCH * DEC_SEQ, D_MODEL)], axis=0)
    cond8 = jnp.concatenate([c_ctx[None, :], c, jnp.zeros((3, D_MODEL), F32)], axis=0)
    mod = _adaln(cond8, w_ada, b_ada)
    seg_row = np.array([0] * N_SEG_P + [1 + b for b in range(DEC_BATCH)])
    mt_all = mod[:, seg_row].reshape(DEPTH, N_SEG, 6, D_MODEL)
    rope64 = _rope_tables(A_QK_DIM)
    rope128 = _rope_tables(HEAD_DIM)
    even_states, odd_states = [], []
    for l in range(DEPTH):
        i = l // 2
        mt = mt_all[l]
        if l % 2 == 0:
            x, st = _even_layer(x, mt, l, i, rope64, cache_a_k, cache_a_v, cache_b_ckv, cache_b_krope, norm_mix_g,
                                ev_w_in, ev_lambda, ev_subln_g, ev_q_norm_g, ev_kv_norm_g, ev_w_uq, ev_w_ukv, ev_w_out)
            even_states.append(st)
        else:
            x, st = _odd_layer(x, mt, l, i, rope128, cache_c_k, cache_c_v, cache_d_k, cache_d_v, norm_mix_g,
                               od_w_in, od_rpb, od_sink, od_w_out)
            odd_states.append(st)
        x = _moe_layer(x, mt, norm_ffn_g[l][None, :], router_w, router_b, l, moe_w_gate, moe_w_up, moe_w_down,
                       final_g[None, :], final=(l == DEPTH - 1))
    y_prompt = x[0].reshape(BATCH, SEQ, D_MODEL)
    y_sample = x[1].reshape(DEC_BATCH, DEC_SEQ, D_MODEL)
    new_even = tuple(jnp.stack([st[k] for st in even_states], axis=1) for k in range(4))
    new_odd = tuple(jnp.stack([st[k] for st in odd_states], axis=1) for k in range(4))
    return (y_prompt, y_sample) + new_even + new_odd
```

```python
import functools
import math

import numpy as np
import jax
import jax.numpy as jnp
from jax import lax
from jax.experimental import pallas as pl
from jax.experimental.pallas import tpu as pltpu

D_MODEL = 2048
BATCH = 32
SEQ = 256
DEPTH = 2
DEC_BATCH = 4
DEC_SEQ = 1024
PAST_LEN = 256
GRID_W = 64
HEAD_DIM = 128
N_HEADS = 8
A_QK_DIM = 64
B_Q_RANK = 768
B_KV_RANK = 512
B_ROPE_DIM = 64
NA_ROWS = 8
NA_COLS = 16
D_KV_HEADS = 2
D_GROUP = 4
D_WINDOW = 128
N_EXPERTS = 16
N_GROUPS = 4
D_EXPERT = 1024
ROPE_THETA = 10000.0
EPS = 1e-6
NEG = -1e30
LOG2E = 1.4426950408889634

SEG = 1024
N_PROMPT = BATCH * SEQ
N_TOK = N_PROMPT + DEC_BATCH * DEC_SEQ
N_SEG = N_TOK // SEG
N_SEG_P = N_PROMPT // SEG
PROJ_W = 4608
TN = 512
LANES = 128
MOE_TM = 256
MOE_ROWS = 2 * N_TOK + N_EXPERTS * MOE_TM
MOE_TILES = MOE_ROWS // MOE_TM

F32 = jnp.float32
BF16 = jnp.bfloat16


def _cparams(sem, vmem_mb):
    return pltpu.CompilerParams(dimension_semantics=sem, vmem_limit_bytes=vmem_mb * 1024 * 1024)


def _dot(a, b):
    return jnp.dot(a, b, preferred_element_type=F32)


def _dot_nt(a, b):
    return lax.dot_general(a, b, (((1,), (1,)), ((), ())), preferred_element_type=F32)


def _sigmoid(x):
    return 1.0 / (1.0 + jnp.exp(-x))


def _pack_bf16_pair(x):
    c = x.shape[1] // 2
    bits = pltpu.bitcast(x.astype(BF16).astype(F32), jnp.int32)
    return bits[:, :c] | lax.shift_right_logical(bits[:, c:], 16)


def _unpack_bf16_pair(w):
    hi = pltpu.bitcast(w & jnp.int32(-65536), F32)
    lo = pltpu.bitcast(lax.shift_left(w, 16), F32)
    return hi, lo


def _adaln_kernel(c_ref, w_ref, b_ref, o_ref):
    c = c_ref[...]
    a = (c * _sigmoid(c)).astype(BF16)
    o_ref[...] = _dot(a, w_ref[...].astype(BF16)) + b_ref[...]


def _adaln(cond8, w_ada, b_ada):
    tn = 1024
    n = 6 * D_MODEL
    return pl.pallas_call(
        _adaln_kernel,
        grid=(DEPTH, n // tn),
        in_specs=[
            pl.BlockSpec((8, D_MODEL), lambda l, j: (0, 0)),
            pl.BlockSpec((None, D_MODEL, tn), lambda l, j: (l, 0, j)),
            pl.BlockSpec((None, 1, tn), lambda l, j: (l, 0, j)),
        ],
        out_specs=pl.BlockSpec((None, 8, tn), lambda l, j: (l, 0, j)),
        out_shape=jax.ShapeDtypeStruct((DEPTH, 8, n), F32),
        compiler_params=_cparams(("arbitrary", "arbitrary"), 40),
        name="adaln",
    )(cond8, w_ada, b_ada.reshape(DEPTH, 1, n))


def _rope(a, cos, sp, sm, sh):
    return a * cos + pltpu.roll(a, sh, 1) * sp + pltpu.roll(a, LANES - sh, 1) * sm


def _proj_kernel(*refs, prologue, k_in, sh, n_tiles, rope_groups, norm_tile, states, out_scale):
    x_ref, mt_ref, g_ref, w_ref, cos_ref, sp_ref, sm_ref, ng_ref, o_ref = refs[:9]
    st_refs = refs[9:-1]
    xn_ref = refs[-1]
    j = pl.program_id(1)

    @pl.when(j == 0)
    def _():
        x = x_ref[:, :k_in]
        if prologue == "cast":
            xn_ref[...] = x.astype(BF16)
        else:
            r = lax.rsqrt(jnp.mean(x * x, axis=-1, keepdims=True) + EPS)
            if prologue == "modulate":
                y = x * r * (g_ref[...] * (1.0 + mt_ref[0, 1:2, :])) + mt_ref[0, 0:1, :]
            else:
                y = x * r * g_ref[...]
            xn_ref[...] = y.astype(BF16)

    acc = _dot(xn_ref[...], w_ref[...])
    if out_scale != 1.0:
        acc = acc * out_scale
    n_grp = acc.shape[1] // LANES

    def treatment(t):
        acts = tuple((k, a, b) for k, st in enumerate(states) for (tt, a, b) in st[1] if tt == t)
        return (tuple(rope_groups.get(t, ())), t == norm_tile, acts)

    branches = {}
    for t in range(n_tiles):
        branches.setdefault(treatment(t), []).append(t)

    o_ref[...] = acc.astype(o_ref.dtype)
    reread = o_ref.dtype == F32

    for (rg, is_norm, acts), tiles in branches.items():
        if not rg and not is_norm and not acts:
            continue
        cond = j == tiles[0]
        for t in tiles[1:]:
            cond = jnp.logical_or(cond, j == t)

        @pl.when(cond)
        def _(rg=rg, is_norm=is_norm, acts=acts):
            vals = [(o_ref if reread else acc)[:, g * LANES:(g + 1) * LANES] for g in range(n_grp)]
            if is_norm:
                ssq = None
                for v in vals:
                    s = jnp.sum(v * v, axis=-1, keepdims=True)
                    ssq = s if ssq is None else ssq + s
                scale = lax.rsqrt(ssq * (1.0 / (n_grp * LANES)) + EPS)
                ng = ng_ref[...]
                vals = [v * scale * ng[:, g * LANES:(g + 1) * LANES] for g, v in enumerate(vals)]
            elif rg:
                cos, sp, sm = cos_ref[...], sp_ref[...], sm_ref[...]
                vals = [_rope(v, cos, sp, sm, sh) if g in rg else v for g, v in enumerate(vals)]
            if is_norm or rg:
                for g, v in enumerate(vals):
                    if is_norm or g in rg:
                        o_ref[:, g * LANES:(g + 1) * LANES] = v.astype(o_ref.dtype)
            for k, a, b in acts:
                n_heads = states[k][0]
                if n_heads:
                    st_refs[k][pl.ds(b, SEG, stride=n_heads), :] = vals[a]
                elif b % LANES == 0:
                    for q in range(b // LANES):
                        st_refs[k][:, q * LANES:(q + 1) * LANES] = vals[a // LANES + q]
                else:
                    st_refs[k][...] = vals[a // LANES][:, a % LANES:a % LANES + b]


def _projection(x, row_off, n_seg, x_blk_w, x_blk_idx, k_in, w, *, prologue, mt=None, g=None, rope=None, sh=16,
                rope_groups=None, norm_tile=None, norm_g=None, states=(), out_dtype=F32, out_scale=1.0, name):
    n = w.shape[1]
    m = n_seg * SEG
    assert n % TN == 0 and w.shape[0] == k_in
    if mt is None:
        mt = jnp.zeros((n_seg, 6, LANES), F32)
    if g is None:
        g = jnp.ones((1, k_in), F32)
    if rope is None:
        rope = tuple(jnp.zeros((8, LANES), F32) for _ in range(3))
    if norm_g is None:
        norm_g = jnp.ones((1, TN), F32)
    kern = functools.partial(_proj_kernel, prologue=prologue, k_in=k_in, sh=sh, n_tiles=n // TN,
                             rope_groups=dict(rope_groups or {}), norm_tile=norm_tile, states=tuple(states),
                             out_scale=float(out_scale))
    full2 = lambda i, j: (0, 0)
    out_shape = [jax.ShapeDtypeStruct((m, n), out_dtype)]
    out_specs = [pl.BlockSpec((SEG, TN), lambda i, j: (i, j))]
    for n_heads, src in states:
        if n_heads:
            out_shape.append(jax.ShapeDtypeStruct((m * n_heads, LANES), F32))
            out_specs.append(pl.BlockSpec((SEG * n_heads, LANES), lambda i, j: (i, 0)))
        else:
            width = src[0][2]
            out_shape.append(jax.ShapeDtypeStruct((m, width), F32))
            out_specs.append(pl.BlockSpec((SEG, width), lambda i, j: (i, 0)))
    x_kw = dict(pipeline_mode=pl.Buffered(1)) if states else {}
    res = pl.pallas_call(
        kern,
        grid=(n_seg, n // TN),
        in_specs=[
            pl.BlockSpec((SEG, x_blk_w), lambda i, j: (i + row_off, x_blk_idx), **x_kw),
            pl.BlockSpec((1, 6, mt.shape[2]), lambda i, j: (i, 0, 0)),
            pl.BlockSpec((1, k_in), full2),
            pl.BlockSpec((k_in, TN), lambda i, j: (0, j)),
            pl.BlockSpec(rope[0].shape, full2),
            pl.BlockSpec(rope[1].shape, full2),
            pl.BlockSpec(rope[2].shape, full2),
            pl.BlockSpec((1, TN), full2),
        ],
        out_specs=out_specs,
        out_shape=out_shape,
        scratch_shapes=[pltpu.VMEM((SEG, k_in), BF16)],
        compiler_params=_cparams(("arbitrary", "arbitrary"), 56 if states else 48),
        name=name,
    )(x, mt, g, w, *rope, norm_g)
    return res if states else res[0]


def _attn_kernel(*refs, kind, n_parts, has_ctx, has_bias, has_band, has_sink, nb, s_len, tq, q_scale, lam_init,
                 own_win, mxu_rowsum):
    it = iter(refs)
    q_refs = [next(it) for _ in range(n_parts)]
    ko_refs = [next(it) for _ in range(n_parts)]
    vo_ref = next(it)
    kc_refs, vc_ref = [], None
    if has_ctx:
        kc_refs = [next(it) for _ in range(n_parts)]
        vc_ref = next(it)
    bias_ref = next(it) if has_bias else None
    lam_ref = sg_ref = None
    if kind == "diff":
        lam_ref = next(it)
        sg_ref = next(it)
    sink_ref = next(it) if has_sink else None
    o_ref = next(it)

    sink2 = sink_ref[pl.program_id(1)] * LOG2E if has_sink else None
    lane = lax.broadcasted_iota(jnp.int32, (1, LANES), 1)

    def softmax_pv(s_list, v_list):
        m = None
        for s in s_list:
            mm = jnp.max(s, axis=-1, keepdims=True)
            m = mm if m is None else jnp.maximum(m, mm)
        if sink2 is not None:
            m = jnp.maximum(m, sink2)
        l = jnp.exp2(sink2 - m) if sink2 is not None else None
        o = None
        for s, v in zip(s_list, v_list):
            e = jnp.exp2(s - m)
            eb = e.astype(BF16)
            if mxu_rowsum:
                ss = _dot(eb, jnp.ones((s.shape[1], LANES), BF16))
            else:
                ss = jnp.sum(e, axis=-1, keepdims=True)
            l = ss if l is None else l + ss
            pv = _dot(eb, v)
            o = pv if o is None else o + pv
        return o, l

    for bi in range(nb):
        qrows = pl.ds(bi * tq, tq)
        q0 = pl.program_id(2) * tq
        if own_win is None:
            krows, k0 = pl.ds(bi * s_len, s_len), 0
            n_own = s_len
        else:
            k0 = pl.multiple_of(jnp.clip(q0 - D_WINDOW, 0, s_len - own_win), LANES)
            krows = pl.ds(k0, own_win)
            n_own = own_win
        k_srcs, v_srcs, is_own = [], [], []
        if has_ctx:
            k_srcs.append([r[...].astype(BF16) for r in kc_refs])
            v_srcs.append(vc_ref[...].astype(BF16))
            is_own.append(False)
        k_srcs.append([r[krows, :].astype(BF16) for r in ko_refs])
        v_srcs.append(vo_ref[krows, :].astype(BF16))
        is_own.append(True)

        def mask_own(s):
            if has_bias:
                s = s + bias_ref[...]
            if has_band:
                qpos = q0 + lax.broadcasted_iota(jnp.int32, (tq, 1), 0)
                kpos = k0 + lax.broadcasted_iota(jnp.int32, (1, n_own), 1)
                s = jnp.where(jnp.abs(kpos - qpos) <= D_WINDOW, s, NEG)
            return s

        if kind == "diff":
            q = q_refs[0][qrows, :] * q_scale
            q1 = jnp.where(lane < A_QK_DIM, q, 0.0).astype(BF16)
            q2 = jnp.where(lane < A_QK_DIM, 0.0, q).astype(BF16)
            o1, l1 = softmax_pv([_dot_nt(q1, ks[0]) for ks in k_srcs], v_srcs)
            o2, l2 = softmax_pv([_dot_nt(q2, ks[0]) for ks in k_srcs], v_srcs)
            lv = lam_ref[...]
            lam = (jnp.exp(jnp.sum(lv[0:1] * lv[1:2], axis=-1, keepdims=True))
                   - jnp.exp(jnp.sum(lv[2:3] * lv[3:4], axis=-1, keepdims=True)) + lam_init)
            o = o1 * (1.0 / l1) - o2 * (lam / l2)
            o = o * lax.rsqrt(jnp.mean(o * o, axis=-1, keepdims=True) + EPS) * sg_ref[...] * (1.0 - lam_init)
        else:
            qs = [r[qrows, :] for r in q_refs]
            qs = [(qp * q_scale).astype(BF16) if qp.dtype == F32 else qp for qp in qs]
            s_list = []
            for ks, own in zip(k_srcs, is_own):
                s = None
                for qp, kp in zip(qs, ks):
                    d = _dot_nt(qp, kp)
                    s = d if s is None else s + d
                s_list.append(mask_own(s) if own else s)
            o, l = softmax_pv(s_list, v_srcs)
            o = o * (1.0 / l)
        o_ref[qrows, :] = o.astype(o_ref.dtype)


def _attention(*, kind, latent, q_parts, ko_parts, vo, kc_parts=(), vc=None, bias=None, lam=None, subln=None,
               sink=None, o_arr, o_blk, scale, lam_init=0.0, has_band=False, tq_lat=512, name):
    n_parts = len(q_parts)
    has_ctx = latent
    if latent:
        tq, s_len, nb = tq_lat, DEC_SEQ, 1
        grid = (DEC_BATCH, N_HEADS, DEC_SEQ // tq)
        qpb = DEC_SEQ // tq
        q_row = lambda b, h, t: b * qpb + t
        o_row = lambda b, h, t: N_PROMPT // tq + b * qpb + t
        k_row = lambda b, h, t: b
        qblk = tq
    else:
        tq, s_len, nb = SEQ, SEQ, SEG // SEQ
        grid = (N_SEG_P, N_HEADS, 1)
        q_row = o_row = k_row = lambda b, h, t: b
        qblk = SEG
    args, specs = [], []

    def add(arr, shape, imap, **kw):
        args.append(arr)
        specs.append(pl.BlockSpec(shape, imap, **kw))

    for arr, f in q_parts:
        add(arr, (qblk, LANES), lambda b, h, t, f=f: (q_row(b, h, t), f(h)))
    for arr, f in list(ko_parts) + [vo]:
        add(arr, (SEG, LANES), lambda b, h, t, f=f: (k_row(b, h, t), f(h)))
    if has_ctx:
        for arr, f in list(kc_parts) + [vc]:
            add(arr, (None, PAST_LEN, LANES), lambda b, h, t, f=f: (b, 0, f(h)))
    if bias is not None:
        add(bias, (None, tq, DEC_SEQ), lambda b, h, t: (h, t, 0))
    if kind == "diff":
        add(lam, lam.shape, lambda b, h, t: (0, 0))
        add(subln, subln.shape, lambda b, h, t: (0, 0))
    if sink is not None:
        args.append(sink)
        specs.append(pl.BlockSpec(memory_space=pltpu.SMEM))
    n_in = len(args)
    args.append(o_arr)
    specs.append(pl.BlockSpec(memory_space=pl.ANY))
    kern = functools.partial(_attn_kernel_aliased, kind=kind, n_parts=n_parts, has_ctx=has_ctx,
                             has_bias=bias is not None, has_band=has_band, has_sink=sink is not None,
                             nb=nb, s_len=s_len, tq=tq, q_scale=scale * LOG2E, lam_init=lam_init,
                             own_win=(tq + 2 * D_WINDOW) if has_band else None, mxu_rowsum=not latent)
    return pl.pallas_call(
        kern,
        grid=grid,
        in_specs=specs,
        out_specs=pl.BlockSpec((qblk, LANES), lambda b, h, t: (o_row(b, h, t), o_blk(h))),
        out_shape=jax.ShapeDtypeStruct(o_arr.shape, o_arr.dtype),
        input_output_aliases={n_in: 0},
        compiler_params=_cparams(("arbitrary", "arbitrary", "arbitrary"), 48),
        name=name,
    )(*args)


def _attn_kernel_aliased(*refs, **kw):
    _attn_kernel(*refs[:-2], refs[-1], **kw)


def _outproj_kernel(o_ref, w_ref, mt_ref, *rest, seg_ranges):
    x_refs, y_ref = rest[:-1], rest[-1]
    upd = mt_ref[0, 2:3, :] * _dot(o_ref[...], w_ref[...])
    if len(x_refs) == 1:
        y_ref[...] = x_refs[0][...] + upd
    else:
        i = pl.program_id(0)
        for x_ref, (lo, hi) in zip(x_refs, seg_ranges):
            @pl.when(jnp.logical_and(i >= lo, i < hi))
            def _(x_ref=x_ref):
                y_ref[...] = x_ref[...] + upd


def _out_projection(o, w, x_parts, mt):
    seg_ranges, lo = [], 0
    for xp in x_parts:
        seg_ranges.append((lo, lo + xp.shape[0] // SEG))
        lo = seg_ranges[-1][1]
    assert lo == N_SEG
    x_specs = [pl.BlockSpec((SEG, TN), lambda i, j, lo=lo, hi=hi: (
        jnp.clip(i - lo, 0, hi - lo - 1), jnp.where(jnp.logical_and(i >= lo, i < hi), j, 0)))
               for lo, hi in seg_ranges]
    return pl.pallas_call(
        functools.partial(_outproj_kernel, seg_ranges=tuple(seg_ranges)),
        grid=(N_SEG, D_MODEL // TN),
        in_specs=[
            pl.BlockSpec((SEG, D_MODEL), lambda i, j: (i, 0)),
            pl.BlockSpec((D_MODEL, TN), lambda i, j: (0, j)),
            pl.BlockSpec((1, 6, TN), lambda i, j: (i, 0, j)),
        ] + x_specs,
        out_specs=pl.BlockSpec((SEG, TN), lambda i, j: (i, j)),
        out_shape=jax.ShapeDtypeStruct((N_TOK, D_MODEL), F32),
        compiler_params=_cparams(("arbitrary", "arbitrary"), 40),
        name="out_proj",
    )(o, w, mt, *x_parts)


def _router_kernel(x_ref, mt_ref, g_ref, wh_ref, wl_ref, rb_ref, h_ref, e_ref, gt_ref):
    x = x_ref[...]
    y = x * lax.rsqrt(jnp.mean(x * x, axis=-1, keepdims=True) + EPS) * g_ref[...]
    h = y * (1.0 + mt_ref[0, 4:5, :]) + mt_ref[0, 3:4, :]
    hh = h.astype(BF16)
    h_ref[...] = _pack_bf16_pair(h)
    hl = (h - hh.astype(F32)).astype(BF16)
    wh, wl = wh_ref[...], wl_ref[...]
    logits = _dot_nt(wh, hh) + _dot_nt(wh, hl) + _dot_nt(wl, hh)
    scores = _sigmoid(logits)
    sel = scores + rb_ref[...]
    per = N_EXPERTS // N_GROUPS
    sc = [scores[e:e + 1, :] for e in range(N_EXPERTS)]
    sl = [sel[e:e + 1, :] for e in range(N_EXPERTS)]
    best_g, best_v = None, None
    for gi in range(N_GROUPS):
        a, b, c, d = sl[gi * per:(gi + 1) * per]
        hi1, lo1, hi2, lo2 = jnp.maximum(a, b), jnp.minimum(a, b), jnp.maximum(c, d), jnp.minimum(c, d)
        gs = jnp.maximum(hi1, hi2) + jnp.maximum(jnp.minimum(hi1, hi2), jnp.maximum(lo1, lo2))
        if gi == 0:
            best_g, best_v = jnp.zeros_like(gs, dtype=jnp.int32), gs
        else:
            better = gs > best_v
            best_g = jnp.where(better, gi, best_g)
            best_v = jnp.where(better, gs, best_v)
    masked = [jnp.where(best_g == (e // per), sl[e], NEG) for e in range(N_EXPERTS)]
    i1, v1 = jnp.zeros_like(best_g), masked[0]
    for e in range(1, N_EXPERTS):
        better = masked[e] > v1
        i1 = jnp.where(better, e, i1)
        v1 = jnp.where(better, masked[e], v1)
    i2, v2 = None, None
    for e in range(N_EXPERTS):
        cand = jnp.where(i1 == e, -2e30, masked[e])
        if e == 0:
            i2, v2 = jnp.zeros_like(best_g), cand
        else:
            better = cand > v2
            i2 = jnp.where(better, e, i2)
            v2 = jnp.where(better, cand, v2)
    g1 = jnp.zeros_like(v1)
    g2 = jnp.zeros_like(v1)
    for e in range(N_EXPERTS):
        g1 = jnp.where(i1 == e, sc[e], g1)
        g2 = jnp.where(i2 == e, sc[e], g2)
    tot = g1 + g2
    rows = i1.shape[1]
    e_ref[...] = jnp.concatenate([i1, i2, jnp.zeros((6, rows), jnp.int32)], axis=0)
    gt_ref[...] = jnp.concatenate([g1 / tot, g2 / tot, jnp.zeros((6, rows), F32)], axis=0)


def _router(x, mt, g, router_w, router_b):
    wt = router_w.T
    wh = wt.astype(BF16)
    wl = (wt - wh.astype(F32)).astype(BF16)
    return pl.pallas_call(
        _router_kernel,
        grid=(N_SEG,),
        in_specs=[
            pl.BlockSpec((SEG, D_MODEL), lambda i: (i, 0)),
            pl.BlockSpec((1, 6, D_MODEL), lambda i: (i, 0, 0)),
            pl.BlockSpec((1, D_MODEL), lambda i: (0, 0)),
            pl.BlockSpec((N_EXPERTS, D_MODEL), lambda i: (0, 0)),
            pl.BlockSpec((N_EXPERTS, D_MODEL), lambda i: (0, 0)),
            pl.BlockSpec((N_EXPERTS, 1), lambda i: (0, 0)),
        ],
        out_specs=[
            pl.BlockSpec((SEG, D_MODEL // 2), lambda i: (i, 0)),
            pl.BlockSpec((8, SEG), lambda i: (0, i)),
            pl.BlockSpec((8, SEG), lambda i: (0, i)),
        ],
        out_shape=[
            jax.ShapeDtypeStruct((N_TOK, D_MODEL // 2), jnp.int32),
            jax.ShapeDtypeStruct((8, N_TOK), jnp.int32),
            jax.ShapeDtypeStruct((8, N_TOK), F32),
        ],
        compiler_params=_cparams(("arbitrary",), 48),
        name="ffn_norm_router",
    )(x, mt, g, wh, wl, router_b.reshape(N_EXPERTS, 1))


def _expert_weights(te_ref, first_ref, slot_ref, nxt_ref, i, t, layer, w_hbm, stage_ref, sem_ref, w_bf_refs):
    def copies(e, s):
        return [pltpu.make_async_copy(w.at[layer, e], stage_ref.at[s, k], sem_ref.at[s, k])
                for k, w in enumerate(w_hbm)]

    s = slot_ref[t]

    @pl.when(i == 0)
    def _():
        for c in copies(te_ref[t], s):
            c.start()

    @pl.when(first_ref[t] == 1)
    def _():
        for c in copies(te_ref[t], s):
            c.wait()

        @pl.when(nxt_ref[t] >= 0)
        def _():
            for c in copies(nxt_ref[t], 1 - s):
                c.start()

        for k, w_bf in enumerate(w_bf_refs):
            w_bf[...] = stage_ref[s, k].astype(BF16)


def _moe_up_kernel(te_ref, na_ref, first_ref, slot_ref, nxt_ref, xs_ref, wg_hbm, wu_hbm, hid_ref,
                   stage_ref, wgb_ref, wub_ref, sem_ref, *, tile0, layer):
    i = pl.program_id(0)
    t = i + tile0
    _expert_weights(te_ref, first_ref, slot_ref, nxt_ref, i, t, layer, (wg_hbm, wu_hbm), stage_ref, sem_ref,
                    (wgb_ref, wub_ref))

    @pl.when(t < na_ref[0])
    def _():
        x_hi, x_lo = _unpack_bf16_pair(xs_ref[...])
        x_hi, x_lo = x_hi.astype(BF16), x_lo.astype(BF16)
        half = D_MODEL // 2
        g = _dot(x_hi, wgb_ref[:half, :]) + _dot(x_lo, wgb_ref[half:, :])
        u = _dot(x_hi, wub_ref[:half, :]) + _dot(x_lo, wub_ref[half:, :])
        hid_ref[...] = (g * _sigmoid(g) * u).astype(BF16)

    @pl.when(t >= na_ref[0])
    def _():
        hid_ref[...] = jnp.zeros_like(hid_ref)


def _moe_down_kernel(te_ref, na_ref, first_ref, slot_ref, nxt_ref, *rest, layer, chunk_tiles):
    ys_ref, stage_ref, wdb_ref, sem_ref = rest[-4:]
    wd_hbm = rest[-5]
    hid_refs = rest[:-5]
    t = pl.program_id(0)
    _expert_weights(te_ref, first_ref, slot_ref, nxt_ref, t, t, layer, (wd_hbm,), stage_ref, sem_ref, (wdb_ref,))

    for c, hid_ref in enumerate(hid_refs):
        @pl.when(jnp.logical_and(t < na_ref[0], t // chunk_tiles == c))
        def _(hid_ref=hid_ref):
            ys_ref[...] = _pack_bf16_pair(_dot(hid_ref[...], wdb_ref[...]))

    @pl.when(t >= na_ref[0])
    def _():
        ys_ref[...] = jnp.zeros_like(ys_ref)


MOE_CHUNKS = 2


def _moe_experts(h2, src_tok, tile_expert, n_active, layer, w_gate, w_up, w_down):
    tiles = MOE_TILES // MOE_CHUNKS
    rows = tiles * MOE_TM

    def weight_runs(call_tiles):
        t_ids = jnp.arange(MOE_TILES, dtype=jnp.int32)
        prev = jnp.concatenate([tile_expert[:1], tile_expert[:-1]])
        first = jnp.logical_or(t_ids % call_tiles == 0, tile_expert != prev)
        slot = ((jnp.cumsum(first.astype(jnp.int32)) - 1) % 2).astype(jnp.int32)
        first_at = lax.cummin(jnp.where(first, t_ids, MOE_TILES), reverse=True)
        next_at = jnp.concatenate([first_at[1:], jnp.full((1,), MOE_TILES, jnp.int32)])
        same_call = jnp.logical_and(next_at < MOE_TILES, next_at // call_tiles == t_ids // call_tiles)
        nxt = jnp.where(jnp.logical_and(first, same_call), tile_expert[jnp.minimum(next_at, MOE_TILES - 1)], -1)
        return (tile_expert, n_active, first.astype(jnp.int32), slot, nxt.astype(jnp.int32))

    row_blk = lambda i, *_: (i, 0)
    hbm = pl.BlockSpec(memory_space=pl.ANY)
    plan = weight_runs(tiles)
    hids = []
    for c in range(MOE_CHUNKS):
        xs = h2.at[src_tok[c * rows:(c + 1) * rows]].get(mode="promise_in_bounds")
        hids.append(pl.pallas_call(
            functools.partial(_moe_up_kernel, tile0=c * tiles, layer=layer),
            grid_spec=pltpu.PrefetchScalarGridSpec(
                num_scalar_prefetch=len(plan),
                grid=(tiles,),
                in_specs=[pl.BlockSpec((MOE_TM, D_MODEL // 2), row_blk), hbm, hbm],
                out_specs=pl.BlockSpec((MOE_TM, D_EXPERT), row_blk),
                scratch_shapes=[pltpu.VMEM((2, 2, D_MODEL, D_EXPERT), F32),
                                pltpu.VMEM((D_MODEL, D_EXPERT), BF16), pltpu.VMEM((D_MODEL, D_EXPERT), BF16),
                                pltpu.SemaphoreType.DMA((2, 2))],
            ),
            out_shape=jax.ShapeDtypeStruct((rows, D_EXPERT), BF16),
            compiler_params=_cparams(("arbitrary",), 52),
            name="moe_up",
        )(*plan, xs, w_gate, w_up))
    plan = weight_runs(MOE_TILES)
    hid_specs = [pl.BlockSpec((MOE_TM, D_EXPERT), lambda i, *_, c=c: (jnp.clip(i - c * tiles, 0, tiles - 1), 0))
                 for c in range(MOE_CHUNKS)]
    return pl.pallas_call(
        functools.partial(_moe_down_kernel, layer=layer, chunk_tiles=tiles),
        grid_spec=pltpu.PrefetchScalarGridSpec(
            num_scalar_prefetch=len(plan),
            grid=(MOE_TILES,),
            in_specs=hid_specs + [hbm],
            out_specs=pl.BlockSpec((MOE_TM, D_MODEL // 2), row_blk),
            scratch_shapes=[pltpu.VMEM((2, 1, D_EXPERT, D_MODEL), F32), pltpu.VMEM((D_EXPERT, D_MODEL), BF16),
                            pltpu.SemaphoreType.DMA((2, 1))],
        ),
        out_shape=jax.ShapeDtypeStruct((MOE_ROWS, D_MODEL // 2), jnp.int32),
        compiler_params=_cparams(("arbitrary",), 40),
        name="moe_down",
    )(*plan, *hids, w_down)


PLAN_R = 2 * N_TOK // LANES


def _plan_kernel(e_ref, pos_ref, meta_ref):
    e = e_ref[...]
    r_i = lax.broadcasted_iota(jnp.int32, (LANES, LANES), 0)
    c_i = lax.broadcasted_iota(jnp.int32, (LANES, LANES), 1)
    upper = jnp.where(r_i <= c_i, 1.0, 0.0).astype(BF16)
    r_j = lax.broadcasted_iota(jnp.int32, (PLAN_R, PLAN_R), 0)
    c_j = lax.broadcasted_iota(jnp.int32, (PLAN_R, PLAN_R), 1)
    lower = jnp.where(c_j < r_j, 1.0, 0.0).astype(BF16)
    lane = lax.broadcasted_iota(jnp.int32, (1, LANES), 1)
    pos = jnp.zeros((PLAN_R, LANES), F32)
    cnt_row = jnp.zeros((1, LANES), F32)
    off_row = jnp.zeros((1, LANES), F32)
    end_row = []
    row_off = jnp.zeros((1, 1), F32)
    tiles_done = jnp.zeros((1, 1), F32)
    for ex in range(N_EXPERTS):
        m = jnp.where(e == ex, 1.0, 0.0)
        inc = _dot(m.astype(BF16), upper)
        tot = jnp.broadcast_to(inc[:, LANES - 1:LANES], (PLAN_R, LANES))
        before = _dot(lower, tot.astype(BF16))
        cnt = jnp.sum(inc[:, LANES - 1:LANES], axis=0, keepdims=True)
        pos = pos + m * (row_off + before + inc - 1.0)
        tiles = jnp.floor((cnt + (MOE_TM - 1)) * (1.0 / MOE_TM))
        cnt_row = jnp.where(lane == ex, cnt, cnt_row)
        off_row = jnp.where(lane == ex, row_off, off_row)
        tiles_done = tiles_done + tiles
        end_row.append(tiles_done)
        row_off = row_off + tiles * MOE_TM
    n_active = tiles_done
    tid = jnp.minimum(lane.astype(F32), n_active - 1.0)
    te = jnp.zeros((1, LANES), F32)
    for ex in range(N_EXPERTS):
        te = te + jnp.where(end_row[ex] <= tid, 1.0, 0.0)
    pos_ref[...] = pos.astype(jnp.int32)
    meta = jnp.concatenate([cnt_row, off_row, te, jnp.broadcast_to(n_active, (1, LANES)),
                            jnp.zeros((4, LANES), F32)], axis=0)
    meta_ref[...] = meta.astype(jnp.int32)


def _route_plan(eidx):
    e2 = eidx[:2].reshape(PLAN_R, LANES)
    pos, meta = pl.pallas_call(
        _plan_kernel,
        out_shape=[jax.ShapeDtypeStruct((PLAN_R, LANES), jnp.int32), jax.ShapeDtypeStruct((8, LANES), jnp.int32)],
        compiler_params=pltpu.CompilerParams(vmem_limit_bytes=32 * 1024 * 1024),
        name="route_plan",
    )(e2)
    cnt, row_off = meta[0, :N_EXPERTS], meta[1, :N_EXPERTS]
    te, n_active = meta[2, :MOE_TILES], meta[3, :1]
    start = jnp.cumsum(cnt) - cnt
    order = jnp.argsort(e2.reshape(-1), stable=True).astype(jnp.int32)
    row_e = jnp.repeat(te, MOE_TM)
    rank = jnp.arange(MOE_ROWS, dtype=jnp.int32) - row_off[row_e]
    valid = rank < cnt[row_e]
    src_asg = order[jnp.clip(start[row_e] + rank, 0, 2 * N_TOK - 1)]
    rows = jnp.arange(MOE_ROWS, dtype=jnp.int32)
    src_tok = jnp.where(valid, src_asg % N_TOK, rows % N_TOK)
    return src_tok, te, n_active, pos.reshape(-1)


def _combine_kernel(x_ref, y0_ref, y1_ref, gt_ref, mt_ref, fg_ref, *o_refs, final, tm):
    if final:
        i = pl.program_id(0)
        for o_ref, cond in zip(o_refs, (i < N_PROMPT // tm, i >= N_PROMPT // tm)):
            pl.when(cond)(functools.partial(_combine_tile, x_ref, y0_ref, y1_ref, gt_ref, mt_ref, fg_ref, o_ref, True))
    else:
        _combine_tile(x_ref, y0_ref, y1_ref, gt_ref, mt_ref, fg_ref, o_refs[0], False)


def _combine_tile(x_ref, y0_ref, y1_ref, gt_ref, mt_ref, fg_ref, o_ref, final):
    gt = gt_ref[...]
    g0, g1 = gt[:, 0:1], gt[:, 1:2]
    half = D_MODEL // 2
    y0 = _unpack_bf16_pair(y0_ref[...])
    y1 = _unpack_bf16_pair(y1_ref[...])
    xs = []
    for c in range(2):
        cols = slice(c * half, (c + 1) * half)
        xs.append(x_ref[:, cols] + mt_ref[0, 5:6, cols] * (y0[c] * g0 + y1[c] * g1))
    if final:
        ssq = jnp.sum(xs[0] * xs[0], axis=-1, keepdims=True) + jnp.sum(xs[1] * xs[1], axis=-1, keepdims=True)
        r = lax.rsqrt(ssq * (1.0 / D_MODEL) + EPS)
        xs = [xc * r * fg_ref[:, c * half:(c + 1) * half] for c, xc in enumerate(xs)]
    for c, xc in enumerate(xs):
        o_ref[:, c * half:(c + 1) * half] = xc


def _combine(x, ysel, gates_t, mt, final_g, final):
    tm = 256
    n_p = N_PROMPT // tm
    if final:
        out_specs = [pl.BlockSpec((tm, D_MODEL), lambda i: (jnp.minimum(i, n_p - 1), 0)),
                     pl.BlockSpec((tm, D_MODEL), lambda i: (jnp.maximum(i - n_p, 0), 0))]
        out_shape = [jax.ShapeDtypeStruct((N_PROMPT, D_MODEL), F32),
                     jax.ShapeDtypeStruct((N_TOK - N_PROMPT, D_MODEL), F32)]
    else:
        out_specs = [pl.BlockSpec((tm, D_MODEL), lambda i: (i, 0))]
        out_shape = [jax.ShapeDtypeStruct((N_TOK, D_MODEL), F32)]
    return pl.pallas_call(
        functools.partial(_combine_kernel, final=final, tm=tm),
        grid=(N_TOK // tm,),
        in_specs=[
            pl.BlockSpec((tm, D_MODEL), lambda i: (i, 0)),
            pl.BlockSpec((None, tm, D_MODEL // 2), lambda i: (0, i, 0)),
            pl.BlockSpec((None, tm, D_MODEL // 2), lambda i: (1, i, 0)),
            pl.BlockSpec((tm, 8), lambda i: (i, 0)),
            pl.BlockSpec((1, 6, D_MODEL), lambda i: (i // (SEG // tm), 0, 0)),
            pl.BlockSpec((1, D_MODEL), lambda i: (0, 0)),
        ],
        out_specs=out_specs,
        out_shape=out_shape,
        compiler_params=_cparams(("arbitrary",), 40),
        name="moe_combine",
    )(x, ysel, ysel, gates_t, mt, final_g)


def _rope_tables(dim):
    half = dim // 2
    inv = ROPE_THETA ** (-jnp.arange(0, half, 2, dtype=F32) / half)
    t = jnp.arange(DEC_SEQ)
    ang_r = (t // GRID_W).astype(F32)[:, None] * inv[None, :]
    ang_c = (t % GRID_W).astype(F32)[:, None] * inv[None, :]
    ang = jnp.concatenate([ang_r, ang_r, ang_c, ang_c], axis=-1)
    cos, sin = jnp.cos(ang), jnp.sin(ang)
    reps = LANES // dim
    cos = jnp.tile(cos, (1, reps))
    sin = jnp.tile(sin, (1, reps))
    sh = dim // 4
    second = (np.arange(LANES) % (2 * sh)) >= sh
    sp = jnp.where(second[None, :], sin, 0.0)
    sm = jnp.where(second[None, :], 0.0, -sin)
    return cos, sp, sm


def _neighbourhood_bias(rpb):
    rows = DEC_SEQ // GRID_W
    kh = min(NA_ROWS, rows)
    r = np.arange(rows)
    r0 = np.clip(r - kh // 2, 0, rows - kh)
    kr = np.arange(rows)
    row_ok = (kr[None, :] >= r0[:, None]) & (kr[None, :] < r0[:, None] + kh)
    c = np.arange(GRID_W)
    ws = np.clip(c - NA_COLS // 2, 0, GRID_W - NA_COLS)
    kc = np.arange(GRID_W)
    col_ok = (kc[None, :] >= ws[:, None]) & (kc[None, :] < ws[:, None] + NA_COLS)
    dc_idx = np.clip(kc[None, :] - c[:, None], -(NA_COLS - 1), NA_COLS - 1) + NA_COLS - 1
    sel_c = ((np.arange(2 * NA_COLS - 1)[:, None, None] == dc_idx[None]) & col_ok[None]).astype(np.float32)
    t = jnp.einsum("hab,bcx->hacx", rpb.astype(F32), sel_c, precision=lax.Precision.HIGHEST)
    t = jnp.where(col_ok[None, None], t * LOG2E, NEG)
    t = jnp.concatenate([t, t], axis=-1)

    def build(t_ref, o_ref):
        left = lax.broadcasted_iota(jnp.int32, (1, LANES), 1) < GRID_W
        neg = jnp.full((GRID_W, LANES), NEG, F32)
        for rr in range(rows):
            for p in range(rows // 2):
                parts = [t_ref[k - rr + NA_ROWS - 1] if row_ok[rr, k] else neg for k in (2 * p, 2 * p + 1)]
                blk = jnp.where(left, parts[0], parts[1]) if (row_ok[rr, 2 * p] or row_ok[rr, 2 * p + 1]) else neg
                o_ref[rr * GRID_W:(rr + 1) * GRID_W, p * LANES:(p + 1) * LANES] = blk

    return pl.pallas_call(
        build,
        grid=(N_HEADS,),
        in_specs=[pl.BlockSpec((None, 2 * NA_ROWS - 1, GRID_W, LANES), lambda h: (h, 0, 0, 0))],
        out_specs=pl.BlockSpec((None, DEC_SEQ, DEC_SEQ), lambda h: (h, 0, 0)),
        out_shape=jax.ShapeDtypeStruct((N_HEADS, DEC_SEQ, DEC_SEQ), F32),
        compiler_params=_cparams(("arbitrary",), 32),
        name="nbr_bias",
    )(t)


def _even_layer(x, mt, l, i, rope64, cache_a_k, cache_a_v, cache_b_ckv, cache_b_krope, norm_mix_g, ev_w_in,
                ev_lambda, ev_subln_g, ev_q_norm_g, ev_kv_norm_g, ev_w_uq, ev_w_ukv, ev_w_out):
    w = ev_w_in[i]
    kr = w[:, 4352:4416]
    z = jnp.zeros_like(kr)
    w_kr = jnp.concatenate([kr, z, z, kr], axis=1).astype(BF16)
    w_in = jnp.concatenate([w[:, :3840].astype(BF16), w_kr, w[:, 3840:4352].astype(BF16)], axis=1)
    all_g = (0, 1, 2, 3)
    in_kw = dict(prologue="modulate", g=norm_mix_g[l][None, :], norm_tile=8, norm_g=ev_kv_norm_g[i][None, :])
    head_src = lambda t0: tuple((t0 + t, gq, 4 * t + gq) for t in range(2) for gq in range(4))
    proj_p, st_ak, st_av, st_ckv, st_kr = _projection(
        *x["ctx"], N_SEG_P, D_MODEL, 0, D_MODEL, w_in, mt=mt[:N_SEG_P], name="even_in_proj_ctx",
        states=((N_HEADS, head_src(2)), (N_HEADS, head_src(4)), (0, ((8, 0, B_KV_RANK),)), (0, ((7, 256, B_ROPE_DIM),))),
        **in_kw)
    proj_l = _projection(*x["lat"], DEC_BATCH, D_MODEL, 0, D_MODEL, w_in, mt=mt[N_SEG_P:], rope=rope64,
                         sh=A_QK_DIM // 4, rope_groups={0: all_g, 1: all_g, 2: all_g, 3: all_g, 7: (2, 3)},
                         name="even_in_proj_lat", **in_kw)
    wq = ev_w_uq[i].reshape(B_Q_RANK, N_HEADS, HEAD_DIM + B_ROPE_DIM)
    w_uq = jnp.concatenate([wq[:, :, :HEAD_DIM].reshape(B_Q_RANK, -1), wq[:, :, HEAD_DIM:].reshape(B_Q_RANK, -1)],
                           axis=1).astype(BF16)
    mla_scale = (HEAD_DIM + B_ROPE_DIM) ** -0.5
    q_kw = dict(prologue="rmsnorm", g=ev_q_norm_g[i][None, :], out_dtype=BF16, out_scale=mla_scale * LOG2E)
    bq_p = _projection(proj_p, 0, N_SEG_P, 1024, 3, B_Q_RANK, w_uq, name="mla_q_up_ctx", **q_kw)
    bq_l = _projection(proj_l, 0, DEC_BATCH, 1024, 3, B_Q_RANK, w_uq, rope=rope64, sh=B_ROPE_DIM // 4,
                       rope_groups={2: all_g}, name="mla_q_up_lat", **q_kw)
    w_ukv = ev_w_ukv[i].astype(BF16)
    kv_kw = dict(prologue="cast", out_dtype=BF16)
    kv_p = _projection(proj_p, 0, N_SEG_P, 512, 8, B_KV_RANK, w_ukv, name="mla_kv_up_ctx", **kv_kw)
    kv_l = _projection(proj_l, 0, DEC_BATCH, 512, 8, B_KV_RANK, w_ukv, name="mla_kv_up_lat", **kv_kw)
    kv_ctx = _projection(cache_b_ckv[:, i].reshape(DEC_BATCH * PAST_LEN, B_KV_RANK), 0, 1, 512, 0, B_KV_RANK, w_ukv,
                         name="mla_kv_up_cache", **kv_kw)
    kv_ctx = kv_ctx.reshape(DEC_BATCH, PAST_LEN, 2 * N_HEADS * HEAD_DIM)
    krc = cache_b_krope[:, i]
    zc = jnp.zeros_like(krc)
    kr_ctx = jnp.concatenate([krc, zc, zc, krc], axis=-1)
    ak_ctx = cache_a_k[:, i].reshape(DEC_BATCH, PAST_LEN, N_HEADS * HEAD_DIM)
    av_ctx = cache_a_v[:, i].reshape(DEC_BATCH, PAST_LEN, N_HEADS * HEAD_DIM)
    lam_init = 0.8 - 0.6 * math.exp(-0.3 * l)
    o = jnp.zeros((N_TOK, D_MODEL), BF16)
    for latent, proj, b_q, kv in ((False, proj_p, bq_p, kv_p), (True, proj_l, bq_l, kv_l)):
        tag = "lat" if latent else "ctx"
        o = _attention(kind="diff", latent=latent, q_parts=[(proj, lambda h: h)], ko_parts=[(proj, lambda h: 8 + h)],
                       vo=(proj, lambda h: 16 + h), kc_parts=[(ak_ctx, lambda h: h)], vc=(av_ctx, lambda h: h),
                       lam=ev_lambda[i], subln=ev_subln_g[i][None, :], o_arr=o, o_blk=lambda h: h,
                       scale=A_QK_DIM ** -0.5, lam_init=lam_init, name="diff_attn_" + tag)
        o = _attention(kind="mla", latent=latent, q_parts=[(b_q, lambda h: h), (b_q, lambda h: 8 + h // 2)],
                       ko_parts=[(kv, lambda h: 2 * h), (proj, lambda h: 30 + h % 2)], vo=(kv, lambda h: 2 * h + 1),
                       kc_parts=[(kv_ctx, lambda h: 2 * h), (kr_ctx, lambda h: h % 2)], vc=(kv_ctx, lambda h: 2 * h + 1),
                       o_arr=o, o_blk=lambda h: 8 + h, scale=(HEAD_DIM + B_ROPE_DIM) ** -0.5, name="mla_attn_" + tag)
    y = _out_projection(o, ev_w_out[i].astype(BF16), x["parts"], mt)
    states = (st_ak.reshape(BATCH, SEQ, N_HEADS, HEAD_DIM), st_av.reshape(BATCH, SEQ, N_HEADS, HEAD_DIM),
              st_ckv.reshape(BATCH, SEQ, B_KV_RANK), st_kr.reshape(BATCH, SEQ, B_ROPE_DIM))
    return y, states


def _odd_layer(x, mt, l, i, rope128, cache_c_k, cache_c_v, cache_d_k, cache_d_v, norm_mix_g, od_w_in, od_rpb,
               od_sink, od_w_out):
    w_in = od_w_in[i].astype(BF16)
    all_g = (0, 1, 2, 3)
    in_kw = dict(prologue="modulate", g=norm_mix_g[l][None, :])
    head_src = lambda t0: tuple((t0 + t, gq, 4 * t + gq) for t in range(2) for gq in range(4))
    proj_p, st_ck, st_cv, st_dk, st_dv = _projection(
        *x["ctx"], N_SEG_P, D_MODEL, 0, D_MODEL, w_in, mt=mt[:N_SEG_P], name="odd_in_proj_ctx",
        states=((N_HEADS, head_src(2)), (N_HEADS, head_src(4)), (D_KV_HEADS, ((8, 0, 0), (8, 1, 1))),
                (D_KV_HEADS, ((8, 2, 0), (8, 3, 1)))), **in_kw)
    proj_l = _projection(*x["lat"], DEC_BATCH, D_MODEL, 0, D_MODEL, w_in, mt=mt[N_SEG_P:], rope=rope128,
                         sh=HEAD_DIM // 4, rope_groups={6: all_g, 7: all_g, 8: (0, 1)}, name="odd_in_proj_lat", **in_kw)
    ck_ctx = cache_c_k[:, i].reshape(DEC_BATCH, PAST_LEN, N_HEADS * HEAD_DIM)
    cv_ctx = cache_c_v[:, i].reshape(DEC_BATCH, PAST_LEN, N_HEADS * HEAD_DIM)
    dk_ctx = cache_d_k[:, i].reshape(DEC_BATCH, PAST_LEN, D_KV_HEADS * HEAD_DIM)
    dv_ctx = cache_d_v[:, i].reshape(DEC_BATCH, PAST_LEN, D_KV_HEADS * HEAD_DIM)
    bias = _neighbourhood_bias(od_rpb[i])
    sink = od_sink[i].astype(F32)
    o = jnp.zeros((N_TOK, D_MODEL), BF16)
    for latent, proj in ((False, proj_p), (True, proj_l)):
        tag = "lat" if latent else "ctx"
        o = _attention(kind="plain", latent=latent, q_parts=[(proj, lambda h: h)], ko_parts=[(proj, lambda h: 8 + h)],
                       vo=(proj, lambda h: 16 + h), kc_parts=[(ck_ctx, lambda h: h)], vc=(cv_ctx, lambda h: h),
                       bias=bias if latent else None, o_arr=o, o_blk=lambda h: h, scale=HEAD_DIM ** -0.5,
                       name="nbr_attn_" + tag)
        o = _attention(kind="plain", latent=latent, q_parts=[(proj, lambda h: 24 + h)],
                       ko_parts=[(proj, lambda h: 32 + h // D_GROUP)], vo=(proj, lambda h: 34 + h // D_GROUP),
                       kc_parts=[(dk_ctx, lambda h: h // D_GROUP)], vc=(dv_ctx, lambda h: h // D_GROUP), sink=sink,
                       o_arr=o, o_blk=lambda h: 8 + h, scale=HEAD_DIM ** -0.5, has_band=latent, tq_lat=256,
                       name="win_attn_" + tag)
    y = _out_projection(o, od_w_out[i].astype(BF16), x["parts"], mt)
    states = (st_ck.reshape(BATCH, SEQ, N_HEADS, HEAD_DIM), st_cv.reshape(BATCH, SEQ, N_HEADS, HEAD_DIM),
              st_dk.reshape(BATCH, SEQ, D_KV_HEADS, HEAD_DIM), st_dv.reshape(BATCH, SEQ, D_KV_HEADS, HEAD_DIM))
    return y, states


def _moe_layer(x, mt, g, router_w, router_b, layer, w_gate, w_up, w_down, final_g, final):
    h2, eidx, gates = _router(x, mt, g, router_w, router_b)
    src_tok, te, n_active, pos = _route_plan(eidx)
    ys = _moe_experts(h2, src_tok, te, n_active, layer, w_gate, w_up, w_down)
    ysel = ys.at[pos].get(mode="promise_in_bounds").reshape(2, N_TOK, D_MODEL // 2)
    out = _combine(x, ysel, gates.T, mt, final_g, final)
    return out if final else out[0]


def kernel(x_prompt, x_sample, cache_a_k, cache_a_v, cache_b_ckv, cache_b_krope, cache_c_k, cache_c_v, cache_d_k, cache_d_v, c, c_ctx, w_ada, b_ada, norm_mix_g, norm_ffn_g, ev_w_in, ev_lambda, ev_subln_g, ev_q_norm_g, ev_kv_norm_g, ev_w_uq, ev_w_ukv, ev_w_out, od_w_in, od_rpb, od_sink, od_w_out, router_w, router_b, moe_w_gate, moe_w_up, moe_w_down, final_g):
    xp = x_prompt.reshape(N_PROMPT, D_MODEL)
    xl = x_sample.reshape(DEC_BATCH * DEC_SEQ, D_MODEL)
    x = dict(ctx=(xp, 0), lat=(xl, 0), parts=[xp, xl])
    cond8 = jnp.concatenate([c_ctx[None, :], c, jnp.zeros((3, D_MODEL), F32)], axis=0)
    mod = _adaln(cond8, w_ada, b_ada)
    seg_row = np.array([0] * N_SEG_P + [1 + b for b in range(DEC_BATCH)])
    mt_all = mod[:, seg_row].reshape(DEPTH, N_SEG, 6, D_MODEL)
    rope64 = _rope_tables(A_QK_DIM)
    rope128 = _rope_tables(HEAD_DIM)
    even_states, odd_states = [], []
    for l in range(DEPTH):
        i = l // 2
        mt = mt_all[l]
        if l % 2 == 0:
            x, st = _even_layer(x, mt, l, i, rope64, cache_a_k, cache_a_v, cache_b_ckv, cache_b_krope, norm_mix_g,
                                ev_w_in, ev_lambda, ev_subln_g, ev_q_norm_g, ev_kv_norm_g, ev_w_uq, ev_w_ukv, ev_w_out)
            even_states.append(st)
        else:
            x, st = _odd_layer(x, mt, l, i, rope128, cache_c_k, cache_c_v, cache_d_k, cache_d_v, norm_mix_g,
                               od_w_in, od_rpb, od_sink, od_w_out)
            odd_states.append(st)
        x = _moe_layer(x, mt, norm_ffn_g[l][None, :], router_w, router_b, l, moe_w_gate, moe_w_up, moe_w_down,
                       final_g[None, :], final=(l == DEPTH - 1))
        if l < DEPTH - 1:
            x = dict(ctx=(x, 0), lat=(x, N_SEG_P), parts=[x])
    y_prompt = x[0].reshape(BATCH, SEQ, D_MODEL)
    y_sample = x[1].reshape(DEC_BATCH, DEC_SEQ, D_MODEL)
    new_even = tuple(jnp.stack([st[k] for st in even_states], axis=1) for k in range(4))
    new_odd = tuple(jnp.stack([st[k] for st in odd_states], axis=1) for k in range(4))
    return (y_prompt, y_sample) + new_even + new_odd
```

```python
import functools
import math

import numpy as np
import jax
import jax.numpy as jnp
from jax import lax
from jax.experimental import pallas as pl
from jax.experimental.pallas import tpu as pltpu

D_MODEL = 2048
BATCH = 32
SEQ = 256
DEPTH = 2
DEC_BATCH = 4
DEC_SEQ = 1024
PAST_LEN = 256
GRID_W = 64
HEAD_DIM = 128
N_HEADS = 8
A_QK_DIM = 64
B_Q_RANK = 768
B_KV_RANK = 512
B_ROPE_DIM = 64
NA_ROWS = 8
NA_COLS = 16
D_KV_HEADS = 2
D_GROUP = 4
D_WINDOW = 128
N_EXPERTS = 16
N_GROUPS = 4
D_EXPERT = 1024
ROPE_THETA = 10000.0
EPS = 1e-6
NEG = -1e30
LOG2E = 1.4426950408889634

SEG = 1024
N_PROMPT = BATCH * SEQ
N_TOK = N_PROMPT + DEC_BATCH * DEC_SEQ
N_SEG = N_TOK // SEG
N_SEG_P = N_PROMPT // SEG
PROJ_W = 4608
TN = 512
LANES = 128
MOE_TM = 256
MOE_ROWS = 2 * N_TOK + N_EXPERTS * MOE_TM
MOE_TILES = MOE_ROWS // MOE_TM

F32 = jnp.float32
BF16 = jnp.bfloat16


def _cparams(sem, vmem_mb):
    return pltpu.CompilerParams(dimension_semantics=sem, vmem_limit_bytes=vmem_mb * 1024 * 1024)


def _dot(a, b):
    return jnp.dot(a, b, preferred_element_type=F32)


def _dot_nt(a, b):
    return lax.dot_general(a, b, (((1,), (1,)), ((), ())), preferred_element_type=F32)


def _sigmoid(x):
    return 1.0 / (1.0 + jnp.exp(-x))


def _pack_bf16_pair(x):
    c = x.shape[1] // 2
    bits = pltpu.bitcast(x.astype(BF16).astype(F32), jnp.int32)
    return bits[:, :c] | lax.shift_right_logical(bits[:, c:], 16)


def _unpack_bf16_pair(w):
    hi = pltpu.bitcast(w & jnp.int32(-65536), F32)
    lo = pltpu.bitcast(lax.shift_left(w, 16), F32)
    return hi, lo


def _adaln_kernel(c_ref, w_ref, b_ref, o_ref):
    c = c_ref[...]
    a = (c * _sigmoid(c)).astype(BF16)
    o_ref[...] = _dot(a, w_ref[...].astype(BF16)) + b_ref[...]


def _adaln(cond8, w_ada, b_ada):
    tn = 1024
    n = 6 * D_MODEL
    return pl.pallas_call(
        _adaln_kernel,
        grid=(DEPTH, n // tn),
        in_specs=[
            pl.BlockSpec((8, D_MODEL), lambda l, j: (0, 0)),
            pl.BlockSpec((None, D_MODEL, tn), lambda l, j: (l, 0, j)),
            pl.BlockSpec((None, 1, tn), lambda l, j: (l, 0, j)),
        ],
        out_specs=pl.BlockSpec((None, 8, tn), lambda l, j: (l, 0, j)),
        out_shape=jax.ShapeDtypeStruct((DEPTH, 8, n), F32),
        compiler_params=_cparams(("arbitrary", "arbitrary"), 40),
        name="adaln",
    )(cond8, w_ada, b_ada.reshape(DEPTH, 1, n))


def _rope(a, cos, sp, sm, sh):
    return a * cos + pltpu.roll(a, sh, 1) * sp + pltpu.roll(a, LANES - sh, 1) * sm


def _proj_kernel(*refs, prologue, k_in, sh, n_tiles, rope_groups, norm_tile, states, tile_scale):
    x_ref, mt_ref, g_ref, w_ref, cos_ref, sp_ref, sm_ref, ng_ref, o_ref = refs[:9]
    st_refs = refs[9:-1]
    xn_ref = refs[-1]
    j = pl.program_id(1)

    @pl.when(j == 0)
    def _():
        x = x_ref[:, :k_in]
        if prologue == "cast":
            xn_ref[...] = x.astype(BF16)
        else:
            r = lax.rsqrt(jnp.mean(x * x, axis=-1, keepdims=True) + EPS)
            if prologue == "modulate":
                y = x * r * (g_ref[...] * (1.0 + mt_ref[0, 1:2, :])) + mt_ref[0, 0:1, :]
            else:
                y = x * r * g_ref[...]
            xn_ref[...] = y.astype(BF16)

    acc = _dot(xn_ref[...], w_ref[...])
    n_grp = acc.shape[1] // LANES

    def treatment(t):
        acts = []
        for k, (n_heads, src) in enumerate(states):
            dest = 0
            for tt, a, b in src:
                if tt == t:
                    acts.append((k, a, b, dest))
                dest += 0 if n_heads else b
        return (tuple(rope_groups.get(t, ())), t == norm_tile, tuple(acts), float(tile_scale.get(t, 1.0)))

    branches = {}
    for t in range(n_tiles):
        branches.setdefault(treatment(t), []).append(t)

    o_ref[...] = acc.astype(o_ref.dtype)
    reread = o_ref.dtype == F32

    for (rg, is_norm, acts, sc), tiles in branches.items():
        if not rg and not is_norm and not acts and sc == 1.0:
            continue
        cond = j == tiles[0]
        for t in tiles[1:]:
            cond = jnp.logical_or(cond, j == t)

        @pl.when(cond)
        def _(rg=rg, is_norm=is_norm, acts=acts, sc=sc):
            vals = [(o_ref if reread else acc)[:, g * LANES:(g + 1) * LANES] for g in range(n_grp)]
            if sc != 1.0:
                vals = [v * sc for v in vals]
            if is_norm:
                ssq = None
                for v in vals:
                    s = jnp.sum(v * v, axis=-1, keepdims=True)
                    ssq = s if ssq is None else ssq + s
                scale = lax.rsqrt(ssq * (1.0 / (n_grp * LANES)) + EPS)
                ng = ng_ref[...]
                vals = [v * scale * ng[:, g * LANES:(g + 1) * LANES] for g, v in enumerate(vals)]
            elif rg:
                cos, sp, sm = cos_ref[...], sp_ref[...], sm_ref[...]
                vals = [_rope(v, cos, sp, sm, sh) if g in rg else v for g, v in enumerate(vals)]
            if is_norm or rg or sc != 1.0:
                for g, v in enumerate(vals):
                    if is_norm or sc != 1.0 or g in rg:
                        o_ref[:, g * LANES:(g + 1) * LANES] = v.astype(o_ref.dtype)
            for k, a, b, dest in acts:
                n_heads = states[k][0]
                if n_heads:
                    st_refs[k][pl.ds(b, SEG, stride=n_heads), :] = vals[a]
                elif b % LANES == 0:
                    for q in range(b // LANES):
                        st_refs[k][:, dest + q * LANES:dest + (q + 1) * LANES] = vals[a // LANES + q]
                else:
                    st_refs[k][...] = vals[a // LANES][:, a % LANES:a % LANES + b]


def _projection(x, row_off, n_seg, x_blk_w, x_blk_idx, k_in, w, *, prologue, mt=None, g=None, rope=None, sh=16,
                rope_groups=None, norm_tile=None, norm_g=None, states=(), out_dtype=F32, tile_scale=None, name):
    n = w.shape[1]
    m = n_seg * SEG
    assert n % TN == 0 and w.shape[0] == k_in
    if mt is None:
        mt = jnp.zeros((n_seg, 6, LANES), F32)
    if g is None:
        g = jnp.ones((1, k_in), F32)
    if rope is None:
        rope = tuple(jnp.zeros((8, LANES), F32) for _ in range(3))
    if norm_g is None:
        norm_g = jnp.ones((1, TN), F32)
    kern = functools.partial(_proj_kernel, prologue=prologue, k_in=k_in, sh=sh, n_tiles=n // TN,
                             rope_groups=dict(rope_groups or {}), norm_tile=norm_tile, states=tuple(states),
                             tile_scale=dict(tile_scale or {}))
    full2 = lambda i, j: (0, 0)
    out_shape = [jax.ShapeDtypeStruct((m, n), out_dtype)]
    out_specs = [pl.BlockSpec((SEG, TN), lambda i, j: (i, j))]
    for n_heads, src in states:
        if n_heads:
            out_shape.append(jax.ShapeDtypeStruct((m * n_heads, LANES), F32))
            out_specs.append(pl.BlockSpec((SEG * n_heads, LANES), lambda i, j: (i, 0)))
        else:
            width = sum(b for _, _, b in src)
            out_shape.append(jax.ShapeDtypeStruct((m, width), F32))
            out_specs.append(pl.BlockSpec((SEG, width), lambda i, j: (i, 0)))
    x_kw = dict(pipeline_mode=pl.Buffered(1)) if states else {}
    res = pl.pallas_call(
        kern,
        grid=(n_seg, n // TN),
        in_specs=[
            pl.BlockSpec((SEG, x_blk_w), lambda i, j: (i + row_off, x_blk_idx), **x_kw),
            pl.BlockSpec((1, 6, mt.shape[2]), lambda i, j: (i, 0, 0)),
            pl.BlockSpec((1, k_in), full2),
            pl.BlockSpec((k_in, TN), lambda i, j: (0, j)),
            pl.BlockSpec(rope[0].shape, full2),
            pl.BlockSpec(rope[1].shape, full2),
            pl.BlockSpec(rope[2].shape, full2),
            pl.BlockSpec((1, TN), full2),
        ],
        out_specs=out_specs,
        out_shape=out_shape,
        scratch_shapes=[pltpu.VMEM((SEG, k_in), BF16)],
        compiler_params=_cparams(("arbitrary", "arbitrary"), 56 if states else 48),
        name=name,
    )(x, mt, g, w, *rope, norm_g)
    return res if states else res[0]


def _attn_kernel(*refs, kind, n_parts, has_ctx, has_bias, has_band, has_sink, nb, s_len, tq, lam_init, own_win):
    it = iter(refs)
    q_refs = [next(it) for _ in range(n_parts)]
    ko_refs = [next(it) for _ in range(n_parts)]
    vo_ref = next(it)
    kc_refs, vc_ref = [], None
    if has_ctx:
        kc_refs = [next(it) for _ in range(n_parts)]
        vc_ref = next(it)
    bias_ref = next(it) if has_bias else None
    lam_ref = sg_ref = None
    if kind == "diff":
        lam_ref = next(it)
        sg_ref = next(it)
    sink_ref = next(it) if has_sink else None
    o_ref = next(it)

    sink2 = sink_ref[pl.program_id(1)] * LOG2E if has_sink else None
    lane = lax.broadcasted_iota(jnp.int32, (1, LANES), 1)

    def softmax_pv(s_list, v_list):
        m = None
        for s in s_list:
            mm = jnp.max(s, axis=-1, keepdims=True)
            m = mm if m is None else jnp.maximum(m, mm)
        if sink2 is not None:
            m = jnp.maximum(m, sink2)
        l = jnp.exp2(sink2 - m) if sink2 is not None else None
        o = None
        for s, v in zip(s_list, v_list):
            e = jnp.exp2(s - m)
            ss = jnp.sum(e, axis=-1, keepdims=True)
            l = ss if l is None else l + ss
            pv = _dot(e.astype(BF16), v)
            o = pv if o is None else o + pv
        return o, l

    for bi in range(nb):
        qrows = pl.ds(bi * tq, tq)
        q0 = pl.program_id(2) * tq
        if own_win is None:
            krows, k0 = pl.ds(bi * s_len, s_len), 0
            n_own = s_len
        else:
            k0 = pl.multiple_of(jnp.clip(q0 - D_WINDOW, 0, s_len - own_win), LANES)
            krows = pl.ds(k0, own_win)
            n_own = own_win
        k_srcs, v_srcs, is_own = [], [], []
        if has_ctx:
            k_srcs.append([r[...].astype(BF16) for r in kc_refs])
            v_srcs.append(vc_ref[...].astype(BF16))
            is_own.append(False)
        k_srcs.append([r[krows, :].astype(BF16) for r in ko_refs])
        v_srcs.append(vo_ref[krows, :].astype(BF16))
        is_own.append(True)

        def mask_own(s):
            if has_bias:
                s = s + bias_ref[...]
            if has_band:
                qpos = q0 + lax.broadcasted_iota(jnp.int32, (tq, 1), 0)
                kpos = k0 + lax.broadcasted_iota(jnp.int32, (1, n_own), 1)
                s = jnp.where(jnp.abs(kpos - qpos) <= D_WINDOW, s, NEG)
            return s

        if kind == "diff":
            q = q_refs[0][qrows, :]
            zero = jnp.zeros_like(q)
            q1 = jnp.where(lane < A_QK_DIM, q, zero)
            q2 = jnp.where(lane < A_QK_DIM, zero, q)
            o1, l1 = softmax_pv([_dot_nt(q1, ks[0]) for ks in k_srcs], v_srcs)
            o2, l2 = softmax_pv([_dot_nt(q2, ks[0]) for ks in k_srcs], v_srcs)
            lv = lam_ref[...]
            lam = (jnp.exp(jnp.sum(lv[0:1] * lv[1:2], axis=-1, keepdims=True))
                   - jnp.exp(jnp.sum(lv[2:3] * lv[3:4], axis=-1, keepdims=True)) + lam_init)
            o = o1 * (1.0 / l1) - o2 * (lam / l2)
            o = o * lax.rsqrt(jnp.mean(o * o, axis=-1, keepdims=True) + EPS) * sg_ref[...] * (1.0 - lam_init)
        else:
            qs = [r[qrows, :] for r in q_refs]
            s_list = []
            for ks, own in zip(k_srcs, is_own):
                s = None
                for qp, kp in zip(qs, ks):
                    d = _dot_nt(qp, kp)
                    s = d if s is None else s + d
                s_list.append(mask_own(s) if own else s)
            o, l = softmax_pv(s_list, v_srcs)
            o = o * (1.0 / l)
        o_ref[qrows, :] = o.astype(o_ref.dtype)


def _attention(*, kind, latent, q_parts, ko_parts, vo, kc_parts=(), vc=None, bias=None, lam=None, subln=None,
               sink=None, o_arr, o_blk, lam_init=0.0, has_band=False, tq_lat=512, name):
    n_parts = len(q_parts)
    has_ctx = latent
    if latent:
        tq, s_len, nb = tq_lat, DEC_SEQ, 1
        grid = (DEC_BATCH, N_HEADS, DEC_SEQ // tq)
        qpb = DEC_SEQ // tq
        q_row = lambda b, h, t: b * qpb + t
        o_row = lambda b, h, t: N_PROMPT // tq + b * qpb + t
        k_row = lambda b, h, t: b
        qblk = tq
    else:
        tq, s_len, nb = SEQ, SEQ, SEG // SEQ
        grid = (N_SEG_P, N_HEADS, 1)
        q_row = o_row = k_row = lambda b, h, t: b
        qblk = SEG
    args, specs = [], []

    def add(arr, shape, imap, **kw):
        args.append(arr)
        specs.append(pl.BlockSpec(shape, imap, **kw))

    for arr, f in q_parts:
        add(arr, (qblk, LANES), lambda b, h, t, f=f: (q_row(b, h, t), f(h)))
    for arr, f in list(ko_parts) + [vo]:
        add(arr, (SEG, LANES), lambda b, h, t, f=f: (k_row(b, h, t), f(h)))
    if has_ctx:
        for arr, f in list(kc_parts) + [vc]:
            add(arr, (None, PAST_LEN, LANES), lambda b, h, t, f=f: (b, 0, f(h)))
    if bias is not None:
        add(bias, (None, tq, DEC_SEQ), lambda b, h, t: (h, t, 0))
    if kind == "diff":
        add(lam, lam.shape, lambda b, h, t: (0, 0))
        add(subln, subln.shape, lambda b, h, t: (0, 0))
    if sink is not None:
        args.append(sink)
        specs.append(pl.BlockSpec(memory_space=pltpu.SMEM))
    n_in = len(args)
    args.append(o_arr)
    specs.append(pl.BlockSpec(memory_space=pl.ANY))
    kern = functools.partial(_attn_kernel_aliased, kind=kind, n_parts=n_parts, has_ctx=has_ctx,
                             has_bias=bias is not None, has_band=has_band, has_sink=sink is not None,
                             nb=nb, s_len=s_len, tq=tq, lam_init=lam_init,
                             own_win=(tq + 2 * D_WINDOW) if has_band else None)
    return pl.pallas_call(
        kern,
        grid=grid,
        in_specs=specs,
        out_specs=pl.BlockSpec((qblk, LANES), lambda b, h, t: (o_row(b, h, t), o_blk(h))),
        out_shape=jax.ShapeDtypeStruct(o_arr.shape, o_arr.dtype),
        input_output_aliases={n_in: 0},
        compiler_params=_cparams(("arbitrary", "arbitrary", "arbitrary"), 48),
        name=name,
    )(*args)


def _attn_kernel_aliased(*refs, **kw):
    _attn_kernel(*refs[:-2], refs[-1], **kw)


def _outproj_kernel(o_ref, w_ref, mt_ref, *rest, seg_ranges):
    x_refs, y_ref = rest[:-1], rest[-1]
    upd = mt_ref[0, 2:3, :] * _dot(o_ref[...], w_ref[...])
    if len(x_refs) == 1:
        y_ref[...] = x_refs[0][...] + upd
    else:
        i = pl.program_id(0)
        for x_ref, (lo, hi) in zip(x_refs, seg_ranges):
            @pl.when(jnp.logical_and(i >= lo, i < hi))
            def _(x_ref=x_ref):
                y_ref[...] = x_ref[...] + upd


def _out_projection(o, w, x_parts, mt):
    seg_ranges, lo = [], 0
    for xp in x_parts:
        seg_ranges.append((lo, lo + xp.shape[0] // SEG))
        lo = seg_ranges[-1][1]
    assert lo == N_SEG
    x_specs = [pl.BlockSpec((SEG, TN), lambda i, j, lo=lo, hi=hi: (
        jnp.clip(i - lo, 0, hi - lo - 1), jnp.where(jnp.logical_and(i >= lo, i < hi), j, 0)))
               for lo, hi in seg_ranges]
    return pl.pallas_call(
        functools.partial(_outproj_kernel, seg_ranges=tuple(seg_ranges)),
        grid=(N_SEG, D_MODEL // TN),
        in_specs=[
            pl.BlockSpec((SEG, D_MODEL), lambda i, j: (i, 0)),
            pl.BlockSpec((D_MODEL, TN), lambda i, j: (0, j)),
            pl.BlockSpec((1, 6, TN), lambda i, j: (i, 0, j)),
        ] + x_specs,
        out_specs=pl.BlockSpec((SEG, TN), lambda i, j: (i, j)),
        out_shape=jax.ShapeDtypeStruct((N_TOK, D_MODEL), F32),
        compiler_params=_cparams(("arbitrary", "arbitrary"), 40),
        name="out_proj",
    )(o, w, mt, *x_parts)


def _router_kernel(x_ref, mt_ref, g_ref, wh_ref, wl_ref, rb_ref, h_ref, e_ref, gt_ref):
    x = x_ref[...]
    y = x * lax.rsqrt(jnp.mean(x * x, axis=-1, keepdims=True) + EPS) * g_ref[...]
    h = y * (1.0 + mt_ref[0, 4:5, :]) + mt_ref[0, 3:4, :]
    hh = h.astype(BF16)
    h_ref[...] = _pack_bf16_pair(h)
    hl = (h - hh.astype(F32)).astype(BF16)
    wh, wl = wh_ref[...], wl_ref[...]
    logits = _dot_nt(wh, hh) + _dot_nt(wh, hl) + _dot_nt(wl, hh)
    scores = _sigmoid(logits)
    sel = scores + rb_ref[...]
    per = N_EXPERTS // N_GROUPS
    sc = [scores[e:e + 1, :] for e in range(N_EXPERTS)]
    sl = [sel[e:e + 1, :] for e in range(N_EXPERTS)]
    best_g, best_v = None, None
    for gi in range(N_GROUPS):
        a, b, c, d = sl[gi * per:(gi + 1) * per]
        hi1, lo1, hi2, lo2 = jnp.maximum(a, b), jnp.minimum(a, b), jnp.maximum(c, d), jnp.minimum(c, d)
        gs = jnp.maximum(hi1, hi2) + jnp.maximum(jnp.minimum(hi1, hi2), jnp.maximum(lo1, lo2))
        if gi == 0:
            best_g, best_v = jnp.zeros_like(gs, dtype=jnp.int32), gs
        else:
            better = gs > best_v
            best_g = jnp.where(better, gi, best_g)
            best_v = jnp.where(better, gs, best_v)
    masked = [jnp.where(best_g == (e // per), sl[e], NEG) for e in range(N_EXPERTS)]
    i1, v1 = jnp.zeros_like(best_g), masked[0]
    for e in range(1, N_EXPERTS):
        better = masked[e] > v1
        i1 = jnp.where(better, e, i1)
        v1 = jnp.where(better, masked[e], v1)
    i2, v2 = None, None
    for e in range(N_EXPERTS):
        cand = jnp.where(i1 == e, -2e30, masked[e])
        if e == 0:
            i2, v2 = jnp.zeros_like(best_g), cand
        else:
            better = cand > v2
            i2 = jnp.where(better, e, i2)
            v2 = jnp.where(better, cand, v2)
    g1 = jnp.zeros_like(v1)
    g2 = jnp.zeros_like(v1)
    for e in range(N_EXPERTS):
        g1 = jnp.where(i1 == e, sc[e], g1)
        g2 = jnp.where(i2 == e, sc[e], g2)
    tot = g1 + g2
    rows = i1.shape[1]
    e_ref[...] = jnp.concatenate([i1, i2, jnp.zeros((6, rows), jnp.int32)], axis=0)
    gt_ref[...] = jnp.concatenate([g1 / tot, g2 / tot, jnp.zeros((6, rows), F32)], axis=0)


def _router(x, mt, g, router_w, router_b):
    wt = router_w.T
    wh = wt.astype(BF16)
    wl = (wt - wh.astype(F32)).astype(BF16)
    return pl.pallas_call(
        _router_kernel,
        grid=(N_SEG,),
        in_specs=[
            pl.BlockSpec((SEG, D_MODEL), lambda i: (i, 0)),
            pl.BlockSpec((1, 6, D_MODEL), lambda i: (i, 0, 0)),
            pl.BlockSpec((1, D_MODEL), lambda i: (0, 0)),
            pl.BlockSpec((N_EXPERTS, D_MODEL), lambda i: (0, 0)),
            pl.BlockSpec((N_EXPERTS, D_MODEL), lambda i: (0, 0)),
            pl.BlockSpec((N_EXPERTS, 1), lambda i: (0, 0)),
        ],
        out_specs=[
            pl.BlockSpec((SEG, D_MODEL // 2), lambda i: (i, 0)),
            pl.BlockSpec((8, SEG), lambda i: (0, i)),
            pl.BlockSpec((8, SEG), lambda i: (0, i)),
        ],
        out_shape=[
            jax.ShapeDtypeStruct((N_TOK, D_MODEL // 2), jnp.int32),
            jax.ShapeDtypeStruct((8, N_TOK), jnp.int32),
            jax.ShapeDtypeStruct((8, N_TOK), F32),
        ],
        compiler_params=_cparams(("arbitrary",), 48),
        name="ffn_norm_router",
    )(x, mt, g, wh, wl, router_b.reshape(N_EXPERTS, 1))


def _expert_weights(te_ref, first_ref, slot_ref, nxt_ref, i, t, layer, w_hbm, stage_ref, sem_ref, w_bf_refs):
    def copies(e, s):
        return [pltpu.make_async_copy(w.at[layer, e], stage_ref.at[s, k], sem_ref.at[s, k])
                for k, w in enumerate(w_hbm)]

    s = slot_ref[t]

    @pl.when(i == 0)
    def _():
        for c in copies(te_ref[t], s):
            c.start()

    @pl.when(first_ref[t] == 1)
    def _():
        for c in copies(te_ref[t], s):
            c.wait()

        @pl.when(nxt_ref[t] >= 0)
        def _():
            for c in copies(nxt_ref[t], 1 - s):
                c.start()

        for k, w_bf in enumerate(w_bf_refs):
            w_bf[...] = stage_ref[s, k].astype(BF16)


def _moe_up_kernel(te_ref, na_ref, first_ref, slot_ref, nxt_ref, xs_ref, wg_hbm, wu_hbm, hid_ref,
                   stage_ref, wgb_ref, wub_ref, sem_ref, *, tile0, layer):
    i = pl.program_id(0)
    t = i + tile0
    _expert_weights(te_ref, first_ref, slot_ref, nxt_ref, i, t, layer, (wg_hbm, wu_hbm), stage_ref, sem_ref,
                    (wgb_ref, wub_ref))

    @pl.when(t < na_ref[0])
    def _():
        x_hi, x_lo = _unpack_bf16_pair(xs_ref[...])
        x_hi, x_lo = x_hi.astype(BF16), x_lo.astype(BF16)
        half = D_MODEL // 2
        g = _dot(x_hi, wgb_ref[:half, :]) + _dot(x_lo, wgb_ref[half:, :])
        u = _dot(x_hi, wub_ref[:half, :]) + _dot(x_lo, wub_ref[half:, :])
        hid_ref[...] = (g * _sigmoid(g) * u).astype(BF16)

    @pl.when(t >= na_ref[0])
    def _():
        hid_ref[...] = jnp.zeros_like(hid_ref)


def _moe_down_kernel(te_ref, na_ref, first_ref, slot_ref, nxt_ref, *rest, layer, chunk_tiles):
    ys_ref, stage_ref, wdb_ref, sem_ref = rest[-4:]
    wd_hbm = rest[-5]
    hid_refs = rest[:-5]
    t = pl.program_id(0)
    _expert_weights(te_ref, first_ref, slot_ref, nxt_ref, t, t, layer, (wd_hbm,), stage_ref, sem_ref, (wdb_ref,))

    for c, hid_ref in enumerate(hid_refs):
        @pl.when(jnp.logical_and(t < na_ref[0], t // chunk_tiles == c))
        def _(hid_ref=hid_ref):
            ys_ref[...] = _pack_bf16_pair(_dot(hid_ref[...], wdb_ref[...]))

    @pl.when(t >= na_ref[0])
    def _():
        ys_ref[...] = jnp.zeros_like(ys_ref)


MOE_CHUNKS = 2


def _moe_experts(h2, src_tok, tile_expert, n_active, layer, w_gate, w_up, w_down):
    tiles = MOE_TILES // MOE_CHUNKS
    rows = tiles * MOE_TM

    def weight_runs(call_tiles):
        t_ids = jnp.arange(MOE_TILES, dtype=jnp.int32)
        prev = jnp.concatenate([tile_expert[:1], tile_expert[:-1]])
        first = jnp.logical_or(t_ids % call_tiles == 0, tile_expert != prev)
        slot = ((jnp.cumsum(first.astype(jnp.int32)) - 1) % 2).astype(jnp.int32)
        first_at = lax.cummin(jnp.where(first, t_ids, MOE_TILES), reverse=True)
        next_at = jnp.concatenate([first_at[1:], jnp.full((1,), MOE_TILES, jnp.int32)])
        same_call = jnp.logical_and(next_at < MOE_TILES, next_at // call_tiles == t_ids // call_tiles)
        nxt = jnp.where(jnp.logical_and(first, same_call), tile_expert[jnp.minimum(next_at, MOE_TILES - 1)], -1)
        return (tile_expert, n_active, first.astype(jnp.int32), slot, nxt.astype(jnp.int32))

    row_blk = lambda i, *_: (i, 0)
    hbm = pl.BlockSpec(memory_space=pl.ANY)
    plan = weight_runs(tiles)
    hids = []
    for c in range(MOE_CHUNKS):
        xs = h2.at[src_tok[c * rows:(c + 1) * rows]].get(mode="promise_in_bounds")
        hids.append(pl.pallas_call(
            functools.partial(_moe_up_kernel, tile0=c * tiles, layer=layer),
            grid_spec=pltpu.PrefetchScalarGridSpec(
                num_scalar_prefetch=len(plan),
                grid=(tiles,),
                in_specs=[pl.BlockSpec((MOE_TM, D_MODEL // 2), row_blk), hbm, hbm],
                out_specs=pl.BlockSpec((MOE_TM, D_EXPERT), row_blk),
                scratch_shapes=[pltpu.VMEM((2, 2, D_MODEL, D_EXPERT), F32),
                                pltpu.VMEM((D_MODEL, D_EXPERT), BF16), pltpu.VMEM((D_MODEL, D_EXPERT), BF16),
                                pltpu.SemaphoreType.DMA((2, 2))],
            ),
            out_shape=jax.ShapeDtypeStruct((rows, D_EXPERT), BF16),
            compiler_params=_cparams(("arbitrary",), 52),
            name="moe_up",
        )(*plan, xs, w_gate, w_up))
    plan = weight_runs(MOE_TILES)
    hid_specs = [pl.BlockSpec((MOE_TM, D_EXPERT), lambda i, *_, c=c: (jnp.clip(i - c * tiles, 0, tiles - 1), 0))
                 for c in range(MOE_CHUNKS)]
    return pl.pallas_call(
        functools.partial(_moe_down_kernel, layer=layer, chunk_tiles=tiles),
        grid_spec=pltpu.PrefetchScalarGridSpec(
            num_scalar_prefetch=len(plan),
            grid=(MOE_TILES,),
            in_specs=hid_specs + [hbm],
            out_specs=pl.BlockSpec((MOE_TM, D_MODEL // 2), row_blk),
            scratch_shapes=[pltpu.VMEM((2, 1, D_EXPERT, D_MODEL), F32), pltpu.VMEM((D_EXPERT, D_MODEL), BF16),
                            pltpu.SemaphoreType.DMA((2, 1))],
        ),
        out_shape=jax.ShapeDtypeStruct((MOE_ROWS, D_MODEL // 2), jnp.int32),
        compiler_params=_cparams(("arbitrary",), 40),
        name="moe_down",
    )(*plan, *hids, w_down)


PLAN_R = 2 * N_TOK // LANES


def _plan_kernel(e_ref, pos_ref, meta_ref):
    e = e_ref[...]
    r_i = lax.broadcasted_iota(jnp.int32, (LANES, LANES), 0)
    c_i = lax.broadcasted_iota(jnp.int32, (LANES, LANES), 1)
    upper = jnp.where(r_i <= c_i, 1.0, 0.0).astype(BF16)
    r_j = lax.broadcasted_iota(jnp.int32, (PLAN_R, PLAN_R), 0)
    c_j = lax.broadcasted_iota(jnp.int32, (PLAN_R, PLAN_R), 1)
    lower = jnp.where(c_j < r_j, 1.0, 0.0).astype(BF16)
    lane = lax.broadcasted_iota(jnp.int32, (1, LANES), 1)
    pos = jnp.zeros((PLAN_R, LANES), F32)
    cnt_row = jnp.zeros((1, LANES), F32)
    off_row = jnp.zeros((1, LANES), F32)
    end_row = []
    row_off = jnp.zeros((1, 1), F32)
    tiles_done = jnp.zeros((1, 1), F32)
    for ex in range(N_EXPERTS):
        m = jnp.where(e == ex, 1.0, 0.0)
        inc = _dot(m.astype(BF16), upper)
        tot = jnp.broadcast_to(inc[:, LANES - 1:LANES], (PLAN_R, LANES))
        before = _dot(lower, tot.astype(BF16))
        cnt = jnp.sum(inc[:, LANES - 1:LANES], axis=0, keepdims=True)
        pos = pos + m * (row_off + before + inc - 1.0)
        tiles = jnp.floor((cnt + (MOE_TM - 1)) * (1.0 / MOE_TM))
        cnt_row = jnp.where(lane == ex, cnt, cnt_row)
        off_row = jnp.where(lane == ex, row_off, off_row)
        tiles_done = tiles_done + tiles
        end_row.append(tiles_done)
        row_off = row_off + tiles * MOE_TM
    n_active = tiles_done
    tid = jnp.minimum(lane.astype(F32), n_active - 1.0)
    te = jnp.zeros((1, LANES), F32)
    for ex in range(N_EXPERTS):
        te = te + jnp.where(end_row[ex] <= tid, 1.0, 0.0)
    pos_ref[...] = pos.astype(jnp.int32)
    meta = jnp.concatenate([cnt_row, off_row, te, jnp.broadcast_to(n_active, (1, LANES)),
                            jnp.zeros((4, LANES), F32)], axis=0)
    meta_ref[...] = meta.astype(jnp.int32)


def _route_plan(eidx):
    e2 = eidx[:2].reshape(PLAN_R, LANES)
    pos, meta = pl.pallas_call(
        _plan_kernel,
        out_shape=[jax.ShapeDtypeStruct((PLAN_R, LANES), jnp.int32), jax.ShapeDtypeStruct((8, LANES), jnp.int32)],
        compiler_params=pltpu.CompilerParams(vmem_limit_bytes=32 * 1024 * 1024),
        name="route_plan",
    )(e2)
    cnt, row_off = meta[0, :N_EXPERTS], meta[1, :N_EXPERTS]
    te, n_active = meta[2, :MOE_TILES], meta[3, :1]
    start = jnp.cumsum(cnt) - cnt
    order = jnp.argsort(e2.reshape(-1), stable=True).astype(jnp.int32)
    row_e = jnp.repeat(te, MOE_TM)
    rank = jnp.arange(MOE_ROWS, dtype=jnp.int32) - row_off[row_e]
    valid = rank < cnt[row_e]
    src_asg = order[jnp.clip(start[row_e] + rank, 0, 2 * N_TOK - 1)]
    rows = jnp.arange(MOE_ROWS, dtype=jnp.int32)
    src_tok = jnp.where(valid, src_asg % N_TOK, rows % N_TOK)
    return src_tok, te, n_active, pos.reshape(-1)


def _combine_kernel(x_ref, y0_ref, y1_ref, gt_ref, mt_ref, fg_ref, *o_refs, final, tm):
    if final:
        i = pl.program_id(0)
        for o_ref, cond in zip(o_refs, (i < N_PROMPT // tm, i >= N_PROMPT // tm)):
            pl.when(cond)(functools.partial(_combine_tile, x_ref, y0_ref, y1_ref, gt_ref, mt_ref, fg_ref, o_ref, True))
    else:
        _combine_tile(x_ref, y0_ref, y1_ref, gt_ref, mt_ref, fg_ref, o_refs[0], False)


def _combine_tile(x_ref, y0_ref, y1_ref, gt_ref, mt_ref, fg_ref, o_ref, final):
    gt = gt_ref[...]
    g0, g1 = gt[:, 0:1], gt[:, 1:2]
    half = D_MODEL // 2
    y0 = _unpack_bf16_pair(y0_ref[...])
    y1 = _unpack_bf16_pair(y1_ref[...])
    xs = []
    for c in range(2):
        cols = slice(c * half, (c + 1) * half)
        xs.append(x_ref[:, cols] + mt_ref[0, 5:6, cols] * (y0[c] * g0 + y1[c] * g1))
    if final:
        ssq = jnp.sum(xs[0] * xs[0], axis=-1, keepdims=True) + jnp.sum(xs[1] * xs[1], axis=-1, keepdims=True)
        r = lax.rsqrt(ssq * (1.0 / D_MODEL) + EPS)
        xs = [xc * r * fg_ref[:, c * half:(c + 1) * half] for c, xc in enumerate(xs)]
    for c, xc in enumerate(xs):
        o_ref[:, c * half:(c + 1) * half] = xc


def _combine(x, ysel, gates_t, mt, final_g, final):
    tm = 256
    n_p = N_PROMPT // tm
    if final:
        out_specs = [pl.BlockSpec((tm, D_MODEL), lambda i: (jnp.minimum(i, n_p - 1), 0)),
                     pl.BlockSpec((tm, D_MODEL), lambda i: (jnp.maximum(i - n_p, 0), 0))]
        out_shape = [jax.ShapeDtypeStruct((N_PROMPT, D_MODEL), F32),
                     jax.ShapeDtypeStruct((N_TOK - N_PROMPT, D_MODEL), F32)]
    else:
        out_specs = [pl.BlockSpec((tm, D_MODEL), lambda i: (i, 0))]
        out_shape = [jax.ShapeDtypeStruct((N_TOK, D_MODEL), F32)]
    return pl.pallas_call(
        functools.partial(_combine_kernel, final=final, tm=tm),
        grid=(N_TOK // tm,),
        in_specs=[
            pl.BlockSpec((tm, D_MODEL), lambda i: (i, 0)),
            pl.BlockSpec((None, tm, D_MODEL // 2), lambda i: (0, i, 0)),
            pl.BlockSpec((None, tm, D_MODEL // 2), lambda i: (1, i, 0)),
            pl.BlockSpec((tm, 8), lambda i: (i, 0)),
            pl.BlockSpec((1, 6, D_MODEL), lambda i: (i // (SEG // tm), 0, 0)),
            pl.BlockSpec((1, D_MODEL), lambda i: (0, 0)),
        ],
        out_specs=out_specs,
        out_shape=out_shape,
        compiler_params=_cparams(("arbitrary",), 40),
        name="moe_combine",
    )(x, ysel, ysel, gates_t, mt, final_g)


def _rope_tables(dim):
    half = dim // 2
    inv = ROPE_THETA ** (-jnp.arange(0, half, 2, dtype=F32) / half)
    t = jnp.arange(DEC_SEQ)
    ang_r = (t // GRID_W).astype(F32)[:, None] * inv[None, :]
    ang_c = (t % GRID_W).astype(F32)[:, None] * inv[None, :]
    ang = jnp.concatenate([ang_r, ang_r, ang_c, ang_c], axis=-1)
    cos, sin = jnp.cos(ang), jnp.sin(ang)
    reps = LANES // dim
    cos = jnp.tile(cos, (1, reps))
    sin = jnp.tile(sin, (1, reps))
    sh = dim // 4
    second = (np.arange(LANES) % (2 * sh)) >= sh
    sp = jnp.where(second[None, :], sin, 0.0)
    sm = jnp.where(second[None, :], 0.0, -sin)
    return cos, sp, sm


def _neighbourhood_bias(rpb):
    rows = DEC_SEQ // GRID_W
    kh = min(NA_ROWS, rows)
    r = np.arange(rows)
    r0 = np.clip(r - kh // 2, 0, rows - kh)
    kr = np.arange(rows)
    row_ok = (kr[None, :] >= r0[:, None]) & (kr[None, :] < r0[:, None] + kh)
    c = np.arange(GRID_W)
    ws = np.clip(c - NA_COLS // 2, 0, GRID_W - NA_COLS)
    kc = np.arange(GRID_W)
    col_ok = (kc[None, :] >= ws[:, None]) & (kc[None, :] < ws[:, None] + NA_COLS)
    dc_idx = np.clip(kc[None, :] - c[:, None], -(NA_COLS - 1), NA_COLS - 1) + NA_COLS - 1
    sel_c = ((np.arange(2 * NA_COLS - 1)[:, None, None] == dc_idx[None]) & col_ok[None]).astype(np.float32)
    t = jnp.einsum("hab,bcx->hacx", rpb.astype(F32), sel_c, precision=lax.Precision.HIGHEST)
    t = jnp.where(col_ok[None, None], t * LOG2E, NEG)
    t = jnp.concatenate([t, t], axis=-1)

    def build(t_ref, o_ref):
        left = lax.broadcasted_iota(jnp.int32, (1, LANES), 1) < GRID_W
        neg = jnp.full((GRID_W, LANES), NEG, F32)
        for rr in range(rows):
            for p in range(rows // 2):
                parts = [t_ref[k - rr + NA_ROWS - 1] if row_ok[rr, k] else neg for k in (2 * p, 2 * p + 1)]
                blk = jnp.where(left, parts[0], parts[1]) if (row_ok[rr, 2 * p] or row_ok[rr, 2 * p + 1]) else neg
                o_ref[rr * GRID_W:(rr + 1) * GRID_W, p * LANES:(p + 1) * LANES] = blk

    return pl.pallas_call(
        build,
        grid=(N_HEADS,),
        in_specs=[pl.BlockSpec((None, 2 * NA_ROWS - 1, GRID_W, LANES), lambda h: (h, 0, 0, 0))],
        out_specs=pl.BlockSpec((None, DEC_SEQ, DEC_SEQ), lambda h: (h, 0, 0)),
        out_shape=jax.ShapeDtypeStruct((N_HEADS, DEC_SEQ, DEC_SEQ), F32),
        compiler_params=_cparams(("arbitrary",), 32),
        name="nbr_bias",
    )(t)


def _even_w_in_layout(w):
    cq_end = 3 * N_HEADS * HEAD_DIM + B_Q_RANK
    kv_end = cq_end + B_KV_RANK
    tk = 256

    def layout(w_ref, o_ref):
        o_ref[:, :cq_end] = w_ref[:, :cq_end].astype(BF16)
        o_ref[:, cq_end + 2 * LANES:] = w_ref[:, cq_end:kv_end].astype(BF16)
        kr = w_ref[:, kv_end:].astype(BF16)
        r = lax.broadcasted_iota(jnp.int32, (B_ROPE_DIM, 2 * LANES), 0)
        c = lax.broadcasted_iota(jnp.int32, (B_ROPE_DIM, 2 * LANES), 1)
        place = jnp.where(jnp.logical_or(c == r, c == r + 2 * LANES - B_ROPE_DIM), 1.0, 0.0).astype(BF16)
        o_ref[:, cq_end:cq_end + 2 * LANES] = _dot(kr, place).astype(BF16)

    return pl.pallas_call(
        layout,
        grid=(D_MODEL // tk,),
        in_specs=[pl.BlockSpec((tk, w.shape[1]), lambda i: (i, 0))],
        out_specs=pl.BlockSpec((tk, PROJ_W), lambda i: (i, 0)),
        out_shape=jax.ShapeDtypeStruct((D_MODEL, PROJ_W), BF16),
        compiler_params=_cparams(("arbitrary",), 32),
        name="even_w_in_layout",
    )(w)


def _even_layer(x, mt, l, i, rope64, cache_a_k, cache_a_v, cache_b_ckv, cache_b_krope, norm_mix_g, ev_w_in,
                ev_lambda, ev_subln_g, ev_q_norm_g, ev_kv_norm_g, ev_w_uq, ev_w_ukv, ev_w_out):
    w_in = _even_w_in_layout(ev_w_in[i])
    all_g = (0, 1, 2, 3)
    a_scale = A_QK_DIM ** -0.5 * LOG2E
    in_kw = dict(prologue="modulate", g=norm_mix_g[l][None, :], norm_tile=8, norm_g=ev_kv_norm_g[i][None, :],
                 out_dtype=BF16, tile_scale={0: a_scale, 1: a_scale})
    bcq_src = (0, ((6, 0, TN), (7, 0, TN)))
    head_src = lambda t0: tuple((t0 + t, gq, 4 * t + gq) for t in range(2) for gq in range(4))
    proj_p, st_ak, st_av, st_ckv, st_kr, bcq_p = _projection(
        *x["ctx"], N_SEG_P, D_MODEL, 0, D_MODEL, w_in, mt=mt[:N_SEG_P], name="even_in_proj_ctx",
        states=((N_HEADS, head_src(2)), (N_HEADS, head_src(4)), (0, ((8, 0, B_KV_RANK),)), (0, ((7, 256, B_ROPE_DIM),)),
                bcq_src), **in_kw)
    proj_l, bcq_l = _projection(*x["lat"], DEC_BATCH, D_MODEL, 0, D_MODEL, w_in, mt=mt[N_SEG_P:], rope=rope64,
                                sh=A_QK_DIM // 4, rope_groups={0: all_g, 1: all_g, 2: all_g, 3: all_g, 7: (2, 3)},
                                states=(bcq_src,), name="even_in_proj_lat", **in_kw)
    wq = ev_w_uq[i].reshape(B_Q_RANK, N_HEADS, HEAD_DIM + B_ROPE_DIM)
    w_uq = jnp.concatenate([wq[:, :, :HEAD_DIM].reshape(B_Q_RANK, -1), wq[:, :, HEAD_DIM:].reshape(B_Q_RANK, -1)],
                           axis=1).astype(BF16)
    mla_scale = (HEAD_DIM + B_ROPE_DIM) ** -0.5
    q_kw = dict(prologue="rmsnorm", g=ev_q_norm_g[i][None, :], out_dtype=BF16,
                tile_scale={t: mla_scale * LOG2E for t in range(3)})
    bq_p = _projection(bcq_p, 0, N_SEG_P, 1024, 0, B_Q_RANK, w_uq, name="mla_q_up_ctx", **q_kw)
    bq_l = _projection(bcq_l, 0, DEC_BATCH, 1024, 0, B_Q_RANK, w_uq, rope=rope64, sh=B_ROPE_DIM // 4,
                       rope_groups={2: all_g}, name="mla_q_up_lat", **q_kw)
    w_ukv = ev_w_ukv[i].astype(BF16)
    kv_kw = dict(prologue="cast", out_dtype=BF16)
    kv_p = _projection(proj_p, 0, N_SEG_P, 512, 8, B_KV_RANK, w_ukv, name="mla_kv_up_ctx", **kv_kw)
    kv_l = _projection(proj_l, 0, DEC_BATCH, 512, 8, B_KV_RANK, w_ukv, name="mla_kv_up_lat", **kv_kw)
    kv_ctx = _projection(cache_b_ckv[:, i].reshape(DEC_BATCH * PAST_LEN, B_KV_RANK), 0, 1, 512, 0, B_KV_RANK, w_ukv,
                         name="mla_kv_up_cache", **kv_kw)
    kv_ctx = kv_ctx.reshape(DEC_BATCH, PAST_LEN, 2 * N_HEADS * HEAD_DIM)
    krc = cache_b_krope[:, i]
    zc = jnp.zeros_like(krc)
    kr_ctx = jnp.concatenate([krc, zc, zc, krc], axis=-1)
    ak_ctx = cache_a_k[:, i].reshape(DEC_BATCH, PAST_LEN, N_HEADS * HEAD_DIM)
    av_ctx = cache_a_v[:, i].reshape(DEC_BATCH, PAST_LEN, N_HEADS * HEAD_DIM)
    lam_init = 0.8 - 0.6 * math.exp(-0.3 * l)
    o = jnp.zeros((N_TOK, D_MODEL), BF16)
    for latent, proj, b_q, kv in ((False, proj_p, bq_p, kv_p), (True, proj_l, bq_l, kv_l)):
        tag = "lat" if latent else "ctx"
        o = _attention(kind="diff", latent=latent, q_parts=[(proj, lambda h: h)], ko_parts=[(proj, lambda h: 8 + h)],
                       vo=(proj, lambda h: 16 + h), kc_parts=[(ak_ctx, lambda h: h)], vc=(av_ctx, lambda h: h),
                       lam=ev_lambda[i], subln=ev_subln_g[i][None, :], o_arr=o, o_blk=lambda h: h,
                       lam_init=lam_init, name="diff_attn_" + tag)
        o = _attention(kind="mla", latent=latent, q_parts=[(b_q, lambda h: h), (b_q, lambda h: 8 + h // 2)],
                       ko_parts=[(kv, lambda h: 2 * h), (proj, lambda h: 30 + h % 2)], vo=(kv, lambda h: 2 * h + 1),
                       kc_parts=[(kv_ctx, lambda h: 2 * h), (kr_ctx, lambda h: h % 2)], vc=(kv_ctx, lambda h: 2 * h + 1),
                       o_arr=o, o_blk=lambda h: 8 + h, name="mla_attn_" + tag)
    y = _out_projection(o, ev_w_out[i].astype(BF16), x["parts"], mt)
    states = (st_ak.reshape(BATCH, SEQ, N_HEADS, HEAD_DIM), st_av.reshape(BATCH, SEQ, N_HEADS, HEAD_DIM),
              st_ckv.reshape(BATCH, SEQ, B_KV_RANK), st_kr.reshape(BATCH, SEQ, B_ROPE_DIM))
    return y, states


def _odd_layer(x, mt, l, i, rope128, cache_c_k, cache_c_v, cache_d_k, cache_d_v, norm_mix_g, od_w_in, od_rpb,
               od_sink, od_w_out):
    w_in = od_w_in[i].astype(BF16)
    all_g = (0, 1, 2, 3)
    q_scale = HEAD_DIM ** -0.5 * LOG2E
    in_kw = dict(prologue="modulate", g=norm_mix_g[l][None, :], out_dtype=BF16,
                 tile_scale={t: q_scale for t in (0, 1, 6, 7)})
    head_src = lambda t0: tuple((t0 + t, gq, 4 * t + gq) for t in range(2) for gq in range(4))
    proj_p, st_ck, st_cv, st_dk, st_dv = _projection(
        *x["ctx"], N_SEG_P, D_MODEL, 0, D_MODEL, w_in, mt=mt[:N_SEG_P], name="odd_in_proj_ctx",
        states=((N_HEADS, head_src(2)), (N_HEADS, head_src(4)), (D_KV_HEADS, ((8, 0, 0), (8, 1, 1))),
                (D_KV_HEADS, ((8, 2, 0), (8, 3, 1)))), **in_kw)
    proj_l = _projection(*x["lat"], DEC_BATCH, D_MODEL, 0, D_MODEL, w_in, mt=mt[N_SEG_P:], rope=rope128,
                         sh=HEAD_DIM // 4, rope_groups={6: all_g, 7: all_g, 8: (0, 1)}, name="odd_in_proj_lat", **in_kw)
    ck_ctx = cache_c_k[:, i].reshape(DEC_BATCH, PAST_LEN, N_HEADS * HEAD_DIM)
    cv_ctx = cache_c_v[:, i].reshape(DEC_BATCH, PAST_LEN, N_HEADS * HEAD_DIM)
    dk_ctx = cache_d_k[:, i].reshape(DEC_BATCH, PAST_LEN, D_KV_HEADS * HEAD_DIM)
    dv_ctx = cache_d_v[:, i].reshape(DEC_BATCH, PAST_LEN, D_KV_HEADS * HEAD_DIM)
    bias = _neighbourhood_bias(od_rpb[i])
    sink = od_sink[i].astype(F32)
    o = jnp.zeros((N_TOK, D_MODEL), BF16)
    for latent, proj in ((False, proj_p), (True, proj_l)):
        tag = "lat" if latent else "ctx"
        o = _attention(kind="plain", latent=latent, q_parts=[(proj, lambda h: h)], ko_parts=[(proj, lambda h: 8 + h)],
                       vo=(proj, lambda h: 16 + h), kc_parts=[(ck_ctx, lambda h: h)], vc=(cv_ctx, lambda h: h),
                       bias=bias if latent else None, o_arr=o, o_blk=lambda h: h,
                       name="nbr_attn_" + tag)
        o = _attention(kind="plain", latent=latent, q_parts=[(proj, lambda h: 24 + h)],
                       ko_parts=[(proj, lambda h: 32 + h // D_GROUP)], vo=(proj, lambda h: 34 + h // D_GROUP),
                       kc_parts=[(dk_ctx, lambda h: h // D_GROUP)], vc=(dv_ctx, lambda h: h // D_GROUP), sink=sink,
                       o_arr=o, o_blk=lambda h: 8 + h, has_band=latent, tq_lat=256,
                       name="win_attn_" + tag)
    y = _out_projection(o, od_w_out[i].astype(BF16), x["parts"], mt)
    states = (st_ck.reshape(BATCH, SEQ, N_HEADS, HEAD_DIM), st_cv.reshape(BATCH, SEQ, N_HEADS, HEAD_DIM),
              st_dk.reshape(BATCH, SEQ, D_KV_HEADS, HEAD_DIM), st_dv.reshape(BATCH, SEQ, D_KV_HEADS, HEAD_DIM))
    return y, states


def _moe_layer(x, mt, g, router_w, router_b, layer, w_gate, w_up, w_down, final_g, final):
    h2, eidx, gates = _router(x, mt, g, router_w, router_b)
    src_tok, te, n_active, pos = _route_plan(eidx)
    ys = _moe_experts(h2, src_tok, te, n_active, layer, w_gate, w_up, w_down)
    ysel = ys.at[pos].get(mode="promise_in_bounds").reshape(2, N_TOK, D_MODEL // 2)
    out = _combine(x, ysel, gates.T, mt, final_g, final)
    return out if final else out[0]


def kernel(x_prompt, x_sample, cache_a_k, cache_a_v, cache_b_ckv, cache_b_krope, cache_c_k, cache_c_v, cache_d_k, cache_d_v, c, c_ctx, w_ada, b_ada, norm_mix_g, norm_ffn_g, ev_w_in, ev_lambda, ev_subln_g, ev_q_norm_g, ev_kv_norm_g, ev_w_uq, ev_w_ukv, ev_w_out, od_w_in, od_rpb, od_sink, od_w_out, router_w, router_b, moe_w_gate, moe_w_up, moe_w_down, final_g):
    xp = x_prompt.reshape(N_PROMPT, D_MODEL)
    xl = x_sample.reshape(DEC_BATCH * DEC_SEQ, D_MODEL)
    x = dict(ctx=(xp, 0), lat=(xl, 0), parts=[xp, xl])
    cond8 = jnp.concatenate([c_ctx[None, :], c, jnp.zeros((3, D_MODEL), F32)], axis=0)
    mod = _adaln(cond8, w_ada, b_ada)
    seg_row = np.array([0] * N_SEG_P + [1 + b for b in range(DEC_BATCH)])
    mt_all = mod[:, seg_row].reshape(DEPTH, N_SEG, 6, D_MODEL)
    rope64 = _rope_tables(A_QK_DIM)
    rope128 = _rope_tables(HEAD_DIM)
    even_states, odd_states = [], []
    for l in range(DEPTH):
        i = l // 2
        mt = mt_all[l]
        if l % 2 == 0:
            x, st = _even_layer(x, mt, l, i, rope64, cache_a_k, cache_a_v, cache_b_ckv, cache_b_krope, norm_mix_g,
                                ev_w_in, ev_lambda, ev_subln_g, ev_q_norm_g, ev_kv_norm_g, ev_w_uq, ev_w_ukv, ev_w_out)
            even_states.append(st)
        else:
            x, st = _odd_layer(x, mt, l, i, rope128, cache_c_k, cache_c_v, cache_d_k, cache_d_v, norm_mix_g,
                               od_w_in, od_rpb, od_sink, od_w_out)
            odd_states.append(st)
        x = _moe_layer(x, mt, norm_ffn_g[l][None, :], router_w, router_b, l, moe_w_gate, moe_w_up, moe_w_down,
                       final_g[None, :], final=(l == DEPTH - 1))
        if l < DEPTH - 1:
            x = dict(ctx=(x, 0), lat=(x, N_SEG_P), parts=[x])
    y_prompt = x[0].reshape(BATCH, SEQ, D_MODEL)
    y_sample = x[1].reshape(DEC_BATCH, DEC_SEQ, D_MODEL)
    new_even = tuple(jnp.stack([st[k] for st in even_states], axis=1) for k in range(4))
    new_odd = tuple(jnp.stack([st[k] for st in odd_states], axis=1) for k in range(4))
    return (y_prompt, y_sample) + new_even + new_odd
```

```python
import functools
import math

import numpy as np
import jax
import jax.numpy as jnp
from jax import lax
from jax.experimental import pallas as pl
from jax.experimental.pallas import tpu as pltpu

D_MODEL = 2048
BATCH = 32
SEQ = 256
DEPTH = 2
DEC_BATCH = 4
DEC_SEQ = 1024
PAST_LEN = 256
GRID_W = 64
HEAD_DIM = 128
N_HEADS = 8
A_QK_DIM = 64
B_Q_RANK = 768
B_KV_RANK = 512
B_ROPE_DIM = 64
NA_ROWS = 8
NA_COLS = 16
D_KV_HEADS = 2
D_GROUP = 4
D_WINDOW = 128
N_EXPERTS = 16
N_GROUPS = 4
D_EXPERT = 1024
ROPE_THETA = 10000.0
EPS = 1e-6
NEG = -1e30
LOG2E = 1.4426950408889634

SEG = 1024
N_PROMPT = BATCH * SEQ
N_TOK = N_PROMPT + DEC_BATCH * DEC_SEQ
N_SEG = N_TOK // SEG
N_SEG_P = N_PROMPT // SEG
PROJ_W = 4608
TN = 512
LANES = 128
MOE_TM = 256
MOE_ROWS = 2 * N_TOK + N_EXPERTS * MOE_TM
MOE_TILES = MOE_ROWS // MOE_TM

F32 = jnp.float32
BF16 = jnp.bfloat16


def _cparams(sem, vmem_mb):
    return pltpu.CompilerParams(dimension_semantics=sem, vmem_limit_bytes=vmem_mb * 1024 * 1024)


def _dot(a, b):
    return jnp.dot(a, b, preferred_element_type=F32)


def _dot_nt(a, b):
    return lax.dot_general(a, b, (((1,), (1,)), ((), ())), preferred_element_type=F32)


def _sigmoid(x):
    return 1.0 / (1.0 + jnp.exp(-x))


def _pack_bf16_pair(x):
    c = x.shape[1] // 2
    bits = pltpu.bitcast(x.astype(BF16).astype(F32), jnp.int32)
    return bits[:, :c] | lax.shift_right_logical(bits[:, c:], 16)


def _unpack_bf16_pair(w):
    hi = pltpu.bitcast(w & jnp.int32(-65536), F32)
    lo = pltpu.bitcast(lax.shift_left(w, 16), F32)
    return hi, lo


def _adaln_kernel(c_ref, w_ref, b_ref, o_ref):
    c = c_ref[...]
    a = (c * _sigmoid(c)).astype(BF16)
    o_ref[...] = _dot(a, w_ref[...].astype(BF16)) + b_ref[...]


def _adaln(cond8, w_ada, b_ada):
    tn = 1024
    n = 6 * D_MODEL
    return pl.pallas_call(
        _adaln_kernel,
        grid=(DEPTH, n // tn),
        in_specs=[
            pl.BlockSpec((8, D_MODEL), lambda l, j: (0, 0)),
            pl.BlockSpec((None, D_MODEL, tn), lambda l, j: (l, 0, j)),
            pl.BlockSpec((None, 1, tn), lambda l, j: (l, 0, j)),
        ],
        out_specs=pl.BlockSpec((None, 8, tn), lambda l, j: (l, 0, j)),
        out_shape=jax.ShapeDtypeStruct((DEPTH, 8, n), F32),
        compiler_params=_cparams(("arbitrary", "arbitrary"), 40),
        name="adaln",
    )(cond8, w_ada, b_ada.reshape(DEPTH, 1, n))


def _rope(a, cos, sp, sm, sh):
    return a * cos + pltpu.roll(a, sh, 1) * sp + pltpu.roll(a, LANES - sh, 1) * sm


def _proj_kernel(*refs, prologue, k_in, sh, n_tiles, rope_groups, norm_tile, states, tile_scale):
    x_ref, mt_ref, g_ref, w_ref, cos_ref, sp_ref, sm_ref, ng_ref, o_ref = refs[:9]
    st_refs = refs[9:-1]
    xn_ref = refs[-1]
    j = pl.program_id(1)

    @pl.when(j == 0)
    def _():
        x = x_ref[:, :k_in]
        if prologue == "cast":
            xn_ref[...] = x.astype(BF16)
        else:
            r = lax.rsqrt(jnp.mean(x * x, axis=-1, keepdims=True) + EPS)
            if prologue == "modulate":
                y = x * r * (g_ref[...] * (1.0 + mt_ref[0, 1:2, :])) + mt_ref[0, 0:1, :]
            else:
                y = x * r * g_ref[...]
            xn_ref[...] = y.astype(BF16)

    acc = _dot(xn_ref[...], w_ref[...])
    n_grp = acc.shape[1] // LANES

    def treatment(t):
        acts = []
        for k, (n_heads, src) in enumerate(states):
            dest = 0
            for tt, a, b in src:
                if tt == t:
                    acts.append((k, a, b, dest))
                dest += 0 if n_heads else b
        return (tuple(rope_groups.get(t, ())), t == norm_tile, tuple(acts), float(tile_scale.get(t, 1.0)))

    branches = {}
    for t in range(n_tiles):
        branches.setdefault(treatment(t), []).append(t)

    o_ref[...] = acc.astype(o_ref.dtype)
    reread = o_ref.dtype == F32

    for (rg, is_norm, acts, sc), tiles in branches.items():
        if not rg and not is_norm and not acts and sc == 1.0:
            continue
        cond = j == tiles[0]
        for t in tiles[1:]:
            cond = jnp.logical_or(cond, j == t)

        @pl.when(cond)
        def _(rg=rg, is_norm=is_norm, acts=acts, sc=sc):
            vals = [(o_ref if reread else acc)[:, g * LANES:(g + 1) * LANES] for g in range(n_grp)]
            if sc != 1.0:
                vals = [v * sc for v in vals]
            if is_norm:
                ssq = None
                for v in vals:
                    s = jnp.sum(v * v, axis=-1, keepdims=True)
                    ssq = s if ssq is None else ssq + s
                scale = lax.rsqrt(ssq * (1.0 / (n_grp * LANES)) + EPS)
                ng = ng_ref[...]
                vals = [v * scale * ng[:, g * LANES:(g + 1) * LANES] for g, v in enumerate(vals)]
            elif rg:
                cos, sp, sm = cos_ref[...], sp_ref[...], sm_ref[...]
                vals = [_rope(v, cos, sp, sm, sh) if g in rg else v for g, v in enumerate(vals)]
            if is_norm or rg or sc != 1.0:
                for g, v in enumerate(vals):
                    if is_norm or sc != 1.0 or g in rg:
                        o_ref[:, g * LANES:(g + 1) * LANES] = v.astype(o_ref.dtype)
            for k, a, b, dest in acts:
                n_heads = states[k][0]
                if n_heads:
                    st_refs[k][pl.ds(b, SEG, stride=n_heads), :] = vals[a]
                elif b % LANES == 0:
                    for q in range(b // LANES):
                        st_refs[k][:, dest + q * LANES:dest + (q + 1) * LANES] = vals[a // LANES + q]
                else:
                    st_refs[k][...] = vals[a // LANES][:, a % LANES:a % LANES + b]


def _projection(x, row_off, n_seg, x_blk_w, x_blk_idx, k_in, w, *, prologue, mt=None, g=None, rope=None, sh=16,
                rope_groups=None, norm_tile=None, norm_g=None, states=(), out_dtype=F32, tile_scale=None, name):
    n = w.shape[1]
    m = n_seg * SEG
    assert n % TN == 0 and w.shape[0] == k_in
    if mt is None:
        mt = jnp.zeros((n_seg, 6, LANES), F32)
    if g is None:
        g = jnp.ones((1, k_in), F32)
    if rope is None:
        rope = tuple(jnp.zeros((8, LANES), F32) for _ in range(3))
    if norm_g is None:
        norm_g = jnp.ones((1, TN), F32)
    kern = functools.partial(_proj_kernel, prologue=prologue, k_in=k_in, sh=sh, n_tiles=n // TN,
                             rope_groups=dict(rope_groups or {}), norm_tile=norm_tile, states=tuple(states),
                             tile_scale=dict(tile_scale or {}))
    full2 = lambda i, j: (0, 0)
    out_shape = [jax.ShapeDtypeStruct((m, n), out_dtype)]
    out_specs = [pl.BlockSpec((SEG, TN), lambda i, j: (i, j))]
    for n_heads, src in states:
        if n_heads:
            out_shape.append(jax.ShapeDtypeStruct((m * n_heads, LANES), F32))
            out_specs.append(pl.BlockSpec((SEG * n_heads, LANES), lambda i, j: (i, 0)))
        else:
            width = sum(b for _, _, b in src)
            out_shape.append(jax.ShapeDtypeStruct((m, width), F32))
            out_specs.append(pl.BlockSpec((SEG, width), lambda i, j: (i, 0)))
    x_kw = dict(pipeline_mode=pl.Buffered(1)) if states else {}
    res = pl.pallas_call(
        kern,
        grid=(n_seg, n // TN),
        in_specs=[
            pl.BlockSpec((SEG, x_blk_w), lambda i, j: (i + row_off, x_blk_idx), **x_kw),
            pl.BlockSpec((1, 6, mt.shape[2]), lambda i, j: (i, 0, 0)),
            pl.BlockSpec((1, k_in), full2),
            pl.BlockSpec((k_in, TN), lambda i, j: (0, j)),
            pl.BlockSpec(rope[0].shape, full2),
            pl.BlockSpec(rope[1].shape, full2),
            pl.BlockSpec(rope[2].shape, full2),
            pl.BlockSpec((1, TN), full2),
        ],
        out_specs=out_specs,
        out_shape=out_shape,
        scratch_shapes=[pltpu.VMEM((SEG, k_in), BF16)],
        compiler_params=_cparams(("arbitrary", "arbitrary"), 56 if states else 48),
        name=name,
    )(x, mt, g, w, *rope, norm_g)
    return res if states else res[0]


def _attn_kernel(*refs, kind, n_parts, has_ctx, has_bias, has_band, has_sink, nb, s_len, tq, lam_init, own_win):
    it = iter(refs)
    q_refs = [next(it) for _ in range(n_parts)]
    ko_refs = [next(it) for _ in range(n_parts)]
    vo_ref = next(it)
    kc_refs, vc_ref = [], None
    if has_ctx:
        kc_refs = [next(it) for _ in range(n_parts)]
        vc_ref = next(it)
    bias_ref = next(it) if has_bias else None
    lam_ref = sg_ref = None
    if kind == "diff":
        lam_ref = next(it)
        sg_ref = next(it)
    sink_ref = next(it) if has_sink else None
    o_ref = next(it)

    sink2 = sink_ref[pl.program_id(1)] * LOG2E if has_sink else None
    lane = lax.broadcasted_iota(jnp.int32, (1, LANES), 1)

    def softmax_pv(s_list, v_list):
        m = None
        for s in s_list:
            mm = jnp.max(s, axis=-1, keepdims=True)
            m = mm if m is None else jnp.maximum(m, mm)
        if sink2 is not None:
            m = jnp.maximum(m, sink2)
        l = jnp.exp2(sink2 - m) if sink2 is not None else None
        o = None
        for s, v in zip(s_list, v_list):
            e = jnp.exp2(s - m)
            ss = jnp.sum(e, axis=-1, keepdims=True)
            l = ss if l is None else l + ss
            pv = _dot(e.astype(BF16), v)
            o = pv if o is None else o + pv
        return o, l

    for bi in range(nb):
        qrows = pl.ds(bi * tq, tq)
        q0 = pl.program_id(2) * tq
        if own_win is None:
            krows, k0 = pl.ds(bi * s_len, s_len), 0
            n_own = s_len
        else:
            k0 = pl.multiple_of(jnp.clip(q0 - D_WINDOW, 0, s_len - own_win), LANES)
            krows = pl.ds(k0, own_win)
            n_own = own_win
        k_srcs, v_srcs, is_own = [], [], []
        if has_ctx:
            k_srcs.append([r[...].astype(BF16) for r in kc_refs])
            v_srcs.append(vc_ref[...].astype(BF16))
            is_own.append(False)
        k_srcs.append([r[krows, :].astype(BF16) for r in ko_refs])
        v_srcs.append(vo_ref[krows, :].astype(BF16))
        is_own.append(True)

        def mask_own(s):
            if has_bias:
                s = s + bias_ref[...]
            if has_band:
                qpos = q0 + lax.broadcasted_iota(jnp.int32, (tq, 1), 0)
                kpos = k0 + lax.broadcasted_iota(jnp.int32, (1, n_own), 1)
                s = jnp.where(jnp.abs(kpos - qpos) <= D_WINDOW, s, NEG)
            return s

        if kind == "diff":
            q = q_refs[0][qrows, :]
            zero = jnp.zeros_like(q)
            q1 = jnp.where(lane < A_QK_DIM, q, zero)
            q2 = jnp.where(lane < A_QK_DIM, zero, q)
            o1, l1 = softmax_pv([_dot_nt(q1, ks[0]) for ks in k_srcs], v_srcs)
            o2, l2 = softmax_pv([_dot_nt(q2, ks[0]) for ks in k_srcs], v_srcs)
            lv = lam_ref[...]
            lam = (jnp.exp(jnp.sum(lv[0:1] * lv[1:2], axis=-1, keepdims=True))
                   - jnp.exp(jnp.sum(lv[2:3] * lv[3:4], axis=-1, keepdims=True)) + lam_init)
            o = o1 * (1.0 / l1) - o2 * (lam / l2)
            o = o * lax.rsqrt(jnp.mean(o * o, axis=-1, keepdims=True) + EPS) * sg_ref[...] * (1.0 - lam_init)
        else:
            qs = [r[qrows, :] for r in q_refs]
            s_list = []
            for ks, own in zip(k_srcs, is_own):
                s = None
                for qp, kp in zip(qs, ks):
                    d = _dot_nt(qp, kp)
                    s = d if s is None else s + d
                s_list.append(mask_own(s) if own else s)
            o, l = softmax_pv(s_list, v_srcs)
            o = o * (1.0 / l)
        o_ref[qrows, :] = o.astype(o_ref.dtype)


def _attention(*, kind, latent, q_parts, ko_parts, vo, kc_parts=(), vc=None, bias=None, lam=None, subln=None,
               sink=None, o_arr, o_blk, lam_init=0.0, has_band=False, tq_lat=512, name):
    n_parts = len(q_parts)
    has_ctx = latent
    if latent:
        tq, s_len, nb = tq_lat, DEC_SEQ, 1
        grid = (DEC_BATCH, N_HEADS, DEC_SEQ // tq)
        qpb = DEC_SEQ // tq
        q_row = lambda b, h, t: b * qpb + t
        o_row = lambda b, h, t: N_PROMPT // tq + b * qpb + t
        k_row = lambda b, h, t: b
        qblk = tq
    else:
        tq, s_len, nb = SEQ, SEQ, SEG // SEQ
        grid = (N_SEG_P, N_HEADS, 1)
        q_row = o_row = k_row = lambda b, h, t: b
        qblk = SEG
    args, specs = [], []

    def add(arr, shape, imap, **kw):
        args.append(arr)
        specs.append(pl.BlockSpec(shape, imap, **kw))

    for arr, f in q_parts:
        add(arr, (qblk, LANES), lambda b, h, t, f=f: (q_row(b, h, t), f(h)))
    for arr, f in list(ko_parts) + [vo]:
        add(arr, (SEG, LANES), lambda b, h, t, f=f: (k_row(b, h, t), f(h)))
    if has_ctx:
        for arr, f in list(kc_parts) + [vc]:
            add(arr, (None, PAST_LEN, LANES), lambda b, h, t, f=f: (b, 0, f(h)))
    if bias is not None:
        add(bias, (None, tq, DEC_SEQ), lambda b, h, t: (h, t, 0))
    if kind == "diff":
        add(lam, lam.shape, lambda b, h, t: (0, 0))
        add(subln, subln.shape, lambda b, h, t: (0, 0))
    if sink is not None:
        args.append(sink)
        specs.append(pl.BlockSpec(memory_space=pltpu.SMEM))
    n_in = len(args)
    args.append(o_arr)
    specs.append(pl.BlockSpec(memory_space=pl.ANY))
    kern = functools.partial(_attn_kernel_aliased, kind=kind, n_parts=n_parts, has_ctx=has_ctx,
                             has_bias=bias is not None, has_band=has_band, has_sink=sink is not None,
                             nb=nb, s_len=s_len, tq=tq, lam_init=lam_init,
                             own_win=(tq + 2 * D_WINDOW) if has_band else None)
    return pl.pallas_call(
        kern,
        grid=grid,
        in_specs=specs,
        out_specs=pl.BlockSpec((qblk, LANES), lambda b, h, t: (o_row(b, h, t), o_blk(h))),
        out_shape=jax.ShapeDtypeStruct(o_arr.shape, o_arr.dtype),
        input_output_aliases={n_in: 0},
        compiler_params=_cparams(("arbitrary", "arbitrary", "arbitrary"), 48),
        name=name,
    )(*args)


def _attn_kernel_aliased(*refs, **kw):
    _attn_kernel(*refs[:-2], refs[-1], **kw)


def _outproj_kernel(o_ref, w_ref, mt_ref, *rest, seg_ranges):
    x_refs, y_ref = rest[:-1], rest[-1]
    upd = mt_ref[0, 2:3, :] * _dot(o_ref[...], w_ref[...])
    if len(x_refs) == 1:
        y_ref[...] = x_refs[0][...] + upd
    else:
        i = pl.program_id(0)
        for x_ref, (lo, hi) in zip(x_refs, seg_ranges):
            @pl.when(jnp.logical_and(i >= lo, i < hi))
            def _(x_ref=x_ref):
                y_ref[...] = x_ref[...] + upd


def _out_projection(o, w, x_parts, mt):
    seg_ranges, lo = [], 0
    for xp in x_parts:
        seg_ranges.append((lo, lo + xp.shape[0] // SEG))
        lo = seg_ranges[-1][1]
    assert lo == N_SEG
    x_specs = [pl.BlockSpec((SEG, TN), lambda i, j, lo=lo, hi=hi: (
        jnp.clip(i - lo, 0, hi - lo - 1), jnp.where(jnp.logical_and(i >= lo, i < hi), j, 0)))
               for lo, hi in seg_ranges]
    return pl.pallas_call(
        functools.partial(_outproj_kernel, seg_ranges=tuple(seg_ranges)),
        grid=(N_SEG, D_MODEL // TN),
        in_specs=[
            pl.BlockSpec((SEG, D_MODEL), lambda i, j: (i, 0)),
            pl.BlockSpec((D_MODEL, TN), lambda i, j: (0, j)),
            pl.BlockSpec((1, 6, TN), lambda i, j: (i, 0, j)),
        ] + x_specs,
        out_specs=pl.BlockSpec((SEG, TN), lambda i, j: (i, j)),
        out_shape=jax.ShapeDtypeStruct((N_TOK, D_MODEL), F32),
        compiler_params=_cparams(("arbitrary", "arbitrary"), 40),
        name="out_proj",
    )(o, w, mt, *x_parts)


def _router_kernel(x_ref, mt_ref, g_ref, w_ref, rb_ref, h_ref, e_ref, gt_ref):
    x = x_ref[...]
    r = lax.rsqrt(jnp.mean(x * x, axis=-1, keepdims=True) + EPS)
    h = x * r * (g_ref[...] * (1.0 + mt_ref[0, 4:5, :])) + mt_ref[0, 3:4, :]
    hh = h.astype(BF16)
    h_ref[...] = _pack_bf16_pair(hh)
    logits = _dot_nt(w_ref[...], hh)
    scores = _sigmoid(logits)
    sel = scores + rb_ref[...]
    per = N_EXPERTS // N_GROUPS
    sc = [scores[e:e + 1, :] for e in range(N_EXPERTS)]
    sl = [sel[e:e + 1, :] for e in range(N_EXPERTS)]
    best_g, best_v = None, None
    for gi in range(N_GROUPS):
        a, b, c, d = sl[gi * per:(gi + 1) * per]
        hi1, lo1, hi2, lo2 = jnp.maximum(a, b), jnp.minimum(a, b), jnp.maximum(c, d), jnp.minimum(c, d)
        gs = jnp.maximum(hi1, hi2) + jnp.maximum(jnp.minimum(hi1, hi2), jnp.maximum(lo1, lo2))
        if gi == 0:
            best_g, best_v = jnp.zeros_like(gs, dtype=jnp.int32), gs
        else:
            better = gs > best_v
            best_g = jnp.where(better, gi, best_g)
            best_v = jnp.where(better, gs, best_v)
    masked = [jnp.where(best_g == (e // per), sl[e], NEG) for e in range(N_EXPERTS)]
    i1, v1 = jnp.zeros_like(best_g), masked[0]
    for e in range(1, N_EXPERTS):
        better = masked[e] > v1
        i1 = jnp.where(better, e, i1)
        v1 = jnp.where(better, masked[e], v1)
    i2, v2 = None, None
    for e in range(N_EXPERTS):
        cand = jnp.where(i1 == e, -2e30, masked[e])
        if e == 0:
            i2, v2 = jnp.zeros_like(best_g), cand
        else:
            better = cand > v2
            i2 = jnp.where(better, e, i2)
            v2 = jnp.where(better, cand, v2)
    g1 = jnp.zeros_like(v1)
    g2 = jnp.zeros_like(v1)
    for e in range(N_EXPERTS):
        g1 = jnp.where(i1 == e, sc[e], g1)
        g2 = jnp.where(i2 == e, sc[e], g2)
    tot = g1 + g2
    rows = i1.shape[1]
    e_ref[...] = jnp.concatenate([i1, i2, jnp.zeros((6, rows), jnp.int32)], axis=0)
    gt_ref[...] = jnp.concatenate([g1 / tot, g2 / tot, jnp.zeros((6, rows), F32)], axis=0)


def _router(x, mt, g, router_w, router_b):
    wt = router_w.T.astype(BF16)
    return pl.pallas_call(
        _router_kernel,
        grid=(N_SEG,),
        in_specs=[
            pl.BlockSpec((SEG, D_MODEL), lambda i: (i, 0)),
            pl.BlockSpec((1, 6, D_MODEL), lambda i: (i, 0, 0)),
            pl.BlockSpec((1, D_MODEL), lambda i: (0, 0)),
            pl.BlockSpec((N_EXPERTS, D_MODEL), lambda i: (0, 0)),
            pl.BlockSpec((N_EXPERTS, 1), lambda i: (0, 0)),
        ],
        out_specs=[
            pl.BlockSpec((SEG, D_MODEL // 2), lambda i: (i, 0)),
            pl.BlockSpec((8, SEG), lambda i: (0, i)),
            pl.BlockSpec((8, SEG), lambda i: (0, i)),
        ],
        out_shape=[
            jax.ShapeDtypeStruct((N_TOK, D_MODEL // 2), jnp.int32),
            jax.ShapeDtypeStruct((8, N_TOK), jnp.int32),
            jax.ShapeDtypeStruct((8, N_TOK), F32),
        ],
        compiler_params=_cparams(("arbitrary",), 48),
        name="ffn_norm_router",
    )(x, mt, g, wt, router_b.reshape(N_EXPERTS, 1))


def _expert_weights(te_ref, first_ref, slot_ref, nxt_ref, i, t, layer, w_hbm, stage_ref, sem_ref, w_bf_refs):
    def copies(e, s):
        return [pltpu.make_async_copy(w.at[layer, e], stage_ref.at[s, k], sem_ref.at[s, k])
                for k, w in enumerate(w_hbm)]

    s = slot_ref[t]

    @pl.when(i == 0)
    def _():
        for c in copies(te_ref[t], s):
            c.start()

    @pl.when(first_ref[t] == 1)
    def _():
        for c in copies(te_ref[t], s):
            c.wait()

        @pl.when(nxt_ref[t] >= 0)
        def _():
            for c in copies(nxt_ref[t], 1 - s):
                c.start()

        for k, w_bf in enumerate(w_bf_refs):
            w_bf[...] = stage_ref[s, k].astype(BF16)


def _moe_up_kernel(te_ref, na_ref, first_ref, slot_ref, nxt_ref, xs_ref, wg_hbm, wu_hbm, hid_ref,
                   stage_ref, wgb_ref, wub_ref, sem_ref, *, tile0, layer):
    i = pl.program_id(0)
    t = i + tile0
    _expert_weights(te_ref, first_ref, slot_ref, nxt_ref, i, t, layer, (wg_hbm, wu_hbm), stage_ref, sem_ref,
                    (wgb_ref, wub_ref))

    @pl.when(t < na_ref[0])
    def _():
        x_hi, x_lo = _unpack_bf16_pair(xs_ref[...])
        x_hi, x_lo = x_hi.astype(BF16), x_lo.astype(BF16)
        half = D_MODEL // 2
        g = _dot(x_hi, wgb_ref[:half, :]) + _dot(x_lo, wgb_ref[half:, :])
        u = _dot(x_hi, wub_ref[:half, :]) + _dot(x_lo, wub_ref[half:, :])
        hid_ref[...] = (g * _sigmoid(g) * u).astype(BF16)

    @pl.when(t >= na_ref[0])
    def _():
        hid_ref[...] = jnp.zeros_like(hid_ref)


def _moe_down_kernel(te_ref, na_ref, first_ref, slot_ref, nxt_ref, *rest, layer, chunk_tiles):
    ys_ref, stage_ref, wdb_ref, sem_ref = rest[-4:]
    wd_hbm = rest[-5]
    hid_refs = rest[:-5]
    t = pl.program_id(0)
    _expert_weights(te_ref, first_ref, slot_ref, nxt_ref, t, t, layer, (wd_hbm,), stage_ref, sem_ref, (wdb_ref,))

    for c, hid_ref in enumerate(hid_refs):
        @pl.when(jnp.logical_and(t < na_ref[0], t // chunk_tiles == c))
        def _(hid_ref=hid_ref):
            ys_ref[...] = _pack_bf16_pair(_dot(hid_ref[...], wdb_ref[...]))

    @pl.when(t >= na_ref[0])
    def _():
        ys_ref[...] = jnp.zeros_like(ys_ref)


MOE_CHUNKS = 2


def _moe_experts(h2, src_tok, tile_expert, n_active, layer, w_gate, w_up, w_down):
    tiles = MOE_TILES // MOE_CHUNKS
    rows = tiles * MOE_TM

    def weight_runs(call_tiles):
        t_ids = jnp.arange(MOE_TILES, dtype=jnp.int32)
        prev = jnp.concatenate([tile_expert[:1], tile_expert[:-1]])
        first = jnp.logical_or(t_ids % call_tiles == 0, tile_expert != prev)
        slot = ((jnp.cumsum(first.astype(jnp.int32)) - 1) % 2).astype(jnp.int32)
        first_at = lax.cummin(jnp.where(first, t_ids, MOE_TILES), reverse=True)
        next_at = jnp.concatenate([first_at[1:], jnp.full((1,), MOE_TILES, jnp.int32)])
        same_call = jnp.logical_and(next_at < MOE_TILES, next_at // call_tiles == t_ids // call_tiles)
        nxt = jnp.where(jnp.logical_and(first, same_call), tile_expert[jnp.minimum(next_at, MOE_TILES - 1)], -1)
        return (tile_expert, n_active, first.astype(jnp.int32), slot, nxt.astype(jnp.int32))

    row_blk = lambda i, *_: (i, 0)
    hbm = pl.BlockSpec(memory_space=pl.ANY)
    plan = weight_runs(tiles)
    hids = []
    for c in range(MOE_CHUNKS):
        xs = h2.at[src_tok[c * rows:(c + 1) * rows]].get(mode="promise_in_bounds")
        hids.append(pl.pallas_call(
            functools.partial(_moe_up_kernel, tile0=c * tiles, layer=layer),
            grid_spec=pltpu.PrefetchScalarGridSpec(
                num_scalar_prefetch=len(plan),
                grid=(tiles,),
                in_specs=[pl.BlockSpec((MOE_TM, D_MODEL // 2), row_blk), hbm, hbm],
                out_specs=pl.BlockSpec((MOE_TM, D_EXPERT), row_blk),
                scratch_shapes=[pltpu.VMEM((2, 2, D_MODEL, D_EXPERT), F32),
                                pltpu.VMEM((D_MODEL, D_EXPERT), BF16), pltpu.VMEM((D_MODEL, D_EXPERT), BF16),
                                pltpu.SemaphoreType.DMA((2, 2))],
            ),
            out_shape=jax.ShapeDtypeStruct((rows, D_EXPERT), BF16),
            compiler_params=_cparams(("arbitrary",), 52),
            name="moe_up",
        )(*plan, xs, w_gate, w_up))
    plan = weight_runs(MOE_TILES)
    hid_specs = [pl.BlockSpec((MOE_TM, D_EXPERT), lambda i, *_, c=c: (jnp.clip(i - c * tiles, 0, tiles - 1), 0))
                 for c in range(MOE_CHUNKS)]
    return pl.pallas_call(
        functools.partial(_moe_down_kernel, layer=layer, chunk_tiles=tiles),
        grid_spec=pltpu.PrefetchScalarGridSpec(
            num_scalar_prefetch=len(plan),
            grid=(MOE_TILES,),
            in_specs=hid_specs + [hbm],
            out_specs=pl.BlockSpec((MOE_TM, D_MODEL // 2), row_blk),
            scratch_shapes=[pltpu.VMEM((2, 1, D_EXPERT, D_MODEL), F32), pltpu.VMEM((D_EXPERT, D_MODEL), BF16),
                            pltpu.SemaphoreType.DMA((2, 1))],
        ),
        out_shape=jax.ShapeDtypeStruct((MOE_ROWS, D_MODEL // 2), jnp.int32),
        compiler_params=_cparams(("arbitrary",), 40),
        name="moe_down",
    )(*plan, *hids, w_down)


PLAN_R = 2 * N_TOK // LANES


def _plan_kernel(e_ref, pos_ref, meta_ref):
    e = e_ref[...]
    r_i = lax.broadcasted_iota(jnp.int32, (LANES, LANES), 0)
    c_i = lax.broadcasted_iota(jnp.int32, (LANES, LANES), 1)
    upper = jnp.where(r_i <= c_i, 1.0, 0.0).astype(BF16)
    r_j = lax.broadcasted_iota(jnp.int32, (PLAN_R, PLAN_R), 0)
    c_j = lax.broadcasted_iota(jnp.int32, (PLAN_R, PLAN_R), 1)
    lower = jnp.where(c_j < r_j, 1.0, 0.0).astype(BF16)
    lane = lax.broadcasted_iota(jnp.int32, (1, LANES), 1)
    pos = jnp.zeros((PLAN_R, LANES), F32)
    cnt_row = jnp.zeros((1, LANES), F32)
    off_row = jnp.zeros((1, LANES), F32)
    end_row = []
    row_off = jnp.zeros((1, 1), F32)
    tiles_done = jnp.zeros((1, 1), F32)
    for ex in range(N_EXPERTS):
        m = jnp.where(e == ex, 1.0, 0.0)
        inc = _dot(m.astype(BF16), upper)
        tot = jnp.broadcast_to(inc[:, LANES - 1:LANES], (PLAN_R, LANES))
        before = _dot(lower, tot.astype(BF16))
        cnt = jnp.sum(inc[:, LANES - 1:LANES], axis=0, keepdims=True)
        pos = pos + m * (row_off + before + inc - 1.0)
        tiles = jnp.floor((cnt + (MOE_TM - 1)) * (1.0 / MOE_TM))
        cnt_row = jnp.where(lane == ex, cnt, cnt_row)
        off_row = jnp.where(lane == ex, row_off, off_row)
        tiles_done = tiles_done + tiles
        end_row.append(tiles_done)
        row_off = row_off + tiles * MOE_TM
    n_active = tiles_done
    tid = jnp.minimum(lane.astype(F32), n_active - 1.0)
    te = jnp.zeros((1, LANES), F32)
    for ex in range(N_EXPERTS):
        te = te + jnp.where(end_row[ex] <= tid, 1.0, 0.0)
    pos_ref[...] = pos.astype(jnp.int32)
    meta = jnp.concatenate([cnt_row, off_row, te, jnp.broadcast_to(n_active, (1, LANES)),
                            jnp.zeros((4, LANES), F32)], axis=0)
    meta_ref[...] = meta.astype(jnp.int32)


def _route_plan(eidx):
    e2 = eidx[:2].reshape(PLAN_R, LANES)
    pos, meta = pl.pallas_call(
        _plan_kernel,
        out_shape=[jax.ShapeDtypeStruct((PLAN_R, LANES), jnp.int32), jax.ShapeDtypeStruct((8, LANES), jnp.int32)],
        compiler_params=pltpu.CompilerParams(vmem_limit_bytes=32 * 1024 * 1024),
        name="route_plan",
    )(e2)
    cnt, row_off = meta[0, :N_EXPERTS], meta[1, :N_EXPERTS]
    te, n_active = meta[2, :MOE_TILES], meta[3, :1]
    start = jnp.cumsum(cnt) - cnt
    order = jnp.argsort(e2.reshape(-1), stable=True).astype(jnp.int32)
    row_e = jnp.repeat(te, MOE_TM)
    rank = jnp.arange(MOE_ROWS, dtype=jnp.int32) - row_off[row_e]
    valid = rank < cnt[row_e]
    src_asg = order[jnp.clip(start[row_e] + rank, 0, 2 * N_TOK - 1)]
    rows = jnp.arange(MOE_ROWS, dtype=jnp.int32)
    src_tok = jnp.where(valid, src_asg % N_TOK, rows % N_TOK)
    return src_tok, te, n_active, pos.reshape(-1)


def _combine_kernel(x_ref, y0_ref, y1_ref, gt_ref, mt_ref, fg_ref, o_ref, *, final):
    gt = gt_ref[...]
    g0, g1 = gt[:, 0:1], gt[:, 1:2]
    half = D_MODEL // 2
    y0 = _unpack_bf16_pair(y0_ref[...])
    y1 = _unpack_bf16_pair(y1_ref[...])
    xs = []
    for c in range(2):
        cols = slice(c * half, (c + 1) * half)
        xs.append(x_ref[:, cols] + mt_ref[0, 5:6, cols] * (y0[c] * g0 + y1[c] * g1))
    if final:
        ssq = jnp.sum(xs[0] * xs[0], axis=-1, keepdims=True) + jnp.sum(xs[1] * xs[1], axis=-1, keepdims=True)
        r = lax.rsqrt(ssq * (1.0 / D_MODEL) + EPS)
        xs = [xc * r * fg_ref[:, c * half:(c + 1) * half] for c, xc in enumerate(xs)]
    for c, xc in enumerate(xs):
        o_ref[:, c * half:(c + 1) * half] = xc


def _combine(x, row0, n_rows, ysel, gates_t, mt, final_g, final):
    tm = 256
    b0 = row0 // tm
    return pl.pallas_call(
        functools.partial(_combine_kernel, final=final),
        grid=(n_rows // tm,),
        in_specs=[
            pl.BlockSpec((tm, D_MODEL), lambda i: (i + b0, 0)),
            pl.BlockSpec((None, tm, D_MODEL // 2), lambda i: (0, i, 0)),
            pl.BlockSpec((None, tm, D_MODEL // 2), lambda i: (1, i, 0)),
            pl.BlockSpec((tm, 8), lambda i: (i + b0, 0)),
            pl.BlockSpec((1, 6, D_MODEL), lambda i: ((i + b0) // (SEG // tm), 0, 0)),
            pl.BlockSpec((1, D_MODEL), lambda i: (0, 0)),
        ],
        out_specs=pl.BlockSpec((tm, D_MODEL), lambda i: (i, 0)),
        out_shape=jax.ShapeDtypeStruct((n_rows, D_MODEL), F32),
        compiler_params=_cparams(("arbitrary",), 40),
        name="moe_combine",
    )(x, ysel, ysel, gates_t, mt, final_g)


def _rope_tables(dim):
    half = dim // 2
    inv = ROPE_THETA ** (-jnp.arange(0, half, 2, dtype=F32) / half)
    t = jnp.arange(DEC_SEQ)
    ang_r = (t // GRID_W).astype(F32)[:, None] * inv[None, :]
    ang_c = (t % GRID_W).astype(F32)[:, None] * inv[None, :]
    ang = jnp.concatenate([ang_r, ang_r, ang_c, ang_c], axis=-1)
    cos, sin = jnp.cos(ang), jnp.sin(ang)
    reps = LANES // dim
    cos = jnp.tile(cos, (1, reps))
    sin = jnp.tile(sin, (1, reps))
    sh = dim // 4
    second = (np.arange(LANES) % (2 * sh)) >= sh
    sp = jnp.where(second[None, :], sin, 0.0)
    sm = jnp.where(second[None, :], 0.0, -sin)
    return cos, sp, sm


def _neighbourhood_bias(rpb):
    rows = DEC_SEQ // GRID_W
    kh = min(NA_ROWS, rows)
    r = np.arange(rows)
    r0 = np.clip(r - kh // 2, 0, rows - kh)
    kr = np.arange(rows)
    row_ok = (kr[None, :] >= r0[:, None]) & (kr[None, :] < r0[:, None] + kh)
    c = np.arange(GRID_W)
    ws = np.clip(c - NA_COLS // 2, 0, GRID_W - NA_COLS)
    kc = np.arange(GRID_W)
    col_ok = (kc[None, :] >= ws[:, None]) & (kc[None, :] < ws[:, None] + NA_COLS)
    dc_idx = np.clip(kc[None, :] - c[:, None], -(NA_COLS - 1), NA_COLS - 1) + NA_COLS - 1
    sel_c = ((np.arange(2 * NA_COLS - 1)[:, None, None] == dc_idx[None]) & col_ok[None]).astype(np.float32)
    t = jnp.einsum("hab,bcx->hacx", rpb.astype(F32), sel_c, precision=lax.Precision.HIGHEST)
    t = jnp.where(col_ok[None, None], t * LOG2E, NEG)
    t = jnp.concatenate([t, t], axis=-1)

    def build(t_ref, o_ref):
        left = lax.broadcasted_iota(jnp.int32, (1, LANES), 1) < GRID_W
        neg = jnp.full((GRID_W, LANES), NEG, F32)
        for rr in range(rows):
            for p in range(rows // 2):
                parts = [t_ref[k - rr + NA_ROWS - 1] if row_ok[rr, k] else neg for k in (2 * p, 2 * p + 1)]
                blk = jnp.where(left, parts[0], parts[1]) if (row_ok[rr, 2 * p] or row_ok[rr, 2 * p + 1]) else neg
                o_ref[rr * GRID_W:(rr + 1) * GRID_W, p * LANES:(p + 1) * LANES] = blk

    return pl.pallas_call(
        build,
        grid=(N_HEADS,),
        in_specs=[pl.BlockSpec((None, 2 * NA_ROWS - 1, GRID_W, LANES), lambda h: (h, 0, 0, 0))],
        out_specs=pl.BlockSpec((None, DEC_SEQ, DEC_SEQ), lambda h: (h, 0, 0)),
        out_shape=jax.ShapeDtypeStruct((N_HEADS, DEC_SEQ, DEC_SEQ), F32),
        compiler_params=_cparams(("arbitrary",), 32),
        name="nbr_bias",
    )(t)


def _even_w_in_layout(w):
    cq_end = 3 * N_HEADS * HEAD_DIM + B_Q_RANK
    kv_end = cq_end + B_KV_RANK
    tk = 256

    def layout(w_ref, o_ref):
        o_ref[:, :cq_end] = w_ref[:, :cq_end].astype(BF16)
        o_ref[:, cq_end + 2 * LANES:] = w_ref[:, cq_end:kv_end].astype(BF16)
        kr = w_ref[:, kv_end:].astype(BF16)
        r = lax.broadcasted_iota(jnp.int32, (B_ROPE_DIM, 2 * LANES), 0)
        c = lax.broadcasted_iota(jnp.int32, (B_ROPE_DIM, 2 * LANES), 1)
        place = jnp.where(jnp.logical_or(c == r, c == r + 2 * LANES - B_ROPE_DIM), 1.0, 0.0).astype(BF16)
        o_ref[:, cq_end:cq_end + 2 * LANES] = _dot(kr, place).astype(BF16)

    return pl.pallas_call(
        layout,
        grid=(D_MODEL // tk,),
        in_specs=[pl.BlockSpec((tk, w.shape[1]), lambda i: (i, 0))],
        out_specs=pl.BlockSpec((tk, PROJ_W), lambda i: (i, 0)),
        out_shape=jax.ShapeDtypeStruct((D_MODEL, PROJ_W), BF16),
        compiler_params=_cparams(("arbitrary",), 32),
        name="even_w_in_layout",
    )(w)


def _even_layer(x, mt, l, i, rope64, cache_a_k, cache_a_v, cache_b_ckv, cache_b_krope, norm_mix_g, ev_w_in,
                ev_lambda, ev_subln_g, ev_q_norm_g, ev_kv_norm_g, ev_w_uq, ev_w_ukv, ev_w_out):
    w_in = _even_w_in_layout(ev_w_in[i])
    all_g = (0, 1, 2, 3)
    a_scale = A_QK_DIM ** -0.5 * LOG2E
    in_kw = dict(prologue="modulate", g=norm_mix_g[l][None, :], norm_tile=8, norm_g=ev_kv_norm_g[i][None, :],
                 out_dtype=BF16, tile_scale={0: a_scale, 1: a_scale})
    bcq_src = (0, ((6, 0, TN), (7, 0, TN)))
    head_src = lambda t0: tuple((t0 + t, gq, 4 * t + gq) for t in range(2) for gq in range(4))
    proj_p, st_ak, st_av, st_ckv, st_kr, bcq_p = _projection(
        *x["ctx"], N_SEG_P, D_MODEL, 0, D_MODEL, w_in, mt=mt[:N_SEG_P], name="even_in_proj_ctx",
        states=((N_HEADS, head_src(2)), (N_HEADS, head_src(4)), (0, ((8, 0, B_KV_RANK),)), (0, ((7, 256, B_ROPE_DIM),)),
                bcq_src), **in_kw)
    proj_l, bcq_l = _projection(*x["lat"], DEC_BATCH, D_MODEL, 0, D_MODEL, w_in, mt=mt[N_SEG_P:], rope=rope64,
                                sh=A_QK_DIM // 4, rope_groups={0: all_g, 1: all_g, 2: all_g, 3: all_g, 7: (2, 3)},
                                states=(bcq_src,), name="even_in_proj_lat", **in_kw)
    wq = ev_w_uq[i].reshape(B_Q_RANK, N_HEADS, HEAD_DIM + B_ROPE_DIM)
    w_uq = jnp.concatenate([wq[:, :, :HEAD_DIM].reshape(B_Q_RANK, -1), wq[:, :, HEAD_DIM:].reshape(B_Q_RANK, -1)],
                           axis=1).astype(BF16)
    mla_scale = (HEAD_DIM + B_ROPE_DIM) ** -0.5
    q_kw = dict(prologue="rmsnorm", g=ev_q_norm_g[i][None, :], out_dtype=BF16,
                tile_scale={t: mla_scale * LOG2E for t in range(3)})
    bq_p = _projection(bcq_p, 0, N_SEG_P, 1024, 0, B_Q_RANK, w_uq, name="mla_q_up_ctx", **q_kw)
    bq_l = _projection(bcq_l, 0, DEC_BATCH, 1024, 0, B_Q_RANK, w_uq, rope=rope64, sh=B_ROPE_DIM // 4,
                       rope_groups={2: all_g}, name="mla_q_up_lat", **q_kw)
    w_ukv = ev_w_ukv[i].astype(BF16)
    kv_kw = dict(prologue="cast", out_dtype=BF16)
    kv_p = _projection(proj_p, 0, N_SEG_P, 512, 8, B_KV_RANK, w_ukv, name="mla_kv_up_ctx", **kv_kw)
    kv_l = _projection(proj_l, 0, DEC_BATCH, 512, 8, B_KV_RANK, w_ukv, name="mla_kv_up_lat", **kv_kw)
    kv_ctx = _projection(cache_b_ckv[:, i].reshape(DEC_BATCH * PAST_LEN, B_KV_RANK), 0, 1, 512, 0, B_KV_RANK, w_ukv,
                         name="mla_kv_up_cache", **kv_kw)
    kv_ctx = kv_ctx.reshape(DEC_BATCH, PAST_LEN, 2 * N_HEADS * HEAD_DIM)
    krc = cache_b_krope[:, i]
    zc = jnp.zeros_like(krc)
    kr_ctx = jnp.concatenate([krc, zc, zc, krc], axis=-1)
    ak_ctx = cache_a_k[:, i].reshape(DEC_BATCH, PAST_LEN, N_HEADS * HEAD_DIM)
    av_ctx = cache_a_v[:, i].reshape(DEC_BATCH, PAST_LEN, N_HEADS * HEAD_DIM)
    lam_init = 0.8 - 0.6 * math.exp(-0.3 * l)
    o = jnp.zeros((N_TOK, D_MODEL), BF16)
    for latent, proj, b_q, kv in ((False, proj_p, bq_p, kv_p), (True, proj_l, bq_l, kv_l)):
        tag = "lat" if latent else "ctx"
        o = _attention(kind="diff", latent=latent, q_parts=[(proj, lambda h: h)], ko_parts=[(proj, lambda h: 8 + h)],
                       vo=(proj, lambda h: 16 + h), kc_parts=[(ak_ctx, lambda h: h)], vc=(av_ctx, lambda h: h),
                       lam=ev_lambda[i], subln=ev_subln_g[i][None, :], o_arr=o, o_blk=lambda h: h,
                       lam_init=lam_init, name="diff_attn_" + tag)
        o = _attention(kind="mla", latent=latent, q_parts=[(b_q, lambda h: h), (b_q, lambda h: 8 + h // 2)],
                       ko_parts=[(kv, lambda h: 2 * h), (proj, lambda h: 30 + h % 2)], vo=(kv, lambda h: 2 * h + 1),
                       kc_parts=[(kv_ctx, lambda h: 2 * h), (kr_ctx, lambda h: h % 2)], vc=(kv_ctx, lambda h: 2 * h + 1),
                       o_arr=o, o_blk=lambda h: 8 + h, name="mla_attn_" + tag)
    y = _out_projection(o, ev_w_out[i].astype(BF16), x["parts"], mt)
    states = (st_ak.reshape(BATCH, SEQ, N_HEADS, HEAD_DIM), st_av.reshape(BATCH, SEQ, N_HEADS, HEAD_DIM),
              st_ckv.reshape(BATCH, SEQ, B_KV_RANK), st_kr.reshape(BATCH, SEQ, B_ROPE_DIM))
    return y, states


def _odd_layer(x, mt, l, i, rope128, cache_c_k, cache_c_v, cache_d_k, cache_d_v, norm_mix_g, od_w_in, od_rpb,
               od_sink, od_w_out):
    w_in = od_w_in[i].astype(BF16)
    all_g = (0, 1, 2, 3)
    q_scale = HEAD_DIM ** -0.5 * LOG2E
    in_kw = dict(prologue="modulate", g=norm_mix_g[l][None, :], out_dtype=BF16,
                 tile_scale={t: q_scale for t in (0, 1, 6, 7)})
    head_src = lambda t0: tuple((t0 + t, gq, 4 * t + gq) for t in range(2) for gq in range(4))
    proj_p, st_ck, st_cv, st_dk, st_dv = _projection(
        *x["ctx"], N_SEG_P, D_MODEL, 0, D_MODEL, w_in, mt=mt[:N_SEG_P], name="odd_in_proj_ctx",
        states=((N_HEADS, head_src(2)), (N_HEADS, head_src(4)), (D_KV_HEADS, ((8, 0, 0), (8, 1, 1))),
                (D_KV_HEADS, ((8, 2, 0), (8, 3, 1)))), **in_kw)
    proj_l = _projection(*x["lat"], DEC_BATCH, D_MODEL, 0, D_MODEL, w_in, mt=mt[N_SEG_P:], rope=rope128,
                         sh=HEAD_DIM // 4, rope_groups={6: all_g, 7: all_g, 8: (0, 1)}, name="odd_in_proj_lat", **in_kw)
    ck_ctx = cache_c_k[:, i].reshape(DEC_BATCH, PAST_LEN, N_HEADS * HEAD_DIM)
    cv_ctx = cache_c_v[:, i].reshape(DEC_BATCH, PAST_LEN, N_HEADS * HEAD_DIM)
    dk_ctx = cache_d_k[:, i].reshape(DEC_BATCH, PAST_LEN, D_KV_HEADS * HEAD_DIM)
    dv_ctx = cache_d_v[:, i].reshape(DEC_BATCH, PAST_LEN, D_KV_HEADS * HEAD_DIM)
    bias = _neighbourhood_bias(od_rpb[i])
    sink = od_sink[i].astype(F32)
    o = jnp.zeros((N_TOK, D_MODEL), BF16)
    for latent, proj in ((False, proj_p), (True, proj_l)):
        tag = "lat" if latent else "ctx"
        o = _attention(kind="plain", latent=latent, q_parts=[(proj, lambda h: h)], ko_parts=[(proj, lambda h: 8 + h)],
                       vo=(proj, lambda h: 16 + h), kc_parts=[(ck_ctx, lambda h: h)], vc=(cv_ctx, lambda h: h),
                       bias=bias if latent else None, o_arr=o, o_blk=lambda h: h,
                       name="nbr_attn_" + tag)
        o = _attention(kind="plain", latent=latent, q_parts=[(proj, lambda h: 24 + h)],
                       ko_parts=[(proj, lambda h: 32 + h // D_GROUP)], vo=(proj, lambda h: 34 + h // D_GROUP),
                       kc_parts=[(dk_ctx, lambda h: h // D_GROUP)], vc=(dv_ctx, lambda h: h // D_GROUP), sink=sink,
                       o_arr=o, o_blk=lambda h: 8 + h, has_band=latent, tq_lat=256,
                       name="win_attn_" + tag)
    y = _out_projection(o, od_w_out[i].astype(BF16), x["parts"], mt)
    states = (st_ck.reshape(BATCH, SEQ, N_HEADS, HEAD_DIM), st_cv.reshape(BATCH, SEQ, N_HEADS, HEAD_DIM),
              st_dk.reshape(BATCH, SEQ, D_KV_HEADS, HEAD_DIM), st_dv.reshape(BATCH, SEQ, D_KV_HEADS, HEAD_DIM))
    return y, states


def _moe_layer(x, mt, g, router_w, router_b, layer, w_gate, w_up, w_down, final_g, final):
    h2, eidx, gates = _router(x, mt, g, router_w, router_b)
    src_tok, te, n_active, pos = _route_plan(eidx)
    ys = _moe_experts(h2, src_tok, te, n_active, layer, w_gate, w_up, w_down)
    pos2 = pos.reshape(2, N_TOK)
    gates_t = gates.T
    outs = []
    for row0, n_rows in ((0, N_PROMPT), (N_PROMPT, N_TOK - N_PROMPT)):
        ysel = ys.at[pos2[:, row0:row0 + n_rows].reshape(-1)].get(mode="promise_in_bounds")
        outs.append(_combine(x, row0, n_rows, ysel.reshape(2, n_rows, D_MODEL // 2), gates_t, mt, final_g, final))
    return outs


def kernel(x_prompt, x_sample, cache_a_k, cache_a_v, cache_b_ckv, cache_b_krope, cache_c_k, cache_c_v, cache_d_k, cache_d_v, c, c_ctx, w_ada, b_ada, norm_mix_g, norm_ffn_g, ev_w_in, ev_lambda, ev_subln_g, ev_q_norm_g, ev_kv_norm_g, ev_w_uq, ev_w_ukv, ev_w_out, od_w_in, od_rpb, od_sink, od_w_out, router_w, router_b, moe_w_gate, moe_w_up, moe_w_down, final_g):
    xp = x_prompt.reshape(N_PROMPT, D_MODEL)
    xl = x_sample.reshape(DEC_BATCH * DEC_SEQ, D_MODEL)
    x = dict(ctx=(xp, 0), lat=(xl, 0), parts=[xp, xl])
    cond8 = jnp.concatenate([c_ctx[None, :], c, jnp.zeros((3, D_MODEL), F32)], axis=0)
    mod = _adaln(cond8, w_ada, b_ada)
    seg_row = np.array([0] * N_SEG_P + [1 + b for b in range(DEC_BATCH)])
    mt_all = mod[:, seg_row].reshape(DEPTH, N_SEG, 6, D_MODEL)
    rope64 = _rope_tables(A_QK_DIM)
    rope128 = _rope_tables(HEAD_DIM)
    even_states, odd_states = [], []
    for l in range(DEPTH):
        i = l // 2
        mt = mt_all[l]
        if l % 2 == 0:
            x, st = _even_layer(x, mt, l, i, rope64, cache_a_k, cache_a_v, cache_b_ckv, cache_b_krope, norm_mix_g,
                                ev_w_in, ev_lambda, ev_subln_g, ev_q_norm_g, ev_kv_norm_g, ev_w_uq, ev_w_ukv, ev_w_out)
            even_states.append(st)
        else:
            x, st = _odd_layer(x, mt, l, i, rope128, cache_c_k, cache_c_v, cache_d_k, cache_d_v, norm_mix_g,
                               od_w_in, od_rpb, od_sink, od_w_out)
            odd_states.append(st)
        x = _moe_layer(x, mt, norm_ffn_g[l][None, :], router_w, router_b, l, moe_w_gate, moe_w_up, moe_w_down,
                       final_g[None, :], final=(l == DEPTH - 1))
        if l < DEPTH - 1:
            x = dict(ctx=(x[0], 0), lat=(x[1], 0), parts=list(x))
    y_prompt = x[0].reshape(BATCH, SEQ, D_MODEL)
    y_sample = x[1].reshape(DEC_BATCH, DEC_SEQ, D_MODEL)
    new_even = tuple(jnp.stack([st[k] for st in even_states], axis=1) for k in range(4))
    new_odd = tuple(jnp.stack([st[k] for st in odd_states], axis=1) for k in range(4))
    return (y_prompt, y_sample) + new_even + new_odd
```

```python
import functools
import math

import numpy as np
import jax
import jax.numpy as jnp
from jax import lax
from jax.experimental import pallas as pl
from jax.experimental.pallas import tpu as pltpu

D_MODEL = 2048
BATCH = 32
SEQ = 256
DEPTH = 2
DEC_BATCH = 4
DEC_SEQ = 1024
PAST_LEN = 256
GRID_W = 64
HEAD_DIM = 128
N_HEADS = 8
A_QK_DIM = 64
B_Q_RANK = 768
B_KV_RANK = 512
B_ROPE_DIM = 64
NA_ROWS = 8
NA_COLS = 16
D_KV_HEADS = 2
D_GROUP = 4
D_WINDOW = 128
N_EXPERTS = 16
N_GROUPS = 4
D_EXPERT = 1024
ROPE_THETA = 10000.0
EPS = 1e-6
NEG = -1e30
LOG2E = 1.4426950408889634

SEG = 1024
N_PROMPT = BATCH * SEQ
N_TOK = N_PROMPT + DEC_BATCH * DEC_SEQ
N_SEG = N_TOK // SEG
N_SEG_P = N_PROMPT // SEG
PROJ_W = 4608
TN = 512
LANES = 128
MOE_TM = 256
MOE_ROWS = 2 * N_TOK + N_EXPERTS * MOE_TM
MOE_TILES = MOE_ROWS // MOE_TM

F32 = jnp.float32
BF16 = jnp.bfloat16


def _cparams(sem, vmem_mb):
    return pltpu.CompilerParams(dimension_semantics=sem, vmem_limit_bytes=vmem_mb * 1024 * 1024)


def _dot(a, b):
    return jnp.dot(a, b, preferred_element_type=F32)


def _dot_nt(a, b):
    return lax.dot_general(a, b, (((1,), (1,)), ((), ())), preferred_element_type=F32)


def _sigmoid(x):
    return 1.0 / (1.0 + jnp.exp(-x))


def _pack_bf16_pair(x):
    c = x.shape[1] // 2
    bits = pltpu.bitcast(x.astype(BF16).astype(F32), jnp.int32)
    return bits[:, :c] | lax.shift_right_logical(bits[:, c:], 16)


def _unpack_bf16_pair(w):
    hi = pltpu.bitcast(w & jnp.int32(-65536), F32)
    lo = pltpu.bitcast(lax.shift_left(w, 16), F32)
    return hi, lo


def _adaln_kernel(c_ref, w_ref, b_ref, o_ref):
    c = c_ref[...]
    a = (c * _sigmoid(c)).astype(BF16)
    o_ref[...] = _dot(a, w_ref[...].astype(BF16)) + b_ref[...]


def _adaln(cond8, w_ada, b_ada):
    tn = 1024
    n = 6 * D_MODEL
    return pl.pallas_call(
        _adaln_kernel,
        grid=(DEPTH, n // tn),
        in_specs=[
            pl.BlockSpec((8, D_MODEL), lambda l, j: (0, 0)),
            pl.BlockSpec((None, D_MODEL, tn), lambda l, j: (l, 0, j)),
            pl.BlockSpec((None, 1, tn), lambda l, j: (l, 0, j)),
        ],
        out_specs=pl.BlockSpec((None, 8, tn), lambda l, j: (l, 0, j)),
        out_shape=jax.ShapeDtypeStruct((DEPTH, 8, n), F32),
        compiler_params=_cparams(("arbitrary", "arbitrary"), 40),
        name="adaln",
    )(cond8, w_ada, b_ada.reshape(DEPTH, 1, n))


def _rope(a, cos, sp, sm, sh):
    return a * cos + pltpu.roll(a, sh, 1) * sp + pltpu.roll(a, LANES - sh, 1) * sm


def _proj_kernel(*refs, prologue, k_in, sh, n_tiles, rope_groups, norm_tile, states, tile_scale):
    x_ref, mt_ref, g_ref, w_ref, cos_ref, sp_ref, sm_ref, ng_ref, o_ref = refs[:9]
    st_refs = refs[9:-1]
    xn_ref = refs[-1]
    j = pl.program_id(1)

    @pl.when(j == 0)
    def _():
        x = x_ref[:, :k_in]
        if prologue == "cast":
            xn_ref[...] = x.astype(BF16)
        else:
            r = lax.rsqrt(jnp.mean(x * x, axis=-1, keepdims=True) + EPS)
            if prologue == "modulate":
                y = x * r * (g_ref[...] * (1.0 + mt_ref[0, 1:2, :])) + mt_ref[0, 0:1, :]
            else:
                y = x * r * g_ref[...]
            xn_ref[...] = y.astype(BF16)

    acc = _dot(xn_ref[...], w_ref[...])
    n_grp = acc.shape[1] // LANES

    def treatment(t):
        acts = []
        for k, (n_heads, src) in enumerate(states):
            dest = 0
            for tt, a, b in src:
                if tt == t:
                    acts.append((k, a, b, dest))
                dest += 0 if n_heads else b
        return (tuple(rope_groups.get(t, ())), t == norm_tile, tuple(acts), float(tile_scale.get(t, 1.0)))

    branches = {}
    for t in range(n_tiles):
        branches.setdefault(treatment(t), []).append(t)

    o_ref[...] = acc.astype(o_ref.dtype)
    reread = o_ref.dtype == F32

    for (rg, is_norm, acts, sc), tiles in branches.items():
        if not rg and not is_norm and not acts and sc == 1.0:
            continue
        cond = j == tiles[0]
        for t in tiles[1:]:
            cond = jnp.logical_or(cond, j == t)

        @pl.when(cond)
        def _(rg=rg, is_norm=is_norm, acts=acts, sc=sc):
            vals = [(o_ref if reread else acc)[:, g * LANES:(g + 1) * LANES] for g in range(n_grp)]
            if sc != 1.0:
                vals = [v * sc for v in vals]
            if is_norm:
                ssq = None
                for v in vals:
                    s = jnp.sum(v * v, axis=-1, keepdims=True)
                    ssq = s if ssq is None else ssq + s
                scale = lax.rsqrt(ssq * (1.0 / (n_grp * LANES)) + EPS)
                ng = ng_ref[...]
                vals = [v * scale * ng[:, g * LANES:(g + 1) * LANES] for g, v in enumerate(vals)]
            elif rg:
                cos, sp, sm = cos_ref[...], sp_ref[...], sm_ref[...]
                vals = [_rope(v, cos, sp, sm, sh) if g in rg else v for g, v in enumerate(vals)]
            if is_norm or rg or sc != 1.0:
                for g, v in enumerate(vals):
                    if is_norm or sc != 1.0 or g in rg:
                        o_ref[:, g * LANES:(g + 1) * LANES] = v.astype(o_ref.dtype)
            for k, a, b, dest in acts:
                n_heads = states[k][0]
                if n_heads:
                    st_refs[k][pl.ds(b, SEG, stride=n_heads), :] = vals[a]
                elif b % LANES == 0:
                    for q in range(b // LANES):
                        st_refs[k][:, dest + q * LANES:dest + (q + 1) * LANES] = vals[a // LANES + q]
                else:
                    st_refs[k][...] = vals[a // LANES][:, a % LANES:a % LANES + b]


def _projection(x, row_off, n_seg, x_blk_w, x_blk_idx, k_in, w, *, prologue, mt=None, g=None, rope=None, sh=16,
                rope_groups=None, norm_tile=None, norm_g=None, states=(), out_dtype=F32, tile_scale=None, name):
    n = w.shape[1]
    m = n_seg * SEG
    assert n % TN == 0 and w.shape[0] == k_in
    if mt is None:
        mt = jnp.zeros((n_seg, 6, LANES), F32)
    if g is None:
        g = jnp.ones((1, k_in), F32)
    if rope is None:
        rope = tuple(jnp.zeros((8, LANES), F32) for _ in range(3))
    if norm_g is None:
        norm_g = jnp.ones((1, TN), F32)
    kern = functools.partial(_proj_kernel, prologue=prologue, k_in=k_in, sh=sh, n_tiles=n // TN,
                             rope_groups=dict(rope_groups or {}), norm_tile=norm_tile, states=tuple(states),
                             tile_scale=dict(tile_scale or {}))
    full2 = lambda i, j: (0, 0)
    out_shape = [jax.ShapeDtypeStruct((m, n), out_dtype)]
    out_specs = [pl.BlockSpec((SEG, TN), lambda i, j: (i, j))]
    for n_heads, src in states:
        if n_heads:
            out_shape.append(jax.ShapeDtypeStruct((m * n_heads, LANES), F32))
            out_specs.append(pl.BlockSpec((SEG * n_heads, LANES), lambda i, j: (i, 0)))
        else:
            width = sum(b for _, _, b in src)
            out_shape.append(jax.ShapeDtypeStruct((m, width), F32))
            out_specs.append(pl.BlockSpec((SEG, width), lambda i, j: (i, 0)))
    x_kw = dict(pipeline_mode=pl.Buffered(1)) if states else {}
    res = pl.pallas_call(
        kern,
        grid=(n_seg, n // TN),
        in_specs=[
            pl.BlockSpec((SEG, x_blk_w), lambda i, j: (i + row_off, x_blk_idx), **x_kw),
            pl.BlockSpec((1, 6, mt.shape[2]), lambda i, j: (i, 0, 0)),
            pl.BlockSpec((1, k_in), full2),
            pl.BlockSpec((k_in, TN), lambda i, j: (0, j)),
            pl.BlockSpec(rope[0].shape, full2),
            pl.BlockSpec(rope[1].shape, full2),
            pl.BlockSpec(rope[2].shape, full2),
            pl.BlockSpec((1, TN), full2),
        ],
        out_specs=out_specs,
        out_shape=out_shape,
        scratch_shapes=[pltpu.VMEM((SEG, k_in), BF16)],
        compiler_params=_cparams(("arbitrary", "arbitrary"), 56 if states else 48),
        name=name,
    )(x, mt, g, w, *rope, norm_g)
    return res if states else res[0]


def _attn_kernel(*refs, kind, n_parts, has_ctx, has_bias, has_band, has_sink, nb, s_len, tq, lam_init, own_win):
    it = iter(refs)
    q_refs = [next(it) for _ in range(n_parts)]
    ko_refs = [next(it) for _ in range(n_parts)]
    vo_ref = next(it)
    kc_refs, vc_ref = [], None
    if has_ctx:
        kc_refs = [next(it) for _ in range(n_parts)]
        vc_ref = next(it)
    bias_ref = next(it) if has_bias else None
    lam_ref = sg_ref = None
    if kind == "diff":
        lam_ref = next(it)
        sg_ref = next(it)
    sink_ref = next(it) if has_sink else None
    o_ref = next(it)

    sink2 = sink_ref[pl.program_id(1)] * LOG2E if has_sink else None
    lane = lax.broadcasted_iota(jnp.int32, (1, LANES), 1)

    def softmax_pv(s_list, v_list):
        m = None
        for s in s_list:
            mm = jnp.max(s, axis=-1, keepdims=True)
            m = mm if m is None else jnp.maximum(m, mm)
        if sink2 is not None:
            m = jnp.maximum(m, sink2)
        l = jnp.exp2(sink2 - m) if sink2 is not None else None
        o = None
        for s, v in zip(s_list, v_list):
            e = jnp.exp2(s - m)
            ss = jnp.sum(e, axis=-1, keepdims=True)
            l = ss if l is None else l + ss
            pv = _dot(e.astype(BF16), v)
            o = pv if o is None else o + pv
        return o, l

    for bi in range(nb):
        qrows = pl.ds(bi * tq, tq)
        q0 = pl.program_id(2) * tq
        if own_win is None:
            krows, k0 = pl.ds(bi * s_len, s_len), 0
            n_own = s_len
        else:
            k0 = pl.multiple_of(jnp.clip(q0 - D_WINDOW, 0, s_len - own_win), LANES)
            krows = pl.ds(k0, own_win)
            n_own = own_win
        k_srcs, v_srcs, is_own = [], [], []
        if has_ctx:
            k_srcs.append([r[...].astype(BF16) for r in kc_refs])
            v_srcs.append(vc_ref[...].astype(BF16))
            is_own.append(False)
        k_srcs.append([r[krows, :].astype(BF16) for r in ko_refs])
        v_srcs.append(vo_ref[krows, :].astype(BF16))
        is_own.append(True)

        def mask_own(s):
            if has_bias:
                s = s + bias_ref[...]
            if has_band:
                qpos = q0 + lax.broadcasted_iota(jnp.int32, (tq, 1), 0)
                kpos = k0 + lax.broadcasted_iota(jnp.int32, (1, n_own), 1)
                s = jnp.where(jnp.abs(kpos - qpos) <= D_WINDOW, s, NEG)
            return s

        if kind == "diff":
            q = q_refs[0][qrows, :]
            zero = jnp.zeros_like(q)
            q1 = jnp.where(lane < A_QK_DIM, q, zero)
            q2 = jnp.where(lane < A_QK_DIM, zero, q)
            o1, l1 = softmax_pv([_dot_nt(q1, ks[0]) for ks in k_srcs], v_srcs)
            o2, l2 = softmax_pv([_dot_nt(q2, ks[0]) for ks in k_srcs], v_srcs)
            lv = lam_ref[...]
            lam = (jnp.exp(jnp.sum(lv[0:1] * lv[1:2], axis=-1, keepdims=True))
                   - jnp.exp(jnp.sum(lv[2:3] * lv[3:4], axis=-1, keepdims=True)) + lam_init)
            o = o1 * (1.0 / l1) - o2 * (lam / l2)
            o = o * lax.rsqrt(jnp.mean(o * o, axis=-1, keepdims=True) + EPS) * sg_ref[...] * (1.0 - lam_init)
        else:
            qs = [r[qrows, :] for r in q_refs]
            s_list = []
            for ks, own in zip(k_srcs, is_own):
                s = None
                for qp, kp in zip(qs, ks):
                    d = _dot_nt(qp, kp)
                    s = d if s is None else s + d
                s_list.append(mask_own(s) if own else s)
            o, l = softmax_pv(s_list, v_srcs)
            o = o * (1.0 / l)
        o_ref[qrows, :] = o.astype(o_ref.dtype)


def _attention(*, kind, latent, q_parts, ko_parts, vo, kc_parts=(), vc=None, bias=None, lam=None, subln=None,
               sink=None, o_arr, o_blk, lam_init=0.0, has_band=False, tq_lat=512, name):
    n_parts = len(q_parts)
    has_ctx = latent
    if latent:
        tq, s_len, nb = tq_lat, DEC_SEQ, 1
        grid = (DEC_BATCH, N_HEADS, DEC_SEQ // tq)
        qpb = DEC_SEQ // tq
        q_row = lambda b, h, t: b * qpb + t
        o_row = lambda b, h, t: N_PROMPT // tq + b * qpb + t
        k_row = lambda b, h, t: b
        qblk = tq
    else:
        tq, s_len, nb = SEQ, SEQ, SEG // SEQ
        grid = (N_SEG_P, N_HEADS, 1)
        q_row = o_row = k_row = lambda b, h, t: b
        qblk = SEG
    args, specs = [], []

    def add(arr, shape, imap, **kw):
        args.append(arr)
        specs.append(pl.BlockSpec(shape, imap, **kw))

    for arr, f in q_parts:
        add(arr, (qblk, LANES), lambda b, h, t, f=f: (q_row(b, h, t), f(h)))
    for arr, f in list(ko_parts) + [vo]:
        add(arr, (SEG, LANES), lambda b, h, t, f=f: (k_row(b, h, t), f(h)))
    if has_ctx:
        for arr, f in list(kc_parts) + [vc]:
            add(arr, (None, PAST_LEN, LANES), lambda b, h, t, f=f: (b, 0, f(h)))
    if bias is not None:
        add(bias, (None, tq, DEC_SEQ), lambda b, h, t: (h, t, 0))
    if kind == "diff":
        add(lam, lam.shape, lambda b, h, t: (0, 0))
        add(subln, subln.shape, lambda b, h, t: (0, 0))
    if sink is not None:
        args.append(sink)
        specs.append(pl.BlockSpec(memory_space=pltpu.SMEM))
    n_in = len(args)
    args.append(o_arr)
    specs.append(pl.BlockSpec(memory_space=pl.ANY))
    kern = functools.partial(_attn_kernel_aliased, kind=kind, n_parts=n_parts, has_ctx=has_ctx,
                             has_bias=bias is not None, has_band=has_band, has_sink=sink is not None,
                             nb=nb, s_len=s_len, tq=tq, lam_init=lam_init,
                             own_win=(tq + 2 * D_WINDOW) if has_band else None)
    return pl.pallas_call(
        kern,
        grid=grid,
        in_specs=specs,
        out_specs=pl.BlockSpec((qblk, LANES), lambda b, h, t: (o_row(b, h, t), o_blk(h))),
        out_shape=jax.ShapeDtypeStruct(o_arr.shape, o_arr.dtype),
        input_output_aliases={n_in: 0},
        compiler_params=_cparams(("arbitrary", "arbitrary", "arbitrary"), 48),
        name=name,
    )(*args)


def _attn_kernel_aliased(*refs, **kw):
    _attn_kernel(*refs[:-2], refs[-1], **kw)


def _ffn_norm_route(x, mt_ref, g_ref, w_ref, rb_ref, h_ref, e_ref, gt_ref):
    r = lax.rsqrt(jnp.mean(x * x, axis=-1, keepdims=True) + EPS)
    h = x * r * (g_ref[...] * (1.0 + mt_ref[0, 4:5, :])) + mt_ref[0, 3:4, :]
    hh = h.astype(BF16)
    h_ref[...] = _pack_bf16_pair(hh)
    logits = _dot_nt(w_ref[...], hh)
    scores = _sigmoid(logits)
    sel = scores + rb_ref[...]
    per = N_EXPERTS // N_GROUPS
    sc = [scores[e:e + 1, :] for e in range(N_EXPERTS)]
    sl = [sel[e:e + 1, :] for e in range(N_EXPERTS)]
    best_g, best_v = None, None
    for gi in range(N_GROUPS):
        a, b, c, d = sl[gi * per:(gi + 1) * per]
        hi1, lo1, hi2, lo2 = jnp.maximum(a, b), jnp.minimum(a, b), jnp.maximum(c, d), jnp.minimum(c, d)
        gs = jnp.maximum(hi1, hi2) + jnp.maximum(jnp.minimum(hi1, hi2), jnp.maximum(lo1, lo2))
        if gi == 0:
            best_g, best_v = jnp.zeros_like(gs, dtype=jnp.int32), gs
        else:
            better = gs > best_v
            best_g = jnp.where(better, gi, best_g)
            best_v = jnp.where(better, gs, best_v)
    masked = [jnp.where(best_g == (e // per), sl[e], NEG) for e in range(N_EXPERTS)]
    i1, v1 = jnp.zeros_like(best_g), masked[0]
    for e in range(1, N_EXPERTS):
        better = masked[e] > v1
        i1 = jnp.where(better, e, i1)
        v1 = jnp.where(better, masked[e], v1)
    i2, v2 = None, None
    for e in range(N_EXPERTS):
        cand = jnp.where(i1 == e, -2e30, masked[e])
        if e == 0:
            i2, v2 = jnp.zeros_like(best_g), cand
        else:
            better = cand > v2
            i2 = jnp.where(better, e, i2)
            v2 = jnp.where(better, cand, v2)
    g1 = jnp.zeros_like(v1)
    g2 = jnp.zeros_like(v1)
    for e in range(N_EXPERTS):
        g1 = jnp.where(i1 == e, sc[e], g1)
        g2 = jnp.where(i2 == e, sc[e], g2)
    tot = g1 + g2
    rows = i1.shape[1]
    e_ref[...] = jnp.concatenate([i1, i2, jnp.zeros((6, rows), jnp.int32)], axis=0)
    gt_ref[...] = jnp.concatenate([g1 / tot, g2 / tot, jnp.zeros((6, rows), F32)], axis=0)


def _outproj_router_kernel(o_ref, w_ref, mt_ref, g_ref, rw_ref, rb_ref, *rest, blk_ranges):
    x_refs = rest[:-4]
    y_ref, h_ref, e_ref, gt_ref = rest[-4:]
    upd = mt_ref[0, 2:3, :] * _dot(o_ref[...], w_ref[...])
    if len(x_refs) == 1:
        y_ref[...] = x_refs[0][...] + upd
    else:
        i = pl.program_id(0)
        for x_ref, (lo, hi) in zip(x_refs, blk_ranges):
            @pl.when(jnp.logical_and(i >= lo, i < hi))
            def _(x_ref=x_ref):
                y_ref[...] = x_ref[...] + upd
    _ffn_norm_route(y_ref[...], mt_ref, g_ref, rw_ref, rb_ref, h_ref, e_ref, gt_ref)


def _out_projection_router(o, w, x_parts, mt, g, router_w, router_b):
    tm = 256
    blk_ranges, lo = [], 0
    for xp in x_parts:
        blk_ranges.append((lo, lo + xp.shape[0] // tm))
        lo = blk_ranges[-1][1]
    assert lo == N_TOK // tm
    x_specs = [pl.BlockSpec((tm, D_MODEL), lambda i, lo=lo, hi=hi: (jnp.clip(i - lo, 0, hi - lo - 1), 0))
               for lo, hi in blk_ranges]
    const = lambda i: (0, 0)
    return pl.pallas_call(
        functools.partial(_outproj_router_kernel, blk_ranges=tuple(blk_ranges)),
        grid=(N_TOK // tm,),
        in_specs=[
            pl.BlockSpec((tm, D_MODEL), lambda i: (i, 0)),
            pl.BlockSpec((D_MODEL, D_MODEL), const, pipeline_mode=pl.Buffered(1)),
            pl.BlockSpec((1, 6, D_MODEL), lambda i: (i // (SEG // tm), 0, 0)),
            pl.BlockSpec((1, D_MODEL), const),
            pl.BlockSpec((N_EXPERTS, D_MODEL), const),
            pl.BlockSpec((N_EXPERTS, 1), const),
        ] + x_specs,
        out_specs=[
            pl.BlockSpec((tm, D_MODEL), lambda i: (i, 0)),
            pl.BlockSpec((tm, D_MODEL // 2), lambda i: (i, 0)),
            pl.BlockSpec((8, tm), lambda i: (0, i)),
            pl.BlockSpec((8, tm), lambda i: (0, i)),
        ],
        out_shape=[
            jax.ShapeDtypeStruct((N_TOK, D_MODEL), F32),
            jax.ShapeDtypeStruct((N_TOK, D_MODEL // 2), jnp.int32),
            jax.ShapeDtypeStruct((8, N_TOK), jnp.int32),
            jax.ShapeDtypeStruct((8, N_TOK), F32),
        ],
        compiler_params=_cparams(("arbitrary",), 48),
        name="out_proj_router",
    )(o, w, mt, g, router_w.T.astype(BF16), router_b.reshape(N_EXPERTS, 1), *x_parts)


def _expert_weights(te_ref, first_ref, slot_ref, nxt_ref, i, t, layer, w_hbm, stage_ref, sem_ref, w_bf_refs):
    def copies(e, s):
        return [pltpu.make_async_copy(w.at[layer, e], stage_ref.at[s, k], sem_ref.at[s, k])
                for k, w in enumerate(w_hbm)]

    s = slot_ref[t]

    @pl.when(i == 0)
    def _():
        for c in copies(te_ref[t], s):
            c.start()

    @pl.when(first_ref[t] == 1)
    def _():
        for c in copies(te_ref[t], s):
            c.wait()

        @pl.when(nxt_ref[t] >= 0)
        def _():
            for c in copies(nxt_ref[t], 1 - s):
                c.start()

        for k, w_bf in enumerate(w_bf_refs):
            w_bf[...] = stage_ref[s, k].astype(BF16)


def _moe_up_kernel(te_ref, na_ref, first_ref, slot_ref, nxt_ref, xs_ref, wg_hbm, wu_hbm, hid_ref,
                   stage_ref, wgb_ref, wub_ref, sem_ref, *, tile0, layer):
    i = pl.program_id(0)
    t = i + tile0
    _expert_weights(te_ref, first_ref, slot_ref, nxt_ref, i, t, layer, (wg_hbm, wu_hbm), stage_ref, sem_ref,
                    (wgb_ref, wub_ref))

    @pl.when(t < na_ref[0])
    def _():
        x_hi, x_lo = _unpack_bf16_pair(xs_ref[...])
        x_hi, x_lo = x_hi.astype(BF16), x_lo.astype(BF16)
        half = D_MODEL // 2
        g = _dot(x_hi, wgb_ref[:half, :]) + _dot(x_lo, wgb_ref[half:, :])
        u = _dot(x_hi, wub_ref[:half, :]) + _dot(x_lo, wub_ref[half:, :])
        hid_ref[...] = (g * _sigmoid(g) * u).astype(BF16)

    @pl.when(t >= na_ref[0])
    def _():
        hid_ref[...] = jnp.zeros_like(hid_ref)


def _moe_down_kernel(te_ref, na_ref, first_ref, slot_ref, nxt_ref, *rest, layer, chunk_tiles):
    ys_ref, stage_ref, wdb_ref, sem_ref = rest[-4:]
    wd_hbm = rest[-5]
    hid_refs = rest[:-5]
    t = pl.program_id(0)
    _expert_weights(te_ref, first_ref, slot_ref, nxt_ref, t, t, layer, (wd_hbm,), stage_ref, sem_ref, (wdb_ref,))

    for c, hid_ref in enumerate(hid_refs):
        @pl.when(jnp.logical_and(t < na_ref[0], t // chunk_tiles == c))
        def _(hid_ref=hid_ref):
            ys_ref[...] = _pack_bf16_pair(_dot(hid_ref[...], wdb_ref[...]))

    @pl.when(t >= na_ref[0])
    def _():
        ys_ref[...] = jnp.zeros_like(ys_ref)


MOE_CHUNKS = 2


def _moe_experts(h2, src_tok, tile_expert, n_active, layer, w_gate, w_up, w_down):
    tiles = MOE_TILES // MOE_CHUNKS
    rows = tiles * MOE_TM

    def weight_runs(call_tiles):
        t_ids = jnp.arange(MOE_TILES, dtype=jnp.int32)
        prev = jnp.concatenate([tile_expert[:1], tile_expert[:-1]])
        first = jnp.logical_or(t_ids % call_tiles == 0, tile_expert != prev)
        slot = ((jnp.cumsum(first.astype(jnp.int32)) - 1) % 2).astype(jnp.int32)
        first_at = lax.cummin(jnp.where(first, t_ids, MOE_TILES), reverse=True)
        next_at = jnp.concatenate([first_at[1:], jnp.full((1,), MOE_TILES, jnp.int32)])
        same_call = jnp.logical_and(next_at < MOE_TILES, next_at // call_tiles == t_ids // call_tiles)
        nxt = jnp.where(jnp.logical_and(first, same_call), tile_expert[jnp.minimum(next_at, MOE_TILES - 1)], -1)
        return (tile_expert, n_active, first.astype(jnp.int32), slot, nxt.astype(jnp.int32))

    row_blk = lambda i, *_: (i, 0)
    hbm = pl.BlockSpec(memory_space=pl.ANY)
    plan = weight_runs(tiles)
    hids = []
    for c in range(MOE_CHUNKS):
        xs = h2.at[src_tok[c * rows:(c + 1) * rows]].get(mode="promise_in_bounds")
        hids.append(pl.pallas_call(
            functools.partial(_moe_up_kernel, tile0=c * tiles, layer=layer),
            grid_spec=pltpu.PrefetchScalarGridSpec(
                num_scalar_prefetch=len(plan),
                grid=(tiles,),
                in_specs=[pl.BlockSpec((MOE_TM, D_MODEL // 2), row_blk), hbm, hbm],
                out_specs=pl.BlockSpec((MOE_TM, D_EXPERT), row_blk),
                scratch_shapes=[pltpu.VMEM((2, 2, D_MODEL, D_EXPERT), F32),
                                pltpu.VMEM((D_MODEL, D_EXPERT), BF16), pltpu.VMEM((D_MODEL, D_EXPERT), BF16),
                                pltpu.SemaphoreType.DMA((2, 2))],
            ),
            out_shape=jax.ShapeDtypeStruct((rows, D_EXPERT), BF16),
            compiler_params=_cparams(("arbitrary",), 52),
            name="moe_up",
        )(*plan, xs, w_gate, w_up))
    plan = weight_runs(MOE_TILES)
    hid_specs = [pl.BlockSpec((MOE_TM, D_EXPERT), lambda i, *_, c=c: (jnp.clip(i - c * tiles, 0, tiles - 1), 0))
                 for c in range(MOE_CHUNKS)]
    return pl.pallas_call(
        functools.partial(_moe_down_kernel, layer=layer, chunk_tiles=tiles),
        grid_spec=pltpu.PrefetchScalarGridSpec(
            num_scalar_prefetch=len(plan),
            grid=(MOE_TILES,),
            in_specs=hid_specs + [hbm],
            out_specs=pl.BlockSpec((MOE_TM, D_MODEL // 2), row_blk),
            scratch_shapes=[pltpu.VMEM((2, 1, D_EXPERT, D_MODEL), F32), pltpu.VMEM((D_EXPERT, D_MODEL), BF16),
                            pltpu.SemaphoreType.DMA((2, 1))],
        ),
        out_shape=jax.ShapeDtypeStruct((MOE_ROWS, D_MODEL // 2), jnp.int32),
        compiler_params=_cparams(("arbitrary",), 40),
        name="moe_down",
    )(*plan, *hids, w_down)


PLAN_R = 2 * N_TOK // LANES


def _plan_kernel(e_ref, pos_ref, meta_ref):
    e = e_ref[...]
    r_i = lax.broadcasted_iota(jnp.int32, (LANES, LANES), 0)
    c_i = lax.broadcasted_iota(jnp.int32, (LANES, LANES), 1)
    upper = jnp.where(r_i <= c_i, 1.0, 0.0).astype(BF16)
    r_j = lax.broadcasted_iota(jnp.int32, (PLAN_R, PLAN_R), 0)
    c_j = lax.broadcasted_iota(jnp.int32, (PLAN_R, PLAN_R), 1)
    lower = jnp.where(c_j < r_j, 1.0, 0.0).astype(BF16)
    lane = lax.broadcasted_iota(jnp.int32, (1, LANES), 1)
    pos = jnp.zeros((PLAN_R, LANES), F32)
    cnt_row = jnp.zeros((1, LANES), F32)
    off_row = jnp.zeros((1, LANES), F32)
    end_row = []
    row_off = jnp.zeros((1, 1), F32)
    tiles_done = jnp.zeros((1, 1), F32)
    for ex in range(N_EXPERTS):
        m = jnp.where(e == ex, 1.0, 0.0)
        inc = _dot(m.astype(BF16), upper)
        tot = jnp.broadcast_to(inc[:, LANES - 1:LANES], (PLAN_R, LANES))
        before = _dot(lower, tot.astype(BF16))
        cnt = jnp.sum(inc[:, LANES - 1:LANES], axis=0, keepdims=True)
        pos = pos + m * (row_off + before + inc - 1.0)
        tiles = jnp.floor((cnt + (MOE_TM - 1)) * (1.0 / MOE_TM))
        cnt_row = jnp.where(lane == ex, cnt, cnt_row)
        off_row = jnp.where(lane == ex, row_off, off_row)
        tiles_done = tiles_done + tiles
        end_row.append(tiles_done)
        row_off = row_off + tiles * MOE_TM
    n_active = tiles_done
    tid = jnp.minimum(lane.astype(F32), n_active - 1.0)
    te = jnp.zeros((1, LANES), F32)
    for ex in range(N_EXPERTS):
        te = te + jnp.where(end_row[ex] <= tid, 1.0, 0.0)
    pos_ref[...] = pos.astype(jnp.int32)
    meta = jnp.concatenate([cnt_row, off_row, te, jnp.broadcast_to(n_active, (1, LANES)),
                            jnp.zeros((4, LANES), F32)], axis=0)
    meta_ref[...] = meta.astype(jnp.int32)


def _route_plan(eidx):
    e2 = eidx[:2].reshape(PLAN_R, LANES)
    pos, meta = pl.pallas_call(
        _plan_kernel,
        out_shape=[jax.ShapeDtypeStruct((PLAN_R, LANES), jnp.int32), jax.ShapeDtypeStruct((8, LANES), jnp.int32)],
        compiler_params=pltpu.CompilerParams(vmem_limit_bytes=32 * 1024 * 1024),
        name="route_plan",
    )(e2)
    cnt, row_off = meta[0, :N_EXPERTS], meta[1, :N_EXPERTS]
    te, n_active = meta[2, :MOE_TILES], meta[3, :1]
    start = jnp.cumsum(cnt) - cnt
    order = jnp.argsort(e2.reshape(-1), stable=True).astype(jnp.int32)
    row_e = jnp.repeat(te, MOE_TM)
    rank = jnp.arange(MOE_ROWS, dtype=jnp.int32) - row_off[row_e]
    valid = rank < cnt[row_e]
    src_asg = order[jnp.clip(start[row_e] + rank, 0, 2 * N_TOK - 1)]
    rows = jnp.arange(MOE_ROWS, dtype=jnp.int32)
    src_tok = jnp.where(valid, src_asg % N_TOK, rows % N_TOK)
    return src_tok, te, n_active, pos.reshape(-1)


def _combine_kernel(x_ref, y0_ref, y1_ref, gt_ref, mt_ref, fg_ref, o_ref, *, final):
    gt = gt_ref[...]
    g0, g1 = gt[:, 0:1], gt[:, 1:2]
    half = D_MODEL // 2
    y0 = _unpack_bf16_pair(y0_ref[...])
    y1 = _unpack_bf16_pair(y1_ref[...])
    xs = []
    for c in range(2):
        cols = slice(c * half, (c + 1) * half)
        xs.append(x_ref[:, cols] + mt_ref[0, 5:6, cols] * (y0[c] * g0 + y1[c] * g1))
    if final:
        ssq = jnp.sum(xs[0] * xs[0], axis=-1, keepdims=True) + jnp.sum(xs[1] * xs[1], axis=-1, keepdims=True)
        r = lax.rsqrt(ssq * (1.0 / D_MODEL) + EPS)
        xs = [xc * r * fg_ref[:, c * half:(c + 1) * half] for c, xc in enumerate(xs)]
    for c, xc in enumerate(xs):
        o_ref[:, c * half:(c + 1) * half] = xc


def _combine(x, row0, n_rows, ysel, gates_t, mt, final_g, final):
    tm = 256
    b0 = row0 // tm
    return pl.pallas_call(
        functools.partial(_combine_kernel, final=final),
        grid=(n_rows // tm,),
        in_specs=[
            pl.BlockSpec((tm, D_MODEL), lambda i: (i + b0, 0)),
            pl.BlockSpec((None, tm, D_MODEL // 2), lambda i: (0, i, 0)),
            pl.BlockSpec((None, tm, D_MODEL // 2), lambda i: (1, i, 0)),
            pl.BlockSpec((tm, 8), lambda i: (i + b0, 0)),
            pl.BlockSpec((1, 6, D_MODEL), lambda i: ((i + b0) // (SEG // tm), 0, 0)),
            pl.BlockSpec((1, D_MODEL), lambda i: (0, 0)),
        ],
        out_specs=pl.BlockSpec((tm, D_MODEL), lambda i: (i, 0)),
        out_shape=jax.ShapeDtypeStruct((n_rows, D_MODEL), F32),
        compiler_params=_cparams(("arbitrary",), 40),
        name="moe_combine",
    )(x, ysel, ysel, gates_t, mt, final_g)


def _rope_tables(dim):
    half = dim // 2
    inv = ROPE_THETA ** (-jnp.arange(0, half, 2, dtype=F32) / half)
    t = jnp.arange(DEC_SEQ)
    ang_r = (t // GRID_W).astype(F32)[:, None] * inv[None, :]
    ang_c = (t % GRID_W).astype(F32)[:, None] * inv[None, :]
    ang = jnp.concatenate([ang_r, ang_r, ang_c, ang_c], axis=-1)
    cos, sin = jnp.cos(ang), jnp.sin(ang)
    reps = LANES // dim
    cos = jnp.tile(cos, (1, reps))
    sin = jnp.tile(sin, (1, reps))
    sh = dim // 4
    second = (np.arange(LANES) % (2 * sh)) >= sh
    sp = jnp.where(second[None, :], sin, 0.0)
    sm = jnp.where(second[None, :], 0.0, -sin)
    return cos, sp, sm


def _neighbourhood_bias(rpb):
    rows = DEC_SEQ // GRID_W
    kh = min(NA_ROWS, rows)
    r = np.arange(rows)
    r0 = np.clip(r - kh // 2, 0, rows - kh)
    kr = np.arange(rows)
    row_ok = (kr[None, :] >= r0[:, None]) & (kr[None, :] < r0[:, None] + kh)
    c = np.arange(GRID_W)
    ws = np.clip(c - NA_COLS // 2, 0, GRID_W - NA_COLS)
    kc = np.arange(GRID_W)
    col_ok = (kc[None, :] >= ws[:, None]) & (kc[None, :] < ws[:, None] + NA_COLS)
    dc_idx = np.clip(kc[None, :] - c[:, None], -(NA_COLS - 1), NA_COLS - 1) + NA_COLS - 1
    sel_c = ((np.arange(2 * NA_COLS - 1)[:, None, None] == dc_idx[None]) & col_ok[None]).astype(np.float32)
    t = jnp.einsum("hab,bcx->hacx", rpb.astype(F32), sel_c, precision=lax.Precision.HIGHEST)
    t = jnp.where(col_ok[None, None], t * LOG2E, NEG)
    t = jnp.concatenate([t, t], axis=-1)

    def build(t_ref, o_ref):
        left = lax.broadcasted_iota(jnp.int32, (1, LANES), 1) < GRID_W
        neg = jnp.full((GRID_W, LANES), NEG, F32)
        for rr in range(rows):
            for p in range(rows // 2):
                parts = [t_ref[k - rr + NA_ROWS - 1] if row_ok[rr, k] else neg for k in (2 * p, 2 * p + 1)]
                blk = jnp.where(left, parts[0], parts[1]) if (row_ok[rr, 2 * p] or row_ok[rr, 2 * p + 1]) else neg
                o_ref[rr * GRID_W:(rr + 1) * GRID_W, p * LANES:(p + 1) * LANES] = blk

    return pl.pallas_call(
        build,
        grid=(N_HEADS,),
        in_specs=[pl.BlockSpec((None, 2 * NA_ROWS - 1, GRID_W, LANES), lambda h: (h, 0, 0, 0))],
        out_specs=pl.BlockSpec((None, DEC_SEQ, DEC_SEQ), lambda h: (h, 0, 0)),
        out_shape=jax.ShapeDtypeStruct((N_HEADS, DEC_SEQ, DEC_SEQ), F32),
        compiler_params=_cparams(("arbitrary",), 32),
        name="nbr_bias",
    )(t)


def _even_w_in_layout(w):
    cq_end = 3 * N_HEADS * HEAD_DIM + B_Q_RANK
    kv_end = cq_end + B_KV_RANK
    tk = 256

    def layout(w_ref, o_ref):
        o_ref[:, :cq_end] = w_ref[:, :cq_end].astype(BF16)
        o_ref[:, cq_end + 2 * LANES:] = w_ref[:, cq_end:kv_end].astype(BF16)
        kr = w_ref[:, kv_end:].astype(BF16)
        r = lax.broadcasted_iota(jnp.int32, (B_ROPE_DIM, 2 * LANES), 0)
        c = lax.broadcasted_iota(jnp.int32, (B_ROPE_DIM, 2 * LANES), 1)
        place = jnp.where(jnp.logical_or(c == r, c == r + 2 * LANES - B_ROPE_DIM), 1.0, 0.0).astype(BF16)
        o_ref[:, cq_end:cq_end + 2 * LANES] = _dot(kr, place).astype(BF16)

    return pl.pallas_call(
        layout,
        grid=(D_MODEL // tk,),
        in_specs=[pl.BlockSpec((tk, w.shape[1]), lambda i: (i, 0))],
        out_specs=pl.BlockSpec((tk, PROJ_W), lambda i: (i, 0)),
        out_shape=jax.ShapeDtypeStruct((D_MODEL, PROJ_W), BF16),
        compiler_params=_cparams(("arbitrary",), 32),
        name="even_w_in_layout",
    )(w)


def _even_layer(x, mt, l, i, rope64, cache_a_k, cache_a_v, cache_b_ckv, cache_b_krope, norm_mix_g, ev_w_in,
                ev_lambda, ev_subln_g, ev_q_norm_g, ev_kv_norm_g, ev_w_uq, ev_w_ukv, ev_w_out):
    w_in = _even_w_in_layout(ev_w_in[i])
    all_g = (0, 1, 2, 3)
    a_scale = A_QK_DIM ** -0.5 * LOG2E
    in_kw = dict(prologue="modulate", g=norm_mix_g[l][None, :], norm_tile=8, norm_g=ev_kv_norm_g[i][None, :],
                 out_dtype=BF16, tile_scale={0: a_scale, 1: a_scale})
    bcq_src = (0, ((6, 0, TN), (7, 0, TN)))
    head_src = lambda t0: tuple((t0 + t, gq, 4 * t + gq) for t in range(2) for gq in range(4))
    proj_p, st_ak, st_av, st_ckv, st_kr, bcq_p = _projection(
        *x["ctx"], N_SEG_P, D_MODEL, 0, D_MODEL, w_in, mt=mt[:N_SEG_P], name="even_in_proj_ctx",
        states=((N_HEADS, head_src(2)), (N_HEADS, head_src(4)), (0, ((8, 0, B_KV_RANK),)), (0, ((7, 256, B_ROPE_DIM),)),
                bcq_src), **in_kw)
    proj_l, bcq_l = _projection(*x["lat"], DEC_BATCH, D_MODEL, 0, D_MODEL, w_in, mt=mt[N_SEG_P:], rope=rope64,
                                sh=A_QK_DIM // 4, rope_groups={0: all_g, 1: all_g, 2: all_g, 3: all_g, 7: (2, 3)},
                                states=(bcq_src,), name="even_in_proj_lat", **in_kw)
    wq = ev_w_uq[i].reshape(B_Q_RANK, N_HEADS, HEAD_DIM + B_ROPE_DIM)
    w_uq = jnp.concatenate([wq[:, :, :HEAD_DIM].reshape(B_Q_RANK, -1), wq[:, :, HEAD_DIM:].reshape(B_Q_RANK, -1)],
                           axis=1).astype(BF16)
    mla_scale = (HEAD_DIM + B_ROPE_DIM) ** -0.5
    q_kw = dict(prologue="rmsnorm", g=ev_q_norm_g[i][None, :], out_dtype=BF16,
                tile_scale={t: mla_scale * LOG2E for t in range(3)})
    bq_p = _projection(bcq_p, 0, N_SEG_P, 1024, 0, B_Q_RANK, w_uq, name="mla_q_up_ctx", **q_kw)
    bq_l = _projection(bcq_l, 0, DEC_BATCH, 1024, 0, B_Q_RANK, w_uq, rope=rope64, sh=B_ROPE_DIM // 4,
                       rope_groups={2: all_g}, name="mla_q_up_lat", **q_kw)
    w_ukv = ev_w_ukv[i].astype(BF16)
    kv_kw = dict(prologue="cast", out_dtype=BF16)
    kv_p = _projection(proj_p, 0, N_SEG_P, 512, 8, B_KV_RANK, w_ukv, name="mla_kv_up_ctx", **kv_kw)
    kv_l = _projection(proj_l, 0, DEC_BATCH, 512, 8, B_KV_RANK, w_ukv, name="mla_kv_up_lat", **kv_kw)
    kv_ctx = _projection(cache_b_ckv[:, i].reshape(DEC_BATCH * PAST_LEN, B_KV_RANK), 0, 1, 512, 0, B_KV_RANK, w_ukv,
                         name="mla_kv_up_cache", **kv_kw)
    kv_ctx = kv_ctx.reshape(DEC_BATCH, PAST_LEN, 2 * N_HEADS * HEAD_DIM)
    krc = cache_b_krope[:, i]
    zc = jnp.zeros_like(krc)
    kr_ctx = jnp.concatenate([krc, zc, zc, krc], axis=-1)
    ak_ctx = cache_a_k[:, i].reshape(DEC_BATCH, PAST_LEN, N_HEADS * HEAD_DIM)
    av_ctx = cache_a_v[:, i].reshape(DEC_BATCH, PAST_LEN, N_HEADS * HEAD_DIM)
    lam_init = 0.8 - 0.6 * math.exp(-0.3 * l)
    o = jnp.zeros((N_TOK, D_MODEL), BF16)
    for latent, proj, b_q, kv in ((False, proj_p, bq_p, kv_p), (True, proj_l, bq_l, kv_l)):
        tag = "lat" if latent else "ctx"
        o = _attention(kind="diff", latent=latent, q_parts=[(proj, lambda h: h)], ko_parts=[(proj, lambda h: 8 + h)],
                       vo=(proj, lambda h: 16 + h), kc_parts=[(ak_ctx, lambda h: h)], vc=(av_ctx, lambda h: h),
                       lam=ev_lambda[i], subln=ev_subln_g[i][None, :], o_arr=o, o_blk=lambda h: h,
                       lam_init=lam_init, name="diff_attn_" + tag)
        o = _attention(kind="mla", latent=latent, q_parts=[(b_q, lambda h: h), (b_q, lambda h: 8 + h // 2)],
                       ko_parts=[(kv, lambda h: 2 * h), (proj, lambda h: 30 + h % 2)], vo=(kv, lambda h: 2 * h + 1),
                       kc_parts=[(kv_ctx, lambda h: 2 * h), (kr_ctx, lambda h: h % 2)], vc=(kv_ctx, lambda h: 2 * h + 1),
                       o_arr=o, o_blk=lambda h: 8 + h, name="mla_attn_" + tag)
    states = (st_ak.reshape(BATCH, SEQ, N_HEADS, HEAD_DIM), st_av.reshape(BATCH, SEQ, N_HEADS, HEAD_DIM),
              st_ckv.reshape(BATCH, SEQ, B_KV_RANK), st_kr.reshape(BATCH, SEQ, B_ROPE_DIM))
    return (o, ev_w_out[i].astype(BF16)), states


def _odd_layer(x, mt, l, i, rope128, cache_c_k, cache_c_v, cache_d_k, cache_d_v, norm_mix_g, od_w_in, od_rpb,
               od_sink, od_w_out):
    w_in = od_w_in[i].astype(BF16)
    all_g = (0, 1, 2, 3)
    q_scale = HEAD_DIM ** -0.5 * LOG2E
    in_kw = dict(prologue="modulate", g=norm_mix_g[l][None, :], out_dtype=BF16,
                 tile_scale={t: q_scale for t in (0, 1, 6, 7)})
    head_src = lambda t0: tuple((t0 + t, gq, 4 * t + gq) for t in range(2) for gq in range(4))
    proj_p, st_ck, st_cv, st_dk, st_dv = _projection(
        *x["ctx"], N_SEG_P, D_MODEL, 0, D_MODEL, w_in, mt=mt[:N_SEG_P], name="odd_in_proj_ctx",
        states=((N_HEADS, head_src(2)), (N_HEADS, head_src(4)), (D_KV_HEADS, ((8, 0, 0), (8, 1, 1))),
                (D_KV_HEADS, ((8, 2, 0), (8, 3, 1)))), **in_kw)
    proj_l = _projection(*x["lat"], DEC_BATCH, D_MODEL, 0, D_MODEL, w_in, mt=mt[N_SEG_P:], rope=rope128,
                         sh=HEAD_DIM // 4, rope_groups={6: all_g, 7: all_g, 8: (0, 1)}, name="odd_in_proj_lat", **in_kw)
    ck_ctx = cache_c_k[:, i].reshape(DEC_BATCH, PAST_LEN, N_HEADS * HEAD_DIM)
    cv_ctx = cache_c_v[:, i].reshape(DEC_BATCH, PAST_LEN, N_HEADS * HEAD_DIM)
    dk_ctx = cache_d_k[:, i].reshape(DEC_BATCH, PAST_LEN, D_KV_HEADS * HEAD_DIM)
    dv_ctx = cache_d_v[:, i].reshape(DEC_BATCH, PAST_LEN, D_KV_HEADS * HEAD_DIM)
    bias = _neighbourhood_bias(od_rpb[i])
    sink = od_sink[i].astype(F32)
    o = jnp.zeros((N_TOK, D_MODEL), BF16)
    for latent, proj in ((False, proj_p), (True, proj_l)):
        tag = "lat" if latent else "ctx"
        o = _attention(kind="plain", latent=latent, q_parts=[(proj, lambda h: h)], ko_parts=[(proj, lambda h: 8 + h)],
                       vo=(proj, lambda h: 16 + h), kc_parts=[(ck_ctx, lambda h: h)], vc=(cv_ctx, lambda h: h),
                       bias=bias if latent else None, o_arr=o, o_blk=lambda h: h,
                       name="nbr_attn_" + tag)
        o = _attention(kind="plain", latent=latent, q_parts=[(proj, lambda h: 24 + h)],
                       ko_parts=[(proj, lambda h: 32 + h // D_GROUP)], vo=(proj, lambda h: 34 + h // D_GROUP),
                       kc_parts=[(dk_ctx, lambda h: h // D_GROUP)], vc=(dv_ctx, lambda h: h // D_GROUP), sink=sink,
                       o_arr=o, o_blk=lambda h: 8 + h, has_band=latent, tq_lat=256,
                       name="win_attn_" + tag)
    states = (st_ck.reshape(BATCH, SEQ, N_HEADS, HEAD_DIM), st_cv.reshape(BATCH, SEQ, N_HEADS, HEAD_DIM),
              st_dk.reshape(BATCH, SEQ, D_KV_HEADS, HEAD_DIM), st_dv.reshape(BATCH, SEQ, D_KV_HEADS, HEAD_DIM))
    return (o, od_w_out[i].astype(BF16)), states


def _moe_layer(mix, x_parts, mt, g, router_w, router_b, layer, w_gate, w_up, w_down, final_g, final):
    x, h2, eidx, gates = _out_projection_router(*mix, x_parts, mt, g, router_w, router_b)
    src_tok, te, n_active, pos = _route_plan(eidx)
    ys = _moe_experts(h2, src_tok, te, n_active, layer, w_gate, w_up, w_down)
    pos2 = pos.reshape(2, N_TOK)
    gates_t = gates.T
    outs = []
    for row0, n_rows in ((0, N_PROMPT), (N_PROMPT, N_TOK - N_PROMPT)):
        ysel = ys.at[pos2[:, row0:row0 + n_rows].reshape(-1)].get(mode="promise_in_bounds")
        outs.append(_combine(x, row0, n_rows, ysel.reshape(2, n_rows, D_MODEL // 2), gates_t, mt, final_g, final))
    return outs


def kernel(x_prompt, x_sample, cache_a_k, cache_a_v, cache_b_ckv, cache_b_krope, cache_c_k, cache_c_v, cache_d_k, cache_d_v, c, c_ctx, w_ada, b_ada, norm_mix_g, norm_ffn_g, ev_w_in, ev_lambda, ev_subln_g, ev_q_norm_g, ev_kv_norm_g, ev_w_uq, ev_w_ukv, ev_w_out, od_w_in, od_rpb, od_sink, od_w_out, router_w, router_b, moe_w_gate, moe_w_up, moe_w_down, final_g):
    xp = x_prompt.reshape(N_PROMPT, D_MODEL)
    xl = x_sample.reshape(DEC_BATCH * DEC_SEQ, D_MODEL)
    x = dict(ctx=(xp, 0), lat=(xl, 0), parts=[xp, xl])
    cond8 = jnp.concatenate([c_ctx[None, :], c, jnp.zeros((3, D_MODEL), F32)], axis=0)
    mod = _adaln(cond8, w_ada, b_ada)
    seg_row = np.array([0] * N_SEG_P + [1 + b for b in range(DEC_BATCH)])
    mt_all = mod[:, seg_row].reshape(DEPTH, N_SEG, 6, D_MODEL)
    rope64 = _rope_tables(A_QK_DIM)
    rope128 = _rope_tables(HEAD_DIM)
    even_states, odd_states = [], []
    for l in range(DEPTH):
        i = l // 2
        mt = mt_all[l]
        if l % 2 == 0:
            mix, st = _even_layer(x, mt, l, i, rope64, cache_a_k, cache_a_v, cache_b_ckv, cache_b_krope, norm_mix_g,
                                ev_w_in, ev_lambda, ev_subln_g, ev_q_norm_g, ev_kv_norm_g, ev_w_uq, ev_w_ukv, ev_w_out)
            even_states.append(st)
        else:
            mix, st = _odd_layer(x, mt, l, i, rope128, cache_c_k, cache_c_v, cache_d_k, cache_d_v, norm_mix_g,
                               od_w_in, od_rpb, od_sink, od_w_out)
            odd_states.append(st)
        x = _moe_layer(mix, x["parts"], mt, norm_ffn_g[l][None, :], router_w, router_b, l, moe_w_gate, moe_w_up, moe_w_down,
                       final_g[None, :], final=(l == DEPTH - 1))
        if l < DEPTH - 1:
            x = dict(ctx=(x[0], 0), lat=(x[1], 0), parts=list(x))
    y_prompt = x[0].reshape(BATCH, SEQ, D_MODEL)
    y_sample = x[1].reshape(DEC_BATCH, DEC_SEQ, D_MODEL)
    new_even = tuple(jnp.stack([st[k] for st in even_states], axis=1) for k in range(4))
    new_odd = tuple(jnp.stack([st[k] for st in odd_states], axis=1) for k in range(4))
    return (y_prompt, y_sample) + new_even + new_odd
```

```python
import functools
import math

import numpy as np
import jax
import jax.numpy as jnp
from jax import lax
from jax.experimental import pallas as pl
from jax.experimental.pallas import tpu as pltpu

D_MODEL = 2048
BATCH = 32
SEQ = 256
DEPTH = 2
DEC_BATCH = 4
DEC_SEQ = 1024
PAST_LEN = 256
GRID_W = 64
HEAD_DIM = 128
N_HEADS = 8
A_QK_DIM = 64
B_Q_RANK = 768
B_KV_RANK = 512
B_ROPE_DIM = 64
NA_ROWS = 8
NA_COLS = 16
D_KV_HEADS = 2
D_GROUP = 4
D_WINDOW = 128
N_EXPERTS = 16
N_GROUPS = 4
D_EXPERT = 1024
ROPE_THETA = 10000.0
EPS = 1e-6
NEG = -1e30
LOG2E = 1.4426950408889634

SEG = 1024
N_PROMPT = BATCH * SEQ
N_TOK = N_PROMPT + DEC_BATCH * DEC_SEQ
N_SEG = N_TOK // SEG
N_SEG_P = N_PROMPT // SEG
PROJ_W = 4608
TN = 512
LANES = 128
MOE_TM = 256
MOE_ROWS = 2 * N_TOK + N_EXPERTS * MOE_TM
MOE_TILES = MOE_ROWS // MOE_TM

F32 = jnp.float32
BF16 = jnp.bfloat16


def _cparams(sem, vmem_mb):
    return pltpu.CompilerParams(dimension_semantics=sem, vmem_limit_bytes=vmem_mb * 1024 * 1024)


def _dot(a, b):
    return jnp.dot(a, b, preferred_element_type=F32)


def _dot_nt(a, b):
    return lax.dot_general(a, b, (((1,), (1,)), ((), ())), preferred_element_type=F32)


def _sigmoid(x):
    return 1.0 / (1.0 + jnp.exp(-x))


def _pack_bf16_pair(x):
    c = x.shape[1] // 2
    bits = pltpu.bitcast(x.astype(BF16).astype(F32), jnp.int32)
    return bits[:, :c] | lax.shift_right_logical(bits[:, c:], 16)


def _unpack_bf16_pair(w):
    hi = pltpu.bitcast(w & jnp.int32(-65536), F32)
    lo = pltpu.bitcast(lax.shift_left(w, 16), F32)
    return hi, lo


def _adaln_kernel(c_ref, w_ref, b_ref, o_ref):
    c = c_ref[...]
    a = (c * _sigmoid(c)).astype(BF16)
    o_ref[...] = _dot(a, w_ref[...].astype(BF16)) + b_ref[...]


def _adaln(cond8, w_ada, b_ada):
    tn = 1024
    n = 6 * D_MODEL
    return pl.pallas_call(
        _adaln_kernel,
        grid=(DEPTH, n // tn),
        in_specs=[
            pl.BlockSpec((8, D_MODEL), lambda l, j: (0, 0)),
            pl.BlockSpec((None, D_MODEL, tn), lambda l, j: (l, 0, j)),
            pl.BlockSpec((None, 1, tn), lambda l, j: (l, 0, j)),
        ],
        out_specs=pl.BlockSpec((None, 8, tn), lambda l, j: (l, 0, j)),
        out_shape=jax.ShapeDtypeStruct((DEPTH, 8, n), F32),
        compiler_params=_cparams(("arbitrary", "arbitrary"), 40),
        name="adaln",
    )(cond8, w_ada, b_ada.reshape(DEPTH, 1, n))


def _rope(a, cos, sp, sm, sh):
    return a * cos + pltpu.roll(a, sh, 1) * sp + pltpu.roll(a, LANES - sh, 1) * sm


def _proj_kernel(*refs, prologue, k_in, sh, n_tiles, rope_groups, norm_tile, states, tile_scale):
    x_ref, mt_ref, g_ref, w_ref, cos_ref, sp_ref, sm_ref, ng_ref, o_ref = refs[:9]
    st_refs = refs[9:-1]
    xn_ref = refs[-1]
    j = pl.program_id(1)

    @pl.when(j == 0)
    def _():
        x = x_ref[:, :k_in]
        if prologue == "cast":
            xn_ref[...] = x.astype(BF16)
        else:
            r = lax.rsqrt(jnp.mean(x * x, axis=-1, keepdims=True) + EPS)
            if prologue == "modulate":
                y = x * r * (g_ref[...] * (1.0 + mt_ref[0, 1:2, :])) + mt_ref[0, 0:1, :]
            else:
                y = x * r * g_ref[...]
            xn_ref[...] = y.astype(BF16)

    acc = _dot(xn_ref[...], w_ref[...])
    n_grp = acc.shape[1] // LANES

    def treatment(t):
        acts = []
        for k, (n_heads, src) in enumerate(states):
            dest = 0
            for tt, a, b in src:
                if tt == t:
                    acts.append((k, a, b, dest))
                dest += 0 if n_heads else b
        return (tuple(rope_groups.get(t, ())), t == norm_tile, tuple(acts), float(tile_scale.get(t, 1.0)))

    branches = {}
    for t in range(n_tiles):
        branches.setdefault(treatment(t), []).append(t)

    o_ref[...] = acc.astype(o_ref.dtype)
    reread = o_ref.dtype == F32

    for (rg, is_norm, acts, sc), tiles in branches.items():
        if not rg and not is_norm and not acts and sc == 1.0:
            continue
        cond = j == tiles[0]
        for t in tiles[1:]:
            cond = jnp.logical_or(cond, j == t)

        @pl.when(cond)
        def _(rg=rg, is_norm=is_norm, acts=acts, sc=sc):
            vals = [(o_ref if reread else acc)[:, g * LANES:(g + 1) * LANES] for g in range(n_grp)]
            if sc != 1.0:
                vals = [v * sc for v in vals]
            if is_norm:
                ssq = None
                for v in vals:
                    s = jnp.sum(v * v, axis=-1, keepdims=True)
                    ssq = s if ssq is None else ssq + s
                scale = lax.rsqrt(ssq * (1.0 / (n_grp * LANES)) + EPS)
                ng = ng_ref[...]
                vals = [v * scale * ng[:, g * LANES:(g + 1) * LANES] for g, v in enumerate(vals)]
            elif rg:
                cos, sp, sm = cos_ref[...], sp_ref[...], sm_ref[...]
                vals = [_rope(v, cos, sp, sm, sh) if g in rg else v for g, v in enumerate(vals)]
            if is_norm or rg or sc != 1.0:
                for g, v in enumerate(vals):
                    if is_norm or sc != 1.0 or g in rg:
                        o_ref[:, g * LANES:(g + 1) * LANES] = v.astype(o_ref.dtype)
            for k, a, b, dest in acts:
                n_heads = states[k][0]
                if n_heads:
                    st_refs[k][pl.ds(b, SEG, stride=n_heads), :] = vals[a]
                elif b % LANES == 0:
                    for q in range(b // LANES):
                        st_refs[k][:, dest + q * LANES:dest + (q + 1) * LANES] = vals[a // LANES + q]
                else:
                    st_refs[k][...] = vals[a // LANES][:, a % LANES:a % LANES + b]


def _projection(x, row_off, n_seg, x_blk_w, x_blk_idx, k_in, w, *, prologue, mt=None, g=None, rope=None, sh=16,
                rope_groups=None, norm_tile=None, norm_g=None, states=(), out_dtype=F32, tile_scale=None, name):
    n = w.shape[1]
    m = n_seg * SEG
    assert n % TN == 0 and w.shape[0] == k_in
    if mt is None:
        mt = jnp.zeros((n_seg, 6, LANES), F32)
    if g is None:
        g = jnp.ones((1, k_in), F32)
    if rope is None:
        rope = tuple(jnp.zeros((8, LANES), F32) for _ in range(3))
    if norm_g is None:
        norm_g = jnp.ones((1, TN), F32)
    kern = functools.partial(_proj_kernel, prologue=prologue, k_in=k_in, sh=sh, n_tiles=n // TN,
                             rope_groups=dict(rope_groups or {}), norm_tile=norm_tile, states=tuple(states),
                             tile_scale=dict(tile_scale or {}))
    full2 = lambda i, j: (0, 0)
    out_shape = [jax.ShapeDtypeStruct((m, n), out_dtype)]
    out_specs = [pl.BlockSpec((SEG, TN), lambda i, j: (i, j))]
    for n_heads, src in states:
        if n_heads:
            out_shape.append(jax.ShapeDtypeStruct((m * n_heads, LANES), F32))
            out_specs.append(pl.BlockSpec((SEG * n_heads, LANES), lambda i, j: (i, 0)))
        else:
            width = sum(b for _, _, b in src)
            out_shape.append(jax.ShapeDtypeStruct((m, width), F32))
            out_specs.append(pl.BlockSpec((SEG, width), lambda i, j: (i, 0)))
    x_kw = dict(pipeline_mode=pl.Buffered(1)) if states else {}
    res = pl.pallas_call(
        kern,
        grid=(n_seg, n // TN),
        in_specs=[
            pl.BlockSpec((SEG, x_blk_w), lambda i, j: (i + row_off, x_blk_idx), **x_kw),
            pl.BlockSpec((1, 6, mt.shape[2]), lambda i, j: (i, 0, 0)),
            pl.BlockSpec((1, k_in), full2),
            pl.BlockSpec((k_in, TN), lambda i, j: (0, j)),
            pl.BlockSpec(rope[0].shape, full2),
            pl.BlockSpec(rope[1].shape, full2),
            pl.BlockSpec(rope[2].shape, full2),
            pl.BlockSpec((1, TN), full2),
        ],
        out_specs=out_specs,
        out_shape=out_shape,
        scratch_shapes=[pltpu.VMEM((SEG, k_in), BF16)],
        compiler_params=_cparams(("arbitrary", "arbitrary"), 56 if states else 48),
        name=name,
    )(x, mt, g, w, *rope, norm_g)
    return res if states else res[0]


def _attn_kernel(*refs, kind, n_parts, has_ctx, has_bias, has_band, has_sink, nb, s_len, tq, lam_init, own_win):
    it = iter(refs)
    q_refs = [next(it) for _ in range(n_parts)]
    ko_refs = [next(it) for _ in range(n_parts)]
    vo_ref = next(it)
    kc_refs, vc_ref = [], None
    if has_ctx:
        kc_refs = [next(it) for _ in range(n_parts)]
        vc_ref = next(it)
    bias_ref = next(it) if has_bias else None
    lam_ref = sg_ref = None
    if kind == "diff":
        lam_ref = next(it)
        sg_ref = next(it)
    sink_ref = next(it) if has_sink else None
    o_ref = next(it)

    sink2 = sink_ref[pl.program_id(1)] * LOG2E if has_sink else None
    lane = lax.broadcasted_iota(jnp.int32, (1, LANES), 1)

    def softmax_pv(s_list, v_list):
        m = None
        for s in s_list:
            mm = jnp.max(s, axis=-1, keepdims=True)
            m = mm if m is None else jnp.maximum(m, mm)
        if sink2 is not None:
            m = jnp.maximum(m, sink2)
        l = jnp.exp2(sink2 - m) if sink2 is not None else None
        o = None
        for s, v in zip(s_list, v_list):
            e = jnp.exp2(s - m)
            ss = jnp.sum(e, axis=-1, keepdims=True)
            l = ss if l is None else l + ss
            pv = _dot(e.astype(BF16), v)
            o = pv if o is None else o + pv
        return o, l

    for bi in range(nb):
        qrows = pl.ds(bi * tq, tq)
        q0 = pl.program_id(2) * tq
        if own_win is None:
            krows, k0 = pl.ds(bi * s_len, s_len), 0
            n_own = s_len
        else:
            k0 = pl.multiple_of(jnp.clip(q0 - D_WINDOW, 0, s_len - own_win), LANES)
            krows = pl.ds(k0, own_win)
            n_own = own_win
        k_srcs, v_srcs, is_own = [], [], []
        if has_ctx:
            k_srcs.append([r[...].astype(BF16) for r in kc_refs])
            v_srcs.append(vc_ref[...].astype(BF16))
            is_own.append(False)
        k_srcs.append([r[krows, :].astype(BF16) for r in ko_refs])
        v_srcs.append(vo_ref[krows, :].astype(BF16))
        is_own.append(True)

        def mask_own(s):
            if has_bias:
                s = s + bias_ref[...]
            if has_band:
                qpos = q0 + lax.broadcasted_iota(jnp.int32, (tq, 1), 0)
                kpos = k0 + lax.broadcasted_iota(jnp.int32, (1, n_own), 1)
                s = jnp.where(jnp.abs(kpos - qpos) <= D_WINDOW, s, NEG)
            return s

        if kind == "diff":
            q = q_refs[0][qrows, :]
            zero = jnp.zeros_like(q)
            q1 = jnp.where(lane < A_QK_DIM, q, zero)
            q2 = jnp.where(lane < A_QK_DIM, zero, q)
            o1, l1 = softmax_pv([_dot_nt(q1, ks[0]) for ks in k_srcs], v_srcs)
            o2, l2 = softmax_pv([_dot_nt(q2, ks[0]) for ks in k_srcs], v_srcs)
            lv = lam_ref[...]
            lam = (jnp.exp(jnp.sum(lv[0:1] * lv[1:2], axis=-1, keepdims=True))
                   - jnp.exp(jnp.sum(lv[2:3] * lv[3:4], axis=-1, keepdims=True)) + lam_init)
            o = o1 * (1.0 / l1) - o2 * (lam / l2)
            o = o * lax.rsqrt(jnp.mean(o * o, axis=-1, keepdims=True) + EPS) * sg_ref[...] * (1.0 - lam_init)
        else:
            qs = [r[qrows, :] for r in q_refs]
            s_list = []
            for ks, own in zip(k_srcs, is_own):
                s = None
                for qp, kp in zip(qs, ks):
                    d = _dot_nt(qp, kp)
                    s = d if s is None else s + d
                s_list.append(mask_own(s) if own else s)
            o, l = softmax_pv(s_list, v_srcs)
            o = o * (1.0 / l)
        o_ref[qrows, :] = o.astype(o_ref.dtype)


def _attention(*, kind, latent, q_parts, ko_parts, vo, kc_parts=(), vc=None, bias=None, lam=None, subln=None,
               sink=None, o_arr, o_blk, lam_init=0.0, has_band=False, tq_lat=512, name):
    n_parts = len(q_parts)
    has_ctx = latent
    if latent:
        tq, s_len, nb = tq_lat, DEC_SEQ, 1
        grid = (DEC_BATCH, N_HEADS, DEC_SEQ // tq)
        qpb = DEC_SEQ // tq
        q_row = lambda b, h, t: b * qpb + t
        o_row = lambda b, h, t: N_PROMPT // tq + b * qpb + t
        k_row = lambda b, h, t: b
        qblk = tq
    else:
        tq, s_len, nb = SEQ, SEQ, SEG // SEQ
        grid = (N_SEG_P, N_HEADS, 1)
        q_row = o_row = k_row = lambda b, h, t: b
        qblk = SEG
    args, specs = [], []

    def add(arr, shape, imap, **kw):
        args.append(arr)
        specs.append(pl.BlockSpec(shape, imap, **kw))

    for arr, f in q_parts:
        add(arr, (qblk, LANES), lambda b, h, t, f=f: (q_row(b, h, t), f(h)))
    for arr, f in list(ko_parts) + [vo]:
        add(arr, (SEG, LANES), lambda b, h, t, f=f: (k_row(b, h, t), f(h)))
    if has_ctx:
        for arr, f in list(kc_parts) + [vc]:
            add(arr, (None, PAST_LEN, LANES), lambda b, h, t, f=f: (b, 0, f(h)))
    if bias is not None:
        add(bias, (None, tq, DEC_SEQ), lambda b, h, t: (h, t, 0))
    if kind == "diff":
        add(lam, lam.shape, lambda b, h, t: (0, 0))
        add(subln, subln.shape, lambda b, h, t: (0, 0))
    if sink is not None:
        args.append(sink)
        specs.append(pl.BlockSpec(memory_space=pltpu.SMEM))
    n_in = len(args)
    args.append(o_arr)
    specs.append(pl.BlockSpec(memory_space=pl.ANY))
    kern = functools.partial(_attn_kernel_aliased, kind=kind, n_parts=n_parts, has_ctx=has_ctx,
                             has_bias=bias is not None, has_band=has_band, has_sink=sink is not None,
                             nb=nb, s_len=s_len, tq=tq, lam_init=lam_init,
                             own_win=(tq + 2 * D_WINDOW) if has_band else None)
    return pl.pallas_call(
        kern,
        grid=grid,
        in_specs=specs,
        out_specs=pl.BlockSpec((qblk, LANES), lambda b, h, t: (o_row(b, h, t), o_blk(h))),
        out_shape=jax.ShapeDtypeStruct(o_arr.shape, o_arr.dtype),
        input_output_aliases={n_in: 0},
        compiler_params=_cparams(("arbitrary", "arbitrary", "arbitrary"), 48),
        name=name,
    )(*args)


def _attn_kernel_aliased(*refs, **kw):
    _attn_kernel(*refs[:-2], refs[-1], **kw)


def _ffn_norm_route(x, mt_ref, g_ref, w_ref, rb_ref, h_ref, e_ref, gt_ref):
    r = lax.rsqrt(jnp.mean(x * x, axis=-1, keepdims=True) + EPS)
    h = x * r * (g_ref[...] * (1.0 + mt_ref[0, 4:5, :])) + mt_ref[0, 3:4, :]
    hh = h.astype(BF16)
    h_ref[...] = _pack_bf16_pair(hh)
    logits = _dot_nt(w_ref[...], hh)
    scores = _sigmoid(logits)
    sel = scores + rb_ref[...]
    per = N_EXPERTS // N_GROUPS
    sc = [scores[e:e + 1, :] for e in range(N_EXPERTS)]
    sl = [sel[e:e + 1, :] for e in range(N_EXPERTS)]
    best_g, best_v = None, None
    for gi in range(N_GROUPS):
        a, b, c, d = sl[gi * per:(gi + 1) * per]
        hi1, lo1, hi2, lo2 = jnp.maximum(a, b), jnp.minimum(a, b), jnp.maximum(c, d), jnp.minimum(c, d)
        gs = jnp.maximum(hi1, hi2) + jnp.maximum(jnp.minimum(hi1, hi2), jnp.maximum(lo1, lo2))
        if gi == 0:
            best_g, best_v = jnp.zeros_like(gs, dtype=jnp.int32), gs
        else:
            better = gs > best_v
            best_g = jnp.where(better, gi, best_g)
            best_v = jnp.where(better, gs, best_v)
    masked = [jnp.where(best_g == (e // per), sl[e], NEG) for e in range(N_EXPERTS)]
    i1, v1 = jnp.zeros_like(best_g), masked[0]
    for e in range(1, N_EXPERTS):
        better = masked[e] > v1
        i1 = jnp.where(better, e, i1)
        v1 = jnp.where(better, masked[e], v1)
    i2, v2 = None, None
    for e in range(N_EXPERTS):
        cand = jnp.where(i1 == e, -2e30, masked[e])
        if e == 0:
            i2, v2 = jnp.zeros_like(best_g), cand
        else:
            better = cand > v2
            i2 = jnp.where(better, e, i2)
            v2 = jnp.where(better, cand, v2)
    g1 = jnp.zeros_like(v1)
    g2 = jnp.zeros_like(v1)
    for e in range(N_EXPERTS):
        g1 = jnp.where(i1 == e, sc[e], g1)
        g2 = jnp.where(i2 == e, sc[e], g2)
    tot = g1 + g2
    rows = i1.shape[1]
    e_ref[...] = jnp.concatenate([i1, i2, jnp.zeros((6, rows), jnp.int32)], axis=0)
    gt_ref[...] = jnp.concatenate([g1 / tot, g2 / tot, jnp.zeros((6, rows), F32)], axis=0)


def _outproj_router_kernel(o_ref, w_ref, mt_ref, g_ref, rw_ref, rb_ref, *rest, blk_ranges):
    x_refs = rest[:-4]
    y_ref, h_ref, e_ref, gt_ref = rest[-4:]
    upd = mt_ref[0, 2:3, :] * _dot(o_ref[...], w_ref[...])
    if len(x_refs) == 1:
        y_ref[...] = x_refs[0][...] + upd
    else:
        i = pl.program_id(0)
        for x_ref, (lo, hi) in zip(x_refs, blk_ranges):
            @pl.when(jnp.logical_and(i >= lo, i < hi))
            def _(x_ref=x_ref):
                y_ref[...] = x_ref[...] + upd
    _ffn_norm_route(y_ref[...], mt_ref, g_ref, rw_ref, rb_ref, h_ref, e_ref, gt_ref)


def _out_projection_router(o, w, x_parts, mt, g, router_w, router_b):
    tm = 256
    blk_ranges, lo = [], 0
    for xp in x_parts:
        blk_ranges.append((lo, lo + xp.shape[0] // tm))
        lo = blk_ranges[-1][1]
    assert lo == N_TOK // tm
    x_specs = [pl.BlockSpec((tm, D_MODEL), lambda i, lo=lo, hi=hi: (jnp.clip(i - lo, 0, hi - lo - 1), 0))
               for lo, hi in blk_ranges]
    const = lambda i: (0, 0)
    return pl.pallas_call(
        functools.partial(_outproj_router_kernel, blk_ranges=tuple(blk_ranges)),
        grid=(N_TOK // tm,),
        in_specs=[
            pl.BlockSpec((tm, D_MODEL), lambda i: (i, 0)),
            pl.BlockSpec((D_MODEL, D_MODEL), const, pipeline_mode=pl.Buffered(1)),
            pl.BlockSpec((1, 6, D_MODEL), lambda i: (i // (SEG // tm), 0, 0)),
            pl.BlockSpec((1, D_MODEL), const),
            pl.BlockSpec((N_EXPERTS, D_MODEL), const),
            pl.BlockSpec((N_EXPERTS, 1), const),
        ] + x_specs,
        out_specs=[
            pl.BlockSpec((tm, D_MODEL), lambda i: (i, 0)),
            pl.BlockSpec((tm, D_MODEL // 2), lambda i: (i, 0)),
            pl.BlockSpec((8, tm), lambda i: (0, i)),
            pl.BlockSpec((8, tm), lambda i: (0, i)),
        ],
        out_shape=[
            jax.ShapeDtypeStruct((N_TOK, D_MODEL), F32),
            jax.ShapeDtypeStruct((N_TOK, D_MODEL // 2), jnp.int32),
            jax.ShapeDtypeStruct((8, N_TOK), jnp.int32),
            jax.ShapeDtypeStruct((8, N_TOK), F32),
        ],
        compiler_params=_cparams(("arbitrary",), 48),
        name="out_proj_router",
    )(o, w, mt, g, router_w.T.astype(BF16), router_b.reshape(N_EXPERTS, 1), *x_parts)


def _expert_weights(te_ref, first_ref, slot_ref, nxt_ref, i, t, layer, w_hbm, stage_ref, sem_ref, w_bf_refs):
    def copies(e, s):
        return [pltpu.make_async_copy(w.at[layer, e], stage_ref.at[s, k], sem_ref.at[s, k])
                for k, w in enumerate(w_hbm)]

    s = slot_ref[t]

    @pl.when(i == 0)
    def _():
        for c in copies(te_ref[t], s):
            c.start()

    @pl.when(first_ref[t] == 1)
    def _():
        for c in copies(te_ref[t], s):
            c.wait()

        @pl.when(nxt_ref[t] >= 0)
        def _():
            for c in copies(nxt_ref[t], 1 - s):
                c.start(priority=1)

        for k, w_bf in enumerate(w_bf_refs):
            w_bf[...] = stage_ref[s, k].astype(BF16)


def _moe_up_kernel(te_ref, na_ref, first_ref, slot_ref, nxt_ref, xs_ref, wg_hbm, wu_hbm, hid_ref,
                   stage_ref, wgb_ref, wub_ref, sem_ref, *, tile0, layer):
    i = pl.program_id(0)
    t = i + tile0
    _expert_weights(te_ref, first_ref, slot_ref, nxt_ref, i, t, layer, (wg_hbm, wu_hbm), stage_ref, sem_ref,
                    (wgb_ref, wub_ref))

    @pl.when(t < na_ref[0])
    def _():
        x_hi, x_lo = _unpack_bf16_pair(xs_ref[...])
        x_hi, x_lo = x_hi.astype(BF16), x_lo.astype(BF16)
        half = D_MODEL // 2
        g = _dot(x_hi, wgb_ref[:half, :]) + _dot(x_lo, wgb_ref[half:, :])
        u = _dot(x_hi, wub_ref[:half, :]) + _dot(x_lo, wub_ref[half:, :])
        hid_ref[...] = (g * _sigmoid(g) * u).astype(BF16)

    @pl.when(t >= na_ref[0])
    def _():
        hid_ref[...] = jnp.zeros_like(hid_ref)


def _moe_down_kernel(te_ref, na_ref, first_ref, slot_ref, nxt_ref, *rest, layer, chunk_tiles):
    ys_ref, stage_ref, wdb_ref, sem_ref = rest[-4:]
    wd_hbm = rest[-5]
    hid_refs = rest[:-5]
    t = pl.program_id(0)
    _expert_weights(te_ref, first_ref, slot_ref, nxt_ref, t, t, layer, (wd_hbm,), stage_ref, sem_ref, (wdb_ref,))

    for c, hid_ref in enumerate(hid_refs):
        @pl.when(jnp.logical_and(t < na_ref[0], t // chunk_tiles == c))
        def _(hid_ref=hid_ref):
            ys_ref[...] = _pack_bf16_pair(_dot(hid_ref[...], wdb_ref[...]))

    @pl.when(t >= na_ref[0])
    def _():
        ys_ref[...] = jnp.zeros_like(ys_ref)


MOE_CHUNKS = 2


def _moe_experts(h2, src_tok, tile_expert, n_active, layer, w_gate, w_up, w_down):
    tiles = MOE_TILES // MOE_CHUNKS
    rows = tiles * MOE_TM

    def weight_runs(call_tiles):
        t_ids = jnp.arange(MOE_TILES, dtype=jnp.int32)
        prev = jnp.concatenate([tile_expert[:1], tile_expert[:-1]])
        first = jnp.logical_or(t_ids % call_tiles == 0, tile_expert != prev)
        slot = ((jnp.cumsum(first.astype(jnp.int32)) - 1) % 2).astype(jnp.int32)
        first_at = lax.cummin(jnp.where(first, t_ids, MOE_TILES), reverse=True)
        next_at = jnp.concatenate([first_at[1:], jnp.full((1,), MOE_TILES, jnp.int32)])
        same_call = jnp.logical_and(next_at < MOE_TILES, next_at // call_tiles == t_ids // call_tiles)
        nxt = jnp.where(jnp.logical_and(first, same_call), tile_expert[jnp.minimum(next_at, MOE_TILES - 1)], -1)
        return (tile_expert, n_active, first.astype(jnp.int32), slot, nxt.astype(jnp.int32))

    row_blk = lambda i, *_: (i, 0)
    hbm = pl.BlockSpec(memory_space=pl.ANY)
    plan = weight_runs(tiles)
    hids = []
    for c in range(MOE_CHUNKS):
        xs = h2.at[src_tok[c * rows:(c + 1) * rows]].get(mode="promise_in_bounds")
        hids.append(pl.pallas_call(
            functools.partial(_moe_up_kernel, tile0=c * tiles, layer=layer),
            grid_spec=pltpu.PrefetchScalarGridSpec(
                num_scalar_prefetch=len(plan),
                grid=(tiles,),
                in_specs=[pl.BlockSpec((MOE_TM, D_MODEL // 2), row_blk), hbm, hbm],
                out_specs=pl.BlockSpec((MOE_TM, D_EXPERT), row_blk),
                scratch_shapes=[pltpu.VMEM((2, 2, D_MODEL, D_EXPERT), F32),
                                pltpu.VMEM((D_MODEL, D_EXPERT), BF16), pltpu.VMEM((D_MODEL, D_EXPERT), BF16),
                                pltpu.SemaphoreType.DMA((2, 2))],
            ),
            out_shape=jax.ShapeDtypeStruct((rows, D_EXPERT), BF16),
            compiler_params=_cparams(("arbitrary",), 52),
            name="moe_up",
        )(*plan, xs, w_gate, w_up))
    plan = weight_runs(MOE_TILES)
    hid_specs = [pl.BlockSpec((MOE_TM, D_EXPERT), lambda i, *_, c=c: (jnp.clip(i - c * tiles, 0, tiles - 1), 0))
                 for c in range(MOE_CHUNKS)]
    return pl.pallas_call(
        functools.partial(_moe_down_kernel, layer=layer, chunk_tiles=tiles),
        grid_spec=pltpu.PrefetchScalarGridSpec(
            num_scalar_prefetch=len(plan),
            grid=(MOE_TILES,),
            in_specs=hid_specs + [hbm],
            out_specs=pl.BlockSpec((MOE_TM, D_MODEL // 2), row_blk),
            scratch_shapes=[pltpu.VMEM((2, 1, D_EXPERT, D_MODEL), F32), pltpu.VMEM((D_EXPERT, D_MODEL), BF16),
                            pltpu.SemaphoreType.DMA((2, 1))],
        ),
        out_shape=jax.ShapeDtypeStruct((MOE_ROWS, D_MODEL // 2), jnp.int32),
        compiler_params=_cparams(("arbitrary",), 40),
        name="moe_down",
    )(*plan, *hids, w_down)


PLAN_R = 2 * N_TOK // LANES


def _plan_kernel(e_ref, pos_ref, meta_ref):
    e = e_ref[...]
    r_i = lax.broadcasted_iota(jnp.int32, (LANES, LANES), 0)
    c_i = lax.broadcasted_iota(jnp.int32, (LANES, LANES), 1)
    upper = jnp.where(r_i <= c_i, 1.0, 0.0).astype(BF16)
    r_j = lax.broadcasted_iota(jnp.int32, (PLAN_R, PLAN_R), 0)
    c_j = lax.broadcasted_iota(jnp.int32, (PLAN_R, PLAN_R), 1)
    lower = jnp.where(c_j < r_j, 1.0, 0.0).astype(BF16)
    lane = lax.broadcasted_iota(jnp.int32, (1, LANES), 1)
    pos = jnp.zeros((PLAN_R, LANES), F32)
    cnt_row = jnp.zeros((1, LANES), F32)
    off_row = jnp.zeros((1, LANES), F32)
    end_row = []
    row_off = jnp.zeros((1, 1), F32)
    tiles_done = jnp.zeros((1, 1), F32)
    for ex in range(N_EXPERTS):
        m = jnp.where(e == ex, 1.0, 0.0)
        inc = _dot(m.astype(BF16), upper)
        tot = jnp.broadcast_to(inc[:, LANES - 1:LANES], (PLAN_R, LANES))
        before = _dot(lower, tot.astype(BF16))
        cnt = jnp.sum(inc[:, LANES - 1:LANES], axis=0, keepdims=True)
        pos = pos + m * (row_off + before + inc - 1.0)
        tiles = jnp.floor((cnt + (MOE_TM - 1)) * (1.0 / MOE_TM))
        cnt_row = jnp.where(lane == ex, cnt, cnt_row)
        off_row = jnp.where(lane == ex, row_off, off_row)
        tiles_done = tiles_done + tiles
        end_row.append(tiles_done)
        row_off = row_off + tiles * MOE_TM
    n_active = tiles_done
    tid = jnp.minimum(lane.astype(F32), n_active - 1.0)
    te = jnp.zeros((1, LANES), F32)
    for ex in range(N_EXPERTS):
        te = te + jnp.where(end_row[ex] <= tid, 1.0, 0.0)
    pos_ref[...] = pos.astype(jnp.int32)
    meta = jnp.concatenate([cnt_row, off_row, te, jnp.broadcast_to(n_active, (1, LANES)),
                            jnp.zeros((4, LANES), F32)], axis=0)
    meta_ref[...] = meta.astype(jnp.int32)


def _route_plan(eidx):
    e2 = eidx[:2].reshape(PLAN_R, LANES)
    pos, meta = pl.pallas_call(
        _plan_kernel,
        out_shape=[jax.ShapeDtypeStruct((PLAN_R, LANES), jnp.int32), jax.ShapeDtypeStruct((8, LANES), jnp.int32)],
        compiler_params=pltpu.CompilerParams(vmem_limit_bytes=32 * 1024 * 1024),
        name="route_plan",
    )(e2)
    cnt, row_off = meta[0, :N_EXPERTS], meta[1, :N_EXPERTS]
    te, n_active = meta[2, :MOE_TILES], meta[3, :1]
    start = jnp.cumsum(cnt) - cnt
    order = jnp.argsort(e2.reshape(-1), stable=True).astype(jnp.int32)
    row_e = jnp.repeat(te, MOE_TM)
    rank = jnp.arange(MOE_ROWS, dtype=jnp.int32) - row_off[row_e]
    valid = rank < cnt[row_e]
    src_asg = order[jnp.clip(start[row_e] + rank, 0, 2 * N_TOK - 1)]
    rows = jnp.arange(MOE_ROWS, dtype=jnp.int32)
    src_tok = jnp.where(valid, src_asg % N_TOK, rows % N_TOK)
    return src_tok, te, n_active, pos.reshape(-1)


def _combine_kernel(x_ref, y0_ref, y1_ref, gt_ref, mt_ref, fg_ref, o_ref, *, final):
    gt = gt_ref[...]
    g0, g1 = gt[:, 0:1], gt[:, 1:2]
    half = D_MODEL // 2
    y0 = _unpack_bf16_pair(y0_ref[...])
    y1 = _unpack_bf16_pair(y1_ref[...])
    xs = []
    for c in range(2):
        cols = slice(c * half, (c + 1) * half)
        xs.append(x_ref[:, cols] + mt_ref[0, 5:6, cols] * (y0[c] * g0 + y1[c] * g1))
    if final:
        ssq = jnp.sum(xs[0] * xs[0], axis=-1, keepdims=True) + jnp.sum(xs[1] * xs[1], axis=-1, keepdims=True)
        r = lax.rsqrt(ssq * (1.0 / D_MODEL) + EPS)
        xs = [xc * r * fg_ref[:, c * half:(c + 1) * half] for c, xc in enumerate(xs)]
    for c, xc in enumerate(xs):
        o_ref[:, c * half:(c + 1) * half] = xc


def _combine(x, row0, n_rows, ysel, gates_t, mt, final_g, final):
    tm = 256
    b0 = row0 // tm
    return pl.pallas_call(
        functools.partial(_combine_kernel, final=final),
        grid=(n_rows // tm,),
        in_specs=[
            pl.BlockSpec((tm, D_MODEL), lambda i: (i + b0, 0)),
            pl.BlockSpec((None, tm, D_MODEL // 2), lambda i: (0, i, 0)),
            pl.BlockSpec((None, tm, D_MODEL // 2), lambda i: (1, i, 0)),
            pl.BlockSpec((tm, 8), lambda i: (i + b0, 0)),
            pl.BlockSpec((1, 6, D_MODEL), lambda i: ((i + b0) // (SEG // tm), 0, 0)),
            pl.BlockSpec((1, D_MODEL), lambda i: (0, 0)),
        ],
        out_specs=pl.BlockSpec((tm, D_MODEL), lambda i: (i, 0)),
        out_shape=jax.ShapeDtypeStruct((n_rows, D_MODEL), F32),
        compiler_params=_cparams(("arbitrary",), 40),
        name="moe_combine",
    )(x, ysel, ysel, gates_t, mt, final_g)


def _rope_tables(dim):
    half = dim // 2
    inv = ROPE_THETA ** (-jnp.arange(0, half, 2, dtype=F32) / half)
    t = jnp.arange(DEC_SEQ)
    ang_r = (t // GRID_W).astype(F32)[:, None] * inv[None, :]
    ang_c = (t % GRID_W).astype(F32)[:, None] * inv[None, :]
    ang = jnp.concatenate([ang_r, ang_r, ang_c, ang_c], axis=-1)
    cos, sin = jnp.cos(ang), jnp.sin(ang)
    reps = LANES // dim
    cos = jnp.tile(cos, (1, reps))
    sin = jnp.tile(sin, (1, reps))
    sh = dim // 4
    second = (np.arange(LANES) % (2 * sh)) >= sh
    sp = jnp.where(second[None, :], sin, 0.0)
    sm = jnp.where(second[None, :], 0.0, -sin)
    return cos, sp, sm


def _neighbourhood_bias(rpb):
    rows = DEC_SEQ // GRID_W
    kh = min(NA_ROWS, rows)
    r = np.arange(rows)
    r0 = np.clip(r - kh // 2, 0, rows - kh)
    kr = np.arange(rows)
    row_ok = (kr[None, :] >= r0[:, None]) & (kr[None, :] < r0[:, None] + kh)
    c = np.arange(GRID_W)
    ws = np.clip(c - NA_COLS // 2, 0, GRID_W - NA_COLS)
    kc = np.arange(GRID_W)
    col_ok = (kc[None, :] >= ws[:, None]) & (kc[None, :] < ws[:, None] + NA_COLS)
    dc_idx = np.clip(kc[None, :] - c[:, None], -(NA_COLS - 1), NA_COLS - 1) + NA_COLS - 1
    sel_c = ((np.arange(2 * NA_COLS - 1)[:, None, None] == dc_idx[None]) & col_ok[None]).astype(np.float32)
    t = jnp.einsum("hab,bcx->hacx", rpb.astype(F32), sel_c, precision=lax.Precision.HIGHEST)
    t = jnp.where(col_ok[None, None], t * LOG2E, NEG)
    t = jnp.concatenate([t, t], axis=-1)

    def build(t_ref, o_ref):
        left = lax.broadcasted_iota(jnp.int32, (1, LANES), 1) < GRID_W
        neg = jnp.full((GRID_W, LANES), NEG, F32)
        for rr in range(rows):
            for p in range(rows // 2):
                parts = [t_ref[k - rr + NA_ROWS - 1] if row_ok[rr, k] else neg for k in (2 * p, 2 * p + 1)]
                blk = jnp.where(left, parts[0], parts[1]) if (row_ok[rr, 2 * p] or row_ok[rr, 2 * p + 1]) else neg
                o_ref[rr * GRID_W:(rr + 1) * GRID_W, p * LANES:(p + 1) * LANES] = blk

    return pl.pallas_call(
        build,
        grid=(N_HEADS,),
        in_specs=[pl.BlockSpec((None, 2 * NA_ROWS - 1, GRID_W, LANES), lambda h: (h, 0, 0, 0))],
        out_specs=pl.BlockSpec((None, DEC_SEQ, DEC_SEQ), lambda h: (h, 0, 0)),
        out_shape=jax.ShapeDtypeStruct((N_HEADS, DEC_SEQ, DEC_SEQ), F32),
        compiler_params=_cparams(("arbitrary",), 32),
        name="nbr_bias",
    )(t)


def _even_w_in_layout(w):
    cq_end = 3 * N_HEADS * HEAD_DIM + B_Q_RANK
    kv_end = cq_end + B_KV_RANK
    tk = 256

    def layout(w_ref, o_ref):
        o_ref[:, :cq_end] = w_ref[:, :cq_end].astype(BF16)
        o_ref[:, cq_end + 2 * LANES:] = w_ref[:, cq_end:kv_end].astype(BF16)
        kr = w_ref[:, kv_end:].astype(BF16)
        r = lax.broadcasted_iota(jnp.int32, (B_ROPE_DIM, 2 * LANES), 0)
        c = lax.broadcasted_iota(jnp.int32, (B_ROPE_DIM, 2 * LANES), 1)
        place = jnp.where(jnp.logical_or(c == r, c == r + 2 * LANES - B_ROPE_DIM), 1.0, 0.0).astype(BF16)
        o_ref[:, cq_end:cq_end + 2 * LANES] = _dot(kr, place).astype(BF16)

    return pl.pallas_call(
        layout,
        grid=(D_MODEL // tk,),
        in_specs=[pl.BlockSpec((tk, w.shape[1]), lambda i: (i, 0))],
        out_specs=pl.BlockSpec((tk, PROJ_W), lambda i: (i, 0)),
        out_shape=jax.ShapeDtypeStruct((D_MODEL, PROJ_W), BF16),
        compiler_params=_cparams(("arbitrary",), 32),
        name="even_w_in_layout",
    )(w)


def _even_layer(x, mt, l, i, rope64, cache_a_k, cache_a_v, cache_b_ckv, cache_b_krope, norm_mix_g, ev_w_in,
                ev_lambda, ev_subln_g, ev_q_norm_g, ev_kv_norm_g, ev_w_uq, ev_w_ukv, ev_w_out):
    w_in = _even_w_in_layout(ev_w_in[i])
    all_g = (0, 1, 2, 3)
    a_scale = A_QK_DIM ** -0.5 * LOG2E
    in_kw = dict(prologue="modulate", g=norm_mix_g[l][None, :], norm_tile=8, norm_g=ev_kv_norm_g[i][None, :],
                 out_dtype=BF16, tile_scale={0: a_scale, 1: a_scale})
    bcq_src = (0, ((6, 0, TN), (7, 0, TN)))
    head_src = lambda t0: tuple((t0 + t, gq, 4 * t + gq) for t in range(2) for gq in range(4))
    proj_p, st_ak, st_av, st_ckv, st_kr, bcq_p = _projection(
        *x["ctx"], N_SEG_P, D_MODEL, 0, D_MODEL, w_in, mt=mt[:N_SEG_P], name="even_in_proj_ctx",
        states=((N_HEADS, head_src(2)), (N_HEADS, head_src(4)), (0, ((8, 0, B_KV_RANK),)), (0, ((7, 256, B_ROPE_DIM),)),
                bcq_src), **in_kw)
    proj_l, bcq_l = _projection(*x["lat"], DEC_BATCH, D_MODEL, 0, D_MODEL, w_in, mt=mt[N_SEG_P:], rope=rope64,
                                sh=A_QK_DIM // 4, rope_groups={0: all_g, 1: all_g, 2: all_g, 3: all_g, 7: (2, 3)},
                                states=(bcq_src,), name="even_in_proj_lat", **in_kw)
    wq = ev_w_uq[i].reshape(B_Q_RANK, N_HEADS, HEAD_DIM + B_ROPE_DIM)
    w_uq = jnp.concatenate([wq[:, :, :HEAD_DIM].reshape(B_Q_RANK, -1), wq[:, :, HEAD_DIM:].reshape(B_Q_RANK, -1)],
                           axis=1).astype(BF16)
    mla_scale = (HEAD_DIM + B_ROPE_DIM) ** -0.5
    q_kw = dict(prologue="rmsnorm", g=ev_q_norm_g[i][None, :], out_dtype=BF16,
                tile_scale={t: mla_scale * LOG2E for t in range(3)})
    bq_p = _projection(bcq_p, 0, N_SEG_P, 1024, 0, B_Q_RANK, w_uq, name="mla_q_up_ctx", **q_kw)
    bq_l = _projection(bcq_l, 0, DEC_BATCH, 1024, 0, B_Q_RANK, w_uq, rope=rope64, sh=B_ROPE_DIM // 4,
                       rope_groups={2: all_g}, name="mla_q_up_lat", **q_kw)
    w_ukv = ev_w_ukv[i].astype(BF16)
    kv_kw = dict(prologue="cast", out_dtype=BF16)
    kv_p = _projection(proj_p, 0, N_SEG_P, 512, 8, B_KV_RANK, w_ukv, name="mla_kv_up_ctx", **kv_kw)
    kv_l = _projection(proj_l, 0, DEC_BATCH, 512, 8, B_KV_RANK, w_ukv, name="mla_kv_up_lat", **kv_kw)
    kv_ctx = _projection(cache_b_ckv[:, i].reshape(DEC_BATCH * PAST_LEN, B_KV_RANK), 0, 1, 512, 0, B_KV_RANK, w_ukv,
                         name="mla_kv_up_cache", **kv_kw)
    kv_ctx = kv_ctx.reshape(DEC_BATCH, PAST_LEN, 2 * N_HEADS * HEAD_DIM)
    krc = cache_b_krope[:, i]
    zc = jnp.zeros_like(krc)
    kr_ctx = jnp.concatenate([krc, zc, zc, krc], axis=-1)
    ak_ctx = cache_a_k[:, i].reshape(DEC_BATCH, PAST_LEN, N_HEADS * HEAD_DIM)
    av_ctx = cache_a_v[:, i].reshape(DEC_BATCH, PAST_LEN, N_HEADS * HEAD_DIM)
    lam_init = 0.8 - 0.6 * math.exp(-0.3 * l)
    o = jnp.zeros((N_TOK, D_MODEL), BF16)
    for latent, proj, b_q, kv in ((False, proj_p, bq_p, kv_p), (True, proj_l, bq_l, kv_l)):
        tag = "lat" if latent else "ctx"
        o = _attention(kind="diff", latent=latent, q_parts=[(proj, lambda h: h)], ko_parts=[(proj, lambda h: 8 + h)],
                       vo=(proj, lambda h: 16 + h), kc_parts=[(ak_ctx, lambda h: h)], vc=(av_ctx, lambda h: h),
                       lam=ev_lambda[i], subln=ev_subln_g[i][None, :], o_arr=o, o_blk=lambda h: h,
                       lam_init=lam_init, name="diff_attn_" + tag)
        o = _attention(kind="mla", latent=latent, q_parts=[(b_q, lambda h: h), (b_q, lambda h: 8 + h // 2)],
                       ko_parts=[(kv, lambda h: 2 * h), (proj, lambda h: 30 + h % 2)], vo=(kv, lambda h: 2 * h + 1),
                       kc_parts=[(kv_ctx, lambda h: 2 * h), (kr_ctx, lambda h: h % 2)], vc=(kv_ctx, lambda h: 2 * h + 1),
                       o_arr=o, o_blk=lambda h: 8 + h, name="mla_attn_" + tag)
    states = (st_ak.reshape(BATCH, SEQ, N_HEADS, HEAD_DIM), st_av.reshape(BATCH, SEQ, N_HEADS, HEAD_DIM),
              st_ckv.reshape(BATCH, SEQ, B_KV_RANK), st_kr.reshape(BATCH, SEQ, B_ROPE_DIM))
    return (o, ev_w_out[i].astype(BF16)), states


def _odd_layer(x, mt, l, i, rope128, cache_c_k, cache_c_v, cache_d_k, cache_d_v, norm_mix_g, od_w_in, od_rpb,
               od_sink, od_w_out):
    w_in = od_w_in[i].astype(BF16)
    all_g = (0, 1, 2, 3)
    q_scale = HEAD_DIM ** -0.5 * LOG2E
    in_kw = dict(prologue="modulate", g=norm_mix_g[l][None, :], out_dtype=BF16,
                 tile_scale={t: q_scale for t in (0, 1, 6, 7)})
    head_src = lambda t0: tuple((t0 + t, gq, 4 * t + gq) for t in range(2) for gq in range(4))
    proj_p, st_ck, st_cv, st_dk, st_dv = _projection(
        *x["ctx"], N_SEG_P, D_MODEL, 0, D_MODEL, w_in, mt=mt[:N_SEG_P], name="odd_in_proj_ctx",
        states=((N_HEADS, head_src(2)), (N_HEADS, head_src(4)), (D_KV_HEADS, ((8, 0, 0), (8, 1, 1))),
                (D_KV_HEADS, ((8, 2, 0), (8, 3, 1)))), **in_kw)
    proj_l = _projection(*x["lat"], DEC_BATCH, D_MODEL, 0, D_MODEL, w_in, mt=mt[N_SEG_P:], rope=rope128,
                         sh=HEAD_DIM // 4, rope_groups={6: all_g, 7: all_g, 8: (0, 1)}, name="odd_in_proj_lat", **in_kw)
    ck_ctx = cache_c_k[:, i].reshape(DEC_BATCH, PAST_LEN, N_HEADS * HEAD_DIM)
    cv_ctx = cache_c_v[:, i].reshape(DEC_BATCH, PAST_LEN, N_HEADS * HEAD_DIM)
    dk_ctx = cache_d_k[:, i].reshape(DEC_BATCH, PAST_LEN, D_KV_HEADS * HEAD_DIM)
    dv_ctx = cache_d_v[:, i].reshape(DEC_BATCH, PAST_LEN, D_KV_HEADS * HEAD_DIM)
    bias = _neighbourhood_bias(od_rpb[i])
    sink = od_sink[i].astype(F32)
    o = jnp.zeros((N_TOK, D_MODEL), BF16)
    for latent, proj in ((False, proj_p), (True, proj_l)):
        tag = "lat" if latent else "ctx"
        o = _attention(kind="plain", latent=latent, q_parts=[(proj, lambda h: h)], ko_parts=[(proj, lambda h: 8 + h)],
                       vo=(proj, lambda h: 16 + h), kc_parts=[(ck_ctx, lambda h: h)], vc=(cv_ctx, lambda h: h),
                       bias=bias if latent else None, o_arr=o, o_blk=lambda h: h,
                       name="nbr_attn_" + tag)
        o = _attention(kind="plain", latent=latent, q_parts=[(proj, lambda h: 24 + h)],
                       ko_parts=[(proj, lambda h: 32 + h // D_GROUP)], vo=(proj, lambda h: 34 + h // D_GROUP),
                       kc_parts=[(dk_ctx, lambda h: h // D_GROUP)], vc=(dv_ctx, lambda h: h // D_GROUP), sink=sink,
                       o_arr=o, o_blk=lambda h: 8 + h, has_band=latent, tq_lat=256,
                       name="win_attn_" + tag)
    states = (st_ck.reshape(BATCH, SEQ, N_HEADS, HEAD_DIM), st_cv.reshape(BATCH, SEQ, N_HEADS, HEAD_DIM),
              st_dk.reshape(BATCH, SEQ, D_KV_HEADS, HEAD_DIM), st_dv.reshape(BATCH, SEQ, D_KV_HEADS, HEAD_DIM))
    return (o, od_w_out[i].astype(BF16)), states


def _moe_layer(mix, x_parts, mt, g, router_w, router_b, layer, w_gate, w_up, w_down, final_g, final):
    x, h2, eidx, gates = _out_projection_router(*mix, x_parts, mt, g, router_w, router_b)
    src_tok, te, n_active, pos = _route_plan(eidx)
    ys = _moe_experts(h2, src_tok, te, n_active, layer, w_gate, w_up, w_down)
    pos2 = pos.reshape(2, N_TOK)
    gates_t = gates.T
    outs = []
    for row0, n_rows in ((0, N_PROMPT), (N_PROMPT, N_TOK - N_PROMPT)):
        ysel = ys.at[pos2[:, row0:row0 + n_rows].reshape(-1)].get(mode="promise_in_bounds")
        outs.append(_combine(x, row0, n_rows, ysel.reshape(2, n_rows, D_MODEL // 2), gates_t, mt, final_g, final))
    return outs


def kernel(x_prompt, x_sample, cache_a_k, cache_a_v, cache_b_ckv, cache_b_krope, cache_c_k, cache_c_v, cache_d_k, cache_d_v, c, c_ctx, w_ada, b_ada, norm_mix_g, norm_ffn_g, ev_w_in, ev_lambda, ev_subln_g, ev_q_norm_g, ev_kv_norm_g, ev_w_uq, ev_w_ukv, ev_w_out, od_w_in, od_rpb, od_sink, od_w_out, router_w, router_b, moe_w_gate, moe_w_up, moe_w_down, final_g):
    xp = x_prompt.reshape(N_PROMPT, D_MODEL)
    xl = x_sample.reshape(DEC_BATCH * DEC_SEQ, D_MODEL)
    x = dict(ctx=(xp, 0), lat=(xl, 0), parts=[xp, xl])
    cond8 = jnp.concatenate([c_ctx[None, :], c, jnp.zeros((3, D_MODEL), F32)], axis=0)
    mod = _adaln(cond8, w_ada, b_ada)
    seg_row = np.array([0] * N_SEG_P + [1 + b for b in range(DEC_BATCH)])
    mt_all = mod[:, seg_row].reshape(DEPTH, N_SEG, 6, D_MODEL)
    rope64 = _rope_tables(A_QK_DIM)
    rope128 = _rope_tables(HEAD_DIM)
    even_states, odd_states = [], []
    for l in range(DEPTH):
        i = l // 2
        mt = mt_all[l]
        if l % 2 == 0:
            mix, st = _even_layer(x, mt, l, i, rope64, cache_a_k, cache_a_v, cache_b_ckv, cache_b_krope, norm_mix_g,
                                ev_w_in, ev_lambda, ev_subln_g, ev_q_norm_g, ev_kv_norm_g, ev_w_uq, ev_w_ukv, ev_w_out)
            even_states.append(st)
        else:
            mix, st = _odd_layer(x, mt, l, i, rope128, cache_c_k, cache_c_v, cache_d_k, cache_d_v, norm_mix_g,
                               od_w_in, od_rpb, od_sink, od_w_out)
            odd_states.append(st)
        x = _moe_layer(mix, x["parts"], mt, norm_ffn_g[l][None, :], router_w, router_b, l, moe_w_gate, moe_w_up, moe_w_down,
                       final_g[None, :], final=(l == DEPTH - 1))
        if l < DEPTH - 1:
            x = dict(ctx=(x[0], 0), lat=(x[1], 0), parts=list(x))
    y_prompt = x[0].reshape(BATCH, SEQ, D_MODEL)
    y_sample = x[1].reshape(DEC_BATCH, DEC_SEQ, D_MODEL)
    new_even = tuple(jnp.stack([st[k] for st in even_states], axis=1) for k in range(4))
    new_odd = tuple(jnp.stack([st[k] for st in odd_states], axis=1) for k in range(4))
    return (y_prompt, y_sample) + new_even + new_odd
```

```python
import functools
import math

import numpy as np
import jax
import jax.numpy as jnp
from jax import lax
from jax.experimental import pallas as pl
from jax.experimental.pallas import tpu as pltpu

D_MODEL = 2048
BATCH = 32
SEQ = 256
DEPTH = 2
DEC_BATCH = 4
DEC_SEQ = 1024
PAST_LEN = 256
GRID_W = 64
HEAD_DIM = 128
N_HEADS = 8
A_QK_DIM = 64
B_Q_RANK = 768
B_KV_RANK = 512
B_ROPE_DIM = 64
NA_ROWS = 8
NA_COLS = 16
D_KV_HEADS = 2
D_GROUP = 4
D_WINDOW = 128
N_EXPERTS = 16
N_GROUPS = 4
D_EXPERT = 1024
ROPE_THETA = 10000.0
EPS = 1e-6
NEG = -1e30
LOG2E = 1.4426950408889634

SEG = 1024
N_PROMPT = BATCH * SEQ
N_TOK = N_PROMPT + DEC_BATCH * DEC_SEQ
N_SEG = N_TOK // SEG
N_SEG_P = N_PROMPT // SEG
PROJ_W = 4608
TN = 512
LANES = 128
MOE_TM = 256
MOE_ROWS = 2 * N_TOK + N_EXPERTS * MOE_TM
MOE_TILES = MOE_ROWS // MOE_TM

F32 = jnp.float32
BF16 = jnp.bfloat16


def _cparams(sem, vmem_mb):
    return pltpu.CompilerParams(dimension_semantics=sem, vmem_limit_bytes=vmem_mb * 1024 * 1024)


def _dot(a, b):
    return jnp.dot(a, b, preferred_element_type=F32)


def _dot_nt(a, b):
    return lax.dot_general(a, b, (((1,), (1,)), ((), ())), preferred_element_type=F32)


def _sigmoid(x):
    return 1.0 / (1.0 + jnp.exp(-x))


def _pack_bf16_pair(x):
    c = x.shape[1] // 2
    bits = pltpu.bitcast(x.astype(BF16).astype(F32), jnp.int32)
    return bits[:, :c] | lax.shift_right_logical(bits[:, c:], 16)


def _unpack_bf16_pair(w):
    hi = pltpu.bitcast(w & jnp.int32(-65536), F32)
    lo = pltpu.bitcast(lax.shift_left(w, 16), F32)
    return hi, lo


def _adaln_kernel(c_ref, w_ref, b_ref, o_ref):
    c = c_ref[...]
    a = (c * _sigmoid(c)).astype(BF16)
    o_ref[...] = _dot(a, w_ref[...].astype(BF16)) + b_ref[...]


def _adaln(cond8, w_ada, b_ada):
    tn = 1024
    n = 6 * D_MODEL
    return pl.pallas_call(
        _adaln_kernel,
        grid=(DEPTH, n // tn),
        in_specs=[
            pl.BlockSpec((8, D_MODEL), lambda l, j: (0, 0)),
            pl.BlockSpec((None, D_MODEL, tn), lambda l, j: (l, 0, j)),
            pl.BlockSpec((None, 1, tn), lambda l, j: (l, 0, j)),
        ],
        out_specs=pl.BlockSpec((None, 8, tn), lambda l, j: (l, 0, j)),
        out_shape=jax.ShapeDtypeStruct((DEPTH, 8, n), F32),
        compiler_params=_cparams(("arbitrary", "arbitrary"), 40),
        name="adaln",
    )(cond8, w_ada, b_ada.reshape(DEPTH, 1, n))


def _rope(a, cos, sp, sm, sh):
    return a * cos + pltpu.roll(a, sh, 1) * sp + pltpu.roll(a, LANES - sh, 1) * sm


def _proj_kernel(*refs, prologue, k_in, sh, n_tiles, rope_groups, norm_tile, states, tile_scale):
    x_ref, mt_ref, g_ref, w_ref, cos_ref, sp_ref, sm_ref, ng_ref, o_ref = refs[:9]
    st_refs = refs[9:-1]
    xn_ref = refs[-1]
    j = pl.program_id(1)

    @pl.when(j == 0)
    def _():
        x = x_ref[:, :k_in]
        if prologue == "cast":
            xn_ref[...] = x.astype(BF16)
        else:
            r = lax.rsqrt(jnp.mean(x * x, axis=-1, keepdims=True) + EPS)
            if prologue == "modulate":
                y = x * r * (g_ref[...] * (1.0 + mt_ref[0, 1:2, :])) + mt_ref[0, 0:1, :]
            else:
                y = x * r * g_ref[...]
            xn_ref[...] = y.astype(BF16)

    acc = _dot(xn_ref[...], w_ref[...])
    n_grp = acc.shape[1] // LANES

    def treatment(t):
        acts = []
        for k, (n_heads, src) in enumerate(states):
            dest = 0
            for tt, a, b in src:
                if tt == t:
                    acts.append((k, a, b, dest))
                dest += 0 if n_heads else b
        return (tuple(rope_groups.get(t, ())), t == norm_tile, tuple(acts), float(tile_scale.get(t, 1.0)))

    branches = {}
    for t in range(n_tiles):
        branches.setdefault(treatment(t), []).append(t)

    o_ref[...] = acc.astype(o_ref.dtype)
    reread = o_ref.dtype == F32

    for (rg, is_norm, acts, sc), tiles in branches.items():
        if not rg and not is_norm and not acts and sc == 1.0:
            continue
        cond = j == tiles[0]
        for t in tiles[1:]:
            cond = jnp.logical_or(cond, j == t)

        @pl.when(cond)
        def _(rg=rg, is_norm=is_norm, acts=acts, sc=sc):
            vals = [(o_ref if reread else acc)[:, g * LANES:(g + 1) * LANES] for g in range(n_grp)]
            if sc != 1.0:
                vals = [v * sc for v in vals]
            if is_norm:
                ssq = None
                for v in vals:
                    s = jnp.sum(v * v, axis=-1, keepdims=True)
                    ssq = s if ssq is None else ssq + s
                scale = lax.rsqrt(ssq * (1.0 / (n_grp * LANES)) + EPS)
                ng = ng_ref[...]
                vals = [v * scale * ng[:, g * LANES:(g + 1) * LANES] for g, v in enumerate(vals)]
            elif rg:
                cos, sp, sm = cos_ref[...], sp_ref[...], sm_ref[...]
                vals = [_rope(v, cos, sp, sm, sh) if g in rg else v for g, v in enumerate(vals)]
            if is_norm or rg or sc != 1.0:
                for g, v in enumerate(vals):
                    if is_norm or sc != 1.0 or g in rg:
                        o_ref[:, g * LANES:(g + 1) * LANES] = v.astype(o_ref.dtype)
            for k, a, b, dest in acts:
                n_heads = states[k][0]
                if n_heads:
                    st_refs[k][pl.ds(b, SEG, stride=n_heads), :] = vals[a]
                elif b % LANES == 0:
                    for q in range(b // LANES):
                        st_refs[k][:, dest + q * LANES:dest + (q + 1) * LANES] = vals[a // LANES + q]
                else:
                    st_refs[k][...] = vals[a // LANES][:, a % LANES:a % LANES + b]


def _projection(x, row_off, n_seg, x_blk_w, x_blk_idx, k_in, w, *, prologue, mt=None, g=None, rope=None, sh=16,
                rope_groups=None, norm_tile=None, norm_g=None, states=(), out_dtype=F32, tile_scale=None, name):
    n = w.shape[1]
    m = n_seg * SEG
    assert n % TN == 0 and w.shape[0] == k_in
    if mt is None:
        mt = jnp.zeros((n_seg, 6, LANES), F32)
    if g is None:
        g = jnp.ones((1, k_in), F32)
    if rope is None:
        rope = tuple(jnp.zeros((8, LANES), F32) for _ in range(3))
    if norm_g is None:
        norm_g = jnp.ones((1, TN), F32)
    kern = functools.partial(_proj_kernel, prologue=prologue, k_in=k_in, sh=sh, n_tiles=n // TN,
                             rope_groups=dict(rope_groups or {}), norm_tile=norm_tile, states=tuple(states),
                             tile_scale=dict(tile_scale or {}))
    full2 = lambda i, j: (0, 0)
    out_shape = [jax.ShapeDtypeStruct((m, n), out_dtype)]
    out_specs = [pl.BlockSpec((SEG, TN), lambda i, j: (i, j))]
    for n_heads, src in states:
        if n_heads:
            out_shape.append(jax.ShapeDtypeStruct((m * n_heads, LANES), F32))
            out_specs.append(pl.BlockSpec((SEG * n_heads, LANES), lambda i, j: (i, 0)))
        else:
            width = sum(b for _, _, b in src)
            out_shape.append(jax.ShapeDtypeStruct((m, width), F32))
            out_specs.append(pl.BlockSpec((SEG, width), lambda i, j: (i, 0)))
    x_kw = dict(pipeline_mode=pl.Buffered(1)) if states else {}
    res = pl.pallas_call(
        kern,
        grid=(n_seg, n // TN),
        in_specs=[
            pl.BlockSpec((SEG, x_blk_w), lambda i, j: (i + row_off, x_blk_idx), **x_kw),
            pl.BlockSpec((1, 6, mt.shape[2]), lambda i, j: (i, 0, 0)),
            pl.BlockSpec((1, k_in), full2),
            pl.BlockSpec((k_in, TN), lambda i, j: (0, j)),
            pl.BlockSpec(rope[0].shape, full2),
            pl.BlockSpec(rope[1].shape, full2),
            pl.BlockSpec(rope[2].shape, full2),
            pl.BlockSpec((1, TN), full2),
        ],
        out_specs=out_specs,
        out_shape=out_shape,
        scratch_shapes=[pltpu.VMEM((SEG, k_in), BF16)],
        compiler_params=_cparams(("arbitrary", "arbitrary"), 56 if states else 48),
        name=name,
    )(x, mt, g, w, *rope, norm_g)
    return res if states else res[0]


def _attn_kernel(*refs, kind, n_parts, has_ctx, has_bias, has_band, has_sink, nb, s_len, tq, lam_init, own_win):
    it = iter(refs)
    q_refs = [next(it) for _ in range(n_parts)]
    ko_refs = [next(it) for _ in range(n_parts)]
    vo_ref = next(it)
    kc_refs, vc_ref = [], None
    if has_ctx:
        kc_refs = [next(it) for _ in range(n_parts)]
        vc_ref = next(it)
    bias_ref = next(it) if has_bias else None
    lam_ref = sg_ref = None
    if kind == "diff":
        lam_ref = next(it)
        sg_ref = next(it)
    sink_ref = next(it) if has_sink else None
    o_ref = next(it)

    sink2 = sink_ref[pl.program_id(1)] * LOG2E if has_sink else None
    lane = lax.broadcasted_iota(jnp.int32, (1, LANES), 1)

    def softmax_pv(s_list, v_list):
        m = None
        for s in s_list:
            mm = jnp.max(s, axis=-1, keepdims=True)
            m = mm if m is None else jnp.maximum(m, mm)
        if sink2 is not None:
            m = jnp.maximum(m, sink2)
        l = jnp.exp2(sink2 - m) if sink2 is not None else None
        o = None
        for s, v in zip(s_list, v_list):
            e = jnp.exp2(s - m)
            ss = jnp.sum(e, axis=-1, keepdims=True)
            l = ss if l is None else l + ss
            pv = _dot(e.astype(BF16), v)
            o = pv if o is None else o + pv
        return o, l

    for bi in range(nb):
        qrows = pl.ds(bi * tq, tq)
        q0 = pl.program_id(2) * tq
        if own_win is None:
            krows, k0 = pl.ds(bi * s_len, s_len), 0
            n_own = s_len
        else:
            k0 = pl.multiple_of(jnp.clip(q0 - D_WINDOW, 0, s_len - own_win), LANES)
            krows = pl.ds(k0, own_win)
            n_own = own_win
        k_srcs, v_srcs, is_own = [], [], []
        if has_ctx:
            k_srcs.append([r[...].astype(BF16) for r in kc_refs])
            v_srcs.append(vc_ref[...].astype(BF16))
            is_own.append(False)
        k_srcs.append([r[krows, :].astype(BF16) for r in ko_refs])
        v_srcs.append(vo_ref[krows, :].astype(BF16))
        is_own.append(True)

        def mask_own(s):
            if has_bias:
                s = s + bias_ref[...]
            if has_band:
                qpos = q0 + lax.broadcasted_iota(jnp.int32, (tq, 1), 0)
                kpos = k0 + lax.broadcasted_iota(jnp.int32, (1, n_own), 1)
                s = jnp.where(jnp.abs(kpos - qpos) <= D_WINDOW, s, NEG)
            return s

        if kind == "diff":
            q = q_refs[0][qrows, :]
            zero = jnp.zeros_like(q)
            q1 = jnp.where(lane < A_QK_DIM, q, zero)
            q2 = jnp.where(lane < A_QK_DIM, zero, q)
            o1, l1 = softmax_pv([_dot_nt(q1, ks[0]) for ks in k_srcs], v_srcs)
            o2, l2 = softmax_pv([_dot_nt(q2, ks[0]) for ks in k_srcs], v_srcs)
            lv = lam_ref[...]
            lam = (jnp.exp(jnp.sum(lv[0:1] * lv[1:2], axis=-1, keepdims=True))
                   - jnp.exp(jnp.sum(lv[2:3] * lv[3:4], axis=-1, keepdims=True)) + lam_init)
            o = o1 * (1.0 / l1) - o2 * (lam / l2)
            o = o * lax.rsqrt(jnp.mean(o * o, axis=-1, keepdims=True) + EPS) * sg_ref[...] * (1.0 - lam_init)
        else:
            qs = [r[qrows, :] for r in q_refs]
            s_list = []
            for ks, own in zip(k_srcs, is_own):
                s = None
                for qp, kp in zip(qs, ks):
                    d = _dot_nt(qp, kp)
                    s = d if s is None else s + d
                s_list.append(mask_own(s) if own else s)
            o, l = softmax_pv(s_list, v_srcs)
            o = o * (1.0 / l)
        o_ref[qrows, :] = o.astype(o_ref.dtype)


def _attention(*, kind, latent, q_parts, ko_parts, vo, kc_parts=(), vc=None, bias=None, lam=None, subln=None,
               sink=None, o_arr, o_blk, lam_init=0.0, has_band=False, tq_lat=512, name):
    n_parts = len(q_parts)
    has_ctx = latent
    if latent:
        tq, s_len, nb = tq_lat, DEC_SEQ, 1
        grid = (DEC_BATCH, N_HEADS, DEC_SEQ // tq)
        qpb = DEC_SEQ // tq
        q_row = lambda b, h, t: b * qpb + t
        o_row = lambda b, h, t: N_PROMPT // tq + b * qpb + t
        k_row = lambda b, h, t: b
        qblk, kblk = tq, SEG
    else:
        qblk = kblk = 4 * SEG
        tq, s_len, nb = SEQ, SEQ, qblk // SEQ
        grid = (N_PROMPT // qblk, N_HEADS, 1)
        q_row = o_row = k_row = lambda b, h, t: b
    args, specs = [], []

    def add(arr, shape, imap, **kw):
        args.append(arr)
        specs.append(pl.BlockSpec(shape, imap, **kw))

    for arr, f in q_parts:
        add(arr, (qblk, LANES), lambda b, h, t, f=f: (q_row(b, h, t), f(h)))
    for arr, f in list(ko_parts) + [vo]:
        add(arr, (kblk, LANES), lambda b, h, t, f=f: (k_row(b, h, t), f(h)))
    if has_ctx:
        for arr, f in list(kc_parts) + [vc]:
            add(arr, (None, PAST_LEN, LANES), lambda b, h, t, f=f: (b, 0, f(h)))
    if bias is not None:
        add(bias, (None, tq, DEC_SEQ), lambda b, h, t: (h, t, 0))
    if kind == "diff":
        add(lam, lam.shape, lambda b, h, t: (0, 0))
        add(subln, subln.shape, lambda b, h, t: (0, 0))
    if sink is not None:
        args.append(sink)
        specs.append(pl.BlockSpec(memory_space=pltpu.SMEM))
    n_in = len(args)
    args.append(o_arr)
    specs.append(pl.BlockSpec(memory_space=pl.ANY))
    kern = functools.partial(_attn_kernel_aliased, kind=kind, n_parts=n_parts, has_ctx=has_ctx,
                             has_bias=bias is not None, has_band=has_band, has_sink=sink is not None,
                             nb=nb, s_len=s_len, tq=tq, lam_init=lam_init,
                             own_win=(tq + 2 * D_WINDOW) if has_band else None)
    return pl.pallas_call(
        kern,
        grid=grid,
        in_specs=specs,
        out_specs=pl.BlockSpec((qblk, LANES), lambda b, h, t: (o_row(b, h, t), o_blk(h))),
        out_shape=jax.ShapeDtypeStruct(o_arr.shape, o_arr.dtype),
        input_output_aliases={n_in: 0},
        compiler_params=_cparams(("arbitrary", "arbitrary", "arbitrary"), 48),
        name=name,
    )(*args)


def _attn_kernel_aliased(*refs, **kw):
    _attn_kernel(*refs[:-2], refs[-1], **kw)


def _ffn_norm_route(x, mt_ref, g_ref, w_ref, rb_ref, h_ref, e_ref, gt_ref):
    r = lax.rsqrt(jnp.mean(x * x, axis=-1, keepdims=True) + EPS)
    h = x * r * (g_ref[...] * (1.0 + mt_ref[0, 4:5, :])) + mt_ref[0, 3:4, :]
    hh = h.astype(BF16)
    h_ref[...] = _pack_bf16_pair(hh)
    logits = _dot_nt(w_ref[...], hh)
    scores = _sigmoid(logits)
    sel = scores + rb_ref[...]
    per = N_EXPERTS // N_GROUPS
    sc = [scores[e:e + 1, :] for e in range(N_EXPERTS)]
    sl = [sel[e:e + 1, :] for e in range(N_EXPERTS)]
    best_g, best_v = None, None
    for gi in range(N_GROUPS):
        a, b, c, d = sl[gi * per:(gi + 1) * per]
        hi1, lo1, hi2, lo2 = jnp.maximum(a, b), jnp.minimum(a, b), jnp.maximum(c, d), jnp.minimum(c, d)
        gs = jnp.maximum(hi1, hi2) + jnp.maximum(jnp.minimum(hi1, hi2), jnp.maximum(lo1, lo2))
        if gi == 0:
            best_g, best_v = jnp.zeros_like(gs, dtype=jnp.int32), gs
        else:
            better = gs > best_v
            best_g = jnp.where(better, gi, best_g)
            best_v = jnp.where(better, gs, best_v)
    masked = [jnp.where(best_g == (e // per), sl[e], NEG) for e in range(N_EXPERTS)]
    i1, v1 = jnp.zeros_like(best_g), masked[0]
    for e in range(1, N_EXPERTS):
        better = masked[e] > v1
        i1 = jnp.where(better, e, i1)
        v1 = jnp.where(better, masked[e], v1)
    i2, v2 = None, None
    for e in range(N_EXPERTS):
        cand = jnp.where(i1 == e, -2e30, masked[e])
        if e == 0:
            i2, v2 = jnp.zeros_like(best_g), cand
        else:
            better = cand > v2
            i2 = jnp.where(better, e, i2)
            v2 = jnp.where(better, cand, v2)
    g1 = jnp.zeros_like(v1)
    g2 = jnp.zeros_like(v1)
    for e in range(N_EXPERTS):
        g1 = jnp.where(i1 == e, sc[e], g1)
        g2 = jnp.where(i2 == e, sc[e], g2)
    tot = g1 + g2
    rows = i1.shape[1]
    e_ref[...] = jnp.concatenate([i1, i2, jnp.zeros((6, rows), jnp.int32)], axis=0)
    gt_ref[...] = jnp.concatenate([g1 / tot, g2 / tot, jnp.zeros((6, rows), F32)], axis=0)


def _outproj_router_kernel(o_ref, w_ref, mt_ref, g_ref, rw_ref, rb_ref, *rest, blk_ranges):
    x_refs = rest[:-4]
    y_ref, h_ref, e_ref, gt_ref = rest[-4:]
    upd = mt_ref[0, 2:3, :] * _dot(o_ref[...], w_ref[...])
    if len(x_refs) == 1:
        y_ref[...] = x_refs[0][...] + upd
    else:
        i = pl.program_id(0)
        for x_ref, (lo, hi) in zip(x_refs, blk_ranges):
            @pl.when(jnp.logical_and(i >= lo, i < hi))
            def _(x_ref=x_ref):
                y_ref[...] = x_ref[...] + upd
    _ffn_norm_route(y_ref[...], mt_ref, g_ref, rw_ref, rb_ref, h_ref, e_ref, gt_ref)


def _out_projection_router(o, w, x_parts, mt, g, router_w, router_b):
    tm = 256
    blk_ranges, lo = [], 0
    for xp in x_parts:
        blk_ranges.append((lo, lo + xp.shape[0] // tm))
        lo = blk_ranges[-1][1]
    assert lo == N_TOK // tm
    x_specs = [pl.BlockSpec((tm, D_MODEL), lambda i, lo=lo, hi=hi: (jnp.clip(i - lo, 0, hi - lo - 1), 0))
               for lo, hi in blk_ranges]
    const = lambda i: (0, 0)
    return pl.pallas_call(
        functools.partial(_outproj_router_kernel, blk_ranges=tuple(blk_ranges)),
        grid=(N_TOK // tm,),
        in_specs=[
            pl.BlockSpec((tm, D_MODEL), lambda i: (i, 0)),
            pl.BlockSpec((D_MODEL, D_MODEL), const, pipeline_mode=pl.Buffered(1)),
            pl.BlockSpec((1, 6, D_MODEL), lambda i: (i // (SEG // tm), 0, 0)),
            pl.BlockSpec((1, D_MODEL), const),
            pl.BlockSpec((N_EXPERTS, D_MODEL), const),
            pl.BlockSpec((N_EXPERTS, 1), const),
        ] + x_specs,
        out_specs=[
            pl.BlockSpec((tm, D_MODEL), lambda i: (i, 0)),
            pl.BlockSpec((tm, D_MODEL // 2), lambda i: (i, 0)),
            pl.BlockSpec((8, tm), lambda i: (0, i)),
            pl.BlockSpec((8, tm), lambda i: (0, i)),
        ],
        out_shape=[
            jax.ShapeDtypeStruct((N_TOK, D_MODEL), F32),
            jax.ShapeDtypeStruct((N_TOK, D_MODEL // 2), jnp.int32),
            jax.ShapeDtypeStruct((8, N_TOK), jnp.int32),
            jax.ShapeDtypeStruct((8, N_TOK), F32),
        ],
        compiler_params=_cparams(("arbitrary",), 48),
        name="out_proj_router",
    )(o, w, mt, g, router_w.T.astype(BF16), router_b.reshape(N_EXPERTS, 1), *x_parts)


def _expert_weights(te_ref, first_ref, slot_ref, nxt_ref, i, t, layer, w_hbm, stage_ref, sem_ref, w_bf_refs):
    def copies(e, s):
        return [pltpu.make_async_copy(w.at[layer, e], stage_ref.at[s, k], sem_ref.at[s, k])
                for k, w in enumerate(w_hbm)]

    s = slot_ref[t]

    @pl.when(i == 0)
    def _():
        for c in copies(te_ref[t], s):
            c.start()

    @pl.when(first_ref[t] == 1)
    def _():
        for c in copies(te_ref[t], s):
            c.wait()

        @pl.when(nxt_ref[t] >= 0)
        def _():
            for c in copies(nxt_ref[t], 1 - s):
                c.start(priority=1)

        for k, w_bf in enumerate(w_bf_refs):
            w_bf[...] = stage_ref[s, k].astype(BF16)


def _moe_up_kernel(te_ref, na_ref, first_ref, slot_ref, nxt_ref, xs_ref, wg_hbm, wu_hbm, hid_ref,
                   stage_ref, wgb_ref, wub_ref, sem_ref, *, tile0, layer):
    i = pl.program_id(0)
    t = i + tile0
    _expert_weights(te_ref, first_ref, slot_ref, nxt_ref, i, t, layer, (wg_hbm, wu_hbm), stage_ref, sem_ref,
                    (wgb_ref, wub_ref))

    @pl.when(t < na_ref[0])
    def _():
        x_hi, x_lo = _unpack_bf16_pair(xs_ref[...])
        x_hi, x_lo = x_hi.astype(BF16), x_lo.astype(BF16)
        half = D_MODEL // 2
        g = _dot(x_hi, wgb_ref[:half, :]) + _dot(x_lo, wgb_ref[half:, :])
        u = _dot(x_hi, wub_ref[:half, :]) + _dot(x_lo, wub_ref[half:, :])
        hid_ref[...] = (g * _sigmoid(g) * u).astype(BF16)

    @pl.when(t >= na_ref[0])
    def _():
        hid_ref[...] = jnp.zeros_like(hid_ref)


def _moe_down_kernel(te_ref, na_ref, first_ref, slot_ref, nxt_ref, *rest, layer, chunk_tiles):
    ys_ref, stage_ref, wdb_ref, sem_ref = rest[-4:]
    wd_hbm = rest[-5]
    hid_refs = rest[:-5]
    t = pl.program_id(0)
    _expert_weights(te_ref, first_ref, slot_ref, nxt_ref, t, t, layer, (wd_hbm,), stage_ref, sem_ref, (wdb_ref,))

    for c, hid_ref in enumerate(hid_refs):
        @pl.when(jnp.logical_and(t < na_ref[0], t // chunk_tiles == c))
        def _(hid_ref=hid_ref):
            ys_ref[...] = _pack_bf16_pair(_dot(hid_ref[...], wdb_ref[...]))

    @pl.when(t >= na_ref[0])
    def _():
        ys_ref[...] = jnp.zeros_like(ys_ref)


MOE_CHUNKS = 2


def _moe_experts(h2, src_tok, tile_expert, n_active, layer, w_gate, w_up, w_down):
    tiles = MOE_TILES // MOE_CHUNKS
    rows = tiles * MOE_TM

    def weight_runs(call_tiles):
        t_ids = jnp.arange(MOE_TILES, dtype=jnp.int32)
        prev = jnp.concatenate([tile_expert[:1], tile_expert[:-1]])
        first = jnp.logical_or(t_ids % call_tiles == 0, tile_expert != prev)
        slot = ((jnp.cumsum(first.astype(jnp.int32)) - 1) % 2).astype(jnp.int32)
        first_at = lax.cummin(jnp.where(first, t_ids, MOE_TILES), reverse=True)
        next_at = jnp.concatenate([first_at[1:], jnp.full((1,), MOE_TILES, jnp.int32)])
        same_call = jnp.logical_and(next_at < MOE_TILES, next_at // call_tiles == t_ids // call_tiles)
        nxt = jnp.where(jnp.logical_and(first, same_call), tile_expert[jnp.minimum(next_at, MOE_TILES - 1)], -1)
        return (tile_expert, n_active, first.astype(jnp.int32), slot, nxt.astype(jnp.int32))

    row_blk = lambda i, *_: (i, 0)
    hbm = pl.BlockSpec(memory_space=pl.ANY)
    plan = weight_runs(tiles)
    hids = []
    for c in range(MOE_CHUNKS):
        xs = h2.at[src_tok[c * rows:(c + 1) * rows]].get(mode="promise_in_bounds")
        hids.append(pl.pallas_call(
            functools.partial(_moe_up_kernel, tile0=c * tiles, layer=layer),
            grid_spec=pltpu.PrefetchScalarGridSpec(
                num_scalar_prefetch=len(plan),
                grid=(tiles,),
                in_specs=[pl.BlockSpec((MOE_TM, D_MODEL // 2), row_blk), hbm, hbm],
                out_specs=pl.BlockSpec((MOE_TM, D_EXPERT), row_blk),
                scratch_shapes=[pltpu.VMEM((2, 2, D_MODEL, D_EXPERT), F32),
                                pltpu.VMEM((D_MODEL, D_EXPERT), BF16), pltpu.VMEM((D_MODEL, D_EXPERT), BF16),
                                pltpu.SemaphoreType.DMA((2, 2))],
            ),
            out_shape=jax.ShapeDtypeStruct((rows, D_EXPERT), BF16),
            compiler_params=_cparams(("arbitrary",), 52),
            name="moe_up",
        )(*plan, xs, w_gate, w_up))
    plan = weight_runs(MOE_TILES)
    hid_specs = [pl.BlockSpec((MOE_TM, D_EXPERT), lambda i, *_, c=c: (jnp.clip(i - c * tiles, 0, tiles - 1), 0))
                 for c in range(MOE_CHUNKS)]
    return pl.pallas_call(
        functools.partial(_moe_down_kernel, layer=layer, chunk_tiles=tiles),
        grid_spec=pltpu.PrefetchScalarGridSpec(
            num_scalar_prefetch=len(plan),
            grid=(MOE_TILES,),
            in_specs=hid_specs + [hbm],
            out_specs=pl.BlockSpec((MOE_TM, D_MODEL // 2), row_blk),
            scratch_shapes=[pltpu.VMEM((2, 1, D_EXPERT, D_MODEL), F32), pltpu.VMEM((D_EXPERT, D_MODEL), BF16),
                            pltpu.SemaphoreType.DMA((2, 1))],
        ),
        out_shape=jax.ShapeDtypeStruct((MOE_ROWS, D_MODEL // 2), jnp.int32),
        compiler_params=_cparams(("arbitrary",), 40),
        name="moe_down",
    )(*plan, *hids, w_down)


PLAN_R = 2 * N_TOK // LANES


def _plan_kernel(e_ref, pos_ref, meta_ref, src_ref):
    e = e_ref[...]
    r_i = lax.broadcasted_iota(jnp.int32, (LANES, LANES), 0)
    c_i = lax.broadcasted_iota(jnp.int32, (LANES, LANES), 1)
    upper = jnp.where(r_i <= c_i, 1.0, 0.0).astype(BF16)
    r_j = lax.broadcasted_iota(jnp.int32, (PLAN_R, PLAN_R), 0)
    c_j = lax.broadcasted_iota(jnp.int32, (PLAN_R, PLAN_R), 1)
    lower = jnp.where(c_j < r_j, 1.0, 0.0).astype(BF16)
    lane = lax.broadcasted_iota(jnp.int32, (1, LANES), 1)
    pos = jnp.zeros((PLAN_R, LANES), F32)
    cnt_row = jnp.zeros((1, LANES), F32)
    off_row = jnp.zeros((1, LANES), F32)
    end_row, off_list, cnt_list = [], [], []
    row_off = jnp.zeros((1, 1), F32)
    tiles_done = jnp.zeros((1, 1), F32)
    for ex in range(N_EXPERTS):
        m = jnp.where(e == ex, 1.0, 0.0)
        inc = _dot(m.astype(BF16), upper)
        tot = jnp.broadcast_to(inc[:, LANES - 1:LANES], (PLAN_R, LANES))
        before = _dot(lower, tot.astype(BF16))
        cnt = jnp.sum(inc[:, LANES - 1:LANES], axis=0, keepdims=True)
        pos = pos + m * (row_off + before + inc - 1.0)
        tiles = jnp.floor((cnt + (MOE_TM - 1)) * (1.0 / MOE_TM))
        cnt_row = jnp.where(lane == ex, cnt, cnt_row)
        off_row = jnp.where(lane == ex, row_off, off_row)
        cnt_list.append(cnt)
        off_list.append(row_off)
        tiles_done = tiles_done + tiles
        end_row.append(tiles_done)
        row_off = row_off + tiles * MOE_TM
    n_active = tiles_done
    tid = jnp.minimum(lane.astype(F32), n_active - 1.0)
    te = jnp.zeros((1, LANES), F32)
    for ex in range(N_EXPERTS):
        te = te + jnp.where(end_row[ex] <= tid, 1.0, 0.0)
    pos_ref[...] = pos.astype(jnp.int32)
    meta = jnp.concatenate([cnt_row, off_row, te, jnp.broadcast_to(n_active, (1, LANES)),
                            jnp.zeros((4, LANES), F32)], axis=0)
    meta_ref[...] = meta.astype(jnp.int32)
    pr = MOE_ROWS // LANES
    q = lax.broadcasted_iota(jnp.int32, (pr, 1), 0)
    tile = jnp.minimum((q // (MOE_TM // LANES)).astype(F32), n_active - 1.0)
    e_q = jnp.zeros((pr, 1), F32)
    for ex in range(N_EXPERTS):
        e_q = e_q + jnp.where(end_row[ex] <= tile, 1.0, 0.0)
    off_q = jnp.zeros((pr, 1), F32)
    cnt_q = jnp.zeros((pr, 1), F32)
    start_q = jnp.zeros((pr, 1), F32)
    start = jnp.zeros((1, 1), F32)
    for ex in range(N_EXPERTS):
        hit = e_q == float(ex)
        off_q = jnp.where(hit, off_list[ex], off_q)
        cnt_q = jnp.where(hit, cnt_list[ex], cnt_q)
        start_q = jnp.where(hit, start, start_q)
        start = start + cnt_list[ex]
    r = (q * LANES + lane).astype(F32)
    rank = r - off_q
    spare = r - jnp.where(r >= float(N_TOK), float(N_TOK), 0.0) - jnp.where(r >= float(2 * N_TOK), float(N_TOK), 0.0)
    src_ref[...] = jnp.where(rank < cnt_q, start_q + rank, -(spare + 1.0)).astype(jnp.int32)


def _route_plan(eidx):
    e2 = eidx[:2].reshape(PLAN_R, LANES)
    pos, meta, src = pl.pallas_call(
        _plan_kernel,
        out_shape=[jax.ShapeDtypeStruct((PLAN_R, LANES), jnp.int32), jax.ShapeDtypeStruct((8, LANES), jnp.int32),
                   jax.ShapeDtypeStruct((MOE_ROWS // LANES, LANES), jnp.int32)],
        compiler_params=pltpu.CompilerParams(vmem_limit_bytes=32 * 1024 * 1024),
        name="route_plan",
    )(e2)
    te, n_active = meta[2, :MOE_TILES], meta[3, :1]
    order = jnp.argsort(e2.reshape(-1), stable=True).astype(jnp.int32)
    src = src.reshape(-1)
    src_tok = jnp.where(src >= 0, order[jnp.maximum(src, 0)] % N_TOK, -src - 1)
    return src_tok, te, n_active, pos.reshape(-1)


def _combine_kernel(x_ref, y0_ref, y1_ref, gt_ref, mt_ref, fg_ref, o_ref, *, final):
    gt = gt_ref[...]
    g0, g1 = gt[:, 0:1], gt[:, 1:2]
    half = D_MODEL // 2
    y0 = _unpack_bf16_pair(y0_ref[...])
    y1 = _unpack_bf16_pair(y1_ref[...])
    xs = []
    for c in range(2):
        cols = slice(c * half, (c + 1) * half)
        xs.append(x_ref[:, cols] + mt_ref[0, 5:6, cols] * (y0[c] * g0 + y1[c] * g1))
    if final:
        ssq = jnp.sum(xs[0] * xs[0], axis=-1, keepdims=True) + jnp.sum(xs[1] * xs[1], axis=-1, keepdims=True)
        r = lax.rsqrt(ssq * (1.0 / D_MODEL) + EPS)
        xs = [xc * r * fg_ref[:, c * half:(c + 1) * half] for c, xc in enumerate(xs)]
    for c, xc in enumerate(xs):
        o_ref[:, c * half:(c + 1) * half] = xc


def _combine(x, row0, n_rows, ysel, gates_t, mt, final_g, final):
    tm = 256
    b0 = row0 // tm
    return pl.pallas_call(
        functools.partial(_combine_kernel, final=final),
        grid=(n_rows // tm,),
        in_specs=[
            pl.BlockSpec((tm, D_MODEL), lambda i: (i + b0, 0)),
            pl.BlockSpec((None, tm, D_MODEL // 2), lambda i: (0, i, 0)),
            pl.BlockSpec((None, tm, D_MODEL // 2), lambda i: (1, i, 0)),
            pl.BlockSpec((tm, 8), lambda i: (i + b0, 0)),
            pl.BlockSpec((1, 6, D_MODEL), lambda i: ((i + b0) // (SEG // tm), 0, 0)),
            pl.BlockSpec((1, D_MODEL), lambda i: (0, 0)),
        ],
        out_specs=pl.BlockSpec((tm, D_MODEL), lambda i: (i, 0)),
        out_shape=jax.ShapeDtypeStruct((n_rows, D_MODEL), F32),
        compiler_params=_cparams(("arbitrary",), 40),
        name="moe_combine",
    )(x, ysel, ysel, gates_t, mt, final_g)


def _rope_tables(dim):
    half = dim // 2
    inv = ROPE_THETA ** (-jnp.arange(0, half, 2, dtype=F32) / half)
    t = jnp.arange(DEC_SEQ)
    ang_r = (t // GRID_W).astype(F32)[:, None] * inv[None, :]
    ang_c = (t % GRID_W).astype(F32)[:, None] * inv[None, :]
    ang = jnp.concatenate([ang_r, ang_r, ang_c, ang_c], axis=-1)
    cos, sin = jnp.cos(ang), jnp.sin(ang)
    reps = LANES // dim
    cos = jnp.tile(cos, (1, reps))
    sin = jnp.tile(sin, (1, reps))
    sh = dim // 4
    second = (np.arange(LANES) % (2 * sh)) >= sh
    sp = jnp.where(second[None, :], sin, 0.0)
    sm = jnp.where(second[None, :], 0.0, -sin)
    return cos, sp, sm


def _neighbourhood_bias(rpb):
    rows = DEC_SEQ // GRID_W
    kh = min(NA_ROWS, rows)
    r = np.arange(rows)
    r0 = np.clip(r - kh // 2, 0, rows - kh)
    kr = np.arange(rows)
    row_ok = (kr[None, :] >= r0[:, None]) & (kr[None, :] < r0[:, None] + kh)
    c = np.arange(GRID_W)
    ws = np.clip(c - NA_COLS // 2, 0, GRID_W - NA_COLS)
    kc = np.arange(GRID_W)
    col_ok = (kc[None, :] >= ws[:, None]) & (kc[None, :] < ws[:, None] + NA_COLS)
    dc_idx = np.clip(kc[None, :] - c[:, None], -(NA_COLS - 1), NA_COLS - 1) + NA_COLS - 1
    sel_c = ((np.arange(2 * NA_COLS - 1)[:, None, None] == dc_idx[None]) & col_ok[None]).astype(np.float32)
    t = jnp.einsum("hab,bcx->hacx", rpb.astype(F32), sel_c, precision=lax.Precision.HIGHEST)
    t = jnp.where(col_ok[None, None], t * LOG2E, NEG)
    t = jnp.concatenate([t, t], axis=-1)

    def build(t_ref, o_ref):
        left = lax.broadcasted_iota(jnp.int32, (1, LANES), 1) < GRID_W
        neg = jnp.full((GRID_W, LANES), NEG, F32)
        for rr in range(rows):
            for p in range(rows // 2):
                parts = [t_ref[k - rr + NA_ROWS - 1] if row_ok[rr, k] else neg for k in (2 * p, 2 * p + 1)]
                blk = jnp.where(left, parts[0], parts[1]) if (row_ok[rr, 2 * p] or row_ok[rr, 2 * p + 1]) else neg
                o_ref[rr * GRID_W:(rr + 1) * GRID_W, p * LANES:(p + 1) * LANES] = blk

    return pl.pallas_call(
        build,
        grid=(N_HEADS,),
        in_specs=[pl.BlockSpec((None, 2 * NA_ROWS - 1, GRID_W, LANES), lambda h: (h, 0, 0, 0))],
        out_specs=pl.BlockSpec((None, DEC_SEQ, DEC_SEQ), lambda h: (h, 0, 0)),
        out_shape=jax.ShapeDtypeStruct((N_HEADS, DEC_SEQ, DEC_SEQ), F32),
        compiler_params=_cparams(("arbitrary",), 32),
        name="nbr_bias",
    )(t)


def _even_w_in_layout(w):
    cq_end = 3 * N_HEADS * HEAD_DIM + B_Q_RANK
    kv_end = cq_end + B_KV_RANK
    tk = 256

    def layout(w_ref, o_ref):
        o_ref[:, :cq_end] = w_ref[:, :cq_end].astype(BF16)
        o_ref[:, cq_end + 2 * LANES:] = w_ref[:, cq_end:kv_end].astype(BF16)
        kr = w_ref[:, kv_end:].astype(BF16)
        r = lax.broadcasted_iota(jnp.int32, (B_ROPE_DIM, 2 * LANES), 0)
        c = lax.broadcasted_iota(jnp.int32, (B_ROPE_DIM, 2 * LANES), 1)
        place = jnp.where(jnp.logical_or(c == r, c == r + 2 * LANES - B_ROPE_DIM), 1.0, 0.0).astype(BF16)
        o_ref[:, cq_end:cq_end + 2 * LANES] = _dot(kr, place).astype(BF16)

    return pl.pallas_call(
        layout,
        grid=(D_MODEL // tk,),
        in_specs=[pl.BlockSpec((tk, w.shape[1]), lambda i: (i, 0))],
        out_specs=pl.BlockSpec((tk, PROJ_W), lambda i: (i, 0)),
        out_shape=jax.ShapeDtypeStruct((D_MODEL, PROJ_W), BF16),
        compiler_params=_cparams(("arbitrary",), 32),
        name="even_w_in_layout",
    )(w)


def _even_layer(x, mt, l, i, rope64, cache_a_k, cache_a_v, cache_b_ckv, cache_b_krope, norm_mix_g, ev_w_in,
                ev_lambda, ev_subln_g, ev_q_norm_g, ev_kv_norm_g, ev_w_uq, ev_w_ukv, ev_w_out):
    w_in = _even_w_in_layout(ev_w_in[i])
    all_g = (0, 1, 2, 3)
    a_scale = A_QK_DIM ** -0.5 * LOG2E
    in_kw = dict(prologue="modulate", g=norm_mix_g[l][None, :], norm_tile=8, norm_g=ev_kv_norm_g[i][None, :],
                 out_dtype=BF16, tile_scale={0: a_scale, 1: a_scale})
    bcq_src = (0, ((6, 0, TN), (7, 0, TN)))
    head_src = lambda t0: tuple((t0 + t, gq, 4 * t + gq) for t in range(2) for gq in range(4))
    proj_p, st_ak, st_av, st_ckv, st_kr, bcq_p = _projection(
        *x["ctx"], N_SEG_P, D_MODEL, 0, D_MODEL, w_in, mt=mt[:N_SEG_P], name="even_in_proj_ctx",
        states=((N_HEADS, head_src(2)), (N_HEADS, head_src(4)), (0, ((8, 0, B_KV_RANK),)), (0, ((7, 256, B_ROPE_DIM),)),
                bcq_src), **in_kw)
    proj_l, bcq_l = _projection(*x["lat"], DEC_BATCH, D_MODEL, 0, D_MODEL, w_in, mt=mt[N_SEG_P:], rope=rope64,
                                sh=A_QK_DIM // 4, rope_groups={0: all_g, 1: all_g, 2: all_g, 3: all_g, 7: (2, 3)},
                                states=(bcq_src,), name="even_in_proj_lat", **in_kw)
    wq = ev_w_uq[i].reshape(B_Q_RANK, N_HEADS, HEAD_DIM + B_ROPE_DIM)
    w_uq = jnp.concatenate([wq[:, :, :HEAD_DIM].reshape(B_Q_RANK, -1), wq[:, :, HEAD_DIM:].reshape(B_Q_RANK, -1)],
                           axis=1).astype(BF16)
    mla_scale = (HEAD_DIM + B_ROPE_DIM) ** -0.5
    q_kw = dict(prologue="rmsnorm", g=ev_q_norm_g[i][None, :], out_dtype=BF16,
                tile_scale={t: mla_scale * LOG2E for t in range(3)})
    bq_p = _projection(bcq_p, 0, N_SEG_P, 1024, 0, B_Q_RANK, w_uq, name="mla_q_up_ctx", **q_kw)
    bq_l = _projection(bcq_l, 0, DEC_BATCH, 1024, 0, B_Q_RANK, w_uq, rope=rope64, sh=B_ROPE_DIM // 4,
                       rope_groups={2: all_g}, name="mla_q_up_lat", **q_kw)
    w_ukv = ev_w_ukv[i].astype(BF16)
    kv_kw = dict(prologue="cast", out_dtype=BF16)
    kv_p = _projection(proj_p, 0, N_SEG_P, 512, 8, B_KV_RANK, w_ukv, name="mla_kv_up_ctx", **kv_kw)
    kv_l = _projection(proj_l, 0, DEC_BATCH, 512, 8, B_KV_RANK, w_ukv, name="mla_kv_up_lat", **kv_kw)
    kv_ctx = _projection(cache_b_ckv[:, i].reshape(DEC_BATCH * PAST_LEN, B_KV_RANK), 0, 1, 512, 0, B_KV_RANK, w_ukv,
                         name="mla_kv_up_cache", **kv_kw)
    kv_ctx = kv_ctx.reshape(DEC_BATCH, PAST_LEN, 2 * N_HEADS * HEAD_DIM)
    krc = cache_b_krope[:, i]
    zc = jnp.zeros_like(krc)
    kr_ctx = jnp.concatenate([krc, zc, zc, krc], axis=-1)
    ak_ctx = cache_a_k[:, i].reshape(DEC_BATCH, PAST_LEN, N_HEADS * HEAD_DIM)
    av_ctx = cache_a_v[:, i].reshape(DEC_BATCH, PAST_LEN, N_HEADS * HEAD_DIM)
    lam_init = 0.8 - 0.6 * math.exp(-0.3 * l)
    o = jnp.zeros((N_TOK, D_MODEL), BF16)
    for latent, proj, b_q, kv in ((False, proj_p, bq_p, kv_p), (True, proj_l, bq_l, kv_l)):
        tag = "lat" if latent else "ctx"
        o = _attention(kind="diff", latent=latent, q_parts=[(proj, lambda h: h)], ko_parts=[(proj, lambda h: 8 + h)],
                       vo=(proj, lambda h: 16 + h), kc_parts=[(ak_ctx, lambda h: h)], vc=(av_ctx, lambda h: h),
                       lam=ev_lambda[i], subln=ev_subln_g[i][None, :], o_arr=o, o_blk=lambda h: h,
                       lam_init=lam_init, name="diff_attn_" + tag)
        o = _attention(kind="mla", latent=latent, q_parts=[(b_q, lambda h: h), (b_q, lambda h: 8 + h // 2)],
                       ko_parts=[(kv, lambda h: 2 * h), (proj, lambda h: 30 + h % 2)], vo=(kv, lambda h: 2 * h + 1),
                       kc_parts=[(kv_ctx, lambda h: 2 * h), (kr_ctx, lambda h: h % 2)], vc=(kv_ctx, lambda h: 2 * h + 1),
                       o_arr=o, o_blk=lambda h: 8 + h, name="mla_attn_" + tag)
    states = (st_ak.reshape(BATCH, SEQ, N_HEADS, HEAD_DIM), st_av.reshape(BATCH, SEQ, N_HEADS, HEAD_DIM),
              st_ckv.reshape(BATCH, SEQ, B_KV_RANK), st_kr.reshape(BATCH, SEQ, B_ROPE_DIM))
    return (o, ev_w_out[i].astype(BF16)), states


def _odd_layer(x, mt, l, i, rope128, cache_c_k, cache_c_v, cache_d_k, cache_d_v, norm_mix_g, od_w_in, od_rpb,
               od_sink, od_w_out):
    w_in = od_w_in[i].astype(BF16)
    all_g = (0, 1, 2, 3)
    q_scale = HEAD_DIM ** -0.5 * LOG2E
    in_kw = dict(prologue="modulate", g=norm_mix_g[l][None, :], out_dtype=BF16,
                 tile_scale={t: q_scale for t in (0, 1, 6, 7)})
    head_src = lambda t0: tuple((t0 + t, gq, 4 * t + gq) for t in range(2) for gq in range(4))
    proj_p, st_ck, st_cv, st_dk, st_dv = _projection(
        *x["ctx"], N_SEG_P, D_MODEL, 0, D_MODEL, w_in, mt=mt[:N_SEG_P], name="odd_in_proj_ctx",
        states=((N_HEADS, head_src(2)), (N_HEADS, head_src(4)), (D_KV_HEADS, ((8, 0, 0), (8, 1, 1))),
                (D_KV_HEADS, ((8, 2, 0), (8, 3, 1)))), **in_kw)
    proj_l = _projection(*x["lat"], DEC_BATCH, D_MODEL, 0, D_MODEL, w_in, mt=mt[N_SEG_P:], rope=rope128,
                         sh=HEAD_DIM // 4, rope_groups={6: all_g, 7: all_g, 8: (0, 1)}, name="odd_in_proj_lat", **in_kw)
    ck_ctx = cache_c_k[:, i].reshape(DEC_BATCH, PAST_LEN, N_HEADS * HEAD_DIM)
    cv_ctx = cache_c_v[:, i].reshape(DEC_BATCH, PAST_LEN, N_HEADS * HEAD_DIM)
    dk_ctx = cache_d_k[:, i].reshape(DEC_BATCH, PAST_LEN, D_KV_HEADS * HEAD_DIM)
    dv_ctx = cache_d_v[:, i].reshape(DEC_BATCH, PAST_LEN, D_KV_HEADS * HEAD_DIM)
    bias = _neighbourhood_bias(od_rpb[i])
    sink = od_sink[i].astype(F32)
    o = jnp.zeros((N_TOK, D_MODEL), BF16)
    for latent, proj in ((False, proj_p), (True, proj_l)):
        tag = "lat" if latent else "ctx"
        o = _attention(kind="plain", latent=latent, q_parts=[(proj, lambda h: h)], ko_parts=[(proj, lambda h: 8 + h)],
                       vo=(proj, lambda h: 16 + h), kc_parts=[(ck_ctx, lambda h: h)], vc=(cv_ctx, lambda h: h),
                       bias=bias if latent else None, o_arr=o, o_blk=lambda h: h,
                       name="nbr_attn_" + tag)
        o = _attention(kind="plain", latent=latent, q_parts=[(proj, lambda h: 24 + h)],
                       ko_parts=[(proj, lambda h: 32 + h // D_GROUP)], vo=(proj, lambda h: 34 + h // D_GROUP),
                       kc_parts=[(dk_ctx, lambda h: h // D_GROUP)], vc=(dv_ctx, lambda h: h // D_GROUP), sink=sink,
                       o_arr=o, o_blk=lambda h: 8 + h, has_band=latent, tq_lat=256,
                       name="win_attn_" + tag)
    states = (st_ck.reshape(BATCH, SEQ, N_HEADS, HEAD_DIM), st_cv.reshape(BATCH, SEQ, N_HEADS, HEAD_DIM),
              st_dk.reshape(BATCH, SEQ, D_KV_HEADS, HEAD_DIM), st_dv.reshape(BATCH, SEQ, D_KV_HEADS, HEAD_DIM))
    return (o, od_w_out[i].astype(BF16)), states


def _moe_layer(mix, x_parts, mt, g, router_w, router_b, layer, w_gate, w_up, w_down, final_g, final):
    x, h2, eidx, gates = _out_projection_router(*mix, x_parts, mt, g, router_w, router_b)
    src_tok, te, n_active, pos = _route_plan(eidx)
    ys = _moe_experts(h2, src_tok, te, n_active, layer, w_gate, w_up, w_down)
    pos2 = pos.reshape(2, N_TOK)
    gates_t = gates.T
    outs = []
    for row0, n_rows in ((0, N_PROMPT), (N_PROMPT, N_TOK - N_PROMPT)):
        ysel = ys.at[pos2[:, row0:row0 + n_rows].reshape(-1)].get(mode="promise_in_bounds")
        outs.append(_combine(x, row0, n_rows, ysel.reshape(2, n_rows, D_MODEL // 2), gates_t, mt, final_g, final))
    return outs


def kernel(x_prompt, x_sample, cache_a_k, cache_a_v, cache_b_ckv, cache_b_krope, cache_c_k, cache_c_v, cache_d_k, cache_d_v, c, c_ctx, w_ada, b_ada, norm_mix_g, norm_ffn_g, ev_w_in, ev_lambda, ev_subln_g, ev_q_norm_g, ev_kv_norm_g, ev_w_uq, ev_w_ukv, ev_w_out, od_w_in, od_rpb, od_sink, od_w_out, router_w, router_b, moe_w_gate, moe_w_up, moe_w_down, final_g):
    xp = x_prompt.reshape(N_PROMPT, D_MODEL)
    xl = x_sample.reshape(DEC_BATCH * DEC_SEQ, D_MODEL)
    x = dict(ctx=(xp, 0), lat=(xl, 0), parts=[xp, xl])
    cond8 = jnp.concatenate([c_ctx[None, :], c, jnp.zeros((3, D_MODEL), F32)], axis=0)
    mod = _adaln(cond8, w_ada, b_ada)
    seg_row = np.array([0] * N_SEG_P + [1 + b for b in range(DEC_BATCH)])
    mt_all = mod[:, seg_row].reshape(DEPTH, N_SEG, 6, D_MODEL)
    rope64 = _rope_tables(A_QK_DIM)
    rope128 = _rope_tables(HEAD_DIM)
    even_states, odd_states = [], []
    for l in range(DEPTH):
        i = l // 2
        mt = mt_all[l]
        if l % 2 == 0:
            mix, st = _even_layer(x, mt, l, i, rope64, cache_a_k, cache_a_v, cache_b_ckv, cache_b_krope, norm_mix_g,
                                ev_w_in, ev_lambda, ev_subln_g, ev_q_norm_g, ev_kv_norm_g, ev_w_uq, ev_w_ukv, ev_w_out)
            even_states.append(st)
        else:
            mix, st = _odd_layer(x, mt, l, i, rope128, cache_c_k, cache_c_v, cache_d_k, cache_d_v, norm_mix_g,
                               od_w_in, od_rpb, od_sink, od_w_out)
            odd_states.append(st)
        x = _moe_layer(mix, x["parts"], mt, norm_ffn_g[l][None, :], router_w, router_b, l, moe_w_gate, moe_w_up, moe_w_down,
                       final_g[None, :], final=(l == DEPTH - 1))
        if l < DEPTH - 1:
            x = dict(ctx=(x[0], 0), lat=(x[1], 0), parts=list(x))
    y_prompt = x[0].reshape(BATCH, SEQ, D_MODEL)
    y_sample = x[1].reshape(DEC_BATCH, DEC_SEQ, D_MODEL)
    new_even = tuple(jnp.stack([st[k] for st in even_states], axis=1) for k in range(4))
    new_odd = tuple(jnp.stack([st[k] for st in odd_states], axis=1) for k in range(4))
    return (y_prompt, y_sample) + new_even + new_odd
```

```python
import functools
import math

import numpy as np
import jax
import jax.numpy as jnp
from jax import lax
from jax.experimental import pallas as pl
from jax.experimental.pallas import tpu as pltpu

D_MODEL = 2048
BATCH = 32
SEQ = 256
DEPTH = 2
DEC_BATCH = 4
DEC_SEQ = 1024
PAST_LEN = 256
GRID_W = 64
HEAD_DIM = 128
N_HEADS = 8
A_QK_DIM = 64
B_Q_RANK = 768
B_KV_RANK = 512
B_ROPE_DIM = 64
NA_ROWS = 8
NA_COLS = 16
D_KV_HEADS = 2
D_GROUP = 4
D_WINDOW = 128
N_EXPERTS = 16
N_GROUPS = 4
D_EXPERT = 1024
ROPE_THETA = 10000.0
EPS = 1e-6
NEG = -1e30
LOG2E = 1.4426950408889634

SEG = 1024
N_PROMPT = BATCH * SEQ
N_TOK = N_PROMPT + DEC_BATCH * DEC_SEQ
N_SEG = N_TOK // SEG
N_SEG_P = N_PROMPT // SEG
PROJ_W = 4608
TN = 512
LANES = 128
MOE_TM = 256
MOE_ROWS = 2 * N_TOK + N_EXPERTS * MOE_TM
MOE_TILES = MOE_ROWS // MOE_TM

F32 = jnp.float32
BF16 = jnp.bfloat16


def _cparams(sem, vmem_mb):
    return pltpu.CompilerParams(dimension_semantics=sem, vmem_limit_bytes=vmem_mb * 1024 * 1024)


def _dot(a, b):
    return jnp.dot(a, b, preferred_element_type=F32)


def _dot_nt(a, b):
    return lax.dot_general(a, b, (((1,), (1,)), ((), ())), preferred_element_type=F32)


def _sigmoid(x):
    return 1.0 / (1.0 + jnp.exp(-x))


def _pack_bf16_pair(x):
    c = x.shape[1] // 2
    bits = pltpu.bitcast(x.astype(BF16).astype(F32), jnp.int32)
    return bits[:, :c] | lax.shift_right_logical(bits[:, c:], 16)


def _unpack_bf16_pair(w):
    hi = pltpu.bitcast(w & jnp.int32(-65536), F32)
    lo = pltpu.bitcast(lax.shift_left(w, 16), F32)
    return hi, lo


def _adaln_kernel(c_ref, w_ref, b_ref, o_ref):
    c = c_ref[...]
    a = (c * _sigmoid(c)).astype(BF16)
    o_ref[...] = _dot(a, w_ref[...].astype(BF16)) + b_ref[...]


def _adaln(cond8, w_ada, b_ada):
    tn = 1024
    n = 6 * D_MODEL
    return pl.pallas_call(
        _adaln_kernel,
        grid=(DEPTH, n // tn),
        in_specs=[
            pl.BlockSpec((8, D_MODEL), lambda l, j: (0, 0)),
            pl.BlockSpec((None, D_MODEL, tn), lambda l, j: (l, 0, j)),
            pl.BlockSpec((None, 1, tn), lambda l, j: (l, 0, j)),
        ],
        out_specs=pl.BlockSpec((None, 8, tn), lambda l, j: (l, 0, j)),
        out_shape=jax.ShapeDtypeStruct((DEPTH, 8, n), F32),
        compiler_params=_cparams(("arbitrary", "arbitrary"), 40),
        name="adaln",
    )(cond8, w_ada, b_ada.reshape(DEPTH, 1, n))


def _rope(a, cos, sp, sm, sh):
    return a * cos + pltpu.roll(a, sh, 1) * sp + pltpu.roll(a, LANES - sh, 1) * sm


def _proj_kernel(*refs, prologue, k_in, sh, n_tiles, rope_groups, norm_tile, states, tile_scale):
    x_ref, mt_ref, g_ref, w_ref, cos_ref, sp_ref, sm_ref, ng_ref, o_ref = refs[:9]
    st_refs = refs[9:-1]
    xn_ref = refs[-1]
    j = pl.program_id(1)

    @pl.when(j == 0)
    def _():
        x = x_ref[:, :k_in]
        if prologue == "cast":
            xn_ref[...] = x.astype(BF16)
        else:
            r = lax.rsqrt(jnp.mean(x * x, axis=-1, keepdims=True) + EPS)
            if prologue == "modulate":
                y = x * r * (g_ref[...] * (1.0 + mt_ref[0, 1:2, :])) + mt_ref[0, 0:1, :]
            else:
                y = x * r * g_ref[...]
            xn_ref[...] = y.astype(BF16)

    acc = _dot(xn_ref[...], w_ref[...])
    n_grp = acc.shape[1] // LANES

    def treatment(t):
        acts = []
        for k, (n_heads, src) in enumerate(states):
            dest = 0
            for tt, a, b in src:
                if tt == t:
                    acts.append((k, a, b, dest))
                dest += 0 if n_heads else b
        return (tuple(rope_groups.get(t, ())), t == norm_tile, tuple(acts), float(tile_scale.get(t, 1.0)))

    branches = {}
    for t in range(n_tiles):
        branches.setdefault(treatment(t), []).append(t)

    o_ref[...] = acc.astype(o_ref.dtype)
    reread = o_ref.dtype == F32

    for (rg, is_norm, acts, sc), tiles in branches.items():
        if not rg and not is_norm and not acts and sc == 1.0:
            continue
        cond = j == tiles[0]
        for t in tiles[1:]:
            cond = jnp.logical_or(cond, j == t)

        @pl.when(cond)
        def _(rg=rg, is_norm=is_norm, acts=acts, sc=sc):
            vals = [(o_ref if reread else acc)[:, g * LANES:(g + 1) * LANES] for g in range(n_grp)]
            if sc != 1.0:
                vals = [v * sc for v in vals]
            if is_norm:
                ssq = None
                for v in vals:
                    s = jnp.sum(v * v, axis=-1, keepdims=True)
                    ssq = s if ssq is None else ssq + s
                scale = lax.rsqrt(ssq * (1.0 / (n_grp * LANES)) + EPS)
                ng = ng_ref[...]
                vals = [v * scale * ng[:, g * LANES:(g + 1) * LANES] for g, v in enumerate(vals)]
            elif rg:
                cos, sp, sm = cos_ref[...], sp_ref[...], sm_ref[...]
                vals = [_rope(v, cos, sp, sm, sh) if g in rg else v for g, v in enumerate(vals)]
            if is_norm or rg or sc != 1.0:
                for g, v in enumerate(vals):
                    if is_norm or sc != 1.0 or g in rg:
                        o_ref[:, g * LANES:(g + 1) * LANES] = v.astype(o_ref.dtype)
            for k, a, b, dest in acts:
                n_heads = states[k][0]
                if n_heads:
                    st_refs[k][pl.ds(b, SEG, stride=n_heads), :] = vals[a]
                elif b % LANES == 0:
                    for q in range(b // LANES):
                        st_refs[k][:, dest + q * LANES:dest + (q + 1) * LANES] = vals[a // LANES + q]
                else:
                    st_refs[k][...] = vals[a // LANES][:, a % LANES:a % LANES + b]


def _projection(x, row_off, n_seg, x_blk_w, x_blk_idx, k_in, w, *, prologue, mt=None, g=None, rope=None, sh=16,
                rope_groups=None, norm_tile=None, norm_g=None, states=(), out_dtype=F32, tile_scale=None, name):
    n = w.shape[1]
    m = n_seg * SEG
    assert n % TN == 0 and w.shape[0] == k_in
    if mt is None:
        mt = jnp.zeros((n_seg, 6, LANES), F32)
    if g is None:
        g = jnp.ones((1, k_in), F32)
    if rope is None:
        rope = tuple(jnp.zeros((8, LANES), F32) for _ in range(3))
    if norm_g is None:
        norm_g = jnp.ones((1, TN), F32)
    kern = functools.partial(_proj_kernel, prologue=prologue, k_in=k_in, sh=sh, n_tiles=n // TN,
                             rope_groups=dict(rope_groups or {}), norm_tile=norm_tile, states=tuple(states),
                             tile_scale=dict(tile_scale or {}))
    full2 = lambda i, j: (0, 0)
    out_shape = [jax.ShapeDtypeStruct((m, n), out_dtype)]
    out_specs = [pl.BlockSpec((SEG, TN), lambda i, j: (i, j))]
    for n_heads, src in states:
        if n_heads:
            out_shape.append(jax.ShapeDtypeStruct((m * n_heads, LANES), F32))
            out_specs.append(pl.BlockSpec((SEG * n_heads, LANES), lambda i, j: (i, 0)))
        else:
            width = sum(b for _, _, b in src)
            out_shape.append(jax.ShapeDtypeStruct((m, width), F32))
            out_specs.append(pl.BlockSpec((SEG, width), lambda i, j: (i, 0)))
    x_kw = dict(pipeline_mode=pl.Buffered(1)) if states else {}
    res = pl.pallas_call(
        kern,
        grid=(n_seg, n // TN),
        in_specs=[
            pl.BlockSpec((SEG, x_blk_w), lambda i, j: (i + row_off, x_blk_idx), **x_kw),
            pl.BlockSpec((1, 6, mt.shape[2]), lambda i, j: (i, 0, 0)),
            pl.BlockSpec((1, k_in), full2),
            pl.BlockSpec((k_in, TN), lambda i, j: (0, j)),
            pl.BlockSpec(rope[0].shape, full2),
            pl.BlockSpec(rope[1].shape, full2),
            pl.BlockSpec(rope[2].shape, full2),
            pl.BlockSpec((1, TN), full2),
        ],
        out_specs=out_specs,
        out_shape=out_shape,
        scratch_shapes=[pltpu.VMEM((SEG, k_in), BF16)],
        compiler_params=_cparams(("arbitrary", "arbitrary"), 56 if states else 48),
        name=name,
    )(x, mt, g, w, *rope, norm_g)
    return res if states else res[0]


def _attn_kernel(*refs, kind, n_parts, has_ctx, has_bias, has_band, has_sink, nb, s_len, tq, lam_init, own_win):
    it = iter(refs)
    q_refs = [next(it) for _ in range(n_parts)]
    ko_refs = [next(it) for _ in range(n_parts)]
    vo_ref = next(it)
    kc_refs, vc_ref = [], None
    if has_ctx:
        kc_refs = [next(it) for _ in range(n_parts)]
        vc_ref = next(it)
    bias_ref = next(it) if has_bias else None
    lam_ref = sg_ref = None
    if kind == "diff":
        lam_ref = next(it)
        sg_ref = next(it)
    sink_ref = next(it) if has_sink else None
    o_ref = next(it)

    sink2 = sink_ref[pl.program_id(1)] * LOG2E if has_sink else None
    lane = lax.broadcasted_iota(jnp.int32, (1, LANES), 1)

    def softmax_pv(s_list, v_list):
        m = None
        for s in s_list:
            mm = jnp.max(s, axis=-1, keepdims=True)
            m = mm if m is None else jnp.maximum(m, mm)
        if sink2 is not None:
            m = jnp.maximum(m, sink2)
        l = jnp.exp2(sink2 - m) if sink2 is not None else None
        o = None
        for s, v in zip(s_list, v_list):
            e = jnp.exp2(s - m)
            ss = jnp.sum(e, axis=-1, keepdims=True)
            l = ss if l is None else l + ss
            pv = _dot(e.astype(BF16), v)
            o = pv if o is None else o + pv
        return o, l

    for bi in range(nb):
        qrows = pl.ds(bi * tq, tq)
        q0 = (pl.program_id(2) * nb + bi) * tq
        if own_win is None:
            krows, k0 = pl.ds(0 if has_ctx else bi * s_len, s_len), 0
            n_own = s_len
        else:
            k0 = pl.multiple_of(jnp.clip(q0 - D_WINDOW, 0, s_len - own_win), LANES)
            krows = pl.ds(k0, own_win)
            n_own = own_win
        k_srcs, v_srcs, is_own = [], [], []
        if has_ctx:
            k_srcs.append([r[...].astype(BF16) for r in kc_refs])
            v_srcs.append(vc_ref[...].astype(BF16))
            is_own.append(False)
        k_srcs.append([r[krows, :].astype(BF16) for r in ko_refs])
        v_srcs.append(vo_ref[krows, :].astype(BF16))
        is_own.append(True)

        def mask_own(s):
            if has_bias:
                s = s + bias_ref[qrows, :]
            if has_band:
                qpos = q0 + lax.broadcasted_iota(jnp.int32, (tq, 1), 0)
                kpos = k0 + lax.broadcasted_iota(jnp.int32, (1, n_own), 1)
                s = jnp.where(jnp.abs(kpos - qpos) <= D_WINDOW, s, NEG)
            return s

        if kind == "diff":
            q = q_refs[0][qrows, :]
            zero = jnp.zeros_like(q)
            q1 = jnp.where(lane < A_QK_DIM, q, zero)
            q2 = jnp.where(lane < A_QK_DIM, zero, q)
            o1, l1 = softmax_pv([_dot_nt(q1, ks[0]) for ks in k_srcs], v_srcs)
            o2, l2 = softmax_pv([_dot_nt(q2, ks[0]) for ks in k_srcs], v_srcs)
            lv = lam_ref[...]
            lam = (jnp.exp(jnp.sum(lv[0:1] * lv[1:2], axis=-1, keepdims=True))
                   - jnp.exp(jnp.sum(lv[2:3] * lv[3:4], axis=-1, keepdims=True)) + lam_init)
            o = o1 * (1.0 / l1) - o2 * (lam / l2)
            o = o * lax.rsqrt(jnp.mean(o * o, axis=-1, keepdims=True) + EPS) * sg_ref[...] * (1.0 - lam_init)
        else:
            qs = [r[qrows, :] for r in q_refs]
            s_list = []
            for ks, own in zip(k_srcs, is_own):
                s = None
                for qp, kp in zip(qs, ks):
                    d = _dot_nt(qp, kp)
                    s = d if s is None else s + d
                s_list.append(mask_own(s) if own else s)
            o, l = softmax_pv(s_list, v_srcs)
            o = o * (1.0 / l)
        o_ref[qrows, :] = o.astype(o_ref.dtype)


def _attention(*, kind, latent, q_parts, ko_parts, vo, kc_parts=(), vc=None, bias=None, lam=None, subln=None,
               sink=None, o_arr, o_blk, lam_init=0.0, has_band=False, tq_lat=512, name):
    n_parts = len(q_parts)
    has_ctx = latent
    if latent:
        tq, s_len, nb = tq_lat, DEC_SEQ, DEC_SEQ // tq_lat
        qblk, kblk = nb * tq, SEG
        grid = (DEC_BATCH, N_HEADS, DEC_SEQ // qblk)
        qpb = DEC_SEQ // qblk
        q_row = lambda b, h, t: b * qpb + t
        o_row = lambda b, h, t: N_PROMPT // qblk + b * qpb + t
        k_row = lambda b, h, t: b
    else:
        qblk = kblk = 4 * SEG
        tq, s_len, nb = SEQ, SEQ, qblk // SEQ
        grid = (N_PROMPT // qblk, N_HEADS, 1)
        q_row = o_row = k_row = lambda b, h, t: b
    args, specs = [], []

    def add(arr, shape, imap, **kw):
        args.append(arr)
        specs.append(pl.BlockSpec(shape, imap, **kw))

    for arr, f in q_parts:
        add(arr, (qblk, LANES), lambda b, h, t, f=f: (q_row(b, h, t), f(h)))
    for arr, f in list(ko_parts) + [vo]:
        add(arr, (kblk, LANES), lambda b, h, t, f=f: (k_row(b, h, t), f(h)))
    if has_ctx:
        for arr, f in list(kc_parts) + [vc]:
            add(arr, (None, PAST_LEN, LANES), lambda b, h, t, f=f: (b, 0, f(h)))
    if bias is not None:
        add(bias, (None, qblk, DEC_SEQ), lambda b, h, t: (h, t, 0))
    if kind == "diff":
        add(lam, lam.shape, lambda b, h, t: (0, 0))
        add(subln, subln.shape, lambda b, h, t: (0, 0))
    if sink is not None:
        args.append(sink)
        specs.append(pl.BlockSpec(memory_space=pltpu.SMEM))
    n_in = len(args)
    args.append(o_arr)
    specs.append(pl.BlockSpec(memory_space=pl.ANY))
    kern = functools.partial(_attn_kernel_aliased, kind=kind, n_parts=n_parts, has_ctx=has_ctx,
                             has_bias=bias is not None, has_band=has_band, has_sink=sink is not None,
                             nb=nb, s_len=s_len, tq=tq, lam_init=lam_init,
                             own_win=(tq + 2 * D_WINDOW) if has_band else None)
    return pl.pallas_call(
        kern,
        grid=grid,
        in_specs=specs,
        out_specs=pl.BlockSpec((qblk, LANES), lambda b, h, t: (o_row(b, h, t), o_blk(h))),
        out_shape=jax.ShapeDtypeStruct(o_arr.shape, o_arr.dtype),
        input_output_aliases={n_in: 0},
        compiler_params=_cparams(("arbitrary", "arbitrary", "arbitrary"), 48),
        name=name,
    )(*args)


def _attn_kernel_aliased(*refs, **kw):
    _attn_kernel(*refs[:-2], refs[-1], **kw)


def _ffn_norm_route(x, mt_ref, g_ref, w_ref, rb_ref, h_ref, e_ref, gt_ref):
    r = lax.rsqrt(jnp.mean(x * x, axis=-1, keepdims=True) + EPS)
    h = x * r * (g_ref[...] * (1.0 + mt_ref[0, 4:5, :])) + mt_ref[0, 3:4, :]
    hh = h.astype(BF16)
    h_ref[...] = _pack_bf16_pair(hh)
    logits = _dot_nt(w_ref[...], hh)
    scores = _sigmoid(logits)
    sel = scores + rb_ref[...]
    per = N_EXPERTS // N_GROUPS
    sc = [scores[e:e + 1, :] for e in range(N_EXPERTS)]
    sl = [sel[e:e + 1, :] for e in range(N_EXPERTS)]
    best_g, best_v = None, None
    for gi in range(N_GROUPS):
        a, b, c, d = sl[gi * per:(gi + 1) * per]
        hi1, lo1, hi2, lo2 = jnp.maximum(a, b), jnp.minimum(a, b), jnp.maximum(c, d), jnp.minimum(c, d)
        gs = jnp.maximum(hi1, hi2) + jnp.maximum(jnp.minimum(hi1, hi2), jnp.maximum(lo1, lo2))
        if gi == 0:
            best_g, best_v = jnp.zeros_like(gs, dtype=jnp.int32), gs
        else:
            better = gs > best_v
            best_g = jnp.where(better, gi, best_g)
            best_v = jnp.where(better, gs, best_v)
    masked = [jnp.where(best_g == (e // per), sl[e], NEG) for e in range(N_EXPERTS)]
    i1, v1 = jnp.zeros_like(best_g), masked[0]
    for e in range(1, N_EXPERTS):
        better = masked[e] > v1
        i1 = jnp.where(better, e, i1)
        v1 = jnp.where(better, masked[e], v1)
    i2, v2 = None, None
    for e in range(N_EXPERTS):
        cand = jnp.where(i1 == e, -2e30, masked[e])
        if e == 0:
            i2, v2 = jnp.zeros_like(best_g), cand
        else:
            better = cand > v2
            i2 = jnp.where(better, e, i2)
            v2 = jnp.where(better, cand, v2)
    g1 = jnp.zeros_like(v1)
    g2 = jnp.zeros_like(v1)
    for e in range(N_EXPERTS):
        g1 = jnp.where(i1 == e, sc[e], g1)
        g2 = jnp.where(i2 == e, sc[e], g2)
    tot = g1 + g2
    rows = i1.shape[1]
    e_ref[...] = jnp.concatenate([i1, i2, jnp.zeros((6, rows), jnp.int32)], axis=0)
    gt_ref[...] = jnp.concatenate([g1 / tot, g2 / tot, jnp.zeros((6, rows), F32)], axis=0)


def _outproj_router_kernel(o_ref, w_ref, mt_ref, g_ref, rw_ref, rb_ref, *rest, blk_ranges):
    x_refs = rest[:-4]
    y_ref, h_ref, e_ref, gt_ref = rest[-4:]
    upd = mt_ref[0, 2:3, :] * _dot(o_ref[...], w_ref[...])
    if len(x_refs) == 1:
        y_ref[...] = x_refs[0][...] + upd
    else:
        i = pl.program_id(0)
        for x_ref, (lo, hi) in zip(x_refs, blk_ranges):
            @pl.when(jnp.logical_and(i >= lo, i < hi))
            def _(x_ref=x_ref):
                y_ref[...] = x_ref[...] + upd
    _ffn_norm_route(y_ref[...], mt_ref, g_ref, rw_ref, rb_ref, h_ref, e_ref, gt_ref)


def _out_projection_router(o, w, x_parts, mt, g, router_w, router_b):
    tm = 256
    blk_ranges, lo = [], 0
    for xp in x_parts:
        blk_ranges.append((lo, lo + xp.shape[0] // tm))
        lo = blk_ranges[-1][1]
    assert lo == N_TOK // tm
    x_specs = [pl.BlockSpec((tm, D_MODEL), lambda i, lo=lo, hi=hi: (jnp.clip(i - lo, 0, hi - lo - 1), 0))
               for lo, hi in blk_ranges]
    const = lambda i: (0, 0)
    return pl.pallas_call(
        functools.partial(_outproj_router_kernel, blk_ranges=tuple(blk_ranges)),
        grid=(N_TOK // tm,),
        in_specs=[
            pl.BlockSpec((tm, D_MODEL), lambda i: (i, 0)),
            pl.BlockSpec((D_MODEL, D_MODEL), const, pipeline_mode=pl.Buffered(1)),
            pl.BlockSpec((1, 6, D_MODEL), lambda i: (i // (SEG // tm), 0, 0)),
            pl.BlockSpec((1, D_MODEL), const),
            pl.BlockSpec((N_EXPERTS, D_MODEL), const),
            pl.BlockSpec((N_EXPERTS, 1), const),
        ] + x_specs,
        out_specs=[
            pl.BlockSpec((tm, D_MODEL), lambda i: (i, 0)),
            pl.BlockSpec((tm, D_MODEL // 2), lambda i: (i, 0)),
            pl.BlockSpec((8, tm), lambda i: (0, i)),
            pl.BlockSpec((8, tm), lambda i: (0, i)),
        ],
        out_shape=[
            jax.ShapeDtypeStruct((N_TOK, D_MODEL), F32),
            jax.ShapeDtypeStruct((N_TOK, D_MODEL // 2), jnp.int32),
            jax.ShapeDtypeStruct((8, N_TOK), jnp.int32),
            jax.ShapeDtypeStruct((8, N_TOK), F32),
        ],
        compiler_params=_cparams(("arbitrary",), 48),
        name="out_proj_router",
    )(o, w, mt, g, router_w.T.astype(BF16), router_b.reshape(N_EXPERTS, 1), *x_parts)


def _expert_weights(te_ref, first_ref, slot_ref, nxt_ref, i, t, layer, w_hbm, stage_ref, sem_ref, w_bf_refs):
    def copies(e, s):
        return [pltpu.make_async_copy(w.at[layer, e], stage_ref.at[s, k], sem_ref.at[s, k])
                for k, w in enumerate(w_hbm)]

    s = slot_ref[t]

    @pl.when(i == 0)
    def _():
        for c in copies(te_ref[t], s):
            c.start()

    @pl.when(first_ref[t] == 1)
    def _():
        for c in copies(te_ref[t], s):
            c.wait()

        @pl.when(nxt_ref[t] >= 0)
        def _():
            for c in copies(nxt_ref[t], 1 - s):
                c.start(priority=1)

        for k, w_bf in enumerate(w_bf_refs):
            w_bf[...] = stage_ref[s, k].astype(BF16)


def _moe_up_kernel(te_ref, na_ref, first_ref, slot_ref, nxt_ref, xs_ref, wg_hbm, wu_hbm, hid_ref,
                   stage_ref, wgb_ref, wub_ref, sem_ref, *, tile0, layer):
    i = pl.program_id(0)
    t = i + tile0
    _expert_weights(te_ref, first_ref, slot_ref, nxt_ref, i, t, layer, (wg_hbm, wu_hbm), stage_ref, sem_ref,
                    (wgb_ref, wub_ref))

    @pl.when(t < na_ref[0])
    def _():
        x_hi, x_lo = _unpack_bf16_pair(xs_ref[...])
        x_hi, x_lo = x_hi.astype(BF16), x_lo.astype(BF16)
        half = D_MODEL // 2
        g = _dot(x_hi, wgb_ref[:half, :]) + _dot(x_lo, wgb_ref[half:, :])
        u = _dot(x_hi, wub_ref[:half, :]) + _dot(x_lo, wub_ref[half:, :])
        hid_ref[...] = (g * _sigmoid(g) * u).astype(BF16)

    @pl.when(t >= na_ref[0])
    def _():
        hid_ref[...] = jnp.zeros_like(hid_ref)


def _moe_down_kernel(te_ref, na_ref, first_ref, slot_ref, nxt_ref, *rest, layer, chunk_tiles):
    ys_ref, stage_ref, wdb_ref, sem_ref = rest[-4:]
    wd_hbm = rest[-5]
    hid_refs = rest[:-5]
    t = pl.program_id(0)
    _expert_weights(te_ref, first_ref, slot_ref, nxt_ref, t, t, layer, (wd_hbm,), stage_ref, sem_ref, (wdb_ref,))

    for c, hid_ref in enumerate(hid_refs):
        @pl.when(jnp.logical_and(t < na_ref[0], t // chunk_tiles == c))
        def _(hid_ref=hid_ref):
            ys_ref[...] = _pack_bf16_pair(_dot(hid_ref[...], wdb_ref[...]))

    @pl.when(t >= na_ref[0])
    def _():
        ys_ref[...] = jnp.zeros_like(ys_ref)


MOE_CHUNKS = 2


def _moe_experts(h2, src_tok, tile_expert, n_active, layer, w_gate, w_up, w_down):
    tiles = MOE_TILES // MOE_CHUNKS
    rows = tiles * MOE_TM

    def weight_runs(call_tiles):
        t_ids = jnp.arange(MOE_TILES, dtype=jnp.int32)
        prev = jnp.concatenate([tile_expert[:1], tile_expert[:-1]])
        first = jnp.logical_or(t_ids % call_tiles == 0, tile_expert != prev)
        slot = ((jnp.cumsum(first.astype(jnp.int32)) - 1) % 2).astype(jnp.int32)
        first_at = lax.cummin(jnp.where(first, t_ids, MOE_TILES), reverse=True)
        next_at = jnp.concatenate([first_at[1:], jnp.full((1,), MOE_TILES, jnp.int32)])
        same_call = jnp.logical_and(next_at < MOE_TILES, next_at // call_tiles == t_ids // call_tiles)
        nxt = jnp.where(jnp.logical_and(first, same_call), tile_expert[jnp.minimum(next_at, MOE_TILES - 1)], -1)
        return (tile_expert, n_active, first.astype(jnp.int32), slot, nxt.astype(jnp.int32))

    row_blk = lambda i, *_: (i, 0)
    hbm = pl.BlockSpec(memory_space=pl.ANY)
    plan = weight_runs(tiles)
    hids = []
    for c in range(MOE_CHUNKS):
        xs = h2.at[src_tok[c * rows:(c + 1) * rows]].get(mode="promise_in_bounds")
        hids.append(pl.pallas_call(
            functools.partial(_moe_up_kernel, tile0=c * tiles, layer=layer),
            grid_spec=pltpu.PrefetchScalarGridSpec(
                num_scalar_prefetch=len(plan),
                grid=(tiles,),
                in_specs=[pl.BlockSpec((MOE_TM, D_MODEL // 2), row_blk), hbm, hbm],
                out_specs=pl.BlockSpec((MOE_TM, D_EXPERT), row_blk),
                scratch_shapes=[pltpu.VMEM((2, 2, D_MODEL, D_EXPERT), F32),
                                pltpu.VMEM((D_MODEL, D_EXPERT), BF16), pltpu.VMEM((D_MODEL, D_EXPERT), BF16),
                                pltpu.SemaphoreType.DMA((2, 2))],
            ),
            out_shape=jax.ShapeDtypeStruct((rows, D_EXPERT), BF16),
            compiler_params=_cparams(("arbitrary",), 52),
            name="moe_up",
        )(*plan, xs, w_gate, w_up))
    plan = weight_runs(MOE_TILES)
    hid_specs = [pl.BlockSpec((MOE_TM, D_EXPERT), lambda i, *_, c=c: (jnp.clip(i - c * tiles, 0, tiles - 1), 0))
                 for c in range(MOE_CHUNKS)]
    return pl.pallas_call(
        functools.partial(_moe_down_kernel, layer=layer, chunk_tiles=tiles),
        grid_spec=pltpu.PrefetchScalarGridSpec(
            num_scalar_prefetch=len(plan),
            grid=(MOE_TILES,),
            in_specs=hid_specs + [hbm],
            out_specs=pl.BlockSpec((MOE_TM, D_MODEL // 2), row_blk),
            scratch_shapes=[pltpu.VMEM((2, 1, D_EXPERT, D_MODEL), F32), pltpu.VMEM((D_EXPERT, D_MODEL), BF16),
                            pltpu.SemaphoreType.DMA((2, 1))],
        ),
        out_shape=jax.ShapeDtypeStruct((MOE_ROWS, D_MODEL // 2), jnp.int32),
        compiler_params=_cparams(("arbitrary",), 40),
        name="moe_down",
    )(*plan, *hids, w_down)


PLAN_R = 2 * N_TOK // LANES


def _plan_kernel(e_ref, pos_ref, meta_ref, src_ref):
    e = e_ref[...]
    r_i = lax.broadcasted_iota(jnp.int32, (LANES, LANES), 0)
    c_i = lax.broadcasted_iota(jnp.int32, (LANES, LANES), 1)
    upper = jnp.where(r_i <= c_i, 1.0, 0.0).astype(BF16)
    r_j = lax.broadcasted_iota(jnp.int32, (PLAN_R, PLAN_R), 0)
    c_j = lax.broadcasted_iota(jnp.int32, (PLAN_R, PLAN_R), 1)
    lower = jnp.where(c_j < r_j, 1.0, 0.0).astype(BF16)
    lane = lax.broadcasted_iota(jnp.int32, (1, LANES), 1)
    pos = jnp.zeros((PLAN_R, LANES), F32)
    cnt_row = jnp.zeros((1, LANES), F32)
    off_row = jnp.zeros((1, LANES), F32)
    end_row, off_list, cnt_list = [], [], []
    row_off = jnp.zeros((1, 1), F32)
    tiles_done = jnp.zeros((1, 1), F32)
    for ex in range(N_EXPERTS):
        m = jnp.where(e == ex, 1.0, 0.0)
        inc = _dot(m.astype(BF16), upper)
        tot = jnp.broadcast_to(inc[:, LANES - 1:LANES], (PLAN_R, LANES))
        before = _dot(lower, tot.astype(BF16))
        cnt = jnp.sum(inc[:, LANES - 1:LANES], axis=0, keepdims=True)
        pos = pos + m * (row_off + before + inc - 1.0)
        tiles = jnp.floor((cnt + (MOE_TM - 1)) * (1.0 / MOE_TM))
        cnt_row = jnp.where(lane == ex, cnt, cnt_row)
        off_row = jnp.where(lane == ex, row_off, off_row)
        cnt_list.append(cnt)
        off_list.append(row_off)
        tiles_done = tiles_done + tiles
        end_row.append(tiles_done)
        row_off = row_off + tiles * MOE_TM
    n_active = tiles_done
    tid = jnp.minimum(lane.astype(F32), n_active - 1.0)
    te = jnp.zeros((1, LANES), F32)
    for ex in range(N_EXPERTS):
        te = te + jnp.where(end_row[ex] <= tid, 1.0, 0.0)
    pos_ref[...] = pos.astype(jnp.int32)
    meta = jnp.concatenate([cnt_row, off_row, te, jnp.broadcast_to(n_active, (1, LANES)),
                            jnp.zeros((4, LANES), F32)], axis=0)
    meta_ref[...] = meta.astype(jnp.int32)
    pr = MOE_ROWS // LANES
    q = lax.broadcasted_iota(jnp.int32, (pr, 1), 0)
    tile = jnp.minimum((q // (MOE_TM // LANES)).astype(F32), n_active - 1.0)
    e_q = jnp.zeros((pr, 1), F32)
    for ex in range(N_EXPERTS):
        e_q = e_q + jnp.where(end_row[ex] <= tile, 1.0, 0.0)
    off_q = jnp.zeros((pr, 1), F32)
    cnt_q = jnp.zeros((pr, 1), F32)
    start_q = jnp.zeros((pr, 1), F32)
    start = jnp.zeros((1, 1), F32)
    for ex in range(N_EXPERTS):
        hit = e_q == float(ex)
        off_q = jnp.where(hit, off_list[ex], off_q)
        cnt_q = jnp.where(hit, cnt_list[ex], cnt_q)
        start_q = jnp.where(hit, start, start_q)
        start = start + cnt_list[ex]
    r = (q * LANES + lane).astype(F32)
    rank = r - off_q
    spare = r - jnp.where(r >= float(N_TOK), float(N_TOK), 0.0) - jnp.where(r >= float(2 * N_TOK), float(N_TOK), 0.0)
    src_ref[...] = jnp.where(rank < cnt_q, start_q + rank, -(spare + 1.0)).astype(jnp.int32)


def _route_plan(eidx):
    e2 = eidx[:2].reshape(PLAN_R, LANES)
    pos, meta, src = pl.pallas_call(
        _plan_kernel,
        out_shape=[jax.ShapeDtypeStruct((PLAN_R, LANES), jnp.int32), jax.ShapeDtypeStruct((8, LANES), jnp.int32),
                   jax.ShapeDtypeStruct((MOE_ROWS // LANES, LANES), jnp.int32)],
        compiler_params=pltpu.CompilerParams(vmem_limit_bytes=32 * 1024 * 1024),
        name="route_plan",
    )(e2)
    te, n_active = meta[2, :MOE_TILES], meta[3, :1]
    order = jnp.argsort(e2.reshape(-1), stable=True).astype(jnp.int32)
    src = src.reshape(-1)
    src_tok = jnp.where(src >= 0, order[jnp.maximum(src, 0)] % N_TOK, -src - 1)
    return src_tok, te, n_active, pos.reshape(-1)


def _combine_kernel(x_ref, y0_ref, y1_ref, gt_ref, mt_ref, fg_ref, o_ref, *, final):
    gt = gt_ref[...]
    g0, g1 = gt[:, 0:1], gt[:, 1:2]
    half = D_MODEL // 2
    y0 = _unpack_bf16_pair(y0_ref[...])
    y1 = _unpack_bf16_pair(y1_ref[...])
    xs = []
    for c in range(2):
        cols = slice(c * half, (c + 1) * half)
        xs.append(x_ref[:, cols] + mt_ref[0, 5:6, cols] * (y0[c] * g0 + y1[c] * g1))
    if final:
        ssq = jnp.sum(xs[0] * xs[0], axis=-1, keepdims=True) + jnp.sum(xs[1] * xs[1], axis=-1, keepdims=True)
        r = lax.rsqrt(ssq * (1.0 / D_MODEL) + EPS)
        xs = [xc * r * fg_ref[:, c * half:(c + 1) * half] for c, xc in enumerate(xs)]
    for c, xc in enumerate(xs):
        o_ref[:, c * half:(c + 1) * half] = xc


def _combine(x, row0, n_rows, ysel, gates_t, mt, final_g, final):
    tm = 256
    b0 = row0 // tm
    return pl.pallas_call(
        functools.partial(_combine_kernel, final=final),
        grid=(n_rows // tm,),
        in_specs=[
            pl.BlockSpec((tm, D_MODEL), lambda i: (i + b0, 0)),
            pl.BlockSpec((None, tm, D_MODEL // 2), lambda i: (0, i, 0)),
            pl.BlockSpec((None, tm, D_MODEL // 2), lambda i: (1, i, 0)),
            pl.BlockSpec((tm, 8), lambda i: (i + b0, 0)),
            pl.BlockSpec((1, 6, D_MODEL), lambda i: ((i + b0) // (SEG // tm), 0, 0)),
            pl.BlockSpec((1, D_MODEL), lambda i: (0, 0)),
        ],
        out_specs=pl.BlockSpec((tm, D_MODEL), lambda i: (i, 0)),
        out_shape=jax.ShapeDtypeStruct((n_rows, D_MODEL), F32),
        compiler_params=_cparams(("arbitrary",), 40),
        name="moe_combine",
    )(x, ysel, ysel, gates_t, mt, final_g)


def _rope_tables(dim):
    half = dim // 2
    inv = ROPE_THETA ** (-jnp.arange(0, half, 2, dtype=F32) / half)
    t = jnp.arange(DEC_SEQ)
    ang_r = (t // GRID_W).astype(F32)[:, None] * inv[None, :]
    ang_c = (t % GRID_W).astype(F32)[:, None] * inv[None, :]
    ang = jnp.concatenate([ang_r, ang_r, ang_c, ang_c], axis=-1)
    cos, sin = jnp.cos(ang), jnp.sin(ang)
    reps = LANES // dim
    cos = jnp.tile(cos, (1, reps))
    sin = jnp.tile(sin, (1, reps))
    sh = dim // 4
    second = (np.arange(LANES) % (2 * sh)) >= sh
    sp = jnp.where(second[None, :], sin, 0.0)
    sm = jnp.where(second[None, :], 0.0, -sin)
    return cos, sp, sm


def _neighbourhood_bias(rpb):
    rows = DEC_SEQ // GRID_W
    kh = min(NA_ROWS, rows)
    r = np.arange(rows)
    r0 = np.clip(r - kh // 2, 0, rows - kh)
    kr = np.arange(rows)
    row_ok = (kr[None, :] >= r0[:, None]) & (kr[None, :] < r0[:, None] + kh)
    c = np.arange(GRID_W)
    ws = np.clip(c - NA_COLS // 2, 0, GRID_W - NA_COLS)
    kc = np.arange(GRID_W)
    col_ok = (kc[None, :] >= ws[:, None]) & (kc[None, :] < ws[:, None] + NA_COLS)
    dc_idx = np.clip(kc[None, :] - c[:, None], -(NA_COLS - 1), NA_COLS - 1) + NA_COLS - 1
    sel_c = ((np.arange(2 * NA_COLS - 1)[:, None, None] == dc_idx[None]) & col_ok[None]).astype(np.float32)
    t = jnp.einsum("hab,bcx->hacx", rpb.astype(F32), sel_c, precision=lax.Precision.HIGHEST)
    t = jnp.where(col_ok[None, None], t * LOG2E, NEG)
    t = jnp.concatenate([t, t], axis=-1)

    def build(t_ref, o_ref):
        left = lax.broadcasted_iota(jnp.int32, (1, LANES), 1) < GRID_W
        neg = jnp.full((GRID_W, LANES), NEG, F32)
        for rr in range(rows):
            for p in range(rows // 2):
                parts = [t_ref[k - rr + NA_ROWS - 1] if row_ok[rr, k] else neg for k in (2 * p, 2 * p + 1)]
                blk = jnp.where(left, parts[0], parts[1]) if (row_ok[rr, 2 * p] or row_ok[rr, 2 * p + 1]) else neg
                o_ref[rr * GRID_W:(rr + 1) * GRID_W, p * LANES:(p + 1) * LANES] = blk

    return pl.pallas_call(
        build,
        grid=(N_HEADS,),
        in_specs=[pl.BlockSpec((None, 2 * NA_ROWS - 1, GRID_W, LANES), lambda h: (h, 0, 0, 0))],
        out_specs=pl.BlockSpec((None, DEC_SEQ, DEC_SEQ), lambda h: (h, 0, 0)),
        out_shape=jax.ShapeDtypeStruct((N_HEADS, DEC_SEQ, DEC_SEQ), F32),
        compiler_params=_cparams(("arbitrary",), 32),
        name="nbr_bias",
    )(t)


def _even_w_in_layout(w):
    cq_end = 3 * N_HEADS * HEAD_DIM + B_Q_RANK
    kv_end = cq_end + B_KV_RANK
    tk = 256

    def layout(w_ref, o_ref):
        o_ref[:, :cq_end] = w_ref[:, :cq_end].astype(BF16)
        o_ref[:, cq_end + 2 * LANES:] = w_ref[:, cq_end:kv_end].astype(BF16)
        kr = w_ref[:, kv_end:].astype(BF16)
        r = lax.broadcasted_iota(jnp.int32, (B_ROPE_DIM, 2 * LANES), 0)
        c = lax.broadcasted_iota(jnp.int32, (B_ROPE_DIM, 2 * LANES), 1)
        place = jnp.where(jnp.logical_or(c == r, c == r + 2 * LANES - B_ROPE_DIM), 1.0, 0.0).astype(BF16)
        o_ref[:, cq_end:cq_end + 2 * LANES] = _dot(kr, place).astype(BF16)

    return pl.pallas_call(
        layout,
        grid=(D_MODEL // tk,),
        in_specs=[pl.BlockSpec((tk, w.shape[1]), lambda i: (i, 0))],
        out_specs=pl.BlockSpec((tk, PROJ_W), lambda i: (i, 0)),
        out_shape=jax.ShapeDtypeStruct((D_MODEL, PROJ_W), BF16),
        compiler_params=_cparams(("arbitrary",), 32),
        name="even_w_in_layout",
    )(w)


def _even_layer(x, mt, l, i, rope64, cache_a_k, cache_a_v, cache_b_ckv, cache_b_krope, norm_mix_g, ev_w_in,
                ev_lambda, ev_subln_g, ev_q_norm_g, ev_kv_norm_g, ev_w_uq, ev_w_ukv, ev_w_out):
    w_in = _even_w_in_layout(ev_w_in[i])
    all_g = (0, 1, 2, 3)
    a_scale = A_QK_DIM ** -0.5 * LOG2E
    in_kw = dict(prologue="modulate", g=norm_mix_g[l][None, :], norm_tile=8, norm_g=ev_kv_norm_g[i][None, :],
                 out_dtype=BF16, tile_scale={0: a_scale, 1: a_scale})
    bcq_src = (0, ((6, 0, TN), (7, 0, TN)))
    head_src = lambda t0: tuple((t0 + t, gq, 4 * t + gq) for t in range(2) for gq in range(4))
    proj_p, st_ak, st_av, st_ckv, st_kr, bcq_p = _projection(
        *x["ctx"], N_SEG_P, D_MODEL, 0, D_MODEL, w_in, mt=mt[:N_SEG_P], name="even_in_proj_ctx",
        states=((N_HEADS, head_src(2)), (N_HEADS, head_src(4)), (0, ((8, 0, B_KV_RANK),)), (0, ((7, 256, B_ROPE_DIM),)),
                bcq_src), **in_kw)
    proj_l, bcq_l = _projection(*x["lat"], DEC_BATCH, D_MODEL, 0, D_MODEL, w_in, mt=mt[N_SEG_P:], rope=rope64,
                                sh=A_QK_DIM // 4, rope_groups={0: all_g, 1: all_g, 2: all_g, 3: all_g, 7: (2, 3)},
                                states=(bcq_src,), name="even_in_proj_lat", **in_kw)
    wq = ev_w_uq[i].reshape(B_Q_RANK, N_HEADS, HEAD_DIM + B_ROPE_DIM)
    w_uq = jnp.concatenate([wq[:, :, :HEAD_DIM].reshape(B_Q_RANK, -1), wq[:, :, HEAD_DIM:].reshape(B_Q_RANK, -1)],
                           axis=1).astype(BF16)
    mla_scale = (HEAD_DIM + B_ROPE_DIM) ** -0.5
    q_kw = dict(prologue="rmsnorm", g=ev_q_norm_g[i][None, :], out_dtype=BF16,
                tile_scale={t: mla_scale * LOG2E for t in range(3)})
    bq_p = _projection(bcq_p, 0, N_SEG_P, 1024, 0, B_Q_RANK, w_uq, name="mla_q_up_ctx", **q_kw)
    bq_l = _projection(bcq_l, 0, DEC_BATCH, 1024, 0, B_Q_RANK, w_uq, rope=rope64, sh=B_ROPE_DIM // 4,
                       rope_groups={2: all_g}, name="mla_q_up_lat", **q_kw)
    w_ukv = ev_w_ukv[i].astype(BF16)
    kv_kw = dict(prologue="cast", out_dtype=BF16)
    kv_p = _projection(proj_p, 0, N_SEG_P, 512, 8, B_KV_RANK, w_ukv, name="mla_kv_up_ctx", **kv_kw)
    kv_l = _projection(proj_l, 0, DEC_BATCH, 512, 8, B_KV_RANK, w_ukv, name="mla_kv_up_lat", **kv_kw)
    kv_ctx = _projection(cache_b_ckv[:, i].reshape(DEC_BATCH * PAST_LEN, B_KV_RANK), 0, 1, 512, 0, B_KV_RANK, w_ukv,
                         name="mla_kv_up_cache", **kv_kw)
    kv_ctx = kv_ctx.reshape(DEC_BATCH, PAST_LEN, 2 * N_HEADS * HEAD_DIM)
    krc = cache_b_krope[:, i]
    zc = jnp.zeros_like(krc)
    kr_ctx = jnp.concatenate([krc, zc, zc, krc], axis=-1)
    ak_ctx = cache_a_k[:, i].reshape(DEC_BATCH, PAST_LEN, N_HEADS * HEAD_DIM)
    av_ctx = cache_a_v[:, i].reshape(DEC_BATCH, PAST_LEN, N_HEADS * HEAD_DIM)
    lam_init = 0.8 - 0.6 * math.exp(-0.3 * l)
    o = jnp.zeros((N_TOK, D_MODEL), BF16)
    for latent, proj, b_q, kv in ((False, proj_p, bq_p, kv_p), (True, proj_l, bq_l, kv_l)):
        tag = "lat" if latent else "ctx"
        o = _attention(kind="diff", latent=latent, q_parts=[(proj, lambda h: h)], ko_parts=[(proj, lambda h: 8 + h)],
                       vo=(proj, lambda h: 16 + h), kc_parts=[(ak_ctx, lambda h: h)], vc=(av_ctx, lambda h: h),
                       lam=ev_lambda[i], subln=ev_subln_g[i][None, :], o_arr=o, o_blk=lambda h: h,
                       lam_init=lam_init, name="diff_attn_" + tag)
        o = _attention(kind="mla", latent=latent, q_parts=[(b_q, lambda h: h), (b_q, lambda h: 8 + h // 2)],
                       ko_parts=[(kv, lambda h: 2 * h), (proj, lambda h: 30 + h % 2)], vo=(kv, lambda h: 2 * h + 1),
                       kc_parts=[(kv_ctx, lambda h: 2 * h), (kr_ctx, lambda h: h % 2)], vc=(kv_ctx, lambda h: 2 * h + 1),
                       o_arr=o, o_blk=lambda h: 8 + h, name="mla_attn_" + tag)
    states = (st_ak.reshape(BATCH, SEQ, N_HEADS, HEAD_DIM), st_av.reshape(BATCH, SEQ, N_HEADS, HEAD_DIM),
              st_ckv.reshape(BATCH, SEQ, B_KV_RANK), st_kr.reshape(BATCH, SEQ, B_ROPE_DIM))
    return (o, ev_w_out[i].astype(BF16)), states


def _odd_layer(x, mt, l, i, rope128, cache_c_k, cache_c_v, cache_d_k, cache_d_v, norm_mix_g, od_w_in, od_rpb,
               od_sink, od_w_out):
    w_in = od_w_in[i].astype(BF16)
    all_g = (0, 1, 2, 3)
    q_scale = HEAD_DIM ** -0.5 * LOG2E
    in_kw = dict(prologue="modulate", g=norm_mix_g[l][None, :], out_dtype=BF16,
                 tile_scale={t: q_scale for t in (0, 1, 6, 7)})
    head_src = lambda t0: tuple((t0 + t, gq, 4 * t + gq) for t in range(2) for gq in range(4))
    proj_p, st_ck, st_cv, st_dk, st_dv = _projection(
        *x["ctx"], N_SEG_P, D_MODEL, 0, D_MODEL, w_in, mt=mt[:N_SEG_P], name="odd_in_proj_ctx",
        states=((N_HEADS, head_src(2)), (N_HEADS, head_src(4)), (D_KV_HEADS, ((8, 0, 0), (8, 1, 1))),
                (D_KV_HEADS, ((8, 2, 0), (8, 3, 1)))), **in_kw)
    proj_l = _projection(*x["lat"], DEC_BATCH, D_MODEL, 0, D_MODEL, w_in, mt=mt[N_SEG_P:], rope=rope128,
                         sh=HEAD_DIM // 4, rope_groups={6: all_g, 7: all_g, 8: (0, 1)}, name="odd_in_proj_lat", **in_kw)
    ck_ctx = cache_c_k[:, i].reshape(DEC_BATCH, PAST_LEN, N_HEADS * HEAD_DIM)
    cv_ctx = cache_c_v[:, i].reshape(DEC_BATCH, PAST_LEN, N_HEADS * HEAD_DIM)
    dk_ctx = cache_d_k[:, i].reshape(DEC_BATCH, PAST_LEN, D_KV_HEADS * HEAD_DIM)
    dv_ctx = cache_d_v[:, i].reshape(DEC_BATCH, PAST_LEN, D_KV_HEADS * HEAD_DIM)
    bias = _neighbourhood_bias(od_rpb[i])
    sink = od_sink[i].astype(F32)
    o = jnp.zeros((N_TOK, D_MODEL), BF16)
    for latent, proj in ((False, proj_p), (True, proj_l)):
        tag = "lat" if latent else "ctx"
        o = _attention(kind="plain", latent=latent, q_parts=[(proj, lambda h: h)], ko_parts=[(proj, lambda h: 8 + h)],
                       vo=(proj, lambda h: 16 + h), kc_parts=[(ck_ctx, lambda h: h)], vc=(cv_ctx, lambda h: h),
                       bias=bias if latent else None, o_arr=o, o_blk=lambda h: h,
                       name="nbr_attn_" + tag)
        o = _attention(kind="plain", latent=latent, q_parts=[(proj, lambda h: 24 + h)],
                       ko_parts=[(proj, lambda h: 32 + h // D_GROUP)], vo=(proj, lambda h: 34 + h // D_GROUP),
                       kc_parts=[(dk_ctx, lambda h: h // D_GROUP)], vc=(dv_ctx, lambda h: h // D_GROUP), sink=sink,
                       o_arr=o, o_blk=lambda h: 8 + h, has_band=latent, tq_lat=256,
                       name="win_attn_" + tag)
    states = (st_ck.reshape(BATCH, SEQ, N_HEADS, HEAD_DIM), st_cv.reshape(BATCH, SEQ, N_HEADS, HEAD_DIM),
              st_dk.reshape(BATCH, SEQ, D_KV_HEADS, HEAD_DIM), st_dv.reshape(BATCH, SEQ, D_KV_HEADS, HEAD_DIM))
    return (o, od_w_out[i].astype(BF16)), states


def _moe_layer(mix, x_parts, mt, g, router_w, router_b, layer, w_gate, w_up, w_down, final_g, final):
    x, h2, eidx, gates = _out_projection_router(*mix, x_parts, mt, g, router_w, router_b)
    src_tok, te, n_active, pos = _route_plan(eidx)
    ys = _moe_experts(h2, src_tok, te, n_active, layer, w_gate, w_up, w_down)
    pos2 = pos.reshape(2, N_TOK)
    gates_t = gates.T
    outs = []
    for row0, n_rows in ((0, N_PROMPT), (N_PROMPT, N_TOK - N_PROMPT)):
        ysel = ys.at[pos2[:, row0:row0 + n_rows].reshape(-1)].get(mode="promise_in_bounds")
        outs.append(_combine(x, row0, n_rows, ysel.reshape(2, n_rows, D_MODEL // 2), gates_t, mt, final_g, final))
    return outs


def kernel(x_prompt, x_sample, cache_a_k, cache_a_v, cache_b_ckv, cache_b_krope, cache_c_k, cache_c_v, cache_d_k, cache_d_v, c, c_ctx, w_ada, b_ada, norm_mix_g, norm_ffn_g, ev_w_in, ev_lambda, ev_subln_g, ev_q_norm_g, ev_kv_norm_g, ev_w_uq, ev_w_ukv, ev_w_out, od_w_in, od_rpb, od_sink, od_w_out, router_w, router_b, moe_w_gate, moe_w_up, moe_w_down, final_g):
    xp = x_prompt.reshape(N_PROMPT, D_MODEL)
    xl = x_sample.reshape(DEC_BATCH * DEC_SEQ, D_MODEL)
    x = dict(ctx=(xp, 0), lat=(xl, 0), parts=[xp, xl])
    cond8 = jnp.concatenate([c_ctx[None, :], c, jnp.zeros((3, D_MODEL), F32)], axis=0)
    mod = _adaln(cond8, w_ada, b_ada)
    seg_row = np.array([0] * N_SEG_P + [1 + b for b in range(DEC_BATCH)])
    mt_all = mod[:, seg_row].reshape(DEPTH, N_SEG, 6, D_MODEL)
    rope64 = _rope_tables(A_QK_DIM)
    rope128 = _rope_tables(HEAD_DIM)
    even_states, odd_states = [], []
    for l in range(DEPTH):
        i = l // 2
        mt = mt_all[l]
        if l % 2 == 0:
            mix, st = _even_layer(x, mt, l, i, rope64, cache_a_k, cache_a_v, cache_b_ckv, cache_b_krope, norm_mix_g,
                                ev_w_in, ev_lambda, ev_subln_g, ev_q_norm_g, ev_kv_norm_g, ev_w_uq, ev_w_ukv, ev_w_out)
            even_states.append(st)
        else:
            mix, st = _odd_layer(x, mt, l, i, rope128, cache_c_k, cache_c_v, cache_d_k, cache_d_v, norm_mix_g,
                               od_w_in, od_rpb, od_sink, od_w_out)
            odd_states.append(st)
        x = _moe_layer(mix, x["parts"], mt, norm_ffn_g[l][None, :], router_w, router_b, l, moe_w_gate, moe_w_up, moe_w_down,
                       final_g[None, :], final=(l == DEPTH - 1))
        if l < DEPTH - 1:
            x = dict(ctx=(x[0], 0), lat=(x[1], 0), parts=list(x))
    y_prompt = x[0].reshape(BATCH, SEQ, D_MODEL)
    y_sample = x[1].reshape(DEC_BATCH, DEC_SEQ, D_MODEL)
    new_even = tuple(jnp.stack([st[k] for st in even_states], axis=1) for k in range(4))
    new_odd = tuple(jnp.stack([st[k] for st in odd_states], axis=1) for k in range(4))
    return (y_prompt, y_sample) + new_even + new_odd
```

```python
import functools
import math

import numpy as np
import jax
import jax.numpy as jnp
from jax import lax
from jax.experimental import pallas as pl
from jax.experimental.pallas import tpu as pltpu

D_MODEL = 2048
BATCH = 32
SEQ = 256
DEPTH = 2
DEC_BATCH = 4
DEC_SEQ = 1024
PAST_LEN = 256
GRID_W = 64
HEAD_DIM = 128
N_HEADS = 8
A_QK_DIM = 64
B_Q_RANK = 768
B_KV_RANK = 512
B_ROPE_DIM = 64
NA_ROWS = 8
NA_COLS = 16
D_KV_HEADS = 2
D_GROUP = 4
D_WINDOW = 128
N_EXPERTS = 16
N_GROUPS = 4
D_EXPERT = 1024
ROPE_THETA = 10000.0
EPS = 1e-6
NEG = -1e30
LOG2E = 1.4426950408889634

SEG = 1024
N_PROMPT = BATCH * SEQ
N_TOK = N_PROMPT + DEC_BATCH * DEC_SEQ
N_SEG = N_TOK // SEG
N_SEG_P = N_PROMPT // SEG
PROJ_W = 4608
TN = 512
LANES = 128
MOE_TM = 256
MOE_ROWS = 2 * N_TOK + N_EXPERTS * MOE_TM
MOE_TILES = MOE_ROWS // MOE_TM

F32 = jnp.float32
BF16 = jnp.bfloat16


def _cparams(sem, vmem_mb):
    return pltpu.CompilerParams(dimension_semantics=sem, vmem_limit_bytes=vmem_mb * 1024 * 1024)


def _dot(a, b):
    return jnp.dot(a, b, preferred_element_type=F32)


def _dot_nt(a, b):
    return lax.dot_general(a, b, (((1,), (1,)), ((), ())), preferred_element_type=F32)


def _sigmoid(x):
    return 1.0 / (1.0 + jnp.exp(-x))


def _pack_bf16_pair(x):
    c = x.shape[1] // 2
    bits = pltpu.bitcast(x.astype(BF16).astype(F32), jnp.int32)
    return bits[:, :c] | lax.shift_right_logical(bits[:, c:], 16)


def _unpack_bf16_pair(w):
    hi = pltpu.bitcast(w & jnp.int32(-65536), F32)
    lo = pltpu.bitcast(lax.shift_left(w, 16), F32)
    return hi, lo


def _adaln_kernel(c_ref, w_ref, b_ref, o_ref):
    c = c_ref[...]
    a = (c * _sigmoid(c)).astype(BF16)
    o_ref[...] = _dot(a, w_ref[...].astype(BF16)) + b_ref[...]


def _adaln(cond8, w_ada, b_ada):
    tn = 1024
    n = 6 * D_MODEL
    return pl.pallas_call(
        _adaln_kernel,
        grid=(DEPTH, n // tn),
        in_specs=[
            pl.BlockSpec((8, D_MODEL), lambda l, j: (0, 0)),
            pl.BlockSpec((None, D_MODEL, tn), lambda l, j: (l, 0, j)),
            pl.BlockSpec((None, 1, tn), lambda l, j: (l, 0, j)),
        ],
        out_specs=pl.BlockSpec((None, 8, tn), lambda l, j: (l, 0, j)),
        out_shape=jax.ShapeDtypeStruct((DEPTH, 8, n), F32),
        compiler_params=_cparams(("arbitrary", "arbitrary"), 40),
        name="adaln",
    )(cond8, w_ada, b_ada.reshape(DEPTH, 1, n))


def _rope(a, cos, sp, sm, sh):
    return a * cos + pltpu.roll(a, sh, 1) * sp + pltpu.roll(a, LANES - sh, 1) * sm


def _proj_kernel(*refs, prologue, k_in, sh, n_tiles, rope_groups, norm_tile, states, tile_scale):
    x_ref, mt_ref, g_ref, w_ref, cos_ref, sp_ref, sm_ref, ng_ref, o_ref = refs[:9]
    st_refs = refs[9:-1]
    xn_ref = refs[-1]
    j = pl.program_id(1)

    @pl.when(j == 0)
    def _():
        x = x_ref[:, :k_in]
        if prologue == "cast":
            xn_ref[...] = x.astype(BF16)
        else:
            r = lax.rsqrt(jnp.mean(x * x, axis=-1, keepdims=True) + EPS)
            if prologue == "modulate":
                y = x * r * (g_ref[...] * (1.0 + mt_ref[0, 1:2, :])) + mt_ref[0, 0:1, :]
            else:
                y = x * r * g_ref[...]
            xn_ref[...] = y.astype(BF16)

    acc = _dot(xn_ref[...], w_ref[...])
    n_grp = acc.shape[1] // LANES

    def treatment(t):
        acts = []
        for k, (n_heads, src) in enumerate(states):
            dest = 0
            for tt, a, b in src:
                if tt == t:
                    acts.append((k, a, b, dest))
                dest += 0 if n_heads else b
        return (tuple(rope_groups.get(t, ())), t == norm_tile, tuple(acts), float(tile_scale.get(t, 1.0)))

    branches = {}
    for t in range(n_tiles):
        branches.setdefault(treatment(t), []).append(t)

    o_ref[...] = acc.astype(o_ref.dtype)
    reread = o_ref.dtype == F32

    for (rg, is_norm, acts, sc), tiles in branches.items():
        if not rg and not is_norm and not acts and sc == 1.0:
            continue
        cond = j == tiles[0]
        for t in tiles[1:]:
            cond = jnp.logical_or(cond, j == t)

        @pl.when(cond)
        def _(rg=rg, is_norm=is_norm, acts=acts, sc=sc):
            vals = [(o_ref if reread else acc)[:, g * LANES:(g + 1) * LANES] for g in range(n_grp)]
            if sc != 1.0:
                vals = [v * sc for v in vals]
            if is_norm:
                ssq = None
                for v in vals:
                    s = jnp.sum(v * v, axis=-1, keepdims=True)
                    ssq = s if ssq is None else ssq + s
                scale = lax.rsqrt(ssq * (1.0 / (n_grp * LANES)) + EPS)
                ng = ng_ref[...]
                vals = [v * scale * ng[:, g * LANES:(g + 1) * LANES] for g, v in enumerate(vals)]
            elif rg:
                cos, sp, sm = cos_ref[...], sp_ref[...], sm_ref[...]
                vals = [_rope(v, cos, sp, sm, sh) if g in rg else v for g, v in enumerate(vals)]
            if is_norm or rg or sc != 1.0:
                for g, v in enumerate(vals):
                    if is_norm or sc != 1.0 or g in rg:
                        o_ref[:, g * LANES:(g + 1) * LANES] = v.astype(o_ref.dtype)
            for k, a, b, dest in acts:
                n_heads = states[k][0]
                if n_heads:
                    st_refs[k][pl.ds(b, SEG, stride=n_heads), :] = vals[a]
                elif b % LANES == 0:
                    for q in range(b // LANES):
                        st_refs[k][:, dest + q * LANES:dest + (q + 1) * LANES] = vals[a // LANES + q]
                else:
                    st_refs[k][...] = vals[a // LANES][:, a % LANES:a % LANES + b]


def _projection(x, row_off, n_seg, x_blk_w, x_blk_idx, k_in, w, *, prologue, mt=None, g=None, rope=None, sh=16,
                rope_groups=None, norm_tile=None, norm_g=None, states=(), out_dtype=F32, tile_scale=None, tn=TN,
                name):
    n = w.shape[1]
    m = n_seg * SEG
    assert n % tn == 0 and w.shape[0] == k_in
    if mt is None:
        mt = jnp.zeros((n_seg, 6, LANES), F32)
    if g is None:
        g = jnp.ones((1, k_in), F32)
    if rope is None:
        rope = tuple(jnp.zeros((8, LANES), F32) for _ in range(3))
    if norm_g is None:
        norm_g = jnp.ones((1, tn), F32)
    kern = functools.partial(_proj_kernel, prologue=prologue, k_in=k_in, sh=sh, n_tiles=n // tn,
                             rope_groups=dict(rope_groups or {}), norm_tile=norm_tile, states=tuple(states),
                             tile_scale=dict(tile_scale or {}))
    full2 = lambda i, j: (0, 0)
    out_shape = [jax.ShapeDtypeStruct((m, n), out_dtype)]
    out_specs = [pl.BlockSpec((SEG, tn), lambda i, j: (i, j))]
    for n_heads, src in states:
        if n_heads:
            out_shape.append(jax.ShapeDtypeStruct((m * n_heads, LANES), F32))
            out_specs.append(pl.BlockSpec((SEG * n_heads, LANES), lambda i, j: (i, 0)))
        else:
            width = sum(b for _, _, b in src)
            out_shape.append(jax.ShapeDtypeStruct((m, width), F32))
            out_specs.append(pl.BlockSpec((SEG, width), lambda i, j: (i, 0)))
    x_kw = dict(pipeline_mode=pl.Buffered(1)) if states else {}
    res = pl.pallas_call(
        kern,
        grid=(n_seg, n // tn),
        in_specs=[
            pl.BlockSpec((SEG, x_blk_w), lambda i, j: (i + row_off, x_blk_idx), **x_kw),
            pl.BlockSpec((1, 6, mt.shape[2]), lambda i, j: (i, 0, 0)),
            pl.BlockSpec((1, k_in), full2),
            pl.BlockSpec((k_in, tn), lambda i, j: (0, j)),
            pl.BlockSpec(rope[0].shape, full2),
            pl.BlockSpec(rope[1].shape, full2),
            pl.BlockSpec(rope[2].shape, full2),
            pl.BlockSpec((1, tn), full2),
        ],
        out_specs=out_specs,
        out_shape=out_shape,
        scratch_shapes=[pltpu.VMEM((SEG, k_in), BF16)],
        compiler_params=_cparams(("arbitrary", "arbitrary"), 56 if states else 48),
        name=name,
    )(x, mt, g, w, *rope, norm_g)
    return res if states else res[0]


def _attn_kernel(*refs, kind, n_parts, has_ctx, has_bias, has_band, has_sink, nb, s_len, tq, lam_init, own_win):
    it = iter(refs)
    q_refs = [next(it) for _ in range(n_parts)]
    ko_refs = [next(it) for _ in range(n_parts)]
    vo_ref = next(it)
    kc_refs, vc_ref = [], None
    if has_ctx:
        kc_refs = [next(it) for _ in range(n_parts)]
        vc_ref = next(it)
    bias_ref = next(it) if has_bias else None
    lam_ref = sg_ref = None
    if kind == "diff":
        lam_ref = next(it)
        sg_ref = next(it)
    sink_ref = next(it) if has_sink else None
    o_ref = next(it)

    sink2 = sink_ref[pl.program_id(1)] * LOG2E if has_sink else None
    lane = lax.broadcasted_iota(jnp.int32, (1, LANES), 1)

    def softmax_pv(s_list, v_list):
        m = None
        for s in s_list:
            mm = jnp.max(s, axis=-1, keepdims=True)
            m = mm if m is None else jnp.maximum(m, mm)
        if sink2 is not None:
            m = jnp.maximum(m, sink2)
        l = jnp.exp2(sink2 - m) if sink2 is not None else None
        o = None
        for s, v in zip(s_list, v_list):
            e = jnp.exp2(s - m)
            ss = jnp.sum(e, axis=-1, keepdims=True)
            l = ss if l is None else l + ss
            pv = _dot(e.astype(BF16), v)
            o = pv if o is None else o + pv
        return o, l

    for bi in range(nb):
        qrows = pl.ds(bi * tq, tq)
        q0 = (pl.program_id(2) * nb + bi) * tq
        if own_win is None:
            krows, k0 = pl.ds(0 if has_ctx else bi * s_len, s_len), 0
            n_own = s_len
        else:
            k0 = pl.multiple_of(jnp.clip(q0 - D_WINDOW, 0, s_len - own_win), LANES)
            krows = pl.ds(k0, own_win)
            n_own = own_win
        k_srcs, v_srcs, is_own = [], [], []
        if has_ctx:
            k_srcs.append([r[...].astype(BF16) for r in kc_refs])
            v_srcs.append(vc_ref[...].astype(BF16))
            is_own.append(False)
        k_srcs.append([r[krows, :].astype(BF16) for r in ko_refs])
        v_srcs.append(vo_ref[krows, :].astype(BF16))
        is_own.append(True)

        def mask_own(s):
            if has_bias:
                s = s + bias_ref[qrows, :]
            if has_band:
                qpos = q0 + lax.broadcasted_iota(jnp.int32, (tq, 1), 0)
                kpos = k0 + lax.broadcasted_iota(jnp.int32, (1, n_own), 1)
                s = jnp.where(jnp.abs(kpos - qpos) <= D_WINDOW, s, NEG)
            return s

        if kind == "diff":
            q = q_refs[0][qrows, :]
            zero = jnp.zeros_like(q)
            q1 = jnp.where(lane < A_QK_DIM, q, zero)
            q2 = jnp.where(lane < A_QK_DIM, zero, q)
            o1, l1 = softmax_pv([_dot_nt(q1, ks[0]) for ks in k_srcs], v_srcs)
            o2, l2 = softmax_pv([_dot_nt(q2, ks[0]) for ks in k_srcs], v_srcs)
            lv = lam_ref[...]
            lam = (jnp.exp(jnp.sum(lv[0:1] * lv[1:2], axis=-1, keepdims=True))
                   - jnp.exp(jnp.sum(lv[2:3] * lv[3:4], axis=-1, keepdims=True)) + lam_init)
            o = o1 * (1.0 / l1) - o2 * (lam / l2)
            o = o * lax.rsqrt(jnp.mean(o * o, axis=-1, keepdims=True) + EPS) * sg_ref[...] * (1.0 - lam_init)
        else:
            qs = [r[qrows, :] for r in q_refs]
            s_list = []
            for ks, own in zip(k_srcs, is_own):
                s = None
                for qp, kp in zip(qs, ks):
                    d = _dot_nt(qp, kp)
                    s = d if s is None else s + d
                s_list.append(mask_own(s) if own else s)
            o, l = softmax_pv(s_list, v_srcs)
            o = o * (1.0 / l)
        o_ref[qrows, :] = o.astype(o_ref.dtype)


def _attention(*, kind, latent, q_parts, ko_parts, vo, kc_parts=(), vc=None, bias=None, lam=None, subln=None,
               sink=None, o_arr, o_blk, lam_init=0.0, has_band=False, tq_lat=512, name):
    n_parts = len(q_parts)
    has_ctx = latent
    if latent:
        tq, s_len, nb = tq_lat, DEC_SEQ, DEC_SEQ // tq_lat
        qblk, kblk = nb * tq, SEG
        grid = (DEC_BATCH, N_HEADS, DEC_SEQ // qblk)
        qpb = DEC_SEQ // qblk
        q_row = lambda b, h, t: b * qpb + t
        o_row = lambda b, h, t: N_PROMPT // qblk + b * qpb + t
        k_row = lambda b, h, t: b
    else:
        qblk = kblk = 4 * SEG
        tq, s_len, nb = SEQ, SEQ, qblk // SEQ
        grid = (N_PROMPT // qblk, N_HEADS, 1)
        q_row = o_row = k_row = lambda b, h, t: b
    args, specs = [], []

    def add(arr, shape, imap, **kw):
        args.append(arr)
        specs.append(pl.BlockSpec(shape, imap, **kw))

    for arr, f in q_parts:
        add(arr, (qblk, LANES), lambda b, h, t, f=f: (q_row(b, h, t), f(h)))
    for arr, f in list(ko_parts) + [vo]:
        add(arr, (kblk, LANES), lambda b, h, t, f=f: (k_row(b, h, t), f(h)))
    if has_ctx:
        for arr, f in list(kc_parts) + [vc]:
            add(arr, (None, PAST_LEN, LANES), lambda b, h, t, f=f: (b, 0, f(h)))
    if bias is not None:
        add(bias, (None, qblk, DEC_SEQ), lambda b, h, t: (h, t, 0))
    if kind == "diff":
        add(lam, lam.shape, lambda b, h, t: (0, 0))
        add(subln, subln.shape, lambda b, h, t: (0, 0))
    if sink is not None:
        args.append(sink)
        specs.append(pl.BlockSpec(memory_space=pltpu.SMEM))
    n_in = len(args)
    args.append(o_arr)
    specs.append(pl.BlockSpec(memory_space=pl.ANY))
    kern = functools.partial(_attn_kernel_aliased, kind=kind, n_parts=n_parts, has_ctx=has_ctx,
                             has_bias=bias is not None, has_band=has_band, has_sink=sink is not None,
                             nb=nb, s_len=s_len, tq=tq, lam_init=lam_init,
                             own_win=(tq + 2 * D_WINDOW) if has_band else None)
    return pl.pallas_call(
        kern,
        grid=grid,
        in_specs=specs,
        out_specs=pl.BlockSpec((qblk, LANES), lambda b, h, t: (o_row(b, h, t), o_blk(h))),
        out_shape=jax.ShapeDtypeStruct(o_arr.shape, o_arr.dtype),
        input_output_aliases={n_in: 0},
        compiler_params=_cparams(("arbitrary", "arbitrary", "arbitrary"), 48),
        name=name,
    )(*args)


def _attn_kernel_aliased(*refs, **kw):
    _attn_kernel(*refs[:-2], refs[-1], **kw)


def _ffn_norm_route(x, mt_ref, g_ref, w_ref, rb_ref, h_ref, e_ref, gt_ref):
    r = lax.rsqrt(jnp.mean(x * x, axis=-1, keepdims=True) + EPS)
    h = x * r * (g_ref[...] * (1.0 + mt_ref[0, 4:5, :])) + mt_ref[0, 3:4, :]
    hh = h.astype(BF16)
    h_ref[...] = _pack_bf16_pair(hh)
    logits = _dot_nt(w_ref[...], hh)
    scores = _sigmoid(logits)
    sel = scores + rb_ref[...]
    per = N_EXPERTS // N_GROUPS
    sc = [scores[e:e + 1, :] for e in range(N_EXPERTS)]
    sl = [sel[e:e + 1, :] for e in range(N_EXPERTS)]
    best_g, best_v = None, None
    for gi in range(N_GROUPS):
        a, b, c, d = sl[gi * per:(gi + 1) * per]
        hi1, lo1, hi2, lo2 = jnp.maximum(a, b), jnp.minimum(a, b), jnp.maximum(c, d), jnp.minimum(c, d)
        gs = jnp.maximum(hi1, hi2) + jnp.maximum(jnp.minimum(hi1, hi2), jnp.maximum(lo1, lo2))
        if gi == 0:
            best_g, best_v = jnp.zeros_like(gs, dtype=jnp.int32), gs
        else:
            better = gs > best_v
            best_g = jnp.where(better, gi, best_g)
            best_v = jnp.where(better, gs, best_v)
    masked = [jnp.where(best_g == (e // per), sl[e], NEG) for e in range(N_EXPERTS)]
    i1, v1 = jnp.zeros_like(best_g), masked[0]
    for e in range(1, N_EXPERTS):
        better = masked[e] > v1
        i1 = jnp.where(better, e, i1)
        v1 = jnp.where(better, masked[e], v1)
    i2, v2 = None, None
    for e in range(N_EXPERTS):
        cand = jnp.where(i1 == e, -2e30, masked[e])
        if e == 0:
            i2, v2 = jnp.zeros_like(best_g), cand
        else:
            better = cand > v2
            i2 = jnp.where(better, e, i2)
            v2 = jnp.where(better, cand, v2)
    g1 = jnp.zeros_like(v1)
    g2 = jnp.zeros_like(v1)
    for e in range(N_EXPERTS):
        g1 = jnp.where(i1 == e, sc[e], g1)
        g2 = jnp.where(i2 == e, sc[e], g2)
    tot = g1 + g2
    rows = i1.shape[1]
    e_ref[...] = jnp.concatenate([i1, i2, jnp.zeros((6, rows), jnp.int32)], axis=0)
    gt_ref[...] = jnp.concatenate([g1 / tot, g2 / tot, jnp.zeros((6, rows), F32)], axis=0)


def _outproj_router_kernel(o_ref, w_ref, mt_ref, g_ref, rw_ref, rb_ref, *rest, blk_ranges):
    x_refs = rest[:-4]
    y_ref, h_ref, e_ref, gt_ref = rest[-4:]
    upd = mt_ref[0, 2:3, :] * _dot(o_ref[...], w_ref[...])
    if len(x_refs) == 1:
        y_ref[...] = x_refs[0][...] + upd
    else:
        i = pl.program_id(0)
        for x_ref, (lo, hi) in zip(x_refs, blk_ranges):
            @pl.when(jnp.logical_and(i >= lo, i < hi))
            def _(x_ref=x_ref):
                y_ref[...] = x_ref[...] + upd
    _ffn_norm_route(y_ref[...], mt_ref, g_ref, rw_ref, rb_ref, h_ref, e_ref, gt_ref)


def _out_projection_router(o, w, x_parts, mt, g, router_w, router_b):
    tm = 256
    blk_ranges, lo = [], 0
    for xp in x_parts:
        blk_ranges.append((lo, lo + xp.shape[0] // tm))
        lo = blk_ranges[-1][1]
    assert lo == N_TOK // tm
    x_specs = [pl.BlockSpec((tm, D_MODEL), lambda i, lo=lo, hi=hi: (jnp.clip(i - lo, 0, hi - lo - 1), 0))
               for lo, hi in blk_ranges]
    const = lambda i: (0, 0)
    return pl.pallas_call(
        functools.partial(_outproj_router_kernel, blk_ranges=tuple(blk_ranges)),
        grid=(N_TOK // tm,),
        in_specs=[
            pl.BlockSpec((tm, D_MODEL), lambda i: (i, 0)),
            pl.BlockSpec((D_MODEL, D_MODEL), const, pipeline_mode=pl.Buffered(1)),
            pl.BlockSpec((1, 6, D_MODEL), lambda i: (i // (SEG // tm), 0, 0)),
            pl.BlockSpec((1, D_MODEL), const),
            pl.BlockSpec((N_EXPERTS, D_MODEL), const),
            pl.BlockSpec((N_EXPERTS, 1), const),
        ] + x_specs,
        out_specs=[
            pl.BlockSpec((tm, D_MODEL), lambda i: (i, 0)),
            pl.BlockSpec((tm, D_MODEL // 2), lambda i: (i, 0)),
            pl.BlockSpec((8, tm), lambda i: (0, i)),
            pl.BlockSpec((8, tm), lambda i: (0, i)),
        ],
        out_shape=[
            jax.ShapeDtypeStruct((N_TOK, D_MODEL), F32),
            jax.ShapeDtypeStruct((N_TOK, D_MODEL // 2), jnp.int32),
            jax.ShapeDtypeStruct((8, N_TOK), jnp.int32),
            jax.ShapeDtypeStruct((8, N_TOK), F32),
        ],
        compiler_params=_cparams(("arbitrary",), 48),
        name="out_proj_router",
    )(o, w, mt, g, router_w.T.astype(BF16), router_b.reshape(N_EXPERTS, 1), *x_parts)


def _expert_weights(te_ref, first_ref, slot_ref, nxt_ref, i, t, layer, w_hbm, stage_ref, sem_ref, w_bf_refs):
    def copies(e, s):
        return [pltpu.make_async_copy(w.at[layer, e], stage_ref.at[s, k], sem_ref.at[s, k])
                for k, w in enumerate(w_hbm)]

    s = slot_ref[t]

    @pl.when(i == 0)
    def _():
        for c in copies(te_ref[t], s):
            c.start()

    @pl.when(first_ref[t] == 1)
    def _():
        for c in copies(te_ref[t], s):
            c.wait()

        @pl.when(nxt_ref[t] >= 0)
        def _():
            for c in copies(nxt_ref[t], 1 - s):
                c.start(priority=1)

        for k, w_bf in enumerate(w_bf_refs):
            w_bf[...] = stage_ref[s, k].astype(BF16)


def _moe_up_kernel(te_ref, na_ref, first_ref, slot_ref, nxt_ref, xs_ref, wg_hbm, wu_hbm, hid_ref,
                   stage_ref, wgb_ref, wub_ref, sem_ref, *, tile0, layer):
    i = pl.program_id(0)
    t = i + tile0
    _expert_weights(te_ref, first_ref, slot_ref, nxt_ref, i, t, layer, (wg_hbm, wu_hbm), stage_ref, sem_ref,
                    (wgb_ref, wub_ref))

    @pl.when(t < na_ref[0])
    def _():
        x_hi, x_lo = _unpack_bf16_pair(xs_ref[...])
        x_hi, x_lo = x_hi.astype(BF16), x_lo.astype(BF16)
        half = D_MODEL // 2
        g = _dot(x_hi, wgb_ref[:half, :]) + _dot(x_lo, wgb_ref[half:, :])
        u = _dot(x_hi, wub_ref[:half, :]) + _dot(x_lo, wub_ref[half:, :])
        hid_ref[...] = (g * _sigmoid(g) * u).astype(BF16)

    @pl.when(t >= na_ref[0])
    def _():
        hid_ref[...] = jnp.zeros_like(hid_ref)


def _moe_down_kernel(te_ref, na_ref, first_ref, slot_ref, nxt_ref, *rest, layer, chunk_tiles):
    ys_ref, stage_ref, wdb_ref, sem_ref = rest[-4:]
    wd_hbm = rest[-5]
    hid_refs = rest[:-5]
    t = pl.program_id(0)
    _expert_weights(te_ref, first_ref, slot_ref, nxt_ref, t, t, layer, (wd_hbm,), stage_ref, sem_ref, (wdb_ref,))

    for c, hid_ref in enumerate(hid_refs):
        @pl.when(jnp.logical_and(t < na_ref[0], t // chunk_tiles == c))
        def _(hid_ref=hid_ref):
            ys_ref[...] = _pack_bf16_pair(_dot(hid_ref[...], wdb_ref[...]))

    @pl.when(t >= na_ref[0])
    def _():
        ys_ref[...] = jnp.zeros_like(ys_ref)


MOE_CHUNKS = 2


def _moe_experts(h2, src_tok, tile_expert, n_active, layer, w_gate, w_up, w_down):
    tiles = MOE_TILES // MOE_CHUNKS
    rows = tiles * MOE_TM

    def weight_runs(call_tiles):
        t_ids = jnp.arange(MOE_TILES, dtype=jnp.int32)
        prev = jnp.concatenate([tile_expert[:1], tile_expert[:-1]])
        first = jnp.logical_or(t_ids % call_tiles == 0, tile_expert != prev)
        slot = ((jnp.cumsum(first.astype(jnp.int32)) - 1) % 2).astype(jnp.int32)
        first_at = lax.cummin(jnp.where(first, t_ids, MOE_TILES), reverse=True)
        next_at = jnp.concatenate([first_at[1:], jnp.full((1,), MOE_TILES, jnp.int32)])
        same_call = jnp.logical_and(next_at < MOE_TILES, next_at // call_tiles == t_ids // call_tiles)
        nxt = jnp.where(jnp.logical_and(first, same_call), tile_expert[jnp.minimum(next_at, MOE_TILES - 1)], -1)
        return (tile_expert, n_active, first.astype(jnp.int32), slot, nxt.astype(jnp.int32))

    row_blk = lambda i, *_: (i, 0)
    hbm = pl.BlockSpec(memory_space=pl.ANY)
    plan = weight_runs(tiles)
    hids = []
    for c in range(MOE_CHUNKS):
        xs = h2.at[src_tok[c * rows:(c + 1) * rows]].get(mode="promise_in_bounds")
        hids.append(pl.pallas_call(
            functools.partial(_moe_up_kernel, tile0=c * tiles, layer=layer),
            grid_spec=pltpu.PrefetchScalarGridSpec(
                num_scalar_prefetch=len(plan),
                grid=(tiles,),
                in_specs=[pl.BlockSpec((MOE_TM, D_MODEL // 2), row_blk), hbm, hbm],
                out_specs=pl.BlockSpec((MOE_TM, D_EXPERT), row_blk),
                scratch_shapes=[pltpu.VMEM((2, 2, D_MODEL, D_EXPERT), F32),
                                pltpu.VMEM((D_MODEL, D_EXPERT), BF16), pltpu.VMEM((D_MODEL, D_EXPERT), BF16),
                                pltpu.SemaphoreType.DMA((2, 2))],
            ),
            out_shape=jax.ShapeDtypeStruct((rows, D_EXPERT), BF16),
            compiler_params=_cparams(("arbitrary",), 52),
            name="moe_up",
        )(*plan, xs, w_gate, w_up))
    plan = weight_runs(MOE_TILES)
    hid_specs = [pl.BlockSpec((MOE_TM, D_EXPERT), lambda i, *_, c=c: (jnp.clip(i - c * tiles, 0, tiles - 1), 0))
                 for c in range(MOE_CHUNKS)]
    return pl.pallas_call(
        functools.partial(_moe_down_kernel, layer=layer, chunk_tiles=tiles),
        grid_spec=pltpu.PrefetchScalarGridSpec(
            num_scalar_prefetch=len(plan),
            grid=(MOE_TILES,),
            in_specs=hid_specs + [hbm],
            out_specs=pl.BlockSpec((MOE_TM, D_MODEL // 2), row_blk),
            scratch_shapes=[pltpu.VMEM((2, 1, D_EXPERT, D_MODEL), F32), pltpu.VMEM((D_EXPERT, D_MODEL), BF16),
                            pltpu.SemaphoreType.DMA((2, 1))],
        ),
        out_shape=jax.ShapeDtypeStruct((MOE_ROWS, D_MODEL // 2), jnp.int32),
        compiler_params=_cparams(("arbitrary",), 40),
        name="moe_down",
    )(*plan, *hids, w_down)


PLAN_R = 2 * N_TOK // LANES


def _plan_kernel(e_ref, pos_ref, meta_ref, src_ref):
    e = e_ref[...]
    r_i = lax.broadcasted_iota(jnp.int32, (LANES, LANES), 0)
    c_i = lax.broadcasted_iota(jnp.int32, (LANES, LANES), 1)
    upper = jnp.where(r_i <= c_i, 1.0, 0.0).astype(BF16)
    r_j = lax.broadcasted_iota(jnp.int32, (PLAN_R, PLAN_R), 0)
    c_j = lax.broadcasted_iota(jnp.int32, (PLAN_R, PLAN_R), 1)
    lower = jnp.where(c_j < r_j, 1.0, 0.0).astype(BF16)
    lane = lax.broadcasted_iota(jnp.int32, (1, LANES), 1)
    pos = jnp.zeros((PLAN_R, LANES), F32)
    cnt_row = jnp.zeros((1, LANES), F32)
    off_row = jnp.zeros((1, LANES), F32)
    end_row, off_list, cnt_list = [], [], []
    row_off = jnp.zeros((1, 1), F32)
    tiles_done = jnp.zeros((1, 1), F32)
    for ex in range(N_EXPERTS):
        m = jnp.where(e == ex, 1.0, 0.0)
        inc = _dot(m.astype(BF16), upper)
        tot = jnp.broadcast_to(inc[:, LANES - 1:LANES], (PLAN_R, LANES))
        before = _dot(lower, tot.astype(BF16))
        cnt = jnp.sum(inc[:, LANES - 1:LANES], axis=0, keepdims=True)
        pos = pos + m * (row_off + before + inc - 1.0)
        tiles = jnp.floor((cnt + (MOE_TM - 1)) * (1.0 / MOE_TM))
        cnt_row = jnp.where(lane == ex, cnt, cnt_row)
        off_row = jnp.where(lane == ex, row_off, off_row)
        cnt_list.append(cnt)
        off_list.append(row_off)
        tiles_done = tiles_done + tiles
        end_row.append(tiles_done)
        row_off = row_off + tiles * MOE_TM
    n_active = tiles_done
    tid = jnp.minimum(lane.astype(F32), n_active - 1.0)
    te = jnp.zeros((1, LANES), F32)
    for ex in range(N_EXPERTS):
        te = te + jnp.where(end_row[ex] <= tid, 1.0, 0.0)
    pos_ref[...] = pos.astype(jnp.int32)
    meta = jnp.concatenate([cnt_row, off_row, te, jnp.broadcast_to(n_active, (1, LANES)),
                            jnp.zeros((4, LANES), F32)], axis=0)
    meta_ref[...] = meta.astype(jnp.int32)
    pr = MOE_ROWS // LANES
    q = lax.broadcasted_iota(jnp.int32, (pr, 1), 0)
    tile = jnp.minimum((q // (MOE_TM // LANES)).astype(F32), n_active - 1.0)
    e_q = jnp.zeros((pr, 1), F32)
    for ex in range(N_EXPERTS):
        e_q = e_q + jnp.where(end_row[ex] <= tile, 1.0, 0.0)
    off_q = jnp.zeros((pr, 1), F32)
    cnt_q = jnp.zeros((pr, 1), F32)
    start_q = jnp.zeros((pr, 1), F32)
    start = jnp.zeros((1, 1), F32)
    for ex in range(N_EXPERTS):
        hit = e_q == float(ex)
        off_q = jnp.where(hit, off_list[ex], off_q)
        cnt_q = jnp.where(hit, cnt_list[ex], cnt_q)
        start_q = jnp.where(hit, start, start_q)
        start = start + cnt_list[ex]
    r = (q * LANES + lane).astype(F32)
    rank = r - off_q
    spare = r - jnp.where(r >= float(N_TOK), float(N_TOK), 0.0) - jnp.where(r >= float(2 * N_TOK), float(N_TOK), 0.0)
    src_ref[...] = jnp.where(rank < cnt_q, start_q + rank, -(spare + 1.0)).astype(jnp.int32)


def _route_plan(eidx):
    e2 = eidx[:2].reshape(PLAN_R, LANES)
    pos, meta, src = pl.pallas_call(
        _plan_kernel,
        out_shape=[jax.ShapeDtypeStruct((PLAN_R, LANES), jnp.int32), jax.ShapeDtypeStruct((8, LANES), jnp.int32),
                   jax.ShapeDtypeStruct((MOE_ROWS // LANES, LANES), jnp.int32)],
        compiler_params=pltpu.CompilerParams(vmem_limit_bytes=32 * 1024 * 1024),
        name="route_plan",
    )(e2)
    te, n_active = meta[2, :MOE_TILES], meta[3, :1]
    order = jnp.argsort(e2.reshape(-1), stable=True).astype(jnp.int32)
    src = src.reshape(-1)
    src_tok = jnp.where(src >= 0, order[jnp.maximum(src, 0)] % N_TOK, -src - 1)
    return src_tok, te, n_active, pos.reshape(-1)


def _combine_kernel(x_ref, y0_ref, y1_ref, gt_ref, mt_ref, fg_ref, o_ref, *, final):
    gt = gt_ref[...]
    g0, g1 = gt[:, 0:1], gt[:, 1:2]
    half = D_MODEL // 2
    y0 = _unpack_bf16_pair(y0_ref[...])
    y1 = _unpack_bf16_pair(y1_ref[...])
    xs = []
    for c in range(2):
        cols = slice(c * half, (c + 1) * half)
        xs.append(x_ref[:, cols] + mt_ref[0, 5:6, cols] * (y0[c] * g0 + y1[c] * g1))
    if final:
        ssq = jnp.sum(xs[0] * xs[0], axis=-1, keepdims=True) + jnp.sum(xs[1] * xs[1], axis=-1, keepdims=True)
        r = lax.rsqrt(ssq * (1.0 / D_MODEL) + EPS)
        xs = [xc * r * fg_ref[:, c * half:(c + 1) * half] for c, xc in enumerate(xs)]
    for c, xc in enumerate(xs):
        o_ref[:, c * half:(c + 1) * half] = xc


def _combine(x, row0, n_rows, ysel, gates_t, mt, final_g, final):
    tm = 256
    b0 = row0 // tm
    return pl.pallas_call(
        functools.partial(_combine_kernel, final=final),
        grid=(n_rows // tm,),
        in_specs=[
            pl.BlockSpec((tm, D_MODEL), lambda i: (i + b0, 0)),
            pl.BlockSpec((None, tm, D_MODEL // 2), lambda i: (0, i, 0)),
            pl.BlockSpec((None, tm, D_MODEL // 2), lambda i: (1, i, 0)),
            pl.BlockSpec((tm, 8), lambda i: (i + b0, 0)),
            pl.BlockSpec((1, 6, D_MODEL), lambda i: ((i + b0) // (SEG // tm), 0, 0)),
            pl.BlockSpec((1, D_MODEL), lambda i: (0, 0)),
        ],
        out_specs=pl.BlockSpec((tm, D_MODEL), lambda i: (i, 0)),
        out_shape=jax.ShapeDtypeStruct((n_rows, D_MODEL), F32),
        compiler_params=_cparams(("arbitrary",), 40),
        name="moe_combine",
    )(x, ysel, ysel, gates_t, mt, final_g)


def _rope_tables(dim):
    half = dim // 2
    inv = ROPE_THETA ** (-jnp.arange(0, half, 2, dtype=F32) / half)
    t = jnp.arange(DEC_SEQ)
    ang_r = (t // GRID_W).astype(F32)[:, None] * inv[None, :]
    ang_c = (t % GRID_W).astype(F32)[:, None] * inv[None, :]
    ang = jnp.concatenate([ang_r, ang_r, ang_c, ang_c], axis=-1)
    cos, sin = jnp.cos(ang), jnp.sin(ang)
    reps = LANES // dim
    cos = jnp.tile(cos, (1, reps))
    sin = jnp.tile(sin, (1, reps))
    sh = dim // 4
    second = (np.arange(LANES) % (2 * sh)) >= sh
    sp = jnp.where(second[None, :], sin, 0.0)
    sm = jnp.where(second[None, :], 0.0, -sin)
    return cos, sp, sm


def _neighbourhood_bias(rpb):
    rows = DEC_SEQ // GRID_W
    kh = min(NA_ROWS, rows)
    r = np.arange(rows)
    r0 = np.clip(r - kh // 2, 0, rows - kh)
    kr = np.arange(rows)
    row_ok = (kr[None, :] >= r0[:, None]) & (kr[None, :] < r0[:, None] + kh)
    c = np.arange(GRID_W)
    ws = np.clip(c - NA_COLS // 2, 0, GRID_W - NA_COLS)
    kc = np.arange(GRID_W)
    col_ok = (kc[None, :] >= ws[:, None]) & (kc[None, :] < ws[:, None] + NA_COLS)
    dc_idx = np.clip(kc[None, :] - c[:, None], -(NA_COLS - 1), NA_COLS - 1) + NA_COLS - 1
    sel_c = ((np.arange(2 * NA_COLS - 1)[:, None, None] == dc_idx[None]) & col_ok[None]).astype(np.float32)
    t = jnp.einsum("hab,bcx->hacx", rpb.astype(F32), sel_c, precision=lax.Precision.HIGHEST)
    t = jnp.where(col_ok[None, None], t * LOG2E, NEG)
    t = jnp.concatenate([t, t], axis=-1)

    def build(t_ref, o_ref):
        left = lax.broadcasted_iota(jnp.int32, (1, LANES), 1) < GRID_W
        neg = jnp.full((GRID_W, LANES), NEG, F32)
        for rr in range(rows):
            for p in range(rows // 2):
                parts = [t_ref[k - rr + NA_ROWS - 1] if row_ok[rr, k] else neg for k in (2 * p, 2 * p + 1)]
                blk = jnp.where(left, parts[0], parts[1]) if (row_ok[rr, 2 * p] or row_ok[rr, 2 * p + 1]) else neg
                o_ref[rr * GRID_W:(rr + 1) * GRID_W, p * LANES:(p + 1) * LANES] = blk

    return pl.pallas_call(
        build,
        grid=(N_HEADS,),
        in_specs=[pl.BlockSpec((None, 2 * NA_ROWS - 1, GRID_W, LANES), lambda h: (h, 0, 0, 0))],
        out_specs=pl.BlockSpec((None, DEC_SEQ, DEC_SEQ), lambda h: (h, 0, 0)),
        out_shape=jax.ShapeDtypeStruct((N_HEADS, DEC_SEQ, DEC_SEQ), F32),
        compiler_params=_cparams(("arbitrary",), 32),
        name="nbr_bias",
    )(t)


def _even_w_in_layout(w, idx):
    cq_end = 3 * N_HEADS * HEAD_DIM + B_Q_RANK
    kv_end = cq_end + B_KV_RANK
    tk = 256

    def layout(w_ref, o_ref):
        o_ref[:, :cq_end] = w_ref[:, :cq_end].astype(BF16)
        o_ref[:, cq_end + 2 * LANES:] = w_ref[:, cq_end:kv_end].astype(BF16)
        kr = w_ref[:, kv_end:].astype(BF16)
        r = lax.broadcasted_iota(jnp.int32, (B_ROPE_DIM, 2 * LANES), 0)
        c = lax.broadcasted_iota(jnp.int32, (B_ROPE_DIM, 2 * LANES), 1)
        place = jnp.where(jnp.logical_or(c == r, c == r + 2 * LANES - B_ROPE_DIM), 1.0, 0.0).astype(BF16)
        o_ref[:, cq_end:cq_end + 2 * LANES] = _dot(kr, place).astype(BF16)

    return pl.pallas_call(
        layout,
        grid=(D_MODEL // tk,),
        in_specs=[pl.BlockSpec((None, tk, w.shape[2]), lambda i: (idx, i, 0))],
        out_specs=pl.BlockSpec((tk, PROJ_W), lambda i: (i, 0)),
        out_shape=jax.ShapeDtypeStruct((D_MODEL, PROJ_W), BF16),
        compiler_params=_cparams(("arbitrary",), 32),
        name="even_w_in_layout",
    )(w)


def _even_layer(x, mt, l, i, rope64, cache_a_k, cache_a_v, cache_b_ckv, cache_b_krope, norm_mix_g, ev_w_in,
                ev_lambda, ev_subln_g, ev_q_norm_g, ev_kv_norm_g, ev_w_uq, ev_w_ukv, ev_w_out):
    w_in = _even_w_in_layout(ev_w_in, i)
    all_g = (0, 1, 2, 3)
    a_scale = A_QK_DIM ** -0.5 * LOG2E
    in_kw = dict(prologue="modulate", g=norm_mix_g[l][None, :], norm_tile=8, norm_g=ev_kv_norm_g[i][None, :],
                 out_dtype=BF16, tile_scale={0: a_scale, 1: a_scale})
    bcq_src = (0, ((6, 0, TN), (7, 0, TN)))
    head_src = lambda t0: tuple((t0 + t, gq, 4 * t + gq) for t in range(2) for gq in range(4))
    proj_p, st_ak, st_av, st_ckv, st_kr, bcq_p = _projection(
        *x["ctx"], N_SEG_P, D_MODEL, 0, D_MODEL, w_in, mt=mt[:N_SEG_P], name="even_in_proj_ctx",
        states=((N_HEADS, head_src(2)), (N_HEADS, head_src(4)), (0, ((8, 0, B_KV_RANK),)), (0, ((7, 256, B_ROPE_DIM),)),
                bcq_src), **in_kw)
    proj_l, bcq_l = _projection(*x["lat"], DEC_BATCH, D_MODEL, 0, D_MODEL, w_in, mt=mt[N_SEG_P:], rope=rope64,
                                sh=A_QK_DIM // 4, rope_groups={0: all_g, 1: all_g, 2: all_g, 3: all_g, 7: (2, 3)},
                                states=(bcq_src,), name="even_in_proj_lat", **in_kw)
    wq = ev_w_uq[i].reshape(B_Q_RANK, N_HEADS, HEAD_DIM + B_ROPE_DIM)
    w_uq = jnp.concatenate([wq[:, :, :HEAD_DIM].reshape(B_Q_RANK, -1), wq[:, :, HEAD_DIM:].reshape(B_Q_RANK, -1)],
                           axis=1).astype(BF16)
    mla_scale = (HEAD_DIM + B_ROPE_DIM) ** -0.5
    q_kw = dict(prologue="rmsnorm", g=ev_q_norm_g[i][None, :], out_dtype=BF16, tn=w_uq.shape[1],
                tile_scale={0: mla_scale * LOG2E})
    bq_p = _projection(bcq_p, 0, N_SEG_P, 1024, 0, B_Q_RANK, w_uq, name="mla_q_up_ctx", **q_kw)
    bq_l = _projection(bcq_l, 0, DEC_BATCH, 1024, 0, B_Q_RANK, w_uq, rope=rope64, sh=B_ROPE_DIM // 4,
                       rope_groups={0: (8, 9, 10, 11)}, name="mla_q_up_lat", **q_kw)
    w_ukv = ev_w_ukv[i].astype(BF16)
    kv_kw = dict(prologue="cast", out_dtype=BF16, tn=w_ukv.shape[1])
    kv_p = _projection(proj_p, 0, N_SEG_P, 512, 8, B_KV_RANK, w_ukv, name="mla_kv_up_ctx", **kv_kw)
    kv_l = _projection(proj_l, 0, DEC_BATCH, 512, 8, B_KV_RANK, w_ukv, name="mla_kv_up_lat", **kv_kw)
    kv_ctx = _projection(cache_b_ckv[:, i].reshape(DEC_BATCH * PAST_LEN, B_KV_RANK), 0, 1, 512, 0, B_KV_RANK, w_ukv,
                         name="mla_kv_up_cache", **kv_kw)
    kv_ctx = kv_ctx.reshape(DEC_BATCH, PAST_LEN, 2 * N_HEADS * HEAD_DIM)
    krc = cache_b_krope[:, i]
    zc = jnp.zeros_like(krc)
    kr_ctx = jnp.concatenate([krc, zc, zc, krc], axis=-1)
    ak_ctx = cache_a_k[:, i].reshape(DEC_BATCH, PAST_LEN, N_HEADS * HEAD_DIM)
    av_ctx = cache_a_v[:, i].reshape(DEC_BATCH, PAST_LEN, N_HEADS * HEAD_DIM)
    lam_init = 0.8 - 0.6 * math.exp(-0.3 * l)
    o = jnp.zeros((N_TOK, D_MODEL), BF16)
    for latent, proj, b_q, kv in ((False, proj_p, bq_p, kv_p), (True, proj_l, bq_l, kv_l)):
        tag = "lat" if latent else "ctx"
        o = _attention(kind="diff", latent=latent, q_parts=[(proj, lambda h: h)], ko_parts=[(proj, lambda h: 8 + h)],
                       vo=(proj, lambda h: 16 + h), kc_parts=[(ak_ctx, lambda h: h)], vc=(av_ctx, lambda h: h),
                       lam=ev_lambda[i], subln=ev_subln_g[i][None, :], o_arr=o, o_blk=lambda h: h,
                       lam_init=lam_init, name="diff_attn_" + tag)
        o = _attention(kind="mla", latent=latent, q_parts=[(b_q, lambda h: h), (b_q, lambda h: 8 + h // 2)],
                       ko_parts=[(kv, lambda h: 2 * h), (proj, lambda h: 30 + h % 2)], vo=(kv, lambda h: 2 * h + 1),
                       kc_parts=[(kv_ctx, lambda h: 2 * h), (kr_ctx, lambda h: h % 2)], vc=(kv_ctx, lambda h: 2 * h + 1),
                       o_arr=o, o_blk=lambda h: 8 + h, name="mla_attn_" + tag)
    states = (st_ak.reshape(BATCH, SEQ, N_HEADS, HEAD_DIM), st_av.reshape(BATCH, SEQ, N_HEADS, HEAD_DIM),
              st_ckv.reshape(BATCH, SEQ, B_KV_RANK), st_kr.reshape(BATCH, SEQ, B_ROPE_DIM))
    return (o, ev_w_out[i].astype(BF16)), states


def _odd_layer(x, mt, l, i, rope128, cache_c_k, cache_c_v, cache_d_k, cache_d_v, norm_mix_g, od_w_in, od_rpb,
               od_sink, od_w_out):
    w_in = od_w_in[i].astype(BF16)
    all_g = (0, 1, 2, 3)
    q_scale = HEAD_DIM ** -0.5 * LOG2E
    in_kw = dict(prologue="modulate", g=norm_mix_g[l][None, :], out_dtype=BF16,
                 tile_scale={t: q_scale for t in (0, 1, 6, 7)})
    head_src = lambda t0: tuple((t0 + t, gq, 4 * t + gq) for t in range(2) for gq in range(4))
    proj_p, st_ck, st_cv, st_dk, st_dv = _projection(
        *x["ctx"], N_SEG_P, D_MODEL, 0, D_MODEL, w_in, mt=mt[:N_SEG_P], name="odd_in_proj_ctx",
        states=((N_HEADS, head_src(2)), (N_HEADS, head_src(4)), (D_KV_HEADS, ((8, 0, 0), (8, 1, 1))),
                (D_KV_HEADS, ((8, 2, 0), (8, 3, 1)))), **in_kw)
    proj_l = _projection(*x["lat"], DEC_BATCH, D_MODEL, 0, D_MODEL, w_in, mt=mt[N_SEG_P:], rope=rope128,
                         sh=HEAD_DIM // 4, rope_groups={6: all_g, 7: all_g, 8: (0, 1)}, name="odd_in_proj_lat", **in_kw)
    ck_ctx = cache_c_k[:, i].reshape(DEC_BATCH, PAST_LEN, N_HEADS * HEAD_DIM)
    cv_ctx = cache_c_v[:, i].reshape(DEC_BATCH, PAST_LEN, N_HEADS * HEAD_DIM)
    dk_ctx = cache_d_k[:, i].reshape(DEC_BATCH, PAST_LEN, D_KV_HEADS * HEAD_DIM)
    dv_ctx = cache_d_v[:, i].reshape(DEC_BATCH, PAST_LEN, D_KV_HEADS * HEAD_DIM)
    bias = _neighbourhood_bias(od_rpb[i])
    sink = od_sink[i].astype(F32)
    o = jnp.zeros((N_TOK, D_MODEL), BF16)
    for latent, proj in ((False, proj_p), (True, proj_l)):
        tag = "lat" if latent else "ctx"
        o = _attention(kind="plain", latent=latent, q_parts=[(proj, lambda h: h)], ko_parts=[(proj, lambda h: 8 + h)],
                       vo=(proj, lambda h: 16 + h), kc_parts=[(ck_ctx, lambda h: h)], vc=(cv_ctx, lambda h: h),
                       bias=bias if latent else None, o_arr=o, o_blk=lambda h: h,
                       name="nbr_attn_" + tag)
        o = _attention(kind="plain", latent=latent, q_parts=[(proj, lambda h: 24 + h)],
                       ko_parts=[(proj, lambda h: 32 + h // D_GROUP)], vo=(proj, lambda h: 34 + h // D_GROUP),
                       kc_parts=[(dk_ctx, lambda h: h // D_GROUP)], vc=(dv_ctx, lambda h: h // D_GROUP), sink=sink,
                       o_arr=o, o_blk=lambda h: 8 + h, has_band=latent, tq_lat=256,
                       name="win_attn_" + tag)
    states = (st_ck.reshape(BATCH, SEQ, N_HEADS, HEAD_DIM), st_cv.reshape(BATCH, SEQ, N_HEADS, HEAD_DIM),
              st_dk.reshape(BATCH, SEQ, D_KV_HEADS, HEAD_DIM), st_dv.reshape(BATCH, SEQ, D_KV_HEADS, HEAD_DIM))
    return (o, od_w_out[i].astype(BF16)), states


def _moe_layer(mix, x_parts, mt, g, router_w, router_b, layer, w_gate, w_up, w_down, final_g, final):
    x, h2, eidx, gates = _out_projection_router(*mix, x_parts, mt, g, router_w, router_b)
    src_tok, te, n_active, pos = _route_plan(eidx)
    ys = _moe_experts(h2, src_tok, te, n_active, layer, w_gate, w_up, w_down)
    pos2 = pos.reshape(2, N_TOK)
    gates_t = gates.T
    outs = []
    for row0, n_rows in ((0, N_PROMPT), (N_PROMPT, N_TOK - N_PROMPT)):
        ysel = ys.at[pos2[:, row0:row0 + n_rows].reshape(-1)].get(mode="promise_in_bounds")
        outs.append(_combine(x, row0, n_rows, ysel.reshape(2, n_rows, D_MODEL // 2), gates_t, mt, final_g, final))
    return outs


def kernel(x_prompt, x_sample, cache_a_k, cache_a_v, cache_b_ckv, cache_b_krope, cache_c_k, cache_c_v, cache_d_k, cache_d_v, c, c_ctx, w_ada, b_ada, norm_mix_g, norm_ffn_g, ev_w_in, ev_lambda, ev_subln_g, ev_q_norm_g, ev_kv_norm_g, ev_w_uq, ev_w_ukv, ev_w_out, od_w_in, od_rpb, od_sink, od_w_out, router_w, router_b, moe_w_gate, moe_w_up, moe_w_down, final_g):
    xp = x_prompt.reshape(N_PROMPT, D_MODEL)
    xl = x_sample.reshape(DEC_BATCH * DEC_SEQ, D_MODEL)
    x = dict(ctx=(xp, 0), lat=(xl, 0), parts=[xp, xl])
    cond8 = jnp.concatenate([c_ctx[None, :], c, jnp.zeros((3, D_MODEL), F32)], axis=0)
    mod = _adaln(cond8, w_ada, b_ada)
    seg_row = np.array([0] * N_SEG_P + [1 + b for b in range(DEC_BATCH)])
    mt_all = mod[:, seg_row].reshape(DEPTH, N_SEG, 6, D_MODEL)
    rope64 = _rope_tables(A_QK_DIM)
    rope128 = _rope_tables(HEAD_DIM)
    even_states, odd_states = [], []
    for l in range(DEPTH):
        i = l // 2
        mt = mt_all[l]
        if l % 2 == 0:
            mix, st = _even_layer(x, mt, l, i, rope64, cache_a_k, cache_a_v, cache_b_ckv, cache_b_krope, norm_mix_g,
                                ev_w_in, ev_lambda, ev_subln_g, ev_q_norm_g, ev_kv_norm_g, ev_w_uq, ev_w_ukv, ev_w_out)
            even_states.append(st)
        else:
            mix, st = _odd_layer(x, mt, l, i, rope128, cache_c_k, cache_c_v, cache_d_k, cache_d_v, norm_mix_g,
                               od_w_in, od_rpb, od_sink, od_w_out)
            odd_states.append(st)
        x = _moe_layer(mix, x["parts"], mt, norm_ffn_g[l][None, :], router_w, router_b, l, moe_w_gate, moe_w_up, moe_w_down,
                       final_g[None, :], final=(l == DEPTH - 1))
        if l < DEPTH - 1:
            x = dict(ctx=(x[0], 0), lat=(x[1], 0), parts=list(x))
    y_prompt = x[0].reshape(BATCH, SEQ, D_MODEL)
    y_sample = x[1].reshape(DEC_BATCH, DEC_SEQ, D_MODEL)
    new_even = tuple(jnp.stack([st[k] for st in even_states], axis=1) for k in range(4))
    new_odd = tuple(jnp.stack([st[k] for st in odd_states], axis=1) for k in range(4))
    return (y_prompt, y_sample) + new_even + new_odd
```

```python
import functools
import math

import numpy as np
import jax
import jax.numpy as jnp
from jax import lax
from jax.experimental import pallas as pl
from jax.experimental.pallas import tpu as pltpu

D_MODEL = 2048
BATCH = 32
SEQ = 256
DEPTH = 2
DEC_BATCH = 4
DEC_SEQ = 1024
PAST_LEN = 256
GRID_W = 64
HEAD_DIM = 128
N_HEADS = 8
A_QK_DIM = 64
B_Q_RANK = 768
B_KV_RANK = 512
B_ROPE_DIM = 64
NA_ROWS = 8
NA_COLS = 16
D_KV_HEADS = 2
D_GROUP = 4
D_WINDOW = 128
N_EXPERTS = 16
N_GROUPS = 4
D_EXPERT = 1024
ROPE_THETA = 10000.0
EPS = 1e-6
NEG = -1e30
LOG2E = 1.4426950408889634

SEG = 1024
N_PROMPT = BATCH * SEQ
N_TOK = N_PROMPT + DEC_BATCH * DEC_SEQ
N_SEG = N_TOK // SEG
N_SEG_P = N_PROMPT // SEG
PROJ_W = 4608
TN = 512
LANES = 128
MOE_TM = 256
MOE_ROWS = 2 * N_TOK + N_EXPERTS * MOE_TM
MOE_TILES = MOE_ROWS // MOE_TM

F32 = jnp.float32
BF16 = jnp.bfloat16


def _cparams(sem, vmem_mb):
    return pltpu.CompilerParams(dimension_semantics=sem, vmem_limit_bytes=vmem_mb * 1024 * 1024)


def _dot(a, b):
    return jnp.dot(a, b, preferred_element_type=F32)


def _dot_nt(a, b):
    return lax.dot_general(a, b, (((1,), (1,)), ((), ())), preferred_element_type=F32)


def _sigmoid(x):
    return 1.0 / (1.0 + jnp.exp(-x))


def _pack_bf16_pair(x):
    c = x.shape[1] // 2
    bits = pltpu.bitcast(x.astype(BF16).astype(F32), jnp.int32)
    return bits[:, :c] | lax.shift_right_logical(bits[:, c:], 16)


def _unpack_bf16_pair(w):
    hi = pltpu.bitcast(w & jnp.int32(-65536), F32)
    lo = pltpu.bitcast(lax.shift_left(w, 16), F32)
    return hi, lo


def _adaln_kernel(c_ref, w_ref, b_ref, o_ref):
    c = c_ref[...]
    a = (c * _sigmoid(c)).astype(BF16)
    o_ref[...] = _dot(a, w_ref[...].astype(BF16)) + b_ref[...]


def _adaln(cond8, w_ada, b_ada):
    tn = 1024
    n = 6 * D_MODEL
    return pl.pallas_call(
        _adaln_kernel,
        grid=(DEPTH, n // tn),
        in_specs=[
            pl.BlockSpec((8, D_MODEL), lambda l, j: (0, 0)),
            pl.BlockSpec((None, D_MODEL, tn), lambda l, j: (l, 0, j)),
            pl.BlockSpec((None, 1, tn), lambda l, j: (l, 0, j)),
        ],
        out_specs=pl.BlockSpec((None, 8, tn), lambda l, j: (l, 0, j)),
        out_shape=jax.ShapeDtypeStruct((DEPTH, 8, n), F32),
        compiler_params=_cparams(("arbitrary", "arbitrary"), 40),
        name="adaln",
    )(cond8, w_ada, b_ada.reshape(DEPTH, 1, n))


def _rope(a, cos, sp, sm, sh):
    return a * cos + pltpu.roll(a, sh, 1) * sp + pltpu.roll(a, LANES - sh, 1) * sm


def _proj_kernel(*refs, prologue, k_in, sh, n_tiles, rope_groups, norm_tile, states, tile_scale):
    x_ref, mt_ref, g_ref, w_ref, cos_ref, sp_ref, sm_ref, ng_ref, o_ref = refs[:9]
    st_refs = refs[9:-1]
    xn_ref = refs[-1]
    j = pl.program_id(1)

    @pl.when(j == 0)
    def _():
        x = x_ref[:, :k_in]
        if prologue == "cast":
            xn_ref[...] = x.astype(BF16)
        else:
            r = lax.rsqrt(jnp.mean(x * x, axis=-1, keepdims=True) + EPS)
            if prologue == "modulate":
                y = x * r * (g_ref[...] * (1.0 + mt_ref[0, 1:2, :])) + mt_ref[0, 0:1, :]
            else:
                y = x * r * g_ref[...]
            xn_ref[...] = y.astype(BF16)

    acc = _dot(xn_ref[...], w_ref[...])
    n_grp = acc.shape[1] // LANES

    def treatment(t):
        acts = []
        for k, (n_heads, src) in enumerate(states):
            dest = 0
            for tt, a, b in src:
                if tt == t:
                    acts.append((k, a, b, dest))
                dest += 0 if n_heads else b
        return (tuple(rope_groups.get(t, ())), t == norm_tile, tuple(acts), float(tile_scale.get(t, 1.0)))

    branches = {}
    for t in range(n_tiles):
        branches.setdefault(treatment(t), []).append(t)

    o_ref[...] = acc.astype(o_ref.dtype)
    reread = o_ref.dtype == F32

    for (rg, is_norm, acts, sc), tiles in branches.items():
        if not rg and not is_norm and not acts and sc == 1.0:
            continue
        cond = j == tiles[0]
        for t in tiles[1:]:
            cond = jnp.logical_or(cond, j == t)

        @pl.when(cond)
        def _(rg=rg, is_norm=is_norm, acts=acts, sc=sc):
            vals = [(o_ref if reread else acc)[:, g * LANES:(g + 1) * LANES] for g in range(n_grp)]
            if sc != 1.0:
                vals = [v * sc for v in vals]
            if is_norm:
                ssq = None
                for v in vals:
                    s = jnp.sum(v * v, axis=-1, keepdims=True)
                    ssq = s if ssq is None else ssq + s
                scale = lax.rsqrt(ssq * (1.0 / (n_grp * LANES)) + EPS)
                ng = ng_ref[...]
                vals = [v * scale * ng[:, g * LANES:(g + 1) * LANES] for g, v in enumerate(vals)]
            elif rg:
                cos, sp, sm = cos_ref[...], sp_ref[...], sm_ref[...]
                vals = [_rope(v, cos, sp, sm, sh) if g in rg else v for g, v in enumerate(vals)]
            if is_norm or rg or sc != 1.0:
                for g, v in enumerate(vals):
                    if is_norm or sc != 1.0 or g in rg:
                        o_ref[:, g * LANES:(g + 1) * LANES] = v.astype(o_ref.dtype)
            for k, a, b, dest in acts:
                n_heads = states[k][0]
                if n_heads:
                    st_refs[k][pl.ds(b, SEG, stride=n_heads), :] = vals[a]
                elif b % LANES == 0:
                    for q in range(b // LANES):
                        st_refs[k][:, dest + q * LANES:dest + (q + 1) * LANES] = vals[a // LANES + q]
                else:
                    st_refs[k][...] = vals[a // LANES][:, a % LANES:a % LANES + b]


def _projection(x, row_off, n_seg, x_blk_w, x_blk_idx, k_in, w, *, prologue, mt=None, g=None, rope=None, sh=16,
                rope_groups=None, norm_tile=None, norm_g=None, states=(), out_dtype=F32, tile_scale=None, tn=TN,
                name):
    n = w.shape[1]
    m = n_seg * SEG
    assert n % tn == 0 and w.shape[0] == k_in
    if mt is None:
        mt = jnp.zeros((n_seg, 6, LANES), F32)
    if g is None:
        g = jnp.ones((1, k_in), F32)
    if rope is None:
        rope = tuple(jnp.zeros((8, LANES), F32) for _ in range(3))
    if norm_g is None:
        norm_g = jnp.ones((1, tn), F32)
    kern = functools.partial(_proj_kernel, prologue=prologue, k_in=k_in, sh=sh, n_tiles=n // tn,
                             rope_groups=dict(rope_groups or {}), norm_tile=norm_tile, states=tuple(states),
                             tile_scale=dict(tile_scale or {}))
    full2 = lambda i, j: (0, 0)
    out_shape = [jax.ShapeDtypeStruct((m, n), out_dtype)]
    out_specs = [pl.BlockSpec((SEG, tn), lambda i, j: (i, j))]
    for n_heads, src in states:
        if n_heads:
            out_shape.append(jax.ShapeDtypeStruct((m * n_heads, LANES), F32))
            out_specs.append(pl.BlockSpec((SEG * n_heads, LANES), lambda i, j: (i, 0)))
        else:
            width = sum(b for _, _, b in src)
            out_shape.append(jax.ShapeDtypeStruct((m, width), F32))
            out_specs.append(pl.BlockSpec((SEG, width), lambda i, j: (i, 0)))
    x_kw = dict(pipeline_mode=pl.Buffered(1)) if states else {}
    res = pl.pallas_call(
        kern,
        grid=(n_seg, n // tn),
        in_specs=[
            pl.BlockSpec((SEG, x_blk_w), lambda i, j: (i + row_off, x_blk_idx), **x_kw),
            pl.BlockSpec((1, 6, mt.shape[2]), lambda i, j: (i, 0, 0)),
            pl.BlockSpec((1, k_in), full2),
            pl.BlockSpec((k_in, tn), lambda i, j: (0, j)),
            pl.BlockSpec(rope[0].shape, full2),
            pl.BlockSpec(rope[1].shape, full2),
            pl.BlockSpec(rope[2].shape, full2),
            pl.BlockSpec((1, tn), full2),
        ],
        out_specs=out_specs,
        out_shape=out_shape,
        scratch_shapes=[pltpu.VMEM((SEG, k_in), BF16)],
        compiler_params=_cparams(("arbitrary", "arbitrary"), 56 if states else 48),
        name=name,
    )(x, mt, g, w, *rope, norm_g)
    return res if states else res[0]


def _attn_kernel(*refs, kind, n_parts, has_ctx, has_bias, has_band, has_sink, nb, s_len, tq, lam_init, own_win):
    it = iter(refs)
    q_refs = [next(it) for _ in range(n_parts)]
    ko_refs = [next(it) for _ in range(n_parts)]
    vo_ref = next(it)
    kc_refs, vc_ref = [], None
    if has_ctx:
        kc_refs = [next(it) for _ in range(n_parts)]
        vc_ref = next(it)
    bias_ref = next(it) if has_bias else None
    lam_ref = sg_ref = None
    if kind == "diff":
        lam_ref = next(it)
        sg_ref = next(it)
    sink_ref = next(it) if has_sink else None
    o_ref = next(it)

    sink2 = sink_ref[pl.program_id(1)] * LOG2E if has_sink else None
    lane = lax.broadcasted_iota(jnp.int32, (1, LANES), 1)

    def softmax_pv(s_list, v_list):
        m = None
        for s in s_list:
            mm = jnp.max(s, axis=-1, keepdims=True)
            m = mm if m is None else jnp.maximum(m, mm)
        if sink2 is not None:
            m = jnp.maximum(m, sink2)
        l = jnp.exp2(sink2 - m) if sink2 is not None else None
        o = None
        for s, v in zip(s_list, v_list):
            e = jnp.exp2(s - m)
            ss = jnp.sum(e, axis=-1, keepdims=True)
            l = ss if l is None else l + ss
            pv = _dot(e.astype(BF16), v)
            o = pv if o is None else o + pv
        return o, l

    for bi in range(nb):
        qrows = pl.ds(bi * tq, tq)
        q0 = (pl.program_id(2) * nb + bi) * tq
        if own_win is None:
            krows, k0 = pl.ds(0 if has_ctx else bi * s_len, s_len), 0
            n_own = s_len
        else:
            k0 = pl.multiple_of(jnp.clip(q0 - D_WINDOW, 0, s_len - own_win), LANES)
            krows = pl.ds(k0, own_win)
            n_own = own_win
        k_srcs, v_srcs, is_own = [], [], []
        if has_ctx:
            k_srcs.append([r[...].astype(BF16) for r in kc_refs])
            v_srcs.append(vc_ref[...].astype(BF16))
            is_own.append(False)
        k_srcs.append([r[krows, :].astype(BF16) for r in ko_refs])
        v_srcs.append(vo_ref[krows, :].astype(BF16))
        is_own.append(True)

        def mask_own(s):
            if has_bias:
                s = s + bias_ref[qrows, :]
            if has_band:
                qpos = q0 + lax.broadcasted_iota(jnp.int32, (tq, 1), 0)
                kpos = k0 + lax.broadcasted_iota(jnp.int32, (1, n_own), 1)
                s = jnp.where(jnp.abs(kpos - qpos) <= D_WINDOW, s, NEG)
            return s

        if kind == "diff":
            q = q_refs[0][qrows, :]
            zero = jnp.zeros_like(q)
            q1 = jnp.where(lane < A_QK_DIM, q, zero)
            q2 = jnp.where(lane < A_QK_DIM, zero, q)
            o1, l1 = softmax_pv([_dot_nt(q1, ks[0]) for ks in k_srcs], v_srcs)
            o2, l2 = softmax_pv([_dot_nt(q2, ks[0]) for ks in k_srcs], v_srcs)
            lv = lam_ref[...]
            lam = (jnp.exp(jnp.sum(lv[0:1] * lv[1:2], axis=-1, keepdims=True))
                   - jnp.exp(jnp.sum(lv[2:3] * lv[3:4], axis=-1, keepdims=True)) + lam_init)
            o = o1 * (1.0 / l1) - o2 * (lam / l2)
            o = o * lax.rsqrt(jnp.mean(o * o, axis=-1, keepdims=True) + EPS) * sg_ref[...] * (1.0 - lam_init)
        else:
            qs = [r[qrows, :] for r in q_refs]
            s_list = []
            for ks, own in zip(k_srcs, is_own):
                s = None
                for qp, kp in zip(qs, ks):
                    d = _dot_nt(qp, kp)
                    s = d if s is None else s + d
                s_list.append(mask_own(s) if own else s)
            o, l = softmax_pv(s_list, v_srcs)
            o = o * (1.0 / l)
        o_ref[qrows, :] = o.astype(o_ref.dtype)


def _attention(*, kind, latent, q_parts, ko_parts, vo, kc_parts=(), vc=None, bias=None, lam=None, subln=None,
               sink=None, o_arr, o_blk, lam_init=0.0, has_band=False, tq_lat=512, name):
    n_parts = len(q_parts)
    has_ctx = latent
    if latent:
        tq, s_len, nb = tq_lat, DEC_SEQ, DEC_SEQ // tq_lat
        qblk, kblk = nb * tq, SEG
        grid = (DEC_BATCH, N_HEADS, DEC_SEQ // qblk)
        qpb = DEC_SEQ // qblk
        q_row = lambda b, h, t: b * qpb + t
        o_row = lambda b, h, t: N_PROMPT // qblk + b * qpb + t
        k_row = lambda b, h, t: b
    else:
        qblk = kblk = 4 * SEG
        tq, s_len, nb = SEQ, SEQ, qblk // SEQ
        grid = (N_PROMPT // qblk, N_HEADS, 1)
        q_row = o_row = k_row = lambda b, h, t: b
    args, specs = [], []

    def add(arr, shape, imap, **kw):
        args.append(arr)
        specs.append(pl.BlockSpec(shape, imap, **kw))

    for arr, f in q_parts:
        add(arr, (qblk, LANES), lambda b, h, t, f=f: (q_row(b, h, t), f(h)))
    for arr, f in list(ko_parts) + [vo]:
        add(arr, (kblk, LANES), lambda b, h, t, f=f: (k_row(b, h, t), f(h)))
    if has_ctx:
        for arr, f in list(kc_parts) + [vc]:
            add(arr, (None, PAST_LEN, LANES), lambda b, h, t, f=f: (b, 0, f(h)))
    if bias is not None:
        add(bias, (None, qblk, DEC_SEQ), lambda b, h, t: (h, t, 0))
    if kind == "diff":
        add(lam, lam.shape, lambda b, h, t: (0, 0))
        add(subln, subln.shape, lambda b, h, t: (0, 0))
    if sink is not None:
        args.append(sink)
        specs.append(pl.BlockSpec(memory_space=pltpu.SMEM))
    n_in = len(args)
    args.append(o_arr)
    specs.append(pl.BlockSpec(memory_space=pl.ANY))
    kern = functools.partial(_attn_kernel_aliased, kind=kind, n_parts=n_parts, has_ctx=has_ctx,
                             has_bias=bias is not None, has_band=has_band, has_sink=sink is not None,
                             nb=nb, s_len=s_len, tq=tq, lam_init=lam_init,
                             own_win=(tq + 2 * D_WINDOW) if has_band else None)
    return pl.pallas_call(
        kern,
        grid=grid,
        in_specs=specs,
        out_specs=pl.BlockSpec((qblk, LANES), lambda b, h, t: (o_row(b, h, t), o_blk(h))),
        out_shape=jax.ShapeDtypeStruct(o_arr.shape, o_arr.dtype),
        input_output_aliases={n_in: 0},
        compiler_params=_cparams(("arbitrary", "arbitrary", "arbitrary"), 48),
        name=name,
    )(*args)


def _attn_kernel_aliased(*refs, **kw):
    _attn_kernel(*refs[:-2], refs[-1], **kw)


def _ffn_norm_route(x, mt_ref, g_ref, w_ref, rb_ref):
    r = lax.rsqrt(jnp.mean(x * x, axis=-1, keepdims=True) + EPS)
    h = x * r * (g_ref[...] * (1.0 + mt_ref[0, 4:5, :])) + mt_ref[0, 3:4, :]
    hh = h.astype(BF16)
    packed = _pack_bf16_pair(hh)
    logits = _dot_nt(w_ref[...], hh)
    scores = _sigmoid(logits)
    sel = scores + rb_ref[...]
    per = N_EXPERTS // N_GROUPS
    sc = [scores[e:e + 1, :] for e in range(N_EXPERTS)]
    sl = [sel[e:e + 1, :] for e in range(N_EXPERTS)]
    best_g, best_v = None, None
    for gi in range(N_GROUPS):
        a, b, c, d = sl[gi * per:(gi + 1) * per]
        hi1, lo1, hi2, lo2 = jnp.maximum(a, b), jnp.minimum(a, b), jnp.maximum(c, d), jnp.minimum(c, d)
        gs = jnp.maximum(hi1, hi2) + jnp.maximum(jnp.minimum(hi1, hi2), jnp.maximum(lo1, lo2))
        if gi == 0:
            best_g, best_v = jnp.zeros_like(gs, dtype=jnp.int32), gs
        else:
            better = gs > best_v
            best_g = jnp.where(better, gi, best_g)
            best_v = jnp.where(better, gs, best_v)
    masked = [jnp.where(best_g == (e // per), sl[e], NEG) for e in range(N_EXPERTS)]
    i1, v1 = jnp.zeros_like(best_g), masked[0]
    for e in range(1, N_EXPERTS):
        better = masked[e] > v1
        i1 = jnp.where(better, e, i1)
        v1 = jnp.where(better, masked[e], v1)
    i2, v2 = None, None
    for e in range(N_EXPERTS):
        cand = jnp.where(i1 == e, -2e30, masked[e])
        if e == 0:
            i2, v2 = jnp.zeros_like(best_g), cand
        else:
            better = cand > v2
            i2 = jnp.where(better, e, i2)
            v2 = jnp.where(better, cand, v2)
    g1 = jnp.zeros_like(v1)
    g2 = jnp.zeros_like(v1)
    for e in range(N_EXPERTS):
        g1 = jnp.where(i1 == e, sc[e], g1)
        g2 = jnp.where(i2 == e, sc[e], g2)
    tot = g1 + g2
    rows = i1.shape[1]
    ids = jnp.concatenate([i1, i2, jnp.zeros((6, rows), jnp.int32)], axis=0)
    gates = jnp.concatenate([g1 / tot, g2 / tot, jnp.zeros((6, rows), F32)], axis=0)
    return packed, ids, gates


def _outproj_router_kernel(o_ref, w_ref, mt_ref, g_ref, rw_ref, rb_ref, *rest, blk_ranges, units):
    x_refs = rest[:-4]
    y_ref, h_ref, e_ref, gt_ref = rest[-4:]
    i = pl.program_id(0)
    n = o_ref.shape[0] // units
    for u in range(units):
        rows = pl.ds(u * n, n)
        x = x_refs[0][rows, :]
        for x_ref, (lo, _) in zip(x_refs[1:], blk_ranges[1:]):
            x = jnp.where(i >= lo, x_ref[rows, :], x)
        x1 = x + mt_ref[0, 2:3, :] * _dot(o_ref[rows, :], w_ref[...])
        y_ref[rows, :] = x1
        packed, ids, gates = _ffn_norm_route(x1, mt_ref, g_ref, rw_ref, rb_ref)
        h_ref[rows, :] = packed
        e_ref[:, u * n:(u + 1) * n] = ids
        gt_ref[:, u * n:(u + 1) * n] = gates


def _out_projection_router(o, w, x_parts, mt, g, router_w, router_b):
    units = 2
    tm = units * 256
    blk_ranges, lo = [], 0
    for xp in x_parts:
        blk_ranges.append((lo, lo + xp.shape[0] // tm))
        lo = blk_ranges[-1][1]
    assert lo == N_TOK // tm
    x_specs = [pl.BlockSpec((tm, D_MODEL), lambda i, lo=lo, hi=hi: (jnp.clip(i - lo, 0, hi - lo - 1), 0))
               for lo, hi in blk_ranges]
    const = lambda i: (0, 0)
    return pl.pallas_call(
        functools.partial(_outproj_router_kernel, blk_ranges=tuple(blk_ranges), units=units),
        grid=(N_TOK // tm,),
        in_specs=[
            pl.BlockSpec((tm, D_MODEL), lambda i: (i, 0)),
            pl.BlockSpec((D_MODEL, D_MODEL), const, pipeline_mode=pl.Buffered(1)),
            pl.BlockSpec((1, 6, D_MODEL), lambda i: (i // (SEG // tm), 0, 0)),
            pl.BlockSpec((1, D_MODEL), const),
            pl.BlockSpec((N_EXPERTS, D_MODEL), const),
            pl.BlockSpec((N_EXPERTS, 1), const),
        ] + x_specs,
        out_specs=[
            pl.BlockSpec((tm, D_MODEL), lambda i: (i, 0)),
            pl.BlockSpec((tm, D_MODEL // 2), lambda i: (i, 0)),
            pl.BlockSpec((8, tm), lambda i: (0, i)),
            pl.BlockSpec((8, tm), lambda i: (0, i)),
        ],
        out_shape=[
            jax.ShapeDtypeStruct((N_TOK, D_MODEL), F32),
            jax.ShapeDtypeStruct((N_TOK, D_MODEL // 2), jnp.int32),
            jax.ShapeDtypeStruct((8, N_TOK), jnp.int32),
            jax.ShapeDtypeStruct((8, N_TOK), F32),
        ],
        compiler_params=_cparams(("arbitrary",), 56),
        name="out_proj_router",
    )(o, w, mt, g, router_w.T.astype(BF16), router_b.reshape(N_EXPERTS, 1), *x_parts)


def _expert_weights(te_ref, first_ref, slot_ref, nxt_ref, i, t, layer, w_hbm, stage_ref, sem_ref, w_bf_refs):
    def copies(e, s):
        return [pltpu.make_async_copy(w.at[layer, e], stage_ref.at[s, k], sem_ref.at[s, k])
                for k, w in enumerate(w_hbm)]

    s = slot_ref[t]

    @pl.when(i == 0)
    def _():
        for c in copies(te_ref[t], s):
            c.start()

    @pl.when(first_ref[t] == 1)
    def _():
        for c in copies(te_ref[t], s):
            c.wait()

        @pl.when(nxt_ref[t] >= 0)
        def _():
            for c in copies(nxt_ref[t], 1 - s):
                c.start(priority=1)

        for k, w_bf in enumerate(w_bf_refs):
            w_bf[...] = stage_ref[s, k].astype(BF16)


def _moe_up_kernel(te_ref, na_ref, first_ref, slot_ref, nxt_ref, xs_ref, wg_hbm, wu_hbm, hid_ref,
                   stage_ref, wgb_ref, wub_ref, sem_ref, *, tile0, layer):
    i = pl.program_id(0)
    t = i + tile0
    _expert_weights(te_ref, first_ref, slot_ref, nxt_ref, i, t, layer, (wg_hbm, wu_hbm), stage_ref, sem_ref,
                    (wgb_ref, wub_ref))

    @pl.when(t < na_ref[0])
    def _():
        x_hi, x_lo = _unpack_bf16_pair(xs_ref[...])
        x_hi, x_lo = x_hi.astype(BF16), x_lo.astype(BF16)
        half = D_MODEL // 2
        g = _dot(x_hi, wgb_ref[:half, :]) + _dot(x_lo, wgb_ref[half:, :])
        u = _dot(x_hi, wub_ref[:half, :]) + _dot(x_lo, wub_ref[half:, :])
        hid_ref[...] = (g * _sigmoid(g) * u).astype(BF16)

    @pl.when(t >= na_ref[0])
    def _():
        hid_ref[...] = jnp.zeros_like(hid_ref)


def _moe_down_kernel(te_ref, na_ref, first_ref, slot_ref, nxt_ref, *rest, layer, chunk_tiles):
    ys_ref, stage_ref, wdb_ref, sem_ref = rest[-4:]
    wd_hbm = rest[-5]
    hid_refs = rest[:-5]
    t = pl.program_id(0)
    _expert_weights(te_ref, first_ref, slot_ref, nxt_ref, t, t, layer, (wd_hbm,), stage_ref, sem_ref, (wdb_ref,))

    for c, hid_ref in enumerate(hid_refs):
        @pl.when(jnp.logical_and(t < na_ref[0], t // chunk_tiles == c))
        def _(hid_ref=hid_ref):
            ys_ref[...] = _pack_bf16_pair(_dot(hid_ref[...], wdb_ref[...]))

    @pl.when(t >= na_ref[0])
    def _():
        ys_ref[...] = jnp.zeros_like(ys_ref)


MOE_CHUNKS = 2


def _moe_experts(h2, src_tok, tile_expert, n_active, layer, w_gate, w_up, w_down):
    tiles = MOE_TILES // MOE_CHUNKS
    rows = tiles * MOE_TM

    def weight_runs(call_tiles):
        t_ids = jnp.arange(MOE_TILES, dtype=jnp.int32)
        prev = jnp.concatenate([tile_expert[:1], tile_expert[:-1]])
        first = jnp.logical_or(t_ids % call_tiles == 0, tile_expert != prev)
        slot = ((jnp.cumsum(first.astype(jnp.int32)) - 1) % 2).astype(jnp.int32)
        first_at = lax.cummin(jnp.where(first, t_ids, MOE_TILES), reverse=True)
        next_at = jnp.concatenate([first_at[1:], jnp.full((1,), MOE_TILES, jnp.int32)])
        same_call = jnp.logical_and(next_at < MOE_TILES, next_at // call_tiles == t_ids // call_tiles)
        nxt = jnp.where(jnp.logical_and(first, same_call), tile_expert[jnp.minimum(next_at, MOE_TILES - 1)], -1)
        return (tile_expert, n_active, first.astype(jnp.int32), slot, nxt.astype(jnp.int32))

    row_blk = lambda i, *_: (i, 0)
    hbm = pl.BlockSpec(memory_space=pl.ANY)
    plan = weight_runs(tiles)
    hids = []
    for c in range(MOE_CHUNKS):
        xs = h2.at[src_tok[c * rows:(c + 1) * rows]].get(mode="promise_in_bounds")
        hids.append(pl.pallas_call(
            functools.partial(_moe_up_kernel, tile0=c * tiles, layer=layer),
            grid_spec=pltpu.PrefetchScalarGridSpec(
                num_scalar_prefetch=len(plan),
                grid=(tiles,),
                in_specs=[pl.BlockSpec((MOE_TM, D_MODEL // 2), row_blk), hbm, hbm],
                out_specs=pl.BlockSpec((MOE_TM, D_EXPERT), row_blk),
                scratch_shapes=[pltpu.VMEM((2, 2, D_MODEL, D_EXPERT), F32),
                                pltpu.VMEM((D_MODEL, D_EXPERT), BF16), pltpu.VMEM((D_MODEL, D_EXPERT), BF16),
                                pltpu.SemaphoreType.DMA((2, 2))],
            ),
            out_shape=jax.ShapeDtypeStruct((rows, D_EXPERT), BF16),
            compiler_params=_cparams(("arbitrary",), 52),
            name="moe_up",
        )(*plan, xs, w_gate, w_up))
    plan = weight_runs(MOE_TILES)
    hid_specs = [pl.BlockSpec((MOE_TM, D_EXPERT), lambda i, *_, c=c: (jnp.clip(i - c * tiles, 0, tiles - 1), 0))
                 for c in range(MOE_CHUNKS)]
    return pl.pallas_call(
        functools.partial(_moe_down_kernel, layer=layer, chunk_tiles=tiles),
        grid_spec=pltpu.PrefetchScalarGridSpec(
            num_scalar_prefetch=len(plan),
            grid=(MOE_TILES,),
            in_specs=hid_specs + [hbm],
            out_specs=pl.BlockSpec((MOE_TM, D_MODEL // 2), row_blk),
            scratch_shapes=[pltpu.VMEM((2, 1, D_EXPERT, D_MODEL), F32), pltpu.VMEM((D_EXPERT, D_MODEL), BF16),
                            pltpu.SemaphoreType.DMA((2, 1))],
        ),
        out_shape=jax.ShapeDtypeStruct((MOE_ROWS, D_MODEL // 2), jnp.int32),
        compiler_params=_cparams(("arbitrary",), 40),
        name="moe_down",
    )(*plan, *hids, w_down)


PLAN_R = 2 * N_TOK // LANES


def _plan_kernel(e_ref, pos_ref, meta_ref, src_ref):
    e = e_ref[...]
    r_i = lax.broadcasted_iota(jnp.int32, (LANES, LANES), 0)
    c_i = lax.broadcasted_iota(jnp.int32, (LANES, LANES), 1)
    upper = jnp.where(r_i <= c_i, 1.0, 0.0).astype(BF16)
    r_j = lax.broadcasted_iota(jnp.int32, (PLAN_R, PLAN_R), 0)
    c_j = lax.broadcasted_iota(jnp.int32, (PLAN_R, PLAN_R), 1)
    lower = jnp.where(c_j < r_j, 1.0, 0.0).astype(BF16)
    lane = lax.broadcasted_iota(jnp.int32, (1, LANES), 1)
    pos = jnp.zeros((PLAN_R, LANES), F32)
    cnt_row = jnp.zeros((1, LANES), F32)
    off_row = jnp.zeros((1, LANES), F32)
    end_row, off_list, cnt_list = [], [], []
    row_off = jnp.zeros((1, 1), F32)
    tiles_done = jnp.zeros((1, 1), F32)
    for ex in range(N_EXPERTS):
        m = jnp.where(e == ex, 1.0, 0.0)
        inc = _dot(m.astype(BF16), upper)
        tot = jnp.broadcast_to(inc[:, LANES - 1:LANES], (PLAN_R, LANES))
        before = _dot(lower, tot.astype(BF16))
        cnt = jnp.sum(inc[:, LANES - 1:LANES], axis=0, keepdims=True)
        pos = pos + m * (row_off + before + inc - 1.0)
        tiles = jnp.floor((cnt + (MOE_TM - 1)) * (1.0 / MOE_TM))
        cnt_row = jnp.where(lane == ex, cnt, cnt_row)
        off_row = jnp.where(lane == ex, row_off, off_row)
        cnt_list.append(cnt)
        off_list.append(row_off)
        tiles_done = tiles_done + tiles
        end_row.append(tiles_done)
        row_off = row_off + tiles * MOE_TM
    n_active = tiles_done
    tid = jnp.minimum(lane.astype(F32), n_active - 1.0)
    te = jnp.zeros((1, LANES), F32)
    for ex in range(N_EXPERTS):
        te = te + jnp.where(end_row[ex] <= tid, 1.0, 0.0)
    pos_ref[...] = pos.astype(jnp.int32)
    meta = jnp.concatenate([cnt_row, off_row, te, jnp.broadcast_to(n_active, (1, LANES)),
                            jnp.zeros((4, LANES), F32)], axis=0)
    meta_ref[...] = meta.astype(jnp.int32)
    pr = MOE_ROWS // LANES
    q = lax.broadcasted_iota(jnp.int32, (pr, 1), 0)
    tile = jnp.minimum((q // (MOE_TM // LANES)).astype(F32), n_active - 1.0)
    e_q = jnp.zeros((pr, 1), F32)
    for ex in range(N_EXPERTS):
        e_q = e_q + jnp.where(end_row[ex] <= tile, 1.0, 0.0)
    off_q = jnp.zeros((pr, 1), F32)
    cnt_q = jnp.zeros((pr, 1), F32)
    start_q = jnp.zeros((pr, 1), F32)
    start = jnp.zeros((1, 1), F32)
    for ex in range(N_EXPERTS):
        hit = e_q == float(ex)
        off_q = jnp.where(hit, off_list[ex], off_q)
        cnt_q = jnp.where(hit, cnt_list[ex], cnt_q)
        start_q = jnp.where(hit, start, start_q)
        start = start + cnt_list[ex]
    r = (q * LANES + lane).astype(F32)
    rank = r - off_q
    spare = r - jnp.where(r >= float(N_TOK), float(N_TOK), 0.0) - jnp.where(r >= float(2 * N_TOK), float(N_TOK), 0.0)
    src_ref[...] = jnp.where(rank < cnt_q, start_q + rank, -(spare + 1.0)).astype(jnp.int32)


def _route_plan(eidx):
    e2 = eidx[:2].reshape(PLAN_R, LANES)
    pos, meta, src = pl.pallas_call(
        _plan_kernel,
        out_shape=[jax.ShapeDtypeStruct((PLAN_R, LANES), jnp.int32), jax.ShapeDtypeStruct((8, LANES), jnp.int32),
                   jax.ShapeDtypeStruct((MOE_ROWS // LANES, LANES), jnp.int32)],
        compiler_params=pltpu.CompilerParams(vmem_limit_bytes=32 * 1024 * 1024),
        name="route_plan",
    )(e2)
    te, n_active = meta[2, :MOE_TILES], meta[3, :1]
    order = jnp.argsort(e2.reshape(-1), stable=True).astype(jnp.int32)
    src = src.reshape(-1)
    src_tok = jnp.where(src >= 0, order[jnp.maximum(src, 0)] % N_TOK, -src - 1)
    return src_tok, te, n_active, pos.reshape(-1)


def _combine_kernel(x_ref, y0_ref, y1_ref, gt_ref, mt_ref, fg_ref, o_ref, *, final):
    gt = gt_ref[...]
    g0, g1 = gt[:, 0:1], gt[:, 1:2]
    half = D_MODEL // 2
    y0 = _unpack_bf16_pair(y0_ref[...])
    y1 = _unpack_bf16_pair(y1_ref[...])
    xs = []
    for c in range(2):
        cols = slice(c * half, (c + 1) * half)
        xs.append(x_ref[:, cols] + mt_ref[0, 5:6, cols] * (y0[c] * g0 + y1[c] * g1))
    if final:
        ssq = jnp.sum(xs[0] * xs[0], axis=-1, keepdims=True) + jnp.sum(xs[1] * xs[1], axis=-1, keepdims=True)
        r = lax.rsqrt(ssq * (1.0 / D_MODEL) + EPS)
        xs = [xc * r * fg_ref[:, c * half:(c + 1) * half] for c, xc in enumerate(xs)]
    for c, xc in enumerate(xs):
        o_ref[:, c * half:(c + 1) * half] = xc


def _combine(x, row0, n_rows, ysel, gates_t, mt, final_g, final):
    tm = 256
    b0 = row0 // tm
    return pl.pallas_call(
        functools.partial(_combine_kernel, final=final),
        grid=(n_rows // tm,),
        in_specs=[
            pl.BlockSpec((tm, D_MODEL), lambda i: (i + b0, 0)),
            pl.BlockSpec((None, tm, D_MODEL // 2), lambda i: (0, i, 0)),
            pl.BlockSpec((None, tm, D_MODEL // 2), lambda i: (1, i, 0)),
            pl.BlockSpec((tm, 8), lambda i: (i + b0, 0)),
            pl.BlockSpec((1, 6, D_MODEL), lambda i: ((i + b0) // (SEG // tm), 0, 0)),
            pl.BlockSpec((1, D_MODEL), lambda i: (0, 0)),
        ],
        out_specs=pl.BlockSpec((tm, D_MODEL), lambda i: (i, 0)),
        out_shape=jax.ShapeDtypeStruct((n_rows, D_MODEL), F32),
        compiler_params=_cparams(("arbitrary",), 40),
        name="moe_combine",
    )(x, ysel, ysel, gates_t, mt, final_g)


def _rope_tables(dim):
    half = dim // 2
    inv = ROPE_THETA ** (-jnp.arange(0, half, 2, dtype=F32) / half)
    t = jnp.arange(DEC_SEQ)
    ang_r = (t // GRID_W).astype(F32)[:, None] * inv[None, :]
    ang_c = (t % GRID_W).astype(F32)[:, None] * inv[None, :]
    ang = jnp.concatenate([ang_r, ang_r, ang_c, ang_c], axis=-1)
    cos, sin = jnp.cos(ang), jnp.sin(ang)
    reps = LANES // dim
    cos = jnp.tile(cos, (1, reps))
    sin = jnp.tile(sin, (1, reps))
    sh = dim // 4
    second = (np.arange(LANES) % (2 * sh)) >= sh
    sp = jnp.where(second[None, :], sin, 0.0)
    sm = jnp.where(second[None, :], 0.0, -sin)
    return cos, sp, sm


def _neighbourhood_bias(rpb):
    rows = DEC_SEQ // GRID_W
    kh = min(NA_ROWS, rows)
    r = np.arange(rows)
    r0 = np.clip(r - kh // 2, 0, rows - kh)
    kr = np.arange(rows)
    row_ok = (kr[None, :] >= r0[:, None]) & (kr[None, :] < r0[:, None] + kh)
    c = np.arange(GRID_W)
    ws = np.clip(c - NA_COLS // 2, 0, GRID_W - NA_COLS)
    kc = np.arange(GRID_W)
    col_ok = (kc[None, :] >= ws[:, None]) & (kc[None, :] < ws[:, None] + NA_COLS)
    dc_idx = np.clip(kc[None, :] - c[:, None], -(NA_COLS - 1), NA_COLS - 1) + NA_COLS - 1
    sel_c = ((np.arange(2 * NA_COLS - 1)[:, None, None] == dc_idx[None]) & col_ok[None]).astype(np.float32)
    t = jnp.einsum("hab,bcx->hacx", rpb.astype(F32), sel_c, precision=lax.Precision.HIGHEST)
    t = jnp.where(col_ok[None, None], t * LOG2E, NEG)
    t = jnp.concatenate([t, t], axis=-1)

    def build(t_ref, o_ref):
        left = lax.broadcasted_iota(jnp.int32, (1, LANES), 1) < GRID_W
        neg = jnp.full((GRID_W, LANES), NEG, F32)
        for rr in range(rows):
            for p in range(rows // 2):
                parts = [t_ref[k - rr + NA_ROWS - 1] if row_ok[rr, k] else neg for k in (2 * p, 2 * p + 1)]
                blk = jnp.where(left, parts[0], parts[1]) if (row_ok[rr, 2 * p] or row_ok[rr, 2 * p + 1]) else neg
                o_ref[rr * GRID_W:(rr + 1) * GRID_W, p * LANES:(p + 1) * LANES] = blk

    return pl.pallas_call(
        build,
        grid=(N_HEADS,),
        in_specs=[pl.BlockSpec((None, 2 * NA_ROWS - 1, GRID_W, LANES), lambda h: (h, 0, 0, 0))],
        out_specs=pl.BlockSpec((None, DEC_SEQ, DEC_SEQ), lambda h: (h, 0, 0)),
        out_shape=jax.ShapeDtypeStruct((N_HEADS, DEC_SEQ, DEC_SEQ), F32),
        compiler_params=_cparams(("arbitrary",), 32),
        name="nbr_bias",
    )(t)


def _even_w_in_layout(w, idx):
    cq_end = 3 * N_HEADS * HEAD_DIM + B_Q_RANK
    kv_end = cq_end + B_KV_RANK
    tk = 256

    def layout(w_ref, o_ref):
        o_ref[:, :cq_end] = w_ref[:, :cq_end].astype(BF16)
        o_ref[:, cq_end + 2 * LANES:] = w_ref[:, cq_end:kv_end].astype(BF16)
        kr = w_ref[:, kv_end:].astype(BF16)
        r = lax.broadcasted_iota(jnp.int32, (B_ROPE_DIM, 2 * LANES), 0)
        c = lax.broadcasted_iota(jnp.int32, (B_ROPE_DIM, 2 * LANES), 1)
        place = jnp.where(jnp.logical_or(c == r, c == r + 2 * LANES - B_ROPE_DIM), 1.0, 0.0).astype(BF16)
        o_ref[:, cq_end:cq_end + 2 * LANES] = _dot(kr, place).astype(BF16)

    return pl.pallas_call(
        layout,
        grid=(D_MODEL // tk,),
        in_specs=[pl.BlockSpec((None, tk, w.shape[2]), lambda i: (idx, i, 0))],
        out_specs=pl.BlockSpec((tk, PROJ_W), lambda i: (i, 0)),
        out_shape=jax.ShapeDtypeStruct((D_MODEL, PROJ_W), BF16),
        compiler_params=_cparams(("arbitrary",), 32),
        name="even_w_in_layout",
    )(w)


def _even_layer(x, mt, l, i, rope64, cache_a_k, cache_a_v, cache_b_ckv, cache_b_krope, norm_mix_g, ev_w_in,
                ev_lambda, ev_subln_g, ev_q_norm_g, ev_kv_norm_g, ev_w_uq, ev_w_ukv, ev_w_out):
    w_in = _even_w_in_layout(ev_w_in, i)
    all_g = (0, 1, 2, 3)
    a_scale = A_QK_DIM ** -0.5 * LOG2E
    in_kw = dict(prologue="modulate", g=norm_mix_g[l][None, :], norm_tile=8, norm_g=ev_kv_norm_g[i][None, :],
                 out_dtype=BF16, tile_scale={0: a_scale, 1: a_scale})
    bcq_src = (0, ((6, 0, TN), (7, 0, TN)))
    head_src = lambda t0: tuple((t0 + t, gq, 4 * t + gq) for t in range(2) for gq in range(4))
    proj_p, st_ak, st_av, st_ckv, st_kr, bcq_p = _projection(
        *x["ctx"], N_SEG_P, D_MODEL, 0, D_MODEL, w_in, mt=mt[:N_SEG_P], name="even_in_proj_ctx",
        states=((N_HEADS, head_src(2)), (N_HEADS, head_src(4)), (0, ((8, 0, B_KV_RANK),)), (0, ((7, 256, B_ROPE_DIM),)),
                bcq_src), **in_kw)
    proj_l, bcq_l = _projection(*x["lat"], DEC_BATCH, D_MODEL, 0, D_MODEL, w_in, mt=mt[N_SEG_P:], rope=rope64,
                                sh=A_QK_DIM // 4, rope_groups={0: all_g, 1: all_g, 2: all_g, 3: all_g, 7: (2, 3)},
                                states=(bcq_src,), name="even_in_proj_lat", **in_kw)
    wq = ev_w_uq[i].reshape(B_Q_RANK, N_HEADS, HEAD_DIM + B_ROPE_DIM)
    w_uq = jnp.concatenate([wq[:, :, :HEAD_DIM].reshape(B_Q_RANK, -1), wq[:, :, HEAD_DIM:].reshape(B_Q_RANK, -1)],
                           axis=1).astype(BF16)
    mla_scale = (HEAD_DIM + B_ROPE_DIM) ** -0.5
    q_kw = dict(prologue="rmsnorm", g=ev_q_norm_g[i][None, :], out_dtype=BF16, tn=w_uq.shape[1],
                tile_scale={0: mla_scale * LOG2E})
    bq_p = _projection(bcq_p, 0, N_SEG_P, 1024, 0, B_Q_RANK, w_uq, name="mla_q_up_ctx", **q_kw)
    bq_l = _projection(bcq_l, 0, DEC_BATCH, 1024, 0, B_Q_RANK, w_uq, rope=rope64, sh=B_ROPE_DIM // 4,
                       rope_groups={0: (8, 9, 10, 11)}, name="mla_q_up_lat", **q_kw)
    w_ukv = ev_w_ukv[i].astype(BF16)
    kv_kw = dict(prologue="cast", out_dtype=BF16, tn=w_ukv.shape[1])
    kv_p = _projection(proj_p, 0, N_SEG_P, 512, 8, B_KV_RANK, w_ukv, name="mla_kv_up_ctx", **kv_kw)
    kv_l = _projection(proj_l, 0, DEC_BATCH, 512, 8, B_KV_RANK, w_ukv, name="mla_kv_up_lat", **kv_kw)
    kv_ctx = _projection(cache_b_ckv[:, i].reshape(DEC_BATCH * PAST_LEN, B_KV_RANK), 0, 1, 512, 0, B_KV_RANK, w_ukv,
                         name="mla_kv_up_cache", **kv_kw)
    kv_ctx = kv_ctx.reshape(DEC_BATCH, PAST_LEN, 2 * N_HEADS * HEAD_DIM)
    krc = cache_b_krope[:, i]
    zc = jnp.zeros_like(krc)
    kr_ctx = jnp.concatenate([krc, zc, zc, krc], axis=-1)
    ak_ctx = cache_a_k[:, i].reshape(DEC_BATCH, PAST_LEN, N_HEADS * HEAD_DIM)
    av_ctx = cache_a_v[:, i].reshape(DEC_BATCH, PAST_LEN, N_HEADS * HEAD_DIM)
    lam_init = 0.8 - 0.6 * math.exp(-0.3 * l)
    o = jnp.zeros((N_TOK, D_MODEL), BF16)
    for latent, proj, b_q, kv in ((False, proj_p, bq_p, kv_p), (True, proj_l, bq_l, kv_l)):
        tag = "lat" if latent else "ctx"
        o = _attention(kind="diff", latent=latent, q_parts=[(proj, lambda h: h)], ko_parts=[(proj, lambda h: 8 + h)],
                       vo=(proj, lambda h: 16 + h), kc_parts=[(ak_ctx, lambda h: h)], vc=(av_ctx, lambda h: h),
                       lam=ev_lambda[i], subln=ev_subln_g[i][None, :], o_arr=o, o_blk=lambda h: h,
                       lam_init=lam_init, name="diff_attn_" + tag)
        o = _attention(kind="mla", latent=latent, q_parts=[(b_q, lambda h: h), (b_q, lambda h: 8 + h // 2)],
                       ko_parts=[(kv, lambda h: 2 * h), (proj, lambda h: 30 + h % 2)], vo=(kv, lambda h: 2 * h + 1),
                       kc_parts=[(kv_ctx, lambda h: 2 * h), (kr_ctx, lambda h: h % 2)], vc=(kv_ctx, lambda h: 2 * h + 1),
                       o_arr=o, o_blk=lambda h: 8 + h, name="mla_attn_" + tag)
    states = (st_ak.reshape(BATCH, SEQ, N_HEADS, HEAD_DIM), st_av.reshape(BATCH, SEQ, N_HEADS, HEAD_DIM),
              st_ckv.reshape(BATCH, SEQ, B_KV_RANK), st_kr.reshape(BATCH, SEQ, B_ROPE_DIM))
    return (o, ev_w_out[i].astype(BF16)), states


def _odd_layer(x, mt, l, i, rope128, cache_c_k, cache_c_v, cache_d_k, cache_d_v, norm_mix_g, od_w_in, od_rpb,
               od_sink, od_w_out):
    w_in = od_w_in[i].astype(BF16)
    all_g = (0, 1, 2, 3)
    q_scale = HEAD_DIM ** -0.5 * LOG2E
    in_kw = dict(prologue="modulate", g=norm_mix_g[l][None, :], out_dtype=BF16,
                 tile_scale={t: q_scale for t in (0, 1, 6, 7)})
    head_src = lambda t0: tuple((t0 + t, gq, 4 * t + gq) for t in range(2) for gq in range(4))
    proj_p, st_ck, st_cv, st_dk, st_dv = _projection(
        *x["ctx"], N_SEG_P, D_MODEL, 0, D_MODEL, w_in, mt=mt[:N_SEG_P], name="odd_in_proj_ctx",
        states=((N_HEADS, head_src(2)), (N_HEADS, head_src(4)), (D_KV_HEADS, ((8, 0, 0), (8, 1, 1))),
                (D_KV_HEADS, ((8, 2, 0), (8, 3, 1)))), **in_kw)
    proj_l = _projection(*x["lat"], DEC_BATCH, D_MODEL, 0, D_MODEL, w_in, mt=mt[N_SEG_P:], rope=rope128,
                         sh=HEAD_DIM // 4, rope_groups={6: all_g, 7: all_g, 8: (0, 1)}, name="odd_in_proj_lat", **in_kw)
    ck_ctx = cache_c_k[:, i].reshape(DEC_BATCH, PAST_LEN, N_HEADS * HEAD_DIM)
    cv_ctx = cache_c_v[:, i].reshape(DEC_BATCH, PAST_LEN, N_HEADS * HEAD_DIM)
    dk_ctx = cache_d_k[:, i].reshape(DEC_BATCH, PAST_LEN, D_KV_HEADS * HEAD_DIM)
    dv_ctx = cache_d_v[:, i].reshape(DEC_BATCH, PAST_LEN, D_KV_HEADS * HEAD_DIM)
    bias = _neighbourhood_bias(od_rpb[i])
    sink = od_sink[i].astype(F32)
    o = jnp.zeros((N_TOK, D_MODEL), BF16)
    for latent, proj in ((False, proj_p), (True, proj_l)):
        tag = "lat" if latent else "ctx"
        o = _attention(kind="plain", latent=latent, q_parts=[(proj, lambda h: h)], ko_parts=[(proj, lambda h: 8 + h)],
                       vo=(proj, lambda h: 16 + h), kc_parts=[(ck_ctx, lambda h: h)], vc=(cv_ctx, lambda h: h),
                       bias=bias if latent else None, o_arr=o, o_blk=lambda h: h,
                       name="nbr_attn_" + tag)
        o = _attention(kind="plain", latent=latent, q_parts=[(proj, lambda h: 24 + h)],
                       ko_parts=[(proj, lambda h: 32 + h // D_GROUP)], vo=(proj, lambda h: 34 + h // D_GROUP),
                       kc_parts=[(dk_ctx, lambda h: h // D_GROUP)], vc=(dv_ctx, lambda h: h // D_GROUP), sink=sink,
                       o_arr=o, o_blk=lambda h: 8 + h, has_band=latent, tq_lat=256,
                       name="win_attn_" + tag)
    states = (st_ck.reshape(BATCH, SEQ, N_HEADS, HEAD_DIM), st_cv.reshape(BATCH, SEQ, N_HEADS, HEAD_DIM),
              st_dk.reshape(BATCH, SEQ, D_KV_HEADS, HEAD_DIM), st_dv.reshape(BATCH, SEQ, D_KV_HEADS, HEAD_DIM))
    return (o, od_w_out[i].astype(BF16)), states


def _moe_layer(mix, x_parts, mt, g, router_w, router_b, layer, w_gate, w_up, w_down, final_g, final):
    x, h2, eidx, gates = _out_projection_router(*mix, x_parts, mt, g, router_w, router_b)
    src_tok, te, n_active, pos = _route_plan(eidx)
    ys = _moe_experts(h2, src_tok, te, n_active, layer, w_gate, w_up, w_down)
    pos2 = pos.reshape(2, N_TOK)
    gates_t = gates.T
    outs = []
    for row0, n_rows in ((0, N_PROMPT), (N_PROMPT, N_TOK - N_PROMPT)):
        ysel = ys.at[pos2[:, row0:row0 + n_rows].reshape(-1)].get(mode="promise_in_bounds")
        outs.append(_combine(x, row0, n_rows, ysel.reshape(2, n_rows, D_MODEL // 2), gates_t, mt, final_g, final))
    return outs


def kernel(x_prompt, x_sample, cache_a_k, cache_a_v, cache_b_ckv, cache_b_krope, cache_c_k, cache_c_v, cache_d_k, cache_d_v, c, c_ctx, w_ada, b_ada, norm_mix_g, norm_ffn_g, ev_w_in, ev_lambda, ev_subln_g, ev_q_norm_g, ev_kv_norm_g, ev_w_uq, ev_w_ukv, ev_w_out, od_w_in, od_rpb, od_sink, od_w_out, router_w, router_b, moe_w_gate, moe_w_up, moe_w_down, final_g):
    xp = x_prompt.reshape(N_PROMPT, D_MODEL)
    xl = x_sample.reshape(DEC_BATCH * DEC_SEQ, D_MODEL)
    x = dict(ctx=(xp, 0), lat=(xl, 0), parts=[xp, xl])
    cond8 = jnp.concatenate([c_ctx[None, :], c, jnp.zeros((3, D_MODEL), F32)], axis=0)
    mod = _adaln(cond8, w_ada, b_ada)
    seg_row = np.array([0] * N_SEG_P + [1 + b for b in range(DEC_BATCH)])
    mt_all = mod[:, seg_row].reshape(DEPTH, N_SEG, 6, D_MODEL)
    rope64 = _rope_tables(A_QK_DIM)
    rope128 = _rope_tables(HEAD_DIM)
    even_states, odd_states = [], []
    for l in range(DEPTH):
        i = l // 2
        mt = mt_all[l]
        if l % 2 == 0:
            mix, st = _even_layer(x, mt, l, i, rope64, cache_a_k, cache_a_v, cache_b_ckv, cache_b_krope, norm_mix_g,
                                ev_w_in, ev_lambda, ev_subln_g, ev_q_norm_g, ev_kv_norm_g, ev_w_uq, ev_w_ukv, ev_w_out)
            even_states.append(st)
        else:
            mix, st = _odd_layer(x, mt, l, i, rope128, cache_c_k, cache_c_v, cache_d_k, cache_d_v, norm_mix_g,
                               od_w_in, od_rpb, od_sink, od_w_out)
            odd_states.append(st)
        x = _moe_layer(mix, x["parts"], mt, norm_ffn_g[l][None, :], router_w, router_b, l, moe_w_gate, moe_w_up, moe_w_down,
                       final_g[None, :], final=(l == DEPTH - 1))
        if l < DEPTH - 1:
            x = dict(ctx=(x[0], 0), lat=(x[1], 0), parts=list(x))
    y_prompt = x[0].reshape(BATCH, SEQ, D_MODEL)
    y_sample = x[1].reshape(DEC_BATCH, DEC_SEQ, D_MODEL)
    new_even = tuple(jnp.stack([st[k] for st in even_states], axis=1) for k in range(4))
    new_odd = tuple(jnp.stack([st[k] for st in odd_states], axis=1) for k in range(4))
    return (y_prompt, y_sample) + new_even + new_odd
```

```python
import functools
import math

import numpy as np
import jax
import jax.numpy as jnp
from jax import lax
from jax.experimental import pallas as pl
from jax.experimental.pallas import tpu as pltpu

D_MODEL = 2048
BATCH = 32
SEQ = 256
DEPTH = 2
DEC_BATCH = 4
DEC_SEQ = 1024
PAST_LEN = 256
GRID_W = 64
HEAD_DIM = 128
N_HEADS = 8
A_QK_DIM = 64
B_Q_RANK = 768
B_KV_RANK = 512
B_ROPE_DIM = 64
NA_ROWS = 8
NA_COLS = 16
D_KV_HEADS = 2
D_GROUP = 4
D_WINDOW = 128
N_EXPERTS = 16
N_GROUPS = 4
D_EXPERT = 1024
ROPE_THETA = 10000.0
EPS = 1e-6
NEG = -1e30
LOG2E = 1.4426950408889634

SEG = 1024
N_PROMPT = BATCH * SEQ
N_TOK = N_PROMPT + DEC_BATCH * DEC_SEQ
N_SEG = N_TOK // SEG
N_SEG_P = N_PROMPT // SEG
PROJ_W = 4608
TN = 512
LANES = 128
MOE_TM = 256
MOE_ROWS = 2 * N_TOK + N_EXPERTS * MOE_TM
MOE_TILES = MOE_ROWS // MOE_TM

F32 = jnp.float32
BF16 = jnp.bfloat16


def _cparams(sem, vmem_mb):
    return pltpu.CompilerParams(dimension_semantics=sem, vmem_limit_bytes=vmem_mb * 1024 * 1024)


def _dot(a, b):
    return jnp.dot(a, b, preferred_element_type=F32)


def _dot_nt(a, b):
    return lax.dot_general(a, b, (((1,), (1,)), ((), ())), preferred_element_type=F32)


def _sigmoid(x):
    return 1.0 / (1.0 + jnp.exp(-x))


def _pack_bf16_pair(x):
    c = x.shape[1] // 2
    bits = pltpu.bitcast(x.astype(BF16).astype(F32), jnp.int32)
    return bits[:, :c] | lax.shift_right_logical(bits[:, c:], 16)


def _unpack_bf16_pair(w):
    hi = pltpu.bitcast(w & jnp.int32(-65536), F32)
    lo = pltpu.bitcast(lax.shift_left(w, 16), F32)
    return hi, lo


def _adaln_kernel(c_ref, w_ref, b_ref, o_ref):
    c = c_ref[...]
    a = (c * _sigmoid(c)).astype(BF16)
    o_ref[...] = _dot(a, w_ref[...].astype(BF16)) + b_ref[...]


def _adaln(cond8, w_ada, b_ada):
    tn = 1024
    n = 6 * D_MODEL
    return pl.pallas_call(
        _adaln_kernel,
        grid=(DEPTH, n // tn),
        in_specs=[
            pl.BlockSpec((8, D_MODEL), lambda l, j: (0, 0)),
            pl.BlockSpec((None, D_MODEL, tn), lambda l, j: (l, 0, j)),
            pl.BlockSpec((None, 1, tn), lambda l, j: (l, 0, j)),
        ],
        out_specs=pl.BlockSpec((None, 8, tn), lambda l, j: (l, 0, j)),
        out_shape=jax.ShapeDtypeStruct((DEPTH, 8, n), F32),
        compiler_params=_cparams(("arbitrary", "arbitrary"), 40),
        name="adaln",
    )(cond8, w_ada, b_ada.reshape(DEPTH, 1, n))


def _rope(a, cos, sp, sm, sh):
    return a * cos + pltpu.roll(a, sh, 1) * sp + pltpu.roll(a, LANES - sh, 1) * sm


def _proj_kernel(*refs, prologue, k_in, sh, n_tiles, tn, tps, tm, rope_groups, norm_tile, states, tile_scale):
    x_ref, mt_ref, g_ref, w_ref, cos_ref, sp_ref, sm_ref, ng_ref, o_ref = refs[:9]
    st_refs = refs[9:-1]
    xn_ref = refs[-1]
    j = pl.program_id(1)

    @pl.when(j == 0)
    def _():
        x = x_ref[:, :k_in]
        if prologue == "cast":
            xn_ref[...] = x.astype(BF16)
        else:
            r = lax.rsqrt(jnp.mean(x * x, axis=-1, keepdims=True) + EPS)
            if prologue == "modulate":
                y = x * r * (g_ref[...] * (1.0 + mt_ref[0, 1:2, :])) + mt_ref[0, 0:1, :]
            else:
                y = x * r * g_ref[...]
            xn_ref[...] = y.astype(BF16)

    n_grp = tn // LANES

    def treatment(t):
        acts = []
        for k, (n_heads, src) in enumerate(states):
            dest = 0
            for tt, a, b in src:
                if tt == t:
                    acts.append((k, a, b, dest))
                dest += 0 if n_heads else b
        return (tuple(rope_groups.get(t, ())), t == norm_tile, tuple(acts), float(tile_scale.get(t, 1.0)))

    def column_tile(t, u):
        rg, is_norm, acts, sc = treatment(t)
        acc = _dot(xn_ref[...], w_ref[:, u * tn:(u + 1) * tn])
        if is_norm:
            scale = lax.rsqrt(jnp.mean(acc * acc, axis=-1, keepdims=True) + EPS)
            acc = acc * scale * ng_ref[...]
        vals = [acc[:, g * LANES:(g + 1) * LANES] for g in range(n_grp)]
        if sc != 1.0:
            vals = [v * sc for v in vals]
        if rg:
            cos, sp, sm = cos_ref[...], sp_ref[...], sm_ref[...]
            vals = [_rope(v, cos, sp, sm, sh) if g in rg else v for g, v in enumerate(vals)]
        for g, v in enumerate(vals):
            o_ref[:, u * tn + g * LANES:u * tn + (g + 1) * LANES] = v.astype(o_ref.dtype)
        for k, a, b, dest in acts:
            n_heads = states[k][0]
            if n_heads:
                st_refs[k][pl.ds(b, tm, stride=n_heads), :] = vals[a]
            elif b % LANES == 0:
                for q in range(b // LANES):
                    st_refs[k][:, dest + q * LANES:dest + (q + 1) * LANES] = vals[a // LANES + q]
            else:
                st_refs[k][...] = vals[a // LANES][:, a % LANES:a % LANES + b]

    n_steps = n_tiles // tps
    for jj in range(n_steps):
        def step(jj=jj):
            for u in range(tps):
                column_tile(jj * tps + u, u)
        if n_steps == 1:
            step()
        else:
            pl.when(j == jj)(step)


def _projection(x, row_off, n_seg, x_blk_w, x_blk_idx, k_in, w, *, prologue, mt=None, g=None, rope=None, sh=16,
                rope_groups=None, norm_tile=None, norm_g=None, states=(), out_dtype=F32, tile_scale=None, tn=TN,
                tps=1, tm=SEG, name):
    n = w.shape[1]
    m = n_seg * SEG
    per_seg = SEG // tm
    assert n % (tn * tps) == 0 and w.shape[0] == k_in and SEG % tm == 0
    assert tm == SEG or not rope_groups
    if mt is None:
        mt = jnp.zeros((n_seg, 6, LANES), F32)
    if g is None:
        g = jnp.ones((1, k_in), F32)
    if rope is None:
        rope = tuple(jnp.zeros((8, LANES), F32) for _ in range(3))
    if norm_g is None:
        norm_g = jnp.ones((1, tn), F32)
    kern = functools.partial(_proj_kernel, prologue=prologue, k_in=k_in, sh=sh, n_tiles=n // tn, tn=tn, tps=tps,
                             tm=tm, rope_groups=dict(rope_groups or {}), norm_tile=norm_tile, states=tuple(states),
                             tile_scale=dict(tile_scale or {}))
    full2 = lambda i, j: (0, 0)
    out_shape = [jax.ShapeDtypeStruct((m, n), out_dtype)]
    out_specs = [pl.BlockSpec((tm, tn * tps), lambda i, j: (i, j))]
    for n_heads, src in states:
        if n_heads:
            out_shape.append(jax.ShapeDtypeStruct((m * n_heads, LANES), F32))
            out_specs.append(pl.BlockSpec((tm * n_heads, LANES), lambda i, j: (i, 0)))
        else:
            width = sum(b for _, _, b in src)
            out_shape.append(jax.ShapeDtypeStruct((m, width), F32))
            out_specs.append(pl.BlockSpec((tm, width), lambda i, j: (i, 0)))
    x_kw = dict(pipeline_mode=pl.Buffered(1)) if states else {}
    res = pl.pallas_call(
        kern,
        grid=(n_seg * per_seg, n // (tn * tps)),
        in_specs=[
            pl.BlockSpec((tm, x_blk_w), lambda i, j: (i + row_off * per_seg, x_blk_idx), **x_kw),
            pl.BlockSpec((1, 6, mt.shape[2]), lambda i, j: (i // per_seg, 0, 0)),
            pl.BlockSpec((1, k_in), full2),
            pl.BlockSpec((k_in, tn * tps), lambda i, j: (0, j)),
            pl.BlockSpec(rope[0].shape, full2),
            pl.BlockSpec(rope[1].shape, full2),
            pl.BlockSpec(rope[2].shape, full2),
            pl.BlockSpec((1, tn), full2),
        ],
        out_specs=out_specs,
        out_shape=out_shape,
        scratch_shapes=[pltpu.VMEM((tm, k_in), BF16)],
        compiler_params=_cparams(("arbitrary", "arbitrary"), 56 if states else 48),
        name=name,
    )(x, mt, g, w, *rope, norm_g)
    return res if states else res[0]


def _attn_kernel(*refs, kind, n_parts, has_ctx, has_bias, has_band, has_sink, nb, s_len, tq, lam_init, own_win):
    it = iter(refs)
    q_refs = [next(it) for _ in range(n_parts)]
    ko_refs = [next(it) for _ in range(n_parts)]
    vo_ref = next(it)
    kc_refs, vc_ref = [], None
    if has_ctx:
        kc_refs = [next(it) for _ in range(n_parts)]
        vc_ref = next(it)
    bias_ref = next(it) if has_bias else None
    lam_ref = sg_ref = None
    if kind == "diff":
        lam_ref = next(it)
        sg_ref = next(it)
    sink_ref = next(it) if has_sink else None
    o_ref = next(it)

    sink2 = sink_ref[pl.program_id(1)] * LOG2E if has_sink else None
    lane = lax.broadcasted_iota(jnp.int32, (1, LANES), 1)

    def softmax_pv(s_list, v_list):
        m = None
        for s in s_list:
            mm = jnp.max(s, axis=-1, keepdims=True)
            m = mm if m is None else jnp.maximum(m, mm)
        if sink2 is not None:
            m = jnp.maximum(m, sink2)
        l = jnp.exp2(sink2 - m) if sink2 is not None else None
        o = None
        for s, v in zip(s_list, v_list):
            e = jnp.exp2(s - m)
            ss = jnp.sum(e, axis=-1, keepdims=True)
            l = ss if l is None else l + ss
            pv = _dot(e.astype(BF16), v)
            o = pv if o is None else o + pv
        return o, l

    for bi in range(nb):
        qrows = pl.ds(bi * tq, tq)
        q0 = (pl.program_id(2) * nb + bi) * tq
        if own_win is None:
            krows, k0 = pl.ds(0 if has_ctx else bi * s_len, s_len), 0
            n_own = s_len
        else:
            k0 = pl.multiple_of(jnp.clip(q0 - D_WINDOW, 0, s_len - own_win), LANES)
            krows = pl.ds(k0, own_win)
            n_own = own_win
        k_srcs, v_srcs, is_own = [], [], []
        if has_ctx:
            k_srcs.append([r[...].astype(BF16) for r in kc_refs])
            v_srcs.append(vc_ref[...].astype(BF16))
            is_own.append(False)
        k_srcs.append([r[krows, :].astype(BF16) for r in ko_refs])
        v_srcs.append(vo_ref[krows, :].astype(BF16))
        is_own.append(True)

        def mask_own(s):
            if has_bias:
                s = s + bias_ref[qrows, :]
            if has_band:
                qpos = q0 + lax.broadcasted_iota(jnp.int32, (tq, 1), 0)
                kpos = k0 + lax.broadcasted_iota(jnp.int32, (1, n_own), 1)
                s = jnp.where(jnp.abs(kpos - qpos) <= D_WINDOW, s, NEG)
            return s

        if kind == "diff":
            q = q_refs[0][qrows, :]
            zero = jnp.zeros_like(q)
            q1 = jnp.where(lane < A_QK_DIM, q, zero)
            q2 = jnp.where(lane < A_QK_DIM, zero, q)
            o1, l1 = softmax_pv([_dot_nt(q1, ks[0]) for ks in k_srcs], v_srcs)
            o2, l2 = softmax_pv([_dot_nt(q2, ks[0]) for ks in k_srcs], v_srcs)
            lv = lam_ref[...]
            lam = (jnp.exp(jnp.sum(lv[0:1] * lv[1:2], axis=-1, keepdims=True))
                   - jnp.exp(jnp.sum(lv[2:3] * lv[3:4], axis=-1, keepdims=True)) + lam_init)
            o = o1 * (1.0 / l1) - o2 * (lam / l2)
            o = o * lax.rsqrt(jnp.mean(o * o, axis=-1, keepdims=True) + EPS) * sg_ref[...] * (1.0 - lam_init)
        else:
            qs = [r[qrows, :] for r in q_refs]
            s_list = []
            for ks, own in zip(k_srcs, is_own):
                s = None
                for qp, kp in zip(qs, ks):
                    d = _dot_nt(qp, kp)
                    s = d if s is None else s + d
                s_list.append(mask_own(s) if own else s)
            o, l = softmax_pv(s_list, v_srcs)
            o = o * (1.0 / l)
        o_ref[qrows, :] = o.astype(o_ref.dtype)


def _attention(*, kind, latent, q_parts, ko_parts, vo, kc_parts=(), vc=None, bias=None, lam=None, subln=None,
               sink=None, o_arr, o_blk, lam_init=0.0, has_band=False, tq_lat=512, name):
    n_parts = len(q_parts)
    has_ctx = latent
    if latent:
        tq, s_len, nb = tq_lat, DEC_SEQ, DEC_SEQ // tq_lat
        qblk, kblk = nb * tq, SEG
        grid = (DEC_BATCH, N_HEADS, DEC_SEQ // qblk)
        qpb = DEC_SEQ // qblk
        q_row = lambda b, h, t: b * qpb + t
        o_row = lambda b, h, t: N_PROMPT // qblk + b * qpb + t
        k_row = lambda b, h, t: b
    else:
        qblk = kblk = 4 * SEG
        tq, s_len, nb = SEQ, SEQ, qblk // SEQ
        grid = (N_PROMPT // qblk, N_HEADS, 1)
        q_row = o_row = k_row = lambda b, h, t: b
    args, specs = [], []

    def add(arr, shape, imap, **kw):
        args.append(arr)
        specs.append(pl.BlockSpec(shape, imap, **kw))

    for arr, f in q_parts:
        add(arr, (qblk, LANES), lambda b, h, t, f=f: (q_row(b, h, t), f(h)))
    for arr, f in list(ko_parts) + [vo]:
        add(arr, (kblk, LANES), lambda b, h, t, f=f: (k_row(b, h, t), f(h)))
    if has_ctx:
        for arr, f in list(kc_parts) + [vc]:
            add(arr, (None, PAST_LEN, LANES), lambda b, h, t, f=f: (b, 0, f(h)))
    if bias is not None:
        add(bias, (None, qblk, DEC_SEQ), lambda b, h, t: (h, t, 0))
    if kind == "diff":
        add(lam, lam.shape, lambda b, h, t: (0, 0))
        add(subln, subln.shape, lambda b, h, t: (0, 0))
    if sink is not None:
        args.append(sink)
        specs.append(pl.BlockSpec(memory_space=pltpu.SMEM))
    n_in = len(args)
    args.append(o_arr)
    specs.append(pl.BlockSpec(memory_space=pl.ANY))
    kern = functools.partial(_attn_kernel_aliased, kind=kind, n_parts=n_parts, has_ctx=has_ctx,
                             has_bias=bias is not None, has_band=has_band, has_sink=sink is not None,
                             nb=nb, s_len=s_len, tq=tq, lam_init=lam_init,
                             own_win=(tq + 2 * D_WINDOW) if has_band else None)
    return pl.pallas_call(
        kern,
        grid=grid,
        in_specs=specs,
        out_specs=pl.BlockSpec((qblk, LANES), lambda b, h, t: (o_row(b, h, t), o_blk(h))),
        out_shape=jax.ShapeDtypeStruct(o_arr.shape, o_arr.dtype),
        input_output_aliases={n_in: 0},
        compiler_params=_cparams(("arbitrary", "arbitrary", "arbitrary"), 48),
        name=name,
    )(*args)


def _attn_kernel_aliased(*refs, **kw):
    _attn_kernel(*refs[:-2], refs[-1], **kw)


def _ffn_norm_route(x, mt_ref, g_ref, w_ref, rb_ref):
    r = lax.rsqrt(jnp.mean(x * x, axis=-1, keepdims=True) + EPS)
    h = x * r * (g_ref[...] * (1.0 + mt_ref[0, 4:5, :])) + mt_ref[0, 3:4, :]
    hh = h.astype(BF16)
    packed = _pack_bf16_pair(hh)
    logits = _dot_nt(w_ref[...], hh)
    scores = _sigmoid(logits)
    sel = scores + rb_ref[...]
    per = N_EXPERTS // N_GROUPS
    sc = [scores[e:e + 1, :] for e in range(N_EXPERTS)]
    sl = [sel[e:e + 1, :] for e in range(N_EXPERTS)]
    best_g, best_v = None, None
    for gi in range(N_GROUPS):
        a, b, c, d = sl[gi * per:(gi + 1) * per]
        hi1, lo1, hi2, lo2 = jnp.maximum(a, b), jnp.minimum(a, b), jnp.maximum(c, d), jnp.minimum(c, d)
        gs = jnp.maximum(hi1, hi2) + jnp.maximum(jnp.minimum(hi1, hi2), jnp.maximum(lo1, lo2))
        if gi == 0:
            best_g, best_v = jnp.zeros_like(gs, dtype=jnp.int32), gs
        else:
            better = gs > best_v
            best_g = jnp.where(better, gi, best_g)
            best_v = jnp.where(better, gs, best_v)
    masked = [jnp.where(best_g == (e // per), sl[e], NEG) for e in range(N_EXPERTS)]
    i1, v1 = jnp.zeros_like(best_g), masked[0]
    for e in range(1, N_EXPERTS):
        better = masked[e] > v1
        i1 = jnp.where(better, e, i1)
        v1 = jnp.where(better, masked[e], v1)
    i2, v2 = None, None
    for e in range(N_EXPERTS):
        cand = jnp.where(i1 == e, -2e30, masked[e])
        if e == 0:
            i2, v2 = jnp.zeros_like(best_g), cand
        else:
            better = cand > v2
            i2 = jnp.where(better, e, i2)
            v2 = jnp.where(better, cand, v2)
    g1 = jnp.zeros_like(v1)
    g2 = jnp.zeros_like(v1)
    for e in range(N_EXPERTS):
        g1 = jnp.where(i1 == e, sc[e], g1)
        g2 = jnp.where(i2 == e, sc[e], g2)
    tot = g1 + g2
    rows = i1.shape[1]
    ids = jnp.concatenate([i1, i2, jnp.zeros((6, rows), jnp.int32)], axis=0)
    gates = jnp.concatenate([g1 / tot, g2 / tot, jnp.zeros((6, rows), F32)], axis=0)
    return packed, ids, gates


def _outproj_router_kernel(o_ref, w_ref, mt_ref, g_ref, rw_ref, rb_ref, *rest, blk_ranges, units):
    x_refs = rest[:-4]
    y_ref, h_ref, e_ref, gt_ref = rest[-4:]
    i = pl.program_id(0)
    n = o_ref.shape[0] // units
    for u in range(units):
        rows = pl.ds(u * n, n)
        x = x_refs[0][rows, :]
        for x_ref, (lo, _) in zip(x_refs[1:], blk_ranges[1:]):
            x = jnp.where(i >= lo, x_ref[rows, :], x)
        x1 = x + mt_ref[0, 2:3, :] * _dot(o_ref[rows, :], w_ref[...])
        y_ref[rows, :] = x1
        packed, ids, gates = _ffn_norm_route(x1, mt_ref, g_ref, rw_ref, rb_ref)
        h_ref[rows, :] = packed
        e_ref[:, u * n:(u + 1) * n] = ids
        gt_ref[:, u * n:(u + 1) * n] = gates


def _out_projection_router(o, w, x_parts, mt, g, router_w, router_b):
    units = 2
    tm = units * 256
    blk_ranges, lo = [], 0
    for xp in x_parts:
        blk_ranges.append((lo, lo + xp.shape[0] // tm))
        lo = blk_ranges[-1][1]
    assert lo == N_TOK // tm
    x_specs = [pl.BlockSpec((tm, D_MODEL), lambda i, lo=lo, hi=hi: (jnp.clip(i - lo, 0, hi - lo - 1), 0))
               for lo, hi in blk_ranges]
    const = lambda i: (0, 0)
    return pl.pallas_call(
        functools.partial(_outproj_router_kernel, blk_ranges=tuple(blk_ranges), units=units),
        grid=(N_TOK // tm,),
        in_specs=[
            pl.BlockSpec((tm, D_MODEL), lambda i: (i, 0)),
            pl.BlockSpec((D_MODEL, D_MODEL), const, pipeline_mode=pl.Buffered(1)),
            pl.BlockSpec((1, 6, D_MODEL), lambda i: (i // (SEG // tm), 0, 0)),
            pl.BlockSpec((1, D_MODEL), const),
            pl.BlockSpec((N_EXPERTS, D_MODEL), const),
            pl.BlockSpec((N_EXPERTS, 1), const),
        ] + x_specs,
        out_specs=[
            pl.BlockSpec((tm, D_MODEL), lambda i: (i, 0)),
            pl.BlockSpec((tm, D_MODEL // 2), lambda i: (i, 0)),
            pl.BlockSpec((8, tm), lambda i: (0, i)),
            pl.BlockSpec((8, tm), lambda i: (0, i)),
        ],
        out_shape=[
            jax.ShapeDtypeStruct((N_TOK, D_MODEL), F32),
            jax.ShapeDtypeStruct((N_TOK, D_MODEL // 2), jnp.int32),
            jax.ShapeDtypeStruct((8, N_TOK), jnp.int32),
            jax.ShapeDtypeStruct((8, N_TOK), F32),
        ],
        compiler_params=_cparams(("arbitrary",), 56),
        name="out_proj_router",
    )(o, w, mt, g, router_w.T.astype(BF16), router_b.reshape(N_EXPERTS, 1), *x_parts)


def _expert_weights(te_ref, first_ref, slot_ref, nxt_ref, i, t, layer, w_hbm, stage_ref, sem_ref, w_bf_refs):
    def copies(e, s):
        return [pltpu.make_async_copy(w.at[layer, e], stage_ref.at[s, k], sem_ref.at[s, k])
                for k, w in enumerate(w_hbm)]

    s = slot_ref[t]

    @pl.when(i == 0)
    def _():
        for c in copies(te_ref[t], s):
            c.start()

    @pl.when(first_ref[t] == 1)
    def _():
        for c in copies(te_ref[t], s):
            c.wait()

        @pl.when(nxt_ref[t] >= 0)
        def _():
            for c in copies(nxt_ref[t], 1 - s):
                c.start(priority=1)

        for k, w_bf in enumerate(w_bf_refs):
            w_bf[...] = stage_ref[s, k].astype(BF16)


def _moe_up_kernel(te_ref, na_ref, first_ref, slot_ref, nxt_ref, xs_ref, wg_hbm, wu_hbm, hid_ref,
                   stage_ref, wgb_ref, wub_ref, sem_ref, *, tile0, layer):
    i = pl.program_id(0)
    t = i + tile0
    _expert_weights(te_ref, first_ref, slot_ref, nxt_ref, i, t, layer, (wg_hbm, wu_hbm), stage_ref, sem_ref,
                    (wgb_ref, wub_ref))

    @pl.when(t < na_ref[0])
    def _():
        x_hi, x_lo = _unpack_bf16_pair(xs_ref[...])
        x_hi, x_lo = x_hi.astype(BF16), x_lo.astype(BF16)
        half = D_MODEL // 2
        g = _dot(x_hi, wgb_ref[:half, :]) + _dot(x_lo, wgb_ref[half:, :])
        u = _dot(x_hi, wub_ref[:half, :]) + _dot(x_lo, wub_ref[half:, :])
        hid_ref[...] = (g * _sigmoid(g) * u).astype(BF16)

    @pl.when(t >= na_ref[0])
    def _():
        hid_ref[...] = jnp.zeros_like(hid_ref)


def _moe_down_kernel(te_ref, na_ref, first_ref, slot_ref, nxt_ref, *rest, layer, chunk_tiles):
    ys_ref, stage_ref, wdb_ref, sem_ref = rest[-4:]
    wd_hbm = rest[-5]
    hid_refs = rest[:-5]
    t = pl.program_id(0)
    _expert_weights(te_ref, first_ref, slot_ref, nxt_ref, t, t, layer, (wd_hbm,), stage_ref, sem_ref, (wdb_ref,))

    for c, hid_ref in enumerate(hid_refs):
        @pl.when(jnp.logical_and(t < na_ref[0], t // chunk_tiles == c))
        def _(hid_ref=hid_ref):
            ys_ref[...] = _pack_bf16_pair(_dot(hid_ref[...], wdb_ref[...]))

    @pl.when(t >= na_ref[0])
    def _():
        ys_ref[...] = jnp.zeros_like(ys_ref)


MOE_CHUNKS = 2


def _moe_experts(h2, src_tok, tile_expert, n_active, layer, w_gate, w_up, w_down):
    tiles = MOE_TILES // MOE_CHUNKS
    rows = tiles * MOE_TM

    def weight_runs(call_tiles):
        t_ids = jnp.arange(MOE_TILES, dtype=jnp.int32)
        prev = jnp.concatenate([tile_expert[:1], tile_expert[:-1]])
        first = jnp.logical_or(t_ids % call_tiles == 0, tile_expert != prev)
        slot = ((jnp.cumsum(first.astype(jnp.int32)) - 1) % 2).astype(jnp.int32)
        first_at = lax.cummin(jnp.where(first, t_ids, MOE_TILES), reverse=True)
        next_at = jnp.concatenate([first_at[1:], jnp.full((1,), MOE_TILES, jnp.int32)])
        same_call = jnp.logical_and(next_at < MOE_TILES, next_at // call_tiles == t_ids // call_tiles)
        nxt = jnp.where(jnp.logical_and(first, same_call), tile_expert[jnp.minimum(next_at, MOE_TILES - 1)], -1)
        return (tile_expert, n_active, first.astype(jnp.int32), slot, nxt.astype(jnp.int32))

    row_blk = lambda i, *_: (i, 0)
    hbm = pl.BlockSpec(memory_space=pl.ANY)
    plan = weight_runs(tiles)
    hids = []
    for c in range(MOE_CHUNKS):
        xs = h2.at[src_tok[c * rows:(c + 1) * rows]].get(mode="promise_in_bounds")
        hids.append(pl.pallas_call(
            functools.partial(_moe_up_kernel, tile0=c * tiles, layer=layer),
            grid_spec=pltpu.PrefetchScalarGridSpec(
                num_scalar_prefetch=len(plan),
                grid=(tiles,),
                in_specs=[pl.BlockSpec((MOE_TM, D_MODEL // 2), row_blk), hbm, hbm],
                out_specs=pl.BlockSpec((MOE_TM, D_EXPERT), row_blk),
                scratch_shapes=[pltpu.VMEM((2, 2, D_MODEL, D_EXPERT), F32),
                                pltpu.VMEM((D_MODEL, D_EXPERT), BF16), pltpu.VMEM((D_MODEL, D_EXPERT), BF16),
                                pltpu.SemaphoreType.DMA((2, 2))],
            ),
            out_shape=jax.ShapeDtypeStruct((rows, D_EXPERT), BF16),
            compiler_params=_cparams(("arbitrary",), 52),
            name="moe_up",
        )(*plan, xs, w_gate, w_up))
    plan = weight_runs(MOE_TILES)
    hid_specs = [pl.BlockSpec((MOE_TM, D_EXPERT), lambda i, *_, c=c: (jnp.clip(i - c * tiles, 0, tiles - 1), 0))
                 for c in range(MOE_CHUNKS)]
    return pl.pallas_call(
        functools.partial(_moe_down_kernel, layer=layer, chunk_tiles=tiles),
        grid_spec=pltpu.PrefetchScalarGridSpec(
            num_scalar_prefetch=len(plan),
            grid=(MOE_TILES,),
            in_specs=hid_specs + [hbm],
            out_specs=pl.BlockSpec((MOE_TM, D_MODEL // 2), row_blk),
            scratch_shapes=[pltpu.VMEM((2, 1, D_EXPERT, D_MODEL), F32), pltpu.VMEM((D_EXPERT, D_MODEL), BF16),
                            pltpu.SemaphoreType.DMA((2, 1))],
        ),
        out_shape=jax.ShapeDtypeStruct((MOE_ROWS, D_MODEL // 2), jnp.int32),
        compiler_params=_cparams(("arbitrary",), 40),
        name="moe_down",
    )(*plan, *hids, w_down)


PLAN_R = 2 * N_TOK // LANES


def _plan_kernel(e_ref, pos_ref, meta_ref, src_ref):
    e = e_ref[...]
    r_i = lax.broadcasted_iota(jnp.int32, (LANES, LANES), 0)
    c_i = lax.broadcasted_iota(jnp.int32, (LANES, LANES), 1)
    upper = jnp.where(r_i <= c_i, 1.0, 0.0).astype(BF16)
    r_j = lax.broadcasted_iota(jnp.int32, (PLAN_R, PLAN_R), 0)
    c_j = lax.broadcasted_iota(jnp.int32, (PLAN_R, PLAN_R), 1)
    lower = jnp.where(c_j < r_j, 1.0, 0.0).astype(BF16)
    lane = lax.broadcasted_iota(jnp.int32, (1, LANES), 1)
    pos = jnp.zeros((PLAN_R, LANES), F32)
    cnt_row = jnp.zeros((1, LANES), F32)
    off_row = jnp.zeros((1, LANES), F32)
    end_row, off_list, cnt_list = [], [], []
    row_off = jnp.zeros((1, 1), F32)
    tiles_done = jnp.zeros((1, 1), F32)
    for ex in range(N_EXPERTS):
        m = jnp.where(e == ex, 1.0, 0.0)
        inc = _dot(m.astype(BF16), upper)
        tot = jnp.broadcast_to(inc[:, LANES - 1:LANES], (PLAN_R, LANES))
        before = _dot(lower, tot.astype(BF16))
        cnt = jnp.sum(inc[:, LANES - 1:LANES], axis=0, keepdims=True)
        pos = pos + m * (row_off + before + inc - 1.0)
        tiles = jnp.floor((cnt + (MOE_TM - 1)) * (1.0 / MOE_TM))
        cnt_row = jnp.where(lane == ex, cnt, cnt_row)
        off_row = jnp.where(lane == ex, row_off, off_row)
        cnt_list.append(cnt)
        off_list.append(row_off)
        tiles_done = tiles_done + tiles
        end_row.append(tiles_done)
        row_off = row_off + tiles * MOE_TM
    n_active = tiles_done
    tid = jnp.minimum(lane.astype(F32), n_active - 1.0)
    te = jnp.zeros((1, LANES), F32)
    for ex in range(N_EXPERTS):
        te = te + jnp.where(end_row[ex] <= tid, 1.0, 0.0)
    pos_ref[...] = pos.astype(jnp.int32)
    meta = jnp.concatenate([cnt_row, off_row, te, jnp.broadcast_to(n_active, (1, LANES)),
                            jnp.zeros((4, LANES), F32)], axis=0)
    meta_ref[...] = meta.astype(jnp.int32)
    pr = MOE_ROWS // LANES
    q = lax.broadcasted_iota(jnp.int32, (pr, 1), 0)
    tile = jnp.minimum((q // (MOE_TM // LANES)).astype(F32), n_active - 1.0)
    e_q = jnp.zeros((pr, 1), F32)
    for ex in range(N_EXPERTS):
        e_q = e_q + jnp.where(end_row[ex] <= tile, 1.0, 0.0)
    off_q = jnp.zeros((pr, 1), F32)
    cnt_q = jnp.zeros((pr, 1), F32)
    start_q = jnp.zeros((pr, 1), F32)
    start = jnp.zeros((1, 1), F32)
    for ex in range(N_EXPERTS):
        hit = e_q == float(ex)
        off_q = jnp.where(hit, off_list[ex], off_q)
        cnt_q = jnp.where(hit, cnt_list[ex], cnt_q)
        start_q = jnp.where(hit, start, start_q)
        start = start + cnt_list[ex]
    r = (q * LANES + lane).astype(F32)
    rank = r - off_q
    spare = r - jnp.where(r >= float(N_TOK), float(N_TOK), 0.0) - jnp.where(r >= float(2 * N_TOK), float(N_TOK), 0.0)
    src_ref[...] = jnp.where(rank < cnt_q, start_q + rank, -(spare + 1.0)).astype(jnp.int32)


def _route_plan(eidx):
    e2 = eidx[:2].reshape(PLAN_R, LANES)
    pos, meta, src = pl.pallas_call(
        _plan_kernel,
        out_shape=[jax.ShapeDtypeStruct((PLAN_R, LANES), jnp.int32), jax.ShapeDtypeStruct((8, LANES), jnp.int32),
                   jax.ShapeDtypeStruct((MOE_ROWS // LANES, LANES), jnp.int32)],
        compiler_params=pltpu.CompilerParams(vmem_limit_bytes=32 * 1024 * 1024),
        name="route_plan",
    )(e2)
    te, n_active = meta[2, :MOE_TILES], meta[3, :1]
    order = jnp.argsort(e2.reshape(-1), stable=True).astype(jnp.int32)
    src = src.reshape(-1)
    src_tok = jnp.where(src >= 0, order[jnp.maximum(src, 0)] % N_TOK, -src - 1)
    return src_tok, te, n_active, pos.reshape(-1)


def _combine_kernel(x_ref, y0_ref, y1_ref, gt_ref, mt_ref, fg_ref, o_ref, *, final):
    gt = gt_ref[...]
    g0, g1 = gt[:, 0:1], gt[:, 1:2]
    half = D_MODEL // 2
    y0 = _unpack_bf16_pair(y0_ref[...])
    y1 = _unpack_bf16_pair(y1_ref[...])
    xs = []
    for c in range(2):
        cols = slice(c * half, (c + 1) * half)
        xs.append(x_ref[:, cols] + mt_ref[0, 5:6, cols] * (y0[c] * g0 + y1[c] * g1))
    if final:
        ssq = jnp.sum(xs[0] * xs[0], axis=-1, keepdims=True) + jnp.sum(xs[1] * xs[1], axis=-1, keepdims=True)
        r = lax.rsqrt(ssq * (1.0 / D_MODEL) + EPS)
        xs = [xc * r * fg_ref[:, c * half:(c + 1) * half] for c, xc in enumerate(xs)]
    for c, xc in enumerate(xs):
        o_ref[:, c * half:(c + 1) * half] = xc


def _combine(x, row0, n_rows, ysel, gates_t, mt, final_g, final):
    tm = 256
    b0 = row0 // tm
    return pl.pallas_call(
        functools.partial(_combine_kernel, final=final),
        grid=(n_rows // tm,),
        in_specs=[
            pl.BlockSpec((tm, D_MODEL), lambda i: (i + b0, 0)),
            pl.BlockSpec((None, tm, D_MODEL // 2), lambda i: (0, i, 0)),
            pl.BlockSpec((None, tm, D_MODEL // 2), lambda i: (1, i, 0)),
            pl.BlockSpec((tm, 8), lambda i: (i + b0, 0)),
            pl.BlockSpec((1, 6, D_MODEL), lambda i: ((i + b0) // (SEG // tm), 0, 0)),
            pl.BlockSpec((1, D_MODEL), lambda i: (0, 0)),
        ],
        out_specs=pl.BlockSpec((tm, D_MODEL), lambda i: (i, 0)),
        out_shape=jax.ShapeDtypeStruct((n_rows, D_MODEL), F32),
        compiler_params=_cparams(("arbitrary",), 40),
        name="moe_combine",
    )(x, ysel, ysel, gates_t, mt, final_g)


def _rope_tables(dim):
    half = dim // 2
    inv = ROPE_THETA ** (-jnp.arange(0, half, 2, dtype=F32) / half)
    t = jnp.arange(DEC_SEQ)
    ang_r = (t // GRID_W).astype(F32)[:, None] * inv[None, :]
    ang_c = (t % GRID_W).astype(F32)[:, None] * inv[None, :]
    ang = jnp.concatenate([ang_r, ang_r, ang_c, ang_c], axis=-1)
    cos, sin = jnp.cos(ang), jnp.sin(ang)
    reps = LANES // dim
    cos = jnp.tile(cos, (1, reps))
    sin = jnp.tile(sin, (1, reps))
    sh = dim // 4
    second = (np.arange(LANES) % (2 * sh)) >= sh
    sp = jnp.where(second[None, :], sin, 0.0)
    sm = jnp.where(second[None, :], 0.0, -sin)
    return cos, sp, sm


def _neighbourhood_bias(rpb):
    rows = DEC_SEQ // GRID_W
    kh = min(NA_ROWS, rows)
    r = np.arange(rows)
    r0 = np.clip(r - kh // 2, 0, rows - kh)
    kr = np.arange(rows)
    row_ok = (kr[None, :] >= r0[:, None]) & (kr[None, :] < r0[:, None] + kh)
    c = np.arange(GRID_W)
    ws = np.clip(c - NA_COLS // 2, 0, GRID_W - NA_COLS)
    kc = np.arange(GRID_W)
    col_ok = (kc[None, :] >= ws[:, None]) & (kc[None, :] < ws[:, None] + NA_COLS)
    dc_idx = np.clip(kc[None, :] - c[:, None], -(NA_COLS - 1), NA_COLS - 1) + NA_COLS - 1
    sel_c = ((np.arange(2 * NA_COLS - 1)[:, None, None] == dc_idx[None]) & col_ok[None]).astype(np.float32)
    t = jnp.einsum("hab,bcx->hacx", rpb.astype(F32), sel_c, precision=lax.Precision.HIGHEST)
    t = jnp.where(col_ok[None, None], t * LOG2E, NEG)
    t = jnp.concatenate([t, t], axis=-1)

    def build(t_ref, o_ref):
        left = lax.broadcasted_iota(jnp.int32, (1, LANES), 1) < GRID_W
        neg = jnp.full((GRID_W, LANES), NEG, F32)
        for rr in range(rows):
            for p in range(rows // 2):
                parts = [t_ref[k - rr + NA_ROWS - 1] if row_ok[rr, k] else neg for k in (2 * p, 2 * p + 1)]
                blk = jnp.where(left, parts[0], parts[1]) if (row_ok[rr, 2 * p] or row_ok[rr, 2 * p + 1]) else neg
                o_ref[rr * GRID_W:(rr + 1) * GRID_W, p * LANES:(p + 1) * LANES] = blk

    return pl.pallas_call(
        build,
        grid=(N_HEADS,),
        in_specs=[pl.BlockSpec((None, 2 * NA_ROWS - 1, GRID_W, LANES), lambda h: (h, 0, 0, 0))],
        out_specs=pl.BlockSpec((None, DEC_SEQ, DEC_SEQ), lambda h: (h, 0, 0)),
        out_shape=jax.ShapeDtypeStruct((N_HEADS, DEC_SEQ, DEC_SEQ), F32),
        compiler_params=_cparams(("arbitrary",), 32),
        name="nbr_bias",
    )(t)


def _even_w_in_layout(w, idx):
    cq_end = 3 * N_HEADS * HEAD_DIM + B_Q_RANK
    kv_end = cq_end + B_KV_RANK
    tk = 256

    def layout(w_ref, o_ref):
        o_ref[:, :cq_end] = w_ref[:, :cq_end].astype(BF16)
        o_ref[:, cq_end + 2 * LANES:] = w_ref[:, cq_end:kv_end].astype(BF16)
        kr = w_ref[:, kv_end:].astype(BF16)
        r = lax.broadcasted_iota(jnp.int32, (B_ROPE_DIM, 2 * LANES), 0)
        c = lax.broadcasted_iota(jnp.int32, (B_ROPE_DIM, 2 * LANES), 1)
        place = jnp.where(jnp.logical_or(c == r, c == r + 2 * LANES - B_ROPE_DIM), 1.0, 0.0).astype(BF16)
        o_ref[:, cq_end:cq_end + 2 * LANES] = _dot(kr, place).astype(BF16)

    return pl.pallas_call(
        layout,
        grid=(D_MODEL // tk,),
        in_specs=[pl.BlockSpec((None, tk, w.shape[2]), lambda i: (idx, i, 0))],
        out_specs=pl.BlockSpec((tk, PROJ_W), lambda i: (i, 0)),
        out_shape=jax.ShapeDtypeStruct((D_MODEL, PROJ_W), BF16),
        compiler_params=_cparams(("arbitrary",), 32),
        name="even_w_in_layout",
    )(w)


def _even_layer(x, mt, l, i, rope64, cache_a_k, cache_a_v, cache_b_ckv, cache_b_krope, norm_mix_g, ev_w_in,
                ev_lambda, ev_subln_g, ev_q_norm_g, ev_kv_norm_g, ev_w_uq, ev_w_ukv, ev_w_out):
    w_in = _even_w_in_layout(ev_w_in, i)
    all_g = (0, 1, 2, 3)
    a_scale = A_QK_DIM ** -0.5 * LOG2E
    in_kw = dict(prologue="modulate", g=norm_mix_g[l][None, :], norm_tile=8, norm_g=ev_kv_norm_g[i][None, :],
                 out_dtype=BF16, tile_scale={0: a_scale, 1: a_scale}, tps=3)
    bcq_src = (0, ((6, 0, TN), (7, 0, TN)))
    head_src = lambda t0: tuple((t0 + t, gq, 4 * t + gq) for t in range(2) for gq in range(4))
    proj_p, st_ak, st_av, st_ckv, st_kr, bcq_p = _projection(
        *x["ctx"], N_SEG_P, D_MODEL, 0, D_MODEL, w_in, mt=mt[:N_SEG_P], tm=SEG // 2, name="even_in_proj_ctx",
        states=((N_HEADS, head_src(2)), (N_HEADS, head_src(4)), (0, ((8, 0, B_KV_RANK),)), (0, ((7, 256, B_ROPE_DIM),)),
                bcq_src), **in_kw)
    proj_l, bcq_l = _projection(*x["lat"], DEC_BATCH, D_MODEL, 0, D_MODEL, w_in, mt=mt[N_SEG_P:], rope=rope64,
                                sh=A_QK_DIM // 4, rope_groups={0: all_g, 1: all_g, 2: all_g, 3: all_g, 7: (2, 3)},
                                states=(bcq_src,), name="even_in_proj_lat", **in_kw)
    wq = ev_w_uq[i].reshape(B_Q_RANK, N_HEADS, HEAD_DIM + B_ROPE_DIM)
    w_uq = jnp.concatenate([wq[:, :, :HEAD_DIM].reshape(B_Q_RANK, -1), wq[:, :, HEAD_DIM:].reshape(B_Q_RANK, -1)],
                           axis=1).astype(BF16)
    mla_scale = (HEAD_DIM + B_ROPE_DIM) ** -0.5
    q_kw = dict(prologue="rmsnorm", g=ev_q_norm_g[i][None, :], out_dtype=BF16, tn=w_uq.shape[1],
                tile_scale={0: mla_scale * LOG2E})
    bq_p = _projection(bcq_p, 0, N_SEG_P, 1024, 0, B_Q_RANK, w_uq, name="mla_q_up_ctx", **q_kw)
    bq_l = _projection(bcq_l, 0, DEC_BATCH, 1024, 0, B_Q_RANK, w_uq, rope=rope64, sh=B_ROPE_DIM // 4,
                       rope_groups={0: (8, 9, 10, 11)}, name="mla_q_up_lat", **q_kw)
    w_ukv = ev_w_ukv[i].astype(BF16)
    kv_kw = dict(prologue="cast", out_dtype=BF16, tn=w_ukv.shape[1])
    kv_p = _projection(proj_p, 0, N_SEG_P, 512, 8, B_KV_RANK, w_ukv, name="mla_kv_up_ctx", **kv_kw)
    kv_l = _projection(proj_l, 0, DEC_BATCH, 512, 8, B_KV_RANK, w_ukv, name="mla_kv_up_lat", **kv_kw)
    kv_ctx = _projection(cache_b_ckv[:, i].reshape(DEC_BATCH * PAST_LEN, B_KV_RANK), 0, 1, 512, 0, B_KV_RANK, w_ukv,
                         name="mla_kv_up_cache", **kv_kw)
    kv_ctx = kv_ctx.reshape(DEC_BATCH, PAST_LEN, 2 * N_HEADS * HEAD_DIM)
    krc = cache_b_krope[:, i]
    zc = jnp.zeros_like(krc)
    kr_ctx = jnp.concatenate([krc, zc, zc, krc], axis=-1)
    ak_ctx = cache_a_k[:, i].reshape(DEC_BATCH, PAST_LEN, N_HEADS * HEAD_DIM)
    av_ctx = cache_a_v[:, i].reshape(DEC_BATCH, PAST_LEN, N_HEADS * HEAD_DIM)
    lam_init = 0.8 - 0.6 * math.exp(-0.3 * l)
    o = jnp.zeros((N_TOK, D_MODEL), BF16)
    for latent, proj, b_q, kv in ((False, proj_p, bq_p, kv_p), (True, proj_l, bq_l, kv_l)):
        tag = "lat" if latent else "ctx"
        o = _attention(kind="diff", latent=latent, q_parts=[(proj, lambda h: h)], ko_parts=[(proj, lambda h: 8 + h)],
                       vo=(proj, lambda h: 16 + h), kc_parts=[(ak_ctx, lambda h: h)], vc=(av_ctx, lambda h: h),
                       lam=ev_lambda[i], subln=ev_subln_g[i][None, :], o_arr=o, o_blk=lambda h: h,
                       lam_init=lam_init, name="diff_attn_" + tag)
        o = _attention(kind="mla", latent=latent, q_parts=[(b_q, lambda h: h), (b_q, lambda h: 8 + h // 2)],
                       ko_parts=[(kv, lambda h: 2 * h), (proj, lambda h: 30 + h % 2)], vo=(kv, lambda h: 2 * h + 1),
                       kc_parts=[(kv_ctx, lambda h: 2 * h), (kr_ctx, lambda h: h % 2)], vc=(kv_ctx, lambda h: 2 * h + 1),
                       o_arr=o, o_blk=lambda h: 8 + h, name="mla_attn_" + tag)
    states = (st_ak.reshape(BATCH, SEQ, N_HEADS, HEAD_DIM), st_av.reshape(BATCH, SEQ, N_HEADS, HEAD_DIM),
              st_ckv.reshape(BATCH, SEQ, B_KV_RANK), st_kr.reshape(BATCH, SEQ, B_ROPE_DIM))
    return (o, ev_w_out[i].astype(BF16)), states


def _odd_layer(x, mt, l, i, rope128, cache_c_k, cache_c_v, cache_d_k, cache_d_v, norm_mix_g, od_w_in, od_rpb,
               od_sink, od_w_out):
    w_in = od_w_in[i].astype(BF16)
    all_g = (0, 1, 2, 3)
    q_scale = HEAD_DIM ** -0.5 * LOG2E
    in_kw = dict(prologue="modulate", g=norm_mix_g[l][None, :], out_dtype=BF16,
                 tile_scale={t: q_scale for t in (0, 1, 6, 7)}, tps=3)
    head_src = lambda t0: tuple((t0 + t, gq, 4 * t + gq) for t in range(2) for gq in range(4))
    proj_p, st_ck, st_cv, st_dk, st_dv = _projection(
        *x["ctx"], N_SEG_P, D_MODEL, 0, D_MODEL, w_in, mt=mt[:N_SEG_P], tm=SEG // 2, name="odd_in_proj_ctx",
        states=((N_HEADS, head_src(2)), (N_HEADS, head_src(4)), (D_KV_HEADS, ((8, 0, 0), (8, 1, 1))),
                (D_KV_HEADS, ((8, 2, 0), (8, 3, 1)))), **in_kw)
    proj_l = _projection(*x["lat"], DEC_BATCH, D_MODEL, 0, D_MODEL, w_in, mt=mt[N_SEG_P:], rope=rope128,
                         sh=HEAD_DIM // 4, rope_groups={6: all_g, 7: all_g, 8: (0, 1)}, name="odd_in_proj_lat", **in_kw)
    ck_ctx = cache_c_k[:, i].reshape(DEC_BATCH, PAST_LEN, N_HEADS * HEAD_DIM)
    cv_ctx = cache_c_v[:, i].reshape(DEC_BATCH, PAST_LEN, N_HEADS * HEAD_DIM)
    dk_ctx = cache_d_k[:, i].reshape(DEC_BATCH, PAST_LEN, D_KV_HEADS * HEAD_DIM)
    dv_ctx = cache_d_v[:, i].reshape(DEC_BATCH, PAST_LEN, D_KV_HEADS * HEAD_DIM)
    bias = _neighbourhood_bias(od_rpb[i])
    sink = od_sink[i].astype(F32)
    o = jnp.zeros((N_TOK, D_MODEL), BF16)
    for latent, proj in ((False, proj_p), (True, proj_l)):
        tag = "lat" if latent else "ctx"
        o = _attention(kind="plain", latent=latent, q_parts=[(proj, lambda h: h)], ko_parts=[(proj, lambda h: 8 + h)],
                       vo=(proj, lambda h: 16 + h), kc_parts=[(ck_ctx, lambda h: h)], vc=(cv_ctx, lambda h: h),
                       bias=bias if latent else None, o_arr=o, o_blk=lambda h: h,
                       name="nbr_attn_" + tag)
        o = _attention(kind="plain", latent=latent, q_parts=[(proj, lambda h: 24 + h)],
                       ko_parts=[(proj, lambda h: 32 + h // D_GROUP)], vo=(proj, lambda h: 34 + h // D_GROUP),
                       kc_parts=[(dk_ctx, lambda h: h // D_GROUP)], vc=(dv_ctx, lambda h: h // D_GROUP), sink=sink,
                       o_arr=o, o_blk=lambda h: 8 + h, has_band=latent, tq_lat=256,
                       name="win_attn_" + tag)
    states = (st_ck.reshape(BATCH, SEQ, N_HEADS, HEAD_DIM), st_cv.reshape(BATCH, SEQ, N_HEADS, HEAD_DIM),
              st_dk.reshape(BATCH, SEQ, D_KV_HEADS, HEAD_DIM), st_dv.reshape(BATCH, SEQ, D_KV_HEADS, HEAD_DIM))
    return (o, od_w_out[i].astype(BF16)), states


def _moe_layer(mix, x_parts, mt, g, router_w, router_b, layer, w_gate, w_up, w_down, final_g, final):
    x, h2, eidx, gates = _out_projection_router(*mix, x_parts, mt, g, router_w, router_b)
    src_tok, te, n_active, pos = _route_plan(eidx)
    ys = _moe_experts(h2, src_tok, te, n_active, layer, w_gate, w_up, w_down)
    pos2 = pos.reshape(2, N_TOK)
    gates_t = gates.T
    outs = []
    for row0, n_rows in ((0, N_PROMPT), (N_PROMPT, N_TOK - N_PROMPT)):
        ysel = ys.at[pos2[:, row0:row0 + n_rows].reshape(-1)].get(mode="promise_in_bounds")
        outs.append(_combine(x, row0, n_rows, ysel.reshape(2, n_rows, D_MODEL // 2), gates_t, mt, final_g, final))
    return outs


def kernel(x_prompt, x_sample, cache_a_k, cache_a_v, cache_b_ckv, cache_b_krope, cache_c_k, cache_c_v, cache_d_k, cache_d_v, c, c_ctx, w_ada, b_ada, norm_mix_g, norm_ffn_g, ev_w_in, ev_lambda, ev_subln_g, ev_q_norm_g, ev_kv_norm_g, ev_w_uq, ev_w_ukv, ev_w_out, od_w_in, od_rpb, od_sink, od_w_out, router_w, router_b, moe_w_gate, moe_w_up, moe_w_down, final_g):
    xp = x_prompt.reshape(N_PROMPT, D_MODEL)
    xl = x_sample.reshape(DEC_BATCH * DEC_SEQ, D_MODEL)
    x = dict(ctx=(xp, 0), lat=(xl, 0), parts=[xp, xl])
    cond8 = jnp.concatenate([c_ctx[None, :], c, jnp.zeros((3, D_MODEL), F32)], axis=0)
    mod = _adaln(cond8, w_ada, b_ada)
    seg_row = np.array([0] * N_SEG_P + [1 + b for b in range(DEC_BATCH)])
    mt_all = mod[:, seg_row].reshape(DEPTH, N_SEG, 6, D_MODEL)
    rope64 = _rope_tables(A_QK_DIM)
    rope128 = _rope_tables(HEAD_DIM)
    even_states, odd_states = [], []
    for l in range(DEPTH):
        i = l // 2
        mt = mt_all[l]
        if l % 2 == 0:
            mix, st = _even_layer(x, mt, l, i, rope64, cache_a_k, cache_a_v, cache_b_ckv, cache_b_krope, norm_mix_g,
                                ev_w_in, ev_lambda, ev_subln_g, ev_q_norm_g, ev_kv_norm_g, ev_w_uq, ev_w_ukv, ev_w_out)
            even_states.append(st)
        else:
            mix, st = _odd_layer(x, mt, l, i, rope128, cache_c_k, cache_c_v, cache_d_k, cache_d_v, norm_mix_g,
                               od_w_in, od_rpb, od_sink, od_w_out)
            odd_states.append(st)
        x = _moe_layer(mix, x["parts"], mt, norm_ffn_g[l][None, :], router_w, router_b, l, moe_w_gate, moe_w_up, moe_w_down,
                       final_g[None, :], final=(l == DEPTH - 1))
        if l < DEPTH - 1:
            x = dict(ctx=(x[0], 0), lat=(x[1], 0), parts=list(x))
    y_prompt = x[0].reshape(BATCH, SEQ, D_MODEL)
    y_sample = x[1].reshape(DEC_BATCH, DEC_SEQ, D_MODEL)
    new_even = tuple(jnp.stack([st[k] for st in even_states], axis=1) for k in range(4))
    new_odd = tuple(jnp.stack([st[k] for st in odd_states], axis=1) for k in range(4))
    return (y_prompt, y_sample) + new_even + new_odd
```

```python
import functools
import math

import numpy as np
import jax
import jax.numpy as jnp
from jax import lax
from jax.experimental import pallas as pl
from jax.experimental.pallas import tpu as pltpu

D_MODEL = 2048
BATCH = 32
SEQ = 256
DEPTH = 2
DEC_BATCH = 4
DEC_SEQ = 1024
PAST_LEN = 256
GRID_W = 64
HEAD_DIM = 128
N_HEADS = 8
A_QK_DIM = 64
B_Q_RANK = 768
B_KV_RANK = 512
B_ROPE_DIM = 64
NA_ROWS = 8
NA_COLS = 16
D_KV_HEADS = 2
D_GROUP = 4
D_WINDOW = 128
N_EXPERTS = 16
N_GROUPS = 4
D_EXPERT = 1024
ROPE_THETA = 10000.0
EPS = 1e-6
NEG = -1e30
LOG2E = 1.4426950408889634

SEG = 1024
N_PROMPT = BATCH * SEQ
N_TOK = N_PROMPT + DEC_BATCH * DEC_SEQ
N_SEG = N_TOK // SEG
N_SEG_P = N_PROMPT // SEG
PROJ_W = 4608
TN = 512
LANES = 128
MOE_TM = 256
MOE_ROWS = 2 * N_TOK + N_EXPERTS * MOE_TM
MOE_TILES = MOE_ROWS // MOE_TM

F32 = jnp.float32
BF16 = jnp.bfloat16


def _cparams(sem, vmem_mb):
    return pltpu.CompilerParams(dimension_semantics=sem, vmem_limit_bytes=vmem_mb * 1024 * 1024)


def _dot(a, b):
    return jnp.dot(a, b, preferred_element_type=F32)


def _dot_nt(a, b):
    return lax.dot_general(a, b, (((1,), (1,)), ((), ())), preferred_element_type=F32)


def _sigmoid(x):
    return 1.0 / (1.0 + jnp.exp(-x))


def _pack_bf16_pair(x):
    c = x.shape[1] // 2
    bits = pltpu.bitcast(x.astype(BF16).astype(F32), jnp.int32)
    return bits[:, :c] | lax.shift_right_logical(bits[:, c:], 16)


def _unpack_bf16_pair(w):
    hi = pltpu.bitcast(w & jnp.int32(-65536), F32)
    lo = pltpu.bitcast(lax.shift_left(w, 16), F32)
    return hi, lo


def _adaln_kernel(c_ref, w_ref, b_ref, o_ref):
    c = c_ref[...]
    a = (c * _sigmoid(c)).astype(BF16)
    o_ref[...] = _dot(a, w_ref[...].astype(BF16)) + b_ref[...]


def _adaln(cond8, w_ada, b_ada):
    tn = 1024
    n = 6 * D_MODEL
    return pl.pallas_call(
        _adaln_kernel,
        grid=(DEPTH, n // tn),
        in_specs=[
            pl.BlockSpec((8, D_MODEL), lambda l, j: (0, 0)),
            pl.BlockSpec((None, D_MODEL, tn), lambda l, j: (l, 0, j)),
            pl.BlockSpec((None, 1, tn), lambda l, j: (l, 0, j)),
        ],
        out_specs=pl.BlockSpec((None, 8, tn), lambda l, j: (l, 0, j)),
        out_shape=jax.ShapeDtypeStruct((DEPTH, 8, n), F32),
        compiler_params=_cparams(("arbitrary", "arbitrary"), 40),
        name="adaln",
    )(cond8, w_ada, b_ada.reshape(DEPTH, 1, n))


def _rope(a, cos, sp, sm, sh):
    return a * cos + pltpu.roll(a, sh, 1) * sp + pltpu.roll(a, LANES - sh, 1) * sm


def _proj_kernel(*refs, prologue, k_in, sh, n_tiles, tn, tps, tm, rope_groups, norm_tile, states, tile_scale):
    x_ref, mt_ref, g_ref, w_ref, cos_ref, sp_ref, sm_ref, ng_ref, o_ref = refs[:9]
    st_refs = refs[9:-1]
    xn_ref = refs[-1]
    j = pl.program_id(1)

    @pl.when(j == 0)
    def _():
        x = x_ref[:, :k_in]
        if prologue == "cast":
            xn_ref[...] = x.astype(BF16)
        else:
            r = lax.rsqrt(jnp.mean(x * x, axis=-1, keepdims=True) + EPS)
            if prologue == "modulate":
                y = x * r * (g_ref[...] * (1.0 + mt_ref[0, 1:2, :])) + mt_ref[0, 0:1, :]
            else:
                y = x * r * g_ref[...]
            xn_ref[...] = y.astype(BF16)

    n_grp = tn // LANES

    def treatment(t):
        acts = []
        for k, (n_heads, src) in enumerate(states):
            dest = 0
            for tt, a, b in src:
                if tt == t:
                    acts.append((k, a, b, dest))
                dest += 0 if n_heads else b
        return (tuple(rope_groups.get(t, ())), t == norm_tile, tuple(acts), float(tile_scale.get(t, 1.0)))

    def column_tile(t, u):
        rg, is_norm, acts, sc = treatment(t)
        acc = _dot(xn_ref[...], w_ref[:, u * tn:(u + 1) * tn])
        if is_norm:
            scale = lax.rsqrt(jnp.mean(acc * acc, axis=-1, keepdims=True) + EPS)
            acc = acc * scale * ng_ref[...]
        vals = [acc[:, g * LANES:(g + 1) * LANES] for g in range(n_grp)]
        if sc != 1.0:
            vals = [v * sc for v in vals]
        if rg:
            cos, sp, sm = cos_ref[...], sp_ref[...], sm_ref[...]
            vals = [_rope(v, cos, sp, sm, sh) if g in rg else v for g, v in enumerate(vals)]
        for g, v in enumerate(vals):
            o_ref[:, u * tn + g * LANES:u * tn + (g + 1) * LANES] = v.astype(o_ref.dtype)
        for k, a, b, dest in acts:
            n_heads = states[k][0]
            if n_heads:
                st_refs[k][pl.ds(b, tm, stride=n_heads), :] = vals[a]
            elif b % LANES == 0:
                for q in range(b // LANES):
                    st_refs[k][:, dest + q * LANES:dest + (q + 1) * LANES] = vals[a // LANES + q]
            else:
                st_refs[k][...] = vals[a // LANES][:, a % LANES:a % LANES + b]

    n_steps = n_tiles // tps
    for jj in range(n_steps):
        def step(jj=jj):
            for u in range(tps):
                column_tile(jj * tps + u, u)
        if n_steps == 1:
            step()
        else:
            pl.when(j == jj)(step)


def _projection(x, row_off, n_seg, x_blk_w, x_blk_idx, k_in, w, *, prologue, mt=None, g=None, rope=None, sh=16,
                rope_groups=None, norm_tile=None, norm_g=None, states=(), out_dtype=F32, tile_scale=None, tn=TN,
                tps=1, tm=SEG, name):
    n = w.shape[1]
    m = n_seg * SEG
    per_seg = SEG // tm
    assert n % (tn * tps) == 0 and w.shape[0] == k_in and SEG % tm == 0
    assert tm == SEG or not rope_groups
    if mt is None:
        mt = jnp.zeros((n_seg, 6, LANES), F32)
    if g is None:
        g = jnp.ones((1, k_in), F32)
    if rope is None:
        rope = tuple(jnp.zeros((8, LANES), F32) for _ in range(3))
    if norm_g is None:
        norm_g = jnp.ones((1, tn), F32)
    kern = functools.partial(_proj_kernel, prologue=prologue, k_in=k_in, sh=sh, n_tiles=n // tn, tn=tn, tps=tps,
                             tm=tm, rope_groups=dict(rope_groups or {}), norm_tile=norm_tile, states=tuple(states),
                             tile_scale=dict(tile_scale or {}))
    full2 = lambda i, j: (0, 0)
    out_shape = [jax.ShapeDtypeStruct((m, n), out_dtype)]
    out_specs = [pl.BlockSpec((tm, tn * tps), lambda i, j: (i, j))]
    for n_heads, src in states:
        if n_heads:
            out_shape.append(jax.ShapeDtypeStruct((m * n_heads, LANES), F32))
            out_specs.append(pl.BlockSpec((tm * n_heads, LANES), lambda i, j: (i, 0)))
        else:
            width = sum(b for _, _, b in src)
            out_shape.append(jax.ShapeDtypeStruct((m, width), F32))
            out_specs.append(pl.BlockSpec((tm, width), lambda i, j: (i, 0)))
    x_kw = dict(pipeline_mode=pl.Buffered(1)) if (states and tm == SEG) else {}
    res = pl.pallas_call(
        kern,
        grid=(n_seg * per_seg, n // (tn * tps)),
        in_specs=[
            pl.BlockSpec((tm, x_blk_w), lambda i, j: (i + row_off * per_seg, x_blk_idx), **x_kw),
            pl.BlockSpec((1, 6, mt.shape[2]), lambda i, j: (i // per_seg, 0, 0)),
            pl.BlockSpec((1, k_in), full2),
            pl.BlockSpec((k_in, tn * tps), lambda i, j: (0, j)),
            pl.BlockSpec(rope[0].shape, full2),
            pl.BlockSpec(rope[1].shape, full2),
            pl.BlockSpec(rope[2].shape, full2),
            pl.BlockSpec((1, tn), full2),
        ],
        out_specs=out_specs,
        out_shape=out_shape,
        scratch_shapes=[pltpu.VMEM((tm, k_in), BF16)],
        compiler_params=_cparams(("arbitrary", "arbitrary"), 56 if states else 48),
        name=name,
    )(x, mt, g, w, *rope, norm_g)
    return res if states else res[0]


def _attn_kernel(*refs, kind, n_parts, has_ctx, has_bias, has_band, has_sink, nb, s_len, tq, lam_init, own_win):
    it = iter(refs)
    q_refs = [next(it) for _ in range(n_parts)]
    ko_refs = [next(it) for _ in range(n_parts)]
    vo_ref = next(it)
    kc_refs, vc_ref = [], None
    if has_ctx:
        kc_refs = [next(it) for _ in range(n_parts)]
        vc_ref = next(it)
    bias_ref = next(it) if has_bias else None
    lam_ref = sg_ref = None
    if kind == "diff":
        lam_ref = next(it)
        sg_ref = next(it)
    sink_ref = next(it) if has_sink else None
    o_ref = next(it)

    sink2 = sink_ref[pl.program_id(1)] * LOG2E if has_sink else None
    lane = lax.broadcasted_iota(jnp.int32, (1, LANES), 1)

    def softmax_pv(s_list, v_list):
        m = None
        for s in s_list:
            mm = jnp.max(s, axis=-1, keepdims=True)
            m = mm if m is None else jnp.maximum(m, mm)
        if sink2 is not None:
            m = jnp.maximum(m, sink2)
        l = jnp.exp2(sink2 - m) if sink2 is not None else None
        o = None
        for s, v in zip(s_list, v_list):
            e = jnp.exp2(s - m)
            ss = jnp.sum(e, axis=-1, keepdims=True)
            l = ss if l is None else l + ss
            pv = _dot(e.astype(BF16), v)
            o = pv if o is None else o + pv
        return o, l

    for bi in range(nb):
        qrows = pl.ds(bi * tq, tq)
        q0 = (pl.program_id(2) * nb + bi) * tq
        if own_win is None:
            krows, k0 = pl.ds(0 if has_ctx else bi * s_len, s_len), 0
            n_own = s_len
        else:
            k0 = pl.multiple_of(jnp.clip(q0 - D_WINDOW, 0, s_len - own_win), LANES)
            krows = pl.ds(k0, own_win)
            n_own = own_win
        k_srcs, v_srcs, is_own = [], [], []
        if has_ctx:
            k_srcs.append([r[...].astype(BF16) for r in kc_refs])
            v_srcs.append(vc_ref[...].astype(BF16))
            is_own.append(False)
        k_srcs.append([r[krows, :].astype(BF16) for r in ko_refs])
        v_srcs.append(vo_ref[krows, :].astype(BF16))
        is_own.append(True)

        def mask_own(s):
            if has_bias:
                s = s + bias_ref[qrows, :]
            if has_band:
                qpos = q0 + lax.broadcasted_iota(jnp.int32, (tq, 1), 0)
                kpos = k0 + lax.broadcasted_iota(jnp.int32, (1, n_own), 1)
                s = jnp.where(jnp.abs(kpos - qpos) <= D_WINDOW, s, NEG)
            return s

        if kind == "diff":
            q = q_refs[0][qrows, :]
            zero = jnp.zeros_like(q)
            q1 = jnp.where(lane < A_QK_DIM, q, zero)
            q2 = jnp.where(lane < A_QK_DIM, zero, q)
            o1, l1 = softmax_pv([_dot_nt(q1, ks[0]) for ks in k_srcs], v_srcs)
            o2, l2 = softmax_pv([_dot_nt(q2, ks[0]) for ks in k_srcs], v_srcs)
            lv = lam_ref[...]
            lam = (jnp.exp(jnp.sum(lv[0:1] * lv[1:2], axis=-1, keepdims=True))
                   - jnp.exp(jnp.sum(lv[2:3] * lv[3:4], axis=-1, keepdims=True)) + lam_init)
            o = o1 * (1.0 / l1) - o2 * (lam / l2)
            o = o * lax.rsqrt(jnp.mean(o * o, axis=-1, keepdims=True) + EPS) * sg_ref[...] * (1.0 - lam_init)
        else:
            qs = [r[qrows, :] for r in q_refs]
            s_list = []
            for ks, own in zip(k_srcs, is_own):
                s = None
                for qp, kp in zip(qs, ks):
                    d = _dot_nt(qp, kp)
                    s = d if s is None else s + d
                s_list.append(mask_own(s) if own else s)
            o, l = softmax_pv(s_list, v_srcs)
            o = o * (1.0 / l)
        o_ref[qrows, :] = o.astype(o_ref.dtype)


def _attention(*, kind, latent, q_parts, ko_parts, vo, kc_parts=(), vc=None, bias=None, lam=None, subln=None,
               sink=None, o_arr, o_blk, lam_init=0.0, has_band=False, tq_lat=512, name):
    n_parts = len(q_parts)
    has_ctx = latent
    if latent:
        tq, s_len, nb = tq_lat, DEC_SEQ, DEC_SEQ // tq_lat
        qblk, kblk = nb * tq, SEG
        grid = (DEC_BATCH, N_HEADS, DEC_SEQ // qblk)
        qpb = DEC_SEQ // qblk
        q_row = lambda b, h, t: b * qpb + t
        o_row = lambda b, h, t: N_PROMPT // qblk + b * qpb + t
        k_row = lambda b, h, t: b
    else:
        qblk = kblk = 4 * SEG
        tq, s_len, nb = SEQ, SEQ, qblk // SEQ
        grid = (N_PROMPT // qblk, N_HEADS, 1)
        q_row = o_row = k_row = lambda b, h, t: b
    args, specs = [], []

    def add(arr, shape, imap, **kw):
        args.append(arr)
        specs.append(pl.BlockSpec(shape, imap, **kw))

    for arr, f in q_parts:
        add(arr, (qblk, LANES), lambda b, h, t, f=f: (q_row(b, h, t), f(h)))
    for arr, f in list(ko_parts) + [vo]:
        add(arr, (kblk, LANES), lambda b, h, t, f=f: (k_row(b, h, t), f(h)))
    if has_ctx:
        for arr, f in list(kc_parts) + [vc]:
            add(arr, (None, PAST_LEN, LANES), lambda b, h, t, f=f: (b, 0, f(h)))
    if bias is not None:
        add(bias, (None, qblk, DEC_SEQ), lambda b, h, t: (h, t, 0))
    if kind == "diff":
        add(lam, lam.shape, lambda b, h, t: (0, 0))
        add(subln, subln.shape, lambda b, h, t: (0, 0))
    if sink is not None:
        args.append(sink)
        specs.append(pl.BlockSpec(memory_space=pltpu.SMEM))
    n_in = len(args)
    args.append(o_arr)
    specs.append(pl.BlockSpec(memory_space=pl.ANY))
    kern = functools.partial(_attn_kernel_aliased, kind=kind, n_parts=n_parts, has_ctx=has_ctx,
                             has_bias=bias is not None, has_band=has_band, has_sink=sink is not None,
                             nb=nb, s_len=s_len, tq=tq, lam_init=lam_init,
                             own_win=(tq + 2 * D_WINDOW) if has_band else None)
    return pl.pallas_call(
        kern,
        grid=grid,
        in_specs=specs,
        out_specs=pl.BlockSpec((qblk, LANES), lambda b, h, t: (o_row(b, h, t), o_blk(h))),
        out_shape=jax.ShapeDtypeStruct(o_arr.shape, o_arr.dtype),
        input_output_aliases={n_in: 0},
        compiler_params=_cparams(("arbitrary", "arbitrary", "arbitrary"), 48),
        name=name,
    )(*args)


def _attn_kernel_aliased(*refs, **kw):
    _attn_kernel(*refs[:-2], refs[-1], **kw)


def _ffn_norm_route(x, mt_ref, g_ref, w_ref, rb_ref):
    r = lax.rsqrt(jnp.mean(x * x, axis=-1, keepdims=True) + EPS)
    h = x * r * (g_ref[...] * (1.0 + mt_ref[0, 4:5, :])) + mt_ref[0, 3:4, :]
    hh = h.astype(BF16)
    packed = _pack_bf16_pair(hh)
    logits = _dot_nt(w_ref[...], hh)
    scores = _sigmoid(logits)
    sel = scores + rb_ref[...]
    per = N_EXPERTS // N_GROUPS
    sc = [scores[e:e + 1, :] for e in range(N_EXPERTS)]
    sl = [sel[e:e + 1, :] for e in range(N_EXPERTS)]
    best_g, best_v = None, None
    for gi in range(N_GROUPS):
        a, b, c, d = sl[gi * per:(gi + 1) * per]
        hi1, lo1, hi2, lo2 = jnp.maximum(a, b), jnp.minimum(a, b), jnp.maximum(c, d), jnp.minimum(c, d)
        gs = jnp.maximum(hi1, hi2) + jnp.maximum(jnp.minimum(hi1, hi2), jnp.maximum(lo1, lo2))
        if gi == 0:
            best_g, best_v = jnp.zeros_like(gs, dtype=jnp.int32), gs
        else:
            better = gs > best_v
            best_g = jnp.where(better, gi, best_g)
            best_v = jnp.where(better, gs, best_v)
    masked = [jnp.where(best_g == (e // per), sl[e], NEG) for e in range(N_EXPERTS)]
    i1, v1 = jnp.zeros_like(best_g), masked[0]
    for e in range(1, N_EXPERTS):
        better = masked[e] > v1
        i1 = jnp.where(better, e, i1)
        v1 = jnp.where(better, masked[e], v1)
    i2, v2 = None, None
    for e in range(N_EXPERTS):
        cand = jnp.where(i1 == e, -2e30, masked[e])
        if e == 0:
            i2, v2 = jnp.zeros_like(best_g), cand
        else:
            better = cand > v2
            i2 = jnp.where(better, e, i2)
            v2 = jnp.where(better, cand, v2)
    g1 = jnp.zeros_like(v1)
    g2 = jnp.zeros_like(v1)
    for e in range(N_EXPERTS):
        g1 = jnp.where(i1 == e, sc[e], g1)
        g2 = jnp.where(i2 == e, sc[e], g2)
    tot = g1 + g2
    rows = i1.shape[1]
    ids = jnp.concatenate([i1, i2, jnp.zeros((6, rows), jnp.int32)], axis=0)
    gates = jnp.concatenate([g1 / tot, g2 / tot, jnp.zeros((6, rows), F32)], axis=0)
    return packed, ids, gates


def _outproj_router_kernel(o_ref, w_ref, mt_ref, g_ref, rw_ref, rb_ref, *rest, blk_ranges, units):
    x_refs = rest[:-4]
    y_ref, h_ref, e_ref, gt_ref = rest[-4:]
    i = pl.program_id(0)
    n = o_ref.shape[0] // units
    for u in range(units):
        rows = pl.ds(u * n, n)
        x = x_refs[0][rows, :]
        for x_ref, (lo, _) in zip(x_refs[1:], blk_ranges[1:]):
            x = jnp.where(i >= lo, x_ref[rows, :], x)
        x1 = x + mt_ref[0, 2:3, :] * _dot(o_ref[rows, :], w_ref[...])
        y_ref[rows, :] = x1
        packed, ids, gates = _ffn_norm_route(x1, mt_ref, g_ref, rw_ref, rb_ref)
        h_ref[rows, :] = packed
        e_ref[:, u * n:(u + 1) * n] = ids
        gt_ref[:, u * n:(u + 1) * n] = gates


def _out_projection_router(o, w, x_parts, mt, g, router_w, router_b):
    units = 2
    tm = units * 256
    blk_ranges, lo = [], 0
    for xp in x_parts:
        blk_ranges.append((lo, lo + xp.shape[0] // tm))
        lo = blk_ranges[-1][1]
    assert lo == N_TOK // tm
    x_specs = [pl.BlockSpec((tm, D_MODEL), lambda i, lo=lo, hi=hi: (jnp.clip(i - lo, 0, hi - lo - 1), 0))
               for lo, hi in blk_ranges]
    const = lambda i: (0, 0)
    return pl.pallas_call(
        functools.partial(_outproj_router_kernel, blk_ranges=tuple(blk_ranges), units=units),
        grid=(N_TOK // tm,),
        in_specs=[
            pl.BlockSpec((tm, D_MODEL), lambda i: (i, 0)),
            pl.BlockSpec((D_MODEL, D_MODEL), const, pipeline_mode=pl.Buffered(1)),
            pl.BlockSpec((1, 6, D_MODEL), lambda i: (i // (SEG // tm), 0, 0)),
            pl.BlockSpec((1, D_MODEL), const),
            pl.BlockSpec((N_EXPERTS, D_MODEL), const),
            pl.BlockSpec((N_EXPERTS, 1), const),
        ] + x_specs,
        out_specs=[
            pl.BlockSpec((tm, D_MODEL), lambda i: (i, 0)),
            pl.BlockSpec((tm, D_MODEL // 2), lambda i: (i, 0)),
            pl.BlockSpec((8, tm), lambda i: (0, i)),
            pl.BlockSpec((8, tm), lambda i: (0, i)),
        ],
        out_shape=[
            jax.ShapeDtypeStruct((N_TOK, D_MODEL), F32),
            jax.ShapeDtypeStruct((N_TOK, D_MODEL // 2), jnp.int32),
            jax.ShapeDtypeStruct((8, N_TOK), jnp.int32),
            jax.ShapeDtypeStruct((8, N_TOK), F32),
        ],
        compiler_params=_cparams(("arbitrary",), 56),
        name="out_proj_router",
    )(o, w, mt, g, router_w.T.astype(BF16), router_b.reshape(N_EXPERTS, 1), *x_parts)


def _expert_weights(te_ref, first_ref, slot_ref, nxt_ref, i, t, layer, w_hbm, stage_ref, sem_ref, w_bf_refs):
    def copies(e, s):
        return [pltpu.make_async_copy(w.at[layer, e], stage_ref.at[s, k], sem_ref.at[s, k])
                for k, w in enumerate(w_hbm)]

    s = slot_ref[t]

    @pl.when(i == 0)
    def _():
        for c in copies(te_ref[t], s):
            c.start()

    @pl.when(first_ref[t] == 1)
    def _():
        for c in copies(te_ref[t], s):
            c.wait()

        @pl.when(nxt_ref[t] >= 0)
        def _():
            for c in copies(nxt_ref[t], 1 - s):
                c.start(priority=1)

        for k, w_bf in enumerate(w_bf_refs):
            w_bf[...] = stage_ref[s, k].astype(BF16)


def _moe_up_kernel(te_ref, na_ref, first_ref, slot_ref, nxt_ref, xs_ref, wg_hbm, wu_hbm, hid_ref,
                   stage_ref, wgb_ref, wub_ref, sem_ref, *, tile0, layer):
    i = pl.program_id(0)
    t = i + tile0
    _expert_weights(te_ref, first_ref, slot_ref, nxt_ref, i, t, layer, (wg_hbm, wu_hbm), stage_ref, sem_ref,
                    (wgb_ref, wub_ref))

    @pl.when(t < na_ref[0])
    def _():
        x_hi, x_lo = _unpack_bf16_pair(xs_ref[...])
        x_hi, x_lo = x_hi.astype(BF16), x_lo.astype(BF16)
        half = D_MODEL // 2
        g = _dot(x_hi, wgb_ref[:half, :]) + _dot(x_lo, wgb_ref[half:, :])
        u = _dot(x_hi, wub_ref[:half, :]) + _dot(x_lo, wub_ref[half:, :])
        hid_ref[...] = (g * _sigmoid(g) * u).astype(BF16)

    @pl.when(t >= na_ref[0])
    def _():
        hid_ref[...] = jnp.zeros_like(hid_ref)


def _moe_down_kernel(te_ref, na_ref, first_ref, slot_ref, nxt_ref, *rest, layer, chunk_tiles):
    ys_ref, stage_ref, wdb_ref, sem_ref = rest[-4:]
    wd_hbm = rest[-5]
    hid_refs = rest[:-5]
    t = pl.program_id(0)
    _expert_weights(te_ref, first_ref, slot_ref, nxt_ref, t, t, layer, (wd_hbm,), stage_ref, sem_ref, (wdb_ref,))

    for c, hid_ref in enumerate(hid_refs):
        @pl.when(jnp.logical_and(t < na_ref[0], t // chunk_tiles == c))
        def _(hid_ref=hid_ref):
            ys_ref[...] = _pack_bf16_pair(_dot(hid_ref[...], wdb_ref[...]))

    @pl.when(t >= na_ref[0])
    def _():
        ys_ref[...] = jnp.zeros_like(ys_ref)


MOE_CHUNKS = 2


def _moe_experts(h2, src_tok, tile_expert, n_active, layer, w_gate, w_up, w_down):
    tiles = MOE_TILES // MOE_CHUNKS
    rows = tiles * MOE_TM

    def weight_runs(call_tiles):
        t_ids = jnp.arange(MOE_TILES, dtype=jnp.int32)
        prev = jnp.concatenate([tile_expert[:1], tile_expert[:-1]])
        first = jnp.logical_or(t_ids % call_tiles == 0, tile_expert != prev)
        slot = ((jnp.cumsum(first.astype(jnp.int32)) - 1) % 2).astype(jnp.int32)
        first_at = lax.cummin(jnp.where(first, t_ids, MOE_TILES), reverse=True)
        next_at = jnp.concatenate([first_at[1:], jnp.full((1,), MOE_TILES, jnp.int32)])
        same_call = jnp.logical_and(next_at < MOE_TILES, next_at // call_tiles == t_ids // call_tiles)
        nxt = jnp.where(jnp.logical_and(first, same_call), tile_expert[jnp.minimum(next_at, MOE_TILES - 1)], -1)
        return (tile_expert, n_active, first.astype(jnp.int32), slot, nxt.astype(jnp.int32))

    row_blk = lambda i, *_: (i, 0)
    hbm = pl.BlockSpec(memory_space=pl.ANY)
    plan = weight_runs(tiles)
    hids = []
    for c in range(MOE_CHUNKS):
        xs = h2.at[src_tok[c * rows:(c + 1) * rows]].get(mode="promise_in_bounds")
        hids.append(pl.pallas_call(
            functools.partial(_moe_up_kernel, tile0=c * tiles, layer=layer),
            grid_spec=pltpu.PrefetchScalarGridSpec(
                num_scalar_prefetch=len(plan),
                grid=(tiles,),
                in_specs=[pl.BlockSpec((MOE_TM, D_MODEL // 2), row_blk), hbm, hbm],
                out_specs=pl.BlockSpec((MOE_TM, D_EXPERT), row_blk),
                scratch_shapes=[pltpu.VMEM((2, 2, D_MODEL, D_EXPERT), F32),
                                pltpu.VMEM((D_MODEL, D_EXPERT), BF16), pltpu.VMEM((D_MODEL, D_EXPERT), BF16),
                                pltpu.SemaphoreType.DMA((2, 2))],
            ),
            out_shape=jax.ShapeDtypeStruct((rows, D_EXPERT), BF16),
            compiler_params=_cparams(("arbitrary",), 52),
            name="moe_up",
        )(*plan, xs, w_gate, w_up))
    plan = weight_runs(MOE_TILES)
    hid_specs = [pl.BlockSpec((MOE_TM, D_EXPERT), lambda i, *_, c=c: (jnp.clip(i - c * tiles, 0, tiles - 1), 0))
                 for c in range(MOE_CHUNKS)]
    return pl.pallas_call(
        functools.partial(_moe_down_kernel, layer=layer, chunk_tiles=tiles),
        grid_spec=pltpu.PrefetchScalarGridSpec(
            num_scalar_prefetch=len(plan),
            grid=(MOE_TILES,),
            in_specs=hid_specs + [hbm],
            out_specs=pl.BlockSpec((MOE_TM, D_MODEL // 2), row_blk),
            scratch_shapes=[pltpu.VMEM((2, 1, D_EXPERT, D_MODEL), F32), pltpu.VMEM((D_EXPERT, D_MODEL), BF16),
                            pltpu.SemaphoreType.DMA((2, 1))],
        ),
        out_shape=jax.ShapeDtypeStruct((MOE_ROWS, D_MODEL // 2), jnp.int32),
        compiler_params=_cparams(("arbitrary",), 40),
        name="moe_down",
    )(*plan, *hids, w_down)


PLAN_R = 2 * N_TOK // LANES


def _plan_kernel(e_ref, pos_ref, meta_ref, src_ref):
    e = e_ref[...]
    r_i = lax.broadcasted_iota(jnp.int32, (LANES, LANES), 0)
    c_i = lax.broadcasted_iota(jnp.int32, (LANES, LANES), 1)
    upper = jnp.where(r_i <= c_i, 1.0, 0.0).astype(BF16)
    r_j = lax.broadcasted_iota(jnp.int32, (PLAN_R, PLAN_R), 0)
    c_j = lax.broadcasted_iota(jnp.int32, (PLAN_R, PLAN_R), 1)
    lower = jnp.where(c_j < r_j, 1.0, 0.0).astype(BF16)
    lane = lax.broadcasted_iota(jnp.int32, (1, LANES), 1)
    pos = jnp.zeros((PLAN_R, LANES), F32)
    cnt_row = jnp.zeros((1, LANES), F32)
    off_row = jnp.zeros((1, LANES), F32)
    end_row, off_list, cnt_list = [], [], []
    row_off = jnp.zeros((1, 1), F32)
    tiles_done = jnp.zeros((1, 1), F32)
    for ex in range(N_EXPERTS):
        m = jnp.where(e == ex, 1.0, 0.0)
        inc = _dot(m.astype(BF16), upper)
        tot = jnp.broadcast_to(inc[:, LANES - 1:LANES], (PLAN_R, LANES))
        before = _dot(lower, tot.astype(BF16))
        cnt = jnp.sum(inc[:, LANES - 1:LANES], axis=0, keepdims=True)
        pos = pos + m * (row_off + before + inc - 1.0)
        tiles = jnp.floor((cnt + (MOE_TM - 1)) * (1.0 / MOE_TM))
        cnt_row = jnp.where(lane == ex, cnt, cnt_row)
        off_row = jnp.where(lane == ex, row_off, off_row)
        cnt_list.append(cnt)
        off_list.append(row_off)
        tiles_done = tiles_done + tiles
        end_row.append(tiles_done)
        row_off = row_off + tiles * MOE_TM
    n_active = tiles_done
    tid = jnp.minimum(lane.astype(F32), n_active - 1.0)
    te = jnp.zeros((1, LANES), F32)
    for ex in range(N_EXPERTS):
        te = te + jnp.where(end_row[ex] <= tid, 1.0, 0.0)
    pos_ref[...] = pos.astype(jnp.int32)
    meta = jnp.concatenate([cnt_row, off_row, te, jnp.broadcast_to(n_active, (1, LANES)),
                            jnp.zeros((4, LANES), F32)], axis=0)
    meta_ref[...] = meta.astype(jnp.int32)
    pr = MOE_ROWS // LANES
    q = lax.broadcasted_iota(jnp.int32, (pr, 1), 0)
    tile = jnp.minimum((q // (MOE_TM // LANES)).astype(F32), n_active - 1.0)
    e_q = jnp.zeros((pr, 1), F32)
    for ex in range(N_EXPERTS):
        e_q = e_q + jnp.where(end_row[ex] <= tile, 1.0, 0.0)
    off_q = jnp.zeros((pr, 1), F32)
    cnt_q = jnp.zeros((pr, 1), F32)
    start_q = jnp.zeros((pr, 1), F32)
    start = jnp.zeros((1, 1), F32)
    for ex in range(N_EXPERTS):
        hit = e_q == float(ex)
        off_q = jnp.where(hit, off_list[ex], off_q)
        cnt_q = jnp.where(hit, cnt_list[ex], cnt_q)
        start_q = jnp.where(hit, start, start_q)
        start = start + cnt_list[ex]
    r = (q * LANES + lane).astype(F32)
    rank = r - off_q
    spare = r - jnp.where(r >= float(N_TOK), float(N_TOK), 0.0) - jnp.where(r >= float(2 * N_TOK), float(N_TOK), 0.0)
    src_ref[...] = jnp.where(rank < cnt_q, start_q + rank, -(spare + 1.0)).astype(jnp.int32)


def _route_plan(eidx):
    e2 = eidx[:2].reshape(PLAN_R, LANES)
    pos, meta, src = pl.pallas_call(
        _plan_kernel,
        out_shape=[jax.ShapeDtypeStruct((PLAN_R, LANES), jnp.int32), jax.ShapeDtypeStruct((8, LANES), jnp.int32),
                   jax.ShapeDtypeStruct((MOE_ROWS // LANES, LANES), jnp.int32)],
        compiler_params=pltpu.CompilerParams(vmem_limit_bytes=32 * 1024 * 1024),
        name="route_plan",
    )(e2)
    te, n_active = meta[2, :MOE_TILES], meta[3, :1]
    order = jnp.argsort(e2.reshape(-1), stable=True).astype(jnp.int32)
    src = src.reshape(-1)
    src_tok = jnp.where(src >= 0, order[jnp.maximum(src, 0)] % N_TOK, -src - 1)
    return src_tok, te, n_active, pos.reshape(-1)


def _combine_kernel(x_ref, y0_ref, y1_ref, gt_ref, mt_ref, fg_ref, o_ref, *, final):
    gt = gt_ref[...]
    g0, g1 = gt[:, 0:1], gt[:, 1:2]
    half = D_MODEL // 2
    y0 = _unpack_bf16_pair(y0_ref[...])
    y1 = _unpack_bf16_pair(y1_ref[...])
    xs = []
    for c in range(2):
        cols = slice(c * half, (c + 1) * half)
        xs.append(x_ref[:, cols] + mt_ref[0, 5:6, cols] * (y0[c] * g0 + y1[c] * g1))
    if final:
        ssq = jnp.sum(xs[0] * xs[0], axis=-1, keepdims=True) + jnp.sum(xs[1] * xs[1], axis=-1, keepdims=True)
        r = lax.rsqrt(ssq * (1.0 / D_MODEL) + EPS)
        xs = [xc * r * fg_ref[:, c * half:(c + 1) * half] for c, xc in enumerate(xs)]
    for c, xc in enumerate(xs):
        o_ref[:, c * half:(c + 1) * half] = xc


def _combine(x, row0, n_rows, ysel, gates_t, mt, final_g, final):
    tm = 256
    b0 = row0 // tm
    return pl.pallas_call(
        functools.partial(_combine_kernel, final=final),
        grid=(n_rows // tm,),
        in_specs=[
            pl.BlockSpec((tm, D_MODEL), lambda i: (i + b0, 0)),
            pl.BlockSpec((None, tm, D_MODEL // 2), lambda i: (0, i, 0)),
            pl.BlockSpec((None, tm, D_MODEL // 2), lambda i: (1, i, 0)),
            pl.BlockSpec((tm, 8), lambda i: (i + b0, 0)),
            pl.BlockSpec((1, 6, D_MODEL), lambda i: ((i + b0) // (SEG // tm), 0, 0)),
            pl.BlockSpec((1, D_MODEL), lambda i: (0, 0)),
        ],
        out_specs=pl.BlockSpec((tm, D_MODEL), lambda i: (i, 0)),
        out_shape=jax.ShapeDtypeStruct((n_rows, D_MODEL), F32),
        compiler_params=_cparams(("arbitrary",), 40),
        name="moe_combine",
    )(x, ysel, ysel, gates_t, mt, final_g)


def _rope_tables(dim):
    half = dim // 2
    inv = ROPE_THETA ** (-jnp.arange(0, half, 2, dtype=F32) / half)
    t = jnp.arange(DEC_SEQ)
    ang_r = (t // GRID_W).astype(F32)[:, None] * inv[None, :]
    ang_c = (t % GRID_W).astype(F32)[:, None] * inv[None, :]
    ang = jnp.concatenate([ang_r, ang_r, ang_c, ang_c], axis=-1)
    cos, sin = jnp.cos(ang), jnp.sin(ang)
    reps = LANES // dim
    cos = jnp.tile(cos, (1, reps))
    sin = jnp.tile(sin, (1, reps))
    sh = dim // 4
    second = (np.arange(LANES) % (2 * sh)) >= sh
    sp = jnp.where(second[None, :], sin, 0.0)
    sm = jnp.where(second[None, :], 0.0, -sin)
    return cos, sp, sm


def _neighbourhood_bias(rpb):
    rows = DEC_SEQ // GRID_W
    kh = min(NA_ROWS, rows)
    r = np.arange(rows)
    r0 = np.clip(r - kh // 2, 0, rows - kh)
    kr = np.arange(rows)
    row_ok = (kr[None, :] >= r0[:, None]) & (kr[None, :] < r0[:, None] + kh)
    c = np.arange(GRID_W)
    ws = np.clip(c - NA_COLS // 2, 0, GRID_W - NA_COLS)
    kc = np.arange(GRID_W)
    col_ok = (kc[None, :] >= ws[:, None]) & (kc[None, :] < ws[:, None] + NA_COLS)
    dc_idx = np.clip(kc[None, :] - c[:, None], -(NA_COLS - 1), NA_COLS - 1) + NA_COLS - 1
    sel_c = ((np.arange(2 * NA_COLS - 1)[:, None, None] == dc_idx[None]) & col_ok[None]).astype(np.float32)
    t = jnp.einsum("hab,bcx->hacx", rpb.astype(F32), sel_c, precision=lax.Precision.HIGHEST)
    t = jnp.where(col_ok[None, None], t * LOG2E, NEG)
    t = jnp.concatenate([t, t], axis=-1)

    def build(t_ref, o_ref):
        left = lax.broadcasted_iota(jnp.int32, (1, LANES), 1) < GRID_W
        neg = jnp.full((GRID_W, LANES), NEG, F32)
        for rr in range(rows):
            for p in range(rows // 2):
                parts = [t_ref[k - rr + NA_ROWS - 1] if row_ok[rr, k] else neg for k in (2 * p, 2 * p + 1)]
                blk = jnp.where(left, parts[0], parts[1]) if (row_ok[rr, 2 * p] or row_ok[rr, 2 * p + 1]) else neg
                o_ref[rr * GRID_W:(rr + 1) * GRID_W, p * LANES:(p + 1) * LANES] = blk

    return pl.pallas_call(
        build,
        grid=(N_HEADS,),
        in_specs=[pl.BlockSpec((None, 2 * NA_ROWS - 1, GRID_W, LANES), lambda h: (h, 0, 0, 0))],
        out_specs=pl.BlockSpec((None, DEC_SEQ, DEC_SEQ), lambda h: (h, 0, 0)),
        out_shape=jax.ShapeDtypeStruct((N_HEADS, DEC_SEQ, DEC_SEQ), F32),
        compiler_params=_cparams(("arbitrary",), 32),
        name="nbr_bias",
    )(t)


def _even_w_in_layout(w, idx):
    cq_end = 3 * N_HEADS * HEAD_DIM + B_Q_RANK
    kv_end = cq_end + B_KV_RANK
    tk = 256

    def layout(w_ref, o_ref):
        o_ref[:, :cq_end] = w_ref[:, :cq_end].astype(BF16)
        o_ref[:, cq_end + 2 * LANES:] = w_ref[:, cq_end:kv_end].astype(BF16)
        kr = w_ref[:, kv_end:].astype(BF16)
        r = lax.broadcasted_iota(jnp.int32, (B_ROPE_DIM, 2 * LANES), 0)
        c = lax.broadcasted_iota(jnp.int32, (B_ROPE_DIM, 2 * LANES), 1)
        place = jnp.where(jnp.logical_or(c == r, c == r + 2 * LANES - B_ROPE_DIM), 1.0, 0.0).astype(BF16)
        o_ref[:, cq_end:cq_end + 2 * LANES] = _dot(kr, place).astype(BF16)

    return pl.pallas_call(
        layout,
        grid=(D_MODEL // tk,),
        in_specs=[pl.BlockSpec((None, tk, w.shape[2]), lambda i: (idx, i, 0))],
        out_specs=pl.BlockSpec((tk, PROJ_W), lambda i: (i, 0)),
        out_shape=jax.ShapeDtypeStruct((D_MODEL, PROJ_W), BF16),
        compiler_params=_cparams(("arbitrary",), 32),
        name="even_w_in_layout",
    )(w)


def _even_layer(x, mt, l, i, rope64, cache_a_k, cache_a_v, cache_b_ckv, cache_b_krope, norm_mix_g, ev_w_in,
                ev_lambda, ev_subln_g, ev_q_norm_g, ev_kv_norm_g, ev_w_uq, ev_w_ukv, ev_w_out):
    w_in = _even_w_in_layout(ev_w_in, i)
    all_g = (0, 1, 2, 3)
    a_scale = A_QK_DIM ** -0.5 * LOG2E
    in_kw = dict(prologue="modulate", g=norm_mix_g[l][None, :], norm_tile=8, norm_g=ev_kv_norm_g[i][None, :],
                 out_dtype=BF16, tile_scale={0: a_scale, 1: a_scale}, tps=3)
    bcq_src = (0, ((6, 0, TN), (7, 0, TN)))
    head_src = lambda t0: tuple((t0 + t, gq, 4 * t + gq) for t in range(2) for gq in range(4))
    proj_p, st_ak, st_av, st_ckv, st_kr, bcq_p = _projection(
        *x["ctx"], N_SEG_P, D_MODEL, 0, D_MODEL, w_in, mt=mt[:N_SEG_P], tm=SEG // 2, name="even_in_proj_ctx",
        states=((N_HEADS, head_src(2)), (N_HEADS, head_src(4)), (0, ((8, 0, B_KV_RANK),)), (0, ((7, 256, B_ROPE_DIM),)),
                bcq_src), **in_kw)
    proj_l, bcq_l = _projection(*x["lat"], DEC_BATCH, D_MODEL, 0, D_MODEL, w_in, mt=mt[N_SEG_P:], rope=rope64,
                                sh=A_QK_DIM // 4, rope_groups={0: all_g, 1: all_g, 2: all_g, 3: all_g, 7: (2, 3)},
                                states=(bcq_src,), name="even_in_proj_lat", **in_kw)
    wq = ev_w_uq[i].reshape(B_Q_RANK, N_HEADS, HEAD_DIM + B_ROPE_DIM)
    w_uq = jnp.concatenate([wq[:, :, :HEAD_DIM].reshape(B_Q_RANK, -1), wq[:, :, HEAD_DIM:].reshape(B_Q_RANK, -1)],
                           axis=1).astype(BF16)
    mla_scale = (HEAD_DIM + B_ROPE_DIM) ** -0.5
    q_kw = dict(prologue="rmsnorm", g=ev_q_norm_g[i][None, :], out_dtype=BF16, tn=w_uq.shape[1],
                tile_scale={0: mla_scale * LOG2E})
    bq_p = _projection(bcq_p, 0, N_SEG_P, 1024, 0, B_Q_RANK, w_uq, name="mla_q_up_ctx", **q_kw)
    bq_l = _projection(bcq_l, 0, DEC_BATCH, 1024, 0, B_Q_RANK, w_uq, rope=rope64, sh=B_ROPE_DIM // 4,
                       rope_groups={0: (8, 9, 10, 11)}, name="mla_q_up_lat", **q_kw)
    w_ukv = ev_w_ukv[i].astype(BF16)
    kv_kw = dict(prologue="cast", out_dtype=BF16, tn=w_ukv.shape[1])
    kv_p = _projection(proj_p, 0, N_SEG_P, 512, 8, B_KV_RANK, w_ukv, name="mla_kv_up_ctx", **kv_kw)
    kv_l = _projection(proj_l, 0, DEC_BATCH, 512, 8, B_KV_RANK, w_ukv, name="mla_kv_up_lat", **kv_kw)
    kv_ctx = _projection(cache_b_ckv[:, i].reshape(DEC_BATCH * PAST_LEN, B_KV_RANK), 0, 1, 512, 0, B_KV_RANK, w_ukv,
                         name="mla_kv_up_cache", **kv_kw)
    kv_ctx = kv_ctx.reshape(DEC_BATCH, PAST_LEN, 2 * N_HEADS * HEAD_DIM)
    krc = cache_b_krope[:, i]
    zc = jnp.zeros_like(krc)
    kr_ctx = jnp.concatenate([krc, zc, zc, krc], axis=-1)
    ak_ctx = cache_a_k[:, i].reshape(DEC_BATCH, PAST_LEN, N_HEADS * HEAD_DIM)
    av_ctx = cache_a_v[:, i].reshape(DEC_BATCH, PAST_LEN, N_HEADS * HEAD_DIM)
    lam_init = 0.8 - 0.6 * math.exp(-0.3 * l)
    o = jnp.zeros((N_TOK, D_MODEL), BF16)
    for latent, proj, b_q, kv in ((False, proj_p, bq_p, kv_p), (True, proj_l, bq_l, kv_l)):
        tag = "lat" if latent else "ctx"
        o = _attention(kind="diff", latent=latent, q_parts=[(proj, lambda h: h)], ko_parts=[(proj, lambda h: 8 + h)],
                       vo=(proj, lambda h: 16 + h), kc_parts=[(ak_ctx, lambda h: h)], vc=(av_ctx, lambda h: h),
                       lam=ev_lambda[i], subln=ev_subln_g[i][None, :], o_arr=o, o_blk=lambda h: h,
                       lam_init=lam_init, name="diff_attn_" + tag)
        o = _attention(kind="mla", latent=latent, q_parts=[(b_q, lambda h: h), (b_q, lambda h: 8 + h // 2)],
                       ko_parts=[(kv, lambda h: 2 * h), (proj, lambda h: 30 + h % 2)], vo=(kv, lambda h: 2 * h + 1),
                       kc_parts=[(kv_ctx, lambda h: 2 * h), (kr_ctx, lambda h: h % 2)], vc=(kv_ctx, lambda h: 2 * h + 1),
                       o_arr=o, o_blk=lambda h: 8 + h, name="mla_attn_" + tag)
    states = (st_ak.reshape(BATCH, SEQ, N_HEADS, HEAD_DIM), st_av.reshape(BATCH, SEQ, N_HEADS, HEAD_DIM),
              st_ckv.reshape(BATCH, SEQ, B_KV_RANK), st_kr.reshape(BATCH, SEQ, B_ROPE_DIM))
    return (o, ev_w_out[i].astype(BF16)), states


def _odd_layer(x, mt, l, i, rope128, cache_c_k, cache_c_v, cache_d_k, cache_d_v, norm_mix_g, od_w_in, od_rpb,
               od_sink, od_w_out):
    w_in = od_w_in[i].astype(BF16)
    all_g = (0, 1, 2, 3)
    q_scale = HEAD_DIM ** -0.5 * LOG2E
    in_kw = dict(prologue="modulate", g=norm_mix_g[l][None, :], out_dtype=BF16,
                 tile_scale={t: q_scale for t in (0, 1, 6, 7)}, tps=3)
    head_src = lambda t0: tuple((t0 + t, gq, 4 * t + gq) for t in range(2) for gq in range(4))
    proj_p, st_ck, st_cv, st_dk, st_dv = _projection(
        *x["ctx"], N_SEG_P, D_MODEL, 0, D_MODEL, w_in, mt=mt[:N_SEG_P], tm=SEG // 2, name="odd_in_proj_ctx",
        states=((N_HEADS, head_src(2)), (N_HEADS, head_src(4)), (D_KV_HEADS, ((8, 0, 0), (8, 1, 1))),
                (D_KV_HEADS, ((8, 2, 0), (8, 3, 1)))), **in_kw)
    proj_l = _projection(*x["lat"], DEC_BATCH, D_MODEL, 0, D_MODEL, w_in, mt=mt[N_SEG_P:], rope=rope128,
                         sh=HEAD_DIM // 4, rope_groups={6: all_g, 7: all_g, 8: (0, 1)}, name="odd_in_proj_lat", **in_kw)
    ck_ctx = cache_c_k[:, i].reshape(DEC_BATCH, PAST_LEN, N_HEADS * HEAD_DIM)
    cv_ctx = cache_c_v[:, i].reshape(DEC_BATCH, PAST_LEN, N_HEADS * HEAD_DIM)
    dk_ctx = cache_d_k[:, i].reshape(DEC_BATCH, PAST_LEN, D_KV_HEADS * HEAD_DIM)
    dv_ctx = cache_d_v[:, i].reshape(DEC_BATCH, PAST_LEN, D_KV_HEADS * HEAD_DIM)
    bias = _neighbourhood_bias(od_rpb[i])
    sink = od_sink[i].astype(F32)
    o = jnp.zeros((N_TOK, D_MODEL), BF16)
    for latent, proj in ((False, proj_p), (True, proj_l)):
        tag = "lat" if latent else "ctx"
        o = _attention(kind="plain", latent=latent, q_parts=[(proj, lambda h: h)], ko_parts=[(proj, lambda h: 8 + h)],
                       vo=(proj, lambda h: 16 + h), kc_parts=[(ck_ctx, lambda h: h)], vc=(cv_ctx, lambda h: h),
                       bias=bias if latent else None, o_arr=o, o_blk=lambda h: h,
                       name="nbr_attn_" + tag)
        o = _attention(kind="plain", latent=latent, q_parts=[(proj, lambda h: 24 + h)],
                       ko_parts=[(proj, lambda h: 32 + h // D_GROUP)], vo=(proj, lambda h: 34 + h // D_GROUP),
                       kc_parts=[(dk_ctx, lambda h: h // D_GROUP)], vc=(dv_ctx, lambda h: h // D_GROUP), sink=sink,
                       o_arr=o, o_blk=lambda h: 8 + h, has_band=latent, tq_lat=256,
                       name="win_attn_" + tag)
    states = (st_ck.reshape(BATCH, SEQ, N_HEADS, HEAD_DIM), st_cv.reshape(BATCH, SEQ, N_HEADS, HEAD_DIM),
              st_dk.reshape(BATCH, SEQ, D_KV_HEADS, HEAD_DIM), st_dv.reshape(BATCH, SEQ, D_KV_HEADS, HEAD_DIM))
    return (o, od_w_out[i].astype(BF16)), states


def _moe_layer(mix, x_parts, mt, g, router_w, router_b, layer, w_gate, w_up, w_down, final_g, final):
    x, h2, eidx, gates = _out_projection_router(*mix, x_parts, mt, g, router_w, router_b)
    src_tok, te, n_active, pos = _route_plan(eidx)
    ys = _moe_experts(h2, src_tok, te, n_active, layer, w_gate, w_up, w_down)
    pos2 = pos.reshape(2, N_TOK)
    gates_t = gates.T
    outs = []
    for row0, n_rows in ((0, N_PROMPT), (N_PROMPT, N_TOK - N_PROMPT)):
        ysel = ys.at[pos2[:, row0:row0 + n_rows].reshape(-1)].get(mode="promise_in_bounds")
        outs.append(_combine(x, row0, n_rows, ysel.reshape(2, n_rows, D_MODEL // 2), gates_t, mt, final_g, final))
    return outs


def kernel(x_prompt, x_sample, cache_a_k, cache_a_v, cache_b_ckv, cache_b_krope, cache_c_k, cache_c_v, cache_d_k, cache_d_v, c, c_ctx, w_ada, b_ada, norm_mix_g, norm_ffn_g, ev_w_in, ev_lambda, ev_subln_g, ev_q_norm_g, ev_kv_norm_g, ev_w_uq, ev_w_ukv, ev_w_out, od_w_in, od_rpb, od_sink, od_w_out, router_w, router_b, moe_w_gate, moe_w_up, moe_w_down, final_g):
    xp = x_prompt.reshape(N_PROMPT, D_MODEL)
    xl = x_sample.reshape(DEC_BATCH * DEC_SEQ, D_MODEL)
    x = dict(ctx=(xp, 0), lat=(xl, 0), parts=[xp, xl])
    cond8 = jnp.concatenate([c_ctx[None, :], c, jnp.zeros((3, D_MODEL), F32)], axis=0)
    mod = _adaln(cond8, w_ada, b_ada)
    seg_row = np.array([0] * N_SEG_P + [1 + b for b in range(DEC_BATCH)])
    mt_all = mod[:, seg_row].reshape(DEPTH, N_SEG, 6, D_MODEL)
    rope64 = _rope_tables(A_QK_DIM)
    rope128 = _rope_tables(HEAD_DIM)
    even_states, odd_states = [], []
    for l in range(DEPTH):
        i = l // 2
        mt = mt_all[l]
        if l % 2 == 0:
            mix, st = _even_layer(x, mt, l, i, rope64, cache_a_k, cache_a_v, cache_b_ckv, cache_b_krope, norm_mix_g,
                                ev_w_in, ev_lambda, ev_subln_g, ev_q_norm_g, ev_kv_norm_g, ev_w_uq, ev_w_ukv, ev_w_out)
            even_states.append(st)
        else:
            mix, st = _odd_layer(x, mt, l, i, rope128, cache_c_k, cache_c_v, cache_d_k, cache_d_v, norm_mix_g,
                               od_w_in, od_rpb, od_sink, od_w_out)
            odd_states.append(st)
        x = _moe_layer(mix, x["parts"], mt, norm_ffn_g[l][None, :], router_w, router_b, l, moe_w_gate, moe_w_up, moe_w_down,
                       final_g[None, :], final=(l == DEPTH - 1))
        if l < DEPTH - 1:
            x = dict(ctx=(x[0], 0), lat=(x[1], 0), parts=list(x))
    y_prompt = x[0].reshape(BATCH, SEQ, D_MODEL)
    y_sample = x[1].reshape(DEC_BATCH, DEC_SEQ, D_MODEL)
    new_even = tuple(jnp.stack([st[k] for st in even_states], axis=1) for k in range(4))
    new_odd = tuple(jnp.stack([st[k] for st in odd_states], axis=1) for k in range(4))
    return (y_prompt, y_sample) + new_even + new_odd
```
